```python
import jax, jax.numpy as jnp
from jax import lax
import numpy as np

D_MODEL = 1024
BATCH = 8
SEQ = 8192
DEPTH = 4

N_A = DEPTH // 2
N_B = DEPTH - N_A
N_HEADS = 16
HEAD_DIM = D_MODEL // N_HEADS
CONV_W = 3
D_FF = 2816
Q_BLOCK = 128
EPS = 1e-6

kernel_name = "yoco_shortconv_fox_hybrid"


def rmsnorm(x, g):
    xf = x.astype(jnp.float32)
    y = xf * lax.rsqrt(jnp.mean(xf * xf, axis=-1, keepdims=True) + EPS)
    return (y * g).astype(x.dtype)


def causal_dwconv(u, w):
    width = w.shape[0]
    s = u.shape[1]
    up = jnp.pad(u, ((0, 0), (width - 1, 0), (0, 0)))
    return sum(up[:, i:i + s] * w[i] for i in range(width))


def short_conv_mixer(xn, w_in, conv_w, w_out):
    proj = xn @ w_in
    b, c, h = jnp.split(proj, 3, axis=-1)
    u = causal_dwconv(c * h, conv_w)
    return (b * u) @ w_out


def conv_ffn(xn, w_up, conv_w, w_down):
    up = xn @ w_up
    a, g = jnp.split(up, 2, axis=-1)
    a = causal_dwconv(a, conv_w)
    return (jax.nn.silu(a) * g) @ w_down


def forgetting_attention(q, k, v, c):
    bsz, nh, s_len, hd = q.shape
    nb = s_len // Q_BLOCK
    scale = hd ** -0.5
    qb = q.reshape(bsz, nh, nb, Q_BLOCK, hd).transpose(2, 0, 1, 3, 4)
    cb = c.reshape(bsz, nh, nb, Q_BLOCK).transpose(2, 0, 1, 3)
    kpos = jnp.arange(s_len)

    def one_block(args):
        q_i, c_i, i = args
        s = jnp.einsum('bhqd,bhkd->bhqk', q_i, k, preferred_element_type=jnp.float32) * scale
        s = s + c_i[..., None] - c[:, :, None, :]
        qpos = i * Q_BLOCK + jnp.arange(Q_BLOCK)
        s = jnp.where(kpos[None, :] <= qpos[:, None], s, -jnp.inf)
        p = jax.nn.softmax(s, axis=-1)
        return jnp.einsum('bhqk,bhkd->bhqd', p.astype(v.dtype), v)

    o = lax.map(one_block, (qb, cb, jnp.arange(nb)))
    return o.transpose(1, 0, 3, 2, 4).reshape(bsz, s_len, nh * hd)


def _fwd_setup_inputs(seed: int = 0) -> dict:
    key = jax.random.key(seed)
    ks = jax.random.split(key, 17)
    f32 = jnp.float32
    out_scale = (2 * DEPTH) ** -0.5

    def nrm(k, shape, scale):
        return jax.random.normal(k, shape, f32) * scale

    def gain(k, shape):
        return 1.0 + 0.02 * jax.random.normal(k, shape, f32)

    x = nrm(ks[0], (BATCH, SEQ, D_MODEL), 1.0)
    attn_norm = gain(ks[1], (DEPTH, D_MODEL))
    ffn_norm = gain(ks[2], (DEPTH, D_MODEL))
    a_w_in = nrm(ks[3], (N_A, D_MODEL, 3 * D_MODEL), D_MODEL ** -0.5)
    a_conv = nrm(ks[4], (N_A, CONV_W, D_MODEL), CONV_W ** -0.5)
    a_w_out = nrm(ks[5], (N_A, D_MODEL, D_MODEL), out_scale * D_MODEL ** -0.5)
    kv_norm = gain(ks[6], (D_MODEL,))
    w_kvf = jnp.concatenate([
        nrm(ks[7], (D_MODEL, 2 * D_MODEL), D_MODEL ** -0.5),
        nrm(ks[8], (D_MODEL, N_HEADS), 0.1 * D_MODEL ** -0.5),
    ], axis=1)
    b_f = jax.random.uniform(ks[9], (N_HEADS,), f32, 1.0, 6.0)
    k_norm = gain(ks[10], (HEAD_DIM,))
    b_w_qg = nrm(ks[11], (N_B, D_MODEL, 2 * D_MODEL), D_MODEL ** -0.5)
    q_norm = gain(ks[12], (N_B, HEAD_DIM))
    b_w_out = nrm(ks[13], (N_B, D_MODEL, D_MODEL), out_scale * D_MODEL ** -0.5)
    ffn_w_up = nrm(ks[14], (DEPTH, D_MODEL, 2 * D_FF), D_MODEL ** -0.5)
    ffn_conv = nrm(ks[15], (DEPTH, CONV_W, D_FF), CONV_W ** -0.5)
    ffn_w_down = nrm(ks[16], (DEPTH, D_FF, D_MODEL), out_scale * D_FF ** -0.5)
    return {"x": x, "attn_norm": attn_norm, "ffn_norm": ffn_norm,
            "a_w_in": a_w_in, "a_conv": a_conv, "a_w_out": a_w_out,
            "kv_norm": kv_norm, "w_kvf": w_kvf, "b_f": b_f, "k_norm": k_norm,
            "b_w_qg": b_w_qg, "q_norm": q_norm, "b_w_out": b_w_out,
            "ffn_w_up": ffn_w_up, "ffn_conv": ffn_conv, "ffn_w_down": ffn_w_down}


def _fwd_reference(x, attn_norm, ffn_norm, a_w_in, a_conv, a_w_out, kv_norm, w_kvf, b_f,
              k_norm, b_w_qg, q_norm, b_w_out, ffn_w_up, ffn_conv, ffn_w_down):
    bsz, s_len, d = x.shape
    k = v = c = None
    for l in range(DEPTH):
        if l < N_A:
            xn = rmsnorm(x, attn_norm[l])
            x = x + short_conv_mixer(xn, a_w_in[l], a_conv[l], a_w_out[l])
        else:
            if l == N_A:
                h = rmsnorm(x, kv_norm)
                kvf = h @ w_kvf
                k_s = kvf[..., :d].reshape(bsz, s_len, N_HEADS, HEAD_DIM)
                v_s = kvf[..., d:2 * d].reshape(bsz, s_len, N_HEADS, HEAD_DIM)
                f_logit = (kvf[..., 2 * d:] + b_f).astype(jnp.float32)
                k = rmsnorm(k_s, k_norm).transpose(0, 2, 1, 3)
                v = v_s.transpose(0, 2, 1, 3)
                c = jnp.cumsum(jax.nn.log_sigmoid(f_logit), axis=1).transpose(0, 2, 1)
            j = l - N_A
            xn = rmsnorm(x, attn_norm[l])
            qg = xn @ b_w_qg[j]
            q = rmsnorm(qg[..., :d].reshape(bsz, s_len, N_HEADS, HEAD_DIM), q_norm[j])
            q = q.transpose(0, 2, 1, 3)
            o = forgetting_attention(q, k, v, c)
            o = o * jax.nn.sigmoid(qg[..., d:])
            x = x + o @ b_w_out[j]
        xn = rmsnorm(x, ffn_norm[l])
        x = x + conv_ffn(xn, ffn_w_up[l], ffn_conv[l], ffn_w_down[l])
    return x


import jax as _jax
import jax.numpy as _jnp

TWIN_FORMAT = 'train_step'
FWD_PARAMS = ['x', 'attn_norm', 'ffn_norm', 'a_w_in', 'a_conv', 'a_w_out', 'kv_norm', 'w_kvf', 'b_f', 'k_norm', 'b_w_qg', 'q_norm', 'b_w_out', 'ffn_w_up', 'ffn_conv', 'ffn_w_down']
TWIN_WEIGHTS = ['attn_norm', 'ffn_norm', 'a_w_in', 'a_conv', 'a_w_out', 'kv_norm', 'w_kvf', 'b_f', 'k_norm', 'b_w_qg', 'q_norm', 'b_w_out', 'ffn_w_up', 'ffn_conv', 'ffn_w_down']
TWIN_DIFF_INPUT = 'x'
TWIN_INPUTS = ['x', 'attn_norm', 'ffn_norm', 'a_w_in', 'a_conv', 'a_w_out', 'kv_norm', 'w_kvf', 'b_f', 'k_norm', 'b_w_qg', 'q_norm', 'b_w_out', 'ffn_w_up', 'ffn_conv', 'ffn_w_down', 'loss_target', 'm_attn_norm', 'm_ffn_norm', 'm_a_w_in', 'm_a_conv', 'm_a_w_out', 'm_kv_norm', 'm_w_kvf', 'm_b_f', 'm_k_norm', 'm_b_w_qg', 'm_q_norm', 'm_b_w_out', 'm_ffn_w_up', 'm_ffn_conv', 'm_ffn_w_down', 'v_attn_norm', 'v_ffn_norm', 'v_a_w_in', 'v_a_conv', 'v_a_w_out', 'v_kv_norm', 'v_w_kvf', 'v_b_f', 'v_k_norm', 'v_b_w_qg', 'v_q_norm', 'v_b_w_out', 'v_ffn_w_up', 'v_ffn_conv', 'v_ffn_w_down']
TWIN_OUTPUTS = ['loss', 'grad_x', 'grad_attn_norm', 'grad_ffn_norm', 'grad_a_w_in', 'grad_a_conv', 'grad_a_w_out', 'grad_kv_norm', 'grad_w_kvf', 'grad_b_f', 'grad_k_norm', 'grad_b_w_qg', 'grad_q_norm', 'grad_b_w_out', 'grad_ffn_w_up', 'grad_ffn_conv', 'grad_ffn_w_down', 'delta_attn_norm', 'delta_ffn_norm', 'delta_a_w_in', 'delta_a_conv', 'delta_a_w_out', 'delta_kv_norm', 'delta_w_kvf', 'delta_b_f', 'delta_k_norm', 'delta_b_w_qg', 'delta_q_norm', 'delta_b_w_out', 'delta_ffn_w_up', 'delta_ffn_conv', 'delta_ffn_w_down', 'new_m_attn_norm', 'new_m_ffn_norm', 'new_m_a_w_in', 'new_m_a_conv', 'new_m_a_w_out', 'new_m_kv_norm', 'new_m_w_kvf', 'new_m_b_f', 'new_m_k_norm', 'new_m_b_w_qg', 'new_m_q_norm', 'new_m_b_w_out', 'new_m_ffn_w_up', 'new_m_ffn_conv', 'new_m_ffn_w_down', 'new_v_attn_norm', 'new_v_ffn_norm', 'new_v_a_w_in', 'new_v_a_conv', 'new_v_a_w_out', 'new_v_kv_norm', 'new_v_w_kvf', 'new_v_b_f', 'new_v_k_norm', 'new_v_b_w_qg', 'new_v_q_norm', 'new_v_b_w_out', 'new_v_ffn_w_up', 'new_v_ffn_conv', 'new_v_ffn_w_down']
TWIN_LEAF_KINDS = {'loss': 'loss', 'grad_x': 'grad_x', 'grad_attn_norm': 'grad_w', 'grad_ffn_norm': 'grad_w', 'grad_a_w_in': 'grad_w', 'grad_a_conv': 'grad_w', 'grad_a_w_out': 'grad_w', 'grad_kv_norm': 'grad_w', 'grad_w_kvf': 'grad_w', 'grad_b_f': 'grad_w', 'grad_k_norm': 'grad_w', 'grad_b_w_qg': 'grad_w', 'grad_q_norm': 'grad_w', 'grad_b_w_out': 'grad_w', 'grad_ffn_w_up': 'grad_w', 'grad_ffn_conv': 'grad_w', 'grad_ffn_w_down': 'grad_w', 'delta_attn_norm': 'delta_w', 'delta_ffn_norm': 'delta_w', 'delta_a_w_in': 'delta_w', 'delta_a_conv': 'delta_w', 'delta_a_w_out': 'delta_w', 'delta_kv_norm': 'delta_w', 'delta_w_kvf': 'delta_w', 'delta_b_f': 'delta_w', 'delta_k_norm': 'delta_w', 'delta_b_w_qg': 'delta_w', 'delta_q_norm': 'delta_w', 'delta_b_w_out': 'delta_w', 'delta_ffn_w_up': 'delta_w', 'delta_ffn_conv': 'delta_w', 'delta_ffn_w_down': 'delta_w', 'new_m_attn_norm': 'new_m', 'new_m_ffn_norm': 'new_m', 'new_m_a_w_in': 'new_m', 'new_m_a_conv': 'new_m', 'new_m_a_w_out': 'new_m', 'new_m_kv_norm': 'new_m', 'new_m_w_kvf': 'new_m', 'new_m_b_f': 'new_m', 'new_m_k_norm': 'new_m', 'new_m_b_w_qg': 'new_m', 'new_m_q_norm': 'new_m', 'new_m_b_w_out': 'new_m', 'new_m_ffn_w_up': 'new_m', 'new_m_ffn_conv': 'new_m', 'new_m_ffn_w_down': 'new_m', 'new_v_attn_norm': 'new_v', 'new_v_ffn_norm': 'new_v', 'new_v_a_w_in': 'new_v', 'new_v_a_conv': 'new_v', 'new_v_a_w_out': 'new_v', 'new_v_kv_norm': 'new_v', 'new_v_w_kvf': 'new_v', 'new_v_b_f': 'new_v', 'new_v_k_norm': 'new_v', 'new_v_b_w_qg': 'new_v', 'new_v_q_norm': 'new_v', 'new_v_b_w_out': 'new_v', 'new_v_ffn_w_up': 'new_v', 'new_v_ffn_conv': 'new_v', 'new_v_ffn_w_down': 'new_v'}


def _forward(args):
    return _fwd_reference(*[args[k] for k in FWD_PARAMS])


def _output_shape():
    def fwd():
        inp = _fwd_setup_inputs(0)
        return _fwd_reference(*[inp[k] for k in FWD_PARAMS])
    out = _jax.eval_shape(fwd)
    return out.shape, out.dtype

N_MICROBATCH = 1
ADAM_LR = 0.001
ADAM_B1 = 0.9
ADAM_B2 = 0.999
ADAM_EPS = 1e-08
ADAM_WD = 0.01
ADAM_STEP = 10
PER_EXAMPLE_BATCH_AXIS = {'x': 0, 'loss_target': 0}
SHARED_INPUTS = []
_WEIGHT_DTYPES = {'attn_norm': _jnp.float32, 'ffn_norm': _jnp.float32, 'a_w_in': _jnp.float32, 'a_conv': _jnp.float32, 'a_w_out': _jnp.float32, 'kv_norm': _jnp.float32, 'w_kvf': _jnp.float32, 'b_f': _jnp.float32, 'k_norm': _jnp.float32, 'b_w_qg': _jnp.float32, 'q_norm': _jnp.float32, 'b_w_out': _jnp.float32, 'ffn_w_up': _jnp.float32, 'ffn_conv': _jnp.float32, 'ffn_w_down': _jnp.float32}
MOMENT_SCALE = {'attn_norm': 1.702970e+01, 'ffn_norm': 6.484139e+00, 'a_w_in': 2.818278e-01, 'a_conv': 4.627670e+00, 'a_w_out': 7.482628e-01, 'kv_norm': 2.021766e-01, 'w_kvf': 3.150901e-02, 'b_f': 8.772901e+00, 'k_norm': 1.812857e+00, 'b_w_qg': 1.493926e-02, 'q_norm': 8.939095e-01, 'b_w_out': 7.137157e-02, 'ffn_w_up': 8.214009e-02, 'ffn_conv': 6.855487e-01, 'ffn_w_down': 4.433368e-01}


def _to_microbatches(a, axis):
    t = _jnp.moveaxis(a, axis, 0)
    t = t.reshape((N_MICROBATCH, t.shape[0] // N_MICROBATCH) + t.shape[1:])
    return _jnp.moveaxis(t, 1, axis + 1)


def setup_inputs(seed: int = 0) -> dict:
    inp = _fwd_setup_inputs(seed)
    key = _jax.random.fold_in(_jax.random.key(seed), 7919)
    shape, _ = _output_shape()
    out = dict(inp)
    out["loss_target"] = _jax.random.normal(_jax.random.fold_in(key, 0), shape, _jnp.float32)
    for i, name in enumerate(TWIN_WEIGHTS):
        w = inp[name].astype(_jnp.float32)
        if MOMENT_SCALE is None:
            s = _jnp.sqrt(_jnp.mean(_jnp.square(w)) + 1e-30)
        else:
            s = MOMENT_SCALE[name]
        km, kv = _jax.random.split(_jax.random.fold_in(key, i + 1))
        out[name] = w
        out["m_" + name] = s * _jax.random.normal(km, w.shape, _jnp.float32)
        out["v_" + name] = (s * s) * _jax.random.uniform(kv, w.shape, _jnp.float32, 0.5, 1.5)
    if N_MICROBATCH > 1:
        for name, axis in PER_EXAMPLE_BATCH_AXIS.items():
            out[name] = _to_microbatches(out[name], axis)
    return {'x': out['x'], 'attn_norm': out['attn_norm'], 'ffn_norm': out['ffn_norm'], 'a_w_in': out['a_w_in'], 'a_conv': out['a_conv'], 'a_w_out': out['a_w_out'], 'kv_norm': out['kv_norm'], 'w_kvf': out['w_kvf'], 'b_f': out['b_f'], 'k_norm': out['k_norm'], 'b_w_qg': out['b_w_qg'], 'q_norm': out['q_norm'], 'b_w_out': out['b_w_out'], 'ffn_w_up': out['ffn_w_up'], 'ffn_conv': out['ffn_conv'], 'ffn_w_down': out['ffn_w_down'], 'loss_target': out['loss_target'], 'm_attn_norm': out['m_attn_norm'], 'm_ffn_norm': out['m_ffn_norm'], 'm_a_w_in': out['m_a_w_in'], 'm_a_conv': out['m_a_conv'], 'm_a_w_out': out['m_a_w_out'], 'm_kv_norm': out['m_kv_norm'], 'm_w_kvf': out['m_w_kvf'], 'm_b_f': out['m_b_f'], 'm_k_norm': out['m_k_norm'], 'm_b_w_qg': out['m_b_w_qg'], 'm_q_norm': out['m_q_norm'], 'm_b_w_out': out['m_b_w_out'], 'm_ffn_w_up': out['m_ffn_w_up'], 'm_ffn_conv': out['m_ffn_conv'], 'm_ffn_w_down': out['m_ffn_w_down'], 'v_attn_norm': out['v_attn_norm'], 'v_ffn_norm': out['v_ffn_norm'], 'v_a_w_in': out['v_a_w_in'], 'v_a_conv': out['v_a_conv'], 'v_a_w_out': out['v_a_w_out'], 'v_kv_norm': out['v_kv_norm'], 'v_w_kvf': out['v_w_kvf'], 'v_b_f': out['v_b_f'], 'v_k_norm': out['v_k_norm'], 'v_b_w_qg': out['v_b_w_qg'], 'v_q_norm': out['v_q_norm'], 'v_b_w_out': out['v_b_w_out'], 'v_ffn_w_up': out['v_ffn_w_up'], 'v_ffn_conv': out['v_ffn_conv'], 'v_ffn_w_down': out['v_ffn_w_down']}


def _loss(weights, diff, rest, loss_target):
    with _jax.named_scope("forward"):
        args = {**rest, TWIN_DIFF_INPUT: diff, **{k: w.astype(_WEIGHT_DTYPES[k]) for k, w in weights.items()}}
        y = _forward(args)
    with _jax.named_scope("loss_head"):
        err = _jnp.square(y.astype(_jnp.float32) - loss_target)
        return 0.5 * _jnp.sum(_jnp.mean(err, axis=-1)) if err.ndim else 0.5 * err


def _adamw(w, g, m, v):
    m = ADAM_B1 * m + (1.0 - ADAM_B1) * g
    v = ADAM_B2 * v + (1.0 - ADAM_B2) * _jnp.square(g)
    m_hat = m / (1.0 - ADAM_B1 ** ADAM_STEP)
    v_hat = v / (1.0 - ADAM_B2 ** ADAM_STEP)
    delta = -ADAM_LR * (m_hat / (_jnp.sqrt(v_hat) + ADAM_EPS) + ADAM_WD * w)
    return delta, m, v


def reference(x, attn_norm, ffn_norm, a_w_in, a_conv, a_w_out, kv_norm, w_kvf, b_f, k_norm, b_w_qg, q_norm, b_w_out, ffn_w_up, ffn_conv, ffn_w_down, loss_target, m_attn_norm, m_ffn_norm, m_a_w_in, m_a_conv, m_a_w_out, m_kv_norm, m_w_kvf, m_b_f, m_k_norm, m_b_w_qg, m_q_norm, m_b_w_out, m_ffn_w_up, m_ffn_conv, m_ffn_w_down, v_attn_norm, v_ffn_norm, v_a_w_in, v_a_conv, v_a_w_out, v_kv_norm, v_w_kvf, v_b_f, v_k_norm, v_b_w_qg, v_q_norm, v_b_w_out, v_ffn_w_up, v_ffn_conv, v_ffn_w_down):
    given = dict(x=x, attn_norm=attn_norm, ffn_norm=ffn_norm, a_w_in=a_w_in, a_conv=a_conv, a_w_out=a_w_out, kv_norm=kv_norm, w_kvf=w_kvf, b_f=b_f, k_norm=k_norm, b_w_qg=b_w_qg, q_norm=q_norm, b_w_out=b_w_out, ffn_w_up=ffn_w_up, ffn_conv=ffn_conv, ffn_w_down=ffn_w_down, loss_target=loss_target, m_attn_norm=m_attn_norm, m_ffn_norm=m_ffn_norm, m_a_w_in=m_a_w_in, m_a_conv=m_a_conv, m_a_w_out=m_a_w_out, m_kv_norm=m_kv_norm, m_w_kvf=m_w_kvf, m_b_f=m_b_f, m_k_norm=m_k_norm, m_b_w_qg=m_b_w_qg, m_q_norm=m_q_norm, m_b_w_out=m_b_w_out, m_ffn_w_up=m_ffn_w_up, m_ffn_conv=m_ffn_conv, m_ffn_w_down=m_ffn_w_down, v_attn_norm=v_attn_norm, v_ffn_norm=v_ffn_norm, v_a_w_in=v_a_w_in, v_a_conv=v_a_conv, v_a_w_out=v_a_w_out, v_kv_norm=v_kv_norm, v_w_kvf=v_w_kvf, v_b_f=v_b_f, v_k_norm=v_k_norm, v_b_w_qg=v_b_w_qg, v_q_norm=v_q_norm, v_b_w_out=v_b_w_out, v_ffn_w_up=v_ffn_w_up, v_ffn_conv=v_ffn_conv, v_ffn_w_down=v_ffn_w_down)
    weights = {n: given[n] for n in TWIN_WEIGHTS}
    shared = {n: given[n] for n in SHARED_INPUTS}
    per_example = {n: given[n] for n in ['x']}
    grad_fn = _jax.value_and_grad(_loss, argnums=(0, 1))

    def one_microbatch(ex, loss_target):
        ex = dict(ex)
        diff = ex.pop(TWIN_DIFF_INPUT)
        return grad_fn(weights, diff, {**shared, **ex}, loss_target)

    if N_MICROBATCH == 1:
        loss, (grad_w, grad_x) = one_microbatch(per_example, given["loss_target"])
    else:
        def body(carry, xs):
            loss_sum, grad_sum = carry
            l_k, (gw_k, gx_k) = one_microbatch(xs[0], xs[1])
            with _jax.named_scope("update"):
                return (loss_sum + l_k, _jax.tree.map(_jnp.add, grad_sum, gw_k)), gx_k

        init = (_jnp.zeros((), _jnp.float32), _jax.tree.map(_jnp.zeros_like, weights))
        (loss, grad_w), grad_x = _jax.lax.scan(body, init, (per_example, given["loss_target"]))
    with _jax.named_scope("update"):
        delta_w, new_m, new_v = {}, {}, {}
        for n in TWIN_WEIGHTS:
            delta_w[n], new_m[n], new_v[n] = _adamw(weights[n], grad_w[n], given["m_" + n], given["v_" + n])
    return (loss, grad_x, *[grad_w[n] for n in TWIN_WEIGHTS], *[delta_w[n] for n in TWIN_WEIGHTS],
            *[new_m[n] for n in TWIN_WEIGHTS], *[new_v[n] for n in TWIN_WEIGHTS])
```

```python
import functools

import jax
import jax.numpy as jnp
from jax import lax
from jax.experimental import pallas as pl
from jax.experimental.pallas import tpu as pltpu

F32 = jnp.float32
BF = jnp.bfloat16
LANES = 128
SUBLANES = 8
RMS_EPS = 1e-6
FLAT_W = 1024
FLAT_ROW_MULT = 512
N_CHIPS = 4
CONV_W = 3
HALO = SUBLANES

ADAM_LR = 0.001
ADAM_B1 = 0.9
ADAM_B2 = 0.999
ADAM_EPS = 1e-08
ADAM_WD = 0.01
ADAM_STEP = 10

MESH = pl.DeviceIdType.MESH
ANY = pl.BlockSpec(memory_space=pl.ANY)
VMEM = pl.BlockSpec(memory_space=pltpu.VMEM)
NT_DIMS = (((1,), (1,)), ((), ()))
TN_DIMS = (((0,), (0,)), ((), ()))


def _tile(n, pref, mult=LANES):
    t = (min(pref, n) // mult) * mult
    while t >= mult:
        if n % t == 0:
            break
        t -= mult
    if t < mult or (t * 4 < pref and n <= 4 * pref):
        return n
    return t


def _params(*sem):
    return pltpu.CompilerParams(dimension_semantics=sem)


def _norm_matmul(x, g, w, parts, out_dtype, name):
    S, D = x.shape
    C = w.shape[1] // parts
    ts, tn = _tile(S, 512, SUBLANES), _tile(C, 1408)
    npc = C // tn

    def body(x_ref, g_ref, w_ref, o_ref, xn_ref):
        @pl.when(pl.program_id(1) == 0)
        def _():
            xf = x_ref[...]
            r = lax.rsqrt(jnp.mean(xf * xf, axis=-1, keepdims=True) + RMS_EPS)
            xn_ref[...] = (xf * r * g_ref[...]).astype(BF)

        o_ref[0] = jnp.dot(xn_ref[...], w_ref[...], preferred_element_type=F32).astype(out_dtype)

    return pl.pallas_call(
        body, name=name, grid=(S // ts, parts * npc),
        in_specs=[pl.BlockSpec((ts, D), lambda s, n: (s, 0)),
                  pl.BlockSpec((1, D), lambda s, n: (0, 0)),
                  pl.BlockSpec((D, tn), lambda s, n: (0, n))],
        out_specs=[pl.BlockSpec((1, ts, tn), lambda s, n: (n // npc, s, n % npc)),
                   pl.BlockSpec((ts, D), lambda s, n: (s, 0))],
        out_shape=[jax.ShapeDtypeStruct((parts, S, C), out_dtype), jax.ShapeDtypeStruct((S, D), BF)],
        compiler_params=_params("parallel", "arbitrary"),
    )(x, g.reshape(1, D), w)


def _shift_down(u, prev, k, row):
    r = pltpu.roll(u, k, 0)
    for j in range(k):
        r = jnp.where(row == j, prev[HALO - k + j:HALO - k + j + 1, :], r)
    return r


def _shift_up(d, nxt, k, row):
    n = d.shape[0]
    r = pltpu.roll(d, n - k, 0)
    for j in range(k):
        r = jnp.where(row == n - k + j, nxt[j:j + 1, :], r)
    return r


def _conv3(u, prev, w, row):
    return _shift_down(u, prev, 2, row) * w[0:1] + _shift_down(u, prev, 1, row) * w[1:2] + u * w[2:3]


def _conv3_t(d, nxt, w, row):
    return d * w[2:3] + _shift_up(d, nxt, 1, row) * w[1:2] + _shift_up(d, nxt, 2, row) * w[0:1]


def _tap_rows(t0, t1, t2):
    row = lax.broadcasted_iota(jnp.int32, (SUBLANES, t0.shape[1]), 0)
    return jnp.where(row == 0, t0, jnp.where(row == 1, t1, jnp.where(row == 2, t2, 0.0)))


def _pad_conv(cw):
    return jnp.pad(cw, ((0, SUBLANES - CONV_W), (0, 0)))


def _mixer_mid_fwd(proj, cw, name):
    _, S, C = proj.shape
    ts, tc = _tile(S, 512, SUBLANES), _tile(C, 1024)

    def body(b_ref, c_ref, h_ref, cw_ref, z_ref, carry):
        @pl.when(pl.program_id(1) == 0)
        def _():
            carry[...] = jnp.zeros_like(carry)

        u = c_ref[0].astype(F32) * h_ref[0].astype(F32)
        row = lax.broadcasted_iota(jnp.int32, u.shape, 0)
        cv = _conv3(u, carry[...], cw_ref[...], row)
        z_ref[...] = (b_ref[0].astype(F32) * cv).astype(BF)
        carry[...] = u[ts - HALO:ts, :]

    part = lambda p: pl.BlockSpec((1, ts, tc), lambda c, s: (p, s, c))
    return pl.pallas_call(
        body, name=name, grid=(C // tc, S // ts),
        in_specs=[part(0), part(1), part(2), pl.BlockSpec((SUBLANES, tc), lambda c, s: (0, c))],
        out_specs=pl.BlockSpec((ts, tc), lambda c, s: (s, c)),
        out_shape=jax.ShapeDtypeStruct((S, C), BF),
        scratch_shapes=[pltpu.VMEM((HALO, tc), F32)],
        compiler_params=_params("parallel", "arbitrary"),
    )(proj, proj, proj, _pad_conv(cw))


def _ffn_mid_fwd(up, cw, name):
    _, S, C = up.shape
    ts, tc = _tile(S, 512, SUBLANES), _tile(C, 1408)

    def body(a_ref, g_ref, cw_ref, z_ref, carry):
        @pl.when(pl.program_id(1) == 0)
        def _():
            carry[...] = jnp.zeros_like(carry)

        a_pre = a_ref[0].astype(F32)
        row = lax.broadcasted_iota(jnp.int32, a_pre.shape, 0)
        a = _conv3(a_pre, carry[...], cw_ref[...], row)
        z_ref[...] = (a * jax.nn.sigmoid(a) * g_ref[0].astype(F32)).astype(BF)
        carry[...] = a_pre[ts - HALO:ts, :]

    part = lambda p: pl.BlockSpec((1, ts, tc), lambda c, s: (p, s, c))
    return pl.pallas_call(
        body, name=name, grid=(C // tc, S // ts),
        in_specs=[part(0), part(1), pl.BlockSpec((SUBLANES, tc), lambda c, s: (0, c))],
        out_specs=pl.BlockSpec((ts, tc), lambda c, s: (s, c)),
        out_shape=jax.ShapeDtypeStruct((S, C), BF),
        scratch_shapes=[pltpu.VMEM((HALO, tc), F32)],
        compiler_params=_params("parallel", "arbitrary"),
    )(up, up, _pad_conv(cw))


def _halo_specs(p, ts, tc, n_s):
    per = ts // HALO
    last = n_s * per - 1
    before = pl.BlockSpec((1, HALO, tc), lambda c, s: (p, jnp.maximum(s * per - 1, 0), c))
    after = pl.BlockSpec((1, HALO, tc), lambda c, s: (p, jnp.minimum((s + 1) * per, last), c))
    return before, after


def _mixer_mid_bwd(proj, dz, cw, name):
    _, S, C = proj.shape
    ts, tc = _tile(S, 512, SUBLANES), _tile(C, 1024)
    n_s = S // ts

    def body(b_ref, c_ref, h_ref, dz_ref, cp_ref, hp_ref, bn_ref, dzn_ref, cw_ref, d_ref, dcw_ref):
        s = pl.program_id(1)
        w = cw_ref[...]
        b, c, h = b_ref[0].astype(F32), c_ref[0].astype(F32), h_ref[0].astype(F32)
        dz_t = dz_ref[0]
        row = lax.broadcasted_iota(jnp.int32, b.shape, 0)
        u = c * h
        prev = jnp.where(s > 0, cp_ref[0].astype(F32) * hp_ref[0].astype(F32), 0.0)
        u1, u2 = _shift_down(u, prev, 1, row), _shift_down(u, prev, 2, row)
        cv = u2 * w[0:1] + u1 * w[1:2] + u * w[2:3]
        dcv = dz_t * b
        nxt = jnp.where(s < n_s - 1, dzn_ref[0] * bn_ref[0].astype(F32), 0.0)
        du = _conv3_t(dcv, nxt, w, row)
        d_ref[0] = (dz_t * cv).astype(BF)
        d_ref[1] = (du * h).astype(BF)
        d_ref[2] = (du * c).astype(BF)
        part = _tap_rows(jnp.sum(dcv * u2, axis=0, keepdims=True), jnp.sum(dcv * u1, axis=0, keepdims=True),
                         jnp.sum(dcv * u, axis=0, keepdims=True))

        @pl.when(s == 0)
        def _():
            dcw_ref[...] = part

        @pl.when(s > 0)
        def _():
            dcw_ref[...] += part

    part_spec = lambda p: pl.BlockSpec((1, ts, tc), lambda c, s: (p, s, c))
    c_before, _ = _halo_specs(1, ts, tc, n_s)
    h_before, _ = _halo_specs(2, ts, tc, n_s)
    _, b_after = _halo_specs(0, ts, tc, n_s)
    _, dz_after = _halo_specs(0, ts, tc, n_s)
    dproj, dcw = pl.pallas_call(
        body, name=name, grid=(C // tc, n_s),
        in_specs=[part_spec(0), part_spec(1), part_spec(2), part_spec(0), c_before, h_before, b_after, dz_after,
                  pl.BlockSpec((SUBLANES, tc), lambda c, s: (0, c))],
        out_specs=[pl.BlockSpec((3, ts, tc), lambda c, s: (0, s, c)),
                   pl.BlockSpec((SUBLANES, tc), lambda c, s: (0, c))],
        out_shape=[jax.ShapeDtypeStruct((3, S, C), BF), jax.ShapeDtypeStruct((SUBLANES, C), F32)],
        compiler_params=_params("parallel", "arbitrary"),
    )(proj, proj, proj, dz[None], proj, proj, proj, dz[None], _pad_conv(cw))
    return dproj, dcw[:CONV_W]


def _ffn_mid_bwd(up, dz, cw, name):
    _, S, C = up.shape
    ts, tc = _tile(S, 512, SUBLANES), _tile(C, 1408)
    n_s = S // ts

    def dact(a, g, dz_t):
        sg = jax.nn.sigmoid(a)
        return dz_t * g * (sg * (1.0 + a * (1.0 - sg))), dz_t * (a * sg)

    def body(a_ref, g_ref, dz_ref, ap_ref, an_ref, gn_ref, dzn_ref, cw_ref, d_ref, dcw_ref):
        s = pl.program_id(1)
        w = cw_ref[...]
        a_pre, g = a_ref[0].astype(F32), g_ref[0].astype(F32)
        row = lax.broadcasted_iota(jnp.int32, a_pre.shape, 0)
        prev = jnp.where(s > 0, ap_ref[0].astype(F32), 0.0)
        a1, a2 = _shift_down(a_pre, prev, 1, row), _shift_down(a_pre, prev, 2, row)
        a = a2 * w[0:1] + a1 * w[1:2] + a_pre * w[2:3]
        da, dg = dact(a, g, dz_ref[0])
        an_pre = an_ref[0].astype(F32)
        row8 = lax.broadcasted_iota(jnp.int32, an_pre.shape, 0)
        a_next = _conv3(an_pre, a_pre[ts - HALO:ts, :], w, row8)
        da_next, _ = dact(a_next, gn_ref[0].astype(F32), dzn_ref[0])
        nxt = jnp.where(s < n_s - 1, da_next, 0.0)
        d_ref[0] = _conv3_t(da, nxt, w, row).astype(BF)
        d_ref[1] = dg.astype(BF)
        part = _tap_rows(jnp.sum(da * a2, axis=0, keepdims=True), jnp.sum(da * a1, axis=0, keepdims=True),
                         jnp.sum(da * a_pre, axis=0, keepdims=True))

        @pl.when(s == 0)
        def _():
            dcw_ref[...] = part

        @pl.when(s > 0)
        def _():
            dcw_ref[...] += part

    part_spec = lambda p: pl.BlockSpec((1, ts, tc), lambda c, s: (p, s, c))
    a_before, a_after = _halo_specs(0, ts, tc, n_s)
    _, g_after = _halo_specs(1, ts, tc, n_s)
    _, dz_after = _halo_specs(0, ts, tc, n_s)
    dup, dcw = pl.pallas_call(
        body, name=name, grid=(C // tc, n_s),
        in_specs=[part_spec(0), part_spec(1), part_spec(0), a_before, a_after, g_after, dz_after,
                  pl.BlockSpec((SUBLANES, tc), lambda c, s: (0, c))],
        out_specs=[pl.BlockSpec((2, ts, tc), lambda c, s: (0, s, c)),
                   pl.BlockSpec((SUBLANES, tc), lambda c, s: (0, c))],
        out_shape=[jax.ShapeDtypeStruct((2, S, C), BF), jax.ShapeDtypeStruct((SUBLANES, C), F32)],
        compiler_params=_params("parallel", "arbitrary"),
    )(up, up, dz[None], up, up, up, dz[None], _pad_conv(cw))
    return dup, dcw[:CONV_W]


def _matmul_residual(z, w, x, name):
    S, K = z.shape
    D = w.shape[1]
    ts = _tile(S, 512, SUBLANES)

    def body(z_ref, w_ref, x_ref, o_ref):
        o_ref[...] = x_ref[...] + jnp.dot(z_ref[...], w_ref[...], preferred_element_type=F32)

    return pl.pallas_call(
        body, name=name, grid=(S // ts,),
        in_specs=[pl.BlockSpec((ts, K), lambda s: (s, 0)), pl.BlockSpec((K, D), lambda s: (0, 0)),
                  pl.BlockSpec((ts, D), lambda s: (s, 0))],
        out_specs=pl.BlockSpec((ts, D), lambda s: (s, 0)),
        out_shape=jax.ShapeDtypeStruct((S, D), F32),
        compiler_params=_params("parallel"),
    )(z, w, x)


def _matmul_nt(a, w, name):
    S, K = a.shape
    N = w.shape[0]
    ts, tn = _tile(S, 512, SUBLANES), _tile(N, 1408)

    def body(a_ref, w_ref, o_ref, abf):
        @pl.when(pl.program_id(1) == 0)
        def _():
            abf[...] = a_ref[...].astype(BF)

        o_ref[...] = lax.dot_general(abf[...], w_ref[...], NT_DIMS, preferred_element_type=F32)

    return pl.pallas_call(
        body, name=name, grid=(S // ts, N // tn),
        in_specs=[pl.BlockSpec((ts, K), lambda s, n: (s, 0)), pl.BlockSpec((tn, K), lambda s, n: (n, 0))],
        out_specs=pl.BlockSpec((ts, tn), lambda s, n: (s, n)),
        out_shape=jax.ShapeDtypeStruct((S, N), F32),
        scratch_shapes=[pltpu.VMEM((ts, K), BF)],
        compiler_params=_params("parallel", "arbitrary"),
    )(a, w)


def _wgrad(a, b, name):
    S, M = a.shape
    P, _, C = b.shape
    tm, tn, tk = _tile(M, 1408), _tile(C, 1408), _tile(S, 512, SUBLANES)
    nnc = C // tn

    def body(a_ref, b_ref, o_ref):
        @pl.when(pl.program_id(2) == 0)
        def _():
            o_ref[...] = jnp.zeros_like(o_ref)

        o_ref[...] += lax.dot_general(a_ref[...], b_ref[0].astype(BF), TN_DIMS, preferred_element_type=F32)

    return pl.pallas_call(
        body, name=name, grid=(M // tm, P * nnc, S // tk),
        in_specs=[pl.BlockSpec((tk, tm), lambda m, n, k: (k, m)),
                  pl.BlockSpec((1, tk, tn), lambda m, n, k: (n // nnc, k, n % nnc))],
        out_specs=pl.BlockSpec((tm, tn), lambda m, n, k: (m, n)),
        out_shape=jax.ShapeDtypeStruct((M, P * C), F32),
        compiler_params=_params("parallel", "parallel", "arbitrary"),
    )(a, b)


def _dnorm(dp, w, x, g, dy, name):
    P, S, C = dp.shape
    D = x.shape[1]
    ts, tk = _tile(S, 512, SUBLANES), _tile(C, 1408)
    nkc = C // tk
    n_k = P * nkc

    def body(dp_ref, w_ref, x_ref, g_ref, dy_ref, dx_ref, dg_ref, acc):
        s, k = pl.program_id(0), pl.program_id(1)

        @pl.when(k == 0)
        def _():
            acc[...] = jnp.zeros_like(acc)

        acc[...] += lax.dot_general(dp_ref[0], w_ref[...], NT_DIMS, preferred_element_type=F32)

        @pl.when((s == 0) & (k == 0))
        def _():
            dg_ref[...] = jnp.zeros_like(dg_ref)

        @pl.when(k == n_k - 1)
        def _():
            xf = x_ref[...]
            r = lax.rsqrt(jnp.mean(xf * xf, axis=-1, keepdims=True) + RMS_EPS)
            xhat = xf * r
            dxn = acc[...]
            dxhat = dxn * g_ref[...]
            dx_ref[...] = dy_ref[...] + r * (dxhat - xhat * jnp.mean(dxhat * xhat, axis=-1, keepdims=True))
            dg_ref[...] += jnp.broadcast_to(jnp.sum(dxn * xhat, axis=0, keepdims=True), dg_ref.shape)

    dx, dg = pl.pallas_call(
        body, name=name, grid=(S // ts, n_k),
        in_specs=[pl.BlockSpec((1, ts, tk), lambda s, k: (k // nkc, s, k % nkc)),
                  pl.BlockSpec((D, tk), lambda s, k: (0, k)),
                  pl.BlockSpec((ts, D), lambda s, k: (s, 0)),
                  pl.BlockSpec((1, D), lambda s, k: (0, 0)),
                  pl.BlockSpec((ts, D), lambda s, k: (s, 0))],
        out_specs=[pl.BlockSpec((ts, D), lambda s, k: (s, 0)),
                   pl.BlockSpec((SUBLANES, D), lambda s, k: (0, 0))],
        out_shape=[jax.ShapeDtypeStruct((S, D), F32), jax.ShapeDtypeStruct((SUBLANES, D), F32)],
        scratch_shapes=[pltpu.VMEM((ts, D), F32)],
        compiler_params=_params("arbitrary", "arbitrary"),
    )(dp, w, x, g.reshape(1, D), dy)
    return dx, dg[0]


def _head_masks(shape, hd):
    lane = lax.broadcasted_iota(jnp.int32, shape, 1)
    return lane < hd


def _pair_sum(v, lo):
    s0 = jnp.sum(jnp.where(lo, v, 0.0), axis=-1, keepdims=True)
    s1 = jnp.sum(jnp.where(lo, 0.0, v), axis=-1, keepdims=True)
    return jnp.where(lo, s0, s1)


def _headnorm(src, part, colblk, w, scale, D, name):
    S = src.shape[1]
    hd = w.shape[0]
    ts = _tile(S, 512, SUBLANES)
    w2 = jnp.tile(w, LANES // hd).reshape(1, LANES)

    def body(x_ref, w_ref, o_ref):
        lo = _head_masks((ts, LANES), hd)
        for t in range(D // LANES):
            xt = x_ref[0, :, t * LANES:(t + 1) * LANES]
            r = lax.rsqrt(_pair_sum(xt * xt, lo) * (1.0 / hd) + RMS_EPS)
            o_ref[:, t * LANES:(t + 1) * LANES] = (xt * r * w_ref[...] * scale).astype(BF)

    return pl.pallas_call(
        body, name=name, grid=(S // ts,),
        in_specs=[pl.BlockSpec((1, ts, D), lambda s: (part, s, colblk)), pl.BlockSpec((1, LANES), lambda s: (0, 0))],
        out_specs=pl.BlockSpec((ts, D), lambda s: (s, 0)),
        out_shape=jax.ShapeDtypeStruct((S, D), BF),
        compiler_params=_params("parallel"),
    )(src, w2)


def _headnorm_bwd(src, part, colblk, w, dys, D, name):
    S = src.shape[1]
    hd = w.shape[0]
    ts = _tile(S, 512, SUBLANES)
    w2 = jnp.tile(w, LANES // hd).reshape(1, LANES)
    n_dy = len(dys)

    def body(x_ref, w_ref, *rest):
        dy_refs, dx_ref, dw_ref = rest[:n_dy], rest[n_dy], rest[n_dy + 1]

        @pl.when(pl.program_id(0) == 0)
        def _():
            dw_ref[...] = jnp.zeros_like(dw_ref)

        lo = _head_masks((ts, LANES), hd)
        for t in range(D // LANES):
            cols = slice(t * LANES, (t + 1) * LANES)
            xt = x_ref[0, :, cols]
            dy = dy_refs[0][:, cols]
            for other in dy_refs[1:]:
                dy = dy + other[:, cols]
            r = lax.rsqrt(_pair_sum(xt * xt, lo) * (1.0 / hd) + RMS_EPS)
            xhat = xt * r
            dxhat = dy * w_ref[...]
            mean = _pair_sum(dxhat * xhat, lo) * (1.0 / hd)
            dx_ref[:, cols] = (r * (dxhat - xhat * mean)).astype(BF)
            dw_ref[:, cols] += jnp.broadcast_to(jnp.sum(dy * xhat, axis=0, keepdims=True), (SUBLANES, LANES))

    dx, dw = pl.pallas_call(
        body, name=name, grid=(S // ts,),
        in_specs=[pl.BlockSpec((1, ts, D), lambda s: (part, s, colblk)), pl.BlockSpec((1, LANES), lambda s: (0, 0))]
        + [pl.BlockSpec((ts, D), lambda s: (s, 0))] * n_dy,
        out_specs=[pl.BlockSpec((ts, D), lambda s: (s, 0)), pl.BlockSpec((SUBLANES, D), lambda s: (0, 0))],
        out_shape=[jax.ShapeDtypeStruct((S, D), BF), jax.ShapeDtypeStruct((SUBLANES, D), F32)],
        compiler_params=_params("arbitrary"),
    )(src, w2, *dys)
    return dx, jnp.sum(dw[0].reshape(D // hd, hd), axis=0)


def _tri(n, lower):
    r, c = lax.broadcasted_iota(jnp.int32, (n, n), 0), lax.broadcasted_iota(jnp.int32, (n, n), 1)
    return jnp.where((c <= r) if lower else (c >= r), 1.0, 0.0).astype(BF)


def _dot_exact(t, v):
    hi = v.astype(BF)
    r1 = v - hi.astype(F32)
    mid = r1.astype(BF)
    lo = (r1 - mid.astype(F32)).astype(BF)
    dot = lambda u: jnp.dot(t, u, preferred_element_type=F32)
    return dot(hi) + dot(mid) + dot(lo)


def _gate_fwd(kvf, b_pad, colblk, name):
    S = kvf.shape[1]
    ts = _tile(S, 512, SUBLANES)

    def body(f_ref, b_ref, c_ref, carry):
        @pl.when(pl.program_id(0) == 0)
        def _():
            carry[...] = jnp.zeros_like(carry)

        f = f_ref[0] + b_ref[...]
        ls = jnp.minimum(f, 0.0) - jnp.log1p(jnp.exp(-jnp.abs(f)))
        tri = _tri(ts, lower=True)
        c = _dot_exact(tri, ls) + carry[0:1, :]
        c_ref[...] = c
        carry[...] = jnp.broadcast_to(c[ts - 1:ts, :], carry.shape)

    return pl.pallas_call(
        body, name=name, grid=(S // ts,),
        in_specs=[pl.BlockSpec((1, ts, LANES), lambda s: (0, s, colblk)), pl.BlockSpec((1, LANES), lambda s: (0, 0))],
        out_specs=pl.BlockSpec((ts, LANES), lambda s: (s, 0)),
        out_shape=jax.ShapeDtypeStruct((S, LANES), F32),
        scratch_shapes=[pltpu.VMEM((SUBLANES, LANES), F32)],
        compiler_params=_params("arbitrary"),
    )(kvf, b_pad)


def _gate_bwd(dc, kvf, b_pad, colblk, name):
    S = kvf.shape[1]
    ts = _tile(S, 512, SUBLANES)
    n_s = S // ts

    def body(dc_ref, f_ref, b_ref, df_ref, db_ref, carry):
        @pl.when(pl.program_id(0) == 0)
        def _():
            carry[...] = jnp.zeros_like(carry)
            db_ref[...] = jnp.zeros_like(db_ref)

        tri = _tri(ts, lower=False)
        dls = _dot_exact(tri, dc_ref[...]) + carry[0:1, :]
        f = f_ref[0] + b_ref[...]
        df = dls * jax.nn.sigmoid(-f)
        df_ref[...] = df
        db_ref[...] += jnp.broadcast_to(jnp.sum(df, axis=0, keepdims=True), db_ref.shape)
        carry[...] = jnp.broadcast_to(dls[0:1, :], carry.shape)

    df, db = pl.pallas_call(
        body, name=name, grid=(n_s,),
        in_specs=[pl.BlockSpec((ts, LANES), lambda s: (n_s - 1 - s, 0)),
                  pl.BlockSpec((1, ts, LANES), lambda s: (0, n_s - 1 - s, colblk)),
                  pl.BlockSpec((1, LANES), lambda s: (0, 0))],
        out_specs=[pl.BlockSpec((ts, LANES), lambda s: (n_s - 1 - s, 0)),
                   pl.BlockSpec((SUBLANES, LANES), lambda s: (0, 0))],
        out_shape=[jax.ShapeDtypeStruct((S, LANES), F32), jax.ShapeDtypeStruct((SUBLANES, LANES), F32)],
        scratch_shapes=[pltpu.VMEM((SUBLANES, LANES), F32)],
        compiler_params=_params("arbitrary"),
    )(dc, kvf, b_pad)
    return df, db[0]


def _attn_tile(S):
    return _tile(S, 512, LANES)


def _split_heads(v, lo):
    zero = jnp.zeros_like(v)
    return jnp.where(lo, v, zero), jnp.where(lo, zero, v)


def _attn_fwd(qn, kn, vb, qg, ccol, crow, hd, name):
    S, D = qn.shape
    P = D // LANES
    tq = _attn_tile(S)
    nq = S // tq

    def body(q_ref, k_ref, v_ref, g_ref, cq_ref, ck_ref, o_ref, og_ref, lse_ref):
        qi = pl.program_id(1)
        lo = _head_masks((tq, LANES), hd)
        qh = _split_heads(q_ref[...], lo)
        cq = cq_ref[0]
        causal = lax.broadcasted_iota(jnp.int32, (tq, tq), 1) <= lax.broadcasted_iota(jnp.int32, (tq, tq), 0)

        def block(ki, carry, masked):
            off = pl.multiple_of(ki * tq, tq)
            kb, vt = k_ref[pl.ds(off, tq), :], v_ref[pl.ds(off, tq), :]
            ck = ck_ref[0, ki]
            out = []
            for h in range(2):
                m, l, acc = carry[h]
                s = lax.dot_general(qh[h], kb, NT_DIMS, preferred_element_type=F32)
                s = s + cq[:, h:h + 1] - ck[h:h + 1, :]
                if masked:
                    s = jnp.where(causal, s, -jnp.inf)
                m_new = jnp.maximum(m, jnp.max(s, axis=-1, keepdims=True))
                p = jnp.exp(s - m_new)
                alpha = jnp.exp(m - m_new)
                l = alpha * l + jnp.sum(p, axis=-1, keepdims=True)
                acc = alpha * acc + jnp.dot(p.astype(BF), vt, preferred_element_type=F32)
                out.append((m_new, l, acc))
            return tuple(out)

        init = tuple((jnp.full((tq, 1), -jnp.inf, F32), jnp.zeros((tq, 1), F32), jnp.zeros((tq, LANES), F32))
                     for _ in range(2))
        carry = lax.fori_loop(0, qi, lambda ki, c: block(ki, c, False), init)
        (m0, l0, a0), (m1, l1, a1) = block(qi, carry, True)
        o = jnp.where(lo, a0 / l0, a1 / l1)
        o_ref[...] = o
        og_ref[...] = (o * jax.nn.sigmoid(g_ref[0])).astype(BF)
        lane2 = lax.broadcasted_iota(jnp.int32, (tq, 2), 1)
        lse_ref[0] = jnp.where(lane2 == 0, m0 + jnp.log(l0), m1 + jnp.log(l1))

    return pl.pallas_call(
        body, name=name, grid=(P, nq),
        in_specs=[pl.BlockSpec((tq, LANES), lambda p, i: (i, p)),
                  pl.BlockSpec((S, LANES), lambda p, i: (0, p)),
                  pl.BlockSpec((S, LANES), lambda p, i: (0, p)),
                  pl.BlockSpec((1, tq, LANES), lambda p, i: (1, i, p)),
                  pl.BlockSpec((1, tq, 2), lambda p, i: (p, i, 0)),
                  pl.BlockSpec((1, nq, 2, tq), lambda p, i: (p, 0, 0, 0))],
        out_specs=[pl.BlockSpec((tq, LANES), lambda p, i: (i, p)),
                   pl.BlockSpec((tq, LANES), lambda p, i: (i, p)),
                   pl.BlockSpec((1, tq, 2), lambda p, i: (p, i, 0))],
        out_shape=[jax.ShapeDtypeStruct((S, D), F32), jax.ShapeDtypeStruct((S, D), BF),
                   jax.ShapeDtypeStruct((P, S, 2), F32)],
        compiler_params=_params("parallel", "arbitrary"),
    )(qn, kn, vb, qg, ccol, crow)


def _attn_out_bwd(dog, o, qg, hd, name):
    S, D = o.shape
    P = D // LANES
    ts = _tile(S, 512, SUBLANES)

    def body(dog_ref, o_ref, g_ref, do_ref, dg_ref, dv_ref):
        lo = _head_masks((ts, LANES), hd)
        lane2 = lax.broadcasted_iota(jnp.int32, (ts, 2), 1)
        for t in range(P):
            cols = slice(t * LANES, (t + 1) * LANES)
            sg = jax.nn.sigmoid(g_ref[0, :, cols])
            dog_t, o_t = dog_ref[:, cols], o_ref[:, cols]
            do = (dog_t * sg).astype(BF)
            do_ref[:, cols] = do
            dg_ref[:, cols] = (dog_t * o_t * sg * (1.0 - sg)).astype(BF)
            prod = do.astype(F32) * o_t
            d0 = jnp.sum(jnp.where(lo, prod, 0.0), axis=-1, keepdims=True)
            d1 = jnp.sum(jnp.where(lo, 0.0, prod), axis=-1, keepdims=True)
            dv_ref[t] = jnp.where(lane2 == 0, d0, d1)

    return pl.pallas_call(
        body, name=name, grid=(S // ts,),
        in_specs=[pl.BlockSpec((ts, D), lambda s: (s, 0)), pl.BlockSpec((ts, D), lambda s: (s, 0)),
                  pl.BlockSpec((1, ts, D), lambda s: (1, s, 0))],
        out_specs=[pl.BlockSpec((ts, D), lambda s: (s, 0)), pl.BlockSpec((ts, D), lambda s: (s, 0)),
                   pl.BlockSpec((P, ts, 2), lambda s: (0, s, 0))],
        out_shape=[jax.ShapeDtypeStruct((S, D), BF), jax.ShapeDtypeStruct((S, D), BF),
                   jax.ShapeDtypeStruct((P, S, 2), F32)],
        compiler_params=_params("parallel"),
    )(dog, o, qg)


def _attn_dq(qn, kn, vb, do, ccol, crow, lse, dvec, hd, name):
    S, D = qn.shape
    P = D // LANES
    tq = _attn_tile(S)
    nq = S // tq
    scale = hd ** -0.5

    def body(q_ref, k_ref, v_ref, do_ref, cq_ref, ck_ref, lse_ref, dv_ref, dq_ref, dv2_ref):
        qi = pl.program_id(1)
        lo = _head_masks((tq, LANES), hd)
        qh = _split_heads(q_ref[...], lo)
        doh = _split_heads(do_ref[...], lo)
        cq, lse_t, dv_t = cq_ref[0], lse_ref[0], dv_ref[0]
        causal = lax.broadcasted_iota(jnp.int32, (tq, tq), 1) <= lax.broadcasted_iota(jnp.int32, (tq, tq), 0)

        def block(ki, carry, masked):
            off = pl.multiple_of(ki * tq, tq)
            kb, vt = k_ref[pl.ds(off, tq), :], v_ref[pl.ds(off, tq), :]
            ck = ck_ref[0, ki]
            out = []
            for h in range(2):
                s = lax.dot_general(qh[h], kb, NT_DIMS, preferred_element_type=F32)
                s = s + cq[:, h:h + 1] - ck[h:h + 1, :]
                p = jnp.exp(s - lse_t[:, h:h + 1])
                if masked:
                    p = jnp.where(causal, p, 0.0)
                dp = lax.dot_general(doh[h], vt, NT_DIMS, preferred_element_type=F32)
                ds = p * (dp - dv_t[:, h:h + 1])
                acc, rs = carry[h]
                out.append((acc + jnp.dot(ds.astype(BF), kb, preferred_element_type=F32),
                            rs + jnp.sum(ds, axis=-1, keepdims=True)))
            return tuple(out)

        init = tuple((jnp.zeros((tq, LANES), F32), jnp.zeros((tq, 1), F32)) for _ in range(2))
        carry = lax.fori_loop(0, qi, lambda ki, c: block(ki, c, False), init)
        (d0, r0), (d1, r1) = block(qi, carry, True)
        dq_ref[...] = jnp.where(lo, d0, d1) * scale
        lane2 = lax.broadcasted_iota(jnp.int32, (tq, 2), 1)
        dv2_ref[0] = dv_t + jnp.where(lane2 == 0, r0, r1)

    return pl.pallas_call(
        body, name=name, grid=(P, nq),
        in_specs=[pl.BlockSpec((tq, LANES), lambda p, i: (i, p)),
                  pl.BlockSpec((S, LANES), lambda p, i: (0, p)),
                  pl.BlockSpec((S, LANES), lambda p, i: (0, p)),
                  pl.BlockSpec((tq, LANES), lambda p, i: (i, p)),
                  pl.BlockSpec((1, tq, 2), lambda p, i: (p, i, 0)),
                  pl.BlockSpec((1, nq, 2, tq), lambda p, i: (p, 0, 0, 0)),
                  pl.BlockSpec((1, tq, 2), lambda p, i: (p, i, 0)),
                  pl.BlockSpec((1, tq, 2), lambda p, i: (p, i, 0))],
        out_specs=[pl.BlockSpec((tq, LANES), lambda p, i: (i, p)),
                   pl.BlockSpec((1, tq, 2), lambda p, i: (p, i, 0))],
        out_shape=[jax.ShapeDtypeStruct((S, D), F32), jax.ShapeDtypeStruct((P, S, 2), F32)],
        compiler_params=_params("parallel", "arbitrary"),
    )(qn, kn, vb, do, ccol, crow, lse, dvec)


def _attn_dkv(qn, kn, vb, do, ccol, crow, lse_row, dvec_row, hd, name):
    S, D = qn.shape
    P = D // LANES
    tk = _attn_tile(S)
    nk = S // tk

    def body(q_ref, do_ref, k_ref, v_ref, ckc_ref, cqr_ref, lse_ref, dvr_ref, dk_ref, dv_ref, dc_ref):
        ki = pl.program_id(1)
        lo = _head_masks((tk, LANES), hd)
        kh = _split_heads(k_ref[...], lo)
        vh = _split_heads(v_ref[...], lo)
        ckc = ckc_ref[0]
        causal_t = lax.broadcasted_iota(jnp.int32, (tk, tk), 0) <= lax.broadcasted_iota(jnp.int32, (tk, tk), 1)

        def block(qi, carry, masked):
            off = pl.multiple_of(qi * tk, tk)
            qb, dob = q_ref[pl.ds(off, tk), :], do_ref[pl.ds(off, tk), :]
            cq, lse_t, dv_t = cqr_ref[0, qi], lse_ref[0, qi], dvr_ref[0, qi]
            out = []
            for h in range(2):
                dk, dv, dc = carry[h]
                st = lax.dot_general(kh[h], qb, NT_DIMS, preferred_element_type=F32)
                st = st + cq[h:h + 1, :] - ckc[:, h:h + 1]
                pt = jnp.exp(st - lse_t[h:h + 1, :])
                if masked:
                    pt = jnp.where(causal_t, pt, 0.0)
                dv = dv + jnp.dot(pt.astype(BF), dob, preferred_element_type=F32)
                dpt = lax.dot_general(vh[h], dob, NT_DIMS, preferred_element_type=F32)
                dst = pt * (dpt - dv_t[h:h + 1, :])
                dk = dk + jnp.dot(dst.astype(BF), qb, preferred_element_type=F32)
                dc = dc - jnp.sum(dst, axis=-1, keepdims=True)
                out.append((dk, dv, dc))
            return tuple(out)

        init = tuple((jnp.zeros((tk, LANES), F32), jnp.zeros((tk, LANES), F32), jnp.zeros((tk, 1), F32))
                     for _ in range(2))
        carry = block(ki, init, True)
        (dk0, dv0, dc0), (dk1, dv1, dc1) = lax.fori_loop(ki + 1, nk, lambda qi, c: block(qi, c, False), carry)
        dk_ref[...] = jnp.where(lo, dk0, dk1)
        dv_ref[...] = jnp.where(lo, dv0, dv1)
        lane2 = lax.broadcasted_iota(jnp.int32, (tk, 2), 1)
        dc_ref[0] = jnp.where(lane2 == 0, dc0, dc1)

    row_spec = pl.BlockSpec((1, nk, 2, tk), lambda p, i: (p, 0, 0, 0))
    return pl.pallas_call(
        body, name=name, grid=(P, nk),
        in_specs=[pl.BlockSpec((S, LANES), lambda p, i: (0, p)),
                  pl.BlockSpec((S, LANES), lambda p, i: (0, p)),
                  pl.BlockSpec((tk, LANES), lambda p, i: (i, p)),
                  pl.BlockSpec((tk, LANES), lambda p, i: (i, p)),
                  pl.BlockSpec((1, tk, 2), lambda p, i: (p, i, 0)),
                  row_spec, row_spec, row_spec],
        out_specs=[pl.BlockSpec((tk, LANES), lambda p, i: (i, p)),
                   pl.BlockSpec((tk, LANES), lambda p, i: (i, p)),
                   pl.BlockSpec((1, tk, 2), lambda p, i: (p, i, 0))],
        out_shape=[jax.ShapeDtypeStruct((S, D), F32), jax.ShapeDtypeStruct((S, D), F32),
                   jax.ShapeDtypeStruct((P, S, 2), F32)],
        compiler_params=_params("parallel", "arbitrary"),
    )(qn, do, kn, vb, ccol, crow, lse_row, dvec_row)


def _loss_head(y, t, name):
    S, D = y.shape
    ts = _tile(S, 512, SUBLANES)

    def body(y_ref, t_ref, dy_ref, l_ref):
        @pl.when(pl.program_id(0) == 0)
        def _():
            l_ref[...] = jnp.zeros_like(l_ref)

        e = y_ref[...] - t_ref[...]
        dy_ref[...] = e * (1.0 / D)
        part = 0.5 * jnp.sum(jnp.mean(e * e, axis=-1, keepdims=True), axis=0, keepdims=True)
        l_ref[...] += jnp.broadcast_to(part, l_ref.shape)

    return pl.pallas_call(
        body, name=name, grid=(S // ts,),
        in_specs=[pl.BlockSpec((ts, D), lambda s: (s, 0)), pl.BlockSpec((ts, D), lambda s: (s, 0))],
        out_specs=[pl.BlockSpec((ts, D), lambda s: (s, 0)), pl.BlockSpec((SUBLANES, LANES), lambda s: (0, 0))],
        out_shape=[jax.ShapeDtypeStruct((S, D), F32), jax.ShapeDtypeStruct((SUBLANES, LANES), F32)],
        compiler_params=_params("arbitrary"),
    )(y, t)


def _adamw(w, g, m, v, name):
    shape = w.shape
    cols = shape[-1]
    as2d = lambda a: a.reshape(-1, cols)
    rows = as2d(w).shape[0]
    tr = _tile(rows, 256, SUBLANES) if rows % SUBLANES == 0 else rows
    c1 = 1.0 - ADAM_B1 ** ADAM_STEP
    c2 = 1.0 - ADAM_B2 ** ADAM_STEP

    def body(w_ref, g_ref, m_ref, v_ref, d_ref, nm_ref, nv_ref):
        gg = g_ref[...]
        nm = ADAM_B1 * m_ref[...] + (1.0 - ADAM_B1) * gg
        nv = ADAM_B2 * v_ref[...] + (1.0 - ADAM_B2) * (gg * gg)
        d_ref[...] = -ADAM_LR * ((nm / c1) / (jnp.sqrt(nv / c2) + ADAM_EPS) + ADAM_WD * w_ref[...])
        nm_ref[...] = nm
        nv_ref[...] = nv

    spec = pl.BlockSpec((tr, cols), lambda r: (r, 0))
    outs = pl.pallas_call(
        body, name=name, grid=(rows // tr,), in_specs=[spec] * 4, out_specs=[spec] * 3,
        out_shape=[jax.ShapeDtypeStruct((rows, cols), F32)] * 3,
        compiler_params=_params("parallel"),
    )(as2d(w), as2d(g), as2d(m), as2d(v))
    return tuple(o.reshape(shape) for o in outs)


def _place():
    return lax.axis_index("x"), lax.axis_index("y"), lax.axis_index("c")


def _other_chips(x, y):
    return [(1 - x, y), (x, 1 - y), (1 - x, 1 - y)]


def _remote(src, dst, send_sems, recv_sems, k, to):
    return pltpu.make_async_remote_copy(src_ref=src, dst_ref=dst, send_sem=send_sems.at[k], recv_sem=recv_sems.at[k],
                                        device_id=to, device_id_type=MESH)


def _allgather_weights(flat):
    R, W = flat.shape
    rh = R // 2

    def body(src, dst, send_sems, recv_sems, local_sem):
        x, y, c = _place()
        me = 2 * x + y
        sib = (x, y, 1 - c)
        chips = _other_chips(x, y)
        half = pl.ds(pl.multiple_of(c * rh, 16), rh)
        other = pl.ds(pl.multiple_of((1 - c) * rh, 16), rh)
        mine = pltpu.make_async_copy(src, dst.at[me], local_sem)
        mine.start()
        first = [_remote(src.at[half], dst.at[me, half], send_sems, recv_sems, j, (cx, cy, c))
                 for j, (cx, cy) in enumerate(chips)]
        for cp in first:
            cp.start()
        passed = []
        for j, (cx, cy) in enumerate(chips):
            landed = dst.at[2 * cx + cy, half]
            _remote(landed, landed, send_sems, recv_sems, j, sib).wait_recv()
            cp = _remote(landed, landed, send_sems, recv_sems, 3 + j, sib)
            cp.start()
            passed.append(cp)
        for j, (cx, cy) in enumerate(chips):
            landed = dst.at[2 * cx + cy, other]
            _remote(landed, landed, send_sems, recv_sems, 3 + j, sib).wait_recv()
        for cp in first + passed:
            cp.wait_send()
        mine.wait()

    return pl.pallas_call(
        body, name="allgather_weights", in_specs=[ANY], out_specs=ANY,
        out_shape=jax.ShapeDtypeStruct((N_CHIPS, R, W), flat.dtype),
        scratch_shapes=[pltpu.SemaphoreType.DMA((6,)), pltpu.SemaphoreType.DMA((6,)), pltpu.SemaphoreType.DMA(())],
    )(flat)


def _pair_exchange(g):
    n, _, rh, W = g.shape

    def body(g_ref, t_ref, send_sems, recv_sems):
        x, y, c = _place()
        cps = [_remote(g_ref.at[k, 1 - c], t_ref.at[k], send_sems, recv_sems, k, (x, y, 1 - c)) for k in range(n)]
        for cp in cps:
            cp.start()
        for cp in cps:
            cp.wait()

    return pl.pallas_call(
        body, name="grad_pair_exchange", in_specs=[ANY], out_specs=ANY,
        out_shape=jax.ShapeDtypeStruct((n, rh, W), g.dtype),
        scratch_shapes=[pltpu.SemaphoreType.DMA((n,)), pltpu.SemaphoreType.DMA((n,))],
    )(g)


def _pair_add(g, t, c):
    n, _, rh, W = g.shape
    tr = _tile(rh, 256, SUBLANES)

    def body(c_ref, g_ref, t_ref, o_ref):
        o_ref[...] = g_ref[0] + t_ref[...]

    return pl.pallas_call(
        body, name="grad_pair_add",
        grid_spec=pltpu.PrefetchScalarGridSpec(
            num_scalar_prefetch=1, grid=(n, rh // tr),
            in_specs=[pl.BlockSpec((1, 1, tr, W), lambda k, i, c_ref: (k, c_ref[0], i, 0)),
                      pl.BlockSpec((1, tr, W), lambda k, i, c_ref: (k, i, 0))],
            out_specs=pl.BlockSpec((1, tr, W), lambda k, i, c_ref: (k, i, 0))),
        out_shape=jax.ShapeDtypeStruct((n, rh, W), g.dtype),
        compiler_params=_params("parallel", "parallel"),
    )(c.reshape(1).astype(jnp.int32), g, t)


def _chip_exchange(a):
    _, rh, W = a.shape

    def body(a_ref, t_ref, send_sems, recv_sems):
        x, y, c = _place()
        cps = [_remote(a_ref.at[2 * cx + cy], t_ref.at[j], send_sems, recv_sems, j, (cx, cy, c))
               for j, (cx, cy) in enumerate(_other_chips(x, y))]
        for cp in cps:
            cp.start()
        for cp in cps:
            cp.wait()

    return pl.pallas_call(
        body, name="grad_chip_exchange", in_specs=[ANY], out_specs=ANY,
        out_shape=jax.ShapeDtypeStruct((3, rh, W), a.dtype),
        scratch_shapes=[pltpu.SemaphoreType.DMA((3,)), pltpu.SemaphoreType.DMA((3,))],
    )(a)


def _chip_add(a, t, me):
    _, rh, W = a.shape
    tr = _tile(rh, 256, SUBLANES)

    def body(me_ref, a_ref, t_ref, o_ref):
        o_ref[...] = a_ref[0] + t_ref[0] + t_ref[1] + t_ref[2]

    return pl.pallas_call(
        body, name="grad_chip_add",
        grid_spec=pltpu.PrefetchScalarGridSpec(
            num_scalar_prefetch=1, grid=(rh // tr,),
            in_specs=[pl.BlockSpec((1, tr, W), lambda i, me_ref: (me_ref[0], i, 0)),
                      pl.BlockSpec((3, tr, W), lambda i, me_ref: (0, i, 0))],
            out_specs=pl.BlockSpec((tr, W), lambda i, me_ref: (i, 0))),
        out_shape=jax.ShapeDtypeStruct((rh, W), a.dtype),
        compiler_params=_params("parallel"),
    )(me.reshape(1).astype(jnp.int32), a, t)


def _pair_share(h):
    rh, W = h.shape

    def body(h_ref, f_ref, send_sems, recv_sems, local_sem):
        x, y, c = _place()
        loc = pltpu.make_async_copy(h_ref, f_ref.at[c], local_sem)
        loc.start()
        cp = _remote(h_ref, f_ref.at[c], send_sems, recv_sems, 0, (x, y, 1 - c))
        cp.start()
        cp.wait_send()
        _remote(h_ref, f_ref.at[1 - c], send_sems, recv_sems, 0, (x, y, 1 - c)).wait_recv()
        loc.wait()

    return pl.pallas_call(
        body, name="grad_pair_share", in_specs=[ANY], out_specs=ANY,
        out_shape=jax.ShapeDtypeStruct((2, rh, W), h.dtype),
        scratch_shapes=[pltpu.SemaphoreType.DMA((1,)), pltpu.SemaphoreType.DMA((1,)), pltpu.SemaphoreType.DMA(())],
    )(h)


def _allreduce_small(pack, name):
    rows, W = pack.shape

    def body(p_ref, o_ref, buf, send_sems, recv_sems):
        x, y, c = _place()
        me = 4 * x + 2 * y + c
        buf[me] = p_ref[...]
        cps = []
        for r in range(1, 8):
            fx, fy, fc = (r >> 2) & 1, (r >> 1) & 1, r & 1
            to = (1 - x if fx else x, 1 - y if fy else y, 1 - c if fc else c)
            cps.append(_remote(p_ref, buf.at[me], send_sems, recv_sems, r - 1, to))
        for cp in cps:
            cp.start()
        for r in range(1, 8):
            fx, fy, fc = (r >> 2) & 1, (r >> 1) & 1, r & 1
            frm = 4 * (1 - x if fx else x) + 2 * (1 - y if fy else y) + (1 - c if fc else c)
            _remote(p_ref, buf.at[frm], send_sems, recv_sems, r - 1, (x, y, c)).wait_recv()
        for cp in cps:
            cp.wait_send()
        acc = buf[0]
        for i in range(1, 8):
            acc = acc + buf[i]
        o_ref[...] = acc

    return pl.pallas_call(
        body, name=name, in_specs=[VMEM], out_specs=VMEM,
        out_shape=jax.ShapeDtypeStruct((rows, W), F32),
        scratch_shapes=[pltpu.VMEM((8, rows, W), F32), pltpu.SemaphoreType.DMA((7,)), pltpu.SemaphoreType.DMA((7,))],
    )(pack)


def _flatten(arrs, dtype):
    flat = jnp.concatenate([a.reshape(-1).astype(dtype) for a in arrs])
    per = FLAT_W * FLAT_ROW_MULT
    total = -(-flat.shape[0] // per) * per
    return jnp.pad(flat, (0, total - flat.shape[0])).reshape(-1, FLAT_W)


def _unflatten(flat2d, shapes):
    out, off = [], 0
    for shp in shapes:
        n = 1
        for d in shp:
            n *= d
        out.append(flat2d[..., off:off + n].reshape(flat2d.shape[:-1] + tuple(shp)))
        off += n
    return out


def _join_cols(g):
    nd = g.ndim
    return jnp.moveaxis(g, 0, nd - 2).reshape(g.shape[1:-1] + (N_CHIPS * g.shape[-1],))


def _split_cols(full):
    c = full.shape[-1] // N_CHIPS
    return jnp.moveaxis(full.reshape(full.shape[:-1] + (N_CHIPS, c)), -2, 0)


def _join_rows(g):
    return jnp.moveaxis(g, 0, 1).reshape(g.shape[1], N_CHIPS * g.shape[2], g.shape[3])


def _split_rows(full):
    L, r4, D = full.shape
    return jnp.moveaxis(full.reshape(L, N_CHIPS, r4 // N_CHIPS, D), 1, 0)


def _row_layout(a, tq):
    P, S, _ = a.shape
    return a.reshape(P, S // tq, tq, 2).transpose(0, 1, 3, 2)


def _pad_row(v, width=FLAT_W):
    flat = v.reshape(-1)
    rows = -(-flat.shape[0] // width)
    return jnp.pad(flat, (0, rows * width - flat.shape[0]))


def kernel(x, attn_norm, ffn_norm, a_w_in, a_conv, a_w_out, kv_norm, w_kvf, b_f, k_norm, b_w_qg, q_norm, b_w_out, ffn_w_up, ffn_conv, ffn_w_down, loss_target, m_attn_norm, m_ffn_norm, m_a_w_in, m_a_conv, m_a_w_out, m_kv_norm, m_w_kvf, m_b_f, m_k_norm, m_b_w_qg, m_q_norm, m_b_w_out, m_ffn_w_up, m_ffn_conv, m_ffn_w_down, v_attn_norm, v_ffn_norm, v_a_w_in, v_a_conv, v_a_w_out, v_kv_norm, v_w_kvf, v_b_f, v_k_norm, v_b_w_qg, v_q_norm, v_b_w_out, v_ffn_w_up, v_ffn_conv, v_ffn_w_down):
    xs = x[0]
    S, D = xs.shape
    H, hd = b_f.shape[0], k_norm.shape[0]
    depth = attn_norm.shape[0]
    n_a = a_w_in.shape[0]
    P = D // LANES
    assert LANES == 2 * hd and H * hd == D, "the attention kernels hold two heads per lane tile"
    mx, my, mc = _place()
    chip = 2 * mx + my

    big = [a_w_in, a_w_out, w_kvf, b_w_qg, b_w_out, ffn_w_up, ffn_w_down]
    big_shapes = [w.shape for w in big]
    gathered = _allgather_weights(_flatten(big, BF))
    g_in, g_out, g_kvf, g_qg, g_bout, g_up, g_down = _unflatten(gathered.reshape(N_CHIPS, -1), big_shapes)
    wa_in, wb_qg, w_up = _join_cols(g_in), _join_cols(g_qg), _join_cols(g_up)
    wa_out, wb_out, w_down = _join_rows(g_out), _join_rows(g_bout), _join_rows(g_down)
    kvf_cols = 2 * D + LANES
    wkvf = jnp.pad(_join_cols(g_kvf), ((0, 0), (0, kvf_cols - (2 * D + H))))

    def placed(shard):
        full = jnp.zeros(shard.shape[:-1] + (N_CHIPS, shard.shape[-1]), F32)
        full = lax.dynamic_update_slice_in_dim(full, shard[..., None, :], chip, axis=full.ndim - 2)
        return jnp.where(mc == 0, full, 0.0).reshape(-1)

    conv_pack = jnp.concatenate([_pad_row(placed(a_conv)), _pad_row(placed(ffn_conv))]).reshape(-1, FLAT_W)
    conv_full = _allreduce_small(conv_pack, "allgather_conv_taps").reshape(-1)
    n_ac = a_conv.size * N_CHIPS
    a_conv_f = conv_full[:n_ac].reshape(a_conv.shape[:-1] + (-1,))
    off = _pad_row(placed(a_conv)).shape[0]
    ffn_conv_f = conv_full[off:off + ffn_conv.size * N_CHIPS].reshape(ffn_conv.shape[:-1] + (-1,))
    F = ffn_conv_f.shape[-1]

    b_pad = jnp.pad(b_f, (0, LANES - H)).reshape(1, LANES)
    gate_blk = 2 * D // LANES
    tq = _attn_tile(S)
    scale = hd ** -0.5

    saved = []
    cur = xs
    kv = None
    for l in range(depth):
        rec = {"x_in": cur}
        if l < n_a:
            proj, xn = _norm_matmul(cur, attn_norm[l], wa_in[l], 3, BF, f"a_in_{l}")
            z = _mixer_mid_fwd(proj, a_conv_f[l], f"a_mid_{l}")
            mid = _matmul_residual(z, wa_out[l], cur, f"a_out_{l}")
            rec.update(proj=proj, xn=xn, z=z)
        else:
            j = l - n_a
            if kv is None:
                kvf, hn = _norm_matmul(cur, kv_norm, wkvf, 1, F32, "kvf_proj")
                kn = _headnorm(kvf, 0, 0, k_norm, 1.0, D, "k_norm")
                vb = kvf[0, :, D:2 * D].astype(BF)
                cgate = _gate_fwd(kvf, b_pad, gate_blk, "gate_cumsum")
                ch = cgate[:, :H]
                ccol = ch.reshape(S, P, 2).transpose(1, 0, 2)
                crow = _row_layout(ccol, tq)
                kv = dict(kvf=kvf, hn=hn, kn=kn, vb=vb, ccol=ccol, crow=crow, x_in=cur, dk=[], dv=[], dc=[])
            qg, xn = _norm_matmul(cur, attn_norm[l], wb_qg[j], 2, F32, f"qg_proj_{j}")
            qn = _headnorm(qg, 0, 0, q_norm[j], scale, D, f"q_norm_{j}")
            o, og, lse = _attn_fwd(qn, kv["kn"], kv["vb"], qg, kv["ccol"], kv["crow"], hd, f"attn_fwd_{j}")
            mid = _matmul_residual(og, wb_out[j], cur, f"b_out_{j}")
            rec.update(qg=qg, xn=xn, qn=qn, o=o, og=og, lse=lse)
        up, xn2 = _norm_matmul(mid, ffn_norm[l], w_up[l], 2, BF, f"ffn_up_{l}")
        z2 = _ffn_mid_fwd(up, ffn_conv_f[l], f"ffn_mid_{l}")
        cur = _matmul_residual(z2, w_down[l], mid, f"ffn_down_{l}")
        rec.update(x_mid=mid, up=up, xn2=xn2, z2=z2)
        saved.append(rec)

    dy, loss_part = _loss_head(cur, loss_target[0], "loss_head")

    g_attn_norm, g_ffn_norm = [None] * depth, [None] * depth
    g_a_in, g_a_conv, g_a_out = [None] * n_a, [None] * n_a, [None] * n_a
    g_qg, g_qn, g_bo = [None] * (depth - n_a), [None] * (depth - n_a), [None] * (depth - n_a)
    g_up, g_fc, g_down = [None] * depth, [None] * depth, [None] * depth
    for l in reversed(range(depth)):
        rec = saved[l]
        dz = _matmul_nt(dy, w_down[l], f"ffn_down_bwd_{l}")
        dup, g_fc[l] = _ffn_mid_bwd(rec["up"], dz, ffn_conv_f[l], f"ffn_mid_bwd_{l}")
        g_down[l] = _wgrad(rec["z2"], dy[None], f"ffn_down_wgrad_{l}")
        g_up[l] = _wgrad(rec["xn2"], dup, f"ffn_up_wgrad_{l}")
        dy, g_ffn_norm[l] = _dnorm(dup, w_up[l], rec["x_mid"], ffn_norm[l], dy, f"ffn_up_bwd_{l}")
        if l < n_a:
            dz = _matmul_nt(dy, wa_out[l], f"a_out_bwd_{l}")
            dproj, g_a_conv[l] = _mixer_mid_bwd(rec["proj"], dz, a_conv_f[l], f"a_mid_bwd_{l}")
            g_a_out[l] = _wgrad(rec["z"], dy[None], f"a_out_wgrad_{l}")
            g_a_in[l] = _wgrad(rec["xn"], dproj, f"a_in_wgrad_{l}")
            dy, g_attn_norm[l] = _dnorm(dproj, wa_in[l], rec["x_in"], attn_norm[l], dy, f"a_in_bwd_{l}")
        else:
            j = l - n_a
            dog = _matmul_nt(dy, wb_out[j], f"b_out_bwd_{j}")
            do, dgate, dvec = _attn_out_bwd(dog, rec["o"], rec["qg"], hd, f"attn_gate_bwd_{j}")
            g_bo[j] = _wgrad(rec["og"], dy[None], f"b_out_wgrad_{j}")
            dqn, drow = _attn_dq(rec["qn"], kv["kn"], kv["vb"], do, kv["ccol"], kv["crow"], rec["lse"], dvec, hd,
                                 f"attn_dq_{j}")
            dk, dv, dc = _attn_dkv(rec["qn"], kv["kn"], kv["vb"], do, kv["ccol"], kv["crow"],
                                   _row_layout(rec["lse"], tq), _row_layout(drow, tq), hd, f"attn_dkv_{j}")
            kv["dk"].append(dk)
            kv["dv"].append(dv)
            kv["dc"].append(dc)
            dq_pre, g_qn[j] = _headnorm_bwd(rec["qg"], 0, 0, q_norm[j], [dqn], D, f"q_norm_bwd_{j}")
            dqg = jnp.stack([dq_pre, dgate])
            g_qg[j] = _wgrad(rec["xn"], dqg, f"qg_wgrad_{j}")
            dy, g_attn_norm[l] = _dnorm(dqg, wb_qg[j], rec["x_in"], attn_norm[l], dy, f"qg_bwd_{j}")
            if l == n_a:
                dk_s, g_k_norm = _headnorm_bwd(kv["kvf"], 0, 0, k_norm, kv["dk"], D, "k_norm_bwd")
                dv_s = functools.reduce(jnp.add, kv["dv"]).astype(BF)
                dc_sum = functools.reduce(jnp.add, kv["dc"])
                dc_pad = jnp.pad(dc_sum.transpose(1, 0, 2).reshape(S, H), ((0, 0), (0, LANES - H)))
                df, db = _gate_bwd(dc_pad, kv["kvf"], b_pad, gate_blk, "gate_bwd")
                dkvf = jnp.concatenate([dk_s, dv_s, df.astype(BF)], axis=1)[None]
                g_kvf = _wgrad(kv["hn"], dkvf, "kvf_wgrad")[:, :2 * D + H]
                g_b_f = db[:H]
                dy, g_kv_norm = _dnorm(dkvf, wkvf, kv["x_in"], kv_norm, dy, "kvf_bwd")
    grad_x = dy[None]

    full_grads = [_split_cols(jnp.stack(g_a_in)), _split_rows(jnp.stack(g_a_out)), _split_cols(g_kvf),
                  _split_cols(jnp.stack(g_qg)), _split_rows(jnp.stack(g_bo)), _split_cols(jnp.stack(g_up)),
                  _split_rows(jnp.stack(g_down))]
    gflat = jnp.stack([_flatten([g[k] for g in full_grads], F32) for k in range(N_CHIPS)])
    R = gflat.shape[1]
    g4 = gflat.reshape(N_CHIPS, 2, R // 2, FLAT_W)
    pair_sum = _pair_add(g4, _pair_exchange(g4), mc)
    mine = _chip_add(pair_sum, _chip_exchange(pair_sum), chip)
    shard_flat = _pair_share(mine).reshape(-1)
    big_grads = _unflatten(shard_flat, big_shapes)

    small = [loss_part[0, :1], jnp.stack(g_attn_norm), jnp.stack(g_ffn_norm), g_kv_norm, g_b_f, g_k_norm,
             jnp.stack(g_qn), jnp.stack(g_a_conv), jnp.stack(g_fc)]
    small_sum = _allreduce_small(jnp.concatenate([_pad_row(s) for s in small]).reshape(-1, FLAT_W),
                                 "allreduce_small_grads").reshape(-1)
    parts, off = [], 0
    for s in small:
        parts.append(small_sum[off:off + s.size].reshape(s.shape))
        off += _pad_row(s).shape[0]
    loss = parts[0][0]
    gr_attn_norm, gr_ffn_norm, gr_kv_norm, gr_b_f, gr_k_norm, gr_q_norm, gr_a_conv_full, gr_ffn_conv_full = parts[1:]

    def my_cols(full):
        c = full.shape[-1] // N_CHIPS
        return lax.dynamic_slice_in_dim(full, chip * c, c, axis=full.ndim - 1)

    gr_a_in, gr_a_out, gr_kvf, gr_qg, gr_bo, gr_up, gr_down = big_grads
    grads = [gr_attn_norm, gr_ffn_norm, gr_a_in, my_cols(gr_a_conv_full), gr_a_out, gr_kv_norm, gr_kvf, gr_b_f,
             gr_k_norm, gr_qg, gr_q_norm, gr_bo, gr_up, my_cols(gr_ffn_conv_full), gr_down]
    weights = [attn_norm, ffn_norm, a_w_in, a_conv, a_w_out, kv_norm, w_kvf, b_f, k_norm, b_w_qg, q_norm, b_w_out,
               ffn_w_up, ffn_conv, ffn_w_down]
    ms = [m_attn_norm, m_ffn_norm, m_a_w_in, m_a_conv, m_a_w_out, m_kv_norm, m_w_kvf, m_b_f, m_k_norm, m_b_w_qg,
          m_q_norm, m_b_w_out, m_ffn_w_up, m_ffn_conv, m_ffn_w_down]
    vs = [v_attn_norm, v_ffn_norm, v_a_w_in, v_a_conv, v_a_w_out, v_kv_norm, v_w_kvf, v_b_f, v_k_norm, v_b_w_qg,
          v_q_norm, v_b_w_out, v_ffn_w_up, v_ffn_conv, v_ffn_w_down]
    deltas, new_ms, new_vs = [], [], []
    for i, (w, g, m, v) in enumerate(zip(weights, grads, ms, vs)):
        d, nm, nv = _adamw(w, g, m, v, f"adamw_{i}")
        deltas.append(d)
        new_ms.append(nm)
        new_vs.append(nv)
    return (loss, grad_x, *grads, *deltas, *new_ms, *new_vs)
```

```python
import functools

import jax
import jax.numpy as jnp
from jax import lax
from jax.experimental import pallas as pl
from jax.experimental.pallas import tpu as pltpu

F32 = jnp.float32
BF = jnp.bfloat16
LANES = 128
SUBLANES = 8
RMS_EPS = 1e-6
LOG2E = 1.4426950408889634
FLAT_W = 1024
FLAT_ROW_MULT = 512
N_CHIPS = 4
CONV_W = 3
HALO = SUBLANES

ADAM_LR = 0.001
ADAM_B1 = 0.9
ADAM_B2 = 0.999
ADAM_EPS = 1e-08
ADAM_WD = 0.01
ADAM_STEP = 10

MESH = pl.DeviceIdType.MESH
ANY = pl.BlockSpec(memory_space=pl.ANY)
VMEM = pl.BlockSpec(memory_space=pltpu.VMEM)
NT_DIMS = (((1,), (1,)), ((), ()))
TN_DIMS = (((0,), (0,)), ((), ()))


def _tile(n, pref, mult=LANES):
    t = (min(pref, n) // mult) * mult
    while t >= mult:
        if n % t == 0:
            break
        t -= mult
    if t < mult or (t * 4 < pref and n <= 4 * pref):
        return n
    return t


def _params(*sem):
    return pltpu.CompilerParams(dimension_semantics=sem)


def _norm_matmul(x, g, w, parts, out_dtype, name):
    S, D = x.shape
    C = w.shape[1] // parts
    ts, tn = _tile(S, 512, SUBLANES), _tile(C, 1408)
    npc = C // tn

    def body(x_ref, g_ref, w_ref, o_ref, xn_ref):
        @pl.when(pl.program_id(1) == 0)
        def _():
            xf = x_ref[...]
            r = lax.rsqrt(jnp.mean(xf * xf, axis=-1, keepdims=True) + RMS_EPS)
            xn_ref[...] = (xf * r * g_ref[...]).astype(BF)

        o_ref[0] = jnp.dot(xn_ref[...], w_ref[...], preferred_element_type=F32).astype(out_dtype)

    return pl.pallas_call(
        body, name=name, grid=(S // ts, parts * npc),
        in_specs=[pl.BlockSpec((ts, D), lambda s, n: (s, 0)),
                  pl.BlockSpec((1, D), lambda s, n: (0, 0)),
                  pl.BlockSpec((D, tn), lambda s, n: (0, n))],
        out_specs=[pl.BlockSpec((1, ts, tn), lambda s, n: (n // npc, s, n % npc)),
                   pl.BlockSpec((ts, D), lambda s, n: (s, 0))],
        out_shape=[jax.ShapeDtypeStruct((parts, S, C), out_dtype), jax.ShapeDtypeStruct((S, D), BF)],
        compiler_params=_params("parallel", "arbitrary"),
    )(x, g.reshape(1, D), w)


def _shift_down(u, prev, k, row):
    r = pltpu.roll(u, k, 0)
    for j in range(k):
        r = jnp.where(row == j, prev[HALO - k + j:HALO - k + j + 1, :], r)
    return r


def _shift_up(d, nxt, k, row):
    n = d.shape[0]
    r = pltpu.roll(d, n - k, 0)
    for j in range(k):
        r = jnp.where(row == n - k + j, nxt[j:j + 1, :], r)
    return r


def _conv3(u, prev, w, row):
    return _shift_down(u, prev, 2, row) * w[0:1] + _shift_down(u, prev, 1, row) * w[1:2] + u * w[2:3]


def _conv3_t(d, nxt, w, row):
    return d * w[2:3] + _shift_up(d, nxt, 1, row) * w[1:2] + _shift_up(d, nxt, 2, row) * w[0:1]


def _tap_rows(t0, t1, t2):
    row = lax.broadcasted_iota(jnp.int32, (SUBLANES, t0.shape[1]), 0)
    return jnp.where(row == 0, t0, jnp.where(row == 1, t1, jnp.where(row == 2, t2, 0.0)))


def _pad_conv(cw):
    return jnp.pad(cw, ((0, SUBLANES - CONV_W), (0, 0)))


def _mixer_mid_fwd(proj, cw, name):
    _, S, C = proj.shape
    ts, tc = _tile(S, 512, SUBLANES), _tile(C, 1024)

    def body(b_ref, c_ref, h_ref, cw_ref, z_ref, carry):
        @pl.when(pl.program_id(1) == 0)
        def _():
            carry[...] = jnp.zeros_like(carry)

        u = c_ref[0].astype(F32) * h_ref[0].astype(F32)
        row = lax.broadcasted_iota(jnp.int32, u.shape, 0)
        cv = _conv3(u, carry[...], cw_ref[...], row)
        z_ref[...] = (b_ref[0].astype(F32) * cv).astype(BF)
        carry[...] = u[ts - HALO:ts, :]

    part = lambda p: pl.BlockSpec((1, ts, tc), lambda c, s: (p, s, c))
    return pl.pallas_call(
        body, name=name, grid=(C // tc, S // ts),
        in_specs=[part(0), part(1), part(2), pl.BlockSpec((SUBLANES, tc), lambda c, s: (0, c))],
        out_specs=pl.BlockSpec((ts, tc), lambda c, s: (s, c)),
        out_shape=jax.ShapeDtypeStruct((S, C), BF),
        scratch_shapes=[pltpu.VMEM((HALO, tc), F32)],
        compiler_params=_params("parallel", "arbitrary"),
    )(proj, proj, proj, _pad_conv(cw))


def _ffn_mid_fwd(up, cw, name):
    _, S, C = up.shape
    ts, tc = _tile(S, 512, SUBLANES), _tile(C, 1408)

    def body(a_ref, g_ref, cw_ref, z_ref, carry):
        @pl.when(pl.program_id(1) == 0)
        def _():
            carry[...] = jnp.zeros_like(carry)

        a_pre = a_ref[0].astype(F32)
        row = lax.broadcasted_iota(jnp.int32, a_pre.shape, 0)
        a = _conv3(a_pre, carry[...], cw_ref[...], row)
        z_ref[...] = (a * jax.nn.sigmoid(a) * g_ref[0].astype(F32)).astype(BF)
        carry[...] = a_pre[ts - HALO:ts, :]

    part = lambda p: pl.BlockSpec((1, ts, tc), lambda c, s: (p, s, c))
    return pl.pallas_call(
        body, name=name, grid=(C // tc, S // ts),
        in_specs=[part(0), part(1), pl.BlockSpec((SUBLANES, tc), lambda c, s: (0, c))],
        out_specs=pl.BlockSpec((ts, tc), lambda c, s: (s, c)),
        out_shape=jax.ShapeDtypeStruct((S, C), BF),
        scratch_shapes=[pltpu.VMEM((HALO, tc), F32)],
        compiler_params=_params("parallel", "arbitrary"),
    )(up, up, _pad_conv(cw))


def _halo_specs(p, ts, tc, n_s):
    per = ts // HALO
    last = n_s * per - 1
    before = pl.BlockSpec((1, HALO, tc), lambda c, s: (p, jnp.maximum(s * per - 1, 0), c))
    after = pl.BlockSpec((1, HALO, tc), lambda c, s: (p, jnp.minimum((s + 1) * per, last), c))
    return before, after


def _mixer_mid_bwd(proj, dz, cw, name):
    _, S, C = proj.shape
    ts, tc = _tile(S, 512, SUBLANES), _tile(C, 1024)
    n_s = S // ts

    def body(b_ref, c_ref, h_ref, dz_ref, cp_ref, hp_ref, bn_ref, dzn_ref, cw_ref, d_ref, dcw_ref):
        s = pl.program_id(1)
        w = cw_ref[...]
        b, c, h = b_ref[0].astype(F32), c_ref[0].astype(F32), h_ref[0].astype(F32)
        dz_t = dz_ref[0]
        row = lax.broadcasted_iota(jnp.int32, b.shape, 0)
        u = c * h
        prev = jnp.where(s > 0, cp_ref[0].astype(F32) * hp_ref[0].astype(F32), 0.0)
        u1, u2 = _shift_down(u, prev, 1, row), _shift_down(u, prev, 2, row)
        cv = u2 * w[0:1] + u1 * w[1:2] + u * w[2:3]
        dcv = dz_t * b
        nxt = jnp.where(s < n_s - 1, dzn_ref[0] * bn_ref[0].astype(F32), 0.0)
        du = _conv3_t(dcv, nxt, w, row)
        d_ref[0] = (dz_t * cv).astype(BF)
        d_ref[1] = (du * h).astype(BF)
        d_ref[2] = (du * c).astype(BF)
        part = _tap_rows(jnp.sum(dcv * u2, axis=0, keepdims=True), jnp.sum(dcv * u1, axis=0, keepdims=True),
                         jnp.sum(dcv * u, axis=0, keepdims=True))

        @pl.when(s == 0)
        def _():
            dcw_ref[...] = part

        @pl.when(s > 0)
        def _():
            dcw_ref[...] += part

    part_spec = lambda p: pl.BlockSpec((1, ts, tc), lambda c, s: (p, s, c))
    c_before, _ = _halo_specs(1, ts, tc, n_s)
    h_before, _ = _halo_specs(2, ts, tc, n_s)
    _, b_after = _halo_specs(0, ts, tc, n_s)
    _, dz_after = _halo_specs(0, ts, tc, n_s)
    dproj, dcw = pl.pallas_call(
        body, name=name, grid=(C // tc, n_s),
        in_specs=[part_spec(0), part_spec(1), part_spec(2), part_spec(0), c_before, h_before, b_after, dz_after,
                  pl.BlockSpec((SUBLANES, tc), lambda c, s: (0, c))],
        out_specs=[pl.BlockSpec((3, ts, tc), lambda c, s: (0, s, c)),
                   pl.BlockSpec((SUBLANES, tc), lambda c, s: (0, c))],
        out_shape=[jax.ShapeDtypeStruct((3, S, C), BF), jax.ShapeDtypeStruct((SUBLANES, C), F32)],
        compiler_params=_params("parallel", "arbitrary"),
    )(proj, proj, proj, dz[None], proj, proj, proj, dz[None], _pad_conv(cw))
    return dproj, dcw[:CONV_W]


def _ffn_mid_bwd(up, dz, cw, name):
    _, S, C = up.shape
    ts, tc = _tile(S, 512, SUBLANES), _tile(C, 1408)
    n_s = S // ts

    def dact(a, g, dz_t):
        sg = jax.nn.sigmoid(a)
        return dz_t * g * (sg * (1.0 + a * (1.0 - sg))), dz_t * (a * sg)

    def body(a_ref, g_ref, dz_ref, ap_ref, an_ref, gn_ref, dzn_ref, cw_ref, d_ref, dcw_ref):
        s = pl.program_id(1)
        w = cw_ref[...]
        a_pre, g = a_ref[0].astype(F32), g_ref[0].astype(F32)
        row = lax.broadcasted_iota(jnp.int32, a_pre.shape, 0)
        prev = jnp.where(s > 0, ap_ref[0].astype(F32), 0.0)
        a1, a2 = _shift_down(a_pre, prev, 1, row), _shift_down(a_pre, prev, 2, row)
        a = a2 * w[0:1] + a1 * w[1:2] + a_pre * w[2:3]
        da, dg = dact(a, g, dz_ref[0])
        an_pre = an_ref[0].astype(F32)
        row8 = lax.broadcasted_iota(jnp.int32, an_pre.shape, 0)
        a_next = _conv3(an_pre, a_pre[ts - HALO:ts, :], w, row8)
        da_next, _ = dact(a_next, gn_ref[0].astype(F32), dzn_ref[0])
        nxt = jnp.where(s < n_s - 1, da_next, 0.0)
        d_ref[0] = _conv3_t(da, nxt, w, row).astype(BF)
        d_ref[1] = dg.astype(BF)
        part = _tap_rows(jnp.sum(da * a2, axis=0, keepdims=True), jnp.sum(da * a1, axis=0, keepdims=True),
                         jnp.sum(da * a_pre, axis=0, keepdims=True))

        @pl.when(s == 0)
        def _():
            dcw_ref[...] = part

        @pl.when(s > 0)
        def _():
            dcw_ref[...] += part

    part_spec = lambda p: pl.BlockSpec((1, ts, tc), lambda c, s: (p, s, c))
    a_before, a_after = _halo_specs(0, ts, tc, n_s)
    _, g_after = _halo_specs(1, ts, tc, n_s)
    _, dz_after = _halo_specs(0, ts, tc, n_s)
    dup, dcw = pl.pallas_call(
        body, name=name, grid=(C // tc, n_s),
        in_specs=[part_spec(0), part_spec(1), part_spec(0), a_before, a_after, g_after, dz_after,
                  pl.BlockSpec((SUBLANES, tc), lambda c, s: (0, c))],
        out_specs=[pl.BlockSpec((2, ts, tc), lambda c, s: (0, s, c)),
                   pl.BlockSpec((SUBLANES, tc), lambda c, s: (0, c))],
        out_shape=[jax.ShapeDtypeStruct((2, S, C), BF), jax.ShapeDtypeStruct((SUBLANES, C), F32)],
        compiler_params=_params("parallel", "arbitrary"),
    )(up, up, dz[None], up, up, up, dz[None], _pad_conv(cw))
    return dup, dcw[:CONV_W]


def _matmul_residual(z, w, x, name):
    S, K = z.shape
    D = w.shape[1]
    ts = _tile(S, 512, SUBLANES)

    def body(z_ref, w_ref, x_ref, o_ref):
        o_ref[...] = x_ref[...] + jnp.dot(z_ref[...], w_ref[...], preferred_element_type=F32)

    return pl.pallas_call(
        body, name=name, grid=(S // ts,),
        in_specs=[pl.BlockSpec((ts, K), lambda s: (s, 0)), pl.BlockSpec((K, D), lambda s: (0, 0)),
                  pl.BlockSpec((ts, D), lambda s: (s, 0))],
        out_specs=pl.BlockSpec((ts, D), lambda s: (s, 0)),
        out_shape=jax.ShapeDtypeStruct((S, D), F32),
        compiler_params=_params("parallel"),
    )(z, w, x)


def _matmul_nt(a, w, name):
    S, K = a.shape
    N = w.shape[0]
    ts, tn = _tile(S, 512, SUBLANES), _tile(N, 1408)

    def body(a_ref, w_ref, o_ref, abf):
        @pl.when(pl.program_id(1) == 0)
        def _():
            abf[...] = a_ref[...].astype(BF)

        o_ref[...] = lax.dot_general(abf[...], w_ref[...], NT_DIMS, preferred_element_type=F32)

    return pl.pallas_call(
        body, name=name, grid=(S // ts, N // tn),
        in_specs=[pl.BlockSpec((ts, K), lambda s, n: (s, 0)), pl.BlockSpec((tn, K), lambda s, n: (n, 0))],
        out_specs=pl.BlockSpec((ts, tn), lambda s, n: (s, n)),
        out_shape=jax.ShapeDtypeStruct((S, N), F32),
        scratch_shapes=[pltpu.VMEM((ts, K), BF)],
        compiler_params=_params("parallel", "arbitrary"),
    )(a, w)


def _wgrad(a, b, name):
    S, M = a.shape
    P, _, C = b.shape
    tm, tn, tk = _tile(M, 1408), _tile(C, 1408), _tile(S, 512, SUBLANES)
    nnc = C // tn

    def body(a_ref, b_ref, o_ref):
        @pl.when(pl.program_id(2) == 0)
        def _():
            o_ref[...] = jnp.zeros_like(o_ref)

        o_ref[...] += lax.dot_general(a_ref[...], b_ref[0].astype(BF), TN_DIMS, preferred_element_type=F32)

    return pl.pallas_call(
        body, name=name, grid=(M // tm, P * nnc, S // tk),
        in_specs=[pl.BlockSpec((tk, tm), lambda m, n, k: (k, m)),
                  pl.BlockSpec((1, tk, tn), lambda m, n, k: (n // nnc, k, n % nnc))],
        out_specs=pl.BlockSpec((tm, tn), lambda m, n, k: (m, n)),
        out_shape=jax.ShapeDtypeStruct((M, P * C), F32),
        compiler_params=_params("parallel", "parallel", "arbitrary"),
    )(a, b)


def _dnorm(dp, w, x, g, dy, name):
    P, S, C = dp.shape
    D = x.shape[1]
    ts, tk = _tile(S, 512, SUBLANES), _tile(C, 1408)
    nkc = C // tk
    n_k = P * nkc

    def body(dp_ref, w_ref, x_ref, g_ref, dy_ref, dx_ref, dg_ref, acc):
        s, k = pl.program_id(0), pl.program_id(1)

        @pl.when(k == 0)
        def _():
            acc[...] = jnp.zeros_like(acc)

        acc[...] += lax.dot_general(dp_ref[0], w_ref[...], NT_DIMS, preferred_element_type=F32)

        @pl.when((s == 0) & (k == 0))
        def _():
            dg_ref[...] = jnp.zeros_like(dg_ref)

        @pl.when(k == n_k - 1)
        def _():
            xf = x_ref[...]
            r = lax.rsqrt(jnp.mean(xf * xf, axis=-1, keepdims=True) + RMS_EPS)
            xhat = xf * r
            dxn = acc[...]
            dxhat = dxn * g_ref[...]
            dx_ref[...] = dy_ref[...] + r * (dxhat - xhat * jnp.mean(dxhat * xhat, axis=-1, keepdims=True))
            dg_ref[...] += jnp.broadcast_to(jnp.sum(dxn * xhat, axis=0, keepdims=True), dg_ref.shape)

    dx, dg = pl.pallas_call(
        body, name=name, grid=(S // ts, n_k),
        in_specs=[pl.BlockSpec((1, ts, tk), lambda s, k: (k // nkc, s, k % nkc)),
                  pl.BlockSpec((D, tk), lambda s, k: (0, k)),
                  pl.BlockSpec((ts, D), lambda s, k: (s, 0)),
                  pl.BlockSpec((1, D), lambda s, k: (0, 0)),
                  pl.BlockSpec((ts, D), lambda s, k: (s, 0))],
        out_specs=[pl.BlockSpec((ts, D), lambda s, k: (s, 0)),
                   pl.BlockSpec((SUBLANES, D), lambda s, k: (0, 0))],
        out_shape=[jax.ShapeDtypeStruct((S, D), F32), jax.ShapeDtypeStruct((SUBLANES, D), F32)],
        scratch_shapes=[pltpu.VMEM((ts, D), F32)],
        compiler_params=_params("arbitrary", "arbitrary"),
    )(dp, w, x, g.reshape(1, D), dy)
    return dx, dg[0]


def _head_masks(shape, hd):
    lane = lax.broadcasted_iota(jnp.int32, shape, 1)
    return lane < hd


def _pair_sum(v, lo):
    s0 = jnp.sum(jnp.where(lo, v, 0.0), axis=-1, keepdims=True)
    s1 = jnp.sum(jnp.where(lo, 0.0, v), axis=-1, keepdims=True)
    return jnp.where(lo, s0, s1)


def _headnorm(src, part, colblk, w, scale, D, name):
    S = src.shape[1]
    hd = w.shape[0]
    ts = _tile(S, 512, SUBLANES)
    w2 = jnp.tile(w, LANES // hd).reshape(1, LANES)

    def body(x_ref, w_ref, o_ref):
        lo = _head_masks((ts, LANES), hd)
        for t in range(D // LANES):
            xt = x_ref[0, :, t * LANES:(t + 1) * LANES]
            r = lax.rsqrt(_pair_sum(xt * xt, lo) * (1.0 / hd) + RMS_EPS)
            o_ref[:, t * LANES:(t + 1) * LANES] = (xt * r * w_ref[...] * scale).astype(BF)

    return pl.pallas_call(
        body, name=name, grid=(S // ts,),
        in_specs=[pl.BlockSpec((1, ts, D), lambda s: (part, s, colblk)), pl.BlockSpec((1, LANES), lambda s: (0, 0))],
        out_specs=pl.BlockSpec((ts, D), lambda s: (s, 0)),
        out_shape=jax.ShapeDtypeStruct((S, D), BF),
        compiler_params=_params("parallel"),
    )(src, w2)


def _headnorm_bwd(src, part, colblk, w, dys, D, name):
    S = src.shape[1]
    hd = w.shape[0]
    ts = _tile(S, 512, SUBLANES)
    w2 = jnp.tile(w, LANES // hd).reshape(1, LANES)
    n_dy = len(dys)

    def body(x_ref, w_ref, *rest):
        dy_refs, dx_ref, dw_ref = rest[:n_dy], rest[n_dy], rest[n_dy + 1]

        @pl.when(pl.program_id(0) == 0)
        def _():
            dw_ref[...] = jnp.zeros_like(dw_ref)

        lo = _head_masks((ts, LANES), hd)
        for t in range(D // LANES):
            cols = slice(t * LANES, (t + 1) * LANES)
            xt = x_ref[0, :, cols]
            dy = dy_refs[0][:, cols]
            for other in dy_refs[1:]:
                dy = dy + other[:, cols]
            r = lax.rsqrt(_pair_sum(xt * xt, lo) * (1.0 / hd) + RMS_EPS)
            xhat = xt * r
            dxhat = dy * w_ref[...]
            mean = _pair_sum(dxhat * xhat, lo) * (1.0 / hd)
            dx_ref[:, cols] = (r * (dxhat - xhat * mean)).astype(BF)
            dw_ref[:, cols] += jnp.broadcast_to(jnp.sum(dy * xhat, axis=0, keepdims=True), (SUBLANES, LANES))

    dx, dw = pl.pallas_call(
        body, name=name, grid=(S // ts,),
        in_specs=[pl.BlockSpec((1, ts, D), lambda s: (part, s, colblk)), pl.BlockSpec((1, LANES), lambda s: (0, 0))]
        + [pl.BlockSpec((ts, D), lambda s: (s, 0))] * n_dy,
        out_specs=[pl.BlockSpec((ts, D), lambda s: (s, 0)), pl.BlockSpec((SUBLANES, D), lambda s: (0, 0))],
        out_shape=[jax.ShapeDtypeStruct((S, D), BF), jax.ShapeDtypeStruct((SUBLANES, D), F32)],
        compiler_params=_params("arbitrary"),
    )(src, w2, *dys)
    return dx, jnp.sum(dw[0].reshape(D // hd, hd), axis=0)


def _tri(n, lower):
    r, c = lax.broadcasted_iota(jnp.int32, (n, n), 0), lax.broadcasted_iota(jnp.int32, (n, n), 1)
    return jnp.where((c <= r) if lower else (c >= r), 1.0, 0.0).astype(BF)


def _dot_exact(t, v):
    hi = v.astype(BF)
    r1 = v - hi.astype(F32)
    mid = r1.astype(BF)
    lo = (r1 - mid.astype(F32)).astype(BF)
    dot = lambda u: jnp.dot(t, u, preferred_element_type=F32)
    return dot(hi) + dot(mid) + dot(lo)


def _gate_fwd(kvf, b_pad, colblk, name):
    S = kvf.shape[1]
    ts = _tile(S, 512, SUBLANES)

    def body(f_ref, b_ref, c_ref, carry):
        @pl.when(pl.program_id(0) == 0)
        def _():
            carry[...] = jnp.zeros_like(carry)

        f = f_ref[0] + b_ref[...]
        ls = jnp.minimum(f, 0.0) - jnp.log1p(jnp.exp(-jnp.abs(f)))
        tri = _tri(ts, lower=True)
        c = _dot_exact(tri, ls) + carry[0:1, :]
        c_ref[...] = c
        carry[...] = jnp.broadcast_to(c[ts - 1:ts, :], carry.shape)

    return pl.pallas_call(
        body, name=name, grid=(S // ts,),
        in_specs=[pl.BlockSpec((1, ts, LANES), lambda s: (0, s, colblk)), pl.BlockSpec((1, LANES), lambda s: (0, 0))],
        out_specs=pl.BlockSpec((ts, LANES), lambda s: (s, 0)),
        out_shape=jax.ShapeDtypeStruct((S, LANES), F32),
        scratch_shapes=[pltpu.VMEM((SUBLANES, LANES), F32)],
        compiler_params=_params("arbitrary"),
    )(kvf, b_pad)


def _gate_bwd(dc, kvf, b_pad, colblk, name):
    S = kvf.shape[1]
    ts = _tile(S, 512, SUBLANES)
    n_s = S // ts

    def body(dc_ref, f_ref, b_ref, df_ref, db_ref, carry):
        @pl.when(pl.program_id(0) == 0)
        def _():
            carry[...] = jnp.zeros_like(carry)
            db_ref[...] = jnp.zeros_like(db_ref)

        tri = _tri(ts, lower=False)
        dls = _dot_exact(tri, dc_ref[...]) + carry[0:1, :]
        f = f_ref[0] + b_ref[...]
        df = dls * jax.nn.sigmoid(-f)
        df_ref[...] = df
        db_ref[...] += jnp.broadcast_to(jnp.sum(df, axis=0, keepdims=True), db_ref.shape)
        carry[...] = jnp.broadcast_to(dls[0:1, :], carry.shape)

    df, db = pl.pallas_call(
        body, name=name, grid=(n_s,),
        in_specs=[pl.BlockSpec((ts, LANES), lambda s: (n_s - 1 - s, 0)),
                  pl.BlockSpec((1, ts, LANES), lambda s: (0, n_s - 1 - s, colblk)),
                  pl.BlockSpec((1, LANES), lambda s: (0, 0))],
        out_specs=[pl.BlockSpec((ts, LANES), lambda s: (n_s - 1 - s, 0)),
                   pl.BlockSpec((SUBLANES, LANES), lambda s: (0, 0))],
        out_shape=[jax.ShapeDtypeStruct((S, LANES), F32), jax.ShapeDtypeStruct((SUBLANES, LANES), F32)],
        scratch_shapes=[pltpu.VMEM((SUBLANES, LANES), F32)],
        compiler_params=_params("arbitrary"),
    )(dc, kvf, b_pad)
    return df, db[0]


def _attn_tile(S):
    return _tile(S, 512, LANES)


def _split_heads(v, lo):
    zero = jnp.zeros_like(v)
    return jnp.where(lo, v, zero), jnp.where(lo, zero, v)


def _augment(base, c, mode, hd, name):
    S, D = base.shape
    ts = _tile(S, 512, 2 * SUBLANES)

    def body(b_ref, c_ref, o0_ref, o1_ref):
        lane = lax.broadcasted_iota(jnp.int32, (ts, LANES), 1)
        cc = c_ref[...] * LOG2E
        for t in range(D // LANES):
            cols = slice(t * LANES, (t + 1) * LANES)
            bt = b_ref[:, cols]
            for h, o_ref in ((0, o0_ref), (1, o1_ref)):
                first = hd if h == 0 else 0
                keep = (lane < hd) if h == 0 else (lane >= hd)
                if mode == "v":
                    vals = (1.0,)
                else:
                    col = cc[:, 2 * t + h:2 * t + h + 1]
                    hi = col.astype(BF).astype(F32)
                    mid = (col - hi).astype(BF).astype(F32)
                    pieces = (hi, mid, col - hi - mid)
                    vals = pieces + (1.0, 1.0, 1.0) if mode == "q" else (1.0, 1.0, 1.0) + tuple(-v for v in pieces)
                aug = jnp.zeros((ts, LANES), F32)
                for i, v in enumerate(vals):
                    aug = jnp.where(lane == first + i, v, aug)
                o_ref[:, cols] = jnp.where(keep, bt, aug.astype(BF))

    spec = pl.BlockSpec((ts, D), lambda s: (s, 0))
    return pl.pallas_call(
        body, name=name, grid=(S // ts,),
        in_specs=[spec, pl.BlockSpec((ts, LANES), lambda s: (s, 0))], out_specs=[spec, spec],
        out_shape=[jax.ShapeDtypeStruct((S, D), BF)] * 2,
        compiler_params=_params("parallel"),
    )(base, c)


def _attn_fwd(qa, ka, va, qg, hd, name):
    S, D = qa[0].shape
    P = D // LANES
    tq = _attn_tile(S)
    nq = S // tq

    def body(q0_ref, q1_ref, k0_ref, k1_ref, v0_ref, v1_ref, g_ref, o_ref, og_ref, lse_ref):
        qi = pl.program_id(1)
        lo = _head_masks((tq, LANES), hd)
        qh = (q0_ref[...], q1_ref[...])
        k_refs, v_refs = (k0_ref, k1_ref), (v0_ref, v1_ref)
        causal = lax.broadcasted_iota(jnp.int32, (tq, tq), 1) <= lax.broadcasted_iota(jnp.int32, (tq, tq), 0)

        def block(ki, carry, masked):
            off = pl.multiple_of(ki * tq, tq)
            out = []
            for h in range(2):
                m, acc = carry[h]
                s = lax.dot_general(qh[h], k_refs[h][pl.ds(off, tq), :], NT_DIMS, preferred_element_type=F32)
                if masked:
                    s = jnp.where(causal, s, -jnp.inf)
                m_new = jnp.maximum(m, jnp.max(s, axis=-1, keepdims=True))
                p = jnp.exp2(s - m_new)
                acc = jnp.exp2(m - m_new) * acc + jnp.dot(p.astype(BF), v_refs[h][pl.ds(off, tq), :],
                                                          preferred_element_type=F32)
                out.append((m_new, acc))
            return tuple(out)

        init = tuple((jnp.full((tq, 1), -jnp.inf, F32), jnp.zeros((tq, LANES), F32)) for _ in range(2))
        carry = lax.fori_loop(0, qi, lambda ki, c: block(ki, c, False), init)
        (m0, a0), (m1, a1) = block(qi, carry, True)
        l0, l1 = a0[:, hd:hd + 1], a1[:, 0:1]
        o = jnp.where(lo, a0 / l0, a1 / l1)
        o_ref[...] = o
        og_ref[...] = (o * jax.nn.sigmoid(g_ref[0])).astype(BF)
        lane2 = lax.broadcasted_iota(jnp.int32, (tq, 2), 1)
        lse_ref[0] = jnp.where(lane2 == 0, m0 + jnp.log2(l0), m1 + jnp.log2(l1))

    tile = pl.BlockSpec((tq, LANES), lambda p, i: (i, p))
    whole = pl.BlockSpec((S, LANES), lambda p, i: (0, p))
    return pl.pallas_call(
        body, name=name, grid=(P, nq),
        in_specs=[tile, tile, whole, whole, whole, whole, pl.BlockSpec((1, tq, LANES), lambda p, i: (1, i, p))],
        out_specs=[tile, tile, pl.BlockSpec((1, tq, 2), lambda p, i: (p, i, 0))],
        out_shape=[jax.ShapeDtypeStruct((S, D), F32), jax.ShapeDtypeStruct((S, D), BF),
                   jax.ShapeDtypeStruct((P, S, 2), F32)],
        compiler_params=_params("parallel", "arbitrary"),
    )(*qa, *ka, *va, qg)


def _attn_out_bwd(dog, o, qg, hd, name):
    S, D = o.shape
    P = D // LANES
    ts = _tile(S, 512, SUBLANES)

    def body(dog_ref, o_ref, g_ref, do_ref, dg_ref, dv_ref):
        lo = _head_masks((ts, LANES), hd)
        lane2 = lax.broadcasted_iota(jnp.int32, (ts, 2), 1)
        for t in range(P):
            cols = slice(t * LANES, (t + 1) * LANES)
            sg = jax.nn.sigmoid(g_ref[0, :, cols])
            dog_t, o_t = dog_ref[:, cols], o_ref[:, cols]
            do = (dog_t * sg).astype(BF)
            do_ref[:, cols] = do
            dg_ref[:, cols] = (dog_t * o_t * sg * (1.0 - sg)).astype(BF)
            prod = do.astype(F32) * o_t
            d0 = jnp.sum(jnp.where(lo, prod, 0.0), axis=-1, keepdims=True)
            d1 = jnp.sum(jnp.where(lo, 0.0, prod), axis=-1, keepdims=True)
            dv_ref[t] = jnp.where(lane2 == 0, d0, d1)

    return pl.pallas_call(
        body, name=name, grid=(S // ts,),
        in_specs=[pl.BlockSpec((ts, D), lambda s: (s, 0)), pl.BlockSpec((ts, D), lambda s: (s, 0)),
                  pl.BlockSpec((1, ts, D), lambda s: (1, s, 0))],
        out_specs=[pl.BlockSpec((ts, D), lambda s: (s, 0)), pl.BlockSpec((ts, D), lambda s: (s, 0)),
                   pl.BlockSpec((P, ts, 2), lambda s: (0, s, 0))],
        out_shape=[jax.ShapeDtypeStruct((S, D), BF), jax.ShapeDtypeStruct((S, D), BF),
                   jax.ShapeDtypeStruct((P, S, 2), F32)],
        compiler_params=_params("parallel"),
    )(dog, o, qg)


def _attn_dq(qa, ka, vb, do, lse, dvec, hd, name):
    S, D = vb.shape
    P = D // LANES
    tq = _attn_tile(S)
    nq = S // tq
    scale = hd ** -0.5

    def body(q0_ref, q1_ref, k0_ref, k1_ref, v_ref, do_ref, lse_ref, dv_ref, dq_ref, dv2_ref):
        qi = pl.program_id(1)
        lo = _head_masks((tq, LANES), hd)
        qh = (q0_ref[...], q1_ref[...])
        k_refs = (k0_ref, k1_ref)
        doh = _split_heads(do_ref[...], lo)
        lse_t, dv_t = lse_ref[0], dv_ref[0]
        causal = lax.broadcasted_iota(jnp.int32, (tq, tq), 1) <= lax.broadcasted_iota(jnp.int32, (tq, tq), 0)

        def block(ki, carry, masked):
            off = pl.multiple_of(ki * tq, tq)
            vt = v_ref[pl.ds(off, tq), :]
            out = []
            for h in range(2):
                kb = k_refs[h][pl.ds(off, tq), :]
                s = lax.dot_general(qh[h], kb, NT_DIMS, preferred_element_type=F32)
                p = jnp.exp2(s - lse_t[:, h:h + 1])
                if masked:
                    p = jnp.where(causal, p, 0.0)
                dp = lax.dot_general(doh[h], vt, NT_DIMS, preferred_element_type=F32)
                ds = p * (dp - dv_t[:, h:h + 1])
                acc, rs = carry[h]
                out.append((acc + jnp.dot(ds.astype(BF), kb, preferred_element_type=F32),
                            rs + jnp.sum(ds, axis=-1, keepdims=True)))
            return tuple(out)

        init = tuple((jnp.zeros((tq, LANES), F32), jnp.zeros((tq, 1), F32)) for _ in range(2))
        carry = lax.fori_loop(0, qi, lambda ki, c: block(ki, c, False), init)
        (d0, r0), (d1, r1) = block(qi, carry, True)
        dq_ref[...] = jnp.where(lo, d0, d1) * scale
        lane2 = lax.broadcasted_iota(jnp.int32, (tq, 2), 1)
        dv2_ref[0] = dv_t + jnp.where(lane2 == 0, r0, r1)

    tile = pl.BlockSpec((tq, LANES), lambda p, i: (i, p))
    whole = pl.BlockSpec((S, LANES), lambda p, i: (0, p))
    stat = pl.BlockSpec((1, tq, 2), lambda p, i: (p, i, 0))
    return pl.pallas_call(
        body, name=name, grid=(P, nq),
        in_specs=[tile, tile, whole, whole, whole, tile, stat, stat],
        out_specs=[tile, stat],
        out_shape=[jax.ShapeDtypeStruct((S, D), F32), jax.ShapeDtypeStruct((P, S, 2), F32)],
        compiler_params=_params("parallel", "arbitrary"),
    )(*qa, *ka, vb, do, lse, dvec)


def _attn_dkv(qa, ka, vb, do, lse_row, dvec_row, hd, name):
    S, D = vb.shape
    P = D // LANES
    tk = _attn_tile(S)
    nk = S // tk

    def body(q0_ref, q1_ref, do_ref, k0_ref, k1_ref, v_ref, lse_ref, dvr_ref, dk_ref, dv_ref, dc_ref):
        ki = pl.program_id(1)
        lo = _head_masks((tk, LANES), hd)
        kh = (k0_ref[...], k1_ref[...])
        q_refs = (q0_ref, q1_ref)
        vh = _split_heads(v_ref[...], lo)
        causal_t = lax.broadcasted_iota(jnp.int32, (tk, tk), 0) <= lax.broadcasted_iota(jnp.int32, (tk, tk), 1)

        def block(qi, carry, masked):
            off = pl.multiple_of(qi * tk, tk)
            dob = do_ref[pl.ds(off, tk), :]
            lse_t, dv_t = lse_ref[0, qi], dvr_ref[0, qi]
            out = []
            for h in range(2):
                dk, dv, dc = carry[h]
                qb = q_refs[h][pl.ds(off, tk), :]
                st = lax.dot_general(kh[h], qb, NT_DIMS, preferred_element_type=F32)
                pt = jnp.exp2(st - lse_t[h:h + 1, :])
                if masked:
                    pt = jnp.where(causal_t, pt, 0.0)
                dv = dv + jnp.dot(pt.astype(BF), dob, preferred_element_type=F32)
                dpt = lax.dot_general(vh[h], dob, NT_DIMS, preferred_element_type=F32)
                dst = pt * (dpt - dv_t[h:h + 1, :])
                dk = dk + jnp.dot(dst.astype(BF), qb, preferred_element_type=F32)
                dc = dc - jnp.sum(dst, axis=-1, keepdims=True)
                out.append((dk, dv, dc))
            return tuple(out)

        init = tuple((jnp.zeros((tk, LANES), F32), jnp.zeros((tk, LANES), F32), jnp.zeros((tk, 1), F32))
                     for _ in range(2))
        carry = block(ki, init, True)
        (dk0, dv0, dc0), (dk1, dv1, dc1) = lax.fori_loop(ki + 1, nk, lambda qi, c: block(qi, c, False), carry)
        dk_ref[...] = jnp.where(lo, dk0, dk1) * (1.0 / LOG2E)
        dv_ref[...] = jnp.where(lo, dv0, dv1)
        lane2 = lax.broadcasted_iota(jnp.int32, (tk, 2), 1)
        dc_ref[0] = jnp.where(lane2 == 0, dc0, dc1)

    tile = pl.BlockSpec((tk, LANES), lambda p, i: (i, p))
    whole = pl.BlockSpec((S, LANES), lambda p, i: (0, p))
    row_spec = pl.BlockSpec((1, nk, 2, tk), lambda p, i: (p, 0, 0, 0))
    return pl.pallas_call(
        body, name=name, grid=(P, nk),
        in_specs=[whole, whole, whole, tile, tile, tile, row_spec, row_spec],
        out_specs=[tile, tile, pl.BlockSpec((1, tk, 2), lambda p, i: (p, i, 0))],
        out_shape=[jax.ShapeDtypeStruct((S, D), F32), jax.ShapeDtypeStruct((S, D), F32),
                   jax.ShapeDtypeStruct((P, S, 2), F32)],
        compiler_params=_params("parallel", "arbitrary"),
    )(*qa, do, *ka, vb, lse_row, dvec_row)


def _loss_head(y, t, name):
    S, D = y.shape
    ts = _tile(S, 512, SUBLANES)

    def body(y_ref, t_ref, dy_ref, l_ref):
        @pl.when(pl.program_id(0) == 0)
        def _():
            l_ref[...] = jnp.zeros_like(l_ref)

        e = y_ref[...] - t_ref[...]
        dy_ref[...] = e * (1.0 / D)
        part = 0.5 * jnp.sum(jnp.mean(e * e, axis=-1, keepdims=True), axis=0, keepdims=True)
        l_ref[...] += jnp.broadcast_to(part, l_ref.shape)

    return pl.pallas_call(
        body, name=name, grid=(S // ts,),
        in_specs=[pl.BlockSpec((ts, D), lambda s: (s, 0)), pl.BlockSpec((ts, D), lambda s: (s, 0))],
        out_specs=[pl.BlockSpec((ts, D), lambda s: (s, 0)), pl.BlockSpec((SUBLANES, LANES), lambda s: (0, 0))],
        out_shape=[jax.ShapeDtypeStruct((S, D), F32), jax.ShapeDtypeStruct((SUBLANES, LANES), F32)],
        compiler_params=_params("arbitrary"),
    )(y, t)


def _adamw(w, g, m, v, name):
    shape = w.shape
    cols = shape[-1]
    as2d = lambda a: a.reshape(-1, cols)
    rows = as2d(w).shape[0]
    tr = _tile(rows, 256, SUBLANES) if rows % SUBLANES == 0 else rows
    c1 = 1.0 - ADAM_B1 ** ADAM_STEP
    c2 = 1.0 - ADAM_B2 ** ADAM_STEP

    def body(w_ref, g_ref, m_ref, v_ref, d_ref, nm_ref, nv_ref):
        gg = g_ref[...]
        nm = ADAM_B1 * m_ref[...] + (1.0 - ADAM_B1) * gg
        nv = ADAM_B2 * v_ref[...] + (1.0 - ADAM_B2) * (gg * gg)
        d_ref[...] = -ADAM_LR * ((nm / c1) / (jnp.sqrt(nv / c2) + ADAM_EPS) + ADAM_WD * w_ref[...])
        nm_ref[...] = nm
        nv_ref[...] = nv

    spec = pl.BlockSpec((tr, cols), lambda r: (r, 0))
    outs = pl.pallas_call(
        body, name=name, grid=(rows // tr,), in_specs=[spec] * 4, out_specs=[spec] * 3,
        out_shape=[jax.ShapeDtypeStruct((rows, cols), F32)] * 3,
        compiler_params=_params("parallel"),
    )(as2d(w), as2d(g), as2d(m), as2d(v))
    return tuple(o.reshape(shape) for o in outs)


def _place():
    return lax.axis_index("x"), lax.axis_index("y"), lax.axis_index("c")


def _other_chips(x, y):
    return [(1 - x, y), (x, 1 - y), (1 - x, 1 - y)]


def _remote(src, dst, send_sems, recv_sems, k, to):
    return pltpu.make_async_remote_copy(src_ref=src, dst_ref=dst, send_sem=send_sems.at[k], recv_sem=recv_sems.at[k],
                                        device_id=to, device_id_type=MESH)


def _allgather_weights(flat):
    R, W = flat.shape
    rh = R // 2

    def body(src, dst, send_sems, recv_sems):
        x, y, c = _place()
        me = 2 * x + y
        sib = (x, y, 1 - c)
        chips = _other_chips(x, y)
        half = pl.ds(pl.multiple_of(c * rh, 16), rh)
        other = pl.ds(pl.multiple_of((1 - c) * rh, 16), rh)
        first = [_remote(src.at[half], dst.at[me, half], send_sems, recv_sems, j, (cx, cy, c))
                 for j, (cx, cy) in enumerate(chips)]
        for cp in first:
            cp.start()
        passed = []
        for j, (cx, cy) in enumerate(chips):
            landed = dst.at[2 * cx + cy, half]
            _remote(landed, landed, send_sems, recv_sems, j, sib).wait_recv()
            cp = _remote(landed, landed, send_sems, recv_sems, 3 + j, sib)
            cp.start()
            passed.append(cp)
        for j, (cx, cy) in enumerate(chips):
            landed = dst.at[2 * cx + cy, other]
            _remote(landed, landed, send_sems, recv_sems, 3 + j, sib).wait_recv()
        for cp in first + passed:
            cp.wait_send()

    out = pl.pallas_call(
        body, name="allgather_weights", in_specs=[ANY], out_specs=ANY,
        out_shape=jax.ShapeDtypeStruct((N_CHIPS, R, W), flat.dtype),
        scratch_shapes=[pltpu.SemaphoreType.DMA((6,)), pltpu.SemaphoreType.DMA((6,))],
    )(flat)
    x, y, _ = _place()
    return lax.dynamic_update_slice_in_dim(out, flat[None], 2 * x + y, axis=0)


def _pair_exchange(g):
    n, _, rh, W = g.shape

    def body(g_ref, t_ref, send_sems, recv_sems):
        x, y, c = _place()
        cps = [_remote(g_ref.at[k, 1 - c], t_ref.at[k], send_sems, recv_sems, k, (x, y, 1 - c)) for k in range(n)]
        for cp in cps:
            cp.start()
        for cp in cps:
            cp.wait()

    return pl.pallas_call(
        body, name="grad_pair_exchange", in_specs=[ANY], out_specs=ANY,
        out_shape=jax.ShapeDtypeStruct((n, rh, W), g.dtype),
        scratch_shapes=[pltpu.SemaphoreType.DMA((n,)), pltpu.SemaphoreType.DMA((n,))],
    )(g)


def _pair_add(g, t, c):
    n, _, rh, W = g.shape
    tr = _tile(rh, 256, 2 * SUBLANES)

    def body(c_ref, g_ref, t_ref, o_ref):
        o_ref[...] = (g_ref[0] + t_ref[...]).astype(BF)

    return pl.pallas_call(
        body, name="grad_pair_add",
        grid_spec=pltpu.PrefetchScalarGridSpec(
            num_scalar_prefetch=1, grid=(n, rh // tr),
            in_specs=[pl.BlockSpec((1, 1, tr, W), lambda k, i, c_ref: (k, c_ref[0], i, 0)),
                      pl.BlockSpec((1, tr, W), lambda k, i, c_ref: (k, i, 0))],
            out_specs=pl.BlockSpec((1, tr, W), lambda k, i, c_ref: (k, i, 0))),
        out_shape=jax.ShapeDtypeStruct((n, rh, W), BF),
        compiler_params=_params("parallel", "parallel"),
    )(c.reshape(1).astype(jnp.int32), g, t)


def _chip_exchange(a):
    _, rh, W = a.shape

    def body(a_ref, t_ref, send_sems, recv_sems):
        x, y, c = _place()
        cps = [_remote(a_ref.at[2 * cx + cy], t_ref.at[j], send_sems, recv_sems, j, (cx, cy, c))
               for j, (cx, cy) in enumerate(_other_chips(x, y))]
        for cp in cps:
            cp.start()
        for cp in cps:
            cp.wait()

    return pl.pallas_call(
        body, name="grad_chip_exchange", in_specs=[ANY], out_specs=ANY,
        out_shape=jax.ShapeDtypeStruct((3, rh, W), a.dtype),
        scratch_shapes=[pltpu.SemaphoreType.DMA((3,)), pltpu.SemaphoreType.DMA((3,))],
    )(a)


def _chip_add(g, t1, t2, c, me):
    _, _, rh, W = g.shape
    tr = _tile(rh, 256, 2 * SUBLANES)

    def body(c_ref, me_ref, g_ref, t1_ref, t2_ref, o_ref):
        own = g_ref[0, 0] + t1_ref[0]
        o_ref[...] = own + t2_ref[0].astype(F32) + t2_ref[1].astype(F32) + t2_ref[2].astype(F32)

    return pl.pallas_call(
        body, name="grad_chip_add",
        grid_spec=pltpu.PrefetchScalarGridSpec(
            num_scalar_prefetch=2, grid=(rh // tr,),
            in_specs=[pl.BlockSpec((1, 1, tr, W), lambda i, c_ref, me_ref: (me_ref[0], c_ref[0], i, 0)),
                      pl.BlockSpec((1, tr, W), lambda i, c_ref, me_ref: (me_ref[0], i, 0)),
                      pl.BlockSpec((3, tr, W), lambda i, c_ref, me_ref: (0, i, 0))],
            out_specs=pl.BlockSpec((tr, W), lambda i, c_ref, me_ref: (i, 0))),
        out_shape=jax.ShapeDtypeStruct((rh, W), F32),
        compiler_params=_params("parallel"),
    )(c.reshape(1).astype(jnp.int32), me.reshape(1).astype(jnp.int32), g, t1, t2)


def _pair_share(h):
    rh, W = h.shape

    def body(h_ref, f_ref, send_sems, recv_sems):
        x, y, c = _place()
        cp = _remote(h_ref, f_ref, send_sems, recv_sems, 0, (x, y, 1 - c))
        cp.start()
        cp.wait()

    return pl.pallas_call(
        body, name="grad_pair_share", in_specs=[ANY], out_specs=ANY,
        out_shape=jax.ShapeDtypeStruct((rh, W), h.dtype),
        scratch_shapes=[pltpu.SemaphoreType.DMA((1,)), pltpu.SemaphoreType.DMA((1,))],
    )(h)


def _allreduce_small(pack, name):
    rows, W = pack.shape

    def body(p_ref, o_ref, buf, send_sems, recv_sems):
        x, y, c = _place()
        me = 4 * x + 2 * y + c
        buf[me] = p_ref[...]
        cps = []
        for r in range(1, 8):
            fx, fy, fc = (r >> 2) & 1, (r >> 1) & 1, r & 1
            to = (1 - x if fx else x, 1 - y if fy else y, 1 - c if fc else c)
            cps.append(_remote(p_ref, buf.at[me], send_sems, recv_sems, r - 1, to))
        for cp in cps:
            cp.start()
        for r in range(1, 8):
            fx, fy, fc = (r >> 2) & 1, (r >> 1) & 1, r & 1
            frm = 4 * (1 - x if fx else x) + 2 * (1 - y if fy else y) + (1 - c if fc else c)
            _remote(p_ref, buf.at[frm], send_sems, recv_sems, r - 1, (x, y, c)).wait_recv()
        for cp in cps:
            cp.wait_send()
        acc = buf[0]
        for i in range(1, 8):
            acc = acc + buf[i]
        o_ref[...] = acc

    return pl.pallas_call(
        body, name=name, in_specs=[VMEM], out_specs=VMEM,
        out_shape=jax.ShapeDtypeStruct((rows, W), F32),
        scratch_shapes=[pltpu.VMEM((8, rows, W), F32), pltpu.SemaphoreType.DMA((7,)), pltpu.SemaphoreType.DMA((7,))],
    )(pack)


def _flatten(arrs, dtype):
    flat = jnp.concatenate([a.reshape(-1).astype(dtype) for a in arrs])
    per = FLAT_W * FLAT_ROW_MULT
    total = -(-flat.shape[0] // per) * per
    return jnp.pad(flat, (0, total - flat.shape[0])).reshape(-1, FLAT_W)


def _unflatten(flat2d, shapes):
    out, off = [], 0
    for shp in shapes:
        n = 1
        for d in shp:
            n *= d
        out.append(flat2d[..., off:off + n].reshape(flat2d.shape[:-1] + tuple(shp)))
        off += n
    return out


def _join_cols(g):
    nd = g.ndim
    return jnp.moveaxis(g, 0, nd - 2).reshape(g.shape[1:-1] + (N_CHIPS * g.shape[-1],))


def _split_cols(full):
    c = full.shape[-1] // N_CHIPS
    return jnp.moveaxis(full.reshape(full.shape[:-1] + (N_CHIPS, c)), -2, 0)


def _join_rows(g):
    return jnp.moveaxis(g, 0, 1).reshape(g.shape[1], N_CHIPS * g.shape[2], g.shape[3])


def _split_rows(full):
    L, r4, D = full.shape
    return jnp.moveaxis(full.reshape(L, N_CHIPS, r4 // N_CHIPS, D), 1, 0)


def _row_layout(a, tq):
    P, S, _ = a.shape
    return a.reshape(P, S // tq, tq, 2).transpose(0, 1, 3, 2)


def _pad_row(v, width=FLAT_W):
    flat = v.reshape(-1)
    rows = -(-flat.shape[0] // width)
    return jnp.pad(flat, (0, rows * width - flat.shape[0]))


def kernel(x, attn_norm, ffn_norm, a_w_in, a_conv, a_w_out, kv_norm, w_kvf, b_f, k_norm, b_w_qg, q_norm, b_w_out, ffn_w_up, ffn_conv, ffn_w_down, loss_target, m_attn_norm, m_ffn_norm, m_a_w_in, m_a_conv, m_a_w_out, m_kv_norm, m_w_kvf, m_b_f, m_k_norm, m_b_w_qg, m_q_norm, m_b_w_out, m_ffn_w_up, m_ffn_conv, m_ffn_w_down, v_attn_norm, v_ffn_norm, v_a_w_in, v_a_conv, v_a_w_out, v_kv_norm, v_w_kvf, v_b_f, v_k_norm, v_b_w_qg, v_q_norm, v_b_w_out, v_ffn_w_up, v_ffn_conv, v_ffn_w_down):
    xs = x[0]
    S, D = xs.shape
    H, hd = b_f.shape[0], k_norm.shape[0]
    depth = attn_norm.shape[0]
    n_a = a_w_in.shape[0]
    P = D // LANES
    assert LANES == 2 * hd and H * hd == D, "the attention kernels hold two heads per lane tile"
    mx, my, mc = _place()
    chip = 2 * mx + my

    big = [a_w_in, a_w_out, w_kvf, b_w_qg, b_w_out, ffn_w_up, ffn_w_down]
    big_shapes = [w.shape for w in big]
    gathered = _allgather_weights(_flatten(big, BF))
    g_in, g_out, g_kvf, g_qg, g_bout, g_up, g_down = _unflatten(gathered.reshape(N_CHIPS, -1), big_shapes)
    wa_in, wb_qg, w_up = _join_cols(g_in), _join_cols(g_qg), _join_cols(g_up)
    wa_out, wb_out, w_down = _join_rows(g_out), _join_rows(g_bout), _join_rows(g_down)
    kvf_cols = 2 * D + LANES
    wkvf = jnp.pad(_join_cols(g_kvf), ((0, 0), (0, kvf_cols - (2 * D + H))))

    def placed(shard):
        full = jnp.zeros(shard.shape[:-1] + (N_CHIPS, shard.shape[-1]), F32)
        full = lax.dynamic_update_slice_in_dim(full, shard[..., None, :], chip, axis=full.ndim - 2)
        return jnp.where(mc == 0, full, 0.0).reshape(-1)

    conv_pack = jnp.concatenate([_pad_row(placed(a_conv)), _pad_row(placed(ffn_conv))]).reshape(-1, FLAT_W)
    conv_full = _allreduce_small(conv_pack, "allgather_conv_taps").reshape(-1)
    n_ac = a_conv.size * N_CHIPS
    a_conv_f = conv_full[:n_ac].reshape(a_conv.shape[:-1] + (-1,))
    off = _pad_row(placed(a_conv)).shape[0]
    ffn_conv_f = conv_full[off:off + ffn_conv.size * N_CHIPS].reshape(ffn_conv.shape[:-1] + (-1,))
    F = ffn_conv_f.shape[-1]

    b_pad = jnp.pad(b_f, (0, LANES - H)).reshape(1, LANES)
    gate_blk = 2 * D // LANES
    tq = _attn_tile(S)
    scale = hd ** -0.5

    saved = []
    cur = xs
    kv = None
    for l in range(depth):
        rec = {"x_in": cur}
        if l < n_a:
            proj, xn = _norm_matmul(cur, attn_norm[l], wa_in[l], 3, BF, f"a_in_{l}")
            z = _mixer_mid_fwd(proj, a_conv_f[l], f"a_mid_{l}")
            mid = _matmul_residual(z, wa_out[l], cur, f"a_out_{l}")
            rec.update(proj=proj, xn=xn, z=z)
        else:
            j = l - n_a
            if kv is None:
                kvf, hn = _norm_matmul(cur, kv_norm, wkvf, 1, F32, "kvf_proj")
                kn = _headnorm(kvf, 0, 0, k_norm, 1.0, D, "k_norm")
                vb = kvf[0, :, D:2 * D].astype(BF)
                cgate = _gate_fwd(kvf, b_pad, gate_blk, "gate_cumsum")
                kv = dict(kvf=kvf, hn=hn, vb=vb, cgate=cgate, x_in=cur, dk=[], dv=[], dc=[],
                          ka=_augment(kn, cgate, "k", hd, "k_augment"), va=_augment(vb, cgate, "v", hd, "v_augment"))
            qg, xn = _norm_matmul(cur, attn_norm[l], wb_qg[j], 2, F32, f"qg_proj_{j}")
            qn = _headnorm(qg, 0, 0, q_norm[j], scale * LOG2E, D, f"q_norm_{j}")
            qa = _augment(qn, kv["cgate"], "q", hd, f"q_augment_{j}")
            o, og, lse = _attn_fwd(qa, kv["ka"], kv["va"], qg, hd, f"attn_fwd_{j}")
            mid = _matmul_residual(og, wb_out[j], cur, f"b_out_{j}")
            rec.update(qg=qg, xn=xn, qa=qa, o=o, og=og, lse=lse)
        up, xn2 = _norm_matmul(mid, ffn_norm[l], w_up[l], 2, BF, f"ffn_up_{l}")
        z2 = _ffn_mid_fwd(up, ffn_conv_f[l], f"ffn_mid_{l}")
        cur = _matmul_residual(z2, w_down[l], mid, f"ffn_down_{l}")
        rec.update(x_mid=mid, up=up, xn2=xn2, z2=z2)
        saved.append(rec)

    dy, loss_part = _loss_head(cur, loss_target[0], "loss_head")

    g_attn_norm, g_ffn_norm = [None] * depth, [None] * depth
    g_a_in, g_a_conv, g_a_out = [None] * n_a, [None] * n_a, [None] * n_a
    g_qg, g_qn, g_bo = [None] * (depth - n_a), [None] * (depth - n_a), [None] * (depth - n_a)
    g_up, g_fc, g_down = [None] * depth, [None] * depth, [None] * depth
    for l in reversed(range(depth)):
        rec = saved[l]
        dz = _matmul_nt(dy, w_down[l], f"ffn_down_bwd_{l}")
        dup, g_fc[l] = _ffn_mid_bwd(rec["up"], dz, ffn_conv_f[l], f"ffn_mid_bwd_{l}")
        g_down[l] = _wgrad(rec["z2"], dy[None], f"ffn_down_wgrad_{l}")
        g_up[l] = _wgrad(rec["xn2"], dup, f"ffn_up_wgrad_{l}")
        dy, g_ffn_norm[l] = _dnorm(dup, w_up[l], rec["x_mid"], ffn_norm[l], dy, f"ffn_up_bwd_{l}")
        if l < n_a:
            dz = _matmul_nt(dy, wa_out[l], f"a_out_bwd_{l}")
            dproj, g_a_conv[l] = _mixer_mid_bwd(rec["proj"], dz, a_conv_f[l], f"a_mid_bwd_{l}")
            g_a_out[l] = _wgrad(rec["z"], dy[None], f"a_out_wgrad_{l}")
            g_a_in[l] = _wgrad(rec["xn"], dproj, f"a_in_wgrad_{l}")
            dy, g_attn_norm[l] = _dnorm(dproj, wa_in[l], rec["x_in"], attn_norm[l], dy, f"a_in_bwd_{l}")
        else:
            j = l - n_a
            dog = _matmul_nt(dy, wb_out[j], f"b_out_bwd_{j}")
            do, dgate, dvec = _attn_out_bwd(dog, rec["o"], rec["qg"], hd, f"attn_gate_bwd_{j}")
            g_bo[j] = _wgrad(rec["og"], dy[None], f"b_out_wgrad_{j}")
            dqn, drow = _attn_dq(rec["qa"], kv["ka"], kv["vb"], do, rec["lse"], dvec, hd, f"attn_dq_{j}")
            dk, dv, dc = _attn_dkv(rec["qa"], kv["ka"], kv["vb"], do, _row_layout(rec["lse"], tq),
                                   _row_layout(drow, tq), hd, f"attn_dkv_{j}")
            kv["dk"].append(dk)
            kv["dv"].append(dv)
            kv["dc"].append(dc)
            dq_pre, g_qn[j] = _headnorm_bwd(rec["qg"], 0, 0, q_norm[j], [dqn], D, f"q_norm_bwd_{j}")
            dqg = jnp.stack([dq_pre, dgate])
            g_qg[j] = _wgrad(rec["xn"], dqg, f"qg_wgrad_{j}")
            dy, g_attn_norm[l] = _dnorm(dqg, wb_qg[j], rec["x_in"], attn_norm[l], dy, f"qg_bwd_{j}")
            if l == n_a:
                dk_s, g_k_norm = _headnorm_bwd(kv["kvf"], 0, 0, k_norm, kv["dk"], D, "k_norm_bwd")
                dv_s = functools.reduce(jnp.add, kv["dv"]).astype(BF)
                dc_sum = functools.reduce(jnp.add, kv["dc"])
                dc_pad = jnp.pad(dc_sum.transpose(1, 0, 2).reshape(S, H), ((0, 0), (0, LANES - H)))
                df, db = _gate_bwd(dc_pad, kv["kvf"], b_pad, gate_blk, "gate_bwd")
                dkvf = jnp.concatenate([dk_s, dv_s, df.astype(BF)], axis=1)[None]
                g_kvf = _wgrad(kv["hn"], dkvf, "kvf_wgrad")[:, :2 * D + H]
                g_b_f = db[:H]
                dy, g_kv_norm = _dnorm(dkvf, wkvf, kv["x_in"], kv_norm, dy, "kvf_bwd")
    grad_x = dy[None]

    full_grads = [_split_cols(jnp.stack(g_a_in)), _split_rows(jnp.stack(g_a_out)), _split_cols(g_kvf),
                  _split_cols(jnp.stack(g_qg)), _split_rows(jnp.stack(g_bo)), _split_cols(jnp.stack(g_up)),
                  _split_rows(jnp.stack(g_down))]
    gflat = jnp.stack([_flatten([g[k] for g in full_grads], F32) for k in range(N_CHIPS)])
    R = gflat.shape[1]
    g4 = gflat.reshape(N_CHIPS, 2, R // 2, FLAT_W)
    from_sibling = _pair_exchange(g4)
    from_chips = _chip_exchange(_pair_add(g4, from_sibling, mc))
    mine = _chip_add(g4, from_sibling, from_chips, mc, chip)
    theirs = _pair_share(mine)
    shard_flat = jnp.where(mc == 0, jnp.stack([mine, theirs]), jnp.stack([theirs, mine])).reshape(-1)
    big_grads = _unflatten(shard_flat, big_shapes)

    small = [loss_part[0, :1], jnp.stack(g_attn_norm), jnp.stack(g_ffn_norm), g_kv_norm, g_b_f, g_k_norm,
             jnp.stack(g_qn), jnp.stack(g_a_conv), jnp.stack(g_fc)]
    small_sum = _allreduce_small(jnp.concatenate([_pad_row(s) for s in small]).reshape(-1, FLAT_W),
                                 "allreduce_small_grads").reshape(-1)
    parts, off = [], 0
    for s in small:
        parts.append(small_sum[off:off + s.size].reshape(s.shape))
        off += _pad_row(s).shape[0]
    loss = parts[0][0]
    gr_attn_norm, gr_ffn_norm, gr_kv_norm, gr_b_f, gr_k_norm, gr_q_norm, gr_a_conv_full, gr_ffn_conv_full = parts[1:]

    def my_cols(full):
        c = full.shape[-1] // N_CHIPS
        return lax.dynamic_slice_in_dim(full, chip * c, c, axis=full.ndim - 1)

    gr_a_in, gr_a_out, gr_kvf, gr_qg, gr_bo, gr_up, gr_down = big_grads
    grads = [gr_attn_norm, gr_ffn_norm, gr_a_in, my_cols(gr_a_conv_full), gr_a_out, gr_kv_norm, gr_kvf, gr_b_f,
             gr_k_norm, gr_qg, gr_q_norm, gr_bo, gr_up, my_cols(gr_ffn_conv_full), gr_down]
    weights = [attn_norm, ffn_norm, a_w_in, a_conv, a_w_out, kv_norm, w_kvf, b_f, k_norm, b_w_qg, q_norm, b_w_out,
               ffn_w_up, ffn_conv, ffn_w_down]
    ms = [m_attn_norm, m_ffn_norm, m_a_w_in, m_a_conv, m_a_w_out, m_kv_norm, m_w_kvf, m_b_f, m_k_norm, m_b_w_qg,
          m_q_norm, m_b_w_out, m_ffn_w_up, m_ffn_conv, m_ffn_w_down]
    vs = [v_attn_norm, v_ffn_norm, v_a_w_in, v_a_conv, v_a_w_out, v_kv_norm, v_w_kvf, v_b_f, v_k_norm, v_b_w_qg,
          v_q_norm, v_b_w_out, v_ffn_w_up, v_ffn_conv, v_ffn_w_down]
    deltas, new_ms, new_vs = [], [], []
    for i, (w, g, m, v) in enumerate(zip(weights, grads, ms, vs)):
        d, nm, nv = _adamw(w, g, m, v, f"adamw_{i}")
        deltas.append(d)
        new_ms.append(nm)
        new_vs.append(nv)
    return (loss, grad_x, *grads, *deltas, *new_ms, *new_vs)
```

```python
import functools

import jax
import jax.numpy as jnp
from jax import lax
from jax.experimental import pallas as pl
from jax.experimental.pallas import tpu as pltpu

F32 = jnp.float32
BF = jnp.bfloat16
LANES = 128
SUBLANES = 8
RMS_EPS = 1e-6
LOG2E = 1.4426950408889634
FLAT_W = 1024
FLAT_ROW_MULT = 512
N_CHIPS = 4
CONV_W = 3
HALO = SUBLANES

ADAM_LR = 0.001
ADAM_B1 = 0.9
ADAM_B2 = 0.999
ADAM_EPS = 1e-08
ADAM_WD = 0.01
ADAM_STEP = 10

MESH = pl.DeviceIdType.MESH
ANY = pl.BlockSpec(memory_space=pl.ANY)
VMEM = pl.BlockSpec(memory_space=pltpu.VMEM)
NT_DIMS = (((1,), (1,)), ((), ()))
TN_DIMS = (((0,), (0,)), ((), ()))


def _tile(n, pref, mult=LANES):
    t = (min(pref, n) // mult) * mult
    while t >= mult:
        if n % t == 0:
            break
        t -= mult
    if t < mult or (t * 4 < pref and n <= 4 * pref):
        return n
    return t


def _params(*sem):
    return pltpu.CompilerParams(dimension_semantics=sem)


def _norm_matmul(x, g, w, parts, out_dtype, name):
    S, D = x.shape
    C = w.shape[1] // parts
    ts, tn = _tile(S, 512, SUBLANES), _tile(C, 1408)
    npc = C // tn

    def body(x_ref, g_ref, w_ref, o_ref, xn_ref):
        @pl.when(pl.program_id(1) == 0)
        def _():
            xf = x_ref[...]
            r = lax.rsqrt(jnp.mean(xf * xf, axis=-1, keepdims=True) + RMS_EPS)
            xn_ref[...] = (xf * r * g_ref[...]).astype(BF)

        o_ref[0] = jnp.dot(xn_ref[...], w_ref[...], preferred_element_type=F32).astype(out_dtype)

    return pl.pallas_call(
        body, name=name, grid=(S // ts, parts * npc),
        in_specs=[pl.BlockSpec((ts, D), lambda s, n: (s, 0)),
                  pl.BlockSpec((1, D), lambda s, n: (0, 0)),
                  pl.BlockSpec((D, tn), lambda s, n: (0, n))],
        out_specs=[pl.BlockSpec((1, ts, tn), lambda s, n: (n // npc, s, n % npc)),
                   pl.BlockSpec((ts, D), lambda s, n: (s, 0))],
        out_shape=[jax.ShapeDtypeStruct((parts, S, C), out_dtype), jax.ShapeDtypeStruct((S, D), BF)],
        compiler_params=_params("parallel", "arbitrary"),
    )(x, g.reshape(1, D), w)


def _shift_down(u, prev, k, row):
    r = pltpu.roll(u, k, 0)
    for j in range(k):
        r = jnp.where(row == j, prev[HALO - k + j:HALO - k + j + 1, :], r)
    return r


def _shift_up(d, nxt, k, row):
    n = d.shape[0]
    r = pltpu.roll(d, n - k, 0)
    for j in range(k):
        r = jnp.where(row == n - k + j, nxt[j:j + 1, :], r)
    return r


def _conv3(u, prev, w, row):
    return _shift_down(u, prev, 2, row) * w[0:1] + _shift_down(u, prev, 1, row) * w[1:2] + u * w[2:3]


def _conv3_t(d, nxt, w, row):
    return d * w[2:3] + _shift_up(d, nxt, 1, row) * w[1:2] + _shift_up(d, nxt, 2, row) * w[0:1]


def _tap_rows(t0, t1, t2):
    row = lax.broadcasted_iota(jnp.int32, (SUBLANES, t0.shape[1]), 0)
    return jnp.where(row == 0, t0, jnp.where(row == 1, t1, jnp.where(row == 2, t2, 0.0)))


def _pad_conv(cw):
    return jnp.pad(cw, ((0, SUBLANES - CONV_W), (0, 0)))


def _mixer_mid_fwd(proj, cw, name):
    _, S, C = proj.shape
    ts, tc = _tile(S, 512, SUBLANES), _tile(C, 1024)

    def body(b_ref, c_ref, h_ref, cw_ref, z_ref, carry):
        @pl.when(pl.program_id(1) == 0)
        def _():
            carry[...] = jnp.zeros_like(carry)

        u = c_ref[0].astype(F32) * h_ref[0].astype(F32)
        row = lax.broadcasted_iota(jnp.int32, u.shape, 0)
        cv = _conv3(u, carry[...], cw_ref[...], row)
        z_ref[...] = (b_ref[0].astype(F32) * cv).astype(BF)
        carry[...] = u[ts - HALO:ts, :]

    part = lambda p: pl.BlockSpec((1, ts, tc), lambda c, s: (p, s, c))
    return pl.pallas_call(
        body, name=name, grid=(C // tc, S // ts),
        in_specs=[part(0), part(1), part(2), pl.BlockSpec((SUBLANES, tc), lambda c, s: (0, c))],
        out_specs=pl.BlockSpec((ts, tc), lambda c, s: (s, c)),
        out_shape=jax.ShapeDtypeStruct((S, C), BF),
        scratch_shapes=[pltpu.VMEM((HALO, tc), F32)],
        compiler_params=_params("parallel", "arbitrary"),
    )(proj, proj, proj, _pad_conv(cw))


def _ffn_mid_fwd(up, cw, name):
    _, S, C = up.shape
    ts, tc = _tile(S, 512, SUBLANES), _tile(C, 1408)

    def body(a_ref, g_ref, cw_ref, z_ref, carry):
        @pl.when(pl.program_id(1) == 0)
        def _():
            carry[...] = jnp.zeros_like(carry)

        a_pre = a_ref[0].astype(F32)
        row = lax.broadcasted_iota(jnp.int32, a_pre.shape, 0)
        a = _conv3(a_pre, carry[...], cw_ref[...], row)
        z_ref[...] = (a * jax.nn.sigmoid(a) * g_ref[0].astype(F32)).astype(BF)
        carry[...] = a_pre[ts - HALO:ts, :]

    part = lambda p: pl.BlockSpec((1, ts, tc), lambda c, s: (p, s, c))
    return pl.pallas_call(
        body, name=name, grid=(C // tc, S // ts),
        in_specs=[part(0), part(1), pl.BlockSpec((SUBLANES, tc), lambda c, s: (0, c))],
        out_specs=pl.BlockSpec((ts, tc), lambda c, s: (s, c)),
        out_shape=jax.ShapeDtypeStruct((S, C), BF),
        scratch_shapes=[pltpu.VMEM((HALO, tc), F32)],
        compiler_params=_params("parallel", "arbitrary"),
    )(up, up, _pad_conv(cw))


def _halo_specs(p, ts, tc, n_s):
    per = ts // HALO
    last = n_s * per - 1
    before = pl.BlockSpec((1, HALO, tc), lambda c, s: (p, jnp.maximum(s * per - 1, 0), c))
    after = pl.BlockSpec((1, HALO, tc), lambda c, s: (p, jnp.minimum((s + 1) * per, last), c))
    return before, after


def _mixer_mid_bwd(proj, dz, cw, name):
    _, S, C = proj.shape
    ts, tc = _tile(S, 512, SUBLANES), _tile(C, 1024)
    n_s = S // ts

    def body(b_ref, c_ref, h_ref, dz_ref, cp_ref, hp_ref, bn_ref, dzn_ref, cw_ref, d_ref, dcw_ref):
        s = pl.program_id(1)
        w = cw_ref[...]
        b, c, h = b_ref[0].astype(F32), c_ref[0].astype(F32), h_ref[0].astype(F32)
        dz_t = dz_ref[0]
        row = lax.broadcasted_iota(jnp.int32, b.shape, 0)
        u = c * h
        prev = jnp.where(s > 0, cp_ref[0].astype(F32) * hp_ref[0].astype(F32), 0.0)
        u1, u2 = _shift_down(u, prev, 1, row), _shift_down(u, prev, 2, row)
        cv = u2 * w[0:1] + u1 * w[1:2] + u * w[2:3]
        dcv = dz_t * b
        nxt = jnp.where(s < n_s - 1, dzn_ref[0] * bn_ref[0].astype(F32), 0.0)
        du = _conv3_t(dcv, nxt, w, row)
        d_ref[0] = (dz_t * cv).astype(BF)
        d_ref[1] = (du * h).astype(BF)
        d_ref[2] = (du * c).astype(BF)
        part = _tap_rows(jnp.sum(dcv * u2, axis=0, keepdims=True), jnp.sum(dcv * u1, axis=0, keepdims=True),
                         jnp.sum(dcv * u, axis=0, keepdims=True))

        @pl.when(s == 0)
        def _():
            dcw_ref[...] = part

        @pl.when(s > 0)
        def _():
            dcw_ref[...] += part

    part_spec = lambda p: pl.BlockSpec((1, ts, tc), lambda c, s: (p, s, c))
    c_before, _ = _halo_specs(1, ts, tc, n_s)
    h_before, _ = _halo_specs(2, ts, tc, n_s)
    _, b_after = _halo_specs(0, ts, tc, n_s)
    _, dz_after = _halo_specs(0, ts, tc, n_s)
    dproj, dcw = pl.pallas_call(
        body, name=name, grid=(C // tc, n_s),
        in_specs=[part_spec(0), part_spec(1), part_spec(2), part_spec(0), c_before, h_before, b_after, dz_after,
                  pl.BlockSpec((SUBLANES, tc), lambda c, s: (0, c))],
        out_specs=[pl.BlockSpec((3, ts, tc), lambda c, s: (0, s, c)),
                   pl.BlockSpec((SUBLANES, tc), lambda c, s: (0, c))],
        out_shape=[jax.ShapeDtypeStruct((3, S, C), BF), jax.ShapeDtypeStruct((SUBLANES, C), F32)],
        compiler_params=_params("parallel", "arbitrary"),
    )(proj, proj, proj, dz[None], proj, proj, proj, dz[None], _pad_conv(cw))
    return dproj, dcw[:CONV_W]


def _ffn_mid_bwd(up, dz, cw, name):
    _, S, C = up.shape
    ts, tc = _tile(S, 512, SUBLANES), _tile(C, 1408)
    n_s = S // ts

    def dact(a, g, dz_t):
        sg = jax.nn.sigmoid(a)
        return dz_t * g * (sg * (1.0 + a * (1.0 - sg))), dz_t * (a * sg)

    def body(a_ref, g_ref, dz_ref, ap_ref, an_ref, gn_ref, dzn_ref, cw_ref, d_ref, dcw_ref):
        s = pl.program_id(1)
        w = cw_ref[...]
        a_pre, g = a_ref[0].astype(F32), g_ref[0].astype(F32)
        row = lax.broadcasted_iota(jnp.int32, a_pre.shape, 0)
        prev = jnp.where(s > 0, ap_ref[0].astype(F32), 0.0)
        a1, a2 = _shift_down(a_pre, prev, 1, row), _shift_down(a_pre, prev, 2, row)
        a = a2 * w[0:1] + a1 * w[1:2] + a_pre * w[2:3]
        da, dg = dact(a, g, dz_ref[0])
        an_pre = an_ref[0].astype(F32)
        row8 = lax.broadcasted_iota(jnp.int32, an_pre.shape, 0)
        a_next = _conv3(an_pre, a_pre[ts - HALO:ts, :], w, row8)
        da_next, _ = dact(a_next, gn_ref[0].astype(F32), dzn_ref[0])
        nxt = jnp.where(s < n_s - 1, da_next, 0.0)
        d_ref[0] = _conv3_t(da, nxt, w, row).astype(BF)
        d_ref[1] = dg.astype(BF)
        part = _tap_rows(jnp.sum(da * a2, axis=0, keepdims=True), jnp.sum(da * a1, axis=0, keepdims=True),
                         jnp.sum(da * a_pre, axis=0, keepdims=True))

        @pl.when(s == 0)
        def _():
            dcw_ref[...] = part

        @pl.when(s > 0)
        def _():
            dcw_ref[...] += part

    part_spec = lambda p: pl.BlockSpec((1, ts, tc), lambda c, s: (p, s, c))
    a_before, a_after = _halo_specs(0, ts, tc, n_s)
    _, g_after = _halo_specs(1, ts, tc, n_s)
    _, dz_after = _halo_specs(0, ts, tc, n_s)
    dup, dcw = pl.pallas_call(
        body, name=name, grid=(C // tc, n_s),
        in_specs=[part_spec(0), part_spec(1), part_spec(0), a_before, a_after, g_after, dz_after,
                  pl.BlockSpec((SUBLANES, tc), lambda c, s: (0, c))],
        out_specs=[pl.BlockSpec((2, ts, tc), lambda c, s: (0, s, c)),
                   pl.BlockSpec((SUBLANES, tc), lambda c, s: (0, c))],
        out_shape=[jax.ShapeDtypeStruct((2, S, C), BF), jax.ShapeDtypeStruct((SUBLANES, C), F32)],
        compiler_params=_params("parallel", "arbitrary"),
    )(up, up, dz[None], up, up, up, dz[None], _pad_conv(cw))
    return dup, dcw[:CONV_W]


def _matmul_residual(z, w, x, name):
    S, K = z.shape
    D = w.shape[1]
    ts = _tile(S, 512, SUBLANES)

    def body(z_ref, w_ref, x_ref, o_ref):
        o_ref[...] = x_ref[...] + jnp.dot(z_ref[...], w_ref[...], preferred_element_type=F32)

    return pl.pallas_call(
        body, name=name, grid=(S // ts,),
        in_specs=[pl.BlockSpec((ts, K), lambda s: (s, 0)), pl.BlockSpec((K, D), lambda s: (0, 0)),
                  pl.BlockSpec((ts, D), lambda s: (s, 0))],
        out_specs=pl.BlockSpec((ts, D), lambda s: (s, 0)),
        out_shape=jax.ShapeDtypeStruct((S, D), F32),
        compiler_params=_params("parallel"),
    )(z, w, x)


def _matmul_nt(a, w, name):
    S, K = a.shape
    N = w.shape[0]
    ts, tn = _tile(S, 512, SUBLANES), _tile(N, 1408)

    def body(a_ref, w_ref, o_ref, abf):
        @pl.when(pl.program_id(1) == 0)
        def _():
            abf[...] = a_ref[...].astype(BF)

        o_ref[...] = lax.dot_general(abf[...], w_ref[...], NT_DIMS, preferred_element_type=F32)

    return pl.pallas_call(
        body, name=name, grid=(S // ts, N // tn),
        in_specs=[pl.BlockSpec((ts, K), lambda s, n: (s, 0)), pl.BlockSpec((tn, K), lambda s, n: (n, 0))],
        out_specs=pl.BlockSpec((ts, tn), lambda s, n: (s, n)),
        out_shape=jax.ShapeDtypeStruct((S, N), F32),
        scratch_shapes=[pltpu.VMEM((ts, K), BF)],
        compiler_params=_params("parallel", "arbitrary"),
    )(a, w)


def _wgrad(a, b, name):
    S, M = a.shape
    P, _, C = b.shape
    tm, tn, tk = _tile(M, 1408), _tile(C, 1408), _tile(S, 512, SUBLANES)
    nnc = C // tn

    def body(a_ref, b_ref, o_ref):
        @pl.when(pl.program_id(2) == 0)
        def _():
            o_ref[...] = jnp.zeros_like(o_ref)

        o_ref[...] += lax.dot_general(a_ref[...], b_ref[0].astype(BF), TN_DIMS, preferred_element_type=F32)

    return pl.pallas_call(
        body, name=name, grid=(M // tm, P * nnc, S // tk),
        in_specs=[pl.BlockSpec((tk, tm), lambda m, n, k: (k, m)),
                  pl.BlockSpec((1, tk, tn), lambda m, n, k: (n // nnc, k, n % nnc))],
        out_specs=pl.BlockSpec((tm, tn), lambda m, n, k: (m, n)),
        out_shape=jax.ShapeDtypeStruct((M, P * C), F32),
        compiler_params=_params("parallel", "parallel", "arbitrary"),
    )(a, b)


def _dnorm(dp, w, x, g, dy, name):
    P, S, C = dp.shape
    D = x.shape[1]
    ts, tk = _tile(S, 512, SUBLANES), _tile(C, 1408)
    nkc = C // tk
    n_k = P * nkc

    def body(dp_ref, w_ref, x_ref, g_ref, dy_ref, dx_ref, dg_ref, acc):
        s, k = pl.program_id(0), pl.program_id(1)

        @pl.when(k == 0)
        def _():
            acc[...] = jnp.zeros_like(acc)

        acc[...] += lax.dot_general(dp_ref[0], w_ref[...], NT_DIMS, preferred_element_type=F32)

        @pl.when((s == 0) & (k == 0))
        def _():
            dg_ref[...] = jnp.zeros_like(dg_ref)

        @pl.when(k == n_k - 1)
        def _():
            xf = x_ref[...]
            r = lax.rsqrt(jnp.mean(xf * xf, axis=-1, keepdims=True) + RMS_EPS)
            xhat = xf * r
            dxn = acc[...]
            dxhat = dxn * g_ref[...]
            dx_ref[...] = dy_ref[...] + r * (dxhat - xhat * jnp.mean(dxhat * xhat, axis=-1, keepdims=True))
            dg_ref[...] += jnp.broadcast_to(jnp.sum(dxn * xhat, axis=0, keepdims=True), dg_ref.shape)

    dx, dg = pl.pallas_call(
        body, name=name, grid=(S // ts, n_k),
        in_specs=[pl.BlockSpec((1, ts, tk), lambda s, k: (k // nkc, s, k % nkc)),
                  pl.BlockSpec((D, tk), lambda s, k: (0, k)),
                  pl.BlockSpec((ts, D), lambda s, k: (s, 0)),
                  pl.BlockSpec((1, D), lambda s, k: (0, 0)),
                  pl.BlockSpec((ts, D), lambda s, k: (s, 0))],
        out_specs=[pl.BlockSpec((ts, D), lambda s, k: (s, 0)),
                   pl.BlockSpec((SUBLANES, D), lambda s, k: (0, 0))],
        out_shape=[jax.ShapeDtypeStruct((S, D), F32), jax.ShapeDtypeStruct((SUBLANES, D), F32)],
        scratch_shapes=[pltpu.VMEM((ts, D), F32)],
        compiler_params=_params("arbitrary", "arbitrary"),
    )(dp, w, x, g.reshape(1, D), dy)
    return dx, dg[0]


def _head_masks(shape, hd):
    lane = lax.broadcasted_iota(jnp.int32, shape, 1)
    return lane < hd


def _pair_sum(v, lo):
    s0 = jnp.sum(jnp.where(lo, v, 0.0), axis=-1, keepdims=True)
    s1 = jnp.sum(jnp.where(lo, 0.0, v), axis=-1, keepdims=True)
    return jnp.where(lo, s0, s1)


def _headnorm(src, part, colblk, w, scale, D, name):
    S = src.shape[1]
    hd = w.shape[0]
    ts = _tile(S, 512, SUBLANES)
    w2 = jnp.tile(w, LANES // hd).reshape(1, LANES)

    def body(x_ref, w_ref, o_ref):
        lo = _head_masks((ts, LANES), hd)
        for t in range(D // LANES):
            xt = x_ref[0, :, t * LANES:(t + 1) * LANES]
            r = lax.rsqrt(_pair_sum(xt * xt, lo) * (1.0 / hd) + RMS_EPS)
            o_ref[:, t * LANES:(t + 1) * LANES] = (xt * r * w_ref[...] * scale).astype(BF)

    return pl.pallas_call(
        body, name=name, grid=(S // ts,),
        in_specs=[pl.BlockSpec((1, ts, D), lambda s: (part, s, colblk)), pl.BlockSpec((1, LANES), lambda s: (0, 0))],
        out_specs=pl.BlockSpec((ts, D), lambda s: (s, 0)),
        out_shape=jax.ShapeDtypeStruct((S, D), BF),
        compiler_params=_params("parallel"),
    )(src, w2)


def _headnorm_bwd(src, part, colblk, w, dys, D, name):
    S = src.shape[1]
    hd = w.shape[0]
    ts = _tile(S, 512, SUBLANES)
    w2 = jnp.tile(w, LANES // hd).reshape(1, LANES)
    n_dy = len(dys)

    def body(x_ref, w_ref, *rest):
        dy_refs, dx_ref, dw_ref = rest[:n_dy], rest[n_dy], rest[n_dy + 1]

        @pl.when(pl.program_id(0) == 0)
        def _():
            dw_ref[...] = jnp.zeros_like(dw_ref)

        lo = _head_masks((ts, LANES), hd)
        for t in range(D // LANES):
            cols = slice(t * LANES, (t + 1) * LANES)
            xt = x_ref[0, :, cols]
            dy = dy_refs[0][:, cols]
            for other in dy_refs[1:]:
                dy = dy + other[:, cols]
            r = lax.rsqrt(_pair_sum(xt * xt, lo) * (1.0 / hd) + RMS_EPS)
            xhat = xt * r
            dxhat = dy * w_ref[...]
            mean = _pair_sum(dxhat * xhat, lo) * (1.0 / hd)
            dx_ref[:, cols] = (r * (dxhat - xhat * mean)).astype(BF)
            dw_ref[:, cols] += jnp.broadcast_to(jnp.sum(dy * xhat, axis=0, keepdims=True), (SUBLANES, LANES))

    dx, dw = pl.pallas_call(
        body, name=name, grid=(S // ts,),
        in_specs=[pl.BlockSpec((1, ts, D), lambda s: (part, s, colblk)), pl.BlockSpec((1, LANES), lambda s: (0, 0))]
        + [pl.BlockSpec((ts, D), lambda s: (s, 0))] * n_dy,
        out_specs=[pl.BlockSpec((ts, D), lambda s: (s, 0)), pl.BlockSpec((SUBLANES, D), lambda s: (0, 0))],
        out_shape=[jax.ShapeDtypeStruct((S, D), BF), jax.ShapeDtypeStruct((SUBLANES, D), F32)],
        compiler_params=_params("arbitrary"),
    )(src, w2, *dys)
    return dx, jnp.sum(dw[0].reshape(D // hd, hd), axis=0)


def _tri(n, lower):
    r, c = lax.broadcasted_iota(jnp.int32, (n, n), 0), lax.broadcasted_iota(jnp.int32, (n, n), 1)
    return jnp.where((c <= r) if lower else (c >= r), 1.0, 0.0).astype(BF)


def _dot_exact(t, v):
    hi = v.astype(BF)
    r1 = v - hi.astype(F32)
    mid = r1.astype(BF)
    lo = (r1 - mid.astype(F32)).astype(BF)
    dot = lambda u: jnp.dot(t, u, preferred_element_type=F32)
    return dot(hi) + dot(mid) + dot(lo)


def _gate_fwd(kvf, b_pad, colblk, name):
    S = kvf.shape[1]
    ts = _tile(S, 512, SUBLANES)

    def body(f_ref, b_ref, c_ref, carry):
        @pl.when(pl.program_id(0) == 0)
        def _():
            carry[...] = jnp.zeros_like(carry)

        f = f_ref[0] + b_ref[...]
        ls = jnp.minimum(f, 0.0) - jnp.log1p(jnp.exp(-jnp.abs(f)))
        tri = _tri(ts, lower=True)
        c = _dot_exact(tri, ls) + carry[0:1, :]
        c_ref[...] = c
        carry[...] = jnp.broadcast_to(c[ts - 1:ts, :], carry.shape)

    return pl.pallas_call(
        body, name=name, grid=(S // ts,),
        in_specs=[pl.BlockSpec((1, ts, LANES), lambda s: (0, s, colblk)), pl.BlockSpec((1, LANES), lambda s: (0, 0))],
        out_specs=pl.BlockSpec((ts, LANES), lambda s: (s, 0)),
        out_shape=jax.ShapeDtypeStruct((S, LANES), F32),
        scratch_shapes=[pltpu.VMEM((SUBLANES, LANES), F32)],
        compiler_params=_params("arbitrary"),
    )(kvf, b_pad)


def _gate_bwd(dc, kvf, b_pad, colblk, name):
    S = kvf.shape[1]
    ts = _tile(S, 512, SUBLANES)
    n_s = S // ts

    def body(dc_ref, f_ref, b_ref, df_ref, db_ref, carry):
        @pl.when(pl.program_id(0) == 0)
        def _():
            carry[...] = jnp.zeros_like(carry)
            db_ref[...] = jnp.zeros_like(db_ref)

        tri = _tri(ts, lower=False)
        dls = _dot_exact(tri, dc_ref[...]) + carry[0:1, :]
        f = f_ref[0] + b_ref[...]
        df = dls * jax.nn.sigmoid(-f)
        df_ref[...] = df
        db_ref[...] += jnp.broadcast_to(jnp.sum(df, axis=0, keepdims=True), db_ref.shape)
        carry[...] = jnp.broadcast_to(dls[0:1, :], carry.shape)

    df, db = pl.pallas_call(
        body, name=name, grid=(n_s,),
        in_specs=[pl.BlockSpec((ts, LANES), lambda s: (n_s - 1 - s, 0)),
                  pl.BlockSpec((1, ts, LANES), lambda s: (0, n_s - 1 - s, colblk)),
                  pl.BlockSpec((1, LANES), lambda s: (0, 0))],
        out_specs=[pl.BlockSpec((ts, LANES), lambda s: (n_s - 1 - s, 0)),
                   pl.BlockSpec((SUBLANES, LANES), lambda s: (0, 0))],
        out_shape=[jax.ShapeDtypeStruct((S, LANES), F32), jax.ShapeDtypeStruct((SUBLANES, LANES), F32)],
        scratch_shapes=[pltpu.VMEM((SUBLANES, LANES), F32)],
        compiler_params=_params("arbitrary"),
    )(dc, kvf, b_pad)
    return df, db[0]


def _attn_tile(S):
    return _tile(S, 512, LANES)


def _split_heads(v, lo):
    zero = jnp.zeros_like(v)
    return jnp.where(lo, v, zero), jnp.where(lo, zero, v)


def _augment(base, c, mode, hd, name):
    S, D = base.shape
    ts = _tile(S, 512, 2 * SUBLANES)

    def body(b_ref, c_ref, o0_ref, o1_ref):
        lane = lax.broadcasted_iota(jnp.int32, (ts, LANES), 1)
        cc = c_ref[...] * LOG2E
        for t in range(D // LANES):
            cols = slice(t * LANES, (t + 1) * LANES)
            bt = b_ref[:, cols]
            for h, o_ref in ((0, o0_ref), (1, o1_ref)):
                first = hd if h == 0 else 0
                keep = (lane < hd) if h == 0 else (lane >= hd)
                if mode == "v":
                    vals = (1.0,)
                else:
                    col = cc[:, 2 * t + h:2 * t + h + 1]
                    hi = col.astype(BF).astype(F32)
                    mid = (col - hi).astype(BF).astype(F32)
                    pieces = (hi, mid, col - hi - mid)
                    vals = pieces + (1.0, 1.0, 1.0) if mode == "q" else (1.0, 1.0, 1.0) + tuple(-v for v in pieces)
                aug = jnp.zeros((ts, LANES), F32)
                for i, v in enumerate(vals):
                    aug = jnp.where(lane == first + i, v, aug)
                o_ref[:, cols] = jnp.where(keep, bt, aug.astype(BF))

    spec = pl.BlockSpec((ts, D), lambda s: (s, 0))
    return pl.pallas_call(
        body, name=name, grid=(S // ts,),
        in_specs=[spec, pl.BlockSpec((ts, LANES), lambda s: (s, 0))], out_specs=[spec, spec],
        out_shape=[jax.ShapeDtypeStruct((S, D), BF)] * 2,
        compiler_params=_params("parallel"),
    )(base, c)


def _attn_fwd(qa, ka, va, qg, hd, name):
    S, D = qa[0].shape
    P = D // LANES
    tq = _attn_tile(S)
    nq = S // tq

    def body(q0_ref, q1_ref, k0_ref, k1_ref, v0_ref, v1_ref, g_ref, o_ref, og_ref, m_ref, l_ref):
        qi = pl.program_id(1)
        lo = _head_masks((tq, LANES), hd)
        qh = (q0_ref[...], q1_ref[...])
        k_refs, v_refs = (k0_ref, k1_ref), (v0_ref, v1_ref)
        causal = lax.broadcasted_iota(jnp.int32, (tq, tq), 1) <= lax.broadcasted_iota(jnp.int32, (tq, tq), 0)

        def block(ki, carry, masked):
            off = pl.multiple_of(ki * tq, tq)
            out = []
            for h in range(2):
                m, acc = carry[h]
                s = lax.dot_general(qh[h], k_refs[h][pl.ds(off, tq), :], NT_DIMS, preferred_element_type=F32)
                if masked:
                    s = jnp.where(causal, s, -jnp.inf)
                m_new = jnp.maximum(m, jnp.ceil(jnp.max(s, axis=-1, keepdims=True)))
                p = jnp.exp2(s - m_new)
                acc = jnp.exp2(m - m_new) * acc + jnp.dot(p.astype(BF), v_refs[h][pl.ds(off, tq), :],
                                                          preferred_element_type=F32)
                out.append((m_new, acc))
            return tuple(out)

        init = tuple((jnp.full((tq, 1), -jnp.inf, F32), jnp.zeros((tq, LANES), F32)) for _ in range(2))
        carry = lax.fori_loop(0, qi, lambda ki, c: block(ki, c, False), init)
        (m0, a0), (m1, a1) = block(qi, carry, True)
        l0, l1 = a0[:, hd:hd + 1], a1[:, 0:1]
        o = jnp.where(lo, a0 / l0, a1 / l1)
        o_ref[...] = o
        og_ref[...] = (o * jax.nn.sigmoid(g_ref[0])).astype(BF)
        lane2 = lax.broadcasted_iota(jnp.int32, (tq, 2), 1)
        m_ref[0] = jnp.where(lane2 == 0, m0, m1)
        l_ref[0] = jnp.where(lane2 == 0, l0, l1)

    tile = pl.BlockSpec((tq, LANES), lambda p, i: (i, p))
    whole = pl.BlockSpec((S, LANES), lambda p, i: (0, p))
    stat = pl.BlockSpec((1, tq, 2), lambda p, i: (p, i, 0))
    return pl.pallas_call(
        body, name=name, grid=(P, nq),
        in_specs=[tile, tile, whole, whole, whole, whole, pl.BlockSpec((1, tq, LANES), lambda p, i: (1, i, p))],
        out_specs=[tile, tile, stat, stat],
        out_shape=[jax.ShapeDtypeStruct((S, D), F32), jax.ShapeDtypeStruct((S, D), BF),
                   jax.ShapeDtypeStruct((P, S, 2), F32), jax.ShapeDtypeStruct((P, S, 2), F32)],
        compiler_params=_params("parallel", "arbitrary"),
    )(*qa, *ka, *va, qg)


def _attn_out_bwd(dog, o, qg, l, hd, name):
    S, D = o.shape
    P = D // LANES
    ts = _tile(S, 512, 2 * SUBLANES)

    def body(dog_ref, o_ref, g_ref, l_ref, do_ref, dg_ref, e_ref):
        lo = _head_masks((ts, LANES), hd)
        lane2 = lax.broadcasted_iota(jnp.int32, (ts, 2), 1)
        for t in range(P):
            cols = slice(t * LANES, (t + 1) * LANES)
            sg = jax.nn.sigmoid(g_ref[0, :, cols])
            dog_t, o_t, l_t = dog_ref[:, cols], o_ref[:, cols], l_ref[t]
            g = (dog_t * sg / jnp.where(lo, l_t[:, 0:1], l_t[:, 1:2])).astype(BF)
            do_ref[:, cols] = g
            dg_ref[:, cols] = (dog_t * o_t * sg * (1.0 - sg)).astype(BF)
            prod = g.astype(F32) * o_t
            e0 = jnp.sum(jnp.where(lo, prod, 0.0), axis=-1, keepdims=True)
            e1 = jnp.sum(jnp.where(lo, 0.0, prod), axis=-1, keepdims=True)
            e_ref[t] = jnp.where(lane2 == 0, e0, e1)

    rows = pl.BlockSpec((ts, D), lambda s: (s, 0))
    stat = pl.BlockSpec((P, ts, 2), lambda s: (0, s, 0))
    return pl.pallas_call(
        body, name=name, grid=(S // ts,),
        in_specs=[rows, rows, pl.BlockSpec((1, ts, D), lambda s: (1, s, 0)), stat],
        out_specs=[rows, rows, stat],
        out_shape=[jax.ShapeDtypeStruct((S, D), BF), jax.ShapeDtypeStruct((S, D), BF),
                   jax.ShapeDtypeStruct((P, S, 2), F32)],
        compiler_params=_params("parallel"),
    )(dog, o, qg, l)


def _attn_bwd(qa, ka, vb, g, m_row, e_row, hd, name):
    S, D = vb.shape
    P = D // LANES
    tk = _attn_tile(S)
    nk = S // tk
    scale = hd ** -0.5

    def body(q0_ref, q1_ref, g_ref, k0_ref, k1_ref, v_ref, m_ref, e_ref, dq_ref, dk_ref, dv_ref, dc_ref):
        ki = pl.program_id(1)

        @pl.when(ki == 0)
        def _():
            dq_ref[...] = jnp.zeros_like(dq_ref)

        lo = _head_masks((tk, LANES), hd)
        kh = (k0_ref[...], k1_ref[...])
        q_refs = (q0_ref, q1_ref)
        vh = _split_heads(v_ref[...], lo)
        causal_t = lax.broadcasted_iota(jnp.int32, (tk, tk), 0) <= lax.broadcasted_iota(jnp.int32, (tk, tk), 1)

        def block(qi, carry, masked):
            off = pl.multiple_of(qi * tk, tk)
            gb = g_ref[pl.ds(off, tk), :]
            m_t, e_t = m_ref[0, qi], e_ref[0, qi]
            out, dq_parts = [], []
            for h in range(2):
                dk, dv, dc = carry[h]
                qb = q_refs[h][pl.ds(off, tk), :]
                st = lax.dot_general(kh[h], qb, NT_DIMS, preferred_element_type=F32)
                pt = jnp.exp2(st - m_t[h:h + 1, :])
                if masked:
                    pt = jnp.where(causal_t, pt, 0.0)
                pb = pt.astype(BF)
                dv = dv + jnp.dot(pb, gb, preferred_element_type=F32)
                dpt = lax.dot_general(vh[h], gb, NT_DIMS, preferred_element_type=F32)
                dst = pb.astype(F32) * (dpt - e_t[h:h + 1, :])
                db = dst.astype(BF)
                dk = dk + jnp.dot(db, qb, preferred_element_type=F32)
                dc = dc - jnp.sum(dst, axis=-1, keepdims=True)
                dq_parts.append(lax.dot_general(db, kh[h], TN_DIMS, preferred_element_type=F32))
                out.append((dk, dv, dc))
            dq_ref[pl.ds(off, tk), :] += jnp.where(lo, dq_parts[0], dq_parts[1]) * scale
            return tuple(out)

        init = tuple((jnp.zeros((tk, LANES), F32), jnp.zeros((tk, LANES), F32), jnp.zeros((tk, 1), F32))
                     for _ in range(2))
        carry = block(ki, init, True)
        (dk0, dv0, dc0), (dk1, dv1, dc1) = lax.fori_loop(ki + 1, nk, lambda qi, c: block(qi, c, False), carry)
        dk_ref[...] = jnp.where(lo, dk0, dk1) * (1.0 / LOG2E)
        dv_ref[...] = jnp.where(lo, dv0, dv1)
        lane2 = lax.broadcasted_iota(jnp.int32, (tk, 2), 1)
        dc_ref[0] = jnp.where(lane2 == 0, dc0, dc1)

    tile = pl.BlockSpec((tk, LANES), lambda p, i: (i, p))
    whole = pl.BlockSpec((S, LANES), lambda p, i: (0, p))
    row_spec = pl.BlockSpec((1, nk, 2, tk), lambda p, i: (p, 0, 0, 0))
    return pl.pallas_call(
        body, name=name, grid=(P, nk),
        in_specs=[whole, whole, whole, tile, tile, tile, row_spec, row_spec],
        out_specs=[whole, tile, tile, pl.BlockSpec((1, tk, 2), lambda p, i: (p, i, 0))],
        out_shape=[jax.ShapeDtypeStruct((S, D), F32), jax.ShapeDtypeStruct((S, D), F32),
                   jax.ShapeDtypeStruct((S, D), F32), jax.ShapeDtypeStruct((P, S, 2), F32)],
        compiler_params=_params("parallel", "arbitrary"),
    )(*qa, g, *ka, vb, m_row, e_row)


def _attn_dq(qa, ka, vb, do, lse, dvec, hd, name):
    S, D = vb.shape
    P = D // LANES
    tq = _attn_tile(S)
    nq = S // tq
    scale = hd ** -0.5

    def body(q0_ref, q1_ref, k0_ref, k1_ref, v_ref, do_ref, lse_ref, dv_ref, dq_ref, dv2_ref):
        qi = pl.program_id(1)
        lo = _head_masks((tq, LANES), hd)
        qh = (q0_ref[...], q1_ref[...])
        k_refs = (k0_ref, k1_ref)
        doh = _split_heads(do_ref[...], lo)
        lse_t, dv_t = lse_ref[0], dv_ref[0]
        causal = lax.broadcasted_iota(jnp.int32, (tq, tq), 1) <= lax.broadcasted_iota(jnp.int32, (tq, tq), 0)

        def block(ki, carry, masked):
            off = pl.multiple_of(ki * tq, tq)
            vt = v_ref[pl.ds(off, tq), :]
            out = []
            for h in range(2):
                kb = k_refs[h][pl.ds(off, tq), :]
                s = lax.dot_general(qh[h], kb, NT_DIMS, preferred_element_type=F32)
                p = jnp.exp2(s - lse_t[:, h:h + 1])
                if masked:
                    p = jnp.where(causal, p, 0.0)
                dp = lax.dot_general(doh[h], vt, NT_DIMS, preferred_element_type=F32)
                ds = p * (dp - dv_t[:, h:h + 1])
                acc, rs = carry[h]
                out.append((acc + jnp.dot(ds.astype(BF), kb, preferred_element_type=F32),
                            rs + jnp.sum(ds, axis=-1, keepdims=True)))
            return tuple(out)

        init = tuple((jnp.zeros((tq, LANES), F32), jnp.zeros((tq, 1), F32)) for _ in range(2))
        carry = lax.fori_loop(0, qi, lambda ki, c: block(ki, c, False), init)
        (d0, r0), (d1, r1) = block(qi, carry, True)
        dq_ref[...] = jnp.where(lo, d0, d1) * scale
        lane2 = lax.broadcasted_iota(jnp.int32, (tq, 2), 1)
        dv2_ref[0] = dv_t + jnp.where(lane2 == 0, r0, r1)

    tile = pl.BlockSpec((tq, LANES), lambda p, i: (i, p))
    whole = pl.BlockSpec((S, LANES), lambda p, i: (0, p))
    stat = pl.BlockSpec((1, tq, 2), lambda p, i: (p, i, 0))
    return pl.pallas_call(
        body, name=name, grid=(P, nq),
        in_specs=[tile, tile, whole, whole, whole, tile, stat, stat],
        out_specs=[tile, stat],
        out_shape=[jax.ShapeDtypeStruct((S, D), F32), jax.ShapeDtypeStruct((P, S, 2), F32)],
        compiler_params=_params("parallel", "arbitrary"),
    )(*qa, *ka, vb, do, lse, dvec)


def _attn_dkv(qa, ka, vb, do, lse_row, dvec_row, hd, name):
    S, D = vb.shape
    P = D // LANES
    tk = _attn_tile(S)
    nk = S // tk

    def body(q0_ref, q1_ref, do_ref, k0_ref, k1_ref, v_ref, lse_ref, dvr_ref, dk_ref, dv_ref, dc_ref):
        ki = pl.program_id(1)
        lo = _head_masks((tk, LANES), hd)
        kh = (k0_ref[...], k1_ref[...])
        q_refs = (q0_ref, q1_ref)
        vh = _split_heads(v_ref[...], lo)
        causal_t = lax.broadcasted_iota(jnp.int32, (tk, tk), 0) <= lax.broadcasted_iota(jnp.int32, (tk, tk), 1)

        def block(qi, carry, masked):
            off = pl.multiple_of(qi * tk, tk)
            dob = do_ref[pl.ds(off, tk), :]
            lse_t, dv_t = lse_ref[0, qi], dvr_ref[0, qi]
            out = []
            for h in range(2):
                dk, dv, dc = carry[h]
                qb = q_refs[h][pl.ds(off, tk), :]
                st = lax.dot_general(kh[h], qb, NT_DIMS, preferred_element_type=F32)
                pt = jnp.exp2(st - lse_t[h:h + 1, :])
                if masked:
                    pt = jnp.where(causal_t, pt, 0.0)
                dv = dv + jnp.dot(pt.astype(BF), dob, preferred_element_type=F32)
                dpt = lax.dot_general(vh[h], dob, NT_DIMS, preferred_element_type=F32)
                dst = pt * (dpt - dv_t[h:h + 1, :])
                dk = dk + jnp.dot(dst.astype(BF), qb, preferred_element_type=F32)
                dc = dc - jnp.sum(dst, axis=-1, keepdims=True)
                out.append((dk, dv, dc))
            return tuple(out)

        init = tuple((jnp.zeros((tk, LANES), F32), jnp.zeros((tk, LANES), F32), jnp.zeros((tk, 1), F32))
                     for _ in range(2))
        carry = block(ki, init, True)
        (dk0, dv0, dc0), (dk1, dv1, dc1) = lax.fori_loop(ki + 1, nk, lambda qi, c: block(qi, c, False), carry)
        dk_ref[...] = jnp.where(lo, dk0, dk1) * (1.0 / LOG2E)
        dv_ref[...] = jnp.where(lo, dv0, dv1)
        lane2 = lax.broadcasted_iota(jnp.int32, (tk, 2), 1)
        dc_ref[0] = jnp.where(lane2 == 0, dc0, dc1)

    tile = pl.BlockSpec((tk, LANES), lambda p, i: (i, p))
    whole = pl.BlockSpec((S, LANES), lambda p, i: (0, p))
    row_spec = pl.BlockSpec((1, nk, 2, tk), lambda p, i: (p, 0, 0, 0))
    return pl.pallas_call(
        body, name=name, grid=(P, nk),
        in_specs=[whole, whole, whole, tile, tile, tile, row_spec, row_spec],
        out_specs=[tile, tile, pl.BlockSpec((1, tk, 2), lambda p, i: (p, i, 0))],
        out_shape=[jax.ShapeDtypeStruct((S, D), F32), jax.ShapeDtypeStruct((S, D), F32),
                   jax.ShapeDtypeStruct((P, S, 2), F32)],
        compiler_params=_params("parallel", "arbitrary"),
    )(*qa, do, *ka, vb, lse_row, dvec_row)


def _loss_head(y, t, name):
    S, D = y.shape
    ts = _tile(S, 512, SUBLANES)

    def body(y_ref, t_ref, dy_ref, l_ref):
        @pl.when(pl.program_id(0) == 0)
        def _():
            l_ref[...] = jnp.zeros_like(l_ref)

        e = y_ref[...] - t_ref[...]
        dy_ref[...] = e * (1.0 / D)
        part = 0.5 * jnp.sum(jnp.mean(e * e, axis=-1, keepdims=True), axis=0, keepdims=True)
        l_ref[...] += jnp.broadcast_to(part, l_ref.shape)

    return pl.pallas_call(
        body, name=name, grid=(S // ts,),
        in_specs=[pl.BlockSpec((ts, D), lambda s: (s, 0)), pl.BlockSpec((ts, D), lambda s: (s, 0))],
        out_specs=[pl.BlockSpec((ts, D), lambda s: (s, 0)), pl.BlockSpec((SUBLANES, LANES), lambda s: (0, 0))],
        out_shape=[jax.ShapeDtypeStruct((S, D), F32), jax.ShapeDtypeStruct((SUBLANES, LANES), F32)],
        compiler_params=_params("arbitrary"),
    )(y, t)


def _adamw(w, g, m, v, name):
    shape = w.shape
    cols = shape[-1]
    as2d = lambda a: a.reshape(-1, cols)
    rows = as2d(w).shape[0]
    tr = _tile(rows, 256, SUBLANES) if rows % SUBLANES == 0 else rows
    c1 = 1.0 - ADAM_B1 ** ADAM_STEP
    c2 = 1.0 - ADAM_B2 ** ADAM_STEP

    def body(w_ref, g_ref, m_ref, v_ref, d_ref, nm_ref, nv_ref):
        gg = g_ref[...]
        nm = ADAM_B1 * m_ref[...] + (1.0 - ADAM_B1) * gg
        nv = ADAM_B2 * v_ref[...] + (1.0 - ADAM_B2) * (gg * gg)
        d_ref[...] = -ADAM_LR * ((nm / c1) / (jnp.sqrt(nv / c2) + ADAM_EPS) + ADAM_WD * w_ref[...])
        nm_ref[...] = nm
        nv_ref[...] = nv

    spec = pl.BlockSpec((tr, cols), lambda r: (r, 0))
    outs = pl.pallas_call(
        body, name=name, grid=(rows // tr,), in_specs=[spec] * 4, out_specs=[spec] * 3,
        out_shape=[jax.ShapeDtypeStruct((rows, cols), F32)] * 3,
        compiler_params=_params("parallel"),
    )(as2d(w), as2d(g), as2d(m), as2d(v))
    return tuple(o.reshape(shape) for o in outs)


def _place():
    return lax.axis_index("x"), lax.axis_index("y"), lax.axis_index("c")


def _other_chips(x, y):
    return [(1 - x, y), (x, 1 - y), (1 - x, 1 - y)]


def _remote(src, dst, send_sems, recv_sems, k, to):
    return pltpu.make_async_remote_copy(src_ref=src, dst_ref=dst, send_sem=send_sems.at[k], recv_sem=recv_sems.at[k],
                                        device_id=to, device_id_type=MESH)


def _allgather_weights(flat):
    R, W = flat.shape
    rh = R // 2

    def body(src, dst, send_sems, recv_sems):
        x, y, c = _place()
        me = 2 * x + y
        sib = (x, y, 1 - c)
        chips = _other_chips(x, y)
        half = pl.ds(pl.multiple_of(c * rh, 16), rh)
        other = pl.ds(pl.multiple_of((1 - c) * rh, 16), rh)
        first = [_remote(src.at[half], dst.at[me, half], send_sems, recv_sems, j, (cx, cy, c))
                 for j, (cx, cy) in enumerate(chips)]
        for cp in first:
            cp.start()
        passed = []
        for j, (cx, cy) in enumerate(chips):
            landed = dst.at[2 * cx + cy, half]
            _remote(landed, landed, send_sems, recv_sems, j, sib).wait_recv()
            cp = _remote(landed, landed, send_sems, recv_sems, 3 + j, sib)
            cp.start()
            passed.append(cp)
        for j, (cx, cy) in enumerate(chips):
            landed = dst.at[2 * cx + cy, other]
            _remote(landed, landed, send_sems, recv_sems, 3 + j, sib).wait_recv()
        for cp in first + passed:
            cp.wait_send()

    out = pl.pallas_call(
        body, name="allgather_weights", in_specs=[ANY], out_specs=ANY,
        out_shape=jax.ShapeDtypeStruct((N_CHIPS, R, W), flat.dtype),
        scratch_shapes=[pltpu.SemaphoreType.DMA((6,)), pltpu.SemaphoreType.DMA((6,))],
    )(flat)
    x, y, _ = _place()
    return lax.dynamic_update_slice_in_dim(out, flat[None], 2 * x + y, axis=0)


def _pair_exchange(g):
    n, _, rh, W = g.shape

    def body(g_ref, t_ref, send_sems, recv_sems):
        x, y, c = _place()
        cps = [_remote(g_ref.at[k, 1 - c], t_ref.at[k], send_sems, recv_sems, k, (x, y, 1 - c)) for k in range(n)]
        for cp in cps:
            cp.start()
        for cp in cps:
            cp.wait()

    return pl.pallas_call(
        body, name="grad_pair_exchange", in_specs=[ANY], out_specs=ANY,
        out_shape=jax.ShapeDtypeStruct((n, rh, W), g.dtype),
        scratch_shapes=[pltpu.SemaphoreType.DMA((n,)), pltpu.SemaphoreType.DMA((n,))],
    )(g)


def _pair_add(g, t, c):
    n, _, rh, W = g.shape
    tr = _tile(rh, 256, 2 * SUBLANES)

    def body(c_ref, g_ref, t_ref, o_ref):
        o_ref[...] = (g_ref[0] + t_ref[...]).astype(BF)

    return pl.pallas_call(
        body, name="grad_pair_add",
        grid_spec=pltpu.PrefetchScalarGridSpec(
            num_scalar_prefetch=1, grid=(n, rh // tr),
            in_specs=[pl.BlockSpec((1, 1, tr, W), lambda k, i, c_ref: (k, c_ref[0], i, 0)),
                      pl.BlockSpec((1, tr, W), lambda k, i, c_ref: (k, i, 0))],
            out_specs=pl.BlockSpec((1, tr, W), lambda k, i, c_ref: (k, i, 0))),
        out_shape=jax.ShapeDtypeStruct((n, rh, W), BF),
        compiler_params=_params("parallel", "parallel"),
    )(c.reshape(1).astype(jnp.int32), g, t)


def _chip_exchange(a):
    _, rh, W = a.shape

    def body(a_ref, t_ref, send_sems, recv_sems):
        x, y, c = _place()
        cps = [_remote(a_ref.at[2 * cx + cy], t_ref.at[j], send_sems, recv_sems, j, (cx, cy, c))
               for j, (cx, cy) in enumerate(_other_chips(x, y))]
        for cp in cps:
            cp.start()
        for cp in cps:
            cp.wait()

    return pl.pallas_call(
        body, name="grad_chip_exchange", in_specs=[ANY], out_specs=ANY,
        out_shape=jax.ShapeDtypeStruct((3, rh, W), a.dtype),
        scratch_shapes=[pltpu.SemaphoreType.DMA((3,)), pltpu.SemaphoreType.DMA((3,))],
    )(a)


def _chip_add(g, t1, t2, c, me):
    _, _, rh, W = g.shape
    tr = _tile(rh, 256, 2 * SUBLANES)

    def body(c_ref, me_ref, g_ref, t1_ref, t2_ref, o_ref):
        own = g_ref[0, 0] + t1_ref[0]
        o_ref[...] = own + t2_ref[0].astype(F32) + t2_ref[1].astype(F32) + t2_ref[2].astype(F32)

    return pl.pallas_call(
        body, name="grad_chip_add",
        grid_spec=pltpu.PrefetchScalarGridSpec(
            num_scalar_prefetch=2, grid=(rh // tr,),
            in_specs=[pl.BlockSpec((1, 1, tr, W), lambda i, c_ref, me_ref: (me_ref[0], c_ref[0], i, 0)),
                      pl.BlockSpec((1, tr, W), lambda i, c_ref, me_ref: (me_ref[0], i, 0)),
                      pl.BlockSpec((3, tr, W), lambda i, c_ref, me_ref: (0, i, 0))],
            out_specs=pl.BlockSpec((tr, W), lambda i, c_ref, me_ref: (i, 0))),
        out_shape=jax.ShapeDtypeStruct((rh, W), F32),
        compiler_params=_params("parallel"),
    )(c.reshape(1).astype(jnp.int32), me.reshape(1).astype(jnp.int32), g, t1, t2)


def _pair_share(h):
    rh, W = h.shape

    def body(h_ref, f_ref, send_sems, recv_sems):
        x, y, c = _place()
        cp = _remote(h_ref, f_ref, send_sems, recv_sems, 0, (x, y, 1 - c))
        cp.start()
        cp.wait()

    return pl.pallas_call(
        body, name="grad_pair_share", in_specs=[ANY], out_specs=ANY,
        out_shape=jax.ShapeDtypeStruct((rh, W), h.dtype),
        scratch_shapes=[pltpu.SemaphoreType.DMA((1,)), pltpu.SemaphoreType.DMA((1,))],
    )(h)


def _allreduce_small(pack, name):
    rows, W = pack.shape

    def body(p_ref, o_ref, buf, send_sems, recv_sems):
        x, y, c = _place()
        me = 4 * x + 2 * y + c
        buf[me] = p_ref[...]
        cps = []
        for r in range(1, 8):
            fx, fy, fc = (r >> 2) & 1, (r >> 1) & 1, r & 1
            to = (1 - x if fx else x, 1 - y if fy else y, 1 - c if fc else c)
            cps.append(_remote(p_ref, buf.at[me], send_sems, recv_sems, r - 1, to))
        for cp in cps:
            cp.start()
        for r in range(1, 8):
            fx, fy, fc = (r >> 2) & 1, (r >> 1) & 1, r & 1
            frm = 4 * (1 - x if fx else x) + 2 * (1 - y if fy else y) + (1 - c if fc else c)
            _remote(p_ref, buf.at[frm], send_sems, recv_sems, r - 1, (x, y, c)).wait_recv()
        for cp in cps:
            cp.wait_send()
        acc = buf[0]
        for i in range(1, 8):
            acc = acc + buf[i]
        o_ref[...] = acc

    return pl.pallas_call(
        body, name=name, in_specs=[VMEM], out_specs=VMEM,
        out_shape=jax.ShapeDtypeStruct((rows, W), F32),
        scratch_shapes=[pltpu.VMEM((8, rows, W), F32), pltpu.SemaphoreType.DMA((7,)), pltpu.SemaphoreType.DMA((7,))],
    )(pack)


def _flatten(arrs, dtype):
    flat = jnp.concatenate([a.reshape(-1).astype(dtype) for a in arrs])
    per = FLAT_W * FLAT_ROW_MULT
    total = -(-flat.shape[0] // per) * per
    return jnp.pad(flat, (0, total - flat.shape[0])).reshape(-1, FLAT_W)


def _unflatten(flat2d, shapes):
    out, off = [], 0
    for shp in shapes:
        n = 1
        for d in shp:
            n *= d
        out.append(flat2d[..., off:off + n].reshape(flat2d.shape[:-1] + tuple(shp)))
        off += n
    return out


def _join_cols(g):
    nd = g.ndim
    return jnp.moveaxis(g, 0, nd - 2).reshape(g.shape[1:-1] + (N_CHIPS * g.shape[-1],))


def _split_cols(full):
    c = full.shape[-1] // N_CHIPS
    return jnp.moveaxis(full.reshape(full.shape[:-1] + (N_CHIPS, c)), -2, 0)


def _join_rows(g):
    return jnp.moveaxis(g, 0, 1).reshape(g.shape[1], N_CHIPS * g.shape[2], g.shape[3])


def _split_rows(full):
    L, r4, D = full.shape
    return jnp.moveaxis(full.reshape(L, N_CHIPS, r4 // N_CHIPS, D), 1, 0)


def _row_layout(a, tq):
    P, S, _ = a.shape
    return a.reshape(P, S // tq, tq, 2).transpose(0, 1, 3, 2)


def _pad_row(v, width=FLAT_W):
    flat = v.reshape(-1)
    rows = -(-flat.shape[0] // width)
    return jnp.pad(flat, (0, rows * width - flat.shape[0]))


def kernel(x, attn_norm, ffn_norm, a_w_in, a_conv, a_w_out, kv_norm, w_kvf, b_f, k_norm, b_w_qg, q_norm, b_w_out, ffn_w_up, ffn_conv, ffn_w_down, loss_target, m_attn_norm, m_ffn_norm, m_a_w_in, m_a_conv, m_a_w_out, m_kv_norm, m_w_kvf, m_b_f, m_k_norm, m_b_w_qg, m_q_norm, m_b_w_out, m_ffn_w_up, m_ffn_conv, m_ffn_w_down, v_attn_norm, v_ffn_norm, v_a_w_in, v_a_conv, v_a_w_out, v_kv_norm, v_w_kvf, v_b_f, v_k_norm, v_b_w_qg, v_q_norm, v_b_w_out, v_ffn_w_up, v_ffn_conv, v_ffn_w_down):
    xs = x[0]
    S, D = xs.shape
    H, hd = b_f.shape[0], k_norm.shape[0]
    depth = attn_norm.shape[0]
    n_a = a_w_in.shape[0]
    P = D // LANES
    assert LANES == 2 * hd and H * hd == D, "the attention kernels hold two heads per lane tile"
    mx, my, mc = _place()
    chip = 2 * mx + my

    big = [a_w_in, a_w_out, w_kvf, b_w_qg, b_w_out, ffn_w_up, ffn_w_down]
    big_shapes = [w.shape for w in big]
    gathered = _allgather_weights(_flatten(big, BF))
    g_in, g_out, g_kvf, g_qg, g_bout, g_up, g_down = _unflatten(gathered.reshape(N_CHIPS, -1), big_shapes)
    wa_in, wb_qg, w_up = _join_cols(g_in), _join_cols(g_qg), _join_cols(g_up)
    wa_out, wb_out, w_down = _join_rows(g_out), _join_rows(g_bout), _join_rows(g_down)
    kvf_cols = 2 * D + LANES
    wkvf = jnp.pad(_join_cols(g_kvf), ((0, 0), (0, kvf_cols - (2 * D + H))))

    def placed(shard):
        full = jnp.zeros(shard.shape[:-1] + (N_CHIPS, shard.shape[-1]), F32)
        full = lax.dynamic_update_slice_in_dim(full, shard[..., None, :], chip, axis=full.ndim - 2)
        return jnp.where(mc == 0, full, 0.0).reshape(-1)

    conv_pack = jnp.concatenate([_pad_row(placed(a_conv)), _pad_row(placed(ffn_conv))]).reshape(-1, FLAT_W)
    conv_full = _allreduce_small(conv_pack, "allgather_conv_taps").reshape(-1)
    n_ac = a_conv.size * N_CHIPS
    a_conv_f = conv_full[:n_ac].reshape(a_conv.shape[:-1] + (-1,))
    off = _pad_row(placed(a_conv)).shape[0]
    ffn_conv_f = conv_full[off:off + ffn_conv.size * N_CHIPS].reshape(ffn_conv.shape[:-1] + (-1,))
    F = ffn_conv_f.shape[-1]

    b_pad = jnp.pad(b_f, (0, LANES - H)).reshape(1, LANES)
    gate_blk = 2 * D // LANES
    tq = _attn_tile(S)
    scale = hd ** -0.5

    saved = []
    cur = xs
    kv = None
    for l in range(depth):
        rec = {"x_in": cur}
        if l < n_a:
            proj, xn = _norm_matmul(cur, attn_norm[l], wa_in[l], 3, BF, f"a_in_{l}")
            z = _mixer_mid_fwd(proj, a_conv_f[l], f"a_mid_{l}")
            mid = _matmul_residual(z, wa_out[l], cur, f"a_out_{l}")
            rec.update(proj=proj, xn=xn, z=z)
        else:
            j = l - n_a
            if kv is None:
                kvf, hn = _norm_matmul(cur, kv_norm, wkvf, 1, F32, "kvf_proj")
                kn = _headnorm(kvf, 0, 0, k_norm, 1.0, D, "k_norm")
                vb = kvf[0, :, D:2 * D].astype(BF)
                cgate = _gate_fwd(kvf, b_pad, gate_blk, "gate_cumsum")
                kv = dict(kvf=kvf, hn=hn, vb=vb, cgate=cgate, x_in=cur, dk=[], dv=[], dc=[],
                          ka=_augment(kn, cgate, "k", hd, "k_augment"), va=_augment(vb, cgate, "v", hd, "v_augment"))
            qg, xn = _norm_matmul(cur, attn_norm[l], wb_qg[j], 2, F32, f"qg_proj_{j}")
            qn = _headnorm(qg, 0, 0, q_norm[j], scale * LOG2E, D, f"q_norm_{j}")
            qa = _augment(qn, kv["cgate"], "q", hd, f"q_augment_{j}")
            o, og, m_max, l_sum = _attn_fwd(qa, kv["ka"], kv["va"], qg, hd, f"attn_fwd_{j}")
            mid = _matmul_residual(og, wb_out[j], cur, f"b_out_{j}")
            rec.update(qg=qg, xn=xn, qa=qa, o=o, og=og, m=m_max, l=l_sum)
        up, xn2 = _norm_matmul(mid, ffn_norm[l], w_up[l], 2, BF, f"ffn_up_{l}")
        z2 = _ffn_mid_fwd(up, ffn_conv_f[l], f"ffn_mid_{l}")
        cur = _matmul_residual(z2, w_down[l], mid, f"ffn_down_{l}")
        rec.update(x_mid=mid, up=up, xn2=xn2, z2=z2)
        saved.append(rec)

    dy, loss_part = _loss_head(cur, loss_target[0], "loss_head")

    g_attn_norm, g_ffn_norm = [None] * depth, [None] * depth
    g_a_in, g_a_conv, g_a_out = [None] * n_a, [None] * n_a, [None] * n_a
    g_qg, g_qn, g_bo = [None] * (depth - n_a), [None] * (depth - n_a), [None] * (depth - n_a)
    g_up, g_fc, g_down = [None] * depth, [None] * depth, [None] * depth
    for l in reversed(range(depth)):
        rec = saved[l]
        dz = _matmul_nt(dy, w_down[l], f"ffn_down_bwd_{l}")
        dup, g_fc[l] = _ffn_mid_bwd(rec["up"], dz, ffn_conv_f[l], f"ffn_mid_bwd_{l}")
        g_down[l] = _wgrad(rec["z2"], dy[None], f"ffn_down_wgrad_{l}")
        g_up[l] = _wgrad(rec["xn2"], dup, f"ffn_up_wgrad_{l}")
        dy, g_ffn_norm[l] = _dnorm(dup, w_up[l], rec["x_mid"], ffn_norm[l], dy, f"ffn_up_bwd_{l}")
        if l < n_a:
            dz = _matmul_nt(dy, wa_out[l], f"a_out_bwd_{l}")
            dproj, g_a_conv[l] = _mixer_mid_bwd(rec["proj"], dz, a_conv_f[l], f"a_mid_bwd_{l}")
            g_a_out[l] = _wgrad(rec["z"], dy[None], f"a_out_wgrad_{l}")
            g_a_in[l] = _wgrad(rec["xn"], dproj, f"a_in_wgrad_{l}")
            dy, g_attn_norm[l] = _dnorm(dproj, wa_in[l], rec["x_in"], attn_norm[l], dy, f"a_in_bwd_{l}")
        else:
            j = l - n_a
            dog = _matmul_nt(dy, wb_out[j], f"b_out_bwd_{j}")
            g_out, dgate, evec = _attn_out_bwd(dog, rec["o"], rec["qg"], rec["l"], hd, f"attn_gate_bwd_{j}")
            g_bo[j] = _wgrad(rec["og"], dy[None], f"b_out_wgrad_{j}")
            dqn, dk, dv, dc = _attn_bwd(rec["qa"], kv["ka"], kv["vb"], g_out, _row_layout(rec["m"], tq),
                                        _row_layout(evec, tq), hd, f"attn_bwd_{j}")
            kv["dk"].append(dk)
            kv["dv"].append(dv)
            kv["dc"].append(dc)
            dq_pre, g_qn[j] = _headnorm_bwd(rec["qg"], 0, 0, q_norm[j], [dqn], D, f"q_norm_bwd_{j}")
            dqg = jnp.stack([dq_pre, dgate])
            g_qg[j] = _wgrad(rec["xn"], dqg, f"qg_wgrad_{j}")
            dy, g_attn_norm[l] = _dnorm(dqg, wb_qg[j], rec["x_in"], attn_norm[l], dy, f"qg_bwd_{j}")
            if l == n_a:
                dk_s, g_k_norm = _headnorm_bwd(kv["kvf"], 0, 0, k_norm, kv["dk"], D, "k_norm_bwd")
                dv_s = functools.reduce(jnp.add, kv["dv"]).astype(BF)
                dc_sum = functools.reduce(jnp.add, kv["dc"])
                dc_pad = jnp.pad(dc_sum.transpose(1, 0, 2).reshape(S, H), ((0, 0), (0, LANES - H)))
                df, db = _gate_bwd(dc_pad, kv["kvf"], b_pad, gate_blk, "gate_bwd")
                dkvf = jnp.concatenate([dk_s, dv_s, df.astype(BF)], axis=1)[None]
                g_kvf = _wgrad(kv["hn"], dkvf, "kvf_wgrad")[:, :2 * D + H]
                g_b_f = db[:H]
                dy, g_kv_norm = _dnorm(dkvf, wkvf, kv["x_in"], kv_norm, dy, "kvf_bwd")
    grad_x = dy[None]

    full_grads = [_split_cols(jnp.stack(g_a_in)), _split_rows(jnp.stack(g_a_out)), _split_cols(g_kvf),
                  _split_cols(jnp.stack(g_qg)), _split_rows(jnp.stack(g_bo)), _split_cols(jnp.stack(g_up)),
                  _split_rows(jnp.stack(g_down))]
    gflat = jnp.stack([_flatten([g[k] for g in full_grads], F32) for k in range(N_CHIPS)])
    R = gflat.shape[1]
    g4 = gflat.reshape(N_CHIPS, 2, R // 2, FLAT_W)
    from_sibling = _pair_exchange(g4)
    from_chips = _chip_exchange(_pair_add(g4, from_sibling, mc))
    mine = _chip_add(g4, from_sibling, from_chips, mc, chip)
    theirs = _pair_share(mine)
    shard_flat = jnp.where(mc == 0, jnp.stack([mine, theirs]), jnp.stack([theirs, mine])).reshape(-1)
    big_grads = _unflatten(shard_flat, big_shapes)

    small = [loss_part[0, :1], jnp.stack(g_attn_norm), jnp.stack(g_ffn_norm), g_kv_norm, g_b_f, g_k_norm,
             jnp.stack(g_qn), jnp.stack(g_a_conv), jnp.stack(g_fc)]
    small_sum = _allreduce_small(jnp.concatenate([_pad_row(s) for s in small]).reshape(-1, FLAT_W),
                                 "allreduce_small_grads").reshape(-1)
    parts, off = [], 0
    for s in small:
        parts.append(small_sum[off:off + s.size].reshape(s.shape))
        off += _pad_row(s).shape[0]
    loss = parts[0][0]
    gr_attn_norm, gr_ffn_norm, gr_kv_norm, gr_b_f, gr_k_norm, gr_q_norm, gr_a_conv_full, gr_ffn_conv_full = parts[1:]

    def my_cols(full):
        c = full.shape[-1] // N_CHIPS
        return lax.dynamic_slice_in_dim(full, chip * c, c, axis=full.ndim - 1)

    gr_a_in, gr_a_out, gr_kvf, gr_qg, gr_bo, gr_up, gr_down = big_grads
    grads = [gr_attn_norm, gr_ffn_norm, gr_a_in, my_cols(gr_a_conv_full), gr_a_out, gr_kv_norm, gr_kvf, gr_b_f,
             gr_k_norm, gr_qg, gr_q_norm, gr_bo, gr_up, my_cols(gr_ffn_conv_full), gr_down]
    weights = [attn_norm, ffn_norm, a_w_in, a_conv, a_w_out, kv_norm, w_kvf, b_f, k_norm, b_w_qg, q_norm, b_w_out,
               ffn_w_up, ffn_conv, ffn_w_down]
    ms = [m_attn_norm, m_ffn_norm, m_a_w_in, m_a_conv, m_a_w_out, m_kv_norm, m_w_kvf, m_b_f, m_k_norm, m_b_w_qg,
          m_q_norm, m_b_w_out, m_ffn_w_up, m_ffn_conv, m_ffn_w_down]
    vs = [v_attn_norm, v_ffn_norm, v_a_w_in, v_a_conv, v_a_w_out, v_kv_norm, v_w_kvf, v_b_f, v_k_norm, v_b_w_qg,
          v_q_norm, v_b_w_out, v_ffn_w_up, v_ffn_conv, v_ffn_w_down]
    deltas, new_ms, new_vs = [], [], []
    for i, (w, g, m, v) in enumerate(zip(weights, grads, ms, vs)):
        d, nm, nv = _adamw(w, g, m, v, f"adamw_{i}")
        deltas.append(d)
        new_ms.append(nm)
        new_vs.append(nv)
    return (loss, grad_x, *grads, *deltas, *new_ms, *new_vs)
```

```python
import functools

import jax
import jax.numpy as jnp
from jax import lax
from jax.experimental import pallas as pl
from jax.experimental.pallas import tpu as pltpu

F32 = jnp.float32
BF = jnp.bfloat16
LANES = 128
SUBLANES = 8
RMS_EPS = 1e-6
LOG2E = 1.4426950408889634
FLAT_W = 1024
FLAT_ROW_MULT = 512
N_CHIPS = 4
CONV_W = 3
HALO = SUBLANES

ADAM_LR = 0.001
ADAM_B1 = 0.9
ADAM_B2 = 0.999
ADAM_EPS = 1e-08
ADAM_WD = 0.01
ADAM_STEP = 10

MESH = pl.DeviceIdType.MESH
ANY = pl.BlockSpec(memory_space=pl.ANY)
VMEM = pl.BlockSpec(memory_space=pltpu.VMEM)
NT_DIMS = (((1,), (1,)), ((), ()))
TN_DIMS = (((0,), (0,)), ((), ()))


def _tile(n, pref, mult=LANES):
    t = (min(pref, n) // mult) * mult
    while t >= mult:
        if n % t == 0:
            break
        t -= mult
    if t < mult or (t * 4 < pref and n <= 4 * pref):
        return n
    return t


def _params(*sem):
    return pltpu.CompilerParams(dimension_semantics=sem)


def _norm_matmul(x, g, w, parts, out_dtype, name):
    S, D = x.shape
    C = w.shape[1] // parts
    ts, tn = _tile(S, 512, SUBLANES), _tile(C, 1408)
    npc = C // tn

    def body(x_ref, g_ref, w_ref, o_ref, xn_ref):
        @pl.when(pl.program_id(1) == 0)
        def _():
            xf = x_ref[...]
            r = lax.rsqrt(jnp.mean(xf * xf, axis=-1, keepdims=True) + RMS_EPS)
            xn_ref[...] = (xf * r * g_ref[...]).astype(BF)

        o_ref[0] = jnp.dot(xn_ref[...], w_ref[...], preferred_element_type=F32).astype(out_dtype)

    return pl.pallas_call(
        body, name=name, grid=(S // ts, parts * npc),
        in_specs=[pl.BlockSpec((ts, D), lambda s, n: (s, 0)),
                  pl.BlockSpec((1, D), lambda s, n: (0, 0)),
                  pl.BlockSpec((D, tn), lambda s, n: (0, n))],
        out_specs=[pl.BlockSpec((1, ts, tn), lambda s, n: (n // npc, s, n % npc)),
                   pl.BlockSpec((ts, D), lambda s, n: (s, 0))],
        out_shape=[jax.ShapeDtypeStruct((parts, S, C), out_dtype), jax.ShapeDtypeStruct((S, D), BF)],
        compiler_params=_params("parallel", "arbitrary"),
    )(x, g.reshape(1, D), w)


def _shift_down(u, prev, k, row):
    r = pltpu.roll(u, k, 0)
    for j in range(k):
        r = jnp.where(row == j, prev[HALO - k + j:HALO - k + j + 1, :], r)
    return r


def _shift_up(d, nxt, k, row):
    n = d.shape[0]
    r = pltpu.roll(d, n - k, 0)
    for j in range(k):
        r = jnp.where(row == n - k + j, nxt[j:j + 1, :], r)
    return r


def _conv3(u, prev, w, row):
    return _shift_down(u, prev, 2, row) * w[0:1] + _shift_down(u, prev, 1, row) * w[1:2] + u * w[2:3]


def _conv3_t(d, nxt, w, row):
    return d * w[2:3] + _shift_up(d, nxt, 1, row) * w[1:2] + _shift_up(d, nxt, 2, row) * w[0:1]


def _tap_rows(t0, t1, t2):
    row = lax.broadcasted_iota(jnp.int32, (SUBLANES, t0.shape[1]), 0)
    return jnp.where(row == 0, t0, jnp.where(row == 1, t1, jnp.where(row == 2, t2, 0.0)))


def _pad_conv(cw):
    return jnp.pad(cw, ((0, SUBLANES - CONV_W), (0, 0)))


def _mixer_mid_fwd(proj, cw, name):
    _, S, C = proj.shape
    ts, tc = _tile(S, 512, SUBLANES), _tile(C, 1024)

    def body(b_ref, c_ref, h_ref, cw_ref, z_ref, carry):
        @pl.when(pl.program_id(1) == 0)
        def _():
            carry[...] = jnp.zeros_like(carry)

        u = c_ref[0].astype(F32) * h_ref[0].astype(F32)
        row = lax.broadcasted_iota(jnp.int32, u.shape, 0)
        cv = _conv3(u, carry[...], cw_ref[...], row)
        z_ref[...] = (b_ref[0].astype(F32) * cv).astype(BF)
        carry[...] = u[ts - HALO:ts, :]

    part = lambda p: pl.BlockSpec((1, ts, tc), lambda c, s: (p, s, c))
    return pl.pallas_call(
        body, name=name, grid=(C // tc, S // ts),
        in_specs=[part(0), part(1), part(2), pl.BlockSpec((SUBLANES, tc), lambda c, s: (0, c))],
        out_specs=pl.BlockSpec((ts, tc), lambda c, s: (s, c)),
        out_shape=jax.ShapeDtypeStruct((S, C), BF),
        scratch_shapes=[pltpu.VMEM((HALO, tc), F32)],
        compiler_params=_params("parallel", "arbitrary"),
    )(proj, proj, proj, _pad_conv(cw))


def _ffn_mid_fwd(up, cw, name):
    _, S, C = up.shape
    ts, tc = _tile(S, 512, SUBLANES), _tile(C, 1408)

    def body(a_ref, g_ref, cw_ref, z_ref, carry):
        @pl.when(pl.program_id(1) == 0)
        def _():
            carry[...] = jnp.zeros_like(carry)

        a_pre = a_ref[0].astype(F32)
        row = lax.broadcasted_iota(jnp.int32, a_pre.shape, 0)
        a = _conv3(a_pre, carry[...], cw_ref[...], row)
        z_ref[...] = (a * jax.nn.sigmoid(a) * g_ref[0].astype(F32)).astype(BF)
        carry[...] = a_pre[ts - HALO:ts, :]

    part = lambda p: pl.BlockSpec((1, ts, tc), lambda c, s: (p, s, c))
    return pl.pallas_call(
        body, name=name, grid=(C // tc, S // ts),
        in_specs=[part(0), part(1), pl.BlockSpec((SUBLANES, tc), lambda c, s: (0, c))],
        out_specs=pl.BlockSpec((ts, tc), lambda c, s: (s, c)),
        out_shape=jax.ShapeDtypeStruct((S, C), BF),
        scratch_shapes=[pltpu.VMEM((HALO, tc), F32)],
        compiler_params=_params("parallel", "arbitrary"),
    )(up, up, _pad_conv(cw))


def _halo_specs(p, ts, tc, n_s):
    per = ts // HALO
    last = n_s * per - 1
    before = pl.BlockSpec((1, HALO, tc), lambda c, s: (p, jnp.maximum(s * per - 1, 0), c))
    after = pl.BlockSpec((1, HALO, tc), lambda c, s: (p, jnp.minimum((s + 1) * per, last), c))
    return before, after


def _mixer_mid_bwd(proj, dz, cw, name):
    _, S, C = proj.shape
    ts, tc = _tile(S, 512, SUBLANES), _tile(C, 1024)
    n_s = S // ts

    def body(b_ref, c_ref, h_ref, dz_ref, cp_ref, hp_ref, bn_ref, dzn_ref, cw_ref, d_ref, dcw_ref):
        s = pl.program_id(1)
        w = cw_ref[...]
        b, c, h = b_ref[0].astype(F32), c_ref[0].astype(F32), h_ref[0].astype(F32)
        dz_t = dz_ref[0]
        row = lax.broadcasted_iota(jnp.int32, b.shape, 0)
        u = c * h
        prev = jnp.where(s > 0, cp_ref[0].astype(F32) * hp_ref[0].astype(F32), 0.0)
        u1, u2 = _shift_down(u, prev, 1, row), _shift_down(u, prev, 2, row)
        cv = u2 * w[0:1] + u1 * w[1:2] + u * w[2:3]
        dcv = dz_t * b
        nxt = jnp.where(s < n_s - 1, dzn_ref[0] * bn_ref[0].astype(F32), 0.0)
        du = _conv3_t(dcv, nxt, w, row)
        d_ref[0] = (dz_t * cv).astype(BF)
        d_ref[1] = (du * h).astype(BF)
        d_ref[2] = (du * c).astype(BF)
        part = _tap_rows(jnp.sum(dcv * u2, axis=0, keepdims=True), jnp.sum(dcv * u1, axis=0, keepdims=True),
                         jnp.sum(dcv * u, axis=0, keepdims=True))

        @pl.when(s == 0)
        def _():
            dcw_ref[...] = part

        @pl.when(s > 0)
        def _():
            dcw_ref[...] += part

    part_spec = lambda p: pl.BlockSpec((1, ts, tc), lambda c, s: (p, s, c))
    c_before, _ = _halo_specs(1, ts, tc, n_s)
    h_before, _ = _halo_specs(2, ts, tc, n_s)
    _, b_after = _halo_specs(0, ts, tc, n_s)
    _, dz_after = _halo_specs(0, ts, tc, n_s)
    dproj, dcw = pl.pallas_call(
        body, name=name, grid=(C // tc, n_s),
        in_specs=[part_spec(0), part_spec(1), part_spec(2), part_spec(0), c_before, h_before, b_after, dz_after,
                  pl.BlockSpec((SUBLANES, tc), lambda c, s: (0, c))],
        out_specs=[pl.BlockSpec((3, ts, tc), lambda c, s: (0, s, c)),
                   pl.BlockSpec((SUBLANES, tc), lambda c, s: (0, c))],
        out_shape=[jax.ShapeDtypeStruct((3, S, C), BF), jax.ShapeDtypeStruct((SUBLANES, C), F32)],
        compiler_params=_params("parallel", "arbitrary"),
    )(proj, proj, proj, dz[None], proj, proj, proj, dz[None], _pad_conv(cw))
    return dproj, dcw[:CONV_W]


def _ffn_mid_bwd(up, dz, cw, name):
    _, S, C = up.shape
    ts, tc = _tile(S, 512, SUBLANES), _tile(C, 1408)
    n_s = S // ts

    def dact(a, g, dz_t):
        sg = jax.nn.sigmoid(a)
        return dz_t * g * (sg * (1.0 + a * (1.0 - sg))), dz_t * (a * sg)

    def body(a_ref, g_ref, dz_ref, ap_ref, an_ref, gn_ref, dzn_ref, cw_ref, d_ref, dcw_ref):
        s = pl.program_id(1)
        w = cw_ref[...]
        a_pre, g = a_ref[0].astype(F32), g_ref[0].astype(F32)
        row = lax.broadcasted_iota(jnp.int32, a_pre.shape, 0)
        prev = jnp.where(s > 0, ap_ref[0].astype(F32), 0.0)
        a1, a2 = _shift_down(a_pre, prev, 1, row), _shift_down(a_pre, prev, 2, row)
        a = a2 * w[0:1] + a1 * w[1:2] + a_pre * w[2:3]
        da, dg = dact(a, g, dz_ref[0])
        an_pre = an_ref[0].astype(F32)
        row8 = lax.broadcasted_iota(jnp.int32, an_pre.shape, 0)
        a_next = _conv3(an_pre, a_pre[ts - HALO:ts, :], w, row8)
        da_next, _ = dact(a_next, gn_ref[0].astype(F32), dzn_ref[0])
        nxt = jnp.where(s < n_s - 1, da_next, 0.0)
        d_ref[0] = _conv3_t(da, nxt, w, row).astype(BF)
        d_ref[1] = dg.astype(BF)
        part = _tap_rows(jnp.sum(da * a2, axis=0, keepdims=True), jnp.sum(da * a1, axis=0, keepdims=True),
                         jnp.sum(da * a_pre, axis=0, keepdims=True))

        @pl.when(s == 0)
        def _():
            dcw_ref[...] = part

        @pl.when(s > 0)
        def _():
            dcw_ref[...] += part

    part_spec = lambda p: pl.BlockSpec((1, ts, tc), lambda c, s: (p, s, c))
    a_before, a_after = _halo_specs(0, ts, tc, n_s)
    _, g_after = _halo_specs(1, ts, tc, n_s)
    _, dz_after = _halo_specs(0, ts, tc, n_s)
    dup, dcw = pl.pallas_call(
        body, name=name, grid=(C // tc, n_s),
        in_specs=[part_spec(0), part_spec(1), part_spec(0), a_before, a_after, g_after, dz_after,
                  pl.BlockSpec((SUBLANES, tc), lambda c, s: (0, c))],
        out_specs=[pl.BlockSpec((2, ts, tc), lambda c, s: (0, s, c)),
                   pl.BlockSpec((SUBLANES, tc), lambda c, s: (0, c))],
        out_shape=[jax.ShapeDtypeStruct((2, S, C), BF), jax.ShapeDtypeStruct((SUBLANES, C), F32)],
        compiler_params=_params("parallel", "arbitrary"),
    )(up, up, dz[None], up, up, up, dz[None], _pad_conv(cw))
    return dup, dcw[:CONV_W]


def _matmul_residual(z, w, x, name):
    S, K = z.shape
    D = w.shape[1]
    ts = _tile(S, 512, SUBLANES)

    def body(z_ref, w_ref, x_ref, o_ref):
        o_ref[...] = x_ref[...] + jnp.dot(z_ref[...], w_ref[...], preferred_element_type=F32)

    return pl.pallas_call(
        body, name=name, grid=(S // ts,),
        in_specs=[pl.BlockSpec((ts, K), lambda s: (s, 0)), pl.BlockSpec((K, D), lambda s: (0, 0)),
                  pl.BlockSpec((ts, D), lambda s: (s, 0))],
        out_specs=pl.BlockSpec((ts, D), lambda s: (s, 0)),
        out_shape=jax.ShapeDtypeStruct((S, D), F32),
        compiler_params=_params("parallel"),
    )(z, w, x)


def _matmul_nt(a, w, name):
    S, K = a.shape
    N = w.shape[0]
    ts, tn = _tile(S, 512, SUBLANES), _tile(N, 1408)

    def body(a_ref, w_ref, o_ref, abf):
        @pl.when(pl.program_id(1) == 0)
        def _():
            abf[...] = a_ref[...].astype(BF)

        o_ref[...] = lax.dot_general(abf[...], w_ref[...], NT_DIMS, preferred_element_type=F32)

    return pl.pallas_call(
        body, name=name, grid=(S // ts, N // tn),
        in_specs=[pl.BlockSpec((ts, K), lambda s, n: (s, 0)), pl.BlockSpec((tn, K), lambda s, n: (n, 0))],
        out_specs=pl.BlockSpec((ts, tn), lambda s, n: (s, n)),
        out_shape=jax.ShapeDtypeStruct((S, N), F32),
        scratch_shapes=[pltpu.VMEM((ts, K), BF)],
        compiler_params=_params("parallel", "arbitrary"),
    )(a, w)


def _wgrad(a, b, name):
    S, M = a.shape
    P, _, C = b.shape
    tm, tn, tk = _tile(M, 1408), _tile(C, 1408), _tile(S, 512, SUBLANES)
    nnc = C // tn

    def body(a_ref, b_ref, o_ref):
        @pl.when(pl.program_id(2) == 0)
        def _():
            o_ref[...] = jnp.zeros_like(o_ref)

        o_ref[...] += lax.dot_general(a_ref[...], b_ref[0].astype(BF), TN_DIMS, preferred_element_type=F32)

    return pl.pallas_call(
        body, name=name, grid=(M // tm, P * nnc, S // tk),
        in_specs=[pl.BlockSpec((tk, tm), lambda m, n, k: (k, m)),
                  pl.BlockSpec((1, tk, tn), lambda m, n, k: (n // nnc, k, n % nnc))],
        out_specs=pl.BlockSpec((tm, tn), lambda m, n, k: (m, n)),
        out_shape=jax.ShapeDtypeStruct((M, P * C), F32),
        compiler_params=_params("parallel", "parallel", "arbitrary"),
    )(a, b)


def _dnorm(dp, w, x, g, dy, name):
    P, S, C = dp.shape
    D = x.shape[1]
    ts, tk = _tile(S, 512, SUBLANES), _tile(C, 1408)
    nkc = C // tk
    n_k = P * nkc

    def body(dp_ref, w_ref, x_ref, g_ref, dy_ref, dx_ref, dg_ref, acc):
        s, k = pl.program_id(0), pl.program_id(1)

        @pl.when(k == 0)
        def _():
            acc[...] = jnp.zeros_like(acc)

        acc[...] += lax.dot_general(dp_ref[0], w_ref[...], NT_DIMS, preferred_element_type=F32)

        @pl.when((s == 0) & (k == 0))
        def _():
            dg_ref[...] = jnp.zeros_like(dg_ref)

        @pl.when(k == n_k - 1)
        def _():
            xf = x_ref[...]
            r = lax.rsqrt(jnp.mean(xf * xf, axis=-1, keepdims=True) + RMS_EPS)
            xhat = xf * r
            dxn = acc[...]
            dxhat = dxn * g_ref[...]
            dx_ref[...] = dy_ref[...] + r * (dxhat - xhat * jnp.mean(dxhat * xhat, axis=-1, keepdims=True))
            dg_ref[...] += jnp.broadcast_to(jnp.sum(dxn * xhat, axis=0, keepdims=True), dg_ref.shape)

    dx, dg = pl.pallas_call(
        body, name=name, grid=(S // ts, n_k),
        in_specs=[pl.BlockSpec((1, ts, tk), lambda s, k: (k // nkc, s, k % nkc)),
                  pl.BlockSpec((D, tk), lambda s, k: (0, k)),
                  pl.BlockSpec((ts, D), lambda s, k: (s, 0)),
                  pl.BlockSpec((1, D), lambda s, k: (0, 0)),
                  pl.BlockSpec((ts, D), lambda s, k: (s, 0))],
        out_specs=[pl.BlockSpec((ts, D), lambda s, k: (s, 0)),
                   pl.BlockSpec((SUBLANES, D), lambda s, k: (0, 0))],
        out_shape=[jax.ShapeDtypeStruct((S, D), F32), jax.ShapeDtypeStruct((SUBLANES, D), F32)],
        scratch_shapes=[pltpu.VMEM((ts, D), F32)],
        compiler_params=_params("arbitrary", "arbitrary"),
    )(dp, w, x, g.reshape(1, D), dy)
    return dx, dg[0]


def _head_masks(shape, hd):
    lane = lax.broadcasted_iota(jnp.int32, shape, 1)
    return lane < hd


def _pair_sum(v, lo):
    s0 = jnp.sum(jnp.where(lo, v, 0.0), axis=-1, keepdims=True)
    s1 = jnp.sum(jnp.where(lo, 0.0, v), axis=-1, keepdims=True)
    return jnp.where(lo, s0, s1)


def _headnorm(src, part, colblk, w, scale, D, name):
    S = src.shape[1]
    hd = w.shape[0]
    ts = _tile(S, 512, SUBLANES)
    w2 = jnp.tile(w, LANES // hd).reshape(1, LANES)

    def body(x_ref, w_ref, o_ref):
        lo = _head_masks((ts, LANES), hd)
        for t in range(D // LANES):
            xt = x_ref[0, :, t * LANES:(t + 1) * LANES]
            r = lax.rsqrt(_pair_sum(xt * xt, lo) * (1.0 / hd) + RMS_EPS)
            o_ref[:, t * LANES:(t + 1) * LANES] = (xt * r * w_ref[...] * scale).astype(BF)

    return pl.pallas_call(
        body, name=name, grid=(S // ts,),
        in_specs=[pl.BlockSpec((1, ts, D), lambda s: (part, s, colblk)), pl.BlockSpec((1, LANES), lambda s: (0, 0))],
        out_specs=pl.BlockSpec((ts, D), lambda s: (s, 0)),
        out_shape=jax.ShapeDtypeStruct((S, D), BF),
        compiler_params=_params("parallel"),
    )(src, w2)


def _headnorm_bwd(src, part, colblk, w, dys, D, name):
    S = src.shape[1]
    hd = w.shape[0]
    ts = _tile(S, 512, SUBLANES)
    w2 = jnp.tile(w, LANES // hd).reshape(1, LANES)
    n_dy = len(dys)

    def body(x_ref, w_ref, *rest):
        dy_refs, dx_ref, dw_ref = rest[:n_dy], rest[n_dy], rest[n_dy + 1]

        @pl.when(pl.program_id(0) == 0)
        def _():
            dw_ref[...] = jnp.zeros_like(dw_ref)

        lo = _head_masks((ts, LANES), hd)
        for t in range(D // LANES):
            cols = slice(t * LANES, (t + 1) * LANES)
            xt = x_ref[0, :, cols]
            dy = dy_refs[0][:, cols]
            for other in dy_refs[1:]:
                dy = dy + other[:, cols]
            r = lax.rsqrt(_pair_sum(xt * xt, lo) * (1.0 / hd) + RMS_EPS)
            xhat = xt * r
            dxhat = dy * w_ref[...]
            mean = _pair_sum(dxhat * xhat, lo) * (1.0 / hd)
            dx_ref[:, cols] = (r * (dxhat - xhat * mean)).astype(BF)
            dw_ref[:, cols] += jnp.broadcast_to(jnp.sum(dy * xhat, axis=0, keepdims=True), (SUBLANES, LANES))

    dx, dw = pl.pallas_call(
        body, name=name, grid=(S // ts,),
        in_specs=[pl.BlockSpec((1, ts, D), lambda s: (part, s, colblk)), pl.BlockSpec((1, LANES), lambda s: (0, 0))]
        + [pl.BlockSpec((ts, D), lambda s: (s, 0))] * n_dy,
        out_specs=[pl.BlockSpec((ts, D), lambda s: (s, 0)), pl.BlockSpec((SUBLANES, D), lambda s: (0, 0))],
        out_shape=[jax.ShapeDtypeStruct((S, D), BF), jax.ShapeDtypeStruct((SUBLANES, D), F32)],
        compiler_params=_params("arbitrary"),
    )(src, w2, *dys)
    return dx, jnp.sum(dw[0].reshape(D // hd, hd), axis=0)


def _tri(n, lower):
    r, c = lax.broadcasted_iota(jnp.int32, (n, n), 0), lax.broadcasted_iota(jnp.int32, (n, n), 1)
    return jnp.where((c <= r) if lower else (c >= r), 1.0, 0.0).astype(BF)


def _dot_exact(t, v):
    hi = v.astype(BF)
    r1 = v - hi.astype(F32)
    mid = r1.astype(BF)
    lo = (r1 - mid.astype(F32)).astype(BF)
    dot = lambda u: jnp.dot(t, u, preferred_element_type=F32)
    return dot(hi) + dot(mid) + dot(lo)


def _gate_fwd(kvf, b_pad, colblk, name):
    S = kvf.shape[1]
    ts = _tile(S, 512, SUBLANES)

    def body(f_ref, b_ref, c_ref, carry):
        @pl.when(pl.program_id(0) == 0)
        def _():
            carry[...] = jnp.zeros_like(carry)

        f = f_ref[0] + b_ref[...]
        ls = jnp.minimum(f, 0.0) - jnp.log1p(jnp.exp(-jnp.abs(f)))
        tri = _tri(ts, lower=True)
        c = _dot_exact(tri, ls) + carry[0:1, :]
        c_ref[...] = c
        carry[...] = jnp.broadcast_to(c[ts - 1:ts, :], carry.shape)

    return pl.pallas_call(
        body, name=name, grid=(S // ts,),
        in_specs=[pl.BlockSpec((1, ts, LANES), lambda s: (0, s, colblk)), pl.BlockSpec((1, LANES), lambda s: (0, 0))],
        out_specs=pl.BlockSpec((ts, LANES), lambda s: (s, 0)),
        out_shape=jax.ShapeDtypeStruct((S, LANES), F32),
        scratch_shapes=[pltpu.VMEM((SUBLANES, LANES), F32)],
        compiler_params=_params("arbitrary"),
    )(kvf, b_pad)


def _gate_bwd(dc, kvf, b_pad, colblk, name):
    S = kvf.shape[1]
    ts = _tile(S, 512, SUBLANES)
    n_s = S // ts

    def body(dc_ref, f_ref, b_ref, df_ref, db_ref, carry):
        @pl.when(pl.program_id(0) == 0)
        def _():
            carry[...] = jnp.zeros_like(carry)
            db_ref[...] = jnp.zeros_like(db_ref)

        tri = _tri(ts, lower=False)
        dls = _dot_exact(tri, dc_ref[...]) + carry[0:1, :]
        f = f_ref[0] + b_ref[...]
        df = dls * jax.nn.sigmoid(-f)
        df_ref[...] = df
        db_ref[...] += jnp.broadcast_to(jnp.sum(df, axis=0, keepdims=True), db_ref.shape)
        carry[...] = jnp.broadcast_to(dls[0:1, :], carry.shape)

    df, db = pl.pallas_call(
        body, name=name, grid=(n_s,),
        in_specs=[pl.BlockSpec((ts, LANES), lambda s: (n_s - 1 - s, 0)),
                  pl.BlockSpec((1, ts, LANES), lambda s: (0, n_s - 1 - s, colblk)),
                  pl.BlockSpec((1, LANES), lambda s: (0, 0))],
        out_specs=[pl.BlockSpec((ts, LANES), lambda s: (n_s - 1 - s, 0)),
                   pl.BlockSpec((SUBLANES, LANES), lambda s: (0, 0))],
        out_shape=[jax.ShapeDtypeStruct((S, LANES), F32), jax.ShapeDtypeStruct((SUBLANES, LANES), F32)],
        scratch_shapes=[pltpu.VMEM((SUBLANES, LANES), F32)],
        compiler_params=_params("arbitrary"),
    )(dc, kvf, b_pad)
    return df, db[0]


def _attn_tile(S):
    return _tile(S, 512, LANES)


def _split_heads(v, lo):
    zero = jnp.zeros_like(v)
    return jnp.where(lo, v, zero), jnp.where(lo, zero, v)


def _augment(base, c, mode, hd, name):
    S, D = base.shape
    ts = _tile(S, 512, 2 * SUBLANES)

    def body(b_ref, c_ref, o0_ref, o1_ref):
        lane = lax.broadcasted_iota(jnp.int32, (ts, LANES), 1)
        cc = c_ref[...] * LOG2E
        for t in range(D // LANES):
            cols = slice(t * LANES, (t + 1) * LANES)
            bt = b_ref[:, cols]
            for h, o_ref in ((0, o0_ref), (1, o1_ref)):
                first = hd if h == 0 else 0
                keep = (lane < hd) if h == 0 else (lane >= hd)
                if mode == "v":
                    vals = (1.0,)
                else:
                    col = cc[:, 2 * t + h:2 * t + h + 1]
                    hi = col.astype(BF).astype(F32)
                    mid = (col - hi).astype(BF).astype(F32)
                    pieces = (hi, mid, col - hi - mid)
                    vals = pieces + (1.0, 1.0, 1.0) if mode == "q" else (1.0, 1.0, 1.0) + tuple(-v for v in pieces)
                aug = jnp.zeros((ts, LANES), F32)
                for i, v in enumerate(vals):
                    aug = jnp.where(lane == first + i, v, aug)
                o_ref[:, cols] = jnp.where(keep, bt, aug.astype(BF))

    spec = pl.BlockSpec((ts, D), lambda s: (s, 0))
    return pl.pallas_call(
        body, name=name, grid=(S // ts,),
        in_specs=[spec, pl.BlockSpec((ts, LANES), lambda s: (s, 0))], out_specs=[spec, spec],
        out_shape=[jax.ShapeDtypeStruct((S, D), BF)] * 2,
        compiler_params=_params("parallel"),
    )(base, c)


def _attn_fwd(qa, ka, va, qg, hd, name):
    S, D = qa[0].shape
    P = D // LANES
    tq = _attn_tile(S)
    nq = S // tq

    def body(q0_ref, q1_ref, k0_ref, k1_ref, v0_ref, v1_ref, g_ref, o_ref, og_ref, m_ref, l_ref, s_buf):
        qi = pl.program_id(1)
        lo = _head_masks((tq, LANES), hd)
        qh = (q0_ref[...], q1_ref[...])
        k_refs, v_refs = (k0_ref, k1_ref), (v0_ref, v1_ref)
        causal = lax.broadcasted_iota(jnp.int32, (tq, tq), 1) <= lax.broadcasted_iota(jnp.int32, (tq, tq), 0)

        def scores(ki, slot):
            off = pl.multiple_of(ki * tq, tq)
            for h in range(2):
                s_buf[slot, h] = lax.dot_general(qh[h], k_refs[h][pl.ds(off, tq), :], NT_DIMS,
                                                 preferred_element_type=F32)

        def consume(ki, slot, carry, masked):
            off = pl.multiple_of(ki * tq, tq)
            out = []
            for h in range(2):
                m, acc = carry[h]
                s = s_buf[slot, h]
                if masked:
                    s = jnp.where(causal, s, -jnp.inf)
                m_new = jnp.maximum(m, jnp.ceil(jnp.max(s, axis=-1, keepdims=True)))
                p = jnp.exp2(s - m_new)
                acc = jnp.exp2(m - m_new) * acc + jnp.dot(p.astype(BF), v_refs[h][pl.ds(off, tq), :],
                                                          preferred_element_type=F32)
                out.append((m_new, acc))
            return tuple(out)

        def step(j, carry):
            scores(2 * j + 1, 1)
            carry = consume(2 * j, 0, carry, False)
            scores(2 * j + 2, 0)
            return consume(2 * j + 1, 1, carry, False)

        def finish_even(carry):
            return consume(qi, 0, carry, True)

        def finish_odd(carry):
            scores(qi, 1)
            return consume(qi, 1, consume(qi - 1, 0, carry, False), True)

        init = tuple((jnp.full((tq, 1), -jnp.inf, F32), jnp.zeros((tq, LANES), F32)) for _ in range(2))
        scores(0, 0)
        carry = lax.fori_loop(0, qi // 2, step, init)
        (m0, a0), (m1, a1) = lax.cond(qi % 2 == 0, finish_even, finish_odd, carry)
        l0, l1 = a0[:, hd:hd + 1], a1[:, 0:1]
        o = jnp.where(lo, a0 / l0, a1 / l1)
        o_ref[...] = o
        og_ref[...] = (o * jax.nn.sigmoid(g_ref[0])).astype(BF)
        lane2 = lax.broadcasted_iota(jnp.int32, (tq, 2), 1)
        m_ref[0] = jnp.where(lane2 == 0, m0, m1)
        l_ref[0] = jnp.where(lane2 == 0, l0, l1)

    tile = pl.BlockSpec((tq, LANES), lambda p, i: (i, p))
    whole = pl.BlockSpec((S, LANES), lambda p, i: (0, p))
    stat = pl.BlockSpec((1, tq, 2), lambda p, i: (p, i, 0))
    return pl.pallas_call(
        body, name=name, grid=(P, nq),
        in_specs=[tile, tile, whole, whole, whole, whole, pl.BlockSpec((1, tq, LANES), lambda p, i: (1, i, p))],
        out_specs=[tile, tile, stat, stat],
        out_shape=[jax.ShapeDtypeStruct((S, D), F32), jax.ShapeDtypeStruct((S, D), BF),
                   jax.ShapeDtypeStruct((P, S, 2), F32), jax.ShapeDtypeStruct((P, S, 2), F32)],
        scratch_shapes=[pltpu.VMEM((2, 2, tq, tq), F32)],
        compiler_params=_params("parallel", "arbitrary"),
    )(*qa, *ka, *va, qg)


def _attn_out_bwd(dog, o, qg, l, hd, name):
    S, D = o.shape
    P = D // LANES
    ts = _tile(S, 512, 2 * SUBLANES)

    def body(dog_ref, o_ref, g_ref, l_ref, do_ref, dg_ref, e_ref):
        lo = _head_masks((ts, LANES), hd)
        lane2 = lax.broadcasted_iota(jnp.int32, (ts, 2), 1)
        for t in range(P):
            cols = slice(t * LANES, (t + 1) * LANES)
            sg = jax.nn.sigmoid(g_ref[0, :, cols])
            dog_t, o_t, l_t = dog_ref[:, cols], o_ref[:, cols], l_ref[t]
            g = (dog_t * sg / jnp.where(lo, l_t[:, 0:1], l_t[:, 1:2])).astype(BF)
            do_ref[:, cols] = g
            dg_ref[:, cols] = (dog_t * o_t * sg * (1.0 - sg)).astype(BF)
            prod = g.astype(F32) * o_t
            e0 = jnp.sum(jnp.where(lo, prod, 0.0), axis=-1, keepdims=True)
            e1 = jnp.sum(jnp.where(lo, 0.0, prod), axis=-1, keepdims=True)
            e_ref[t] = jnp.where(lane2 == 0, e0, e1)

    rows = pl.BlockSpec((ts, D), lambda s: (s, 0))
    stat = pl.BlockSpec((P, ts, 2), lambda s: (0, s, 0))
    return pl.pallas_call(
        body, name=name, grid=(S // ts,),
        in_specs=[rows, rows, pl.BlockSpec((1, ts, D), lambda s: (1, s, 0)), stat],
        out_specs=[rows, rows, stat],
        out_shape=[jax.ShapeDtypeStruct((S, D), BF), jax.ShapeDtypeStruct((S, D), BF),
                   jax.ShapeDtypeStruct((P, S, 2), F32)],
        compiler_params=_params("parallel"),
    )(dog, o, qg, l)


def _attn_bwd(qa, ka, vb, g, m_row, e_row, hd, name):
    S, D = vb.shape
    P = D // LANES
    tk = _attn_tile(S)
    nk = S // tk
    scale = hd ** -0.5

    def body(q0_ref, q1_ref, g_ref, k0_ref, k1_ref, v_ref, m_ref, e_ref, dq_ref, dk_ref, dv_ref, dc_ref,
             st_buf, dp_buf):
        ki = pl.program_id(1)

        @pl.when(ki == 0)
        def _():
            dq_ref[...] = jnp.zeros_like(dq_ref)

        lo = _head_masks((tk, LANES), hd)
        kh = (k0_ref[...], k1_ref[...])
        q_refs = (q0_ref, q1_ref)
        vh = _split_heads(v_ref[...], lo)
        causal_t = lax.broadcasted_iota(jnp.int32, (tk, tk), 0) <= lax.broadcasted_iota(jnp.int32, (tk, tk), 1)

        def stage(qi, slot):
            off = pl.multiple_of(qi * tk, tk)
            gb = g_ref[pl.ds(off, tk), :]
            for h in range(2):
                st_buf[slot, h] = lax.dot_general(kh[h], q_refs[h][pl.ds(off, tk), :], NT_DIMS,
                                                  preferred_element_type=F32)
                dp_buf[slot, h] = lax.dot_general(vh[h], gb, NT_DIMS, preferred_element_type=F32)

        def consume(qi, slot, carry, masked):
            off = pl.multiple_of(qi * tk, tk)
            gb = g_ref[pl.ds(off, tk), :]
            m_t, e_t = m_ref[0, qi], e_ref[0, qi]
            out, dq_parts = [], []
            for h in range(2):
                dk, dv, dc = carry[h]
                qb = q_refs[h][pl.ds(off, tk), :]
                pt = jnp.exp2(st_buf[slot, h] - m_t[h:h + 1, :])
                if masked:
                    pt = jnp.where(causal_t, pt, 0.0)
                pb = pt.astype(BF)
                dv = dv + jnp.dot(pb, gb, preferred_element_type=F32)
                dst = pb.astype(F32) * (dp_buf[slot, h] - e_t[h:h + 1, :])
                db = dst.astype(BF)
                dk = dk + jnp.dot(db, qb, preferred_element_type=F32)
                dc = dc - jnp.sum(dst, axis=-1, keepdims=True)
                dq_parts.append(lax.dot_general(db, kh[h], TN_DIMS, preferred_element_type=F32))
                out.append((dk, dv, dc))
            dq_ref[pl.ds(off, tk), :] += jnp.where(lo, dq_parts[0], dq_parts[1]) * scale
            return tuple(out)

        n_after = nk - 1 - ki

        def step(j, carry):
            b = ki + 1 + 2 * j
            stage(b + 1, 0)
            carry = consume(b, 1, carry, False)
            stage(b + 2, 1)
            return consume(b + 1, 0, carry, False)

        def rest_one(carry):
            return consume(nk - 1, 1, carry, False)

        def rest_two(carry):
            stage(nk - 1, 0)
            return consume(nk - 1, 0, consume(nk - 2, 1, carry, False), False)

        init = tuple((jnp.zeros((tk, LANES), F32), jnp.zeros((tk, LANES), F32), jnp.zeros((tk, 1), F32))
                     for _ in range(2))
        stage(ki, 0)
        stage(jnp.minimum(ki + 1, nk - 1), 1)
        carry = consume(ki, 0, init, True)
        carry = lax.fori_loop(0, (n_after - 1) // 2, step, carry)
        which = jnp.where(n_after == 0, 0, 2 - n_after % 2)
        (dk0, dv0, dc0), (dk1, dv1, dc1) = lax.switch(which, [lambda c: c, rest_one, rest_two], carry)
        dk_ref[...] = jnp.where(lo, dk0, dk1) * (1.0 / LOG2E)
        dv_ref[...] = jnp.where(lo, dv0, dv1)
        lane2 = lax.broadcasted_iota(jnp.int32, (tk, 2), 1)
        dc_ref[0] = jnp.where(lane2 == 0, dc0, dc1)

    tile = pl.BlockSpec((tk, LANES), lambda p, i: (i, p))
    whole = pl.BlockSpec((S, LANES), lambda p, i: (0, p))
    row_spec = pl.BlockSpec((1, nk, 2, tk), lambda p, i: (p, 0, 0, 0))
    return pl.pallas_call(
        body, name=name, grid=(P, nk),
        in_specs=[whole, whole, whole, tile, tile, tile, row_spec, row_spec],
        out_specs=[whole, tile, tile, pl.BlockSpec((1, tk, 2), lambda p, i: (p, i, 0))],
        out_shape=[jax.ShapeDtypeStruct((S, D), F32), jax.ShapeDtypeStruct((S, D), F32),
                   jax.ShapeDtypeStruct((S, D), F32), jax.ShapeDtypeStruct((P, S, 2), F32)],
        scratch_shapes=[pltpu.VMEM((2, 2, tk, tk), F32), pltpu.VMEM((2, 2, tk, tk), F32)],
        compiler_params=_params("parallel", "arbitrary"),
    )(*qa, g, *ka, vb, m_row, e_row)


def _attn_dq(qa, ka, vb, do, lse, dvec, hd, name):
    S, D = vb.shape
    P = D // LANES
    tq = _attn_tile(S)
    nq = S // tq
    scale = hd ** -0.5

    def body(q0_ref, q1_ref, k0_ref, k1_ref, v_ref, do_ref, lse_ref, dv_ref, dq_ref, dv2_ref):
        qi = pl.program_id(1)
        lo = _head_masks((tq, LANES), hd)
        qh = (q0_ref[...], q1_ref[...])
        k_refs = (k0_ref, k1_ref)
        doh = _split_heads(do_ref[...], lo)
        lse_t, dv_t = lse_ref[0], dv_ref[0]
        causal = lax.broadcasted_iota(jnp.int32, (tq, tq), 1) <= lax.broadcasted_iota(jnp.int32, (tq, tq), 0)

        def block(ki, carry, masked):
            off = pl.multiple_of(ki * tq, tq)
            vt = v_ref[pl.ds(off, tq), :]
            out = []
            for h in range(2):
                kb = k_refs[h][pl.ds(off, tq), :]
                s = lax.dot_general(qh[h], kb, NT_DIMS, preferred_element_type=F32)
                p = jnp.exp2(s - lse_t[:, h:h + 1])
                if masked:
                    p = jnp.where(causal, p, 0.0)
                dp = lax.dot_general(doh[h], vt, NT_DIMS, preferred_element_type=F32)
                ds = p * (dp - dv_t[:, h:h + 1])
                acc, rs = carry[h]
                out.append((acc + jnp.dot(ds.astype(BF), kb, preferred_element_type=F32),
                            rs + jnp.sum(ds, axis=-1, keepdims=True)))
            return tuple(out)

        init = tuple((jnp.zeros((tq, LANES), F32), jnp.zeros((tq, 1), F32)) for _ in range(2))
        carry = lax.fori_loop(0, qi, lambda ki, c: block(ki, c, False), init)
        (d0, r0), (d1, r1) = block(qi, carry, True)
        dq_ref[...] = jnp.where(lo, d0, d1) * scale
        lane2 = lax.broadcasted_iota(jnp.int32, (tq, 2), 1)
        dv2_ref[0] = dv_t + jnp.where(lane2 == 0, r0, r1)

    tile = pl.BlockSpec((tq, LANES), lambda p, i: (i, p))
    whole = pl.BlockSpec((S, LANES), lambda p, i: (0, p))
    stat = pl.BlockSpec((1, tq, 2), lambda p, i: (p, i, 0))
    return pl.pallas_call(
        body, name=name, grid=(P, nq),
        in_specs=[tile, tile, whole, whole, whole, tile, stat, stat],
        out_specs=[tile, stat],
        out_shape=[jax.ShapeDtypeStruct((S, D), F32), jax.ShapeDtypeStruct((P, S, 2), F32)],
        compiler_params=_params("parallel", "arbitrary"),
    )(*qa, *ka, vb, do, lse, dvec)


def _attn_dkv(qa, ka, vb, do, lse_row, dvec_row, hd, name):
    S, D = vb.shape
    P = D // LANES
    tk = _attn_tile(S)
    nk = S // tk

    def body(q0_ref, q1_ref, do_ref, k0_ref, k1_ref, v_ref, lse_ref, dvr_ref, dk_ref, dv_ref, dc_ref):
        ki = pl.program_id(1)
        lo = _head_masks((tk, LANES), hd)
        kh = (k0_ref[...], k1_ref[...])
        q_refs = (q0_ref, q1_ref)
        vh = _split_heads(v_ref[...], lo)
        causal_t = lax.broadcasted_iota(jnp.int32, (tk, tk), 0) <= lax.broadcasted_iota(jnp.int32, (tk, tk), 1)

        def block(qi, carry, masked):
            off = pl.multiple_of(qi * tk, tk)
            dob = do_ref[pl.ds(off, tk), :]
            lse_t, dv_t = lse_ref[0, qi], dvr_ref[0, qi]
            out = []
            for h in range(2):
                dk, dv, dc = carry[h]
                qb = q_refs[h][pl.ds(off, tk), :]
                st = lax.dot_general(kh[h], qb, NT_DIMS, preferred_element_type=F32)
                pt = jnp.exp2(st - lse_t[h:h + 1, :])
                if masked:
                    pt = jnp.where(causal_t, pt, 0.0)
                dv = dv + jnp.dot(pt.astype(BF), dob, preferred_element_type=F32)
                dpt = lax.dot_general(vh[h], dob, NT_DIMS, preferred_element_type=F32)
                dst = pt * (dpt - dv_t[h:h + 1, :])
                dk = dk + jnp.dot(dst.astype(BF), qb, preferred_element_type=F32)
                dc = dc - jnp.sum(dst, axis=-1, keepdims=True)
                out.append((dk, dv, dc))
            return tuple(out)

        init = tuple((jnp.zeros((tk, LANES), F32), jnp.zeros((tk, LANES), F32), jnp.zeros((tk, 1), F32))
                     for _ in range(2))
        carry = block(ki, init, True)
        (dk0, dv0, dc0), (dk1, dv1, dc1) = lax.fori_loop(ki + 1, nk, lambda qi, c: block(qi, c, False), carry)
        dk_ref[...] = jnp.where(lo, dk0, dk1) * (1.0 / LOG2E)
        dv_ref[...] = jnp.where(lo, dv0, dv1)
        lane2 = lax.broadcasted_iota(jnp.int32, (tk, 2), 1)
        dc_ref[0] = jnp.where(lane2 == 0, dc0, dc1)

    tile = pl.BlockSpec((tk, LANES), lambda p, i: (i, p))
    whole = pl.BlockSpec((S, LANES), lambda p, i: (0, p))
    row_spec = pl.BlockSpec((1, nk, 2, tk), lambda p, i: (p, 0, 0, 0))
    return pl.pallas_call(
        body, name=name, grid=(P, nk),
        in_specs=[whole, whole, whole, tile, tile, tile, row_spec, row_spec],
        out_specs=[tile, tile, pl.BlockSpec((1, tk, 2), lambda p, i: (p, i, 0))],
        out_shape=[jax.ShapeDtypeStruct((S, D), F32), jax.ShapeDtypeStruct((S, D), F32),
                   jax.ShapeDtypeStruct((P, S, 2), F32)],
        compiler_params=_params("parallel", "arbitrary"),
    )(*qa, do, *ka, vb, lse_row, dvec_row)


def _loss_head(y, t, name):
    S, D = y.shape
    ts = _tile(S, 512, SUBLANES)

    def body(y_ref, t_ref, dy_ref, l_ref):
        @pl.when(pl.program_id(0) == 0)
        def _():
            l_ref[...] = jnp.zeros_like(l_ref)

        e = y_ref[...] - t_ref[...]
        dy_ref[...] = e * (1.0 / D)
        part = 0.5 * jnp.sum(jnp.mean(e * e, axis=-1, keepdims=True), axis=0, keepdims=True)
        l_ref[...] += jnp.broadcast_to(part, l_ref.shape)

    return pl.pallas_call(
        body, name=name, grid=(S // ts,),
        in_specs=[pl.BlockSpec((ts, D), lambda s: (s, 0)), pl.BlockSpec((ts, D), lambda s: (s, 0))],
        out_specs=[pl.BlockSpec((ts, D), lambda s: (s, 0)), pl.BlockSpec((SUBLANES, LANES), lambda s: (0, 0))],
        out_shape=[jax.ShapeDtypeStruct((S, D), F32), jax.ShapeDtypeStruct((SUBLANES, LANES), F32)],
        compiler_params=_params("arbitrary"),
    )(y, t)


def _adamw(w, g, m, v, name):
    shape = w.shape
    cols = shape[-1]
    as2d = lambda a: a.reshape(-1, cols)
    rows = as2d(w).shape[0]
    tr = _tile(rows, 256, SUBLANES) if rows % SUBLANES == 0 else rows
    c1 = 1.0 - ADAM_B1 ** ADAM_STEP
    c2 = 1.0 - ADAM_B2 ** ADAM_STEP

    def body(w_ref, g_ref, m_ref, v_ref, d_ref, nm_ref, nv_ref):
        gg = g_ref[...]
        nm = ADAM_B1 * m_ref[...] + (1.0 - ADAM_B1) * gg
        nv = ADAM_B2 * v_ref[...] + (1.0 - ADAM_B2) * (gg * gg)
        d_ref[...] = -ADAM_LR * ((nm / c1) / (jnp.sqrt(nv / c2) + ADAM_EPS) + ADAM_WD * w_ref[...])
        nm_ref[...] = nm
        nv_ref[...] = nv

    spec = pl.BlockSpec((tr, cols), lambda r: (r, 0))
    outs = pl.pallas_call(
        body, name=name, grid=(rows // tr,), in_specs=[spec] * 4, out_specs=[spec] * 3,
        out_shape=[jax.ShapeDtypeStruct((rows, cols), F32)] * 3,
        compiler_params=_params("parallel"),
    )(as2d(w), as2d(g), as2d(m), as2d(v))
    return tuple(o.reshape(shape) for o in outs)


def _place():
    return lax.axis_index("x"), lax.axis_index("y"), lax.axis_index("c")


def _other_chips(x, y):
    return [(1 - x, y), (x, 1 - y), (1 - x, 1 - y)]


def _remote(src, dst, send_sems, recv_sems, k, to):
    return pltpu.make_async_remote_copy(src_ref=src, dst_ref=dst, send_sem=send_sems.at[k], recv_sem=recv_sems.at[k],
                                        device_id=to, device_id_type=MESH)


def _allgather_weights(flat):
    R, W = flat.shape
    rh = R // 2

    def body(src, dst, send_sems, recv_sems):
        x, y, c = _place()
        me = 2 * x + y
        sib = (x, y, 1 - c)
        chips = _other_chips(x, y)
        half = pl.ds(pl.multiple_of(c * rh, 16), rh)
        other = pl.ds(pl.multiple_of((1 - c) * rh, 16), rh)
        first = [_remote(src.at[half], dst.at[me, half], send_sems, recv_sems, j, (cx, cy, c))
                 for j, (cx, cy) in enumerate(chips)]
        for cp in first:
            cp.start()
        passed = []
        for j, (cx, cy) in enumerate(chips):
            landed = dst.at[2 * cx + cy, half]
            _remote(landed, landed, send_sems, recv_sems, j, sib).wait_recv()
            cp = _remote(landed, landed, send_sems, recv_sems, 3 + j, sib)
            cp.start()
            passed.append(cp)
        for j, (cx, cy) in enumerate(chips):
            landed = dst.at[2 * cx + cy, other]
            _remote(landed, landed, send_sems, recv_sems, 3 + j, sib).wait_recv()
        for cp in first + passed:
            cp.wait_send()

    out = pl.pallas_call(
        body, name="allgather_weights", in_specs=[ANY], out_specs=ANY,
        out_shape=jax.ShapeDtypeStruct((N_CHIPS, R, W), flat.dtype),
        scratch_shapes=[pltpu.SemaphoreType.DMA((6,)), pltpu.SemaphoreType.DMA((6,))],
    )(flat)
    x, y, _ = _place()
    return lax.dynamic_update_slice_in_dim(out, flat[None], 2 * x + y, axis=0)


def _pair_exchange(g):
    n, _, rh, W = g.shape

    def body(g_ref, t_ref, send_sems, recv_sems):
        x, y, c = _place()
        cps = [_remote(g_ref.at[k, 1 - c], t_ref.at[k], send_sems, recv_sems, k, (x, y, 1 - c)) for k in range(n)]
        for cp in cps:
            cp.start()
        for cp in cps:
            cp.wait()

    return pl.pallas_call(
        body, name="grad_pair_exchange", in_specs=[ANY], out_specs=ANY,
        out_shape=jax.ShapeDtypeStruct((n, rh, W), g.dtype),
        scratch_shapes=[pltpu.SemaphoreType.DMA((n,)), pltpu.SemaphoreType.DMA((n,))],
    )(g)


def _pair_add(g, t, c):
    n, _, rh, W = g.shape
    tr = _tile(rh, 256, 2 * SUBLANES)

    def body(c_ref, g_ref, t_ref, o_ref):
        o_ref[...] = (g_ref[0] + t_ref[...]).astype(BF)

    return pl.pallas_call(
        body, name="grad_pair_add",
        grid_spec=pltpu.PrefetchScalarGridSpec(
            num_scalar_prefetch=1, grid=(n, rh // tr),
            in_specs=[pl.BlockSpec((1, 1, tr, W), lambda k, i, c_ref: (k, c_ref[0], i, 0)),
                      pl.BlockSpec((1, tr, W), lambda k, i, c_ref: (k, i, 0))],
            out_specs=pl.BlockSpec((1, tr, W), lambda k, i, c_ref: (k, i, 0))),
        out_shape=jax.ShapeDtypeStruct((n, rh, W), BF),
        compiler_params=_params("parallel", "parallel"),
    )(c.reshape(1).astype(jnp.int32), g, t)


def _chip_exchange(a):
    _, rh, W = a.shape

    def body(a_ref, t_ref, send_sems, recv_sems):
        x, y, c = _place()
        cps = [_remote(a_ref.at[2 * cx + cy], t_ref.at[j], send_sems, recv_sems, j, (cx, cy, c))
               for j, (cx, cy) in enumerate(_other_chips(x, y))]
        for cp in cps:
            cp.start()
        for cp in cps:
            cp.wait()

    return pl.pallas_call(
        body, name="grad_chip_exchange", in_specs=[ANY], out_specs=ANY,
        out_shape=jax.ShapeDtypeStruct((3, rh, W), a.dtype),
        scratch_shapes=[pltpu.SemaphoreType.DMA((3,)), pltpu.SemaphoreType.DMA((3,))],
    )(a)


def _chip_add(g, t1, t2, c, me):
    _, _, rh, W = g.shape
    tr = _tile(rh, 256, 2 * SUBLANES)

    def body(c_ref, me_ref, g_ref, t1_ref, t2_ref, o_ref):
        own = g_ref[0, 0] + t1_ref[0]
        o_ref[...] = own + t2_ref[0].astype(F32) + t2_ref[1].astype(F32) + t2_ref[2].astype(F32)

    return pl.pallas_call(
        body, name="grad_chip_add",
        grid_spec=pltpu.PrefetchScalarGridSpec(
            num_scalar_prefetch=2, grid=(rh // tr,),
            in_specs=[pl.BlockSpec((1, 1, tr, W), lambda i, c_ref, me_ref: (me_ref[0], c_ref[0], i, 0)),
                      pl.BlockSpec((1, tr, W), lambda i, c_ref, me_ref: (me_ref[0], i, 0)),
                      pl.BlockSpec((3, tr, W), lambda i, c_ref, me_ref: (0, i, 0))],
            out_specs=pl.BlockSpec((tr, W), lambda i, c_ref, me_ref: (i, 0))),
        out_shape=jax.ShapeDtypeStruct((rh, W), F32),
        compiler_params=_params("parallel"),
    )(c.reshape(1).astype(jnp.int32), me.reshape(1).astype(jnp.int32), g, t1, t2)


def _pair_share(h):
    rh, W = h.shape

    def body(h_ref, f_ref, send_sems, recv_sems):
        x, y, c = _place()
        cp = _remote(h_ref, f_ref, send_sems, recv_sems, 0, (x, y, 1 - c))
        cp.start()
        cp.wait()

    return pl.pallas_call(
        body, name="grad_pair_share", in_specs=[ANY], out_specs=ANY,
        out_shape=jax.ShapeDtypeStruct((rh, W), h.dtype),
        scratch_shapes=[pltpu.SemaphoreType.DMA((1,)), pltpu.SemaphoreType.DMA((1,))],
    )(h)


def _allreduce_small(pack, name):
    rows, W = pack.shape

    def body(p_ref, o_ref, buf, send_sems, recv_sems):
        x, y, c = _place()
        me = 4 * x + 2 * y + c
        buf[me] = p_ref[...]
        cps = []
        for r in range(1, 8):
            fx, fy, fc = (r >> 2) & 1, (r >> 1) & 1, r & 1
            to = (1 - x if fx else x, 1 - y if fy else y, 1 - c if fc else c)
            cps.append(_remote(p_ref, buf.at[me], send_sems, recv_sems, r - 1, to))
        for cp in cps:
            cp.start()
        for r in range(1, 8):
            fx, fy, fc = (r >> 2) & 1, (r >> 1) & 1, r & 1
            frm = 4 * (1 - x if fx else x) + 2 * (1 - y if fy else y) + (1 - c if fc else c)
            _remote(p_ref, buf.at[frm], send_sems, recv_sems, r - 1, (x, y, c)).wait_recv()
        for cp in cps:
            cp.wait_send()
        acc = buf[0]
        for i in range(1, 8):
            acc = acc + buf[i]
        o_ref[...] = acc

    return pl.pallas_call(
        body, name=name, in_specs=[VMEM], out_specs=VMEM,
        out_shape=jax.ShapeDtypeStruct((rows, W), F32),
        scratch_shapes=[pltpu.VMEM((8, rows, W), F32), pltpu.SemaphoreType.DMA((7,)), pltpu.SemaphoreType.DMA((7,))],
    )(pack)


def _flatten(arrs, dtype):
    flat = jnp.concatenate([a.reshape(-1).astype(dtype) for a in arrs])
    per = FLAT_W * FLAT_ROW_MULT
    total = -(-flat.shape[0] // per) * per
    return jnp.pad(flat, (0, total - flat.shape[0])).reshape(-1, FLAT_W)


def _unflatten(flat2d, shapes):
    out, off = [], 0
    for shp in shapes:
        n = 1
        for d in shp:
            n *= d
        out.append(flat2d[..., off:off + n].reshape(flat2d.shape[:-1] + tuple(shp)))
        off += n
    return out


def _join_cols(g):
    nd = g.ndim
    return jnp.moveaxis(g, 0, nd - 2).reshape(g.shape[1:-1] + (N_CHIPS * g.shape[-1],))


def _split_cols(full):
    c = full.shape[-1] // N_CHIPS
    return jnp.moveaxis(full.reshape(full.shape[:-1] + (N_CHIPS, c)), -2, 0)


def _join_rows(g):
    return jnp.moveaxis(g, 0, 1).reshape(g.shape[1], N_CHIPS * g.shape[2], g.shape[3])


def _split_rows(full):
    L, r4, D = full.shape
    return jnp.moveaxis(full.reshape(L, N_CHIPS, r4 // N_CHIPS, D), 1, 0)


def _row_layout(a, tq):
    P, S, _ = a.shape
    return a.reshape(P, S // tq, tq, 2).transpose(0, 1, 3, 2)


def _pad_row(v, width=FLAT_W):
    flat = v.reshape(-1)
    rows = -(-flat.shape[0] // width)
    return jnp.pad(flat, (0, rows * width - flat.shape[0]))


def kernel(x, attn_norm, ffn_norm, a_w_in, a_conv, a_w_out, kv_norm, w_kvf, b_f, k_norm, b_w_qg, q_norm, b_w_out, ffn_w_up, ffn_conv, ffn_w_down, loss_target, m_attn_norm, m_ffn_norm, m_a_w_in, m_a_conv, m_a_w_out, m_kv_norm, m_w_kvf, m_b_f, m_k_norm, m_b_w_qg, m_q_norm, m_b_w_out, m_ffn_w_up, m_ffn_conv, m_ffn_w_down, v_attn_norm, v_ffn_norm, v_a_w_in, v_a_conv, v_a_w_out, v_kv_norm, v_w_kvf, v_b_f, v_k_norm, v_b_w_qg, v_q_norm, v_b_w_out, v_ffn_w_up, v_ffn_conv, v_ffn_w_down):
    xs = x[0]
    S, D = xs.shape
    H, hd = b_f.shape[0], k_norm.shape[0]
    depth = attn_norm.shape[0]
    n_a = a_w_in.shape[0]
    P = D // LANES
    assert LANES == 2 * hd and H * hd == D, "the attention kernels hold two heads per lane tile"
    mx, my, mc = _place()
    chip = 2 * mx + my

    big = [a_w_in, a_w_out, w_kvf, b_w_qg, b_w_out, ffn_w_up, ffn_w_down]
    big_shapes = [w.shape for w in big]
    gathered = _allgather_weights(_flatten(big, BF))
    g_in, g_out, g_kvf, g_qg, g_bout, g_up, g_down = _unflatten(gathered.reshape(N_CHIPS, -1), big_shapes)
    wa_in, wb_qg, w_up = _join_cols(g_in), _join_cols(g_qg), _join_cols(g_up)
    wa_out, wb_out, w_down = _join_rows(g_out), _join_rows(g_bout), _join_rows(g_down)
    kvf_cols = 2 * D + LANES
    wkvf = jnp.pad(_join_cols(g_kvf), ((0, 0), (0, kvf_cols - (2 * D + H))))

    def placed(shard):
        full = jnp.zeros(shard.shape[:-1] + (N_CHIPS, shard.shape[-1]), F32)
        full = lax.dynamic_update_slice_in_dim(full, shard[..., None, :], chip, axis=full.ndim - 2)
        return jnp.where(mc == 0, full, 0.0).reshape(-1)

    conv_pack = jnp.concatenate([_pad_row(placed(a_conv)), _pad_row(placed(ffn_conv))]).reshape(-1, FLAT_W)
    conv_full = _allreduce_small(conv_pack, "allgather_conv_taps").reshape(-1)
    n_ac = a_conv.size * N_CHIPS
    a_conv_f = conv_full[:n_ac].reshape(a_conv.shape[:-1] + (-1,))
    off = _pad_row(placed(a_conv)).shape[0]
    ffn_conv_f = conv_full[off:off + ffn_conv.size * N_CHIPS].reshape(ffn_conv.shape[:-1] + (-1,))
    F = ffn_conv_f.shape[-1]

    b_pad = jnp.pad(b_f, (0, LANES - H)).reshape(1, LANES)
    gate_blk = 2 * D // LANES
    tq = _attn_tile(S)
    scale = hd ** -0.5

    saved = []
    cur = xs
    kv = None
    for l in range(depth):
        rec = {"x_in": cur}
        if l < n_a:
            proj, xn = _norm_matmul(cur, attn_norm[l], wa_in[l], 3, BF, f"a_in_{l}")
            z = _mixer_mid_fwd(proj, a_conv_f[l], f"a_mid_{l}")
            mid = _matmul_residual(z, wa_out[l], cur, f"a_out_{l}")
            rec.update(proj=proj, xn=xn, z=z)
        else:
            j = l - n_a
            if kv is None:
                kvf, hn = _norm_matmul(cur, kv_norm, wkvf, 1, F32, "kvf_proj")
                kn = _headnorm(kvf, 0, 0, k_norm, 1.0, D, "k_norm")
                vb = kvf[0, :, D:2 * D].astype(BF)
                cgate = _gate_fwd(kvf, b_pad, gate_blk, "gate_cumsum")
                kv = dict(kvf=kvf, hn=hn, vb=vb, cgate=cgate, x_in=cur, dk=[], dv=[], dc=[],
                          ka=_augment(kn, cgate, "k", hd, "k_augment"), va=_augment(vb, cgate, "v", hd, "v_augment"))
            qg, xn = _norm_matmul(cur, attn_norm[l], wb_qg[j], 2, F32, f"qg_proj_{j}")
            qn = _headnorm(qg, 0, 0, q_norm[j], scale * LOG2E, D, f"q_norm_{j}")
            qa = _augment(qn, kv["cgate"], "q", hd, f"q_augment_{j}")
            o, og, m_max, l_sum = _attn_fwd(qa, kv["ka"], kv["va"], qg, hd, f"attn_fwd_{j}")
            mid = _matmul_residual(og, wb_out[j], cur, f"b_out_{j}")
            rec.update(qg=qg, xn=xn, qa=qa, o=o, og=og, m=m_max, l=l_sum)
        up, xn2 = _norm_matmul(mid, ffn_norm[l], w_up[l], 2, BF, f"ffn_up_{l}")
        z2 = _ffn_mid_fwd(up, ffn_conv_f[l], f"ffn_mid_{l}")
        cur = _matmul_residual(z2, w_down[l], mid, f"ffn_down_{l}")
        rec.update(x_mid=mid, up=up, xn2=xn2, z2=z2)
        saved.append(rec)

    dy, loss_part = _loss_head(cur, loss_target[0], "loss_head")

    g_attn_norm, g_ffn_norm = [None] * depth, [None] * depth
    g_a_in, g_a_conv, g_a_out = [None] * n_a, [None] * n_a, [None] * n_a
    g_qg, g_qn, g_bo = [None] * (depth - n_a), [None] * (depth - n_a), [None] * (depth - n_a)
    g_up, g_fc, g_down = [None] * depth, [None] * depth, [None] * depth
    for l in reversed(range(depth)):
        rec = saved[l]
        dz = _matmul_nt(dy, w_down[l], f"ffn_down_bwd_{l}")
        dup, g_fc[l] = _ffn_mid_bwd(rec["up"], dz, ffn_conv_f[l], f"ffn_mid_bwd_{l}")
        g_down[l] = _wgrad(rec["z2"], dy[None], f"ffn_down_wgrad_{l}")
        g_up[l] = _wgrad(rec["xn2"], dup, f"ffn_up_wgrad_{l}")
        dy, g_ffn_norm[l] = _dnorm(dup, w_up[l], rec["x_mid"], ffn_norm[l], dy, f"ffn_up_bwd_{l}")
        if l < n_a:
            dz = _matmul_nt(dy, wa_out[l], f"a_out_bwd_{l}")
            dproj, g_a_conv[l] = _mixer_mid_bwd(rec["proj"], dz, a_conv_f[l], f"a_mid_bwd_{l}")
            g_a_out[l] = _wgrad(rec["z"], dy[None], f"a_out_wgrad_{l}")
            g_a_in[l] = _wgrad(rec["xn"], dproj, f"a_in_wgrad_{l}")
            dy, g_attn_norm[l] = _dnorm(dproj, wa_in[l], rec["x_in"], attn_norm[l], dy, f"a_in_bwd_{l}")
        else:
            j = l - n_a
            dog = _matmul_nt(dy, wb_out[j], f"b_out_bwd_{j}")
            g_out, dgate, evec = _attn_out_bwd(dog, rec["o"], rec["qg"], rec["l"], hd, f"attn_gate_bwd_{j}")
            g_bo[j] = _wgrad(rec["og"], dy[None], f"b_out_wgrad_{j}")
            dqn, dk, dv, dc = _attn_bwd(rec["qa"], kv["ka"], kv["vb"], g_out, _row_layout(rec["m"], tq),
                                        _row_layout(evec, tq), hd, f"attn_bwd_{j}")
            kv["dk"].append(dk)
            kv["dv"].append(dv)
            kv["dc"].append(dc)
            dq_pre, g_qn[j] = _headnorm_bwd(rec["qg"], 0, 0, q_norm[j], [dqn], D, f"q_norm_bwd_{j}")
            dqg = jnp.stack([dq_pre, dgate])
            g_qg[j] = _wgrad(rec["xn"], dqg, f"qg_wgrad_{j}")
            dy, g_attn_norm[l] = _dnorm(dqg, wb_qg[j], rec["x_in"], attn_norm[l], dy, f"qg_bwd_{j}")
            if l == n_a:
                dk_s, g_k_norm = _headnorm_bwd(kv["kvf"], 0, 0, k_norm, kv["dk"], D, "k_norm_bwd")
                dv_s = functools.reduce(jnp.add, kv["dv"]).astype(BF)
                dc_sum = functools.reduce(jnp.add, kv["dc"])
                dc_pad = jnp.pad(dc_sum.transpose(1, 0, 2).reshape(S, H), ((0, 0), (0, LANES - H)))
                df, db = _gate_bwd(dc_pad, kv["kvf"], b_pad, gate_blk, "gate_bwd")
                dkvf = jnp.concatenate([dk_s, dv_s, df.astype(BF)], axis=1)[None]
                g_kvf = _wgrad(kv["hn"], dkvf, "kvf_wgrad")[:, :2 * D + H]
                g_b_f = db[:H]
                dy, g_kv_norm = _dnorm(dkvf, wkvf, kv["x_in"], kv_norm, dy, "kvf_bwd")
    grad_x = dy[None]

    full_grads = [_split_cols(jnp.stack(g_a_in)), _split_rows(jnp.stack(g_a_out)), _split_cols(g_kvf),
                  _split_cols(jnp.stack(g_qg)), _split_rows(jnp.stack(g_bo)), _split_cols(jnp.stack(g_up)),
                  _split_rows(jnp.stack(g_down))]
    gflat = jnp.stack([_flatten([g[k] for g in full_grads], F32) for k in range(N_CHIPS)])
    R = gflat.shape[1]
    g4 = gflat.reshape(N_CHIPS, 2, R // 2, FLAT_W)
    from_sibling = _pair_exchange(g4)
    from_chips = _chip_exchange(_pair_add(g4, from_sibling, mc))
    mine = _chip_add(g4, from_sibling, from_chips, mc, chip)
    theirs = _pair_share(mine)
    shard_flat = jnp.where(mc == 0, jnp.stack([mine, theirs]), jnp.stack([theirs, mine])).reshape(-1)
    big_grads = _unflatten(shard_flat, big_shapes)

    small = [loss_part[0, :1], jnp.stack(g_attn_norm), jnp.stack(g_ffn_norm), g_kv_norm, g_b_f, g_k_norm,
             jnp.stack(g_qn), jnp.stack(g_a_conv), jnp.stack(g_fc)]
    small_sum = _allreduce_small(jnp.concatenate([_pad_row(s) for s in small]).reshape(-1, FLAT_W),
                                 "allreduce_small_grads").reshape(-1)
    parts, off = [], 0
    for s in small:
        parts.append(small_sum[off:off + s.size].reshape(s.shape))
        off += _pad_row(s).shape[0]
    loss = parts[0][0]
    gr_attn_norm, gr_ffn_norm, gr_kv_norm, gr_b_f, gr_k_norm, gr_q_norm, gr_a_conv_full, gr_ffn_conv_full = parts[1:]

    def my_cols(full):
        c = full.shape[-1] // N_CHIPS
        return lax.dynamic_slice_in_dim(full, chip * c, c, axis=full.ndim - 1)

    gr_a_in, gr_a_out, gr_kvf, gr_qg, gr_bo, gr_up, gr_down = big_grads
    grads = [gr_attn_norm, gr_ffn_norm, gr_a_in, my_cols(gr_a_conv_full), gr_a_out, gr_kv_norm, gr_kvf, gr_b_f,
             gr_k_norm, gr_qg, gr_q_norm, gr_bo, gr_up, my_cols(gr_ffn_conv_full), gr_down]
    weights = [attn_norm, ffn_norm, a_w_in, a_conv, a_w_out, kv_norm, w_kvf, b_f, k_norm, b_w_qg, q_norm, b_w_out,
               ffn_w_up, ffn_conv, ffn_w_down]
    ms = [m_attn_norm, m_ffn_norm, m_a_w_in, m_a_conv, m_a_w_out, m_kv_norm, m_w_kvf, m_b_f, m_k_norm, m_b_w_qg,
          m_q_norm, m_b_w_out, m_ffn_w_up, m_ffn_conv, m_ffn_w_down]
    vs = [v_attn_norm, v_ffn_norm, v_a_w_in, v_a_conv, v_a_w_out, v_kv_norm, v_w_kvf, v_b_f, v_k_norm, v_b_w_qg,
          v_q_norm, v_b_w_out, v_ffn_w_up, v_ffn_conv, v_ffn_w_down]
    deltas, new_ms, new_vs = [], [], []
    for i, (w, g, m, v) in enumerate(zip(weights, grads, ms, vs)):
        d, nm, nv = _adamw(w, g, m, v, f"adamw_{i}")
        deltas.append(d)
        new_ms.append(nm)
        new_vs.append(nv)
    return (loss, grad_x, *grads, *deltas, *new_ms, *new_vs)
```

```python
import functools

import jax
import jax.numpy as jnp
from jax import lax
from jax.experimental import pallas as pl
from jax.experimental.pallas import tpu as pltpu

F32 = jnp.float32
BF = jnp.bfloat16
LANES = 128
SUBLANES = 8
RMS_EPS = 1e-6
LOG2E = 1.4426950408889634
FLAT_W = 1024
FLAT_ROW_MULT = 512
N_CHIPS = 4
CONV_W = 3
HALO = SUBLANES

ADAM_LR = 0.001
ADAM_B1 = 0.9
ADAM_B2 = 0.999
ADAM_EPS = 1e-08
ADAM_WD = 0.01
ADAM_STEP = 10

MESH = pl.DeviceIdType.MESH
ANY = pl.BlockSpec(memory_space=pl.ANY)
VMEM = pl.BlockSpec(memory_space=pltpu.VMEM)
NT_DIMS = (((1,), (1,)), ((), ()))
TN_DIMS = (((0,), (0,)), ((), ()))


def _tile(n, pref, mult=LANES):
    t = (min(pref, n) // mult) * mult
    while t >= mult:
        if n % t == 0:
            break
        t -= mult
    if t < mult or (t * 4 < pref and n <= 4 * pref):
        return n
    return t


def _params(*sem):
    return pltpu.CompilerParams(dimension_semantics=sem)


def _norm_matmul(x, g, w, parts, out_dtype, name):
    S, D = x.shape
    C = w.shape[1] // parts
    ts, tn = _tile(S, 512, SUBLANES), _tile(C, 1408)
    npc = C // tn

    def body(x_ref, g_ref, w_ref, o_ref, xn_ref):
        @pl.when(pl.program_id(1) == 0)
        def _():
            xf = x_ref[...]
            r = lax.rsqrt(jnp.mean(xf * xf, axis=-1, keepdims=True) + RMS_EPS)
            xn_ref[...] = (xf * r * g_ref[...]).astype(BF)

        o_ref[0] = jnp.dot(xn_ref[...], w_ref[...], preferred_element_type=F32).astype(out_dtype)

    return pl.pallas_call(
        body, name=name, grid=(S // ts, parts * npc),
        in_specs=[pl.BlockSpec((ts, D), lambda s, n: (s, 0)),
                  pl.BlockSpec((1, D), lambda s, n: (0, 0)),
                  pl.BlockSpec((D, tn), lambda s, n: (0, n))],
        out_specs=[pl.BlockSpec((1, ts, tn), lambda s, n: (n // npc, s, n % npc)),
                   pl.BlockSpec((ts, D), lambda s, n: (s, 0))],
        out_shape=[jax.ShapeDtypeStruct((parts, S, C), out_dtype), jax.ShapeDtypeStruct((S, D), BF)],
        compiler_params=_params("parallel", "arbitrary"),
    )(x, g.reshape(1, D), w)


def _shift_down(u, prev, k, row):
    r = pltpu.roll(u, k, 0)
    for j in range(k):
        r = jnp.where(row == j, prev[HALO - k + j:HALO - k + j + 1, :], r)
    return r


def _shift_up(d, nxt, k, row):
    n = d.shape[0]
    r = pltpu.roll(d, n - k, 0)
    for j in range(k):
        r = jnp.where(row == n - k + j, nxt[j:j + 1, :], r)
    return r


def _conv3(u, prev, w, row):
    return _shift_down(u, prev, 2, row) * w[0:1] + _shift_down(u, prev, 1, row) * w[1:2] + u * w[2:3]


def _conv3_t(d, nxt, w, row):
    return d * w[2:3] + _shift_up(d, nxt, 1, row) * w[1:2] + _shift_up(d, nxt, 2, row) * w[0:1]


def _tap_rows(t0, t1, t2):
    row = lax.broadcasted_iota(jnp.int32, (SUBLANES, t0.shape[1]), 0)
    return jnp.where(row == 0, t0, jnp.where(row == 1, t1, jnp.where(row == 2, t2, 0.0)))


def _pad_conv(cw):
    return jnp.pad(cw, ((0, SUBLANES - CONV_W), (0, 0)))


def _mixer_mid_fwd(proj, cw, name):
    _, S, C = proj.shape
    ts, tc = _tile(S, 512, SUBLANES), _tile(C, 1024)

    def body(b_ref, c_ref, h_ref, cw_ref, z_ref, carry):
        @pl.when(pl.program_id(1) == 0)
        def _():
            carry[...] = jnp.zeros_like(carry)

        u = c_ref[0].astype(F32) * h_ref[0].astype(F32)
        row = lax.broadcasted_iota(jnp.int32, u.shape, 0)
        cv = _conv3(u, carry[...], cw_ref[...], row)
        z_ref[...] = (b_ref[0].astype(F32) * cv).astype(BF)
        carry[...] = u[ts - HALO:ts, :]

    part = lambda p: pl.BlockSpec((1, ts, tc), lambda c, s: (p, s, c))
    return pl.pallas_call(
        body, name=name, grid=(C // tc, S // ts),
        in_specs=[part(0), part(1), part(2), pl.BlockSpec((SUBLANES, tc), lambda c, s: (0, c))],
        out_specs=pl.BlockSpec((ts, tc), lambda c, s: (s, c)),
        out_shape=jax.ShapeDtypeStruct((S, C), BF),
        scratch_shapes=[pltpu.VMEM((HALO, tc), F32)],
        compiler_params=_params("parallel", "arbitrary"),
    )(proj, proj, proj, _pad_conv(cw))


def _ffn_up_fwd(x, g, w, cw, name):
    S, D = x.shape
    C = w.shape[1] // 2
    ts, tc = _tile(S, 512, SUBLANES), _tile(C, 1408)
    nc = C // tc

    def body(x_ref, g_ref, wa_ref, wg_ref, cw_ref, up_ref, xn_ref, z_ref, carry):
        s, c = pl.program_id(0), pl.program_id(1)

        @pl.when(c == 0)
        def _():
            xf = x_ref[...]
            r = lax.rsqrt(jnp.mean(xf * xf, axis=-1, keepdims=True) + RMS_EPS)
            xn_ref[...] = (xf * r * g_ref[...]).astype(BF)

        @pl.when(s == 0)
        def _():
            carry[c] = jnp.zeros((HALO, tc), F32)

        xn = xn_ref[...]
        a_b = jnp.dot(xn, wa_ref[...], preferred_element_type=F32).astype(BF)
        g_b = jnp.dot(xn, wg_ref[...], preferred_element_type=F32).astype(BF)
        up_ref[0] = a_b
        up_ref[1] = g_b
        a_pre = a_b.astype(F32)
        row = lax.broadcasted_iota(jnp.int32, a_pre.shape, 0)
        a = _conv3(a_pre, carry[c], cw_ref[...], row)
        z_ref[...] = (a * jax.nn.sigmoid(a) * g_b.astype(F32)).astype(BF)
        carry[c] = a_pre[ts - HALO:ts, :]

    return pl.pallas_call(
        body, name=name, grid=(S // ts, nc),
        in_specs=[pl.BlockSpec((ts, D), lambda s, c: (s, 0)),
                  pl.BlockSpec((1, D), lambda s, c: (0, 0)),
                  pl.BlockSpec((D, tc), lambda s, c: (0, c)),
                  pl.BlockSpec((D, tc), lambda s, c: (0, nc + c)),
                  pl.BlockSpec((SUBLANES, tc), lambda s, c: (0, c))],
        out_specs=[pl.BlockSpec((2, ts, tc), lambda s, c: (0, s, c)),
                   pl.BlockSpec((ts, D), lambda s, c: (s, 0)),
                   pl.BlockSpec((ts, tc), lambda s, c: (s, c))],
        out_shape=[jax.ShapeDtypeStruct((2, S, C), BF), jax.ShapeDtypeStruct((S, D), BF),
                   jax.ShapeDtypeStruct((S, C), BF)],
        scratch_shapes=[pltpu.VMEM((nc, HALO, tc), F32)],
        compiler_params=_params("arbitrary", "arbitrary"),
    )(x, g.reshape(1, D), w, w, _pad_conv(cw))


def _halo_specs(p, ts, tc, n_s):
    per = ts // HALO
    last = n_s * per - 1
    before = pl.BlockSpec((1, HALO, tc), lambda c, s: (p, jnp.maximum(s * per - 1, 0), c))
    after = pl.BlockSpec((1, HALO, tc), lambda c, s: (p, jnp.minimum((s + 1) * per, last), c))
    return before, after


def _mixer_mid_bwd(proj, dz, cw, name):
    _, S, C = proj.shape
    ts, tc = _tile(S, 512, SUBLANES), _tile(C, 1024)
    n_s = S // ts

    def body(b_ref, c_ref, h_ref, dz_ref, cp_ref, hp_ref, bn_ref, dzn_ref, cw_ref, d_ref, dcw_ref):
        s = pl.program_id(1)
        w = cw_ref[...]
        b, c, h = b_ref[0].astype(F32), c_ref[0].astype(F32), h_ref[0].astype(F32)
        dz_t = dz_ref[0]
        row = lax.broadcasted_iota(jnp.int32, b.shape, 0)
        u = c * h
        prev = jnp.where(s > 0, cp_ref[0].astype(F32) * hp_ref[0].astype(F32), 0.0)
        u1, u2 = _shift_down(u, prev, 1, row), _shift_down(u, prev, 2, row)
        cv = u2 * w[0:1] + u1 * w[1:2] + u * w[2:3]
        dcv = dz_t * b
        nxt = jnp.where(s < n_s - 1, dzn_ref[0] * bn_ref[0].astype(F32), 0.0)
        du = _conv3_t(dcv, nxt, w, row)
        d_ref[0] = (dz_t * cv).astype(BF)
        d_ref[1] = (du * h).astype(BF)
        d_ref[2] = (du * c).astype(BF)
        part = _tap_rows(jnp.sum(dcv * u2, axis=0, keepdims=True), jnp.sum(dcv * u1, axis=0, keepdims=True),
                         jnp.sum(dcv * u, axis=0, keepdims=True))

        @pl.when(s == 0)
        def _():
            dcw_ref[...] = part

        @pl.when(s > 0)
        def _():
            dcw_ref[...] += part

    part_spec = lambda p: pl.BlockSpec((1, ts, tc), lambda c, s: (p, s, c))
    c_before, _ = _halo_specs(1, ts, tc, n_s)
    h_before, _ = _halo_specs(2, ts, tc, n_s)
    _, b_after = _halo_specs(0, ts, tc, n_s)
    _, dz_after = _halo_specs(0, ts, tc, n_s)
    dproj, dcw = pl.pallas_call(
        body, name=name, grid=(C // tc, n_s),
        in_specs=[part_spec(0), part_spec(1), part_spec(2), part_spec(0), c_before, h_before, b_after, dz_after,
                  pl.BlockSpec((SUBLANES, tc), lambda c, s: (0, c))],
        out_specs=[pl.BlockSpec((3, ts, tc), lambda c, s: (0, s, c)),
                   pl.BlockSpec((SUBLANES, tc), lambda c, s: (0, c))],
        out_shape=[jax.ShapeDtypeStruct((3, S, C), BF), jax.ShapeDtypeStruct((SUBLANES, C), F32)],
        compiler_params=_params("parallel", "arbitrary"),
    )(proj, proj, proj, dz[None], proj, proj, proj, dz[None], _pad_conv(cw))
    return dproj, dcw[:CONV_W]


def _ffn_mid_bwd(up, dy, w_down, cw, name):
    _, S, C = up.shape
    D = dy.shape[1]
    ts, tc = _tile(S, 512, SUBLANES), _tile(C, 1408)
    n_s = S // ts
    per = ts // HALO

    def body(a_ref, g_ref, dy_ref, w_ref, ap_ref, cw_ref, d_ref, dcw_ref, carry):
        i = pl.program_id(1)
        w = cw_ref[...]
        dz = lax.dot_general(dy_ref[...].astype(BF), w_ref[...], NT_DIMS, preferred_element_type=F32)
        a_pre, g = a_ref[0].astype(F32), g_ref[0].astype(F32)
        row = lax.broadcasted_iota(jnp.int32, a_pre.shape, 0)
        prev = jnp.where(i < n_s - 1, ap_ref[0].astype(F32), 0.0)
        a1, a2 = _shift_down(a_pre, prev, 1, row), _shift_down(a_pre, prev, 2, row)
        a = a2 * w[0:1] + a1 * w[1:2] + a_pre * w[2:3]
        sg = jax.nn.sigmoid(a)
        da = dz * g * (sg * (1.0 + a * (1.0 - sg)))
        nxt = jnp.where(i > 0, carry[...], 0.0)
        d_ref[0] = _conv3_t(da, nxt, w, row).astype(BF)
        d_ref[1] = (dz * (a * sg)).astype(BF)
        carry[...] = da[0:HALO, :]
        part = _tap_rows(jnp.sum(da * a2, axis=0, keepdims=True), jnp.sum(da * a1, axis=0, keepdims=True),
                         jnp.sum(da * a_pre, axis=0, keepdims=True))

        @pl.when(i == 0)
        def _():
            dcw_ref[...] = part

        @pl.when(i > 0)
        def _():
            dcw_ref[...] += part

    tile = lambda p: pl.BlockSpec((1, ts, tc), lambda c, i: (p, n_s - 1 - i, c))
    dup, dcw = pl.pallas_call(
        body, name=name, grid=(C // tc, n_s),
        in_specs=[tile(0), tile(1),
                  pl.BlockSpec((ts, D), lambda c, i: (n_s - 1 - i, 0)),
                  pl.BlockSpec((tc, D), lambda c, i: (c, 0)),
                  pl.BlockSpec((1, HALO, tc), lambda c, i: (0, jnp.maximum((n_s - 1 - i) * per - 1, 0), c)),
                  pl.BlockSpec((SUBLANES, tc), lambda c, i: (0, c))],
        out_specs=[pl.BlockSpec((2, ts, tc), lambda c, i: (0, n_s - 1 - i, c)),
                   pl.BlockSpec((SUBLANES, tc), lambda c, i: (0, c))],
        out_shape=[jax.ShapeDtypeStruct((2, S, C), BF), jax.ShapeDtypeStruct((SUBLANES, C), F32)],
        scratch_shapes=[pltpu.VMEM((HALO, tc), F32)],
        compiler_params=_params("parallel", "arbitrary"),
    )(up, up, dy, w_down, up, _pad_conv(cw))
    return dup, dcw[:CONV_W]


def _matmul_residual(z, w, x, name):
    S, K = z.shape
    D = w.shape[1]
    ts = _tile(S, 512, SUBLANES)

    def body(z_ref, w_ref, x_ref, o_ref):
        o_ref[...] = x_ref[...] + jnp.dot(z_ref[...], w_ref[...], preferred_element_type=F32)

    return pl.pallas_call(
        body, name=name, grid=(S // ts,),
        in_specs=[pl.BlockSpec((ts, K), lambda s: (s, 0)), pl.BlockSpec((K, D), lambda s: (0, 0)),
                  pl.BlockSpec((ts, D), lambda s: (s, 0))],
        out_specs=pl.BlockSpec((ts, D), lambda s: (s, 0)),
        out_shape=jax.ShapeDtypeStruct((S, D), F32),
        compiler_params=_params("parallel"),
    )(z, w, x)


def _matmul_nt(a, w, name):
    S, K = a.shape
    N = w.shape[0]
    ts, tn = _tile(S, 512, SUBLANES), _tile(N, 1408)

    def body(a_ref, w_ref, o_ref, abf):
        @pl.when(pl.program_id(1) == 0)
        def _():
            abf[...] = a_ref[...].astype(BF)

        o_ref[...] = lax.dot_general(abf[...], w_ref[...], NT_DIMS, preferred_element_type=F32)

    return pl.pallas_call(
        body, name=name, grid=(S // ts, N // tn),
        in_specs=[pl.BlockSpec((ts, K), lambda s, n: (s, 0)), pl.BlockSpec((tn, K), lambda s, n: (n, 0))],
        out_specs=pl.BlockSpec((ts, tn), lambda s, n: (s, n)),
        out_shape=jax.ShapeDtypeStruct((S, N), F32),
        scratch_shapes=[pltpu.VMEM((ts, K), BF)],
        compiler_params=_params("parallel", "arbitrary"),
    )(a, w)


def _wgrad(a, b, name):
    S, M = a.shape
    P, _, C = b.shape
    tm, tn, tk = _tile(M, 1408), _tile(C, 1408), _tile(S, 512, SUBLANES)
    nnc = C // tn

    def body(a_ref, b_ref, o_ref):
        @pl.when(pl.program_id(2) == 0)
        def _():
            o_ref[...] = jnp.zeros_like(o_ref)

        o_ref[...] += lax.dot_general(a_ref[...], b_ref[0].astype(BF), TN_DIMS, preferred_element_type=F32)

    return pl.pallas_call(
        body, name=name, grid=(M // tm, P * nnc, S // tk),
        in_specs=[pl.BlockSpec((tk, tm), lambda m, n, k: (k, m)),
                  pl.BlockSpec((1, tk, tn), lambda m, n, k: (n // nnc, k, n % nnc))],
        out_specs=pl.BlockSpec((tm, tn), lambda m, n, k: (m, n)),
        out_shape=jax.ShapeDtypeStruct((M, P * C), F32),
        compiler_params=_params("parallel", "parallel", "arbitrary"),
    )(a, b)


def _dnorm(dp, w, x, g, dy, name):
    P, S, C = dp.shape
    D = x.shape[1]
    ts, tk = _tile(S, 512, SUBLANES), _tile(C, 1408)
    nkc = C // tk
    n_k = P * nkc

    def body(dp_ref, w_ref, x_ref, g_ref, dy_ref, dx_ref, dg_ref, acc):
        s, k = pl.program_id(0), pl.program_id(1)

        @pl.when(k == 0)
        def _():
            acc[...] = jnp.zeros_like(acc)

        acc[...] += lax.dot_general(dp_ref[0], w_ref[...], NT_DIMS, preferred_element_type=F32)

        @pl.when((s == 0) & (k == 0))
        def _():
            dg_ref[...] = jnp.zeros_like(dg_ref)

        @pl.when(k == n_k - 1)
        def _():
            xf = x_ref[...]
            r = lax.rsqrt(jnp.mean(xf * xf, axis=-1, keepdims=True) + RMS_EPS)
            xhat = xf * r
            dxn = acc[...]
            dxhat = dxn * g_ref[...]
            dx_ref[...] = dy_ref[...] + r * (dxhat - xhat * jnp.mean(dxhat * xhat, axis=-1, keepdims=True))
            dg_ref[...] += jnp.broadcast_to(jnp.sum(dxn * xhat, axis=0, keepdims=True), dg_ref.shape)

    dx, dg = pl.pallas_call(
        body, name=name, grid=(S // ts, n_k),
        in_specs=[pl.BlockSpec((1, ts, tk), lambda s, k: (k // nkc, s, k % nkc)),
                  pl.BlockSpec((D, tk), lambda s, k: (0, k)),
                  pl.BlockSpec((ts, D), lambda s, k: (s, 0)),
                  pl.BlockSpec((1, D), lambda s, k: (0, 0)),
                  pl.BlockSpec((ts, D), lambda s, k: (s, 0))],
        out_specs=[pl.BlockSpec((ts, D), lambda s, k: (s, 0)),
                   pl.BlockSpec((SUBLANES, D), lambda s, k: (0, 0))],
        out_shape=[jax.ShapeDtypeStruct((S, D), F32), jax.ShapeDtypeStruct((SUBLANES, D), F32)],
        scratch_shapes=[pltpu.VMEM((ts, D), F32)],
        compiler_params=_params("arbitrary", "arbitrary"),
    )(dp, w, x, g.reshape(1, D), dy)
    return dx, dg[0]


def _head_masks(shape, hd):
    lane = lax.broadcasted_iota(jnp.int32, shape, 1)
    return lane < hd


def _pair_sum(v, lo):
    s0 = jnp.sum(jnp.where(lo, v, 0.0), axis=-1, keepdims=True)
    s1 = jnp.sum(jnp.where(lo, 0.0, v), axis=-1, keepdims=True)
    return jnp.where(lo, s0, s1)


def _headnorm(src, part, colblk, w, scale, D, name):
    S = src.shape[1]
    hd = w.shape[0]
    ts = _tile(S, 512, SUBLANES)
    w2 = jnp.tile(w, LANES // hd).reshape(1, LANES)

    def body(x_ref, w_ref, o_ref):
        lo = _head_masks((ts, LANES), hd)
        for t in range(D // LANES):
            xt = x_ref[0, :, t * LANES:(t + 1) * LANES]
            r = lax.rsqrt(_pair_sum(xt * xt, lo) * (1.0 / hd) + RMS_EPS)
            o_ref[:, t * LANES:(t + 1) * LANES] = (xt * r * w_ref[...] * scale).astype(BF)

    return pl.pallas_call(
        body, name=name, grid=(S // ts,),
        in_specs=[pl.BlockSpec((1, ts, D), lambda s: (part, s, colblk)), pl.BlockSpec((1, LANES), lambda s: (0, 0))],
        out_specs=pl.BlockSpec((ts, D), lambda s: (s, 0)),
        out_shape=jax.ShapeDtypeStruct((S, D), BF),
        compiler_params=_params("parallel"),
    )(src, w2)


def _headnorm_bwd(src, part, colblk, w, dys, D, name):
    S = src.shape[1]
    hd = w.shape[0]
    ts = _tile(S, 512, SUBLANES)
    w2 = jnp.tile(w, LANES // hd).reshape(1, LANES)
    n_dy = len(dys)

    def body(x_ref, w_ref, *rest):
        dy_refs, dx_ref, dw_ref = rest[:n_dy], rest[n_dy], rest[n_dy + 1]

        @pl.when(pl.program_id(0) == 0)
        def _():
            dw_ref[...] = jnp.zeros_like(dw_ref)

        lo = _head_masks((ts, LANES), hd)
        for t in range(D // LANES):
            cols = slice(t * LANES, (t + 1) * LANES)
            xt = x_ref[0, :, cols]
            dy = dy_refs[0][:, cols]
            for other in dy_refs[1:]:
                dy = dy + other[:, cols]
            r = lax.rsqrt(_pair_sum(xt * xt, lo) * (1.0 / hd) + RMS_EPS)
            xhat = xt * r
            dxhat = dy * w_ref[...]
            mean = _pair_sum(dxhat * xhat, lo) * (1.0 / hd)
            dx_ref[:, cols] = (r * (dxhat - xhat * mean)).astype(BF)
            dw_ref[:, cols] += jnp.broadcast_to(jnp.sum(dy * xhat, axis=0, keepdims=True), (SUBLANES, LANES))

    dx, dw = pl.pallas_call(
        body, name=name, grid=(S // ts,),
        in_specs=[pl.BlockSpec((1, ts, D), lambda s: (part, s, colblk)), pl.BlockSpec((1, LANES), lambda s: (0, 0))]
        + [pl.BlockSpec((ts, D), lambda s: (s, 0))] * n_dy,
        out_specs=[pl.BlockSpec((ts, D), lambda s: (s, 0)), pl.BlockSpec((SUBLANES, D), lambda s: (0, 0))],
        out_shape=[jax.ShapeDtypeStruct((S, D), BF), jax.ShapeDtypeStruct((SUBLANES, D), F32)],
        compiler_params=_params("arbitrary"),
    )(src, w2, *dys)
    return dx, jnp.sum(dw[0].reshape(D // hd, hd), axis=0)


def _tri(n, lower):
    r, c = lax.broadcasted_iota(jnp.int32, (n, n), 0), lax.broadcasted_iota(jnp.int32, (n, n), 1)
    return jnp.where((c <= r) if lower else (c >= r), 1.0, 0.0).astype(BF)


def _dot_exact(t, v):
    hi = v.astype(BF)
    r1 = v - hi.astype(F32)
    mid = r1.astype(BF)
    lo = (r1 - mid.astype(F32)).astype(BF)
    dot = lambda u: jnp.dot(t, u, preferred_element_type=F32)
    return dot(hi) + dot(mid) + dot(lo)


def _gate_fwd(kvf, b_pad, colblk, name):
    S = kvf.shape[1]
    ts = _tile(S, 512, SUBLANES)

    def body(f_ref, b_ref, c_ref, carry):
        @pl.when(pl.program_id(0) == 0)
        def _():
            carry[...] = jnp.zeros_like(carry)

        f = f_ref[0] + b_ref[...]
        ls = jnp.minimum(f, 0.0) - jnp.log1p(jnp.exp(-jnp.abs(f)))
        tri = _tri(ts, lower=True)
        c = _dot_exact(tri, ls) + carry[0:1, :]
        c_ref[...] = c
        carry[...] = jnp.broadcast_to(c[ts - 1:ts, :], carry.shape)

    return pl.pallas_call(
        body, name=name, grid=(S // ts,),
        in_specs=[pl.BlockSpec((1, ts, LANES), lambda s: (0, s, colblk)), pl.BlockSpec((1, LANES), lambda s: (0, 0))],
        out_specs=pl.BlockSpec((ts, LANES), lambda s: (s, 0)),
        out_shape=jax.ShapeDtypeStruct((S, LANES), F32),
        scratch_shapes=[pltpu.VMEM((SUBLANES, LANES), F32)],
        compiler_params=_params("arbitrary"),
    )(kvf, b_pad)


def _gate_bwd(dc, kvf, b_pad, colblk, name):
    S = kvf.shape[1]
    ts = _tile(S, 512, SUBLANES)
    n_s = S // ts

    def body(dc_ref, f_ref, b_ref, df_ref, db_ref, carry):
        @pl.when(pl.program_id(0) == 0)
        def _():
            carry[...] = jnp.zeros_like(carry)
            db_ref[...] = jnp.zeros_like(db_ref)

        tri = _tri(ts, lower=False)
        dls = _dot_exact(tri, dc_ref[...]) + carry[0:1, :]
        f = f_ref[0] + b_ref[...]
        df = dls * jax.nn.sigmoid(-f)
        df_ref[...] = df
        db_ref[...] += jnp.broadcast_to(jnp.sum(df, axis=0, keepdims=True), db_ref.shape)
        carry[...] = jnp.broadcast_to(dls[0:1, :], carry.shape)

    df, db = pl.pallas_call(
        body, name=name, grid=(n_s,),
        in_specs=[pl.BlockSpec((ts, LANES), lambda s: (n_s - 1 - s, 0)),
                  pl.BlockSpec((1, ts, LANES), lambda s: (0, n_s - 1 - s, colblk)),
                  pl.BlockSpec((1, LANES), lambda s: (0, 0))],
        out_specs=[pl.BlockSpec((ts, LANES), lambda s: (n_s - 1 - s, 0)),
                   pl.BlockSpec((SUBLANES, LANES), lambda s: (0, 0))],
        out_shape=[jax.ShapeDtypeStruct((S, LANES), F32), jax.ShapeDtypeStruct((SUBLANES, LANES), F32)],
        scratch_shapes=[pltpu.VMEM((SUBLANES, LANES), F32)],
        compiler_params=_params("arbitrary"),
    )(dc, kvf, b_pad)
    return df, db[0]


def _attn_tile(S):
    return _tile(S, 512, LANES)


def _split_heads(v, lo):
    zero = jnp.zeros_like(v)
    return jnp.where(lo, v, zero), jnp.where(lo, zero, v)


def _augment(base, c, mode, hd, name):
    S, D = base.shape
    ts = _tile(S, 512, 2 * SUBLANES)

    def body(b_ref, c_ref, o0_ref, o1_ref):
        lane = lax.broadcasted_iota(jnp.int32, (ts, LANES), 1)
        cc = c_ref[...] * LOG2E
        for t in range(D // LANES):
            cols = slice(t * LANES, (t + 1) * LANES)
            bt = b_ref[:, cols]
            for h, o_ref in ((0, o0_ref), (1, o1_ref)):
                first = hd if h == 0 else 0
                keep = (lane < hd) if h == 0 else (lane >= hd)
                if mode == "v":
                    vals = (1.0,)
                else:
                    col = cc[:, 2 * t + h:2 * t + h + 1]
                    hi = col.astype(BF).astype(F32)
                    mid = (col - hi).astype(BF).astype(F32)
                    pieces = (hi, mid, col - hi - mid)
                    vals = pieces + (1.0, 1.0, 1.0) if mode == "q" else (1.0, 1.0, 1.0) + tuple(-v for v in pieces)
                aug = jnp.zeros((ts, LANES), F32)
                for i, v in enumerate(vals):
                    aug = jnp.where(lane == first + i, v, aug)
                o_ref[:, cols] = jnp.where(keep, bt, aug.astype(BF))

    spec = pl.BlockSpec((ts, D), lambda s: (s, 0))
    return pl.pallas_call(
        body, name=name, grid=(S // ts,),
        in_specs=[spec, pl.BlockSpec((ts, LANES), lambda s: (s, 0))], out_specs=[spec, spec],
        out_shape=[jax.ShapeDtypeStruct((S, D), BF)] * 2,
        compiler_params=_params("parallel"),
    )(base, c)


def _attn_fwd(qa, ka, va, qg, hd, name):
    S, D = qa[0].shape
    P = D // LANES
    tq = _attn_tile(S)
    nq = S // tq

    def body(q0_ref, q1_ref, k0_ref, k1_ref, v0_ref, v1_ref, g_ref, o_ref, og_ref, m_ref, l_ref, s_buf):
        qi = pl.program_id(1)
        lo = _head_masks((tq, LANES), hd)
        qh = (q0_ref[...], q1_ref[...])
        k_refs, v_refs = (k0_ref, k1_ref), (v0_ref, v1_ref)
        causal = lax.broadcasted_iota(jnp.int32, (tq, tq), 1) <= lax.broadcasted_iota(jnp.int32, (tq, tq), 0)

        def scores(ki, slot):
            off = pl.multiple_of(ki * tq, tq)
            for h in range(2):
                s_buf[slot, h] = lax.dot_general(qh[h], k_refs[h][pl.ds(off, tq), :], NT_DIMS,
                                                 preferred_element_type=F32)

        def consume(ki, slot, carry, masked):
            off = pl.multiple_of(ki * tq, tq)
            out = []
            for h in range(2):
                m, acc = carry[h]
                s = s_buf[slot, h]
                if masked:
                    s = jnp.where(causal, s, -jnp.inf)
                m_new = jnp.maximum(m, jnp.ceil(jnp.max(s, axis=-1, keepdims=True)))
                p = jnp.exp2(s - m_new)
                acc = jnp.exp2(m - m_new) * acc + jnp.dot(p.astype(BF), v_refs[h][pl.ds(off, tq), :],
                                                          preferred_element_type=F32)
                out.append((m_new, acc))
            return tuple(out)

        def step(j, carry):
            scores(2 * j + 1, 1)
            carry = consume(2 * j, 0, carry, False)
            scores(2 * j + 2, 0)
            return consume(2 * j + 1, 1, carry, False)

        def finish_even(carry):
            return consume(qi, 0, carry, True)

        def finish_odd(carry):
            scores(qi, 1)
            return consume(qi, 1, consume(qi - 1, 0, carry, False), True)

        init = tuple((jnp.full((tq, 1), -jnp.inf, F32), jnp.zeros((tq, LANES), F32)) for _ in range(2))
        scores(0, 0)
        carry = lax.fori_loop(0, qi // 2, step, init)
        (m0, a0), (m1, a1) = lax.cond(qi % 2 == 0, finish_even, finish_odd, carry)
        l0, l1 = a0[:, hd:hd + 1], a1[:, 0:1]
        o = jnp.where(lo, a0 / l0, a1 / l1)
        o_ref[...] = o
        og_ref[...] = (o * jax.nn.sigmoid(g_ref[0])).astype(BF)
        lane2 = lax.broadcasted_iota(jnp.int32, (tq, 2), 1)
        m_ref[0] = jnp.where(lane2 == 0, m0, m1)
        l_ref[0] = jnp.where(lane2 == 0, l0, l1)

    tile = pl.BlockSpec((tq, LANES), lambda p, i: (i, p))
    whole = pl.BlockSpec((S, LANES), lambda p, i: (0, p))
    stat = pl.BlockSpec((1, tq, 2), lambda p, i: (p, i, 0))
    return pl.pallas_call(
        body, name=name, grid=(P, nq),
        in_specs=[tile, tile, whole, whole, whole, whole, pl.BlockSpec((1, tq, LANES), lambda p, i: (1, i, p))],
        out_specs=[tile, tile, stat, stat],
        out_shape=[jax.ShapeDtypeStruct((S, D), F32), jax.ShapeDtypeStruct((S, D), BF),
                   jax.ShapeDtypeStruct((P, S, 2), F32), jax.ShapeDtypeStruct((P, S, 2), F32)],
        scratch_shapes=[pltpu.VMEM((2, 2, tq, tq), F32)],
        compiler_params=_params("parallel", "arbitrary"),
    )(*qa, *ka, *va, qg)


def _attn_out_bwd(dog, o, qg, l, hd, name):
    S, D = o.shape
    P = D // LANES
    ts = _tile(S, 512, 2 * SUBLANES)

    def body(dog_ref, o_ref, g_ref, l_ref, do_ref, dg_ref, e_ref):
        lo = _head_masks((ts, LANES), hd)
        lane2 = lax.broadcasted_iota(jnp.int32, (ts, 2), 1)
        for t in range(P):
            cols = slice(t * LANES, (t + 1) * LANES)
            sg = jax.nn.sigmoid(g_ref[0, :, cols])
            dog_t, o_t, l_t = dog_ref[:, cols], o_ref[:, cols], l_ref[t]
            g = (dog_t * sg / jnp.where(lo, l_t[:, 0:1], l_t[:, 1:2])).astype(BF)
            do_ref[:, cols] = g
            dg_ref[:, cols] = (dog_t * o_t * sg * (1.0 - sg)).astype(BF)
            prod = g.astype(F32) * o_t
            e0 = jnp.sum(jnp.where(lo, prod, 0.0), axis=-1, keepdims=True)
            e1 = jnp.sum(jnp.where(lo, 0.0, prod), axis=-1, keepdims=True)
            e_ref[t] = jnp.where(lane2 == 0, e0, e1)

    rows = pl.BlockSpec((ts, D), lambda s: (s, 0))
    stat = pl.BlockSpec((P, ts, 2), lambda s: (0, s, 0))
    return pl.pallas_call(
        body, name=name, grid=(S // ts,),
        in_specs=[rows, rows, pl.BlockSpec((1, ts, D), lambda s: (1, s, 0)), stat],
        out_specs=[rows, rows, stat],
        out_shape=[jax.ShapeDtypeStruct((S, D), BF), jax.ShapeDtypeStruct((S, D), BF),
                   jax.ShapeDtypeStruct((P, S, 2), F32)],
        compiler_params=_params("parallel"),
    )(dog, o, qg, l)


def _attn_bwd(qa, ka, vb, g, m_row, e_row, hd, name):
    S, D = vb.shape
    P = D // LANES
    tk = _attn_tile(S)
    nk = S // tk
    scale = hd ** -0.5

    def body(q0_ref, q1_ref, g_ref, k0_ref, k1_ref, v_ref, m_ref, e_ref, dq_ref, dk_ref, dv_ref, dc_ref,
             st_buf, dp_buf):
        ki = pl.program_id(1)

        @pl.when(ki == 0)
        def _():
            dq_ref[...] = jnp.zeros_like(dq_ref)

        lo = _head_masks((tk, LANES), hd)
        kh = (k0_ref[...], k1_ref[...])
        q_refs = (q0_ref, q1_ref)
        vh = _split_heads(v_ref[...], lo)
        causal_t = lax.broadcasted_iota(jnp.int32, (tk, tk), 0) <= lax.broadcasted_iota(jnp.int32, (tk, tk), 1)

        def stage(qi, slot):
            off = pl.multiple_of(qi * tk, tk)
            gb = g_ref[pl.ds(off, tk), :]
            for h in range(2):
                st_buf[slot, h] = lax.dot_general(kh[h], q_refs[h][pl.ds(off, tk), :], NT_DIMS,
                                                  preferred_element_type=F32)
                dp_buf[slot, h] = lax.dot_general(vh[h], gb, NT_DIMS, preferred_element_type=F32)

        def consume(qi, slot, carry, masked):
            off = pl.multiple_of(qi * tk, tk)
            gb = g_ref[pl.ds(off, tk), :]
            m_t, e_t = m_ref[0, qi], e_ref[0, qi]
            out, dq_parts = [], []
            for h in range(2):
                dk, dv, dc = carry[h]
                qb = q_refs[h][pl.ds(off, tk), :]
                pt = jnp.exp2(st_buf[slot, h] - m_t[h:h + 1, :])
                if masked:
                    pt = jnp.where(causal_t, pt, 0.0)
                pb = pt.astype(BF)
                dv = dv + jnp.dot(pb, gb, preferred_element_type=F32)
                dst = pb.astype(F32) * (dp_buf[slot, h] - e_t[h:h + 1, :])
                db = dst.astype(BF)
                dk = dk + jnp.dot(db, qb, preferred_element_type=F32)
                dc = dc - jnp.sum(dst, axis=-1, keepdims=True)
                dq_parts.append(lax.dot_general(db, kh[h], TN_DIMS, preferred_element_type=F32))
                out.append((dk, dv, dc))
            dq_ref[pl.ds(off, tk), :] += jnp.where(lo, dq_parts[0], dq_parts[1]) * scale
            return tuple(out)

        n_after = nk - 1 - ki

        def step(j, carry):
            b = ki + 1 + 2 * j
            stage(b + 1, 0)
            carry = consume(b, 1, carry, False)
            stage(b + 2, 1)
            return consume(b + 1, 0, carry, False)

        def rest_one(carry):
            return consume(nk - 1, 1, carry, False)

        def rest_two(carry):
            stage(nk - 1, 0)
            return consume(nk - 1, 0, consume(nk - 2, 1, carry, False), False)

        init = tuple((jnp.zeros((tk, LANES), F32), jnp.zeros((tk, LANES), F32), jnp.zeros((tk, 1), F32))
                     for _ in range(2))
        stage(ki, 0)
        stage(jnp.minimum(ki + 1, nk - 1), 1)
        carry = consume(ki, 0, init, True)
        carry = lax.fori_loop(0, (n_after - 1) // 2, step, carry)
        which = jnp.where(n_after == 0, 0, 2 - n_after % 2)
        (dk0, dv0, dc0), (dk1, dv1, dc1) = lax.switch(which, [lambda c: c, rest_one, rest_two], carry)
        dk_ref[...] = jnp.where(lo, dk0, dk1) * (1.0 / LOG2E)
        dv_ref[...] = jnp.where(lo, dv0, dv1)
        lane2 = lax.broadcasted_iota(jnp.int32, (tk, 2), 1)
        dc_ref[0] = jnp.where(lane2 == 0, dc0, dc1)

    tile = pl.BlockSpec((tk, LANES), lambda p, i: (i, p))
    whole = pl.BlockSpec((S, LANES), lambda p, i: (0, p))
    row_spec = pl.BlockSpec((1, nk, 2, tk), lambda p, i: (p, 0, 0, 0))
    return pl.pallas_call(
        body, name=name, grid=(P, nk),
        in_specs=[whole, whole, whole, tile, tile, tile, row_spec, row_spec],
        out_specs=[whole, tile, tile, pl.BlockSpec((1, tk, 2), lambda p, i: (p, i, 0))],
        out_shape=[jax.ShapeDtypeStruct((S, D), F32), jax.ShapeDtypeStruct((S, D), F32),
                   jax.ShapeDtypeStruct((S, D), F32), jax.ShapeDtypeStruct((P, S, 2), F32)],
        scratch_shapes=[pltpu.VMEM((2, 2, tk, tk), F32), pltpu.VMEM((2, 2, tk, tk), F32)],
        compiler_params=_params("parallel", "arbitrary"),
    )(*qa, g, *ka, vb, m_row, e_row)


def _loss_head(y, t, name):
    S, D = y.shape
    ts = _tile(S, 512, SUBLANES)

    def body(y_ref, t_ref, dy_ref, l_ref):
        @pl.when(pl.program_id(0) == 0)
        def _():
            l_ref[...] = jnp.zeros_like(l_ref)

        e = y_ref[...] - t_ref[...]
        dy_ref[...] = e * (1.0 / D)
        part = 0.5 * jnp.sum(jnp.mean(e * e, axis=-1, keepdims=True), axis=0, keepdims=True)
        l_ref[...] += jnp.broadcast_to(part, l_ref.shape)

    return pl.pallas_call(
        body, name=name, grid=(S // ts,),
        in_specs=[pl.BlockSpec((ts, D), lambda s: (s, 0)), pl.BlockSpec((ts, D), lambda s: (s, 0))],
        out_specs=[pl.BlockSpec((ts, D), lambda s: (s, 0)), pl.BlockSpec((SUBLANES, LANES), lambda s: (0, 0))],
        out_shape=[jax.ShapeDtypeStruct((S, D), F32), jax.ShapeDtypeStruct((SUBLANES, LANES), F32)],
        compiler_params=_params("arbitrary"),
    )(y, t)


def _adamw(w, g, m, v, name):
    shape = w.shape
    cols = shape[-1]
    as2d = lambda a: a.reshape(-1, cols)
    rows = as2d(w).shape[0]
    tr = _tile(rows, 256, SUBLANES) if rows % SUBLANES == 0 else rows
    c1 = 1.0 - ADAM_B1 ** ADAM_STEP
    c2 = 1.0 - ADAM_B2 ** ADAM_STEP

    def body(w_ref, g_ref, m_ref, v_ref, d_ref, nm_ref, nv_ref):
        gg = g_ref[...]
        nm = ADAM_B1 * m_ref[...] + (1.0 - ADAM_B1) * gg
        nv = ADAM_B2 * v_ref[...] + (1.0 - ADAM_B2) * (gg * gg)
        d_ref[...] = -ADAM_LR * ((nm / c1) / (jnp.sqrt(nv / c2) + ADAM_EPS) + ADAM_WD * w_ref[...])
        nm_ref[...] = nm
        nv_ref[...] = nv

    spec = pl.BlockSpec((tr, cols), lambda r: (r, 0))
    outs = pl.pallas_call(
        body, name=name, grid=(rows // tr,), in_specs=[spec] * 4, out_specs=[spec] * 3,
        out_shape=[jax.ShapeDtypeStruct((rows, cols), F32)] * 3,
        compiler_params=_params("parallel"),
    )(as2d(w), as2d(g), as2d(m), as2d(v))
    return tuple(o.reshape(shape) for o in outs)


def _place():
    return lax.axis_index("x"), lax.axis_index("y"), lax.axis_index("c")


def _other_chips(x, y):
    return [(1 - x, y), (x, 1 - y), (1 - x, 1 - y)]


def _remote(src, dst, send_sems, recv_sems, k, to):
    return pltpu.make_async_remote_copy(src_ref=src, dst_ref=dst, send_sem=send_sems.at[k], recv_sem=recv_sems.at[k],
                                        device_id=to, device_id_type=MESH)


def _allgather_weights(flat):
    R, W = flat.shape
    rh = R // 2

    def body(src, dst, send_sems, recv_sems):
        x, y, c = _place()
        me = 2 * x + y
        sib = (x, y, 1 - c)
        chips = _other_chips(x, y)
        half = pl.ds(pl.multiple_of(c * rh, 16), rh)
        other = pl.ds(pl.multiple_of((1 - c) * rh, 16), rh)
        first = [_remote(src.at[half], dst.at[me, half], send_sems, recv_sems, j, (cx, cy, c))
                 for j, (cx, cy) in enumerate(chips)]
        for cp in first:
            cp.start()
        passed = []
        for j, (cx, cy) in enumerate(chips):
            landed = dst.at[2 * cx + cy, half]
            _remote(landed, landed, send_sems, recv_sems, j, sib).wait_recv()
            cp = _remote(landed, landed, send_sems, recv_sems, 3 + j, sib)
            cp.start()
            passed.append(cp)
        for j, (cx, cy) in enumerate(chips):
            landed = dst.at[2 * cx + cy, other]
            _remote(landed, landed, send_sems, recv_sems, 3 + j, sib).wait_recv()
        for cp in first + passed:
            cp.wait_send()

    out = pl.pallas_call(
        body, name="allgather_weights", in_specs=[ANY], out_specs=ANY,
        out_shape=jax.ShapeDtypeStruct((N_CHIPS, R, W), flat.dtype),
        scratch_shapes=[pltpu.SemaphoreType.DMA((6,)), pltpu.SemaphoreType.DMA((6,))],
    )(flat)
    x, y, _ = _place()
    return lax.dynamic_update_slice_in_dim(out, flat[None], 2 * x + y, axis=0)


def _pair_exchange(g):
    n, _, rh, W = g.shape

    def body(g_ref, t_ref, send_sems, recv_sems):
        x, y, c = _place()
        cps = [_remote(g_ref.at[k, 1 - c], t_ref.at[k], send_sems, recv_sems, k, (x, y, 1 - c)) for k in range(n)]
        for cp in cps:
            cp.start()
        for cp in cps:
            cp.wait()

    return pl.pallas_call(
        body, name="grad_pair_exchange", in_specs=[ANY], out_specs=ANY,
        out_shape=jax.ShapeDtypeStruct((n, rh, W), g.dtype),
        scratch_shapes=[pltpu.SemaphoreType.DMA((n,)), pltpu.SemaphoreType.DMA((n,))],
    )(g)


def _pair_add(g, t, c):
    n, _, rh, W = g.shape
    tr = _tile(rh, 256, 2 * SUBLANES)

    def body(c_ref, g_ref, t_ref, o_ref):
        o_ref[...] = (g_ref[0] + t_ref[...]).astype(BF)

    return pl.pallas_call(
        body, name="grad_pair_add",
        grid_spec=pltpu.PrefetchScalarGridSpec(
            num_scalar_prefetch=1, grid=(n, rh // tr),
            in_specs=[pl.BlockSpec((1, 1, tr, W), lambda k, i, c_ref: (k, c_ref[0], i, 0)),
                      pl.BlockSpec((1, tr, W), lambda k, i, c_ref: (k, i, 0))],
            out_specs=pl.BlockSpec((1, tr, W), lambda k, i, c_ref: (k, i, 0))),
        out_shape=jax.ShapeDtypeStruct((n, rh, W), BF),
        compiler_params=_params("parallel", "parallel"),
    )(c.reshape(1).astype(jnp.int32), g, t)


def _chip_exchange(a):
    _, rh, W = a.shape

    def body(a_ref, t_ref, send_sems, recv_sems):
        x, y, c = _place()
        cps = [_remote(a_ref.at[2 * cx + cy], t_ref.at[j], send_sems, recv_sems, j, (cx, cy, c))
               for j, (cx, cy) in enumerate(_other_chips(x, y))]
        for cp in cps:
            cp.start()
        for cp in cps:
            cp.wait()

    return pl.pallas_call(
        body, name="grad_chip_exchange", in_specs=[ANY], out_specs=ANY,
        out_shape=jax.ShapeDtypeStruct((3, rh, W), a.dtype),
        scratch_shapes=[pltpu.SemaphoreType.DMA((3,)), pltpu.SemaphoreType.DMA((3,))],
    )(a)


def _chip_add(g, t1, t2, c, me):
    _, _, rh, W = g.shape
    tr = _tile(rh, 256, 2 * SUBLANES)

    def body(c_ref, me_ref, g_ref, t1_ref, t2_ref, o_ref):
        own = g_ref[0, 0] + t1_ref[0]
        o_ref[...] = own + t2_ref[0].astype(F32) + t2_ref[1].astype(F32) + t2_ref[2].astype(F32)

    return pl.pallas_call(
        body, name="grad_chip_add",
        grid_spec=pltpu.PrefetchScalarGridSpec(
            num_scalar_prefetch=2, grid=(rh // tr,),
            in_specs=[pl.BlockSpec((1, 1, tr, W), lambda i, c_ref, me_ref: (me_ref[0], c_ref[0], i, 0)),
                      pl.BlockSpec((1, tr, W), lambda i, c_ref, me_ref: (me_ref[0], i, 0)),
                      pl.BlockSpec((3, tr, W), lambda i, c_ref, me_ref: (0, i, 0))],
            out_specs=pl.BlockSpec((tr, W), lambda i, c_ref, me_ref: (i, 0))),
        out_shape=jax.ShapeDtypeStruct((rh, W), F32),
        compiler_params=_params("parallel"),
    )(c.reshape(1).astype(jnp.int32), me.reshape(1).astype(jnp.int32), g, t1, t2)


def _pair_share(h):
    rh, W = h.shape

    def body(h_ref, f_ref, send_sems, recv_sems):
        x, y, c = _place()
        cp = _remote(h_ref, f_ref, send_sems, recv_sems, 0, (x, y, 1 - c))
        cp.start()
        cp.wait()

    return pl.pallas_call(
        body, name="grad_pair_share", in_specs=[ANY], out_specs=ANY,
        out_shape=jax.ShapeDtypeStruct((rh, W), h.dtype),
        scratch_shapes=[pltpu.SemaphoreType.DMA((1,)), pltpu.SemaphoreType.DMA((1,))],
    )(h)


def _allreduce_small(pack, name):
    rows, W = pack.shape

    def body(p_ref, o_ref, buf, send_sems, recv_sems):
        x, y, c = _place()
        me = 4 * x + 2 * y + c
        buf[me] = p_ref[...]
        cps = []
        for r in range(1, 8):
            fx, fy, fc = (r >> 2) & 1, (r >> 1) & 1, r & 1
            to = (1 - x if fx else x, 1 - y if fy else y, 1 - c if fc else c)
            cps.append(_remote(p_ref, buf.at[me], send_sems, recv_sems, r - 1, to))
        for cp in cps:
            cp.start()
        for r in range(1, 8):
            fx, fy, fc = (r >> 2) & 1, (r >> 1) & 1, r & 1
            frm = 4 * (1 - x if fx else x) + 2 * (1 - y if fy else y) + (1 - c if fc else c)
            _remote(p_ref, buf.at[frm], send_sems, recv_sems, r - 1, (x, y, c)).wait_recv()
        for cp in cps:
            cp.wait_send()
        acc = buf[0]
        for i in range(1, 8):
            acc = acc + buf[i]
        o_ref[...] = acc

    return pl.pallas_call(
        body, name=name, in_specs=[VMEM], out_specs=VMEM,
        out_shape=jax.ShapeDtypeStruct((rows, W), F32),
        scratch_shapes=[pltpu.VMEM((8, rows, W), F32), pltpu.SemaphoreType.DMA((7,)), pltpu.SemaphoreType.DMA((7,))],
    )(pack)


def _flatten(arrs, dtype):
    flat = jnp.concatenate([a.reshape(-1).astype(dtype) for a in arrs])
    per = FLAT_W * FLAT_ROW_MULT
    total = -(-flat.shape[0] // per) * per
    return jnp.pad(flat, (0, total - flat.shape[0])).reshape(-1, FLAT_W)


def _unflatten(flat2d, shapes):
    out, off = [], 0
    for shp in shapes:
        n = 1
        for d in shp:
            n *= d
        out.append(flat2d[..., off:off + n].reshape(flat2d.shape[:-1] + tuple(shp)))
        off += n
    return out


def _join_cols(g):
    nd = g.ndim
    return jnp.moveaxis(g, 0, nd - 2).reshape(g.shape[1:-1] + (N_CHIPS * g.shape[-1],))


def _split_cols(full):
    c = full.shape[-1] // N_CHIPS
    return jnp.moveaxis(full.reshape(full.shape[:-1] + (N_CHIPS, c)), -2, 0)


def _join_rows(g):
    return jnp.moveaxis(g, 0, 1).reshape(g.shape[1], N_CHIPS * g.shape[2], g.shape[3])


def _split_rows(full):
    L, r4, D = full.shape
    return jnp.moveaxis(full.reshape(L, N_CHIPS, r4 // N_CHIPS, D), 1, 0)


def _row_layout(a, tq):
    P, S, _ = a.shape
    return a.reshape(P, S // tq, tq, 2).transpose(0, 1, 3, 2)


def _pad_row(v, width=FLAT_W):
    flat = v.reshape(-1)
    rows = -(-flat.shape[0] // width)
    return jnp.pad(flat, (0, rows * width - flat.shape[0]))


def kernel(x, attn_norm, ffn_norm, a_w_in, a_conv, a_w_out, kv_norm, w_kvf, b_f, k_norm, b_w_qg, q_norm, b_w_out, ffn_w_up, ffn_conv, ffn_w_down, loss_target, m_attn_norm, m_ffn_norm, m_a_w_in, m_a_conv, m_a_w_out, m_kv_norm, m_w_kvf, m_b_f, m_k_norm, m_b_w_qg, m_q_norm, m_b_w_out, m_ffn_w_up, m_ffn_conv, m_ffn_w_down, v_attn_norm, v_ffn_norm, v_a_w_in, v_a_conv, v_a_w_out, v_kv_norm, v_w_kvf, v_b_f, v_k_norm, v_b_w_qg, v_q_norm, v_b_w_out, v_ffn_w_up, v_ffn_conv, v_ffn_w_down):
    xs = x[0]
    S, D = xs.shape
    H, hd = b_f.shape[0], k_norm.shape[0]
    depth = attn_norm.shape[0]
    n_a = a_w_in.shape[0]
    P = D // LANES
    assert LANES == 2 * hd and H * hd == D, "the attention kernels hold two heads per lane tile"
    mx, my, mc = _place()
    chip = 2 * mx + my

    big = [a_w_in, a_w_out, w_kvf, b_w_qg, b_w_out, ffn_w_up, ffn_w_down]
    big_shapes = [w.shape for w in big]
    gathered = _allgather_weights(_flatten(big, BF))
    g_in, g_out, g_kvf, g_qg, g_bout, g_up, g_down = _unflatten(gathered.reshape(N_CHIPS, -1), big_shapes)
    wa_in, wb_qg, w_up = _join_cols(g_in), _join_cols(g_qg), _join_cols(g_up)
    wa_out, wb_out, w_down = _join_rows(g_out), _join_rows(g_bout), _join_rows(g_down)
    kvf_cols = 2 * D + LANES
    wkvf = jnp.pad(_join_cols(g_kvf), ((0, 0), (0, kvf_cols - (2 * D + H))))

    def placed(shard):
        full = jnp.zeros(shard.shape[:-1] + (N_CHIPS, shard.shape[-1]), F32)
        full = lax.dynamic_update_slice_in_dim(full, shard[..., None, :], chip, axis=full.ndim - 2)
        return jnp.where(mc == 0, full, 0.0).reshape(-1)

    conv_pack = jnp.concatenate([_pad_row(placed(a_conv)), _pad_row(placed(ffn_conv))]).reshape(-1, FLAT_W)
    conv_full = _allreduce_small(conv_pack, "allgather_conv_taps").reshape(-1)
    n_ac = a_conv.size * N_CHIPS
    a_conv_f = conv_full[:n_ac].reshape(a_conv.shape[:-1] + (-1,))
    off = _pad_row(placed(a_conv)).shape[0]
    ffn_conv_f = conv_full[off:off + ffn_conv.size * N_CHIPS].reshape(ffn_conv.shape[:-1] + (-1,))
    F = ffn_conv_f.shape[-1]

    b_pad = jnp.pad(b_f, (0, LANES - H)).reshape(1, LANES)
    gate_blk = 2 * D // LANES
    tq = _attn_tile(S)
    scale = hd ** -0.5

    saved = []
    cur = xs
    kv = None
    for l in range(depth):
        rec = {"x_in": cur}
        if l < n_a:
            proj, xn = _norm_matmul(cur, attn_norm[l], wa_in[l], 3, BF, f"a_in_{l}")
            z = _mixer_mid_fwd(proj, a_conv_f[l], f"a_mid_{l}")
            mid = _matmul_residual(z, wa_out[l], cur, f"a_out_{l}")
            rec.update(proj=proj, xn=xn, z=z)
        else:
            j = l - n_a
            if kv is None:
                kvf, hn = _norm_matmul(cur, kv_norm, wkvf, 1, F32, "kvf_proj")
                kn = _headnorm(kvf, 0, 0, k_norm, 1.0, D, "k_norm")
                vb = kvf[0, :, D:2 * D].astype(BF)
                cgate = _gate_fwd(kvf, b_pad, gate_blk, "gate_cumsum")
                kv = dict(kvf=kvf, hn=hn, vb=vb, cgate=cgate, x_in=cur, dk=[], dv=[], dc=[],
                          ka=_augment(kn, cgate, "k", hd, "k_augment"), va=_augment(vb, cgate, "v", hd, "v_augment"))
            qg, xn = _norm_matmul(cur, attn_norm[l], wb_qg[j], 2, F32, f"qg_proj_{j}")
            qn = _headnorm(qg, 0, 0, q_norm[j], scale * LOG2E, D, f"q_norm_{j}")
            qa = _augment(qn, kv["cgate"], "q", hd, f"q_augment_{j}")
            o, og, m_max, l_sum = _attn_fwd(qa, kv["ka"], kv["va"], qg, hd, f"attn_fwd_{j}")
            mid = _matmul_residual(og, wb_out[j], cur, f"b_out_{j}")
            rec.update(qg=qg, xn=xn, qa=qa, o=o, og=og, m=m_max, l=l_sum)
        up, xn2, z2 = _ffn_up_fwd(mid, ffn_norm[l], w_up[l], ffn_conv_f[l], f"ffn_up_{l}")
        cur = _matmul_residual(z2, w_down[l], mid, f"ffn_down_{l}")
        rec.update(x_mid=mid, up=up, xn2=xn2, z2=z2)
        saved.append(rec)

    dy, loss_part = _loss_head(cur, loss_target[0], "loss_head")

    g_attn_norm, g_ffn_norm = [None] * depth, [None] * depth
    g_a_in, g_a_conv, g_a_out = [None] * n_a, [None] * n_a, [None] * n_a
    g_qg, g_qn, g_bo = [None] * (depth - n_a), [None] * (depth - n_a), [None] * (depth - n_a)
    g_up, g_fc, g_down = [None] * depth, [None] * depth, [None] * depth
    for l in reversed(range(depth)):
        rec = saved[l]
        dup, g_fc[l] = _ffn_mid_bwd(rec["up"], dy, w_down[l], ffn_conv_f[l], f"ffn_mid_bwd_{l}")
        g_down[l] = _wgrad(rec["z2"], dy[None], f"ffn_down_wgrad_{l}")
        g_up[l] = _wgrad(rec["xn2"], dup, f"ffn_up_wgrad_{l}")
        dy, g_ffn_norm[l] = _dnorm(dup, w_up[l], rec["x_mid"], ffn_norm[l], dy, f"ffn_up_bwd_{l}")
        if l < n_a:
            dz = _matmul_nt(dy, wa_out[l], f"a_out_bwd_{l}")
            dproj, g_a_conv[l] = _mixer_mid_bwd(rec["proj"], dz, a_conv_f[l], f"a_mid_bwd_{l}")
            g_a_out[l] = _wgrad(rec["z"], dy[None], f"a_out_wgrad_{l}")
            g_a_in[l] = _wgrad(rec["xn"], dproj, f"a_in_wgrad_{l}")
            dy, g_attn_norm[l] = _dnorm(dproj, wa_in[l], rec["x_in"], attn_norm[l], dy, f"a_in_bwd_{l}")
        else:
            j = l - n_a
            dog = _matmul_nt(dy, wb_out[j], f"b_out_bwd_{j}")
            g_out, dgate, evec = _attn_out_bwd(dog, rec["o"], rec["qg"], rec["l"], hd, f"attn_gate_bwd_{j}")
            g_bo[j] = _wgrad(rec["og"], dy[None], f"b_out_wgrad_{j}")
            dqn, dk, dv, dc = _attn_bwd(rec["qa"], kv["ka"], kv["vb"], g_out, _row_layout(rec["m"], tq),
                                        _row_layout(evec, tq), hd, f"attn_bwd_{j}")
            kv["dk"].append(dk)
            kv["dv"].append(dv)
            kv["dc"].append(dc)
            dq_pre, g_qn[j] = _headnorm_bwd(rec["qg"], 0, 0, q_norm[j], [dqn], D, f"q_norm_bwd_{j}")
            dqg = jnp.stack([dq_pre, dgate])
            g_qg[j] = _wgrad(rec["xn"], dqg, f"qg_wgrad_{j}")
            dy, g_attn_norm[l] = _dnorm(dqg, wb_qg[j], rec["x_in"], attn_norm[l], dy, f"qg_bwd_{j}")
            if l == n_a:
                dk_s, g_k_norm = _headnorm_bwd(kv["kvf"], 0, 0, k_norm, kv["dk"], D, "k_norm_bwd")
                dv_s = functools.reduce(jnp.add, kv["dv"]).astype(BF)
                dc_sum = functools.reduce(jnp.add, kv["dc"])
                dc_pad = jnp.pad(dc_sum.transpose(1, 0, 2).reshape(S, H), ((0, 0), (0, LANES - H)))
                df, db = _gate_bwd(dc_pad, kv["kvf"], b_pad, gate_blk, "gate_bwd")
                dkvf = jnp.concatenate([dk_s, dv_s, df.astype(BF)], axis=1)[None]
                g_kvf = _wgrad(kv["hn"], dkvf, "kvf_wgrad")[:, :2 * D + H]
                g_b_f = db[:H]
                dy, g_kv_norm = _dnorm(dkvf, wkvf, kv["x_in"], kv_norm, dy, "kvf_bwd")
    grad_x = dy[None]

    def cols_of(g, k):
        c = g.shape[-1] // N_CHIPS
        return g[:, k * c:(k + 1) * c]

    def rows_of(g, k):
        r = g.shape[0] // N_CHIPS
        return g[k * r:(k + 1) * r]

    def chip_pieces(k):
        return ([cols_of(g, k) for g in g_a_in] + [rows_of(g, k) for g in g_a_out] + [cols_of(g_kvf, k)]
                + [cols_of(g, k) for g in g_qg] + [rows_of(g, k) for g in g_bo]
                + [cols_of(g, k) for g in g_up] + [rows_of(g, k) for g in g_down])

    gflat = jnp.stack([_flatten(chip_pieces(k), F32) for k in range(N_CHIPS)])
    R = gflat.shape[1]
    g4 = gflat.reshape(N_CHIPS, 2, R // 2, FLAT_W)
    from_sibling = _pair_exchange(g4)
    from_chips = _chip_exchange(_pair_add(g4, from_sibling, mc))
    mine = _chip_add(g4, from_sibling, from_chips, mc, chip)
    theirs = _pair_share(mine)
    shard_flat = jnp.where(mc == 0, jnp.stack([mine, theirs]), jnp.stack([theirs, mine])).reshape(-1)
    big_grads = _unflatten(shard_flat, big_shapes)

    small = [loss_part[0, :1], jnp.stack(g_attn_norm), jnp.stack(g_ffn_norm), g_kv_norm, g_b_f, g_k_norm,
             jnp.stack(g_qn), jnp.stack(g_a_conv), jnp.stack(g_fc)]
    small_sum = _allreduce_small(jnp.concatenate([_pad_row(s) for s in small]).reshape(-1, FLAT_W),
                                 "allreduce_small_grads").reshape(-1)
    parts, off = [], 0
    for s in small:
        parts.append(small_sum[off:off + s.size].reshape(s.shape))
        off += _pad_row(s).shape[0]
    loss = parts[0][0]
    gr_attn_norm, gr_ffn_norm, gr_kv_norm, gr_b_f, gr_k_norm, gr_q_norm, gr_a_conv_full, gr_ffn_conv_full = parts[1:]

    def my_cols(full):
        c = full.shape[-1] // N_CHIPS
        return lax.dynamic_slice_in_dim(full, chip * c, c, axis=full.ndim - 1)

    gr_a_in, gr_a_out, gr_kvf, gr_qg, gr_bo, gr_up, gr_down = big_grads
    grads = [gr_attn_norm, gr_ffn_norm, gr_a_in, my_cols(gr_a_conv_full), gr_a_out, gr_kv_norm, gr_kvf, gr_b_f,
             gr_k_norm, gr_qg, gr_q_norm, gr_bo, gr_up, my_cols(gr_ffn_conv_full), gr_down]
    weights = [attn_norm, ffn_norm, a_w_in, a_conv, a_w_out, kv_norm, w_kvf, b_f, k_norm, b_w_qg, q_norm, b_w_out,
               ffn_w_up, ffn_conv, ffn_w_down]
    ms = [m_attn_norm, m_ffn_norm, m_a_w_in, m_a_conv, m_a_w_out, m_kv_norm, m_w_kvf, m_b_f, m_k_norm, m_b_w_qg,
          m_q_norm, m_b_w_out, m_ffn_w_up, m_ffn_conv, m_ffn_w_down]
    vs = [v_attn_norm, v_ffn_norm, v_a_w_in, v_a_conv, v_a_w_out, v_kv_norm, v_w_kvf, v_b_f, v_k_norm, v_b_w_qg,
          v_q_norm, v_b_w_out, v_ffn_w_up, v_ffn_conv, v_ffn_w_down]
    deltas, new_ms, new_vs = [], [], []
    for i, (w, g, m, v) in enumerate(zip(weights, grads, ms, vs)):
        d, nm, nv = _adamw(w, g, m, v, f"adamw_{i}")
        deltas.append(d)
        new_ms.append(nm)
        new_vs.append(nv)
    return (loss, grad_x, *grads, *deltas, *new_ms, *new_vs)
```

```python
import functools

import jax
import jax.numpy as jnp
from jax import lax
from jax.experimental import pallas as pl
from jax.experimental.pallas import tpu as pltpu

F32 = jnp.float32
BF = jnp.bfloat16
LANES = 128
SUBLANES = 8
RMS_EPS = 1e-6
LOG2E = 1.4426950408889634
FLAT_W = 1024
N_CHIPS = 4
CONV_W = 3
HALO = SUBLANES

ADAM_LR = 0.001
ADAM_B1 = 0.9
ADAM_B2 = 0.999
ADAM_EPS = 1e-08
ADAM_WD = 0.01
ADAM_STEP = 10

MESH = pl.DeviceIdType.MESH
ANY = pl.BlockSpec(memory_space=pl.ANY)
VMEM = pl.BlockSpec(memory_space=pltpu.VMEM)
NT_DIMS = (((1,), (1,)), ((), ()))
TN_DIMS = (((0,), (0,)), ((), ()))


def _tile(n, pref, mult=LANES):
    t = (min(pref, n) // mult) * mult
    while t >= mult:
        if n % t == 0:
            break
        t -= mult
    if t < mult or (t * 4 < pref and n <= 4 * pref):
        return n
    return t


def _params(*sem):
    return pltpu.CompilerParams(dimension_semantics=sem)


def _norm_matmul(x, g, w, parts, out_dtype, name):
    S, D = x.shape
    C = w.shape[1] // parts
    ts, tn = _tile(S, 512, SUBLANES), _tile(C, 1408)
    npc = C // tn

    def body(x_ref, g_ref, w_ref, o_ref, xn_ref):
        @pl.when(pl.program_id(1) == 0)
        def _():
            xf = x_ref[...]
            r = lax.rsqrt(jnp.mean(xf * xf, axis=-1, keepdims=True) + RMS_EPS)
            xn_ref[...] = (xf * r * g_ref[...]).astype(BF)

        o_ref[0] = jnp.dot(xn_ref[...], w_ref[...], preferred_element_type=F32).astype(out_dtype)

    return pl.pallas_call(
        body, name=name, grid=(S // ts, parts * npc),
        in_specs=[pl.BlockSpec((ts, D), lambda s, n: (s, 0)),
                  pl.BlockSpec((1, D), lambda s, n: (0, 0)),
                  pl.BlockSpec((D, tn), lambda s, n: (0, n))],
        out_specs=[pl.BlockSpec((1, ts, tn), lambda s, n: (n // npc, s, n % npc)),
                   pl.BlockSpec((ts, D), lambda s, n: (s, 0))],
        out_shape=[jax.ShapeDtypeStruct((parts, S, C), out_dtype), jax.ShapeDtypeStruct((S, D), BF)],
        compiler_params=_params("parallel", "arbitrary"),
    )(x, g.reshape(1, D), w)


def _shift_down(u, prev, k, row):
    r = pltpu.roll(u, k, 0)
    for j in range(k):
        r = jnp.where(row == j, prev[HALO - k + j:HALO - k + j + 1, :], r)
    return r


def _shift_up(d, nxt, k, row):
    n = d.shape[0]
    r = pltpu.roll(d, n - k, 0)
    for j in range(k):
        r = jnp.where(row == n - k + j, nxt[j:j + 1, :], r)
    return r


def _conv3(u, prev, w, row):
    return _shift_down(u, prev, 2, row) * w[0:1] + _shift_down(u, prev, 1, row) * w[1:2] + u * w[2:3]


def _conv3_t(d, nxt, w, row):
    return d * w[2:3] + _shift_up(d, nxt, 1, row) * w[1:2] + _shift_up(d, nxt, 2, row) * w[0:1]


def _tap_rows(t0, t1, t2):
    row = lax.broadcasted_iota(jnp.int32, (SUBLANES, t0.shape[1]), 0)
    return jnp.where(row == 0, t0, jnp.where(row == 1, t1, jnp.where(row == 2, t2, 0.0)))


def _pad_conv(cw):
    return jnp.pad(cw, ((0, SUBLANES - CONV_W), (0, 0)))


def _mixer_in_fwd(x, g, w, cw, name):
    S, D = x.shape
    C = w.shape[1] // 3
    ts, tc = _tile(S, 512, SUBLANES), _tile(C, 1024)
    nc = C // tc

    def body(x_ref, g_ref, wb_ref, wc_ref, wh_ref, cw_ref, p_ref, xn_ref, z_ref, carry):
        s, c = pl.program_id(0), pl.program_id(1)

        @pl.when(c == 0)
        def _():
            xf = x_ref[...]
            r = lax.rsqrt(jnp.mean(xf * xf, axis=-1, keepdims=True) + RMS_EPS)
            xn_ref[...] = (xf * r * g_ref[...]).astype(BF)

        @pl.when(s == 0)
        def _():
            carry[c] = jnp.zeros((HALO, tc), F32)

        xn = xn_ref[...]
        parts = [jnp.dot(xn, w_ref[...], preferred_element_type=F32).astype(BF) for w_ref in (wb_ref, wc_ref, wh_ref)]
        for p, v in enumerate(parts):
            p_ref[p] = v
        u = parts[1].astype(F32) * parts[2].astype(F32)
        row = lax.broadcasted_iota(jnp.int32, u.shape, 0)
        cv = _conv3(u, carry[c], cw_ref[...], row)
        z_ref[...] = (parts[0].astype(F32) * cv).astype(BF)
        carry[c] = u[ts - HALO:ts, :]

    wspec = lambda p: pl.BlockSpec((D, tc), lambda s, c: (0, p * nc + c))
    return pl.pallas_call(
        body, name=name, grid=(S // ts, nc),
        in_specs=[pl.BlockSpec((ts, D), lambda s, c: (s, 0)), pl.BlockSpec((1, D), lambda s, c: (0, 0)),
                  wspec(0), wspec(1), wspec(2), pl.BlockSpec((SUBLANES, tc), lambda s, c: (0, c))],
        out_specs=[pl.BlockSpec((3, ts, tc), lambda s, c: (0, s, c)),
                   pl.BlockSpec((ts, D), lambda s, c: (s, 0)),
                   pl.BlockSpec((ts, tc), lambda s, c: (s, c))],
        out_shape=[jax.ShapeDtypeStruct((3, S, C), BF), jax.ShapeDtypeStruct((S, D), BF),
                   jax.ShapeDtypeStruct((S, C), BF)],
        scratch_shapes=[pltpu.VMEM((nc, HALO, tc), F32)],
        compiler_params=_params("arbitrary", "arbitrary"),
    )(x, g.reshape(1, D), w, w, w, _pad_conv(cw))


def _ffn_up_fwd(x, g, w, cw, name):
    S, D = x.shape
    C = w.shape[1] // 2
    ts, tc = _tile(S, 512, SUBLANES), _tile(C, 1408)
    nc = C // tc

    def body(x_ref, g_ref, wa_ref, wg_ref, cw_ref, up_ref, xn_ref, z_ref, carry):
        s, c = pl.program_id(0), pl.program_id(1)

        @pl.when(c == 0)
        def _():
            xf = x_ref[...]
            r = lax.rsqrt(jnp.mean(xf * xf, axis=-1, keepdims=True) + RMS_EPS)
            xn_ref[...] = (xf * r * g_ref[...]).astype(BF)

        @pl.when(s == 0)
        def _():
            carry[c] = jnp.zeros((HALO, tc), F32)

        xn = xn_ref[...]
        a_b = jnp.dot(xn, wa_ref[...], preferred_element_type=F32).astype(BF)
        g_b = jnp.dot(xn, wg_ref[...], preferred_element_type=F32).astype(BF)
        up_ref[0] = a_b
        up_ref[1] = g_b
        a_pre = a_b.astype(F32)
        row = lax.broadcasted_iota(jnp.int32, a_pre.shape, 0)
        a = _conv3(a_pre, carry[c], cw_ref[...], row)
        z_ref[...] = (a * jax.nn.sigmoid(a) * g_b.astype(F32)).astype(BF)
        carry[c] = a_pre[ts - HALO:ts, :]

    return pl.pallas_call(
        body, name=name, grid=(S // ts, nc),
        in_specs=[pl.BlockSpec((ts, D), lambda s, c: (s, 0)),
                  pl.BlockSpec((1, D), lambda s, c: (0, 0)),
                  pl.BlockSpec((D, tc), lambda s, c: (0, c)),
                  pl.BlockSpec((D, tc), lambda s, c: (0, nc + c)),
                  pl.BlockSpec((SUBLANES, tc), lambda s, c: (0, c))],
        out_specs=[pl.BlockSpec((2, ts, tc), lambda s, c: (0, s, c)),
                   pl.BlockSpec((ts, D), lambda s, c: (s, 0)),
                   pl.BlockSpec((ts, tc), lambda s, c: (s, c))],
        out_shape=[jax.ShapeDtypeStruct((2, S, C), BF), jax.ShapeDtypeStruct((S, D), BF),
                   jax.ShapeDtypeStruct((S, C), BF)],
        scratch_shapes=[pltpu.VMEM((nc, HALO, tc), F32)],
        compiler_params=_params("arbitrary", "arbitrary"),
    )(x, g.reshape(1, D), w, w, _pad_conv(cw))


def _mixer_mid_bwd(proj, dy, w_out, cw, name):
    _, S, C = proj.shape
    D = dy.shape[1]
    ts, tc = _tile(S, 512, SUBLANES), _tile(C, 1024)
    n_s = S // ts
    per = ts // HALO

    def body(b_ref, c_ref, h_ref, dy_ref, w_ref, cp_ref, hp_ref, cw_ref, d_ref, dcw_ref, carry):
        i = pl.program_id(1)
        w = cw_ref[...]
        dz = lax.dot_general(dy_ref[...].astype(BF), w_ref[...], NT_DIMS, preferred_element_type=F32)
        b, c, h = b_ref[0].astype(F32), c_ref[0].astype(F32), h_ref[0].astype(F32)
        row = lax.broadcasted_iota(jnp.int32, b.shape, 0)
        u = c * h
        prev = jnp.where(i < n_s - 1, cp_ref[0].astype(F32) * hp_ref[0].astype(F32), 0.0)
        u1, u2 = _shift_down(u, prev, 1, row), _shift_down(u, prev, 2, row)
        cv = u2 * w[0:1] + u1 * w[1:2] + u * w[2:3]
        dcv = dz * b
        nxt = jnp.where(i > 0, carry[...], 0.0)
        du = _conv3_t(dcv, nxt, w, row)
        d_ref[0] = (dz * cv).astype(BF)
        d_ref[1] = (du * h).astype(BF)
        d_ref[2] = (du * c).astype(BF)
        carry[...] = dcv[0:HALO, :]
        part = _tap_rows(jnp.sum(dcv * u2, axis=0, keepdims=True), jnp.sum(dcv * u1, axis=0, keepdims=True),
                         jnp.sum(dcv * u, axis=0, keepdims=True))

        @pl.when(i == 0)
        def _():
            dcw_ref[...] = part

        @pl.when(i > 0)
        def _():
            dcw_ref[...] += part

    tile = lambda p: pl.BlockSpec((1, ts, tc), lambda c, i: (p, n_s - 1 - i, c))
    before = lambda p: pl.BlockSpec((1, HALO, tc), lambda c, i: (p, jnp.maximum((n_s - 1 - i) * per - 1, 0), c))
    dproj, dcw = pl.pallas_call(
        body, name=name, grid=(C // tc, n_s),
        in_specs=[tile(0), tile(1), tile(2),
                  pl.BlockSpec((ts, D), lambda c, i: (n_s - 1 - i, 0)),
                  pl.BlockSpec((tc, D), lambda c, i: (c, 0)),
                  before(1), before(2),
                  pl.BlockSpec((SUBLANES, tc), lambda c, i: (0, c))],
        out_specs=[pl.BlockSpec((3, ts, tc), lambda c, i: (0, n_s - 1 - i, c)),
                   pl.BlockSpec((SUBLANES, tc), lambda c, i: (0, c))],
        out_shape=[jax.ShapeDtypeStruct((3, S, C), BF), jax.ShapeDtypeStruct((SUBLANES, C), F32)],
        scratch_shapes=[pltpu.VMEM((HALO, tc), F32)],
        compiler_params=_params("parallel", "arbitrary"),
    )(proj, proj, proj, dy, w_out, proj, proj, _pad_conv(cw))
    return dproj, dcw[:CONV_W]


def _ffn_mid_bwd(up, dy, w_down, cw, name):
    _, S, C = up.shape
    D = dy.shape[1]
    ts, tc = _tile(S, 512, SUBLANES), _tile(C, 1408)
    n_s = S // ts
    per = ts // HALO

    def body(a_ref, g_ref, dy_ref, w_ref, ap_ref, cw_ref, d_ref, dcw_ref, carry):
        i = pl.program_id(1)
        w = cw_ref[...]
        dz = lax.dot_general(dy_ref[...].astype(BF), w_ref[...], NT_DIMS, preferred_element_type=F32)
        a_pre, g = a_ref[0].astype(F32), g_ref[0].astype(F32)
        row = lax.broadcasted_iota(jnp.int32, a_pre.shape, 0)
        prev = jnp.where(i < n_s - 1, ap_ref[0].astype(F32), 0.0)
        a1, a2 = _shift_down(a_pre, prev, 1, row), _shift_down(a_pre, prev, 2, row)
        a = a2 * w[0:1] + a1 * w[1:2] + a_pre * w[2:3]
        sg = jax.nn.sigmoid(a)
        da = dz * g * (sg * (1.0 + a * (1.0 - sg)))
        nxt = jnp.where(i > 0, carry[...], 0.0)
        d_ref[0] = _conv3_t(da, nxt, w, row).astype(BF)
        d_ref[1] = (dz * (a * sg)).astype(BF)
        carry[...] = da[0:HALO, :]
        part = _tap_rows(jnp.sum(da * a2, axis=0, keepdims=True), jnp.sum(da * a1, axis=0, keepdims=True),
                         jnp.sum(da * a_pre, axis=0, keepdims=True))

        @pl.when(i == 0)
        def _():
            dcw_ref[...] = part

        @pl.when(i > 0)
        def _():
            dcw_ref[...] += part

    tile = lambda p: pl.BlockSpec((1, ts, tc), lambda c, i: (p, n_s - 1 - i, c))
    dup, dcw = pl.pallas_call(
        body, name=name, grid=(C // tc, n_s),
        in_specs=[tile(0), tile(1),
                  pl.BlockSpec((ts, D), lambda c, i: (n_s - 1 - i, 0)),
                  pl.BlockSpec((tc, D), lambda c, i: (c, 0)),
                  pl.BlockSpec((1, HALO, tc), lambda c, i: (0, jnp.maximum((n_s - 1 - i) * per - 1, 0), c)),
                  pl.BlockSpec((SUBLANES, tc), lambda c, i: (0, c))],
        out_specs=[pl.BlockSpec((2, ts, tc), lambda c, i: (0, n_s - 1 - i, c)),
                   pl.BlockSpec((SUBLANES, tc), lambda c, i: (0, c))],
        out_shape=[jax.ShapeDtypeStruct((2, S, C), BF), jax.ShapeDtypeStruct((SUBLANES, C), F32)],
        scratch_shapes=[pltpu.VMEM((HALO, tc), F32)],
        compiler_params=_params("parallel", "arbitrary"),
    )(up, up, dy, w_down, up, _pad_conv(cw))
    return dup, dcw[:CONV_W]


def _matmul_residual(z, w, x, name):
    S, K = z.shape
    D = w.shape[1]
    ts = _tile(S, 512, SUBLANES)

    def body(z_ref, w_ref, x_ref, o_ref):
        o_ref[...] = x_ref[...] + jnp.dot(z_ref[...], w_ref[...], preferred_element_type=F32)

    return pl.pallas_call(
        body, name=name, grid=(S // ts,),
        in_specs=[pl.BlockSpec((ts, K), lambda s: (s, 0)), pl.BlockSpec((K, D), lambda s: (0, 0)),
                  pl.BlockSpec((ts, D), lambda s: (s, 0))],
        out_specs=pl.BlockSpec((ts, D), lambda s: (s, 0)),
        out_shape=jax.ShapeDtypeStruct((S, D), F32),
        compiler_params=_params("parallel"),
    )(z, w, x)


def _matmul_nt(a, w, name):
    S, K = a.shape
    N = w.shape[0]
    ts, tn = _tile(S, 512, SUBLANES), _tile(N, 1408)

    def body(a_ref, w_ref, o_ref, abf):
        @pl.when(pl.program_id(1) == 0)
        def _():
            abf[...] = a_ref[...].astype(BF)

        o_ref[...] = lax.dot_general(abf[...], w_ref[...], NT_DIMS, preferred_element_type=F32)

    return pl.pallas_call(
        body, name=name, grid=(S // ts, N // tn),
        in_specs=[pl.BlockSpec((ts, K), lambda s, n: (s, 0)), pl.BlockSpec((tn, K), lambda s, n: (n, 0))],
        out_specs=pl.BlockSpec((ts, tn), lambda s, n: (s, n)),
        out_shape=jax.ShapeDtypeStruct((S, N), F32),
        scratch_shapes=[pltpu.VMEM((ts, K), BF)],
        compiler_params=_params("parallel", "arbitrary"),
    )(a, w)


def _wgrad(a, b, name):
    S, M = a.shape
    P, _, C = b.shape
    tm, tn, tk = _tile(M, 1408), _tile(C, 1408), _tile(S, 512, SUBLANES)
    nnc = C // tn

    def body(a_ref, b_ref, o_ref):
        @pl.when(pl.program_id(2) == 0)
        def _():
            o_ref[...] = jnp.zeros_like(o_ref)

        o_ref[...] += lax.dot_general(a_ref[...], b_ref[0].astype(BF), TN_DIMS, preferred_element_type=F32)

    return pl.pallas_call(
        body, name=name, grid=(M // tm, P * nnc, S // tk),
        in_specs=[pl.BlockSpec((tk, tm), lambda m, n, k: (k, m)),
                  pl.BlockSpec((1, tk, tn), lambda m, n, k: (n // nnc, k, n % nnc))],
        out_specs=pl.BlockSpec((tm, tn), lambda m, n, k: (m, n)),
        out_shape=jax.ShapeDtypeStruct((M, P * C), F32),
        compiler_params=_params("parallel", "parallel", "arbitrary"),
    )(a, b)


def _dnorm(dp, w, x, g, dy, name):
    P, S, C = dp.shape
    D = x.shape[1]
    ts, tk = _tile(S, 512, SUBLANES), _tile(C, 1408)
    nkc = C // tk
    n_k = P * nkc

    def body(dp_ref, w_ref, x_ref, g_ref, dy_ref, dx_ref, dg_ref, acc):
        s, k = pl.program_id(0), pl.program_id(1)

        @pl.when(k == 0)
        def _():
            acc[...] = jnp.zeros_like(acc)

        acc[...] += lax.dot_general(dp_ref[0], w_ref[...], NT_DIMS, preferred_element_type=F32)

        @pl.when((s == 0) & (k == 0))
        def _():
            dg_ref[...] = jnp.zeros_like(dg_ref)

        @pl.when(k == n_k - 1)
        def _():
            xf = x_ref[...]
            r = lax.rsqrt(jnp.mean(xf * xf, axis=-1, keepdims=True) + RMS_EPS)
            xhat = xf * r
            dxn = acc[...]
            dxhat = dxn * g_ref[...]
            dx_ref[...] = dy_ref[...] + r * (dxhat - xhat * jnp.mean(dxhat * xhat, axis=-1, keepdims=True))
            dg_ref[...] += jnp.broadcast_to(jnp.sum(dxn * xhat, axis=0, keepdims=True), dg_ref.shape)

    dx, dg = pl.pallas_call(
        body, name=name, grid=(S // ts, n_k),
        in_specs=[pl.BlockSpec((1, ts, tk), lambda s, k: (k // nkc, s, k % nkc)),
                  pl.BlockSpec((D, tk), lambda s, k: (0, k)),
                  pl.BlockSpec((ts, D), lambda s, k: (s, 0)),
                  pl.BlockSpec((1, D), lambda s, k: (0, 0)),
                  pl.BlockSpec((ts, D), lambda s, k: (s, 0))],
        out_specs=[pl.BlockSpec((ts, D), lambda s, k: (s, 0)),
                   pl.BlockSpec((SUBLANES, D), lambda s, k: (0, 0))],
        out_shape=[jax.ShapeDtypeStruct((S, D), F32), jax.ShapeDtypeStruct((SUBLANES, D), F32)],
        scratch_shapes=[pltpu.VMEM((ts, D), F32)],
        compiler_params=_params("arbitrary", "arbitrary"),
    )(dp, w, x, g.reshape(1, D), dy)
    return dx, dg[0]


def _head_masks(shape, hd):
    lane = lax.broadcasted_iota(jnp.int32, shape, 1)
    return lane < hd


def _pair_sum(v, lo):
    s0 = jnp.sum(jnp.where(lo, v, 0.0), axis=-1, keepdims=True)
    s1 = jnp.sum(jnp.where(lo, 0.0, v), axis=-1, keepdims=True)
    return jnp.where(lo, s0, s1)


def _headnorm(src, part, colblk, w, scale, D, name):
    S = src.shape[1]
    hd = w.shape[0]
    ts = _tile(S, 512, SUBLANES)
    w2 = jnp.tile(w, LANES // hd).reshape(1, LANES)

    def body(x_ref, w_ref, o_ref):
        lo = _head_masks((ts, LANES), hd)
        for t in range(D // LANES):
            xt = x_ref[0, :, t * LANES:(t + 1) * LANES]
            r = lax.rsqrt(_pair_sum(xt * xt, lo) * (1.0 / hd) + RMS_EPS)
            o_ref[:, t * LANES:(t + 1) * LANES] = (xt * r * w_ref[...] * scale).astype(BF)

    return pl.pallas_call(
        body, name=name, grid=(S // ts,),
        in_specs=[pl.BlockSpec((1, ts, D), lambda s: (part, s, colblk)), pl.BlockSpec((1, LANES), lambda s: (0, 0))],
        out_specs=pl.BlockSpec((ts, D), lambda s: (s, 0)),
        out_shape=jax.ShapeDtypeStruct((S, D), BF),
        compiler_params=_params("parallel"),
    )(src, w2)


def _headnorm_bwd(src, part, colblk, w, dys, D, name):
    S = src.shape[1]
    hd = w.shape[0]
    ts = _tile(S, 512, SUBLANES)
    w2 = jnp.tile(w, LANES // hd).reshape(1, LANES)
    n_dy = len(dys)

    def body(x_ref, w_ref, *rest):
        dy_refs, dx_ref, dw_ref = rest[:n_dy], rest[n_dy], rest[n_dy + 1]

        @pl.when(pl.program_id(0) == 0)
        def _():
            dw_ref[...] = jnp.zeros_like(dw_ref)

        lo = _head_masks((ts, LANES), hd)
        for t in range(D // LANES):
            cols = slice(t * LANES, (t + 1) * LANES)
            xt = x_ref[0, :, cols]
            dy = dy_refs[0][:, cols]
            for other in dy_refs[1:]:
                dy = dy + other[:, cols]
            r = lax.rsqrt(_pair_sum(xt * xt, lo) * (1.0 / hd) + RMS_EPS)
            xhat = xt * r
            dxhat = dy * w_ref[...]
            mean = _pair_sum(dxhat * xhat, lo) * (1.0 / hd)
            dx_ref[:, cols] = (r * (dxhat - xhat * mean)).astype(BF)
            dw_ref[:, cols] += jnp.broadcast_to(jnp.sum(dy * xhat, axis=0, keepdims=True), (SUBLANES, LANES))

    dx, dw = pl.pallas_call(
        body, name=name, grid=(S // ts,),
        in_specs=[pl.BlockSpec((1, ts, D), lambda s: (part, s, colblk)), pl.BlockSpec((1, LANES), lambda s: (0, 0))]
        + [pl.BlockSpec((ts, D), lambda s: (s, 0))] * n_dy,
        out_specs=[pl.BlockSpec((ts, D), lambda s: (s, 0)), pl.BlockSpec((SUBLANES, D), lambda s: (0, 0))],
        out_shape=[jax.ShapeDtypeStruct((S, D), BF), jax.ShapeDtypeStruct((SUBLANES, D), F32)],
        compiler_params=_params("arbitrary"),
    )(src, w2, *dys)
    return dx, jnp.sum(dw[0].reshape(D // hd, hd), axis=0)


def _tri(n, lower):
    r, c = lax.broadcasted_iota(jnp.int32, (n, n), 0), lax.broadcasted_iota(jnp.int32, (n, n), 1)
    return jnp.where((c <= r) if lower else (c >= r), 1.0, 0.0).astype(BF)


def _dot_exact(t, v):
    hi = v.astype(BF)
    r1 = v - hi.astype(F32)
    mid = r1.astype(BF)
    lo = (r1 - mid.astype(F32)).astype(BF)
    dot = lambda u: jnp.dot(t, u, preferred_element_type=F32)
    return dot(hi) + dot(mid) + dot(lo)


def _gate_fwd(kvf, b_pad, colblk, name):
    S = kvf.shape[1]
    ts = _tile(S, 512, SUBLANES)

    def body(f_ref, b_ref, c_ref, carry):
        @pl.when(pl.program_id(0) == 0)
        def _():
            carry[...] = jnp.zeros_like(carry)

        f = f_ref[0] + b_ref[...]
        ls = jnp.minimum(f, 0.0) - jnp.log1p(jnp.exp(-jnp.abs(f)))
        tri = _tri(ts, lower=True)
        c = _dot_exact(tri, ls) + carry[0:1, :]
        c_ref[...] = c
        carry[...] = jnp.broadcast_to(c[ts - 1:ts, :], carry.shape)

    return pl.pallas_call(
        body, name=name, grid=(S // ts,),
        in_specs=[pl.BlockSpec((1, ts, LANES), lambda s: (0, s, colblk)), pl.BlockSpec((1, LANES), lambda s: (0, 0))],
        out_specs=pl.BlockSpec((ts, LANES), lambda s: (s, 0)),
        out_shape=jax.ShapeDtypeStruct((S, LANES), F32),
        scratch_shapes=[pltpu.VMEM((SUBLANES, LANES), F32)],
        compiler_params=_params("arbitrary"),
    )(kvf, b_pad)


def _gate_bwd(dc, kvf, b_pad, colblk, name):
    S = kvf.shape[1]
    ts = _tile(S, 512, SUBLANES)
    n_s = S // ts

    def body(dc_ref, f_ref, b_ref, df_ref, db_ref, carry):
        @pl.when(pl.program_id(0) == 0)
        def _():
            carry[...] = jnp.zeros_like(carry)
            db_ref[...] = jnp.zeros_like(db_ref)

        tri = _tri(ts, lower=False)
        dls = _dot_exact(tri, dc_ref[...]) + carry[0:1, :]
        f = f_ref[0] + b_ref[...]
        df = dls * jax.nn.sigmoid(-f)
        df_ref[...] = df
        db_ref[...] += jnp.broadcast_to(jnp.sum(df, axis=0, keepdims=True), db_ref.shape)
        carry[...] = jnp.broadcast_to(dls[0:1, :], carry.shape)

    df, db = pl.pallas_call(
        body, name=name, grid=(n_s,),
        in_specs=[pl.BlockSpec((ts, LANES), lambda s: (n_s - 1 - s, 0)),
                  pl.BlockSpec((1, ts, LANES), lambda s: (0, n_s - 1 - s, colblk)),
                  pl.BlockSpec((1, LANES), lambda s: (0, 0))],
        out_specs=[pl.BlockSpec((ts, LANES), lambda s: (n_s - 1 - s, 0)),
                   pl.BlockSpec((SUBLANES, LANES), lambda s: (0, 0))],
        out_shape=[jax.ShapeDtypeStruct((S, LANES), F32), jax.ShapeDtypeStruct((SUBLANES, LANES), F32)],
        scratch_shapes=[pltpu.VMEM((SUBLANES, LANES), F32)],
        compiler_params=_params("arbitrary"),
    )(dc, kvf, b_pad)
    return df, db[0]


def _attn_tile(S):
    return _tile(S, 512, LANES)


def _split_heads(v, lo):
    zero = jnp.zeros_like(v)
    return jnp.where(lo, v, zero), jnp.where(lo, zero, v)


def _augment(base, c, mode, hd, name):
    S, D = base.shape
    ts = _tile(S, 512, 2 * SUBLANES)

    def body(b_ref, c_ref, o0_ref, o1_ref):
        lane = lax.broadcasted_iota(jnp.int32, (ts, LANES), 1)
        cc = c_ref[...] * LOG2E
        for t in range(D // LANES):
            cols = slice(t * LANES, (t + 1) * LANES)
            bt = b_ref[:, cols]
            for h, o_ref in ((0, o0_ref), (1, o1_ref)):
                first = hd if h == 0 else 0
                keep = (lane < hd) if h == 0 else (lane >= hd)
                if mode == "v":
                    vals = (1.0,)
                else:
                    col = cc[:, 2 * t + h:2 * t + h + 1]
                    hi = col.astype(BF).astype(F32)
                    mid = (col - hi).astype(BF).astype(F32)
                    pieces = (hi, mid, col - hi - mid)
                    vals = pieces + (1.0, 1.0, 1.0) if mode == "q" else (1.0, 1.0, 1.0) + tuple(-v for v in pieces)
                aug = jnp.zeros((ts, LANES), F32)
                for i, v in enumerate(vals):
                    aug = jnp.where(lane == first + i, v, aug)
                o_ref[:, cols] = jnp.where(keep, bt, aug.astype(BF))

    spec = pl.BlockSpec((ts, D), lambda s: (s, 0))
    return pl.pallas_call(
        body, name=name, grid=(S // ts,),
        in_specs=[spec, pl.BlockSpec((ts, LANES), lambda s: (s, 0))], out_specs=[spec, spec],
        out_shape=[jax.ShapeDtypeStruct((S, D), BF)] * 2,
        compiler_params=_params("parallel"),
    )(base, c)


def _attn_fwd(qa, ka, va, qg, hd, name):
    S, D = qa[0].shape
    P = D // LANES
    tq = _attn_tile(S)
    nq = S // tq

    def body(q0_ref, q1_ref, k0_ref, k1_ref, v0_ref, v1_ref, g_ref, o_ref, og_ref, m_ref, l_ref, s_buf):
        qi = pl.program_id(1)
        lo = _head_masks((tq, LANES), hd)
        qh = (q0_ref[...], q1_ref[...])
        k_refs, v_refs = (k0_ref, k1_ref), (v0_ref, v1_ref)
        causal = lax.broadcasted_iota(jnp.int32, (tq, tq), 1) <= lax.broadcasted_iota(jnp.int32, (tq, tq), 0)

        def scores(ki, slot):
            off = pl.multiple_of(ki * tq, tq)
            for h in range(2):
                s_buf[slot, h] = lax.dot_general(qh[h], k_refs[h][pl.ds(off, tq), :], NT_DIMS,
                                                 preferred_element_type=F32)

        def consume(ki, slot, carry, masked):
            off = pl.multiple_of(ki * tq, tq)
            out = []
            for h in range(2):
                m, acc = carry[h]
                s = s_buf[slot, h]
                if masked:
                    s = jnp.where(causal, s, -jnp.inf)
                m_new = jnp.maximum(m, jnp.ceil(jnp.max(s, axis=-1, keepdims=True)))
                p = jnp.exp2(s - m_new)
                acc = jnp.exp2(m - m_new) * acc + jnp.dot(p.astype(BF), v_refs[h][pl.ds(off, tq), :],
                                                          preferred_element_type=F32)
                out.append((m_new, acc))
            return tuple(out)

        def step(j, carry):
            scores(2 * j + 1, 1)
            carry = consume(2 * j, 0, carry, False)
            scores(2 * j + 2, 0)
            return consume(2 * j + 1, 1, carry, False)

        def finish_even(carry):
            return consume(qi, 0, carry, True)

        def finish_odd(carry):
            scores(qi, 1)
            return consume(qi, 1, consume(qi - 1, 0, carry, False), True)

        init = tuple((jnp.full((tq, 1), -jnp.inf, F32), jnp.zeros((tq, LANES), F32)) for _ in range(2))
        scores(0, 0)
        carry = lax.fori_loop(0, qi // 2, step, init)
        (m0, a0), (m1, a1) = lax.cond(qi % 2 == 0, finish_even, finish_odd, carry)
        l0, l1 = a0[:, hd:hd + 1], a1[:, 0:1]
        o = jnp.where(lo, a0 / l0, a1 / l1)
        o_ref[...] = o
        og_ref[...] = (o * jax.nn.sigmoid(g_ref[0])).astype(BF)
        lane2 = lax.broadcasted_iota(jnp.int32, (tq, 2), 1)
        m_ref[0] = jnp.where(lane2 == 0, m0, m1)
        l_ref[0] = jnp.where(lane2 == 0, l0, l1)

    tile = pl.BlockSpec((tq, LANES), lambda p, i: (i, p))
    whole = pl.BlockSpec((S, LANES), lambda p, i: (0, p))
    stat = pl.BlockSpec((1, tq, 2), lambda p, i: (p, i, 0))
    return pl.pallas_call(
        body, name=name, grid=(P, nq),
        in_specs=[tile, tile, whole, whole, whole, whole, pl.BlockSpec((1, tq, LANES), lambda p, i: (1, i, p))],
        out_specs=[tile, tile, stat, stat],
        out_shape=[jax.ShapeDtypeStruct((S, D), F32), jax.ShapeDtypeStruct((S, D), BF),
                   jax.ShapeDtypeStruct((P, S, 2), F32), jax.ShapeDtypeStruct((P, S, 2), F32)],
        scratch_shapes=[pltpu.VMEM((2, 2, tq, tq), F32)],
        compiler_params=_params("parallel", "arbitrary"),
    )(*qa, *ka, *va, qg)


def _attn_out_bwd(dog, o, qg, l, hd, name):
    S, D = o.shape
    P = D // LANES
    ts = _tile(S, 512, 2 * SUBLANES)

    def body(dog_ref, o_ref, g_ref, l_ref, do_ref, dg_ref, e_ref):
        lo = _head_masks((ts, LANES), hd)
        lane2 = lax.broadcasted_iota(jnp.int32, (ts, 2), 1)
        for t in range(P):
            cols = slice(t * LANES, (t + 1) * LANES)
            sg = jax.nn.sigmoid(g_ref[0, :, cols])
            dog_t, o_t, l_t = dog_ref[:, cols], o_ref[:, cols], l_ref[t]
            g = (dog_t * sg / jnp.where(lo, l_t[:, 0:1], l_t[:, 1:2])).astype(BF)
            do_ref[:, cols] = g
            dg_ref[:, cols] = (dog_t * o_t * sg * (1.0 - sg)).astype(BF)
            prod = g.astype(F32) * o_t
            e0 = jnp.sum(jnp.where(lo, prod, 0.0), axis=-1, keepdims=True)
            e1 = jnp.sum(jnp.where(lo, 0.0, prod), axis=-1, keepdims=True)
            e_ref[t] = jnp.where(lane2 == 0, e0, e1)

    rows = pl.BlockSpec((ts, D), lambda s: (s, 0))
    stat = pl.BlockSpec((P, ts, 2), lambda s: (0, s, 0))
    return pl.pallas_call(
        body, name=name, grid=(S // ts,),
        in_specs=[rows, rows, pl.BlockSpec((1, ts, D), lambda s: (1, s, 0)), stat],
        out_specs=[rows, rows, stat],
        out_shape=[jax.ShapeDtypeStruct((S, D), BF), jax.ShapeDtypeStruct((S, D), BF),
                   jax.ShapeDtypeStruct((P, S, 2), F32)],
        compiler_params=_params("parallel"),
    )(dog, o, qg, l)


def _attn_bwd(qa, ka, vb, g, m_row, e_row, hd, name):
    S, D = vb.shape
    P = D // LANES
    tk = _attn_tile(S)
    nk = S // tk
    scale = hd ** -0.5

    def body(q0_ref, q1_ref, g_ref, k0_ref, k1_ref, v_ref, m_ref, e_ref, dq_ref, dk_ref, dv_ref, dc_ref,
             st_buf, dp_buf):
        ki = pl.program_id(1)

        @pl.when(ki == 0)
        def _():
            dq_ref[...] = jnp.zeros_like(dq_ref)

        lo = _head_masks((tk, LANES), hd)
        kh = (k0_ref[...], k1_ref[...])
        q_refs = (q0_ref, q1_ref)
        vh = _split_heads(v_ref[...], lo)
        causal_t = lax.broadcasted_iota(jnp.int32, (tk, tk), 0) <= lax.broadcasted_iota(jnp.int32, (tk, tk), 1)

        def stage(qi, slot):
            off = pl.multiple_of(qi * tk, tk)
            gb = g_ref[pl.ds(off, tk), :]
            for h in range(2):
                st_buf[slot, h] = lax.dot_general(kh[h], q_refs[h][pl.ds(off, tk), :], NT_DIMS,
                                                  preferred_element_type=F32)
                dp_buf[slot, h] = lax.dot_general(vh[h], gb, NT_DIMS, preferred_element_type=F32)

        def consume(qi, slot, carry, masked):
            off = pl.multiple_of(qi * tk, tk)
            gb = g_ref[pl.ds(off, tk), :]
            m_t, e_t = m_ref[0, qi], e_ref[0, qi]
            out, dq_parts = [], []
            for h in range(2):
                dk, dv, dc = carry[h]
                qb = q_refs[h][pl.ds(off, tk), :]
                pt = jnp.exp2(st_buf[slot, h] - m_t[h:h + 1, :])
                if masked:
                    pt = jnp.where(causal_t, pt, 0.0)
                pb = pt.astype(BF)
                dv = dv + jnp.dot(pb, gb, preferred_element_type=F32)
                dst = pb.astype(F32) * (dp_buf[slot, h] - e_t[h:h + 1, :])
                db = dst.astype(BF)
                dk = dk + jnp.dot(db, qb, preferred_element_type=F32)
                dc = dc - jnp.sum(dst, axis=-1, keepdims=True)
                dq_parts.append(lax.dot_general(db, kh[h], TN_DIMS, preferred_element_type=F32))
                out.append((dk, dv, dc))
            dq_ref[pl.ds(off, tk), :] += jnp.where(lo, dq_parts[0], dq_parts[1]) * scale
            return tuple(out)

        n_after = nk - 1 - ki

        def step(j, carry):
            b = ki + 1 + 2 * j
            stage(b + 1, 0)
            carry = consume(b, 1, carry, False)
            stage(b + 2, 1)
            return consume(b + 1, 0, carry, False)

        def rest_one(carry):
            return consume(nk - 1, 1, carry, False)

        def rest_two(carry):
            stage(nk - 1, 0)
            return consume(nk - 1, 0, consume(nk - 2, 1, carry, False), False)

        init = tuple((jnp.zeros((tk, LANES), F32), jnp.zeros((tk, LANES), F32), jnp.zeros((tk, 1), F32))
                     for _ in range(2))
        stage(ki, 0)
        stage(jnp.minimum(ki + 1, nk - 1), 1)
        carry = consume(ki, 0, init, True)
        carry = lax.fori_loop(0, (n_after - 1) // 2, step, carry)
        which = jnp.where(n_after == 0, 0, 2 - n_after % 2)
        (dk0, dv0, dc0), (dk1, dv1, dc1) = lax.switch(which, [lambda c: c, rest_one, rest_two], carry)
        dk_ref[...] = jnp.where(lo, dk0, dk1) * (1.0 / LOG2E)
        dv_ref[...] = jnp.where(lo, dv0, dv1)
        lane2 = lax.broadcasted_iota(jnp.int32, (tk, 2), 1)
        dc_ref[0] = jnp.where(lane2 == 0, dc0, dc1)

    tile = pl.BlockSpec((tk, LANES), lambda p, i: (i, p))
    whole = pl.BlockSpec((S, LANES), lambda p, i: (0, p))
    row_spec = pl.BlockSpec((1, nk, 2, tk), lambda p, i: (p, 0, 0, 0))
    return pl.pallas_call(
        body, name=name, grid=(P, nk),
        in_specs=[whole, whole, whole, tile, tile, tile, row_spec, row_spec],
        out_specs=[whole, tile, tile, pl.BlockSpec((1, tk, 2), lambda p, i: (p, i, 0))],
        out_shape=[jax.ShapeDtypeStruct((S, D), F32), jax.ShapeDtypeStruct((S, D), F32),
                   jax.ShapeDtypeStruct((S, D), F32), jax.ShapeDtypeStruct((P, S, 2), F32)],
        scratch_shapes=[pltpu.VMEM((2, 2, tk, tk), F32), pltpu.VMEM((2, 2, tk, tk), F32)],
        compiler_params=_params("parallel", "arbitrary"),
    )(*qa, g, *ka, vb, m_row, e_row)


def _loss_head(y, t, name):
    S, D = y.shape
    ts = _tile(S, 512, SUBLANES)

    def body(y_ref, t_ref, dy_ref, l_ref):
        @pl.when(pl.program_id(0) == 0)
        def _():
            l_ref[...] = jnp.zeros_like(l_ref)

        e = y_ref[...] - t_ref[...]
        dy_ref[...] = e * (1.0 / D)
        part = 0.5 * jnp.sum(jnp.mean(e * e, axis=-1, keepdims=True), axis=0, keepdims=True)
        l_ref[...] += jnp.broadcast_to(part, l_ref.shape)

    return pl.pallas_call(
        body, name=name, grid=(S // ts,),
        in_specs=[pl.BlockSpec((ts, D), lambda s: (s, 0)), pl.BlockSpec((ts, D), lambda s: (s, 0))],
        out_specs=[pl.BlockSpec((ts, D), lambda s: (s, 0)), pl.BlockSpec((SUBLANES, LANES), lambda s: (0, 0))],
        out_shape=[jax.ShapeDtypeStruct((S, D), F32), jax.ShapeDtypeStruct((SUBLANES, LANES), F32)],
        compiler_params=_params("arbitrary"),
    )(y, t)


def _adamw(w, g, m, v, name):
    shape = w.shape
    cols = shape[-1]
    as2d = lambda a: a.reshape(-1, cols)
    rows = as2d(w).shape[0]
    tr = _tile(rows, 256, SUBLANES) if rows % SUBLANES == 0 else rows
    c1 = 1.0 - ADAM_B1 ** ADAM_STEP
    c2 = 1.0 - ADAM_B2 ** ADAM_STEP

    def body(w_ref, g_ref, m_ref, v_ref, d_ref, nm_ref, nv_ref):
        gg = g_ref[...]
        nm = ADAM_B1 * m_ref[...] + (1.0 - ADAM_B1) * gg
        nv = ADAM_B2 * v_ref[...] + (1.0 - ADAM_B2) * (gg * gg)
        d_ref[...] = -ADAM_LR * ((nm / c1) / (jnp.sqrt(nv / c2) + ADAM_EPS) + ADAM_WD * w_ref[...])
        nm_ref[...] = nm
        nv_ref[...] = nv

    spec = pl.BlockSpec((tr, cols), lambda r: (r, 0))
    outs = pl.pallas_call(
        body, name=name, grid=(rows // tr,), in_specs=[spec] * 4, out_specs=[spec] * 3,
        out_shape=[jax.ShapeDtypeStruct((rows, cols), F32)] * 3,
        compiler_params=_params("parallel"),
    )(as2d(w), as2d(g), as2d(m), as2d(v))
    return tuple(o.reshape(shape) for o in outs)


def _place():
    return lax.axis_index("x"), lax.axis_index("y"), lax.axis_index("c")


def _other_chips(x, y):
    return [(1 - x, y), (x, 1 - y), (1 - x, 1 - y)]


def _remote(src, dst, send_sems, recv_sems, k, to):
    return pltpu.make_async_remote_copy(src_ref=src, dst_ref=dst, send_sem=send_sems.at[k], recv_sem=recv_sems.at[k],
                                        device_id=to, device_id_type=MESH)


def _half(c, rh):
    return pl.ds(pl.multiple_of(c * rh, 2 * SUBLANES), rh)


def _allgather_weights(srcs):
    n = len(srcs)

    def body(*refs):
        src, dst, (send_sems, recv_sems) = refs[:n], refs[n:2 * n], refs[2 * n:]
        x, y, c = _place()
        me = 2 * x + y
        sib = (x, y, 1 - c)
        chips = _other_chips(x, y)
        rh = [a.shape[0] // 2 for a in srcs]
        first = [_remote(src[g].at[_half(c, rh[g])], dst[g].at[me, _half(c, rh[g])], send_sems, recv_sems,
                         6 * g + j, (cx, cy, c)) for j, (cx, cy) in enumerate(chips) for g in range(n)]
        for cp in first:
            cp.start()
        passed = []
        for j, (cx, cy) in enumerate(chips):
            for g in range(n):
                landed = dst[g].at[2 * cx + cy, _half(c, rh[g])]
                _remote(landed, landed, send_sems, recv_sems, 6 * g + j, sib).wait_recv()
                cp = _remote(landed, landed, send_sems, recv_sems, 6 * g + 3 + j, sib)
                cp.start()
                passed.append(cp)
        for j, (cx, cy) in enumerate(chips):
            for g in range(n):
                landed = dst[g].at[2 * cx + cy, _half(1 - c, rh[g])]
                _remote(landed, landed, send_sems, recv_sems, 6 * g + 3 + j, sib).wait_recv()
        for cp in first + passed:
            cp.wait_send()

    outs = pl.pallas_call(
        body, name="allgather_weights", in_specs=[ANY] * n, out_specs=[ANY] * n,
        out_shape=[jax.ShapeDtypeStruct((N_CHIPS,) + a.shape, a.dtype) for a in srcs],
        scratch_shapes=[pltpu.SemaphoreType.DMA((6 * n,)), pltpu.SemaphoreType.DMA((6 * n,))],
    )(*srcs)
    x, y, _ = _place()
    return [lax.dynamic_update_slice_in_dim(o, a[None], 2 * x + y, axis=0) for o, a in zip(outs, srcs)]


def _pair_exchange(gs):
    n = len(gs)

    def body(*refs):
        g_refs, t_refs, (send_sems, recv_sems) = refs[:n], refs[n:2 * n], refs[2 * n:]
        x, y, c = _place()
        cps = [_remote(g_refs[g].at[k, 1 - c], t_refs[g].at[k], send_sems, recv_sems, N_CHIPS * g + k, (x, y, 1 - c))
               for g in range(n) for k in range(N_CHIPS)]
        for cp in cps:
            cp.start()
        for cp in cps:
            cp.wait()

    return pl.pallas_call(
        body, name="grad_pair_exchange", in_specs=[ANY] * n, out_specs=[ANY] * n,
        out_shape=[jax.ShapeDtypeStruct((a.shape[0],) + a.shape[2:], a.dtype) for a in gs],
        scratch_shapes=[pltpu.SemaphoreType.DMA((N_CHIPS * n,)), pltpu.SemaphoreType.DMA((N_CHIPS * n,))],
    )(*gs)


def _pair_add(g, t, c, name):
    n, _, rh, W = g.shape
    tr = _tile(rh, 256, 2 * SUBLANES)

    def body(c_ref, g_ref, t_ref, o_ref):
        o_ref[...] = (g_ref[0] + t_ref[...]).astype(BF)

    return pl.pallas_call(
        body, name=name,
        grid_spec=pltpu.PrefetchScalarGridSpec(
            num_scalar_prefetch=1, grid=(n, rh // tr),
            in_specs=[pl.BlockSpec((1, 1, tr, W), lambda k, i, c_ref: (k, c_ref[0], i, 0)),
                      pl.BlockSpec((1, tr, W), lambda k, i, c_ref: (k, i, 0))],
            out_specs=pl.BlockSpec((1, tr, W), lambda k, i, c_ref: (k, i, 0))),
        out_shape=jax.ShapeDtypeStruct((n, rh, W), BF),
        compiler_params=_params("parallel", "parallel"),
    )(c.reshape(1).astype(jnp.int32), g, t)


def _chip_exchange(parts):
    n = len(parts)

    def body(*refs):
        a_refs, t_refs, (send_sems, recv_sems) = refs[:n], refs[n:2 * n], refs[2 * n:]
        x, y, c = _place()
        cps = [_remote(a_refs[g].at[2 * cx + cy], t_refs[g].at[j], send_sems, recv_sems, 3 * g + j, (cx, cy, c))
               for j, (cx, cy) in enumerate(_other_chips(x, y)) for g in range(n)]
        for cp in cps:
            cp.start()
        for cp in cps:
            cp.wait()

    return pl.pallas_call(
        body, name="grad_chip_exchange", in_specs=[ANY] * n, out_specs=[ANY] * n,
        out_shape=[jax.ShapeDtypeStruct((3,) + a.shape[1:], a.dtype) for a in parts],
        scratch_shapes=[pltpu.SemaphoreType.DMA((3 * n,)), pltpu.SemaphoreType.DMA((3 * n,))],
    )(*parts)


def _chip_add(g, t1, t2, c, me, name):
    _, _, rh, W = g.shape
    tr = _tile(rh, 256, 2 * SUBLANES)

    def body(c_ref, me_ref, g_ref, t1_ref, t2_ref, o_ref):
        own = g_ref[0, 0] + t1_ref[0]
        o_ref[...] = own + t2_ref[0].astype(F32) + t2_ref[1].astype(F32) + t2_ref[2].astype(F32)

    return pl.pallas_call(
        body, name=name,
        grid_spec=pltpu.PrefetchScalarGridSpec(
            num_scalar_prefetch=2, grid=(rh // tr,),
            in_specs=[pl.BlockSpec((1, 1, tr, W), lambda i, c_ref, me_ref: (me_ref[0], c_ref[0], i, 0)),
                      pl.BlockSpec((1, tr, W), lambda i, c_ref, me_ref: (me_ref[0], i, 0)),
                      pl.BlockSpec((3, tr, W), lambda i, c_ref, me_ref: (0, i, 0))],
            out_specs=pl.BlockSpec((tr, W), lambda i, c_ref, me_ref: (i, 0))),
        out_shape=jax.ShapeDtypeStruct((rh, W), F32),
        compiler_params=_params("parallel"),
    )(c.reshape(1).astype(jnp.int32), me.reshape(1).astype(jnp.int32), g, t1, t2)


def _pair_share(hs):
    n = len(hs)

    def body(*refs):
        h_refs, f_refs, (send_sems, recv_sems) = refs[:n], refs[n:2 * n], refs[2 * n:]
        x, y, c = _place()
        cps = [_remote(h_refs[g], f_refs[g], send_sems, recv_sems, g, (x, y, 1 - c)) for g in range(n)]
        for cp in cps:
            cp.start()
        for cp in cps:
            cp.wait()

    return pl.pallas_call(
        body, name="grad_pair_share", in_specs=[ANY] * n, out_specs=[ANY] * n,
        out_shape=[jax.ShapeDtypeStruct(a.shape, a.dtype) for a in hs],
        scratch_shapes=[pltpu.SemaphoreType.DMA((n,)), pltpu.SemaphoreType.DMA((n,))],
    )(*hs)


def _allreduce_small(pack, name):
    rows, W = pack.shape

    def body(p_ref, o_ref, buf, send_sems, recv_sems):
        x, y, c = _place()
        me = 4 * x + 2 * y + c
        buf[me] = p_ref[...]
        cps = []
        for r in range(1, 8):
            fx, fy, fc = (r >> 2) & 1, (r >> 1) & 1, r & 1
            to = (1 - x if fx else x, 1 - y if fy else y, 1 - c if fc else c)
            cps.append(_remote(p_ref, buf.at[me], send_sems, recv_sems, r - 1, to))
        for cp in cps:
            cp.start()
        for r in range(1, 8):
            fx, fy, fc = (r >> 2) & 1, (r >> 1) & 1, r & 1
            frm = 4 * (1 - x if fx else x) + 2 * (1 - y if fy else y) + (1 - c if fc else c)
            _remote(p_ref, buf.at[frm], send_sems, recv_sems, r - 1, (x, y, c)).wait_recv()
        for cp in cps:
            cp.wait_send()
        acc = buf[0]
        for i in range(1, 8):
            acc = acc + buf[i]
        o_ref[...] = acc

    return pl.pallas_call(
        body, name=name, in_specs=[VMEM], out_specs=VMEM,
        out_shape=jax.ShapeDtypeStruct((rows, W), F32),
        scratch_shapes=[pltpu.VMEM((8, rows, W), F32), pltpu.SemaphoreType.DMA((7,)), pltpu.SemaphoreType.DMA((7,))],
    )(pack)


def _width_groups(arrs):
    widths = []
    for a in arrs:
        if a.shape[-1] not in widths:
            widths.append(a.shape[-1])
    return [[i for i, a in enumerate(arrs) if a.shape[-1] == w] for w in widths]


def _rows2d(a):
    return a.reshape(-1, a.shape[-1])


def _split_rows_like(buf, like, lead=()):
    out, off = [], 0
    for a in like:
        n = a.size // a.shape[-1]
        out.append(buf[..., off:off + n, :].reshape(tuple(lead) + a.shape))
        off += n
    return out


def _join_cols(g):
    nd = g.ndim
    return jnp.moveaxis(g, 0, nd - 2).reshape(g.shape[1:-1] + (N_CHIPS * g.shape[-1],))


def _join_rows(g):
    return jnp.moveaxis(g, 0, 1).reshape(g.shape[1], N_CHIPS * g.shape[2], g.shape[3])


def _row_layout(a, tq):
    P, S, _ = a.shape
    return a.reshape(P, S // tq, tq, 2).transpose(0, 1, 3, 2)


def _pad_row(v, width=FLAT_W):
    flat = v.reshape(-1)
    rows = -(-flat.shape[0] // width)
    return jnp.pad(flat, (0, rows * width - flat.shape[0]))


def kernel(x, attn_norm, ffn_norm, a_w_in, a_conv, a_w_out, kv_norm, w_kvf, b_f, k_norm, b_w_qg, q_norm, b_w_out, ffn_w_up, ffn_conv, ffn_w_down, loss_target, m_attn_norm, m_ffn_norm, m_a_w_in, m_a_conv, m_a_w_out, m_kv_norm, m_w_kvf, m_b_f, m_k_norm, m_b_w_qg, m_q_norm, m_b_w_out, m_ffn_w_up, m_ffn_conv, m_ffn_w_down, v_attn_norm, v_ffn_norm, v_a_w_in, v_a_conv, v_a_w_out, v_kv_norm, v_w_kvf, v_b_f, v_k_norm, v_b_w_qg, v_q_norm, v_b_w_out, v_ffn_w_up, v_ffn_conv, v_ffn_w_down):
    xs = x[0]
    S, D = xs.shape
    H, hd = b_f.shape[0], k_norm.shape[0]
    depth = attn_norm.shape[0]
    n_a = a_w_in.shape[0]
    P = D // LANES
    assert LANES == 2 * hd and H * hd == D, "the attention kernels hold two heads per lane tile"
    mx, my, mc = _place()
    chip = 2 * mx + my

    big = [a_w_in, a_w_out, w_kvf, b_w_qg, b_w_out, ffn_w_up, ffn_w_down]
    groups = _width_groups(big)
    gathered = _allgather_weights([jnp.concatenate([_rows2d(big[i]).astype(BF) for i in idx]) for idx in groups])
    by_chip = [None] * len(big)
    for idx, buf in zip(groups, gathered):
        for i, part in zip(idx, _split_rows_like(buf, [big[i] for i in idx], (N_CHIPS,))):
            by_chip[i] = part
    g_in, g_out, g_kvf, g_qg, g_bout, g_up, g_down = by_chip
    wa_in, wb_qg, w_up = _join_cols(g_in), _join_cols(g_qg), _join_cols(g_up)
    wa_out, wb_out, w_down = _join_rows(g_out), _join_rows(g_bout), _join_rows(g_down)
    kvf_cols = 2 * D + LANES
    wkvf = jnp.pad(_join_cols(g_kvf), ((0, 0), (0, kvf_cols - (2 * D + H))))

    def placed(shard):
        full = jnp.zeros(shard.shape[:-1] + (N_CHIPS, shard.shape[-1]), F32)
        full = lax.dynamic_update_slice_in_dim(full, shard[..., None, :], chip, axis=full.ndim - 2)
        return jnp.where(mc == 0, full, 0.0).reshape(-1)

    conv_pack = jnp.concatenate([_pad_row(placed(a_conv)), _pad_row(placed(ffn_conv))]).reshape(-1, FLAT_W)
    conv_full = _allreduce_small(conv_pack, "allgather_conv_taps").reshape(-1)
    n_ac = a_conv.size * N_CHIPS
    a_conv_f = conv_full[:n_ac].reshape(a_conv.shape[:-1] + (-1,))
    off = _pad_row(placed(a_conv)).shape[0]
    ffn_conv_f = conv_full[off:off + ffn_conv.size * N_CHIPS].reshape(ffn_conv.shape[:-1] + (-1,))
    F = ffn_conv_f.shape[-1]

    b_pad = jnp.pad(b_f, (0, LANES - H)).reshape(1, LANES)
    gate_blk = 2 * D // LANES
    tq = _attn_tile(S)
    scale = hd ** -0.5

    saved = []
    cur = xs
    kv = None
    for l in range(depth):
        rec = {"x_in": cur}
        if l < n_a:
            proj, xn, z = _mixer_in_fwd(cur, attn_norm[l], wa_in[l], a_conv_f[l], f"a_in_{l}")
            mid = _matmul_residual(z, wa_out[l], cur, f"a_out_{l}")
            rec.update(proj=proj, xn=xn, z=z)
        else:
            j = l - n_a
            if kv is None:
                kvf, hn = _norm_matmul(cur, kv_norm, wkvf, 1, F32, "kvf_proj")
                kn = _headnorm(kvf, 0, 0, k_norm, 1.0, D, "k_norm")
                vb = kvf[0, :, D:2 * D].astype(BF)
                cgate = _gate_fwd(kvf, b_pad, gate_blk, "gate_cumsum")
                kv = dict(kvf=kvf, hn=hn, vb=vb, cgate=cgate, x_in=cur, dk=[], dv=[], dc=[],
                          ka=_augment(kn, cgate, "k", hd, "k_augment"), va=_augment(vb, cgate, "v", hd, "v_augment"))
            qg, xn = _norm_matmul(cur, attn_norm[l], wb_qg[j], 2, F32, f"qg_proj_{j}")
            qn = _headnorm(qg, 0, 0, q_norm[j], scale * LOG2E, D, f"q_norm_{j}")
            qa = _augment(qn, kv["cgate"], "q", hd, f"q_augment_{j}")
            o, og, m_max, l_sum = _attn_fwd(qa, kv["ka"], kv["va"], qg, hd, f"attn_fwd_{j}")
            mid = _matmul_residual(og, wb_out[j], cur, f"b_out_{j}")
            rec.update(qg=qg, xn=xn, qa=qa, o=o, og=og, m=m_max, l=l_sum)
        up, xn2, z2 = _ffn_up_fwd(mid, ffn_norm[l], w_up[l], ffn_conv_f[l], f"ffn_up_{l}")
        cur = _matmul_residual(z2, w_down[l], mid, f"ffn_down_{l}")
        rec.update(x_mid=mid, up=up, xn2=xn2, z2=z2)
        saved.append(rec)

    dy, loss_part = _loss_head(cur, loss_target[0], "loss_head")

    g_attn_norm, g_ffn_norm = [None] * depth, [None] * depth
    g_a_in, g_a_conv, g_a_out = [None] * n_a, [None] * n_a, [None] * n_a
    g_qg, g_qn, g_bo = [None] * (depth - n_a), [None] * (depth - n_a), [None] * (depth - n_a)
    g_up, g_fc, g_down = [None] * depth, [None] * depth, [None] * depth
    for l in reversed(range(depth)):
        rec = saved[l]
        dup, g_fc[l] = _ffn_mid_bwd(rec["up"], dy, w_down[l], ffn_conv_f[l], f"ffn_mid_bwd_{l}")
        g_down[l] = _wgrad(rec["z2"], dy[None], f"ffn_down_wgrad_{l}")
        g_up[l] = _wgrad(rec["xn2"], dup, f"ffn_up_wgrad_{l}")
        dy, g_ffn_norm[l] = _dnorm(dup, w_up[l], rec["x_mid"], ffn_norm[l], dy, f"ffn_up_bwd_{l}")
        if l < n_a:
            dproj, g_a_conv[l] = _mixer_mid_bwd(rec["proj"], dy, wa_out[l], a_conv_f[l], f"a_mid_bwd_{l}")
            g_a_out[l] = _wgrad(rec["z"], dy[None], f"a_out_wgrad_{l}")
            g_a_in[l] = _wgrad(rec["xn"], dproj, f"a_in_wgrad_{l}")
            dy, g_attn_norm[l] = _dnorm(dproj, wa_in[l], rec["x_in"], attn_norm[l], dy, f"a_in_bwd_{l}")
        else:
            j = l - n_a
            dog = _matmul_nt(dy, wb_out[j], f"b_out_bwd_{j}")
            g_out, dgate, evec = _attn_out_bwd(dog, rec["o"], rec["qg"], rec["l"], hd, f"attn_gate_bwd_{j}")
            g_bo[j] = _wgrad(rec["og"], dy[None], f"b_out_wgrad_{j}")
            dqn, dk, dv, dc = _attn_bwd(rec["qa"], kv["ka"], kv["vb"], g_out, _row_layout(rec["m"], tq),
                                        _row_layout(evec, tq), hd, f"attn_bwd_{j}")
            kv["dk"].append(dk)
            kv["dv"].append(dv)
            kv["dc"].append(dc)
            dq_pre, g_qn[j] = _headnorm_bwd(rec["qg"], 0, 0, q_norm[j], [dqn], D, f"q_norm_bwd_{j}")
            dqg = jnp.stack([dq_pre, dgate])
            g_qg[j] = _wgrad(rec["xn"], dqg, f"qg_wgrad_{j}")
            dy, g_attn_norm[l] = _dnorm(dqg, wb_qg[j], rec["x_in"], attn_norm[l], dy, f"qg_bwd_{j}")
            if l == n_a:
                dk_s, g_k_norm = _headnorm_bwd(kv["kvf"], 0, 0, k_norm, kv["dk"], D, "k_norm_bwd")
                dv_s = functools.reduce(jnp.add, kv["dv"]).astype(BF)
                dc_sum = functools.reduce(jnp.add, kv["dc"])
                dc_pad = jnp.pad(dc_sum.transpose(1, 0, 2).reshape(S, H), ((0, 0), (0, LANES - H)))
                df, db = _gate_bwd(dc_pad, kv["kvf"], b_pad, gate_blk, "gate_bwd")
                dkvf = jnp.concatenate([dk_s, dv_s, df.astype(BF)], axis=1)[None]
                g_kvf = _wgrad(kv["hn"], dkvf, "kvf_wgrad")[:, :2 * D + H]
                g_b_f = db[:H]
                dy, g_kv_norm = _dnorm(dkvf, wkvf, kv["x_in"], kv_norm, dy, "kvf_bwd")
    grad_x = dy[None]

    def cols_of(g, k):
        c = g.shape[-1] // N_CHIPS
        return g[:, k * c:(k + 1) * c]

    def rows_of(g, k):
        r = g.shape[0] // N_CHIPS
        return g[k * r:(k + 1) * r]

    per_layer = [g_a_in, g_a_out, [g_kvf], g_qg, g_bo, g_up, g_down]
    of_chip = [cols_of, rows_of, cols_of, cols_of, rows_of, cols_of, rows_of]

    def group_buffer(idx):
        rows = [of_chip[i](g, k) for k in range(N_CHIPS) for i in idx for g in per_layer[i]]
        buf = jnp.concatenate(rows)
        return buf.reshape(N_CHIPS, 2, buf.shape[0] // (2 * N_CHIPS), buf.shape[1])

    g4 = [group_buffer(idx) for idx in groups]
    from_sibling = _pair_exchange(g4)
    from_chips = _chip_exchange([_pair_add(g, t, mc, f"grad_pair_add_{n}") for n, (g, t) in
                                 enumerate(zip(g4, from_sibling))])
    mine = [_chip_add(g, t1, t2, mc, chip, f"grad_chip_add_{n}") for n, (g, t1, t2) in
            enumerate(zip(g4, from_sibling, from_chips))]
    theirs = _pair_share(mine)
    big_grads = [None] * len(big)
    for idx, m_half, t_half in zip(groups, mine, theirs):
        shard = jnp.where(mc == 0, jnp.concatenate([m_half, t_half]), jnp.concatenate([t_half, m_half]))
        for i, part in zip(idx, _split_rows_like(shard, [big[i] for i in idx])):
            big_grads[i] = part

    small = [loss_part[0, :1], jnp.stack(g_attn_norm), jnp.stack(g_ffn_norm), g_kv_norm, g_b_f, g_k_norm,
             jnp.stack(g_qn), jnp.stack(g_a_conv), jnp.stack(g_fc)]
    small_sum = _allreduce_small(jnp.concatenate([_pad_row(s) for s in small]).reshape(-1, FLAT_W),
                                 "allreduce_small_grads").reshape(-1)
    parts, off = [], 0
    for s in small:
        parts.append(small_sum[off:off + s.size].reshape(s.shape))
        off += _pad_row(s).shape[0]
    loss = parts[0][0]
    gr_attn_norm, gr_ffn_norm, gr_kv_norm, gr_b_f, gr_k_norm, gr_q_norm, gr_a_conv_full, gr_ffn_conv_full = parts[1:]

    def my_cols(full):
        c = full.shape[-1] // N_CHIPS
        return lax.dynamic_slice_in_dim(full, chip * c, c, axis=full.ndim - 1)

    gr_a_in, gr_a_out, gr_kvf, gr_qg, gr_bo, gr_up, gr_down = big_grads
    grads = [gr_attn_norm, gr_ffn_norm, gr_a_in, my_cols(gr_a_conv_full), gr_a_out, gr_kv_norm, gr_kvf, gr_b_f,
             gr_k_norm, gr_qg, gr_q_norm, gr_bo, gr_up, my_cols(gr_ffn_conv_full), gr_down]
    weights = [attn_norm, ffn_norm, a_w_in, a_conv, a_w_out, kv_norm, w_kvf, b_f, k_norm, b_w_qg, q_norm, b_w_out,
               ffn_w_up, ffn_conv, ffn_w_down]
    ms = [m_attn_norm, m_ffn_norm, m_a_w_in, m_a_conv, m_a_w_out, m_kv_norm, m_w_kvf, m_b_f, m_k_norm, m_b_w_qg,
          m_q_norm, m_b_w_out, m_ffn_w_up, m_ffn_conv, m_ffn_w_down]
    vs = [v_attn_norm, v_ffn_norm, v_a_w_in, v_a_conv, v_a_w_out, v_kv_norm, v_w_kvf, v_b_f, v_k_norm, v_b_w_qg,
          v_q_norm, v_b_w_out, v_ffn_w_up, v_ffn_conv, v_ffn_w_down]
    deltas, new_ms, new_vs = [], [], []
    for i, (w, g, m, v) in enumerate(zip(weights, grads, ms, vs)):
        d, nm, nv = _adamw(w, g, m, v, f"adamw_{i}")
        deltas.append(d)
        new_ms.append(nm)
        new_vs.append(nv)
    return (loss, grad_x, *grads, *deltas, *new_ms, *new_vs)
```

```python
import functools

import jax
import jax.numpy as jnp
from jax import lax
from jax.experimental import pallas as pl
from jax.experimental.pallas import tpu as pltpu

F32 = jnp.float32
BF = jnp.bfloat16
LANES = 128
SUBLANES = 8
RMS_EPS = 1e-6
LOG2E = 1.4426950408889634
FLAT_W = 1024
N_CHIPS = 4
CONV_W = 3
HALO = SUBLANES

ADAM_LR = 0.001
ADAM_B1 = 0.9
ADAM_B2 = 0.999
ADAM_EPS = 1e-08
ADAM_WD = 0.01
ADAM_STEP = 10

MESH = pl.DeviceIdType.MESH
ANY = pl.BlockSpec(memory_space=pl.ANY)
VMEM = pl.BlockSpec(memory_space=pltpu.VMEM)
NT_DIMS = (((1,), (1,)), ((), ()))
TN_DIMS = (((0,), (0,)), ((), ()))


def _tile(n, pref, mult=LANES):
    t = (min(pref, n) // mult) * mult
    while t >= mult:
        if n % t == 0:
            break
        t -= mult
    if t < mult or (t * 4 < pref and n <= 4 * pref):
        return n
    return t


def _params(*sem):
    return pltpu.CompilerParams(dimension_semantics=sem)


def _norm_matmul(x, g, w, parts, out_dtype, name):
    S, D = x.shape
    C = w.shape[1] // parts
    ts, tn = _tile(S, 512, SUBLANES), _tile(C, 1408)
    npc = C // tn

    def body(x_ref, g_ref, w_ref, o_ref, xn_ref):
        @pl.when(pl.program_id(1) == 0)
        def _():
            xf = x_ref[...]
            r = lax.rsqrt(jnp.mean(xf * xf, axis=-1, keepdims=True) + RMS_EPS)
            xn_ref[...] = (xf * r * g_ref[...]).astype(BF)

        o_ref[0] = jnp.dot(xn_ref[...], w_ref[...], preferred_element_type=F32).astype(out_dtype)

    return pl.pallas_call(
        body, name=name, grid=(S // ts, parts * npc),
        in_specs=[pl.BlockSpec((ts, D), lambda s, n: (s, 0)),
                  pl.BlockSpec((1, D), lambda s, n: (0, 0)),
                  pl.BlockSpec((D, tn), lambda s, n: (0, n))],
        out_specs=[pl.BlockSpec((1, ts, tn), lambda s, n: (n // npc, s, n % npc)),
                   pl.BlockSpec((ts, D), lambda s, n: (s, 0))],
        out_shape=[jax.ShapeDtypeStruct((parts, S, C), out_dtype), jax.ShapeDtypeStruct((S, D), BF)],
        compiler_params=_params("parallel", "arbitrary"),
    )(x, g.reshape(1, D), w)


def _shift_down(u, prev, k):
    r = pltpu.roll(u, k, 0)
    row = lax.broadcasted_iota(jnp.int32, (HALO, u.shape[1]), 0)
    head = r[0:HALO]
    for j in range(k):
        head = jnp.where(row == j, prev[HALO - k + j:HALO - k + j + 1, :], head)
    return jnp.concatenate([head, r[HALO:]], axis=0)


def _shift_up(d, nxt, k):
    n = d.shape[0]
    r = pltpu.roll(d, n - k, 0)
    row = lax.broadcasted_iota(jnp.int32, (HALO, d.shape[1]), 0)
    tail = r[n - HALO:n]
    for j in range(k):
        tail = jnp.where(row == HALO - k + j, nxt[j:j + 1, :], tail)
    return jnp.concatenate([r[0:n - HALO], tail], axis=0)


def _conv3(u, prev, w):
    return _shift_down(u, prev, 2) * w[0:1] + _shift_down(u, prev, 1) * w[1:2] + u * w[2:3]


def _conv3_t(d, nxt, w):
    return d * w[2:3] + _shift_up(d, nxt, 1) * w[1:2] + _shift_up(d, nxt, 2) * w[0:1]


def _tap_rows(t0, t1, t2):
    row = lax.broadcasted_iota(jnp.int32, (SUBLANES, t0.shape[1]), 0)
    return jnp.where(row == 0, t0, jnp.where(row == 1, t1, jnp.where(row == 2, t2, 0.0)))


def _pad_conv(cw):
    return jnp.pad(cw, ((0, SUBLANES - CONV_W), (0, 0)))


def _mixer_in_fwd(x, g, w, cw, name):
    S, D = x.shape
    C = w.shape[1] // 3
    ts, tc = _tile(S, 512, SUBLANES), _tile(C, 1024)
    nc = C // tc

    def body(x_ref, g_ref, wb_ref, wc_ref, wh_ref, cw_ref, p_ref, xn_ref, z_ref, carry):
        s, c = pl.program_id(0), pl.program_id(1)

        @pl.when(c == 0)
        def _():
            xf = x_ref[...]
            r = lax.rsqrt(jnp.mean(xf * xf, axis=-1, keepdims=True) + RMS_EPS)
            xn_ref[...] = (xf * r * g_ref[...]).astype(BF)

        @pl.when(s == 0)
        def _():
            carry[c] = jnp.zeros((HALO, tc), F32)

        xn = xn_ref[...]
        parts = [jnp.dot(xn, w_ref[...], preferred_element_type=F32).astype(BF) for w_ref in (wb_ref, wc_ref, wh_ref)]
        for p, v in enumerate(parts):
            p_ref[p] = v
        u = parts[1].astype(F32) * parts[2].astype(F32)
        cv = _conv3(u, carry[c], cw_ref[...])
        z_ref[...] = (parts[0].astype(F32) * cv).astype(BF)
        carry[c] = u[ts - HALO:ts, :]

    wspec = lambda p: pl.BlockSpec((D, tc), lambda s, c: (0, p * nc + c))
    return pl.pallas_call(
        body, name=name, grid=(S // ts, nc),
        in_specs=[pl.BlockSpec((ts, D), lambda s, c: (s, 0)), pl.BlockSpec((1, D), lambda s, c: (0, 0)),
                  wspec(0), wspec(1), wspec(2), pl.BlockSpec((SUBLANES, tc), lambda s, c: (0, c))],
        out_specs=[pl.BlockSpec((3, ts, tc), lambda s, c: (0, s, c)),
                   pl.BlockSpec((ts, D), lambda s, c: (s, 0)),
                   pl.BlockSpec((ts, tc), lambda s, c: (s, c))],
        out_shape=[jax.ShapeDtypeStruct((3, S, C), BF), jax.ShapeDtypeStruct((S, D), BF),
                   jax.ShapeDtypeStruct((S, C), BF)],
        scratch_shapes=[pltpu.VMEM((nc, HALO, tc), F32)],
        compiler_params=_params("arbitrary", "arbitrary"),
    )(x, g.reshape(1, D), w, w, w, _pad_conv(cw))


def _ffn_up_fwd(x, g, w, cw, name):
    S, D = x.shape
    C = w.shape[1] // 2
    ts, tc = _tile(S, 512, SUBLANES), _tile(C, 1408)
    nc = C // tc

    def body(x_ref, g_ref, wa_ref, wg_ref, cw_ref, up_ref, xn_ref, z_ref, carry):
        s, c = pl.program_id(0), pl.program_id(1)

        @pl.when(c == 0)
        def _():
            xf = x_ref[...]
            r = lax.rsqrt(jnp.mean(xf * xf, axis=-1, keepdims=True) + RMS_EPS)
            xn_ref[...] = (xf * r * g_ref[...]).astype(BF)

        @pl.when(s == 0)
        def _():
            carry[c] = jnp.zeros((HALO, tc), F32)

        xn = xn_ref[...]
        a_b = jnp.dot(xn, wa_ref[...], preferred_element_type=F32).astype(BF)
        g_b = jnp.dot(xn, wg_ref[...], preferred_element_type=F32).astype(BF)
        up_ref[0] = a_b
        up_ref[1] = g_b
        a_pre = a_b.astype(F32)
        a = _conv3(a_pre, carry[c], cw_ref[...])
        z_ref[...] = (a * jax.nn.sigmoid(a) * g_b.astype(F32)).astype(BF)
        carry[c] = a_pre[ts - HALO:ts, :]

    return pl.pallas_call(
        body, name=name, grid=(S // ts, nc),
        in_specs=[pl.BlockSpec((ts, D), lambda s, c: (s, 0)),
                  pl.BlockSpec((1, D), lambda s, c: (0, 0)),
                  pl.BlockSpec((D, tc), lambda s, c: (0, c)),
                  pl.BlockSpec((D, tc), lambda s, c: (0, nc + c)),
                  pl.BlockSpec((SUBLANES, tc), lambda s, c: (0, c))],
        out_specs=[pl.BlockSpec((2, ts, tc), lambda s, c: (0, s, c)),
                   pl.BlockSpec((ts, D), lambda s, c: (s, 0)),
                   pl.BlockSpec((ts, tc), lambda s, c: (s, c))],
        out_shape=[jax.ShapeDtypeStruct((2, S, C), BF), jax.ShapeDtypeStruct((S, D), BF),
                   jax.ShapeDtypeStruct((S, C), BF)],
        scratch_shapes=[pltpu.VMEM((nc, HALO, tc), F32)],
        compiler_params=_params("arbitrary", "arbitrary"),
    )(x, g.reshape(1, D), w, w, _pad_conv(cw))


def _mixer_mid_bwd(proj, dy, w_out, cw, name):
    _, S, C = proj.shape
    D = dy.shape[1]
    ts, tc = _tile(S, 512, SUBLANES), _tile(C, 1024)
    n_s = S // ts
    per = ts // HALO

    def body(b_ref, c_ref, h_ref, dy_ref, w_ref, cp_ref, hp_ref, cw_ref, d_ref, dcw_ref, carry):
        i = pl.program_id(1)
        w = cw_ref[...]
        dz = lax.dot_general(dy_ref[...].astype(BF), w_ref[...], NT_DIMS, preferred_element_type=F32)
        b, c, h = b_ref[0].astype(F32), c_ref[0].astype(F32), h_ref[0].astype(F32)
        u = c * h
        prev = jnp.where(i < n_s - 1, cp_ref[0].astype(F32) * hp_ref[0].astype(F32), 0.0)
        u1, u2 = _shift_down(u, prev, 1), _shift_down(u, prev, 2)
        cv = u2 * w[0:1] + u1 * w[1:2] + u * w[2:3]
        dcv = dz * b
        nxt = jnp.where(i > 0, carry[...], 0.0)
        du = _conv3_t(dcv, nxt, w)
        d_ref[0] = (dz * cv).astype(BF)
        d_ref[1] = (du * h).astype(BF)
        d_ref[2] = (du * c).astype(BF)
        carry[...] = dcv[0:HALO, :]
        part = _tap_rows(jnp.sum(dcv * u2, axis=0, keepdims=True), jnp.sum(dcv * u1, axis=0, keepdims=True),
                         jnp.sum(dcv * u, axis=0, keepdims=True))

        @pl.when(i == 0)
        def _():
            dcw_ref[...] = part

        @pl.when(i > 0)
        def _():
            dcw_ref[...] += part

    tile = lambda p: pl.BlockSpec((1, ts, tc), lambda c, i: (p, n_s - 1 - i, c))
    before = lambda p: pl.BlockSpec((1, HALO, tc), lambda c, i: (p, jnp.maximum((n_s - 1 - i) * per - 1, 0), c))
    dproj, dcw = pl.pallas_call(
        body, name=name, grid=(C // tc, n_s),
        in_specs=[tile(0), tile(1), tile(2),
                  pl.BlockSpec((ts, D), lambda c, i: (n_s - 1 - i, 0)),
                  pl.BlockSpec((tc, D), lambda c, i: (c, 0)),
                  before(1), before(2),
                  pl.BlockSpec((SUBLANES, tc), lambda c, i: (0, c))],
        out_specs=[pl.BlockSpec((3, ts, tc), lambda c, i: (0, n_s - 1 - i, c)),
                   pl.BlockSpec((SUBLANES, tc), lambda c, i: (0, c))],
        out_shape=[jax.ShapeDtypeStruct((3, S, C), BF), jax.ShapeDtypeStruct((SUBLANES, C), F32)],
        scratch_shapes=[pltpu.VMEM((HALO, tc), F32)],
        compiler_params=_params("parallel", "arbitrary"),
    )(proj, proj, proj, dy, w_out, proj, proj, _pad_conv(cw))
    return dproj, dcw[:CONV_W]


def _ffn_mid_bwd(up, dy, w_down, cw, name):
    _, S, C = up.shape
    D = dy.shape[1]
    ts, tc = _tile(S, 512, SUBLANES), _tile(C, 1408)
    n_s = S // ts
    per = ts // HALO

    def body(a_ref, g_ref, dy_ref, w_ref, ap_ref, cw_ref, d_ref, dcw_ref, carry):
        i = pl.program_id(1)
        w = cw_ref[...]
        dz = lax.dot_general(dy_ref[...].astype(BF), w_ref[...], NT_DIMS, preferred_element_type=F32)
        a_pre, g = a_ref[0].astype(F32), g_ref[0].astype(F32)
        prev = jnp.where(i < n_s - 1, ap_ref[0].astype(F32), 0.0)
        a1, a2 = _shift_down(a_pre, prev, 1), _shift_down(a_pre, prev, 2)
        a = a2 * w[0:1] + a1 * w[1:2] + a_pre * w[2:3]
        sg = jax.nn.sigmoid(a)
        da = dz * g * (sg * (1.0 + a * (1.0 - sg)))
        nxt = jnp.where(i > 0, carry[...], 0.0)
        d_ref[0] = _conv3_t(da, nxt, w).astype(BF)
        d_ref[1] = (dz * (a * sg)).astype(BF)
        carry[...] = da[0:HALO, :]
        part = _tap_rows(jnp.sum(da * a2, axis=0, keepdims=True), jnp.sum(da * a1, axis=0, keepdims=True),
                         jnp.sum(da * a_pre, axis=0, keepdims=True))

        @pl.when(i == 0)
        def _():
            dcw_ref[...] = part

        @pl.when(i > 0)
        def _():
            dcw_ref[...] += part

    tile = lambda p: pl.BlockSpec((1, ts, tc), lambda c, i: (p, n_s - 1 - i, c))
    dup, dcw = pl.pallas_call(
        body, name=name, grid=(C // tc, n_s),
        in_specs=[tile(0), tile(1),
                  pl.BlockSpec((ts, D), lambda c, i: (n_s - 1 - i, 0)),
                  pl.BlockSpec((tc, D), lambda c, i: (c, 0)),
                  pl.BlockSpec((1, HALO, tc), lambda c, i: (0, jnp.maximum((n_s - 1 - i) * per - 1, 0), c)),
                  pl.BlockSpec((SUBLANES, tc), lambda c, i: (0, c))],
        out_specs=[pl.BlockSpec((2, ts, tc), lambda c, i: (0, n_s - 1 - i, c)),
                   pl.BlockSpec((SUBLANES, tc), lambda c, i: (0, c))],
        out_shape=[jax.ShapeDtypeStruct((2, S, C), BF), jax.ShapeDtypeStruct((SUBLANES, C), F32)],
        scratch_shapes=[pltpu.VMEM((HALO, tc), F32)],
        compiler_params=_params("parallel", "arbitrary"),
    )(up, up, dy, w_down, up, _pad_conv(cw))
    return dup, dcw[:CONV_W]


def _matmul_residual(z, w, x, name):
    S, K = z.shape
    D = w.shape[1]
    ts = _tile(S, 512, SUBLANES)

    def body(z_ref, w_ref, x_ref, o_ref):
        o_ref[...] = x_ref[...] + jnp.dot(z_ref[...], w_ref[...], preferred_element_type=F32)

    return pl.pallas_call(
        body, name=name, grid=(S // ts,),
        in_specs=[pl.BlockSpec((ts, K), lambda s: (s, 0)), pl.BlockSpec((K, D), lambda s: (0, 0)),
                  pl.BlockSpec((ts, D), lambda s: (s, 0))],
        out_specs=pl.BlockSpec((ts, D), lambda s: (s, 0)),
        out_shape=jax.ShapeDtypeStruct((S, D), F32),
        compiler_params=_params("parallel"),
    )(z, w, x)


def _matmul_nt(a, w, name):
    S, K = a.shape
    N = w.shape[0]
    ts, tn = _tile(S, 512, SUBLANES), _tile(N, 1408)

    def body(a_ref, w_ref, o_ref, abf):
        @pl.when(pl.program_id(1) == 0)
        def _():
            abf[...] = a_ref[...].astype(BF)

        o_ref[...] = lax.dot_general(abf[...], w_ref[...], NT_DIMS, preferred_element_type=F32)

    return pl.pallas_call(
        body, name=name, grid=(S // ts, N // tn),
        in_specs=[pl.BlockSpec((ts, K), lambda s, n: (s, 0)), pl.BlockSpec((tn, K), lambda s, n: (n, 0))],
        out_specs=pl.BlockSpec((ts, tn), lambda s, n: (s, n)),
        out_shape=jax.ShapeDtypeStruct((S, N), F32),
        scratch_shapes=[pltpu.VMEM((ts, K), BF)],
        compiler_params=_params("parallel", "arbitrary"),
    )(a, w)


def _wgrad(a, b, name):
    S, M = a.shape
    P, _, C = b.shape
    tm, tn, tk = _tile(M, 1408), _tile(C, 1408), _tile(S, 512, SUBLANES)
    nnc = C // tn

    def body(a_ref, b_ref, o_ref):
        @pl.when(pl.program_id(2) == 0)
        def _():
            o_ref[...] = jnp.zeros_like(o_ref)

        o_ref[...] += lax.dot_general(a_ref[...], b_ref[0].astype(BF), TN_DIMS, preferred_element_type=F32)

    return pl.pallas_call(
        body, name=name, grid=(M // tm, P * nnc, S // tk),
        in_specs=[pl.BlockSpec((tk, tm), lambda m, n, k: (k, m)),
                  pl.BlockSpec((1, tk, tn), lambda m, n, k: (n // nnc, k, n % nnc))],
        out_specs=pl.BlockSpec((tm, tn), lambda m, n, k: (m, n)),
        out_shape=jax.ShapeDtypeStruct((M, P * C), F32),
        compiler_params=_params("parallel", "parallel", "arbitrary"),
    )(a, b)


def _dnorm(dp, w, x, g, dy, name):
    P, S, C = dp.shape
    D = x.shape[1]
    ts = _tile(S, 512, SUBLANES)

    def body(dp_ref, w_ref, x_ref, g_ref, dy_ref, dx_ref, dg_ref):
        @pl.when(pl.program_id(0) == 0)
        def _():
            dg_ref[...] = jnp.zeros_like(dg_ref)

        dxn = lax.dot_general(dp_ref[0], w_ref[:, 0:C], NT_DIMS, preferred_element_type=F32)
        for p in range(1, P):
            dxn = dxn + lax.dot_general(dp_ref[p], w_ref[:, p * C:(p + 1) * C], NT_DIMS, preferred_element_type=F32)
        xf = x_ref[...]
        r = lax.rsqrt(jnp.mean(xf * xf, axis=-1, keepdims=True) + RMS_EPS)
        xhat = xf * r
        dxhat = dxn * g_ref[...]
        dx_ref[...] = dy_ref[...] + r * (dxhat - xhat * jnp.mean(dxhat * xhat, axis=-1, keepdims=True))
        dg_ref[...] += jnp.broadcast_to(jnp.sum(dxn * xhat, axis=0, keepdims=True), dg_ref.shape)

    dx, dg = pl.pallas_call(
        body, name=name, grid=(S // ts,),
        in_specs=[pl.BlockSpec((P, ts, C), lambda s: (0, s, 0)),
                  pl.BlockSpec((D, P * C), lambda s: (0, 0), pipeline_mode=pl.Buffered(1)),
                  pl.BlockSpec((ts, D), lambda s: (s, 0)),
                  pl.BlockSpec((1, D), lambda s: (0, 0)),
                  pl.BlockSpec((ts, D), lambda s: (s, 0))],
        out_specs=[pl.BlockSpec((ts, D), lambda s: (s, 0)),
                   pl.BlockSpec((SUBLANES, D), lambda s: (0, 0))],
        out_shape=[jax.ShapeDtypeStruct((S, D), F32), jax.ShapeDtypeStruct((SUBLANES, D), F32)],
        compiler_params=_params("arbitrary"),
    )(dp, w, x, g.reshape(1, D), dy)
    return dx, dg[0]


def _head_masks(shape, hd):
    lane = lax.broadcasted_iota(jnp.int32, shape, 1)
    return lane < hd


def _pair_sum(v, lo):
    s0 = jnp.sum(jnp.where(lo, v, 0.0), axis=-1, keepdims=True)
    s1 = jnp.sum(jnp.where(lo, 0.0, v), axis=-1, keepdims=True)
    return jnp.where(lo, s0, s1)


def _headnorm(src, part, colblk, w, scale, D, name):
    S = src.shape[1]
    hd = w.shape[0]
    ts = _tile(S, 512, SUBLANES)
    w2 = jnp.tile(w, LANES // hd).reshape(1, LANES)

    def body(x_ref, w_ref, o_ref):
        lo = _head_masks((ts, LANES), hd)
        for t in range(D // LANES):
            xt = x_ref[0, :, t * LANES:(t + 1) * LANES]
            r = lax.rsqrt(_pair_sum(xt * xt, lo) * (1.0 / hd) + RMS_EPS)
            o_ref[:, t * LANES:(t + 1) * LANES] = (xt * r * w_ref[...] * scale).astype(BF)

    return pl.pallas_call(
        body, name=name, grid=(S // ts,),
        in_specs=[pl.BlockSpec((1, ts, D), lambda s: (part, s, colblk)), pl.BlockSpec((1, LANES), lambda s: (0, 0))],
        out_specs=pl.BlockSpec((ts, D), lambda s: (s, 0)),
        out_shape=jax.ShapeDtypeStruct((S, D), BF),
        compiler_params=_params("parallel"),
    )(src, w2)


def _headnorm_bwd(src, part, colblk, w, dys, D, name):
    S = src.shape[1]
    hd = w.shape[0]
    ts = _tile(S, 512, SUBLANES)
    w2 = jnp.tile(w, LANES // hd).reshape(1, LANES)
    n_dy = len(dys)

    def body(x_ref, w_ref, *rest):
        dy_refs, dx_ref, dw_ref = rest[:n_dy], rest[n_dy], rest[n_dy + 1]

        @pl.when(pl.program_id(0) == 0)
        def _():
            dw_ref[...] = jnp.zeros_like(dw_ref)

        lo = _head_masks((ts, LANES), hd)
        for t in range(D // LANES):
            cols = slice(t * LANES, (t + 1) * LANES)
            xt = x_ref[0, :, cols]
            dy = dy_refs[0][:, cols]
            for other in dy_refs[1:]:
                dy = dy + other[:, cols]
            r = lax.rsqrt(_pair_sum(xt * xt, lo) * (1.0 / hd) + RMS_EPS)
            xhat = xt * r
            dxhat = dy * w_ref[...]
            mean = _pair_sum(dxhat * xhat, lo) * (1.0 / hd)
            dx_ref[:, cols] = (r * (dxhat - xhat * mean)).astype(BF)
            dw_ref[:, cols] += jnp.broadcast_to(jnp.sum(dy * xhat, axis=0, keepdims=True), (SUBLANES, LANES))

    dx, dw = pl.pallas_call(
        body, name=name, grid=(S // ts,),
        in_specs=[pl.BlockSpec((1, ts, D), lambda s: (part, s, colblk)), pl.BlockSpec((1, LANES), lambda s: (0, 0))]
        + [pl.BlockSpec((ts, D), lambda s: (s, 0))] * n_dy,
        out_specs=[pl.BlockSpec((ts, D), lambda s: (s, 0)), pl.BlockSpec((SUBLANES, D), lambda s: (0, 0))],
        out_shape=[jax.ShapeDtypeStruct((S, D), BF), jax.ShapeDtypeStruct((SUBLANES, D), F32)],
        compiler_params=_params("arbitrary"),
    )(src, w2, *dys)
    return dx, jnp.sum(dw[0].reshape(D // hd, hd), axis=0)


def _tri(n, lower):
    r, c = lax.broadcasted_iota(jnp.int32, (n, n), 0), lax.broadcasted_iota(jnp.int32, (n, n), 1)
    return jnp.where((c <= r) if lower else (c >= r), 1.0, 0.0).astype(BF)


def _dot_exact(t, v):
    hi = v.astype(BF)
    r1 = v - hi.astype(F32)
    mid = r1.astype(BF)
    lo = (r1 - mid.astype(F32)).astype(BF)
    dot = lambda u: jnp.dot(t, u, preferred_element_type=F32)
    return dot(hi) + dot(mid) + dot(lo)


def _gate_fwd(kvf, b_pad, colblk, name):
    S = kvf.shape[1]
    ts = _tile(S, 512, SUBLANES)

    def body(f_ref, b_ref, c_ref, carry):
        @pl.when(pl.program_id(0) == 0)
        def _():
            carry[...] = jnp.zeros_like(carry)

        f = f_ref[0] + b_ref[...]
        ls = jnp.minimum(f, 0.0) - jnp.log1p(jnp.exp(-jnp.abs(f)))
        tri = _tri(ts, lower=True)
        c = _dot_exact(tri, ls) + carry[0:1, :]
        c_ref[...] = c
        carry[...] = jnp.broadcast_to(c[ts - 1:ts, :], carry.shape)

    return pl.pallas_call(
        body, name=name, grid=(S // ts,),
        in_specs=[pl.BlockSpec((1, ts, LANES), lambda s: (0, s, colblk)), pl.BlockSpec((1, LANES), lambda s: (0, 0))],
        out_specs=pl.BlockSpec((ts, LANES), lambda s: (s, 0)),
        out_shape=jax.ShapeDtypeStruct((S, LANES), F32),
        scratch_shapes=[pltpu.VMEM((SUBLANES, LANES), F32)],
        compiler_params=_params("arbitrary"),
    )(kvf, b_pad)


def _gate_bwd(dc, kvf, b_pad, colblk, name):
    S = kvf.shape[1]
    ts = _tile(S, 512, SUBLANES)
    n_s = S // ts

    def body(dc_ref, f_ref, b_ref, df_ref, db_ref, carry):
        @pl.when(pl.program_id(0) == 0)
        def _():
            carry[...] = jnp.zeros_like(carry)
            db_ref[...] = jnp.zeros_like(db_ref)

        tri = _tri(ts, lower=False)
        dls = _dot_exact(tri, dc_ref[...]) + carry[0:1, :]
        f = f_ref[0] + b_ref[...]
        df = dls * jax.nn.sigmoid(-f)
        df_ref[...] = df
        db_ref[...] += jnp.broadcast_to(jnp.sum(df, axis=0, keepdims=True), db_ref.shape)
        carry[...] = jnp.broadcast_to(dls[0:1, :], carry.shape)

    df, db = pl.pallas_call(
        body, name=name, grid=(n_s,),
        in_specs=[pl.BlockSpec((ts, LANES), lambda s: (n_s - 1 - s, 0)),
                  pl.BlockSpec((1, ts, LANES), lambda s: (0, n_s - 1 - s, colblk)),
                  pl.BlockSpec((1, LANES), lambda s: (0, 0))],
        out_specs=[pl.BlockSpec((ts, LANES), lambda s: (n_s - 1 - s, 0)),
                   pl.BlockSpec((SUBLANES, LANES), lambda s: (0, 0))],
        out_shape=[jax.ShapeDtypeStruct((S, LANES), F32), jax.ShapeDtypeStruct((SUBLANES, LANES), F32)],
        scratch_shapes=[pltpu.VMEM((SUBLANES, LANES), F32)],
        compiler_params=_params("arbitrary"),
    )(dc, kvf, b_pad)
    return df, db[0]


def _attn_tile(S):
    return _tile(S, 512, LANES)


def _split_heads(v, lo):
    zero = jnp.zeros_like(v)
    return jnp.where(lo, v, zero), jnp.where(lo, zero, v)


def _augment(base, c, mode, hd, name):
    S, D = base.shape
    ts = _tile(S, 512, 2 * SUBLANES)

    def body(b_ref, c_ref, o0_ref, o1_ref):
        lane = lax.broadcasted_iota(jnp.int32, (ts, LANES), 1)
        cc = c_ref[...] * LOG2E
        for t in range(D // LANES):
            cols = slice(t * LANES, (t + 1) * LANES)
            bt = b_ref[:, cols]
            for h, o_ref in ((0, o0_ref), (1, o1_ref)):
                first = hd if h == 0 else 0
                keep = (lane < hd) if h == 0 else (lane >= hd)
                if mode == "v":
                    vals = (1.0,)
                else:
                    col = cc[:, 2 * t + h:2 * t + h + 1]
                    hi = col.astype(BF).astype(F32)
                    mid = (col - hi).astype(BF).astype(F32)
                    pieces = (hi, mid, col - hi - mid)
                    vals = pieces + (1.0, 1.0, 1.0) if mode == "q" else (1.0, 1.0, 1.0) + tuple(-v for v in pieces)
                aug = jnp.zeros((ts, LANES), F32)
                for i, v in enumerate(vals):
                    aug = jnp.where(lane == first + i, v, aug)
                o_ref[:, cols] = jnp.where(keep, bt, aug.astype(BF))

    spec = pl.BlockSpec((ts, D), lambda s: (s, 0))
    return pl.pallas_call(
        body, name=name, grid=(S // ts,),
        in_specs=[spec, pl.BlockSpec((ts, LANES), lambda s: (s, 0))], out_specs=[spec, spec],
        out_shape=[jax.ShapeDtypeStruct((S, D), BF)] * 2,
        compiler_params=_params("parallel"),
    )(base, c)


def _attn_fwd(qa, ka, va, qg, hd, name):
    S, D = qa[0].shape
    P = D // LANES
    tq = _attn_tile(S)
    nq = S // tq

    def body(q0_ref, q1_ref, k0_ref, k1_ref, v0_ref, v1_ref, g_ref, o_ref, og_ref, m_ref, l_ref, s_buf):
        qi = pl.program_id(1)
        lo = _head_masks((tq, LANES), hd)
        qh = (q0_ref[...], q1_ref[...])
        k_refs, v_refs = (k0_ref, k1_ref), (v0_ref, v1_ref)
        causal = lax.broadcasted_iota(jnp.int32, (tq, tq), 1) <= lax.broadcasted_iota(jnp.int32, (tq, tq), 0)

        def scores(ki, slot):
            off = pl.multiple_of(ki * tq, tq)
            for h in range(2):
                s_buf[slot, h] = lax.dot_general(qh[h], k_refs[h][pl.ds(off, tq), :], NT_DIMS,
                                                 preferred_element_type=F32)

        def consume(ki, slot, carry, masked):
            off = pl.multiple_of(ki * tq, tq)
            out = []
            for h in range(2):
                m, acc = carry[h]
                s = s_buf[slot, h]
                if masked:
                    s = jnp.where(causal, s, -jnp.inf)
                m_new = jnp.maximum(m, jnp.ceil(jnp.max(s, axis=-1, keepdims=True)))
                p = jnp.exp2(s - m_new)
                acc = jnp.exp2(m - m_new) * acc + jnp.dot(p.astype(BF), v_refs[h][pl.ds(off, tq), :],
                                                          preferred_element_type=F32)
                out.append((m_new, acc))
            return tuple(out)

        def step(j, carry):
            scores(2 * j + 1, 1)
            carry = consume(2 * j, 0, carry, False)
            scores(2 * j + 2, 0)
            return consume(2 * j + 1, 1, carry, False)

        def finish_even(carry):
            return consume(qi, 0, carry, True)

        def finish_odd(carry):
            scores(qi, 1)
            return consume(qi, 1, consume(qi - 1, 0, carry, False), True)

        init = tuple((jnp.full((tq, 1), -jnp.inf, F32), jnp.zeros((tq, LANES), F32)) for _ in range(2))
        scores(0, 0)
        carry = lax.fori_loop(0, qi // 2, step, init)
        (m0, a0), (m1, a1) = lax.cond(qi % 2 == 0, finish_even, finish_odd, carry)
        l0, l1 = a0[:, hd:hd + 1], a1[:, 0:1]
        o = jnp.where(lo, a0 / l0, a1 / l1)
        o_ref[...] = o
        og_ref[...] = (o * jax.nn.sigmoid(g_ref[0])).astype(BF)
        lane2 = lax.broadcasted_iota(jnp.int32, (tq, 2), 1)
        m_ref[0] = jnp.where(lane2 == 0, m0, m1)
        l_ref[0] = jnp.where(lane2 == 0, l0, l1)

    tile = pl.BlockSpec((tq, LANES), lambda p, i: (i, p))
    whole = pl.BlockSpec((S, LANES), lambda p, i: (0, p))
    stat = pl.BlockSpec((1, tq, 2), lambda p, i: (p, i, 0))
    return pl.pallas_call(
        body, name=name, grid=(P, nq),
        in_specs=[tile, tile, whole, whole, whole, whole, pl.BlockSpec((1, tq, LANES), lambda p, i: (1, i, p))],
        out_specs=[tile, tile, stat, stat],
        out_shape=[jax.ShapeDtypeStruct((S, D), F32), jax.ShapeDtypeStruct((S, D), BF),
                   jax.ShapeDtypeStruct((P, S, 2), F32), jax.ShapeDtypeStruct((P, S, 2), F32)],
        scratch_shapes=[pltpu.VMEM((2, 2, tq, tq), F32)],
        compiler_params=_params("parallel", "arbitrary"),
    )(*qa, *ka, *va, qg)


def _attn_out_bwd(dog, o, qg, l, hd, name):
    S, D = o.shape
    P = D // LANES
    ts = _tile(S, 512, 2 * SUBLANES)

    def body(dog_ref, o_ref, g_ref, l_ref, do_ref, dg_ref, e_ref):
        lo = _head_masks((ts, LANES), hd)
        lane2 = lax.broadcasted_iota(jnp.int32, (ts, 2), 1)
        for t in range(P):
            cols = slice(t * LANES, (t + 1) * LANES)
            sg = jax.nn.sigmoid(g_ref[0, :, cols])
            dog_t, o_t, l_t = dog_ref[:, cols], o_ref[:, cols], l_ref[t]
            g = (dog_t * sg / jnp.where(lo, l_t[:, 0:1], l_t[:, 1:2])).astype(BF)
            do_ref[:, cols] = g
            dg_ref[:, cols] = (dog_t * o_t * sg * (1.0 - sg)).astype(BF)
            prod = g.astype(F32) * o_t
            e0 = jnp.sum(jnp.where(lo, prod, 0.0), axis=-1, keepdims=True)
            e1 = jnp.sum(jnp.where(lo, 0.0, prod), axis=-1, keepdims=True)
            e_ref[t] = jnp.where(lane2 == 0, e0, e1)

    rows = pl.BlockSpec((ts, D), lambda s: (s, 0))
    stat = pl.BlockSpec((P, ts, 2), lambda s: (0, s, 0))
    return pl.pallas_call(
        body, name=name, grid=(S // ts,),
        in_specs=[rows, rows, pl.BlockSpec((1, ts, D), lambda s: (1, s, 0)), stat],
        out_specs=[rows, rows, stat],
        out_shape=[jax.ShapeDtypeStruct((S, D), BF), jax.ShapeDtypeStruct((S, D), BF),
                   jax.ShapeDtypeStruct((P, S, 2), F32)],
        compiler_params=_params("parallel"),
    )(dog, o, qg, l)


def _attn_bwd(qa, ka, vb, g, m_row, e_row, hd, name):
    S, D = vb.shape
    P = D // LANES
    tk = _attn_tile(S)
    nk = S // tk
    scale = hd ** -0.5

    def body(q0_ref, q1_ref, g_ref, k0_ref, k1_ref, v_ref, m_ref, e_ref, dq_ref, dk_ref, dv_ref, dc_ref,
             st_buf, dp_buf):
        ki = pl.program_id(1)

        @pl.when(ki == 0)
        def _():
            dq_ref[...] = jnp.zeros_like(dq_ref)

        lo = _head_masks((tk, LANES), hd)
        kh = (k0_ref[...], k1_ref[...])
        q_refs = (q0_ref, q1_ref)
        vh = _split_heads(v_ref[...], lo)
        causal_t = lax.broadcasted_iota(jnp.int32, (tk, tk), 0) <= lax.broadcasted_iota(jnp.int32, (tk, tk), 1)

        def stage(qi, slot):
            off = pl.multiple_of(qi * tk, tk)
            gb = g_ref[pl.ds(off, tk), :]
            for h in range(2):
                st_buf[slot, h] = lax.dot_general(kh[h], q_refs[h][pl.ds(off, tk), :], NT_DIMS,
                                                  preferred_element_type=F32)
                dp_buf[slot, h] = lax.dot_general(vh[h], gb, NT_DIMS, preferred_element_type=F32)

        def consume(qi, slot, carry, masked):
            off = pl.multiple_of(qi * tk, tk)
            gb = g_ref[pl.ds(off, tk), :]
            m_t, e_t = m_ref[0, qi], e_ref[0, qi]
            out, dq_parts = [], []
            for h in range(2):
                dk, dv, dc = carry[h]
                qb = q_refs[h][pl.ds(off, tk), :]
                pt = jnp.exp2(st_buf[slot, h] - m_t[h:h + 1, :])
                if masked:
                    pt = jnp.where(causal_t, pt, 0.0)
                pb = pt.astype(BF)
                dv = dv + jnp.dot(pb, gb, preferred_element_type=F32)
                dst = pb.astype(F32) * (dp_buf[slot, h] - e_t[h:h + 1, :])
                db = dst.astype(BF)
                dk = dk + jnp.dot(db, qb, preferred_element_type=F32)
                dc = dc - jnp.sum(dst, axis=-1, keepdims=True)
                dq_parts.append(lax.dot_general(db, kh[h], TN_DIMS, preferred_element_type=F32))
                out.append((dk, dv, dc))
            dq_ref[pl.ds(off, tk), :] += jnp.where(lo, dq_parts[0], dq_parts[1]) * scale
            return tuple(out)

        n_after = nk - 1 - ki

        def step(j, carry):
            b = ki + 1 + 2 * j
            stage(b + 1, 0)
            carry = consume(b, 1, carry, False)
            stage(b + 2, 1)
            return consume(b + 1, 0, carry, False)

        def rest_one(carry):
            return consume(nk - 1, 1, carry, False)

        def rest_two(carry):
            stage(nk - 1, 0)
            return consume(nk - 1, 0, consume(nk - 2, 1, carry, False), False)

        init = tuple((jnp.zeros((tk, LANES), F32), jnp.zeros((tk, LANES), F32), jnp.zeros((tk, 1), F32))
                     for _ in range(2))
        stage(ki, 0)
        stage(jnp.minimum(ki + 1, nk - 1), 1)
        carry = consume(ki, 0, init, True)
        carry = lax.fori_loop(0, (n_after - 1) // 2, step, carry)
        which = jnp.where(n_after == 0, 0, 2 - n_after % 2)
        (dk0, dv0, dc0), (dk1, dv1, dc1) = lax.switch(which, [lambda c: c, rest_one, rest_two], carry)
        dk_ref[...] = jnp.where(lo, dk0, dk1) * (1.0 / LOG2E)
        dv_ref[...] = jnp.where(lo, dv0, dv1)
        lane2 = lax.broadcasted_iota(jnp.int32, (tk, 2), 1)
        dc_ref[0] = jnp.where(lane2 == 0, dc0, dc1)

    tile = pl.BlockSpec((tk, LANES), lambda p, i: (i, p))
    whole = pl.BlockSpec((S, LANES), lambda p, i: (0, p))
    row_spec = pl.BlockSpec((1, nk, 2, tk), lambda p, i: (p, 0, 0, 0))
    return pl.pallas_call(
        body, name=name, grid=(P, nk),
        in_specs=[whole, whole, whole, tile, tile, tile, row_spec, row_spec],
        out_specs=[whole, tile, tile, pl.BlockSpec((1, tk, 2), lambda p, i: (p, i, 0))],
        out_shape=[jax.ShapeDtypeStruct((S, D), F32), jax.ShapeDtypeStruct((S, D), F32),
                   jax.ShapeDtypeStruct((S, D), F32), jax.ShapeDtypeStruct((P, S, 2), F32)],
        scratch_shapes=[pltpu.VMEM((2, 2, tk, tk), F32), pltpu.VMEM((2, 2, tk, tk), F32)],
        compiler_params=_params("parallel", "arbitrary"),
    )(*qa, g, *ka, vb, m_row, e_row)


def _loss_head(y, t, name):
    S, D = y.shape
    ts = _tile(S, 512, SUBLANES)

    def body(y_ref, t_ref, dy_ref, l_ref):
        @pl.when(pl.program_id(0) == 0)
        def _():
            l_ref[...] = jnp.zeros_like(l_ref)

        e = y_ref[...] - t_ref[...]
        dy_ref[...] = e * (1.0 / D)
        part = 0.5 * jnp.sum(jnp.mean(e * e, axis=-1, keepdims=True), axis=0, keepdims=True)
        l_ref[...] += jnp.broadcast_to(part, l_ref.shape)

    return pl.pallas_call(
        body, name=name, grid=(S // ts,),
        in_specs=[pl.BlockSpec((ts, D), lambda s: (s, 0)), pl.BlockSpec((ts, D), lambda s: (s, 0))],
        out_specs=[pl.BlockSpec((ts, D), lambda s: (s, 0)), pl.BlockSpec((SUBLANES, LANES), lambda s: (0, 0))],
        out_shape=[jax.ShapeDtypeStruct((S, D), F32), jax.ShapeDtypeStruct((SUBLANES, LANES), F32)],
        compiler_params=_params("arbitrary"),
    )(y, t)


def _adamw(w, g, m, v, name):
    shape = w.shape
    cols = shape[-1]
    as2d = lambda a: a.reshape(-1, cols)
    rows = as2d(w).shape[0]
    tr = _tile(rows, 256, SUBLANES) if rows % SUBLANES == 0 else rows
    c1 = 1.0 - ADAM_B1 ** ADAM_STEP
    c2 = 1.0 - ADAM_B2 ** ADAM_STEP

    def body(w_ref, g_ref, m_ref, v_ref, d_ref, nm_ref, nv_ref):
        gg = g_ref[...]
        nm = ADAM_B1 * m_ref[...] + (1.0 - ADAM_B1) * gg
        nv = ADAM_B2 * v_ref[...] + (1.0 - ADAM_B2) * (gg * gg)
        d_ref[...] = -ADAM_LR * ((nm / c1) / (jnp.sqrt(nv / c2) + ADAM_EPS) + ADAM_WD * w_ref[...])
        nm_ref[...] = nm
        nv_ref[...] = nv

    spec = pl.BlockSpec((tr, cols), lambda r: (r, 0))
    outs = pl.pallas_call(
        body, name=name, grid=(rows // tr,), in_specs=[spec] * 4, out_specs=[spec] * 3,
        out_shape=[jax.ShapeDtypeStruct((rows, cols), F32)] * 3,
        compiler_params=_params("parallel"),
    )(as2d(w), as2d(g), as2d(m), as2d(v))
    return tuple(o.reshape(shape) for o in outs)


def _place():
    return lax.axis_index("x"), lax.axis_index("y"), lax.axis_index("c")


def _other_chips(x, y):
    return [(1 - x, y), (x, 1 - y), (1 - x, 1 - y)]


def _remote(src, dst, send_sems, recv_sems, k, to):
    return pltpu.make_async_remote_copy(src_ref=src, dst_ref=dst, send_sem=send_sems.at[k], recv_sem=recv_sems.at[k],
                                        device_id=to, device_id_type=MESH)


def _half(c, rh):
    return pl.ds(pl.multiple_of(c * rh, 2 * SUBLANES), rh)


def _allgather_weights(srcs):
    n = len(srcs)

    def body(*refs):
        src, dst, (send_sems, recv_sems) = refs[:n], refs[n:2 * n], refs[2 * n:]
        x, y, c = _place()
        me = 2 * x + y
        sib = (x, y, 1 - c)
        chips = _other_chips(x, y)
        rh = [a.shape[0] // 2 for a in srcs]
        first = [_remote(src[g].at[_half(c, rh[g])], dst[g].at[me, _half(c, rh[g])], send_sems, recv_sems,
                         6 * g + j, (cx, cy, c)) for j, (cx, cy) in enumerate(chips) for g in range(n)]
        for cp in first:
            cp.start()
        passed = []
        for j, (cx, cy) in enumerate(chips):
            for g in range(n):
                landed = dst[g].at[2 * cx + cy, _half(c, rh[g])]
                _remote(landed, landed, send_sems, recv_sems, 6 * g + j, sib).wait_recv()
                cp = _remote(landed, landed, send_sems, recv_sems, 6 * g + 3 + j, sib)
                cp.start()
                passed.append(cp)
        for j, (cx, cy) in enumerate(chips):
            for g in range(n):
                landed = dst[g].at[2 * cx + cy, _half(1 - c, rh[g])]
                _remote(landed, landed, send_sems, recv_sems, 6 * g + 3 + j, sib).wait_recv()
        for cp in first + passed:
            cp.wait_send()

    outs = pl.pallas_call(
        body, name="allgather_weights", in_specs=[ANY] * n, out_specs=[ANY] * n,
        out_shape=[jax.ShapeDtypeStruct((N_CHIPS,) + a.shape, a.dtype) for a in srcs],
        scratch_shapes=[pltpu.SemaphoreType.DMA((6 * n,)), pltpu.SemaphoreType.DMA((6 * n,))],
    )(*srcs)
    x, y, _ = _place()
    return [lax.dynamic_update_slice_in_dim(o, a[None], 2 * x + y, axis=0) for o, a in zip(outs, srcs)]


def _pair_exchange(gs):
    n = len(gs)

    def body(*refs):
        g_refs, t_refs, (send_sems, recv_sems) = refs[:n], refs[n:2 * n], refs[2 * n:]
        x, y, c = _place()
        cps = [_remote(g_refs[g].at[k, 1 - c], t_refs[g].at[k], send_sems, recv_sems, N_CHIPS * g + k, (x, y, 1 - c))
               for g in range(n) for k in range(N_CHIPS)]
        for cp in cps:
            cp.start()
        for cp in cps:
            cp.wait()

    return pl.pallas_call(
        body, name="grad_pair_exchange", in_specs=[ANY] * n, out_specs=[ANY] * n,
        out_shape=[jax.ShapeDtypeStruct((a.shape[0],) + a.shape[2:], a.dtype) for a in gs],
        scratch_shapes=[pltpu.SemaphoreType.DMA((N_CHIPS * n,)), pltpu.SemaphoreType.DMA((N_CHIPS * n,))],
    )(*gs)


def _pair_add(g, t, c, name):
    n, _, rh, W = g.shape
    tr = _tile(rh, 256, 2 * SUBLANES)

    def body(c_ref, g_ref, t_ref, o_ref):
        o_ref[...] = (g_ref[0] + t_ref[...]).astype(BF)

    return pl.pallas_call(
        body, name=name,
        grid_spec=pltpu.PrefetchScalarGridSpec(
            num_scalar_prefetch=1, grid=(n, rh // tr),
            in_specs=[pl.BlockSpec((1, 1, tr, W), lambda k, i, c_ref: (k, c_ref[0], i, 0)),
                      pl.BlockSpec((1, tr, W), lambda k, i, c_ref: (k, i, 0))],
            out_specs=pl.BlockSpec((1, tr, W), lambda k, i, c_ref: (k, i, 0))),
        out_shape=jax.ShapeDtypeStruct((n, rh, W), BF),
        compiler_params=_params("parallel", "parallel"),
    )(c.reshape(1).astype(jnp.int32), g, t)


def _chip_exchange(parts):
    n = len(parts)

    def body(*refs):
        a_refs, t_refs, (send_sems, recv_sems) = refs[:n], refs[n:2 * n], refs[2 * n:]
        x, y, c = _place()
        cps = [_remote(a_refs[g].at[2 * cx + cy], t_refs[g].at[j], send_sems, recv_sems, 3 * g + j, (cx, cy, c))
               for j, (cx, cy) in enumerate(_other_chips(x, y)) for g in range(n)]
        for cp in cps:
            cp.start()
        for cp in cps:
            cp.wait()

    return pl.pallas_call(
        body, name="grad_chip_exchange", in_specs=[ANY] * n, out_specs=[ANY] * n,
        out_shape=[jax.ShapeDtypeStruct((3,) + a.shape[1:], a.dtype) for a in parts],
        scratch_shapes=[pltpu.SemaphoreType.DMA((3 * n,)), pltpu.SemaphoreType.DMA((3 * n,))],
    )(*parts)


def _chip_add(g, t1, t2, c, me, name):
    _, _, rh, W = g.shape
    tr = _tile(rh, 256, 2 * SUBLANES)

    def body(c_ref, me_ref, g_ref, t1_ref, t2_ref, o_ref):
        own = g_ref[0, 0] + t1_ref[0]
        o_ref[...] = own + t2_ref[0].astype(F32) + t2_ref[1].astype(F32) + t2_ref[2].astype(F32)

    return pl.pallas_call(
        body, name=name,
        grid_spec=pltpu.PrefetchScalarGridSpec(
            num_scalar_prefetch=2, grid=(rh // tr,),
            in_specs=[pl.BlockSpec((1, 1, tr, W), lambda i, c_ref, me_ref: (me_ref[0], c_ref[0], i, 0)),
                      pl.BlockSpec((1, tr, W), lambda i, c_ref, me_ref: (me_ref[0], i, 0)),
                      pl.BlockSpec((3, tr, W), lambda i, c_ref, me_ref: (0, i, 0))],
            out_specs=pl.BlockSpec((tr, W), lambda i, c_ref, me_ref: (i, 0))),
        out_shape=jax.ShapeDtypeStruct((rh, W), F32),
        compiler_params=_params("parallel"),
    )(c.reshape(1).astype(jnp.int32), me.reshape(1).astype(jnp.int32), g, t1, t2)


def _pair_share(hs):
    n = len(hs)

    def body(*refs):
        h_refs, f_refs, (send_sems, recv_sems) = refs[:n], refs[n:2 * n], refs[2 * n:]
        x, y, c = _place()
        cps = [_remote(h_refs[g], f_refs[g], send_sems, recv_sems, g, (x, y, 1 - c)) for g in range(n)]
        for cp in cps:
            cp.start()
        for cp in cps:
            cp.wait()

    return pl.pallas_call(
        body, name="grad_pair_share", in_specs=[ANY] * n, out_specs=[ANY] * n,
        out_shape=[jax.ShapeDtypeStruct(a.shape, a.dtype) for a in hs],
        scratch_shapes=[pltpu.SemaphoreType.DMA((n,)), pltpu.SemaphoreType.DMA((n,))],
    )(*hs)


def _allreduce_small(pack, name):
    rows, W = pack.shape

    def body(p_ref, o_ref, buf, send_sems, recv_sems):
        x, y, c = _place()
        me = 4 * x + 2 * y + c
        buf[me] = p_ref[...]
        cps = []
        for r in range(1, 8):
            fx, fy, fc = (r >> 2) & 1, (r >> 1) & 1, r & 1
            to = (1 - x if fx else x, 1 - y if fy else y, 1 - c if fc else c)
            cps.append(_remote(p_ref, buf.at[me], send_sems, recv_sems, r - 1, to))
        for cp in cps:
            cp.start()
        for r in range(1, 8):
            fx, fy, fc = (r >> 2) & 1, (r >> 1) & 1, r & 1
            frm = 4 * (1 - x if fx else x) + 2 * (1 - y if fy else y) + (1 - c if fc else c)
            _remote(p_ref, buf.at[frm], send_sems, recv_sems, r - 1, (x, y, c)).wait_recv()
        for cp in cps:
            cp.wait_send()
        acc = buf[0]
        for i in range(1, 8):
            acc = acc + buf[i]
        o_ref[...] = acc

    return pl.pallas_call(
        body, name=name, in_specs=[VMEM], out_specs=VMEM,
        out_shape=jax.ShapeDtypeStruct((rows, W), F32),
        scratch_shapes=[pltpu.VMEM((8, rows, W), F32), pltpu.SemaphoreType.DMA((7,)), pltpu.SemaphoreType.DMA((7,))],
    )(pack)


def _width_groups(arrs):
    widths = []
    for a in arrs:
        if a.shape[-1] not in widths:
            widths.append(a.shape[-1])
    return [[i for i, a in enumerate(arrs) if a.shape[-1] == w] for w in widths]


def _rows2d(a):
    return a.reshape(-1, a.shape[-1])


def _split_rows_like(buf, like, lead=()):
    out, off = [], 0
    for a in like:
        n = a.size // a.shape[-1]
        out.append(buf[..., off:off + n, :].reshape(tuple(lead) + a.shape))
        off += n
    return out


def _join_cols(g):
    nd = g.ndim
    return jnp.moveaxis(g, 0, nd - 2).reshape(g.shape[1:-1] + (N_CHIPS * g.shape[-1],))


def _join_rows(g):
    return jnp.moveaxis(g, 0, 1).reshape(g.shape[1], N_CHIPS * g.shape[2], g.shape[3])


def _row_layout(a, tq):
    P, S, _ = a.shape
    return a.reshape(P, S // tq, tq, 2).transpose(0, 1, 3, 2)


def _pad_row(v, width=FLAT_W):
    flat = v.reshape(-1)
    rows = -(-flat.shape[0] // width)
    return jnp.pad(flat, (0, rows * width - flat.shape[0]))


def kernel(x, attn_norm, ffn_norm, a_w_in, a_conv, a_w_out, kv_norm, w_kvf, b_f, k_norm, b_w_qg, q_norm, b_w_out, ffn_w_up, ffn_conv, ffn_w_down, loss_target, m_attn_norm, m_ffn_norm, m_a_w_in, m_a_conv, m_a_w_out, m_kv_norm, m_w_kvf, m_b_f, m_k_norm, m_b_w_qg, m_q_norm, m_b_w_out, m_ffn_w_up, m_ffn_conv, m_ffn_w_down, v_attn_norm, v_ffn_norm, v_a_w_in, v_a_conv, v_a_w_out, v_kv_norm, v_w_kvf, v_b_f, v_k_norm, v_b_w_qg, v_q_norm, v_b_w_out, v_ffn_w_up, v_ffn_conv, v_ffn_w_down):
    xs = x[0]
    S, D = xs.shape
    H, hd = b_f.shape[0], k_norm.shape[0]
    depth = attn_norm.shape[0]
    n_a = a_w_in.shape[0]
    P = D // LANES
    assert LANES == 2 * hd and H * hd == D, "the attention kernels hold two heads per lane tile"
    mx, my, mc = _place()
    chip = 2 * mx + my

    big = [a_w_in, a_w_out, w_kvf, b_w_qg, b_w_out, ffn_w_up, ffn_w_down]
    groups = _width_groups(big)
    gathered = _allgather_weights([jnp.concatenate([_rows2d(big[i]).astype(BF) for i in idx]) for idx in groups])
    by_chip = [None] * len(big)
    for idx, buf in zip(groups, gathered):
        for i, part in zip(idx, _split_rows_like(buf, [big[i] for i in idx], (N_CHIPS,))):
            by_chip[i] = part
    g_in, g_out, g_kvf, g_qg, g_bout, g_up, g_down = by_chip
    wa_in, wb_qg, w_up = _join_cols(g_in), _join_cols(g_qg), _join_cols(g_up)
    wa_out, wb_out, w_down = _join_rows(g_out), _join_rows(g_bout), _join_rows(g_down)
    kvf_cols = 2 * D + LANES
    wkvf = jnp.pad(_join_cols(g_kvf), ((0, 0), (0, kvf_cols - (2 * D + H))))

    def placed(shard):
        full = jnp.zeros(shard.shape[:-1] + (N_CHIPS, shard.shape[-1]), F32)
        full = lax.dynamic_update_slice_in_dim(full, shard[..., None, :], chip, axis=full.ndim - 2)
        return jnp.where(mc == 0, full, 0.0).reshape(-1)

    conv_pack = jnp.concatenate([_pad_row(placed(a_conv)), _pad_row(placed(ffn_conv))]).reshape(-1, FLAT_W)
    conv_full = _allreduce_small(conv_pack, "allgather_conv_taps").reshape(-1)
    n_ac = a_conv.size * N_CHIPS
    a_conv_f = conv_full[:n_ac].reshape(a_conv.shape[:-1] + (-1,))
    off = _pad_row(placed(a_conv)).shape[0]
    ffn_conv_f = conv_full[off:off + ffn_conv.size * N_CHIPS].reshape(ffn_conv.shape[:-1] + (-1,))
    F = ffn_conv_f.shape[-1]

    b_pad = jnp.pad(b_f, (0, LANES - H)).reshape(1, LANES)
    gate_blk = 2 * D // LANES
    tq = _attn_tile(S)
    scale = hd ** -0.5

    saved = []
    cur = xs
    kv = None
    for l in range(depth):
        rec = {"x_in": cur}
        if l < n_a:
            proj, xn, z = _mixer_in_fwd(cur, attn_norm[l], wa_in[l], a_conv_f[l], f"a_in_{l}")
            mid = _matmul_residual(z, wa_out[l], cur, f"a_out_{l}")
            rec.update(proj=proj, xn=xn, z=z)
        else:
            j = l - n_a
            if kv is None:
                kvf, hn = _norm_matmul(cur, kv_norm, wkvf, 1, F32, "kvf_proj")
                kn = _headnorm(kvf, 0, 0, k_norm, 1.0, D, "k_norm")
                vb = kvf[0, :, D:2 * D].astype(BF)
                cgate = _gate_fwd(kvf, b_pad, gate_blk, "gate_cumsum")
                kv = dict(kvf=kvf, hn=hn, vb=vb, cgate=cgate, x_in=cur, dk=[], dv=[], dc=[],
                          ka=_augment(kn, cgate, "k", hd, "k_augment"), va=_augment(vb, cgate, "v", hd, "v_augment"))
            qg, xn = _norm_matmul(cur, attn_norm[l], wb_qg[j], 2, F32, f"qg_proj_{j}")
            qn = _headnorm(qg, 0, 0, q_norm[j], scale * LOG2E, D, f"q_norm_{j}")
            qa = _augment(qn, kv["cgate"], "q", hd, f"q_augment_{j}")
            o, og, m_max, l_sum = _attn_fwd(qa, kv["ka"], kv["va"], qg, hd, f"attn_fwd_{j}")
            mid = _matmul_residual(og, wb_out[j], cur, f"b_out_{j}")
            rec.update(qg=qg, xn=xn, qa=qa, o=o, og=og, m=m_max, l=l_sum)
        up, xn2, z2 = _ffn_up_fwd(mid, ffn_norm[l], w_up[l], ffn_conv_f[l], f"ffn_up_{l}")
        cur = _matmul_residual(z2, w_down[l], mid, f"ffn_down_{l}")
        rec.update(x_mid=mid, up=up, xn2=xn2, z2=z2)
        saved.append(rec)

    dy, loss_part = _loss_head(cur, loss_target[0], "loss_head")

    g_attn_norm, g_ffn_norm = [None] * depth, [None] * depth
    g_a_in, g_a_conv, g_a_out = [None] * n_a, [None] * n_a, [None] * n_a
    g_qg, g_qn, g_bo = [None] * (depth - n_a), [None] * (depth - n_a), [None] * (depth - n_a)
    g_up, g_fc, g_down = [None] * depth, [None] * depth, [None] * depth
    for l in reversed(range(depth)):
        rec = saved[l]
        dup, g_fc[l] = _ffn_mid_bwd(rec["up"], dy, w_down[l], ffn_conv_f[l], f"ffn_mid_bwd_{l}")
        g_down[l] = _wgrad(rec["z2"], dy[None], f"ffn_down_wgrad_{l}")
        g_up[l] = _wgrad(rec["xn2"], dup, f"ffn_up_wgrad_{l}")
        dy, g_ffn_norm[l] = _dnorm(dup, w_up[l], rec["x_mid"], ffn_norm[l], dy, f"ffn_up_bwd_{l}")
        if l < n_a:
            dproj, g_a_conv[l] = _mixer_mid_bwd(rec["proj"], dy, wa_out[l], a_conv_f[l], f"a_mid_bwd_{l}")
            g_a_out[l] = _wgrad(rec["z"], dy[None], f"a_out_wgrad_{l}")
            g_a_in[l] = _wgrad(rec["xn"], dproj, f"a_in_wgrad_{l}")
            dy, g_attn_norm[l] = _dnorm(dproj, wa_in[l], rec["x_in"], attn_norm[l], dy, f"a_in_bwd_{l}")
        else:
            j = l - n_a
            dog = _matmul_nt(dy, wb_out[j], f"b_out_bwd_{j}")
            g_out, dgate, evec = _attn_out_bwd(dog, rec["o"], rec["qg"], rec["l"], hd, f"attn_gate_bwd_{j}")
            g_bo[j] = _wgrad(rec["og"], dy[None], f"b_out_wgrad_{j}")
            dqn, dk, dv, dc = _attn_bwd(rec["qa"], kv["ka"], kv["vb"], g_out, _row_layout(rec["m"], tq),
                                        _row_layout(evec, tq), hd, f"attn_bwd_{j}")
            kv["dk"].append(dk)
            kv["dv"].append(dv)
            kv["dc"].append(dc)
            dq_pre, g_qn[j] = _headnorm_bwd(rec["qg"], 0, 0, q_norm[j], [dqn], D, f"q_norm_bwd_{j}")
            dqg = jnp.stack([dq_pre, dgate])
            g_qg[j] = _wgrad(rec["xn"], dqg, f"qg_wgrad_{j}")
            dy, g_attn_norm[l] = _dnorm(dqg, wb_qg[j], rec["x_in"], attn_norm[l], dy, f"qg_bwd_{j}")
            if l == n_a:
                dk_s, g_k_norm = _headnorm_bwd(kv["kvf"], 0, 0, k_norm, kv["dk"], D, "k_norm_bwd")
                dv_s = functools.reduce(jnp.add, kv["dv"]).astype(BF)
                dc_sum = functools.reduce(jnp.add, kv["dc"])
                dc_pad = jnp.pad(dc_sum.transpose(1, 0, 2).reshape(S, H), ((0, 0), (0, LANES - H)))
                df, db = _gate_bwd(dc_pad, kv["kvf"], b_pad, gate_blk, "gate_bwd")
                dkvf = jnp.concatenate([dk_s, dv_s, df.astype(BF)], axis=1)[None]
                g_kvf = _wgrad(kv["hn"], dkvf, "kvf_wgrad")[:, :2 * D + H]
                g_b_f = db[:H]
                dy, g_kv_norm = _dnorm(dkvf, wkvf, kv["x_in"], kv_norm, dy, "kvf_bwd")
    grad_x = dy[None]

    def cols_of(g, k):
        c = g.shape[-1] // N_CHIPS
        return g[:, k * c:(k + 1) * c]

    def rows_of(g, k):
        r = g.shape[0] // N_CHIPS
        return g[k * r:(k + 1) * r]

    per_layer = [g_a_in, g_a_out, [g_kvf], g_qg, g_bo, g_up, g_down]
    of_chip = [cols_of, rows_of, cols_of, cols_of, rows_of, cols_of, rows_of]

    def group_buffer(idx):
        rows = [of_chip[i](g, k) for k in range(N_CHIPS) for i in idx for g in per_layer[i]]
        buf = jnp.concatenate(rows)
        return buf.reshape(N_CHIPS, 2, buf.shape[0] // (2 * N_CHIPS), buf.shape[1])

    g4 = [group_buffer(idx) for idx in groups]
    from_sibling = _pair_exchange(g4)
    from_chips = _chip_exchange([_pair_add(g, t, mc, f"grad_pair_add_{n}") for n, (g, t) in
                                 enumerate(zip(g4, from_sibling))])
    mine = [_chip_add(g, t1, t2, mc, chip, f"grad_chip_add_{n}") for n, (g, t1, t2) in
            enumerate(zip(g4, from_sibling, from_chips))]
    theirs = _pair_share(mine)
    big_grads = [None] * len(big)
    for idx, m_half, t_half in zip(groups, mine, theirs):
        shard = jnp.where(mc == 0, jnp.concatenate([m_half, t_half]), jnp.concatenate([t_half, m_half]))
        for i, part in zip(idx, _split_rows_like(shard, [big[i] for i in idx])):
            big_grads[i] = part

    small = [loss_part[0, :1], jnp.stack(g_attn_norm), jnp.stack(g_ffn_norm), g_kv_norm, g_b_f, g_k_norm,
             jnp.stack(g_qn), jnp.stack(g_a_conv), jnp.stack(g_fc)]
    small_sum = _allreduce_small(jnp.concatenate([_pad_row(s) for s in small]).reshape(-1, FLAT_W),
                                 "allreduce_small_grads").reshape(-1)
    parts, off = [], 0
    for s in small:
        parts.append(small_sum[off:off + s.size].reshape(s.shape))
        off += _pad_row(s).shape[0]
    loss = parts[0][0]
    gr_attn_norm, gr_ffn_norm, gr_kv_norm, gr_b_f, gr_k_norm, gr_q_norm, gr_a_conv_full, gr_ffn_conv_full = parts[1:]

    def my_cols(full):
        c = full.shape[-1] // N_CHIPS
        return lax.dynamic_slice_in_dim(full, chip * c, c, axis=full.ndim - 1)

    gr_a_in, gr_a_out, gr_kvf, gr_qg, gr_bo, gr_up, gr_down = big_grads
    grads = [gr_attn_norm, gr_ffn_norm, gr_a_in, my_cols(gr_a_conv_full), gr_a_out, gr_kv_norm, gr_kvf, gr_b_f,
             gr_k_norm, gr_qg, gr_q_norm, gr_bo, gr_up, my_cols(gr_ffn_conv_full), gr_down]
    weights = [attn_norm, ffn_norm, a_w_in, a_conv, a_w_out, kv_norm, w_kvf, b_f, k_norm, b_w_qg, q_norm, b_w_out,
               ffn_w_up, ffn_conv, ffn_w_down]
    ms = [m_attn_norm, m_ffn_norm, m_a_w_in, m_a_conv, m_a_w_out, m_kv_norm, m_w_kvf, m_b_f, m_k_norm, m_b_w_qg,
          m_q_norm, m_b_w_out, m_ffn_w_up, m_ffn_conv, m_ffn_w_down]
    vs = [v_attn_norm, v_ffn_norm, v_a_w_in, v_a_conv, v_a_w_out, v_kv_norm, v_w_kvf, v_b_f, v_k_norm, v_b_w_qg,
          v_q_norm, v_b_w_out, v_ffn_w_up, v_ffn_conv, v_ffn_w_down]
    deltas, new_ms, new_vs = [], [], []
    for i, (w, g, m, v) in enumerate(zip(weights, grads, ms, vs)):
        d, nm, nv = _adamw(w, g, m, v, f"adamw_{i}")
        deltas.append(d)
        new_ms.append(nm)
        new_vs.append(nv)
    return (loss, grad_x, *grads, *deltas, *new_ms, *new_vs)
```

```python
import functools

import jax
import jax.numpy as jnp
from jax import lax
from jax.experimental import pallas as pl
from jax.experimental.pallas import tpu as pltpu

F32 = jnp.float32
BF = jnp.bfloat16
LANES = 128
SUBLANES = 8
RMS_EPS = 1e-6
LOG2E = 1.4426950408889634
FLAT_W = 1024
N_CHIPS = 4
CONV_W = 3
HALO = SUBLANES

ADAM_LR = 0.001
ADAM_B1 = 0.9
ADAM_B2 = 0.999
ADAM_EPS = 1e-08
ADAM_WD = 0.01
ADAM_STEP = 10

MESH = pl.DeviceIdType.MESH
ANY = pl.BlockSpec(memory_space=pl.ANY)
VMEM = pl.BlockSpec(memory_space=pltpu.VMEM)
NT_DIMS = (((1,), (1,)), ((), ()))
TN_DIMS = (((0,), (0,)), ((), ()))


def _tile(n, pref, mult=LANES):
    t = (min(pref, n) // mult) * mult
    while t >= mult:
        if n % t == 0:
            break
        t -= mult
    if t < mult or (t * 4 < pref and n <= 4 * pref):
        return n
    return t


def _params(*sem):
    return pltpu.CompilerParams(dimension_semantics=sem)


def _norm_matmul(x, g, w, parts, out_dtype, name):
    S, D = x.shape
    C = w.shape[1] // parts
    ts, tn = _tile(S, 512, SUBLANES), _tile(C, 1408)
    npc = C // tn

    def body(x_ref, g_ref, w_ref, o_ref, xn_ref):
        @pl.when(pl.program_id(1) == 0)
        def _():
            xf = x_ref[...]
            r = lax.rsqrt(jnp.mean(xf * xf, axis=-1, keepdims=True) + RMS_EPS)
            xn_ref[...] = (xf * r * g_ref[...]).astype(BF)

        o_ref[0] = jnp.dot(xn_ref[...], w_ref[...], preferred_element_type=F32).astype(out_dtype)

    return pl.pallas_call(
        body, name=name, grid=(S // ts, parts * npc),
        in_specs=[pl.BlockSpec((ts, D), lambda s, n: (s, 0)),
                  pl.BlockSpec((1, D), lambda s, n: (0, 0)),
                  pl.BlockSpec((D, tn), lambda s, n: (0, n))],
        out_specs=[pl.BlockSpec((1, ts, tn), lambda s, n: (n // npc, s, n % npc)),
                   pl.BlockSpec((ts, D), lambda s, n: (s, 0))],
        out_shape=[jax.ShapeDtypeStruct((parts, S, C), out_dtype), jax.ShapeDtypeStruct((S, D), BF)],
        compiler_params=_params("parallel", "arbitrary"),
    )(x, g.reshape(1, D), w)


def _shift_down(u, prev, k):
    r = pltpu.roll(u, k, 0)
    row = lax.broadcasted_iota(jnp.int32, (HALO, u.shape[1]), 0)
    head = r[0:HALO]
    for j in range(k):
        head = jnp.where(row == j, prev[HALO - k + j:HALO - k + j + 1, :], head)
    return jnp.concatenate([head, r[HALO:]], axis=0)


def _shift_up(d, nxt, k):
    n = d.shape[0]
    r = pltpu.roll(d, n - k, 0)
    row = lax.broadcasted_iota(jnp.int32, (HALO, d.shape[1]), 0)
    tail = r[n - HALO:n]
    for j in range(k):
        tail = jnp.where(row == HALO - k + j, nxt[j:j + 1, :], tail)
    return jnp.concatenate([r[0:n - HALO], tail], axis=0)


def _conv3(u, prev, w):
    return _shift_down(u, prev, 2) * w[0:1] + _shift_down(u, prev, 1) * w[1:2] + u * w[2:3]


def _conv3_t(d, nxt, w):
    return d * w[2:3] + _shift_up(d, nxt, 1) * w[1:2] + _shift_up(d, nxt, 2) * w[0:1]


def _tap_rows(t0, t1, t2):
    row = lax.broadcasted_iota(jnp.int32, (SUBLANES, t0.shape[1]), 0)
    return jnp.where(row == 0, t0, jnp.where(row == 1, t1, jnp.where(row == 2, t2, 0.0)))


def _pad_conv(cw):
    return jnp.pad(cw, ((0, SUBLANES - CONV_W), (0, 0)))


def _mixer_in_fwd(x, g, w, cw, name):
    S, D = x.shape
    C = w.shape[1] // 3
    ts, tc = _tile(S, 512, SUBLANES), _tile(C, 1024)
    nc = C // tc

    def body(x_ref, g_ref, wb_ref, wc_ref, wh_ref, cw_ref, p_ref, xn_ref, z_ref, carry):
        s, c = pl.program_id(0), pl.program_id(1)

        @pl.when(c == 0)
        def _():
            xf = x_ref[...]
            r = lax.rsqrt(jnp.mean(xf * xf, axis=-1, keepdims=True) + RMS_EPS)
            xn_ref[...] = (xf * r * g_ref[...]).astype(BF)

        @pl.when(s == 0)
        def _():
            carry[c] = jnp.zeros((HALO, tc), F32)

        xn = xn_ref[...]
        parts = [jnp.dot(xn, w_ref[...], preferred_element_type=F32).astype(BF) for w_ref in (wb_ref, wc_ref, wh_ref)]
        for p, v in enumerate(parts):
            p_ref[p] = v
        u = parts[1].astype(F32) * parts[2].astype(F32)
        cv = _conv3(u, carry[c], cw_ref[...])
        z_ref[...] = (parts[0].astype(F32) * cv).astype(BF)
        carry[c] = u[ts - HALO:ts, :]

    wspec = lambda p: pl.BlockSpec((D, tc), lambda s, c: (0, p * nc + c))
    return pl.pallas_call(
        body, name=name, grid=(S // ts, nc),
        in_specs=[pl.BlockSpec((ts, D), lambda s, c: (s, 0)), pl.BlockSpec((1, D), lambda s, c: (0, 0)),
                  wspec(0), wspec(1), wspec(2), pl.BlockSpec((SUBLANES, tc), lambda s, c: (0, c))],
        out_specs=[pl.BlockSpec((3, ts, tc), lambda s, c: (0, s, c)),
                   pl.BlockSpec((ts, D), lambda s, c: (s, 0)),
                   pl.BlockSpec((ts, tc), lambda s, c: (s, c))],
        out_shape=[jax.ShapeDtypeStruct((3, S, C), BF), jax.ShapeDtypeStruct((S, D), BF),
                   jax.ShapeDtypeStruct((S, C), BF)],
        scratch_shapes=[pltpu.VMEM((nc, HALO, tc), F32)],
        compiler_params=_params("arbitrary", "arbitrary"),
    )(x, g.reshape(1, D), w, w, w, _pad_conv(cw))


def _ffn_up_fwd(x, g, w, cw, name):
    S, D = x.shape
    C = w.shape[1] // 2
    ts, tc = _tile(S, 512, SUBLANES), _tile(C, 1408)
    nc = C // tc

    def body(x_ref, g_ref, wa_ref, wg_ref, cw_ref, up_ref, xn_ref, z_ref, carry):
        s, c = pl.program_id(0), pl.program_id(1)

        @pl.when(c == 0)
        def _():
            xf = x_ref[...]
            r = lax.rsqrt(jnp.mean(xf * xf, axis=-1, keepdims=True) + RMS_EPS)
            xn_ref[...] = (xf * r * g_ref[...]).astype(BF)

        @pl.when(s == 0)
        def _():
            carry[c] = jnp.zeros((HALO, tc), F32)

        xn = xn_ref[...]
        a_b = jnp.dot(xn, wa_ref[...], preferred_element_type=F32).astype(BF)
        g_b = jnp.dot(xn, wg_ref[...], preferred_element_type=F32).astype(BF)
        up_ref[0] = a_b
        up_ref[1] = g_b
        a_pre = a_b.astype(F32)
        a = _conv3(a_pre, carry[c], cw_ref[...])
        z_ref[...] = (a * jax.nn.sigmoid(a) * g_b.astype(F32)).astype(BF)
        carry[c] = a_pre[ts - HALO:ts, :]

    return pl.pallas_call(
        body, name=name, grid=(S // ts, nc),
        in_specs=[pl.BlockSpec((ts, D), lambda s, c: (s, 0)),
                  pl.BlockSpec((1, D), lambda s, c: (0, 0)),
                  pl.BlockSpec((D, tc), lambda s, c: (0, c)),
                  pl.BlockSpec((D, tc), lambda s, c: (0, nc + c)),
                  pl.BlockSpec((SUBLANES, tc), lambda s, c: (0, c))],
        out_specs=[pl.BlockSpec((2, ts, tc), lambda s, c: (0, s, c)),
                   pl.BlockSpec((ts, D), lambda s, c: (s, 0)),
                   pl.BlockSpec((ts, tc), lambda s, c: (s, c))],
        out_shape=[jax.ShapeDtypeStruct((2, S, C), BF), jax.ShapeDtypeStruct((S, D), BF),
                   jax.ShapeDtypeStruct((S, C), BF)],
        scratch_shapes=[pltpu.VMEM((nc, HALO, tc), F32)],
        compiler_params=_params("arbitrary", "arbitrary"),
    )(x, g.reshape(1, D), w, w, _pad_conv(cw))


def _mixer_mid_bwd(proj, dy, w_out, cw, name):
    _, S, C = proj.shape
    D = dy.shape[1]
    ts, tc = _tile(S, 512, SUBLANES), _tile(C, 1024)
    n_s = S // ts
    per = ts // HALO

    def body(b_ref, c_ref, h_ref, dy_ref, w_ref, cp_ref, hp_ref, cw_ref, d_ref, dcw_ref, carry):
        i = pl.program_id(1)
        w = cw_ref[...]
        dz = lax.dot_general(dy_ref[...].astype(BF), w_ref[...], NT_DIMS, preferred_element_type=F32)
        b, c, h = b_ref[0].astype(F32), c_ref[0].astype(F32), h_ref[0].astype(F32)
        u = c * h
        prev = jnp.where(i < n_s - 1, cp_ref[0].astype(F32) * hp_ref[0].astype(F32), 0.0)
        u1, u2 = _shift_down(u, prev, 1), _shift_down(u, prev, 2)
        cv = u2 * w[0:1] + u1 * w[1:2] + u * w[2:3]
        dcv = dz * b
        nxt = jnp.where(i > 0, carry[...], 0.0)
        du = _conv3_t(dcv, nxt, w)
        d_ref[0] = (dz * cv).astype(BF)
        d_ref[1] = (du * h).astype(BF)
        d_ref[2] = (du * c).astype(BF)
        carry[...] = dcv[0:HALO, :]
        part = _tap_rows(jnp.sum(dcv * u2, axis=0, keepdims=True), jnp.sum(dcv * u1, axis=0, keepdims=True),
                         jnp.sum(dcv * u, axis=0, keepdims=True))

        @pl.when(i == 0)
        def _():
            dcw_ref[...] = part

        @pl.when(i > 0)
        def _():
            dcw_ref[...] += part

    tile = lambda p: pl.BlockSpec((1, ts, tc), lambda c, i: (p, n_s - 1 - i, c))
    before = lambda p: pl.BlockSpec((1, HALO, tc), lambda c, i: (p, jnp.maximum((n_s - 1 - i) * per - 1, 0), c))
    dproj, dcw = pl.pallas_call(
        body, name=name, grid=(C // tc, n_s),
        in_specs=[tile(0), tile(1), tile(2),
                  pl.BlockSpec((ts, D), lambda c, i: (n_s - 1 - i, 0)),
                  pl.BlockSpec((tc, D), lambda c, i: (c, 0)),
                  before(1), before(2),
                  pl.BlockSpec((SUBLANES, tc), lambda c, i: (0, c))],
        out_specs=[pl.BlockSpec((3, ts, tc), lambda c, i: (0, n_s - 1 - i, c)),
                   pl.BlockSpec((SUBLANES, tc), lambda c, i: (0, c))],
        out_shape=[jax.ShapeDtypeStruct((3, S, C), BF), jax.ShapeDtypeStruct((SUBLANES, C), F32)],
        scratch_shapes=[pltpu.VMEM((HALO, tc), F32)],
        compiler_params=_params("parallel", "arbitrary"),
    )(proj, proj, proj, dy, w_out, proj, proj, _pad_conv(cw))
    return dproj, dcw[:CONV_W]


def _ffn_mid_bwd(up, dy, w_down, cw, name):
    _, S, C = up.shape
    D = dy.shape[1]
    ts, tc = _tile(S, 512, SUBLANES), _tile(C, 1408)
    n_s = S // ts
    per = ts // HALO

    def body(a_ref, g_ref, dy_ref, w_ref, ap_ref, cw_ref, d_ref, dcw_ref, carry):
        i = pl.program_id(1)
        w = cw_ref[...]
        dz = lax.dot_general(dy_ref[...].astype(BF), w_ref[...], NT_DIMS, preferred_element_type=F32)
        a_pre, g = a_ref[0].astype(F32), g_ref[0].astype(F32)
        prev = jnp.where(i < n_s - 1, ap_ref[0].astype(F32), 0.0)
        a1, a2 = _shift_down(a_pre, prev, 1), _shift_down(a_pre, prev, 2)
        a = a2 * w[0:1] + a1 * w[1:2] + a_pre * w[2:3]
        sg = jax.nn.sigmoid(a)
        da = dz * g * (sg * (1.0 + a * (1.0 - sg)))
        nxt = jnp.where(i > 0, carry[...], 0.0)
        d_ref[0] = _conv3_t(da, nxt, w).astype(BF)
        d_ref[1] = (dz * (a * sg)).astype(BF)
        carry[...] = da[0:HALO, :]
        part = _tap_rows(jnp.sum(da * a2, axis=0, keepdims=True), jnp.sum(da * a1, axis=0, keepdims=True),
                         jnp.sum(da * a_pre, axis=0, keepdims=True))

        @pl.when(i == 0)
        def _():
            dcw_ref[...] = part

        @pl.when(i > 0)
        def _():
            dcw_ref[...] += part

    tile = lambda p: pl.BlockSpec((1, ts, tc), lambda c, i: (p, n_s - 1 - i, c))
    dup, dcw = pl.pallas_call(
        body, name=name, grid=(C // tc, n_s),
        in_specs=[tile(0), tile(1),
                  pl.BlockSpec((ts, D), lambda c, i: (n_s - 1 - i, 0)),
                  pl.BlockSpec((tc, D), lambda c, i: (c, 0)),
                  pl.BlockSpec((1, HALO, tc), lambda c, i: (0, jnp.maximum((n_s - 1 - i) * per - 1, 0), c)),
                  pl.BlockSpec((SUBLANES, tc), lambda c, i: (0, c))],
        out_specs=[pl.BlockSpec((2, ts, tc), lambda c, i: (0, n_s - 1 - i, c)),
                   pl.BlockSpec((SUBLANES, tc), lambda c, i: (0, c))],
        out_shape=[jax.ShapeDtypeStruct((2, S, C), BF), jax.ShapeDtypeStruct((SUBLANES, C), F32)],
        scratch_shapes=[pltpu.VMEM((HALO, tc), F32)],
        compiler_params=_params("parallel", "arbitrary"),
    )(up, up, dy, w_down, up, _pad_conv(cw))
    return dup, dcw[:CONV_W]


def _matmul_residual(z, w, x, name):
    S, K = z.shape
    D = w.shape[1]
    ts = _tile(S, 512, SUBLANES)

    def body(z_ref, w_ref, x_ref, o_ref):
        o_ref[...] = x_ref[...] + jnp.dot(z_ref[...], w_ref[...], preferred_element_type=F32)

    return pl.pallas_call(
        body, name=name, grid=(S // ts,),
        in_specs=[pl.BlockSpec((ts, K), lambda s: (s, 0)), pl.BlockSpec((K, D), lambda s: (0, 0)),
                  pl.BlockSpec((ts, D), lambda s: (s, 0))],
        out_specs=pl.BlockSpec((ts, D), lambda s: (s, 0)),
        out_shape=jax.ShapeDtypeStruct((S, D), F32),
        compiler_params=_params("parallel"),
    )(z, w, x)


def _wgrad(a, b, name):
    S, M = a.shape
    P, _, C = b.shape
    tm, tn, tk = _tile(M, 1408), _tile(C, 1408), _tile(S, 1024, SUBLANES)
    nnc = C // tn

    def body(a_ref, b_ref, o_ref):
        @pl.when(pl.program_id(2) == 0)
        def _():
            o_ref[...] = jnp.zeros_like(o_ref)

        o_ref[...] += lax.dot_general(a_ref[...], b_ref[0].astype(BF), TN_DIMS, preferred_element_type=F32)

    return pl.pallas_call(
        body, name=name, grid=(M // tm, P * nnc, S // tk),
        in_specs=[pl.BlockSpec((tk, tm), lambda m, n, k: (k, m)),
                  pl.BlockSpec((1, tk, tn), lambda m, n, k: (n // nnc, k, n % nnc))],
        out_specs=pl.BlockSpec((tm, tn), lambda m, n, k: (m, n)),
        out_shape=jax.ShapeDtypeStruct((M, P * C), F32),
        compiler_params=_params("parallel", "parallel", "arbitrary"),
    )(a, b)


def _dnorm(dp, w, x, g, dy, name):
    P, S, C = dp.shape
    D = x.shape[1]
    ts = _tile(S, 512, SUBLANES)

    def body(dp_ref, w_ref, x_ref, g_ref, dy_ref, dx_ref, dg_ref):
        @pl.when(pl.program_id(0) == 0)
        def _():
            dg_ref[...] = jnp.zeros_like(dg_ref)

        dxn = lax.dot_general(dp_ref[0], w_ref[:, 0:C], NT_DIMS, preferred_element_type=F32)
        for p in range(1, P):
            dxn = dxn + lax.dot_general(dp_ref[p], w_ref[:, p * C:(p + 1) * C], NT_DIMS, preferred_element_type=F32)
        xf = x_ref[...]
        r = lax.rsqrt(jnp.mean(xf * xf, axis=-1, keepdims=True) + RMS_EPS)
        xhat = xf * r
        dxhat = dxn * g_ref[...]
        dx_ref[...] = dy_ref[...] + r * (dxhat - xhat * jnp.mean(dxhat * xhat, axis=-1, keepdims=True))
        dg_ref[...] += jnp.broadcast_to(jnp.sum(dxn * xhat, axis=0, keepdims=True), dg_ref.shape)

    dx, dg = pl.pallas_call(
        body, name=name, grid=(S // ts,),
        in_specs=[pl.BlockSpec((P, ts, C), lambda s: (0, s, 0)),
                  pl.BlockSpec((D, P * C), lambda s: (0, 0), pipeline_mode=pl.Buffered(1)),
                  pl.BlockSpec((ts, D), lambda s: (s, 0)),
                  pl.BlockSpec((1, D), lambda s: (0, 0)),
                  pl.BlockSpec((ts, D), lambda s: (s, 0))],
        out_specs=[pl.BlockSpec((ts, D), lambda s: (s, 0)),
                   pl.BlockSpec((SUBLANES, D), lambda s: (0, 0))],
        out_shape=[jax.ShapeDtypeStruct((S, D), F32), jax.ShapeDtypeStruct((SUBLANES, D), F32)],
        compiler_params=_params("arbitrary"),
    )(dp, w, x, g.reshape(1, D), dy)
    return dx, dg[0]


def _head_masks(shape, hd):
    lane = lax.broadcasted_iota(jnp.int32, shape, 1)
    return lane < hd


def _pair_sum(v, lo):
    s0 = jnp.sum(jnp.where(lo, v, 0.0), axis=-1, keepdims=True)
    s1 = jnp.sum(jnp.where(lo, 0.0, v), axis=-1, keepdims=True)
    return jnp.where(lo, s0, s1)


def _headnorm_bwd(src, part, colblk, w, dys, D, name):
    S = src.shape[1]
    hd = w.shape[0]
    ts = _tile(S, 512, SUBLANES)
    w2 = jnp.tile(w, LANES // hd).reshape(1, LANES)
    n_dy = len(dys)

    def body(x_ref, w_ref, *rest):
        dy_refs, dx_ref, dw_ref = rest[:n_dy], rest[n_dy], rest[n_dy + 1]

        @pl.when(pl.program_id(0) == 0)
        def _():
            dw_ref[...] = jnp.zeros_like(dw_ref)

        lo = _head_masks((ts, LANES), hd)
        for t in range(D // LANES):
            cols = slice(t * LANES, (t + 1) * LANES)
            xt = x_ref[0, :, cols]
            dy = dy_refs[0][:, cols]
            for other in dy_refs[1:]:
                dy = dy + other[:, cols]
            r = lax.rsqrt(_pair_sum(xt * xt, lo) * (1.0 / hd) + RMS_EPS)
            xhat = xt * r
            dxhat = dy * w_ref[...]
            mean = _pair_sum(dxhat * xhat, lo) * (1.0 / hd)
            dx_ref[:, cols] = (r * (dxhat - xhat * mean)).astype(BF)
            dw_ref[:, cols] += jnp.broadcast_to(jnp.sum(dy * xhat, axis=0, keepdims=True), (SUBLANES, LANES))

    dx, dw = pl.pallas_call(
        body, name=name, grid=(S // ts,),
        in_specs=[pl.BlockSpec((1, ts, D), lambda s: (part, s, colblk)), pl.BlockSpec((1, LANES), lambda s: (0, 0))]
        + [pl.BlockSpec((ts, D), lambda s: (s, 0))] * n_dy,
        out_specs=[pl.BlockSpec((ts, D), lambda s: (s, 0)), pl.BlockSpec((SUBLANES, D), lambda s: (0, 0))],
        out_shape=[jax.ShapeDtypeStruct((S, D), BF), jax.ShapeDtypeStruct((SUBLANES, D), F32)],
        compiler_params=_params("arbitrary"),
    )(src, w2, *dys)
    return dx, jnp.sum(dw[0].reshape(D // hd, hd), axis=0)


def _tri(n, lower):
    r, c = lax.broadcasted_iota(jnp.int32, (n, n), 0), lax.broadcasted_iota(jnp.int32, (n, n), 1)
    return jnp.where((c <= r) if lower else (c >= r), 1.0, 0.0).astype(BF)


def _dot_exact(t, v):
    hi = v.astype(BF)
    r1 = v - hi.astype(F32)
    mid = r1.astype(BF)
    lo = (r1 - mid.astype(F32)).astype(BF)
    dot = lambda u: jnp.dot(t, u, preferred_element_type=F32)
    return dot(hi) + dot(mid) + dot(lo)


def _gate_fwd(kvf, b_pad, colblk, name):
    S = kvf.shape[1]
    ts = _tile(S, 512, SUBLANES)

    def body(f_ref, b_ref, c_ref, carry):
        @pl.when(pl.program_id(0) == 0)
        def _():
            carry[...] = jnp.zeros_like(carry)

        f = f_ref[0] + b_ref[...]
        ls = jnp.minimum(f, 0.0) - jnp.log1p(jnp.exp(-jnp.abs(f)))
        tri = _tri(ts, lower=True)
        c = _dot_exact(tri, ls) + carry[0:1, :]
        c_ref[...] = c
        carry[...] = jnp.broadcast_to(c[ts - 1:ts, :], carry.shape)

    return pl.pallas_call(
        body, name=name, grid=(S // ts,),
        in_specs=[pl.BlockSpec((1, ts, LANES), lambda s: (0, s, colblk)), pl.BlockSpec((1, LANES), lambda s: (0, 0))],
        out_specs=pl.BlockSpec((ts, LANES), lambda s: (s, 0)),
        out_shape=jax.ShapeDtypeStruct((S, LANES), F32),
        scratch_shapes=[pltpu.VMEM((SUBLANES, LANES), F32)],
        compiler_params=_params("arbitrary"),
    )(kvf, b_pad)


def _gate_bwd(dc, kvf, b_pad, colblk, name):
    S = kvf.shape[1]
    ts = _tile(S, 512, SUBLANES)
    n_s = S // ts

    def body(dc_ref, f_ref, b_ref, df_ref, db_ref, carry):
        @pl.when(pl.program_id(0) == 0)
        def _():
            carry[...] = jnp.zeros_like(carry)
            db_ref[...] = jnp.zeros_like(db_ref)

        tri = _tri(ts, lower=False)
        dls = _dot_exact(tri, dc_ref[...]) + carry[0:1, :]
        f = f_ref[0] + b_ref[...]
        df = dls * jax.nn.sigmoid(-f)
        df_ref[...] = df
        db_ref[...] += jnp.broadcast_to(jnp.sum(df, axis=0, keepdims=True), db_ref.shape)
        carry[...] = jnp.broadcast_to(dls[0:1, :], carry.shape)

    df, db = pl.pallas_call(
        body, name=name, grid=(n_s,),
        in_specs=[pl.BlockSpec((ts, LANES), lambda s: (n_s - 1 - s, 0)),
                  pl.BlockSpec((1, ts, LANES), lambda s: (0, n_s - 1 - s, colblk)),
                  pl.BlockSpec((1, LANES), lambda s: (0, 0))],
        out_specs=[pl.BlockSpec((ts, LANES), lambda s: (n_s - 1 - s, 0)),
                   pl.BlockSpec((SUBLANES, LANES), lambda s: (0, 0))],
        out_shape=[jax.ShapeDtypeStruct((S, LANES), F32), jax.ShapeDtypeStruct((SUBLANES, LANES), F32)],
        scratch_shapes=[pltpu.VMEM((SUBLANES, LANES), F32)],
        compiler_params=_params("arbitrary"),
    )(dc, kvf, b_pad)
    return df, db[0]


def _attn_tile(S):
    return _tile(S, 512, LANES)


def _split_heads(v, lo):
    zero = jnp.zeros_like(v)
    return jnp.where(lo, v, zero), jnp.where(lo, zero, v)


def _augment(src, part, colblk, c, mode, hd, D, name, norm_w=None, scale=1.0):
    S = src.shape[1]
    ts = _tile(S, 512, 2 * SUBLANES)
    w2 = jnp.tile(jnp.ones((hd,), F32) if norm_w is None else norm_w, LANES // hd).reshape(1, LANES)

    def body(b_ref, w_ref, c_ref, o0_ref, o1_ref):
        lane = lax.broadcasted_iota(jnp.int32, (ts, LANES), 1)
        lo = lane < hd
        cc = c_ref[...] * LOG2E
        for t in range(D // LANES):
            cols = slice(t * LANES, (t + 1) * LANES)
            bt = b_ref[0, :, cols]
            if norm_w is not None:
                r = lax.rsqrt(_pair_sum(bt * bt, lo) * (1.0 / hd) + RMS_EPS)
                bt = bt * r * w_ref[...] * scale
            bt = bt.astype(BF)
            for h, o_ref in ((0, o0_ref), (1, o1_ref)):
                first = hd if h == 0 else 0
                keep = (lane < hd) if h == 0 else (lane >= hd)
                if mode == "v":
                    vals = (1.0,)
                else:
                    col = cc[:, 2 * t + h:2 * t + h + 1]
                    hi = col.astype(BF).astype(F32)
                    mid = (col - hi).astype(BF).astype(F32)
                    pieces = (hi, mid, col - hi - mid)
                    vals = pieces + (1.0, 1.0, 1.0) if mode == "q" else (1.0, 1.0, 1.0) + tuple(-v for v in pieces)
                aug = jnp.zeros((ts, LANES), F32)
                for i, v in enumerate(vals):
                    aug = jnp.where(lane == first + i, v, aug)
                o_ref[:, cols] = jnp.where(keep, bt, aug.astype(BF))

    spec = pl.BlockSpec((ts, D), lambda s: (s, 0))
    return pl.pallas_call(
        body, name=name, grid=(S // ts,),
        in_specs=[pl.BlockSpec((1, ts, D), lambda s: (part, s, colblk)), pl.BlockSpec((1, LANES), lambda s: (0, 0)),
                  pl.BlockSpec((ts, LANES), lambda s: (s, 0))],
        out_specs=[spec, spec],
        out_shape=[jax.ShapeDtypeStruct((S, D), BF)] * 2,
        compiler_params=_params("parallel"),
    )(src, w2, c)


def _attn_fwd(qa, ka, va, qg, hd, name):
    S, D = qa[0].shape
    P = D // LANES
    tq = _attn_tile(S)
    nq = S // tq

    def body(q0_ref, q1_ref, k0_ref, k1_ref, v0_ref, v1_ref, g_ref, o_ref, og_ref, m_ref, l_ref, s_buf):
        qi = pl.program_id(1)
        lo = _head_masks((tq, LANES), hd)
        qh = (q0_ref[...], q1_ref[...])
        k_refs, v_refs = (k0_ref, k1_ref), (v0_ref, v1_ref)
        causal = lax.broadcasted_iota(jnp.int32, (tq, tq), 1) <= lax.broadcasted_iota(jnp.int32, (tq, tq), 0)

        def scores(ki, slot):
            off = pl.multiple_of(ki * tq, tq)
            for h in range(2):
                s_buf[slot, h] = lax.dot_general(qh[h], k_refs[h][pl.ds(off, tq), :], NT_DIMS,
                                                 preferred_element_type=F32)

        def consume(ki, slot, carry, masked):
            off = pl.multiple_of(ki * tq, tq)
            out = []
            for h in range(2):
                m, acc = carry[h]
                s = s_buf[slot, h]
                if masked:
                    s = jnp.where(causal, s, -jnp.inf)
                m_new = jnp.maximum(m, jnp.ceil(jnp.max(s, axis=-1, keepdims=True)))
                p = jnp.exp2(s - m_new)
                acc = jnp.exp2(m - m_new) * acc + jnp.dot(p.astype(BF), v_refs[h][pl.ds(off, tq), :],
                                                          preferred_element_type=F32)
                out.append((m_new, acc))
            return tuple(out)

        def step(j, carry):
            scores(2 * j + 1, 1)
            carry = consume(2 * j, 0, carry, False)
            scores(2 * j + 2, 0)
            return consume(2 * j + 1, 1, carry, False)

        def finish_even(carry):
            return consume(qi, 0, carry, True)

        def finish_odd(carry):
            scores(qi, 1)
            return consume(qi, 1, consume(qi - 1, 0, carry, False), True)

        init = tuple((jnp.full((tq, 1), -jnp.inf, F32), jnp.zeros((tq, LANES), F32)) for _ in range(2))
        scores(0, 0)
        carry = lax.fori_loop(0, qi // 2, step, init)
        (m0, a0), (m1, a1) = lax.cond(qi % 2 == 0, finish_even, finish_odd, carry)
        l0, l1 = a0[:, hd:hd + 1], a1[:, 0:1]
        o = jnp.where(lo, a0 / l0, a1 / l1)
        o_ref[...] = o
        og_ref[...] = (o * jax.nn.sigmoid(g_ref[0])).astype(BF)
        lane2 = lax.broadcasted_iota(jnp.int32, (tq, 2), 1)
        m_ref[0] = jnp.where(lane2 == 0, m0, m1)
        l_ref[0] = jnp.where(lane2 == 0, l0, l1)

    tile = pl.BlockSpec((tq, LANES), lambda p, i: (i, p))
    whole = pl.BlockSpec((S, LANES), lambda p, i: (0, p))
    stat = pl.BlockSpec((1, tq, 2), lambda p, i: (p, i, 0))
    return pl.pallas_call(
        body, name=name, grid=(P, nq),
        in_specs=[tile, tile, whole, whole, whole, whole, pl.BlockSpec((1, tq, LANES), lambda p, i: (1, i, p))],
        out_specs=[tile, tile, stat, stat],
        out_shape=[jax.ShapeDtypeStruct((S, D), F32), jax.ShapeDtypeStruct((S, D), BF),
                   jax.ShapeDtypeStruct((P, S, 2), F32), jax.ShapeDtypeStruct((P, S, 2), F32)],
        scratch_shapes=[pltpu.VMEM((2, 2, tq, tq), F32)],
        compiler_params=_params("parallel", "arbitrary"),
    )(*qa, *ka, *va, qg)


def _attn_out_bwd(dy, w_out, o, qg, l, hd, name):
    S, D = o.shape
    P = D // LANES
    ts = _tile(S, 512, 2 * SUBLANES)

    def body(dy_ref, w_ref, o_ref, g_ref, l_ref, do_ref, dg_ref, e_ref):
        lo = _head_masks((ts, LANES), hd)
        lane2 = lax.broadcasted_iota(jnp.int32, (ts, 2), 1)
        dog = lax.dot_general(dy_ref[...].astype(BF), w_ref[...], NT_DIMS, preferred_element_type=F32)
        for t in range(P):
            cols = slice(t * LANES, (t + 1) * LANES)
            sg = jax.nn.sigmoid(g_ref[0, :, cols])
            dog_t, o_t, l_t = dog[:, cols], o_ref[:, cols], l_ref[t]
            g = (dog_t * sg / jnp.where(lo, l_t[:, 0:1], l_t[:, 1:2])).astype(BF)
            do_ref[:, cols] = g
            dg_ref[:, cols] = (dog_t * o_t * sg * (1.0 - sg)).astype(BF)
            prod = g.astype(F32) * o_t
            e0 = jnp.sum(jnp.where(lo, prod, 0.0), axis=-1, keepdims=True)
            e1 = jnp.sum(jnp.where(lo, 0.0, prod), axis=-1, keepdims=True)
            e_ref[t] = jnp.where(lane2 == 0, e0, e1)

    rows = pl.BlockSpec((ts, D), lambda s: (s, 0))
    stat = pl.BlockSpec((P, ts, 2), lambda s: (0, s, 0))
    return pl.pallas_call(
        body, name=name, grid=(S // ts,),
        in_specs=[rows, pl.BlockSpec(w_out.shape, lambda s: (0, 0)), rows,
                  pl.BlockSpec((1, ts, D), lambda s: (1, s, 0)), stat],
        out_specs=[rows, rows, stat],
        out_shape=[jax.ShapeDtypeStruct((S, D), BF), jax.ShapeDtypeStruct((S, D), BF),
                   jax.ShapeDtypeStruct((P, S, 2), F32)],
        compiler_params=_params("parallel"),
    )(dy, w_out, o, qg, l)


def _attn_bwd(qa, ka, vb, g, m_row, e_row, hd, name):
    S, D = vb.shape
    P = D // LANES
    tk = _attn_tile(S)
    nk = S // tk
    scale = hd ** -0.5

    def body(q0_ref, q1_ref, g_ref, k0_ref, k1_ref, v_ref, m_ref, e_ref, dq_ref, dk_ref, dv_ref, dc_ref,
             st_buf, dp_buf):
        ki = pl.program_id(1)

        @pl.when(ki == 0)
        def _():
            dq_ref[...] = jnp.zeros_like(dq_ref)

        lo = _head_masks((tk, LANES), hd)
        kh = (k0_ref[...], k1_ref[...])
        q_refs = (q0_ref, q1_ref)
        vh = _split_heads(v_ref[...], lo)
        causal_t = lax.broadcasted_iota(jnp.int32, (tk, tk), 0) <= lax.broadcasted_iota(jnp.int32, (tk, tk), 1)

        def stage(qi, slot):
            off = pl.multiple_of(qi * tk, tk)
            gb = g_ref[pl.ds(off, tk), :]
            for h in range(2):
                st_buf[slot, h] = lax.dot_general(kh[h], q_refs[h][pl.ds(off, tk), :], NT_DIMS,
                                                  preferred_element_type=F32)
                dp_buf[slot, h] = lax.dot_general(vh[h], gb, NT_DIMS, preferred_element_type=F32)

        def consume(qi, slot, carry, masked):
            off = pl.multiple_of(qi * tk, tk)
            gb = g_ref[pl.ds(off, tk), :]
            m_t, e_t = m_ref[0, qi], e_ref[0, qi]
            out, dq_parts = [], []
            for h in range(2):
                dk, dv, dc = carry[h]
                qb = q_refs[h][pl.ds(off, tk), :]
                pt = jnp.exp2(st_buf[slot, h] - m_t[h:h + 1, :])
                if masked:
                    pt = jnp.where(causal_t, pt, 0.0)
                pb = pt.astype(BF)
                dv = dv + jnp.dot(pb, gb, preferred_element_type=F32)
                dst = pb.astype(F32) * (dp_buf[slot, h] - e_t[h:h + 1, :])
                db = dst.astype(BF)
                dk = dk + jnp.dot(db, qb, preferred_element_type=F32)
                dc = dc - jnp.sum(dst, axis=-1, keepdims=True)
                dq_parts.append(lax.dot_general(db, kh[h], TN_DIMS, preferred_element_type=F32))
                out.append((dk, dv, dc))
            dq_ref[pl.ds(off, tk), :] += jnp.where(lo, dq_parts[0], dq_parts[1]) * scale
            return tuple(out)

        n_after = nk - 1 - ki

        def step(j, carry):
            b = ki + 1 + 2 * j
            stage(b + 1, 0)
            carry = consume(b, 1, carry, False)
            stage(b + 2, 1)
            return consume(b + 1, 0, carry, False)

        def rest_one(carry):
            return consume(nk - 1, 1, carry, False)

        def rest_two(carry):
            stage(nk - 1, 0)
            return consume(nk - 1, 0, consume(nk - 2, 1, carry, False), False)

        init = tuple((jnp.zeros((tk, LANES), F32), jnp.zeros((tk, LANES), F32), jnp.zeros((tk, 1), F32))
                     for _ in range(2))
        stage(ki, 0)
        stage(jnp.minimum(ki + 1, nk - 1), 1)
        carry = consume(ki, 0, init, True)
        carry = lax.fori_loop(0, (n_after - 1) // 2, step, carry)
        which = jnp.where(n_after == 0, 0, 2 - n_after % 2)
        (dk0, dv0, dc0), (dk1, dv1, dc1) = lax.switch(which, [lambda c: c, rest_one, rest_two], carry)
        dk_ref[...] = jnp.where(lo, dk0, dk1) * (1.0 / LOG2E)
        dv_ref[...] = jnp.where(lo, dv0, dv1)
        lane2 = lax.broadcasted_iota(jnp.int32, (tk, 2), 1)
        dc_ref[0] = jnp.where(lane2 == 0, dc0, dc1)

    tile = pl.BlockSpec((tk, LANES), lambda p, i: (i, p))
    whole = pl.BlockSpec((S, LANES), lambda p, i: (0, p))
    row_spec = pl.BlockSpec((1, nk, 2, tk), lambda p, i: (p, 0, 0, 0))
    return pl.pallas_call(
        body, name=name, grid=(P, nk),
        in_specs=[whole, whole, whole, tile, tile, tile, row_spec, row_spec],
        out_specs=[whole, tile, tile, pl.BlockSpec((1, tk, 2), lambda p, i: (p, i, 0))],
        out_shape=[jax.ShapeDtypeStruct((S, D), F32), jax.ShapeDtypeStruct((S, D), F32),
                   jax.ShapeDtypeStruct((S, D), F32), jax.ShapeDtypeStruct((P, S, 2), F32)],
        scratch_shapes=[pltpu.VMEM((2, 2, tk, tk), F32), pltpu.VMEM((2, 2, tk, tk), F32)],
        compiler_params=_params("parallel", "arbitrary"),
    )(*qa, g, *ka, vb, m_row, e_row)


def _loss_head(y, t, name):
    S, D = y.shape
    ts = _tile(S, 512, SUBLANES)

    def body(y_ref, t_ref, dy_ref, l_ref):
        @pl.when(pl.program_id(0) == 0)
        def _():
            l_ref[...] = jnp.zeros_like(l_ref)

        e = y_ref[...] - t_ref[...]
        dy_ref[...] = e * (1.0 / D)
        part = 0.5 * jnp.sum(jnp.mean(e * e, axis=-1, keepdims=True), axis=0, keepdims=True)
        l_ref[...] += jnp.broadcast_to(part, l_ref.shape)

    return pl.pallas_call(
        body, name=name, grid=(S // ts,),
        in_specs=[pl.BlockSpec((ts, D), lambda s: (s, 0)), pl.BlockSpec((ts, D), lambda s: (s, 0))],
        out_specs=[pl.BlockSpec((ts, D), lambda s: (s, 0)), pl.BlockSpec((SUBLANES, LANES), lambda s: (0, 0))],
        out_shape=[jax.ShapeDtypeStruct((S, D), F32), jax.ShapeDtypeStruct((SUBLANES, LANES), F32)],
        compiler_params=_params("arbitrary"),
    )(y, t)


def _adamw(w, g, m, v, name):
    shape = w.shape
    cols = shape[-1]
    as2d = lambda a: a.reshape(-1, cols)
    rows = as2d(w).shape[0]
    tr = _tile(rows, 256, SUBLANES) if rows % SUBLANES == 0 else rows
    c1 = 1.0 - ADAM_B1 ** ADAM_STEP
    c2 = 1.0 - ADAM_B2 ** ADAM_STEP

    def body(w_ref, g_ref, m_ref, v_ref, d_ref, nm_ref, nv_ref):
        gg = g_ref[...]
        nm = ADAM_B1 * m_ref[...] + (1.0 - ADAM_B1) * gg
        nv = ADAM_B2 * v_ref[...] + (1.0 - ADAM_B2) * (gg * gg)
        d_ref[...] = -ADAM_LR * ((nm / c1) / (jnp.sqrt(nv / c2) + ADAM_EPS) + ADAM_WD * w_ref[...])
        nm_ref[...] = nm
        nv_ref[...] = nv

    spec = pl.BlockSpec((tr, cols), lambda r: (r, 0))
    outs = pl.pallas_call(
        body, name=name, grid=(rows // tr,), in_specs=[spec] * 4, out_specs=[spec] * 3,
        out_shape=[jax.ShapeDtypeStruct((rows, cols), F32)] * 3,
        compiler_params=_params("parallel"),
    )(as2d(w), as2d(g), as2d(m), as2d(v))
    return tuple(o.reshape(shape) for o in outs)


def _place():
    return lax.axis_index("x"), lax.axis_index("y"), lax.axis_index("c")


def _other_chips(x, y):
    return [(1 - x, y), (x, 1 - y), (1 - x, 1 - y)]


def _remote(src, dst, send_sems, recv_sems, k, to):
    return pltpu.make_async_remote_copy(src_ref=src, dst_ref=dst, send_sem=send_sems.at[k], recv_sem=recv_sems.at[k],
                                        device_id=to, device_id_type=MESH)


def _half(c, rh):
    return pl.ds(pl.multiple_of(c * rh, 2 * SUBLANES), rh)


def _allgather_weights(srcs):
    n = len(srcs)

    def body(*refs):
        src, dst, (send_sems, recv_sems) = refs[:n], refs[n:2 * n], refs[2 * n:]
        x, y, c = _place()
        me = 2 * x + y
        sib = (x, y, 1 - c)
        chips = _other_chips(x, y)
        rh = [a.shape[0] // 2 for a in srcs]
        first = [_remote(src[g].at[_half(c, rh[g])], dst[g].at[me, _half(c, rh[g])], send_sems, recv_sems,
                         6 * g + j, (cx, cy, c)) for j, (cx, cy) in enumerate(chips) for g in range(n)]
        for cp in first:
            cp.start()
        passed = []
        for j, (cx, cy) in enumerate(chips):
            for g in range(n):
                landed = dst[g].at[2 * cx + cy, _half(c, rh[g])]
                _remote(landed, landed, send_sems, recv_sems, 6 * g + j, sib).wait_recv()
                cp = _remote(landed, landed, send_sems, recv_sems, 6 * g + 3 + j, sib)
                cp.start()
                passed.append(cp)
        for j, (cx, cy) in enumerate(chips):
            for g in range(n):
                landed = dst[g].at[2 * cx + cy, _half(1 - c, rh[g])]
                _remote(landed, landed, send_sems, recv_sems, 6 * g + 3 + j, sib).wait_recv()
        for cp in first + passed:
            cp.wait_send()

    outs = pl.pallas_call(
        body, name="allgather_weights", in_specs=[ANY] * n, out_specs=[ANY] * n,
        out_shape=[jax.ShapeDtypeStruct((N_CHIPS,) + a.shape, a.dtype) for a in srcs],
        scratch_shapes=[pltpu.SemaphoreType.DMA((6 * n,)), pltpu.SemaphoreType.DMA((6 * n,))],
    )(*srcs)
    x, y, _ = _place()
    return [lax.dynamic_update_slice_in_dim(o, a[None], 2 * x + y, axis=0) for o, a in zip(outs, srcs)]


def _pair_exchange(gs):
    n = len(gs)

    def body(*refs):
        g_refs, t_refs, (send_sems, recv_sems) = refs[:n], refs[n:2 * n], refs[2 * n:]
        x, y, c = _place()
        cps = [_remote(g_refs[g].at[k, 1 - c], t_refs[g].at[k], send_sems, recv_sems, N_CHIPS * g + k, (x, y, 1 - c))
               for g in range(n) for k in range(N_CHIPS)]
        for cp in cps:
            cp.start()
        for cp in cps:
            cp.wait()

    return pl.pallas_call(
        body, name="grad_pair_exchange", in_specs=[ANY] * n, out_specs=[ANY] * n,
        out_shape=[jax.ShapeDtypeStruct((a.shape[0],) + a.shape[2:], a.dtype) for a in gs],
        scratch_shapes=[pltpu.SemaphoreType.DMA((N_CHIPS * n,)), pltpu.SemaphoreType.DMA((N_CHIPS * n,))],
    )(*gs)


def _pair_add(g, t, c, name):
    n, _, rh, W = g.shape
    tr = _tile(rh, 256, 2 * SUBLANES)

    def body(c_ref, g_ref, t_ref, o_ref):
        o_ref[...] = (g_ref[0] + t_ref[...]).astype(BF)

    return pl.pallas_call(
        body, name=name,
        grid_spec=pltpu.PrefetchScalarGridSpec(
            num_scalar_prefetch=1, grid=(n, rh // tr),
            in_specs=[pl.BlockSpec((1, 1, tr, W), lambda k, i, c_ref: (k, c_ref[0], i, 0)),
                      pl.BlockSpec((1, tr, W), lambda k, i, c_ref: (k, i, 0))],
            out_specs=pl.BlockSpec((1, tr, W), lambda k, i, c_ref: (k, i, 0))),
        out_shape=jax.ShapeDtypeStruct((n, rh, W), BF),
        compiler_params=_params("parallel", "parallel"),
    )(c.reshape(1).astype(jnp.int32), g, t)


def _chip_exchange(parts):
    n = len(parts)

    def body(*refs):
        a_refs, t_refs, (send_sems, recv_sems) = refs[:n], refs[n:2 * n], refs[2 * n:]
        x, y, c = _place()
        cps = [_remote(a_refs[g].at[2 * cx + cy], t_refs[g].at[j], send_sems, recv_sems, 3 * g + j, (cx, cy, c))
               for j, (cx, cy) in enumerate(_other_chips(x, y)) for g in range(n)]
        for cp in cps:
            cp.start()
        for cp in cps:
            cp.wait()

    return pl.pallas_call(
        body, name="grad_chip_exchange", in_specs=[ANY] * n, out_specs=[ANY] * n,
        out_shape=[jax.ShapeDtypeStruct((3,) + a.shape[1:], a.dtype) for a in parts],
        scratch_shapes=[pltpu.SemaphoreType.DMA((3 * n,)), pltpu.SemaphoreType.DMA((3 * n,))],
    )(*parts)


def _chip_add(g, t1, t2, c, me, name):
    _, _, rh, W = g.shape
    tr = _tile(rh, 256, 2 * SUBLANES)

    def body(c_ref, me_ref, g_ref, t1_ref, t2_ref, o_ref):
        own = g_ref[0, 0] + t1_ref[0]
        o_ref[...] = own + t2_ref[0].astype(F32) + t2_ref[1].astype(F32) + t2_ref[2].astype(F32)

    return pl.pallas_call(
        body, name=name,
        grid_spec=pltpu.PrefetchScalarGridSpec(
            num_scalar_prefetch=2, grid=(rh // tr,),
            in_specs=[pl.BlockSpec((1, 1, tr, W), lambda i, c_ref, me_ref: (me_ref[0], c_ref[0], i, 0)),
                      pl.BlockSpec((1, tr, W), lambda i, c_ref, me_ref: (me_ref[0], i, 0)),
                      pl.BlockSpec((3, tr, W), lambda i, c_ref, me_ref: (0, i, 0))],
            out_specs=pl.BlockSpec((tr, W), lambda i, c_ref, me_ref: (i, 0))),
        out_shape=jax.ShapeDtypeStruct((rh, W), F32),
        compiler_params=_params("parallel"),
    )(c.reshape(1).astype(jnp.int32), me.reshape(1).astype(jnp.int32), g, t1, t2)


def _pair_share(hs):
    n = len(hs)

    def body(*refs):
        h_refs, f_refs, (send_sems, recv_sems) = refs[:n], refs[n:2 * n], refs[2 * n:]
        x, y, c = _place()
        cps = [_remote(h_refs[g], f_refs[g], send_sems, recv_sems, g, (x, y, 1 - c)) for g in range(n)]
        for cp in cps:
            cp.start()
        for cp in cps:
            cp.wait()

    return pl.pallas_call(
        body, name="grad_pair_share", in_specs=[ANY] * n, out_specs=[ANY] * n,
        out_shape=[jax.ShapeDtypeStruct(a.shape, a.dtype) for a in hs],
        scratch_shapes=[pltpu.SemaphoreType.DMA((n,)), pltpu.SemaphoreType.DMA((n,))],
    )(*hs)


def _allreduce_small(pack, name):
    rows, W = pack.shape

    def body(p_ref, o_ref, buf, send_sems, recv_sems):
        x, y, c = _place()
        me = 4 * x + 2 * y + c
        buf[me] = p_ref[...]
        cps = []
        for r in range(1, 8):
            fx, fy, fc = (r >> 2) & 1, (r >> 1) & 1, r & 1
            to = (1 - x if fx else x, 1 - y if fy else y, 1 - c if fc else c)
            cps.append(_remote(p_ref, buf.at[me], send_sems, recv_sems, r - 1, to))
        for cp in cps:
            cp.start()
        for r in range(1, 8):
            fx, fy, fc = (r >> 2) & 1, (r >> 1) & 1, r & 1
            frm = 4 * (1 - x if fx else x) + 2 * (1 - y if fy else y) + (1 - c if fc else c)
            _remote(p_ref, buf.at[frm], send_sems, recv_sems, r - 1, (x, y, c)).wait_recv()
        for cp in cps:
            cp.wait_send()
        acc = buf[0]
        for i in range(1, 8):
            acc = acc + buf[i]
        o_ref[...] = acc

    return pl.pallas_call(
        body, name=name, in_specs=[VMEM], out_specs=VMEM,
        out_shape=jax.ShapeDtypeStruct((rows, W), F32),
        scratch_shapes=[pltpu.VMEM((8, rows, W), F32), pltpu.SemaphoreType.DMA((7,)), pltpu.SemaphoreType.DMA((7,))],
    )(pack)


def _width_groups(arrs):
    widths = []
    for a in arrs:
        if a.shape[-1] not in widths:
            widths.append(a.shape[-1])
    return [[i for i, a in enumerate(arrs) if a.shape[-1] == w] for w in widths]


def _rows2d(a):
    return a.reshape(-1, a.shape[-1])


def _split_rows_like(buf, like, lead=()):
    out, off = [], 0
    for a in like:
        n = a.size // a.shape[-1]
        out.append(buf[..., off:off + n, :].reshape(tuple(lead) + a.shape))
        off += n
    return out


def _join_cols(g):
    nd = g.ndim
    return jnp.moveaxis(g, 0, nd - 2).reshape(g.shape[1:-1] + (N_CHIPS * g.shape[-1],))


def _join_rows(g):
    return jnp.moveaxis(g, 0, 1).reshape(g.shape[1], N_CHIPS * g.shape[2], g.shape[3])


def _row_layout(a, tq):
    P, S, _ = a.shape
    return a.reshape(P, S // tq, tq, 2).transpose(0, 1, 3, 2)


def _pad_row(v, width=FLAT_W):
    flat = v.reshape(-1)
    rows = -(-flat.shape[0] // width)
    return jnp.pad(flat, (0, rows * width - flat.shape[0]))


def kernel(x, attn_norm, ffn_norm, a_w_in, a_conv, a_w_out, kv_norm, w_kvf, b_f, k_norm, b_w_qg, q_norm, b_w_out, ffn_w_up, ffn_conv, ffn_w_down, loss_target, m_attn_norm, m_ffn_norm, m_a_w_in, m_a_conv, m_a_w_out, m_kv_norm, m_w_kvf, m_b_f, m_k_norm, m_b_w_qg, m_q_norm, m_b_w_out, m_ffn_w_up, m_ffn_conv, m_ffn_w_down, v_attn_norm, v_ffn_norm, v_a_w_in, v_a_conv, v_a_w_out, v_kv_norm, v_w_kvf, v_b_f, v_k_norm, v_b_w_qg, v_q_norm, v_b_w_out, v_ffn_w_up, v_ffn_conv, v_ffn_w_down):
    xs = x[0]
    S, D = xs.shape
    H, hd = b_f.shape[0], k_norm.shape[0]
    depth = attn_norm.shape[0]
    n_a = a_w_in.shape[0]
    P = D // LANES
    assert LANES == 2 * hd and H * hd == D, "the attention kernels hold two heads per lane tile"
    mx, my, mc = _place()
    chip = 2 * mx + my

    big = [a_w_in, a_w_out, w_kvf, b_w_qg, b_w_out, ffn_w_up, ffn_w_down]
    groups = _width_groups(big)
    gathered = _allgather_weights([jnp.concatenate([_rows2d(big[i]).astype(BF) for i in idx]) for idx in groups])
    by_chip = [None] * len(big)
    for idx, buf in zip(groups, gathered):
        for i, part in zip(idx, _split_rows_like(buf, [big[i] for i in idx], (N_CHIPS,))):
            by_chip[i] = part
    g_in, g_out, g_kvf, g_qg, g_bout, g_up, g_down = by_chip
    wa_in, wb_qg, w_up = _join_cols(g_in), _join_cols(g_qg), _join_cols(g_up)
    wa_out, wb_out, w_down = _join_rows(g_out), _join_rows(g_bout), _join_rows(g_down)
    kvf_cols = 2 * D + LANES
    wkvf = jnp.pad(_join_cols(g_kvf), ((0, 0), (0, kvf_cols - (2 * D + H))))

    def placed(shard):
        full = jnp.zeros(shard.shape[:-1] + (N_CHIPS, shard.shape[-1]), F32)
        full = lax.dynamic_update_slice_in_dim(full, shard[..., None, :], chip, axis=full.ndim - 2)
        return jnp.where(mc == 0, full, 0.0).reshape(-1)

    conv_pack = jnp.concatenate([_pad_row(placed(a_conv)), _pad_row(placed(ffn_conv))]).reshape(-1, FLAT_W)
    conv_full = _allreduce_small(conv_pack, "allgather_conv_taps").reshape(-1)
    n_ac = a_conv.size * N_CHIPS
    a_conv_f = conv_full[:n_ac].reshape(a_conv.shape[:-1] + (-1,))
    off = _pad_row(placed(a_conv)).shape[0]
    ffn_conv_f = conv_full[off:off + ffn_conv.size * N_CHIPS].reshape(ffn_conv.shape[:-1] + (-1,))
    F = ffn_conv_f.shape[-1]

    b_pad = jnp.pad(b_f, (0, LANES - H)).reshape(1, LANES)
    gate_blk = 2 * D // LANES
    tq = _attn_tile(S)
    scale = hd ** -0.5

    saved = []
    cur = xs
    kv = None
    for l in range(depth):
        rec = {"x_in": cur}
        if l < n_a:
            proj, xn, z = _mixer_in_fwd(cur, attn_norm[l], wa_in[l], a_conv_f[l], f"a_in_{l}")
            mid = _matmul_residual(z, wa_out[l], cur, f"a_out_{l}")
            rec.update(proj=proj, xn=xn, z=z)
        else:
            j = l - n_a
            if kv is None:
                kvf, hn = _norm_matmul(cur, kv_norm, wkvf, 1, F32, "kvf_proj")
                vb = kvf[0, :, D:2 * D].astype(BF)
                cgate = _gate_fwd(kvf, b_pad, gate_blk, "gate_cumsum")
                kv = dict(kvf=kvf, hn=hn, vb=vb, cgate=cgate, x_in=cur, dk=[], dv=[], dc=[],
                          ka=_augment(kvf, 0, 0, cgate, "k", hd, D, "k_augment", norm_w=k_norm),
                          va=_augment(kvf, 0, 1, cgate, "v", hd, D, "v_augment"))
            qg, xn = _norm_matmul(cur, attn_norm[l], wb_qg[j], 2, F32, f"qg_proj_{j}")
            qa = _augment(qg, 0, 0, kv["cgate"], "q", hd, D, f"q_augment_{j}", norm_w=q_norm[j], scale=scale * LOG2E)
            o, og, m_max, l_sum = _attn_fwd(qa, kv["ka"], kv["va"], qg, hd, f"attn_fwd_{j}")
            mid = _matmul_residual(og, wb_out[j], cur, f"b_out_{j}")
            rec.update(qg=qg, xn=xn, qa=qa, o=o, og=og, m=m_max, l=l_sum)
        up, xn2, z2 = _ffn_up_fwd(mid, ffn_norm[l], w_up[l], ffn_conv_f[l], f"ffn_up_{l}")
        cur = _matmul_residual(z2, w_down[l], mid, f"ffn_down_{l}")
        rec.update(x_mid=mid, up=up, xn2=xn2, z2=z2)
        saved.append(rec)

    dy, loss_part = _loss_head(cur, loss_target[0], "loss_head")

    g_attn_norm, g_ffn_norm = [None] * depth, [None] * depth
    g_a_in, g_a_conv, g_a_out = [None] * n_a, [None] * n_a, [None] * n_a
    g_qg, g_qn, g_bo = [None] * (depth - n_a), [None] * (depth - n_a), [None] * (depth - n_a)
    g_up, g_fc, g_down = [None] * depth, [None] * depth, [None] * depth
    for l in reversed(range(depth)):
        rec = saved[l]
        dup, g_fc[l] = _ffn_mid_bwd(rec["up"], dy, w_down[l], ffn_conv_f[l], f"ffn_mid_bwd_{l}")
        g_down[l] = _wgrad(rec["z2"], dy[None], f"ffn_down_wgrad_{l}")
        g_up[l] = _wgrad(rec["xn2"], dup, f"ffn_up_wgrad_{l}")
        dy, g_ffn_norm[l] = _dnorm(dup, w_up[l], rec["x_mid"], ffn_norm[l], dy, f"ffn_up_bwd_{l}")
        if l < n_a:
            dproj, g_a_conv[l] = _mixer_mid_bwd(rec["proj"], dy, wa_out[l], a_conv_f[l], f"a_mid_bwd_{l}")
            g_a_out[l] = _wgrad(rec["z"], dy[None], f"a_out_wgrad_{l}")
            g_a_in[l] = _wgrad(rec["xn"], dproj, f"a_in_wgrad_{l}")
            dy, g_attn_norm[l] = _dnorm(dproj, wa_in[l], rec["x_in"], attn_norm[l], dy, f"a_in_bwd_{l}")
        else:
            j = l - n_a
            g_out, dgate, evec = _attn_out_bwd(dy, wb_out[j], rec["o"], rec["qg"], rec["l"], hd,
                                               f"attn_gate_bwd_{j}")
            g_bo[j] = _wgrad(rec["og"], dy[None], f"b_out_wgrad_{j}")
            dqn, dk, dv, dc = _attn_bwd(rec["qa"], kv["ka"], kv["vb"], g_out, _row_layout(rec["m"], tq),
                                        _row_layout(evec, tq), hd, f"attn_bwd_{j}")
            kv["dk"].append(dk)
            kv["dv"].append(dv)
            kv["dc"].append(dc)
            dq_pre, g_qn[j] = _headnorm_bwd(rec["qg"], 0, 0, q_norm[j], [dqn], D, f"q_norm_bwd_{j}")
            dqg = jnp.stack([dq_pre, dgate])
            g_qg[j] = _wgrad(rec["xn"], dqg, f"qg_wgrad_{j}")
            dy, g_attn_norm[l] = _dnorm(dqg, wb_qg[j], rec["x_in"], attn_norm[l], dy, f"qg_bwd_{j}")
            if l == n_a:
                dk_s, g_k_norm = _headnorm_bwd(kv["kvf"], 0, 0, k_norm, kv["dk"], D, "k_norm_bwd")
                dv_s = functools.reduce(jnp.add, kv["dv"]).astype(BF)
                dc_sum = functools.reduce(jnp.add, kv["dc"])
                dc_pad = jnp.pad(dc_sum.transpose(1, 0, 2).reshape(S, H), ((0, 0), (0, LANES - H)))
                df, db = _gate_bwd(dc_pad, kv["kvf"], b_pad, gate_blk, "gate_bwd")
                dkvf = jnp.concatenate([dk_s, dv_s, df.astype(BF)], axis=1)[None]
                g_kvf = _wgrad(kv["hn"], dkvf, "kvf_wgrad")[:, :2 * D + H]
                g_b_f = db[:H]
                dy, g_kv_norm = _dnorm(dkvf, wkvf, kv["x_in"], kv_norm, dy, "kvf_bwd")
    grad_x = dy[None]

    def cols_of(g, k):
        c = g.shape[-1] // N_CHIPS
        return g[:, k * c:(k + 1) * c]

    def rows_of(g, k):
        r = g.shape[0] // N_CHIPS
        return g[k * r:(k + 1) * r]

    per_layer = [g_a_in, g_a_out, [g_kvf], g_qg, g_bo, g_up, g_down]
    of_chip = [cols_of, rows_of, cols_of, cols_of, rows_of, cols_of, rows_of]

    def group_buffer(idx):
        rows = [of_chip[i](g, k) for k in range(N_CHIPS) for i in idx for g in per_layer[i]]
        buf = jnp.concatenate(rows)
        return buf.reshape(N_CHIPS, 2, buf.shape[0] // (2 * N_CHIPS), buf.shape[1])

    g4 = [group_buffer(idx) for idx in groups]
    from_sibling = _pair_exchange(g4)
    from_chips = _chip_exchange([_pair_add(g, t, mc, f"grad_pair_add_{n}") for n, (g, t) in
                                 enumerate(zip(g4, from_sibling))])
    mine = [_chip_add(g, t1, t2, mc, chip, f"grad_chip_add_{n}") for n, (g, t1, t2) in
            enumerate(zip(g4, from_sibling, from_chips))]
    theirs = _pair_share(mine)
    big_grads = [None] * len(big)
    for idx, m_half, t_half in zip(groups, mine, theirs):
        shard = jnp.where(mc == 0, jnp.concatenate([m_half, t_half]), jnp.concatenate([t_half, m_half]))
        for i, part in zip(idx, _split_rows_like(shard, [big[i] for i in idx])):
            big_grads[i] = part

    small = [loss_part[0, :1], jnp.stack(g_attn_norm), jnp.stack(g_ffn_norm), g_kv_norm, g_b_f, g_k_norm,
             jnp.stack(g_qn), jnp.stack(g_a_conv), jnp.stack(g_fc)]
    small_sum = _allreduce_small(jnp.concatenate([_pad_row(s) for s in small]).reshape(-1, FLAT_W),
                                 "allreduce_small_grads").reshape(-1)
    parts, off = [], 0
    for s in small:
        parts.append(small_sum[off:off + s.size].reshape(s.shape))
        off += _pad_row(s).shape[0]
    loss = parts[0][0]
    gr_attn_norm, gr_ffn_norm, gr_kv_norm, gr_b_f, gr_k_norm, gr_q_norm, gr_a_conv_full, gr_ffn_conv_full = parts[1:]

    def my_cols(full):
        c = full.shape[-1] // N_CHIPS
        return lax.dynamic_slice_in_dim(full, chip * c, c, axis=full.ndim - 1)

    gr_a_in, gr_a_out, gr_kvf, gr_qg, gr_bo, gr_up, gr_down = big_grads
    grads = [gr_attn_norm, gr_ffn_norm, gr_a_in, my_cols(gr_a_conv_full), gr_a_out, gr_kv_norm, gr_kvf, gr_b_f,
             gr_k_norm, gr_qg, gr_q_norm, gr_bo, gr_up, my_cols(gr_ffn_conv_full), gr_down]
    weights = [attn_norm, ffn_norm, a_w_in, a_conv, a_w_out, kv_norm, w_kvf, b_f, k_norm, b_w_qg, q_norm, b_w_out,
               ffn_w_up, ffn_conv, ffn_w_down]
    ms = [m_attn_norm, m_ffn_norm, m_a_w_in, m_a_conv, m_a_w_out, m_kv_norm, m_w_kvf, m_b_f, m_k_norm, m_b_w_qg,
          m_q_norm, m_b_w_out, m_ffn_w_up, m_ffn_conv, m_ffn_w_down]
    vs = [v_attn_norm, v_ffn_norm, v_a_w_in, v_a_conv, v_a_w_out, v_kv_norm, v_w_kvf, v_b_f, v_k_norm, v_b_w_qg,
          v_q_norm, v_b_w_out, v_ffn_w_up, v_ffn_conv, v_ffn_w_down]
    deltas, new_ms, new_vs = [], [], []
    for i, (w, g, m, v) in enumerate(zip(weights, grads, ms, vs)):
        d, nm, nv = _adamw(w, g, m, v, f"adamw_{i}")
        deltas.append(d)
        new_ms.append(nm)
        new_vs.append(nv)
    return (loss, grad_x, *grads, *deltas, *new_ms, *new_vs)
```

```python
import functools

import jax
import jax.numpy as jnp
from jax import lax
from jax.experimental import pallas as pl
from jax.experimental.pallas import tpu as pltpu

F32 = jnp.float32
BF = jnp.bfloat16
LANES = 128
SUBLANES = 8
RMS_EPS = 1e-6
LOG2E = 1.4426950408889634
FLAT_W = 1024
N_CHIPS = 4
CONV_W = 3
HALO = SUBLANES

ADAM_LR = 0.001
ADAM_B1 = 0.9
ADAM_B2 = 0.999
ADAM_EPS = 1e-08
ADAM_WD = 0.01
ADAM_STEP = 10

MESH = pl.DeviceIdType.MESH
ANY = pl.BlockSpec(memory_space=pl.ANY)
VMEM = pl.BlockSpec(memory_space=pltpu.VMEM)
NT_DIMS = (((1,), (1,)), ((), ()))
TN_DIMS = (((0,), (0,)), ((), ()))


def _tile(n, pref, mult=LANES):
    t = (min(pref, n) // mult) * mult
    while t >= mult:
        if n % t == 0:
            break
        t -= mult
    if t < mult or (t * 4 < pref and n <= 4 * pref):
        return n
    return t


def _params(*sem):
    return pltpu.CompilerParams(dimension_semantics=sem)


def _norm_matmul(x, g, w, parts, out_dtype, name):
    S, D = x.shape
    C = w.shape[1] // parts
    ts, tn = _tile(S, 512, SUBLANES), _tile(C, 1408)
    npc = C // tn

    def body(x_ref, g_ref, w_ref, o_ref, xn_ref):
        @pl.when(pl.program_id(1) == 0)
        def _():
            xf = x_ref[...]
            r = lax.rsqrt(jnp.mean(xf * xf, axis=-1, keepdims=True) + RMS_EPS)
            xn_ref[...] = (xf * r * g_ref[...]).astype(BF)

        o_ref[0] = jnp.dot(xn_ref[...], w_ref[...], preferred_element_type=F32).astype(out_dtype)

    return pl.pallas_call(
        body, name=name, grid=(S // ts, parts * npc),
        in_specs=[pl.BlockSpec((ts, D), lambda s, n: (s, 0)),
                  pl.BlockSpec((1, D), lambda s, n: (0, 0)),
                  pl.BlockSpec((D, tn), lambda s, n: (0, n))],
        out_specs=[pl.BlockSpec((1, ts, tn), lambda s, n: (n // npc, s, n % npc)),
                   pl.BlockSpec((ts, D), lambda s, n: (s, 0))],
        out_shape=[jax.ShapeDtypeStruct((parts, S, C), out_dtype), jax.ShapeDtypeStruct((S, D), BF)],
        compiler_params=_params("parallel", "arbitrary"),
    )(x, g.reshape(1, D), w)


def _shift_down(u, prev, k):
    r = pltpu.roll(u, k, 0)
    row = lax.broadcasted_iota(jnp.int32, (HALO, u.shape[1]), 0)
    head = r[0:HALO]
    for j in range(k):
        head = jnp.where(row == j, prev[HALO - k + j:HALO - k + j + 1, :], head)
    return jnp.concatenate([head, r[HALO:]], axis=0)


def _shift_up(d, nxt, k):
    n = d.shape[0]
    r = pltpu.roll(d, n - k, 0)
    row = lax.broadcasted_iota(jnp.int32, (HALO, d.shape[1]), 0)
    tail = r[n - HALO:n]
    for j in range(k):
        tail = jnp.where(row == HALO - k + j, nxt[j:j + 1, :], tail)
    return jnp.concatenate([r[0:n - HALO], tail], axis=0)


def _conv3(u, prev, w):
    return _shift_down(u, prev, 2) * w[0:1] + _shift_down(u, prev, 1) * w[1:2] + u * w[2:3]


def _conv3_t(d, nxt, w):
    return d * w[2:3] + _shift_up(d, nxt, 1) * w[1:2] + _shift_up(d, nxt, 2) * w[0:1]


def _tap_rows(t0, t1, t2):
    row = lax.broadcasted_iota(jnp.int32, (SUBLANES, t0.shape[1]), 0)
    return jnp.where(row == 0, t0, jnp.where(row == 1, t1, jnp.where(row == 2, t2, 0.0)))


def _pad_conv(cw):
    return jnp.pad(cw, ((0, SUBLANES - CONV_W), (0, 0)))


def _mixer_in_fwd(x, g, w, cw, name):
    S, D = x.shape
    C = w.shape[1] // 3
    ts, tc = _tile(S, 512, SUBLANES), _tile(C, 1024)
    nc = C // tc

    def body(x_ref, g_ref, wb_ref, wc_ref, wh_ref, cw_ref, p_ref, xn_ref, z_ref, carry):
        s, c = pl.program_id(0), pl.program_id(1)

        @pl.when(c == 0)
        def _():
            xf = x_ref[...]
            r = lax.rsqrt(jnp.mean(xf * xf, axis=-1, keepdims=True) + RMS_EPS)
            xn_ref[...] = (xf * r * g_ref[...]).astype(BF)

        @pl.when(s == 0)
        def _():
            carry[c] = jnp.zeros((HALO, tc), F32)

        xn = xn_ref[...]
        parts = [jnp.dot(xn, w_ref[...], preferred_element_type=F32).astype(BF) for w_ref in (wb_ref, wc_ref, wh_ref)]
        for p, v in enumerate(parts):
            p_ref[p] = v
        u = parts[1].astype(F32) * parts[2].astype(F32)
        cv = _conv3(u, carry[c], cw_ref[...])
        z_ref[...] = (parts[0].astype(F32) * cv).astype(BF)
        carry[c] = u[ts - HALO:ts, :]

    wspec = lambda p: pl.BlockSpec((D, tc), lambda s, c: (0, p * nc + c))
    return pl.pallas_call(
        body, name=name, grid=(S // ts, nc),
        in_specs=[pl.BlockSpec((ts, D), lambda s, c: (s, 0)), pl.BlockSpec((1, D), lambda s, c: (0, 0)),
                  wspec(0), wspec(1), wspec(2), pl.BlockSpec((SUBLANES, tc), lambda s, c: (0, c))],
        out_specs=[pl.BlockSpec((3, ts, tc), lambda s, c: (0, s, c)),
                   pl.BlockSpec((ts, D), lambda s, c: (s, 0)),
                   pl.BlockSpec((ts, tc), lambda s, c: (s, c))],
        out_shape=[jax.ShapeDtypeStruct((3, S, C), BF), jax.ShapeDtypeStruct((S, D), BF),
                   jax.ShapeDtypeStruct((S, C), BF)],
        scratch_shapes=[pltpu.VMEM((nc, HALO, tc), F32)],
        compiler_params=_params("arbitrary", "arbitrary"),
    )(x, g.reshape(1, D), w, w, w, _pad_conv(cw))


def _ffn_up_fwd(x, g, w, cw, name):
    S, D = x.shape
    C = w.shape[1] // 2
    ts, tc = _tile(S, 512, SUBLANES), _tile(C, 1408)
    nc = C // tc

    def body(x_ref, g_ref, wa_ref, wg_ref, cw_ref, up_ref, xn_ref, z_ref, carry):
        s, c = pl.program_id(0), pl.program_id(1)

        @pl.when(c == 0)
        def _():
            xf = x_ref[...]
            r = lax.rsqrt(jnp.mean(xf * xf, axis=-1, keepdims=True) + RMS_EPS)
            xn_ref[...] = (xf * r * g_ref[...]).astype(BF)

        @pl.when(s == 0)
        def _():
            carry[c] = jnp.zeros((HALO, tc), F32)

        xn = xn_ref[...]
        a_b = jnp.dot(xn, wa_ref[...], preferred_element_type=F32).astype(BF)
        g_b = jnp.dot(xn, wg_ref[...], preferred_element_type=F32).astype(BF)
        up_ref[0] = a_b
        up_ref[1] = g_b
        a_pre = a_b.astype(F32)
        a = _conv3(a_pre, carry[c], cw_ref[...])
        z_ref[...] = (a * jax.nn.sigmoid(a) * g_b.astype(F32)).astype(BF)
        carry[c] = a_pre[ts - HALO:ts, :]

    return pl.pallas_call(
        body, name=name, grid=(S // ts, nc),
        in_specs=[pl.BlockSpec((ts, D), lambda s, c: (s, 0)),
                  pl.BlockSpec((1, D), lambda s, c: (0, 0)),
                  pl.BlockSpec((D, tc), lambda s, c: (0, c)),
                  pl.BlockSpec((D, tc), lambda s, c: (0, nc + c)),
                  pl.BlockSpec((SUBLANES, tc), lambda s, c: (0, c))],
        out_specs=[pl.BlockSpec((2, ts, tc), lambda s, c: (0, s, c)),
                   pl.BlockSpec((ts, D), lambda s, c: (s, 0)),
                   pl.BlockSpec((ts, tc), lambda s, c: (s, c))],
        out_shape=[jax.ShapeDtypeStruct((2, S, C), BF), jax.ShapeDtypeStruct((S, D), BF),
                   jax.ShapeDtypeStruct((S, C), BF)],
        scratch_shapes=[pltpu.VMEM((nc, HALO, tc), F32)],
        compiler_params=_params("arbitrary", "arbitrary"),
    )(x, g.reshape(1, D), w, w, _pad_conv(cw))


def _mixer_mid_bwd(proj, dy, w_out, cw, name):
    _, S, C = proj.shape
    D = dy.shape[1]
    ts, tc = _tile(S, 512, SUBLANES), _tile(C, 1024)
    n_s = S // ts
    per = ts // HALO

    def body(b_ref, c_ref, h_ref, dy_ref, w_ref, cp_ref, hp_ref, cw_ref, d_ref, dcw_ref, carry):
        i = pl.program_id(1)
        w = cw_ref[...]
        dz = lax.dot_general(dy_ref[...].astype(BF), w_ref[...], NT_DIMS, preferred_element_type=F32)
        b, c, h = b_ref[0].astype(F32), c_ref[0].astype(F32), h_ref[0].astype(F32)
        u = c * h
        prev = jnp.where(i < n_s - 1, cp_ref[0].astype(F32) * hp_ref[0].astype(F32), 0.0)
        u1, u2 = _shift_down(u, prev, 1), _shift_down(u, prev, 2)
        cv = u2 * w[0:1] + u1 * w[1:2] + u * w[2:3]
        dcv = dz * b
        nxt = jnp.where(i > 0, carry[...], 0.0)
        du = _conv3_t(dcv, nxt, w)
        d_ref[0] = (dz * cv).astype(BF)
        d_ref[1] = (du * h).astype(BF)
        d_ref[2] = (du * c).astype(BF)
        carry[...] = dcv[0:HALO, :]
        part = _tap_rows(jnp.sum(dcv * u2, axis=0, keepdims=True), jnp.sum(dcv * u1, axis=0, keepdims=True),
                         jnp.sum(dcv * u, axis=0, keepdims=True))

        @pl.when(i == 0)
        def _():
            dcw_ref[...] = part

        @pl.when(i > 0)
        def _():
            dcw_ref[...] += part

    tile = lambda p: pl.BlockSpec((1, ts, tc), lambda c, i: (p, n_s - 1 - i, c))
    before = lambda p: pl.BlockSpec((1, HALO, tc), lambda c, i: (p, jnp.maximum((n_s - 1 - i) * per - 1, 0), c))
    dproj, dcw = pl.pallas_call(
        body, name=name, grid=(C // tc, n_s),
        in_specs=[tile(0), tile(1), tile(2),
                  pl.BlockSpec((ts, D), lambda c, i: (n_s - 1 - i, 0)),
                  pl.BlockSpec((tc, D), lambda c, i: (c, 0)),
                  before(1), before(2),
                  pl.BlockSpec((SUBLANES, tc), lambda c, i: (0, c))],
        out_specs=[pl.BlockSpec((3, ts, tc), lambda c, i: (0, n_s - 1 - i, c)),
                   pl.BlockSpec((SUBLANES, tc), lambda c, i: (0, c))],
        out_shape=[jax.ShapeDtypeStruct((3, S, C), BF), jax.ShapeDtypeStruct((SUBLANES, C), F32)],
        scratch_shapes=[pltpu.VMEM((HALO, tc), F32)],
        compiler_params=_params("parallel", "arbitrary"),
    )(proj, proj, proj, dy, w_out, proj, proj, _pad_conv(cw))
    return dproj, dcw[:CONV_W]


def _ffn_mid_bwd(up, dy, w_down, cw, name):
    _, S, C = up.shape
    D = dy.shape[1]
    ts, tc = _tile(S, 512, SUBLANES), _tile(C, 1408)
    n_s = S // ts
    per = ts // HALO

    def body(a_ref, g_ref, dy_ref, w_ref, ap_ref, cw_ref, d_ref, dcw_ref, carry):
        i = pl.program_id(1)
        w = cw_ref[...]
        dz = lax.dot_general(dy_ref[...].astype(BF), w_ref[...], NT_DIMS, preferred_element_type=F32)
        a_pre, g = a_ref[0].astype(F32), g_ref[0].astype(F32)
        prev = jnp.where(i < n_s - 1, ap_ref[0].astype(F32), 0.0)
        a1, a2 = _shift_down(a_pre, prev, 1), _shift_down(a_pre, prev, 2)
        a = a2 * w[0:1] + a1 * w[1:2] + a_pre * w[2:3]
        sg = jax.nn.sigmoid(a)
        da = dz * g * (sg * (1.0 + a * (1.0 - sg)))
        nxt = jnp.where(i > 0, carry[...], 0.0)
        d_ref[0] = _conv3_t(da, nxt, w).astype(BF)
        d_ref[1] = (dz * (a * sg)).astype(BF)
        carry[...] = da[0:HALO, :]
        part = _tap_rows(jnp.sum(da * a2, axis=0, keepdims=True), jnp.sum(da * a1, axis=0, keepdims=True),
                         jnp.sum(da * a_pre, axis=0, keepdims=True))

        @pl.when(i == 0)
        def _():
            dcw_ref[...] = part

        @pl.when(i > 0)
        def _():
            dcw_ref[...] += part

    tile = lambda p: pl.BlockSpec((1, ts, tc), lambda c, i: (p, n_s - 1 - i, c))
    dup, dcw = pl.pallas_call(
        body, name=name, grid=(C // tc, n_s),
        in_specs=[tile(0), tile(1),
                  pl.BlockSpec((ts, D), lambda c, i: (n_s - 1 - i, 0)),
                  pl.BlockSpec((tc, D), lambda c, i: (c, 0)),
                  pl.BlockSpec((1, HALO, tc), lambda c, i: (0, jnp.maximum((n_s - 1 - i) * per - 1, 0), c)),
                  pl.BlockSpec((SUBLANES, tc), lambda c, i: (0, c))],
        out_specs=[pl.BlockSpec((2, ts, tc), lambda c, i: (0, n_s - 1 - i, c)),
                   pl.BlockSpec((SUBLANES, tc), lambda c, i: (0, c))],
        out_shape=[jax.ShapeDtypeStruct((2, S, C), BF), jax.ShapeDtypeStruct((SUBLANES, C), F32)],
        scratch_shapes=[pltpu.VMEM((HALO, tc), F32)],
        compiler_params=_params("parallel", "arbitrary"),
    )(up, up, dy, w_down, up, _pad_conv(cw))
    return dup, dcw[:CONV_W]


def _matmul_residual(z, w, x, name):
    S, K = z.shape
    D = w.shape[1]
    ts = _tile(S, 512, SUBLANES)

    def body(z_ref, w_ref, x_ref, o_ref):
        o_ref[...] = x_ref[...] + jnp.dot(z_ref[...], w_ref[...], preferred_element_type=F32)

    return pl.pallas_call(
        body, name=name, grid=(S // ts,),
        in_specs=[pl.BlockSpec((ts, K), lambda s: (s, 0)), pl.BlockSpec((K, D), lambda s: (0, 0)),
                  pl.BlockSpec((ts, D), lambda s: (s, 0))],
        out_specs=pl.BlockSpec((ts, D), lambda s: (s, 0)),
        out_shape=jax.ShapeDtypeStruct((S, D), F32),
        compiler_params=_params("parallel"),
    )(z, w, x)


def _wgrad(a, b, name):
    S, M = a.shape
    P, _, C = b.shape
    tm, tn, tk = _tile(M, 1408), _tile(C, 1408), _tile(S, 1024, SUBLANES)
    nnc = C // tn

    def body(a_ref, b_ref, o_ref):
        @pl.when(pl.program_id(2) == 0)
        def _():
            o_ref[...] = jnp.zeros_like(o_ref)

        o_ref[...] += lax.dot_general(a_ref[...], b_ref[0].astype(BF), TN_DIMS, preferred_element_type=F32)

    return pl.pallas_call(
        body, name=name, grid=(M // tm, P * nnc, S // tk),
        in_specs=[pl.BlockSpec((tk, tm), lambda m, n, k: (k, m)),
                  pl.BlockSpec((1, tk, tn), lambda m, n, k: (n // nnc, k, n % nnc))],
        out_specs=pl.BlockSpec((tm, tn), lambda m, n, k: (m, n)),
        out_shape=jax.ShapeDtypeStruct((M, P * C), F32),
        compiler_params=_params("parallel", "parallel", "arbitrary"),
    )(a, b)


def _dnorm(dp, w, x, g, dy, name):
    P, S, C = dp.shape
    D = x.shape[1]
    ts = _tile(S, 512, SUBLANES)

    def body(dp_ref, w_ref, x_ref, g_ref, dy_ref, dx_ref, dg_ref):
        @pl.when(pl.program_id(0) == 0)
        def _():
            dg_ref[...] = jnp.zeros_like(dg_ref)

        dxn = lax.dot_general(dp_ref[0], w_ref[:, 0:C], NT_DIMS, preferred_element_type=F32)
        for p in range(1, P):
            dxn = dxn + lax.dot_general(dp_ref[p], w_ref[:, p * C:(p + 1) * C], NT_DIMS, preferred_element_type=F32)
        xf = x_ref[...]
        r = lax.rsqrt(jnp.mean(xf * xf, axis=-1, keepdims=True) + RMS_EPS)
        xhat = xf * r
        dxhat = dxn * g_ref[...]
        dx_ref[...] = dy_ref[...] + r * (dxhat - xhat * jnp.mean(dxhat * xhat, axis=-1, keepdims=True))
        dg_ref[...] += jnp.broadcast_to(jnp.sum(dxn * xhat, axis=0, keepdims=True), dg_ref.shape)

    dx, dg = pl.pallas_call(
        body, name=name, grid=(S // ts,),
        in_specs=[pl.BlockSpec((P, ts, C), lambda s: (0, s, 0)),
                  pl.BlockSpec((D, P * C), lambda s: (0, 0), pipeline_mode=pl.Buffered(1)),
                  pl.BlockSpec((ts, D), lambda s: (s, 0)),
                  pl.BlockSpec((1, D), lambda s: (0, 0)),
                  pl.BlockSpec((ts, D), lambda s: (s, 0))],
        out_specs=[pl.BlockSpec((ts, D), lambda s: (s, 0)),
                   pl.BlockSpec((SUBLANES, D), lambda s: (0, 0))],
        out_shape=[jax.ShapeDtypeStruct((S, D), F32), jax.ShapeDtypeStruct((SUBLANES, D), F32)],
        compiler_params=_params("arbitrary"),
    )(dp, w, x, g.reshape(1, D), dy)
    return dx, dg[0]


def _head_masks(shape, hd):
    lane = lax.broadcasted_iota(jnp.int32, shape, 1)
    return lane < hd


def _pair_sum(v, lo):
    s0 = jnp.sum(jnp.where(lo, v, 0.0), axis=-1, keepdims=True)
    s1 = jnp.sum(jnp.where(lo, 0.0, v), axis=-1, keepdims=True)
    return jnp.where(lo, s0, s1)


def _headnorm_bwd(src, part, colblk, w, dys, D, name):
    S = src.shape[1]
    hd = w.shape[0]
    ts = _tile(S, 512, SUBLANES)
    w2 = jnp.tile(w, LANES // hd).reshape(1, LANES)
    n_dy = len(dys)

    def body(x_ref, w_ref, *rest):
        dy_refs, dx_ref, dw_ref = rest[:n_dy], rest[n_dy], rest[n_dy + 1]

        @pl.when(pl.program_id(0) == 0)
        def _():
            dw_ref[...] = jnp.zeros_like(dw_ref)

        lo = _head_masks((ts, LANES), hd)
        for t in range(D // LANES):
            cols = slice(t * LANES, (t + 1) * LANES)
            xt = x_ref[0, :, cols]
            dy = dy_refs[0][:, cols]
            for other in dy_refs[1:]:
                dy = dy + other[:, cols]
            r = lax.rsqrt(_pair_sum(xt * xt, lo) * (1.0 / hd) + RMS_EPS)
            xhat = xt * r
            dxhat = dy * w_ref[...]
            mean = _pair_sum(dxhat * xhat, lo) * (1.0 / hd)
            dx_ref[:, cols] = (r * (dxhat - xhat * mean)).astype(BF)
            dw_ref[:, cols] += jnp.broadcast_to(jnp.sum(dy * xhat, axis=0, keepdims=True), (SUBLANES, LANES))

    dx, dw = pl.pallas_call(
        body, name=name, grid=(S // ts,),
        in_specs=[pl.BlockSpec((1, ts, D), lambda s: (part, s, colblk)), pl.BlockSpec((1, LANES), lambda s: (0, 0))]
        + [pl.BlockSpec((ts, D), lambda s: (s, 0))] * n_dy,
        out_specs=[pl.BlockSpec((ts, D), lambda s: (s, 0)), pl.BlockSpec((SUBLANES, D), lambda s: (0, 0))],
        out_shape=[jax.ShapeDtypeStruct((S, D), BF), jax.ShapeDtypeStruct((SUBLANES, D), F32)],
        compiler_params=_params("arbitrary"),
    )(src, w2, *dys)
    return dx, jnp.sum(dw[0].reshape(D // hd, hd), axis=0)


def _tri(n, lower):
    r, c = lax.broadcasted_iota(jnp.int32, (n, n), 0), lax.broadcasted_iota(jnp.int32, (n, n), 1)
    return jnp.where((c <= r) if lower else (c >= r), 1.0, 0.0).astype(BF)


def _dot_exact(t, v):
    hi = v.astype(BF)
    r1 = v - hi.astype(F32)
    mid = r1.astype(BF)
    lo = (r1 - mid.astype(F32)).astype(BF)
    dot = lambda u: jnp.dot(t, u, preferred_element_type=F32)
    return dot(hi) + dot(mid) + dot(lo)


def _gate_fwd(kvf, b_pad, colblk, name):
    S = kvf.shape[1]
    ts = _tile(S, 512, SUBLANES)

    def body(f_ref, b_ref, c_ref, carry):
        @pl.when(pl.program_id(0) == 0)
        def _():
            carry[...] = jnp.zeros_like(carry)

        f = f_ref[0] + b_ref[...]
        ls = jnp.minimum(f, 0.0) - jnp.log1p(jnp.exp(-jnp.abs(f)))
        tri = _tri(ts, lower=True)
        c = _dot_exact(tri, ls) + carry[0:1, :]
        c_ref[...] = c
        carry[...] = jnp.broadcast_to(c[ts - 1:ts, :], carry.shape)

    return pl.pallas_call(
        body, name=name, grid=(S // ts,),
        in_specs=[pl.BlockSpec((1, ts, LANES), lambda s: (0, s, colblk)), pl.BlockSpec((1, LANES), lambda s: (0, 0))],
        out_specs=pl.BlockSpec((ts, LANES), lambda s: (s, 0)),
        out_shape=jax.ShapeDtypeStruct((S, LANES), F32),
        scratch_shapes=[pltpu.VMEM((SUBLANES, LANES), F32)],
        compiler_params=_params("arbitrary"),
    )(kvf, b_pad)


def _gate_bwd(dc, kvf, b_pad, colblk, name):
    S = kvf.shape[1]
    ts = _tile(S, 512, SUBLANES)
    n_s = S // ts

    def body(dc_ref, f_ref, b_ref, df_ref, db_ref, carry):
        @pl.when(pl.program_id(0) == 0)
        def _():
            carry[...] = jnp.zeros_like(carry)
            db_ref[...] = jnp.zeros_like(db_ref)

        tri = _tri(ts, lower=False)
        dls = _dot_exact(tri, dc_ref[...]) + carry[0:1, :]
        f = f_ref[0] + b_ref[...]
        df = dls * jax.nn.sigmoid(-f)
        df_ref[...] = df
        db_ref[...] += jnp.broadcast_to(jnp.sum(df, axis=0, keepdims=True), db_ref.shape)
        carry[...] = jnp.broadcast_to(dls[0:1, :], carry.shape)

    df, db = pl.pallas_call(
        body, name=name, grid=(n_s,),
        in_specs=[pl.BlockSpec((ts, LANES), lambda s: (n_s - 1 - s, 0)),
                  pl.BlockSpec((1, ts, LANES), lambda s: (0, n_s - 1 - s, colblk)),
                  pl.BlockSpec((1, LANES), lambda s: (0, 0))],
        out_specs=[pl.BlockSpec((ts, LANES), lambda s: (n_s - 1 - s, 0)),
                   pl.BlockSpec((SUBLANES, LANES), lambda s: (0, 0))],
        out_shape=[jax.ShapeDtypeStruct((S, LANES), F32), jax.ShapeDtypeStruct((SUBLANES, LANES), F32)],
        scratch_shapes=[pltpu.VMEM((SUBLANES, LANES), F32)],
        compiler_params=_params("arbitrary"),
    )(dc, kvf, b_pad)
    return df, db[0]


def _attn_tile(S):
    return _tile(S, 512, LANES)


def _split_heads(v, lo):
    zero = jnp.zeros_like(v)
    return jnp.where(lo, v, zero), jnp.where(lo, zero, v)


def _augment(src, part, colblk, c, mode, hd, D, name, norm_w=None, scale=1.0):
    S = src.shape[1]
    ts = _tile(S, 512, 2 * SUBLANES)
    w2 = jnp.tile(jnp.ones((hd,), F32) if norm_w is None else norm_w, LANES // hd).reshape(1, LANES)

    def body(b_ref, w_ref, c_ref, o0_ref, o1_ref):
        lane = lax.broadcasted_iota(jnp.int32, (ts, LANES), 1)
        lo = lane < hd
        cc = c_ref[...] * LOG2E
        for t in range(D // LANES):
            cols = slice(t * LANES, (t + 1) * LANES)
            bt = b_ref[0, :, cols]
            if norm_w is not None:
                r = lax.rsqrt(_pair_sum(bt * bt, lo) * (1.0 / hd) + RMS_EPS)
                bt = bt * r * w_ref[...] * scale
            bt = bt.astype(BF)
            for h, o_ref in ((0, o0_ref), (1, o1_ref)):
                first = hd if h == 0 else 0
                keep = (lane < hd) if h == 0 else (lane >= hd)
                if mode == "v":
                    vals = (1.0,)
                else:
                    col = cc[:, 2 * t + h:2 * t + h + 1]
                    hi = col.astype(BF).astype(F32)
                    mid = (col - hi).astype(BF).astype(F32)
                    pieces = (hi, mid, col - hi - mid)
                    vals = pieces + (1.0, 1.0, 1.0) if mode == "q" else (1.0, 1.0, 1.0) + tuple(-v for v in pieces)
                aug = jnp.zeros((ts, LANES), F32)
                for i, v in enumerate(vals):
                    aug = jnp.where(lane == first + i, v, aug)
                o_ref[:, cols] = jnp.where(keep, bt, aug.astype(BF))

    spec = pl.BlockSpec((ts, D), lambda s: (s, 0))
    return pl.pallas_call(
        body, name=name, grid=(S // ts,),
        in_specs=[pl.BlockSpec((1, ts, D), lambda s: (part, s, colblk)), pl.BlockSpec((1, LANES), lambda s: (0, 0)),
                  pl.BlockSpec((ts, LANES), lambda s: (s, 0))],
        out_specs=[spec, spec],
        out_shape=[jax.ShapeDtypeStruct((S, D), BF)] * 2,
        compiler_params=_params("parallel"),
    )(src, w2, c)


def _attn_fwd(qa, ka, va, qg, hd, name, gather=()):
    S, D = qa[0].shape
    P = D // LANES
    tq = _attn_tile(S)
    nq = S // tq
    n_g = len(gather)

    def body(q0_ref, q1_ref, k0_ref, k1_ref, v0_ref, v1_ref, g_ref, *rest):
        o_ref, og_ref, m_ref, l_ref = rest[n_g:n_g + 4]
        s_buf = rest[2 * n_g + 4]
        pair, qi = pl.program_id(0), pl.program_id(1)
        if n_g:
            start, pass_on, finish = _gather_phases([a.shape[0] for a in gather], rest[:n_g],
                                                    rest[n_g + 4:2 * n_g + 4], *rest[2 * n_g + 5:])
            pl.when((pair == 0) & (qi == 0))(start)
            pl.when((pair == P // 2) & (qi == 0))(pass_on)
        lo = _head_masks((tq, LANES), hd)
        qh = (q0_ref[...], q1_ref[...])
        k_refs, v_refs = (k0_ref, k1_ref), (v0_ref, v1_ref)
        causal = lax.broadcasted_iota(jnp.int32, (tq, tq), 1) <= lax.broadcasted_iota(jnp.int32, (tq, tq), 0)

        def scores(ki, slot):
            off = pl.multiple_of(ki * tq, tq)
            for h in range(2):
                s_buf[slot, h] = lax.dot_general(qh[h], k_refs[h][pl.ds(off, tq), :], NT_DIMS,
                                                 preferred_element_type=F32)

        def consume(ki, slot, carry, masked):
            off = pl.multiple_of(ki * tq, tq)
            out = []
            for h in range(2):
                m, acc = carry[h]
                s = s_buf[slot, h]
                if masked:
                    s = jnp.where(causal, s, -jnp.inf)
                m_new = jnp.maximum(m, jnp.ceil(jnp.max(s, axis=-1, keepdims=True)))
                p = jnp.exp2(s - m_new)
                acc = jnp.exp2(m - m_new) * acc + jnp.dot(p.astype(BF), v_refs[h][pl.ds(off, tq), :],
                                                          preferred_element_type=F32)
                out.append((m_new, acc))
            return tuple(out)

        def step(j, carry):
            scores(2 * j + 1, 1)
            carry = consume(2 * j, 0, carry, False)
            scores(2 * j + 2, 0)
            return consume(2 * j + 1, 1, carry, False)

        def finish_even(carry):
            return consume(qi, 0, carry, True)

        def finish_odd(carry):
            scores(qi, 1)
            return consume(qi, 1, consume(qi - 1, 0, carry, False), True)

        init = tuple((jnp.full((tq, 1), -jnp.inf, F32), jnp.zeros((tq, LANES), F32)) for _ in range(2))
        scores(0, 0)
        carry = lax.fori_loop(0, qi // 2, step, init)
        (m0, a0), (m1, a1) = lax.cond(qi % 2 == 0, finish_even, finish_odd, carry)
        l0, l1 = a0[:, hd:hd + 1], a1[:, 0:1]
        o = jnp.where(lo, a0 / l0, a1 / l1)
        o_ref[...] = o
        og_ref[...] = (o * jax.nn.sigmoid(g_ref[0])).astype(BF)
        lane2 = lax.broadcasted_iota(jnp.int32, (tq, 2), 1)
        m_ref[0] = jnp.where(lane2 == 0, m0, m1)
        l_ref[0] = jnp.where(lane2 == 0, l0, l1)
        if n_g:
            pl.when((pair == P - 1) & (qi == nq - 1))(finish)

    tile = pl.BlockSpec((tq, LANES), lambda p, i: (i, p))
    whole = pl.BlockSpec((S, LANES), lambda p, i: (0, p))
    stat = pl.BlockSpec((1, tq, 2), lambda p, i: (p, i, 0))
    sems = [pltpu.SemaphoreType.DMA((6 * n_g,)), pltpu.SemaphoreType.DMA((6 * n_g,))] if n_g else []
    outs = pl.pallas_call(
        body, name=name, grid=(P, nq),
        in_specs=[tile, tile, whole, whole, whole, whole, pl.BlockSpec((1, tq, LANES), lambda p, i: (1, i, p))]
        + [ANY] * n_g,
        out_specs=[tile, tile, stat, stat] + [ANY] * n_g,
        out_shape=[jax.ShapeDtypeStruct((S, D), F32), jax.ShapeDtypeStruct((S, D), BF),
                   jax.ShapeDtypeStruct((P, S, 2), F32), jax.ShapeDtypeStruct((P, S, 2), F32)]
        + [jax.ShapeDtypeStruct((N_CHIPS,) + a.shape, a.dtype) for a in gather],
        scratch_shapes=[pltpu.VMEM((2, 2, tq, tq), F32)] + sems,
        compiler_params=_params("arbitrary" if n_g else "parallel", "arbitrary"),
    )(*qa, *ka, *va, qg, *gather)
    return tuple(outs[:4]) + (_fill_own(outs[4:], gather),)


def _attn_out_bwd(dy, w_out, o, qg, l, hd, name):
    S, D = o.shape
    P = D // LANES
    ts = _tile(S, 512, 2 * SUBLANES)

    def body(dy_ref, w_ref, o_ref, g_ref, l_ref, do_ref, dg_ref, e_ref):
        lo = _head_masks((ts, LANES), hd)
        lane2 = lax.broadcasted_iota(jnp.int32, (ts, 2), 1)
        dog = lax.dot_general(dy_ref[...].astype(BF), w_ref[...], NT_DIMS, preferred_element_type=F32)
        for t in range(P):
            cols = slice(t * LANES, (t + 1) * LANES)
            sg = jax.nn.sigmoid(g_ref[0, :, cols])
            dog_t, o_t, l_t = dog[:, cols], o_ref[:, cols], l_ref[t]
            g = (dog_t * sg / jnp.where(lo, l_t[:, 0:1], l_t[:, 1:2])).astype(BF)
            do_ref[:, cols] = g
            dg_ref[:, cols] = (dog_t * o_t * sg * (1.0 - sg)).astype(BF)
            prod = g.astype(F32) * o_t
            e0 = jnp.sum(jnp.where(lo, prod, 0.0), axis=-1, keepdims=True)
            e1 = jnp.sum(jnp.where(lo, 0.0, prod), axis=-1, keepdims=True)
            e_ref[t] = jnp.where(lane2 == 0, e0, e1)

    rows = pl.BlockSpec((ts, D), lambda s: (s, 0))
    stat = pl.BlockSpec((P, ts, 2), lambda s: (0, s, 0))
    return pl.pallas_call(
        body, name=name, grid=(S // ts,),
        in_specs=[rows, pl.BlockSpec(w_out.shape, lambda s: (0, 0)), rows,
                  pl.BlockSpec((1, ts, D), lambda s: (1, s, 0)), stat],
        out_specs=[rows, rows, stat],
        out_shape=[jax.ShapeDtypeStruct((S, D), BF), jax.ShapeDtypeStruct((S, D), BF),
                   jax.ShapeDtypeStruct((P, S, 2), F32)],
        compiler_params=_params("parallel"),
    )(dy, w_out, o, qg, l)


def _attn_bwd(qa, ka, vb, g, m_row, e_row, hd, name):
    S, D = vb.shape
    P = D // LANES
    tk = _attn_tile(S)
    nk = S // tk
    scale = hd ** -0.5

    def body(q0_ref, q1_ref, g_ref, k0_ref, k1_ref, v_ref, m_ref, e_ref, dq_ref, dk_ref, dv_ref, dc_ref,
             st_buf, dp_buf):
        ki = pl.program_id(1)

        @pl.when(ki == 0)
        def _():
            dq_ref[...] = jnp.zeros_like(dq_ref)

        lo = _head_masks((tk, LANES), hd)
        kh = (k0_ref[...], k1_ref[...])
        q_refs = (q0_ref, q1_ref)
        vh = _split_heads(v_ref[...], lo)
        causal_t = lax.broadcasted_iota(jnp.int32, (tk, tk), 0) <= lax.broadcasted_iota(jnp.int32, (tk, tk), 1)

        def stage(qi, slot):
            off = pl.multiple_of(qi * tk, tk)
            gb = g_ref[pl.ds(off, tk), :]
            for h in range(2):
                st_buf[slot, h] = lax.dot_general(kh[h], q_refs[h][pl.ds(off, tk), :], NT_DIMS,
                                                  preferred_element_type=F32)
                dp_buf[slot, h] = lax.dot_general(vh[h], gb, NT_DIMS, preferred_element_type=F32)

        def consume(qi, slot, carry, masked):
            off = pl.multiple_of(qi * tk, tk)
            gb = g_ref[pl.ds(off, tk), :]
            m_t, e_t = m_ref[0, qi], e_ref[0, qi]
            out, dq_parts = [], []
            for h in range(2):
                dk, dv, dc = carry[h]
                qb = q_refs[h][pl.ds(off, tk), :]
                pt = jnp.exp2(st_buf[slot, h] - m_t[h:h + 1, :])
                if masked:
                    pt = jnp.where(causal_t, pt, 0.0)
                pb = pt.astype(BF)
                dv = dv + jnp.dot(pb, gb, preferred_element_type=F32)
                dst = pb.astype(F32) * (dp_buf[slot, h] - e_t[h:h + 1, :])
                db = dst.astype(BF)
                dk = dk + jnp.dot(db, qb, preferred_element_type=F32)
                dc = dc - jnp.sum(dst, axis=-1, keepdims=True)
                dq_parts.append(lax.dot_general(db, kh[h], TN_DIMS, preferred_element_type=F32))
                out.append((dk, dv, dc))
            dq_ref[pl.ds(off, tk), :] += jnp.where(lo, dq_parts[0], dq_parts[1]) * scale
            return tuple(out)

        n_after = nk - 1 - ki

        def step(j, carry):
            b = ki + 1 + 2 * j
            stage(b + 1, 0)
            carry = consume(b, 1, carry, False)
            stage(b + 2, 1)
            return consume(b + 1, 0, carry, False)

        def rest_one(carry):
            return consume(nk - 1, 1, carry, False)

        def rest_two(carry):
            stage(nk - 1, 0)
            return consume(nk - 1, 0, consume(nk - 2, 1, carry, False), False)

        init = tuple((jnp.zeros((tk, LANES), F32), jnp.zeros((tk, LANES), F32), jnp.zeros((tk, 1), F32))
                     for _ in range(2))
        stage(ki, 0)
        stage(jnp.minimum(ki + 1, nk - 1), 1)
        carry = consume(ki, 0, init, True)
        carry = lax.fori_loop(0, (n_after - 1) // 2, step, carry)
        which = jnp.where(n_after == 0, 0, 2 - n_after % 2)
        (dk0, dv0, dc0), (dk1, dv1, dc1) = lax.switch(which, [lambda c: c, rest_one, rest_two], carry)
        dk_ref[...] = jnp.where(lo, dk0, dk1) * (1.0 / LOG2E)
        dv_ref[...] = jnp.where(lo, dv0, dv1)
        lane2 = lax.broadcasted_iota(jnp.int32, (tk, 2), 1)
        dc_ref[0] = jnp.where(lane2 == 0, dc0, dc1)

    tile = pl.BlockSpec((tk, LANES), lambda p, i: (i, p))
    whole = pl.BlockSpec((S, LANES), lambda p, i: (0, p))
    row_spec = pl.BlockSpec((1, nk, 2, tk), lambda p, i: (p, 0, 0, 0))
    return pl.pallas_call(
        body, name=name, grid=(P, nk),
        in_specs=[whole, whole, whole, tile, tile, tile, row_spec, row_spec],
        out_specs=[whole, tile, tile, pl.BlockSpec((1, tk, 2), lambda p, i: (p, i, 0))],
        out_shape=[jax.ShapeDtypeStruct((S, D), F32), jax.ShapeDtypeStruct((S, D), F32),
                   jax.ShapeDtypeStruct((S, D), F32), jax.ShapeDtypeStruct((P, S, 2), F32)],
        scratch_shapes=[pltpu.VMEM((2, 2, tk, tk), F32), pltpu.VMEM((2, 2, tk, tk), F32)],
        compiler_params=_params("parallel", "arbitrary"),
    )(*qa, g, *ka, vb, m_row, e_row)


def _loss_head(y, t, name):
    S, D = y.shape
    ts = _tile(S, 512, SUBLANES)

    def body(y_ref, t_ref, dy_ref, l_ref):
        @pl.when(pl.program_id(0) == 0)
        def _():
            l_ref[...] = jnp.zeros_like(l_ref)

        e = y_ref[...] - t_ref[...]
        dy_ref[...] = e * (1.0 / D)
        part = 0.5 * jnp.sum(jnp.mean(e * e, axis=-1, keepdims=True), axis=0, keepdims=True)
        l_ref[...] += jnp.broadcast_to(part, l_ref.shape)

    return pl.pallas_call(
        body, name=name, grid=(S // ts,),
        in_specs=[pl.BlockSpec((ts, D), lambda s: (s, 0)), pl.BlockSpec((ts, D), lambda s: (s, 0))],
        out_specs=[pl.BlockSpec((ts, D), lambda s: (s, 0)), pl.BlockSpec((SUBLANES, LANES), lambda s: (0, 0))],
        out_shape=[jax.ShapeDtypeStruct((S, D), F32), jax.ShapeDtypeStruct((SUBLANES, LANES), F32)],
        compiler_params=_params("arbitrary"),
    )(y, t)


def _adamw(w, g, m, v, name):
    shape = w.shape
    cols = shape[-1]
    as2d = lambda a: a.reshape(-1, cols)
    rows = as2d(w).shape[0]
    tr = _tile(rows, 256, SUBLANES) if rows % SUBLANES == 0 else rows
    c1 = 1.0 - ADAM_B1 ** ADAM_STEP
    c2 = 1.0 - ADAM_B2 ** ADAM_STEP

    def body(w_ref, g_ref, m_ref, v_ref, d_ref, nm_ref, nv_ref):
        gg = g_ref[...]
        nm = ADAM_B1 * m_ref[...] + (1.0 - ADAM_B1) * gg
        nv = ADAM_B2 * v_ref[...] + (1.0 - ADAM_B2) * (gg * gg)
        d_ref[...] = -ADAM_LR * ((nm / c1) / (jnp.sqrt(nv / c2) + ADAM_EPS) + ADAM_WD * w_ref[...])
        nm_ref[...] = nm
        nv_ref[...] = nv

    spec = pl.BlockSpec((tr, cols), lambda r: (r, 0))
    outs = pl.pallas_call(
        body, name=name, grid=(rows // tr,), in_specs=[spec] * 4, out_specs=[spec] * 3,
        out_shape=[jax.ShapeDtypeStruct((rows, cols), F32)] * 3,
        compiler_params=_params("parallel"),
    )(as2d(w), as2d(g), as2d(m), as2d(v))
    return tuple(o.reshape(shape) for o in outs)


def _place():
    return lax.axis_index("x"), lax.axis_index("y"), lax.axis_index("c")


def _other_chips(x, y):
    return [(1 - x, y), (x, 1 - y), (1 - x, 1 - y)]


def _remote(src, dst, send_sems, recv_sems, k, to):
    return pltpu.make_async_remote_copy(src_ref=src, dst_ref=dst, send_sem=send_sems.at[k], recv_sem=recv_sems.at[k],
                                        device_id=to, device_id_type=MESH)


def _half(c, rh):
    return pl.ds(pl.multiple_of(c * rh, 2 * SUBLANES), rh)


def _gather_phases(rows, src, dst, send_sems, recv_sems):
    n = len(rows)
    x, y, c = _place()
    me = 2 * x + y
    sib = (x, y, 1 - c)
    chips = _other_chips(x, y)
    rh = [r // 2 for r in rows]

    def first():
        return [_remote(src[g].at[_half(c, rh[g])], dst[g].at[me, _half(c, rh[g])], send_sems, recv_sems,
                        6 * g + j, (cx, cy, c)) for j, (cx, cy) in enumerate(chips) for g in range(n)]

    def landed(j, g, core):
        cx, cy = chips[j]
        return dst[g].at[2 * cx + cy, _half(core, rh[g])]

    def passed():
        return [_remote(landed(j, g, c), landed(j, g, c), send_sems, recv_sems, 6 * g + 3 + j, sib)
                for j in range(3) for g in range(n)]

    def start():
        for cp in first():
            cp.start()

    def pass_on():
        cps = passed()
        for j in range(3):
            for g in range(n):
                _remote(landed(j, g, c), landed(j, g, c), send_sems, recv_sems, 6 * g + j, sib).wait_recv()
                cps[j * n + g].start()

    def finish():
        for j in range(3):
            for g in range(n):
                _remote(landed(j, g, 1 - c), landed(j, g, 1 - c), send_sems, recv_sems, 6 * g + 3 + j, sib).wait_recv()
        for cp in first() + passed():
            cp.wait_send()

    return start, pass_on, finish


def _fill_own(outs, srcs):
    x, y, _ = _place()
    return [lax.dynamic_update_slice_in_dim(o, a[None], 2 * x + y, axis=0) for o, a in zip(outs, srcs)]


def _allgather_weights(srcs):
    n = len(srcs)

    def body(*refs):
        for step in _gather_phases([a.shape[0] for a in srcs], refs[:n], refs[n:2 * n], *refs[2 * n:]):
            step()

    outs = pl.pallas_call(
        body, name="allgather_weights", in_specs=[ANY] * n, out_specs=[ANY] * n,
        out_shape=[jax.ShapeDtypeStruct((N_CHIPS,) + a.shape, a.dtype) for a in srcs],
        scratch_shapes=[pltpu.SemaphoreType.DMA((6 * n,)), pltpu.SemaphoreType.DMA((6 * n,))],
    )(*srcs)
    return _fill_own(outs, srcs)


def _pair_exchange(gs):
    n = len(gs)

    def body(*refs):
        g_refs, t_refs, (send_sems, recv_sems) = refs[:n], refs[n:2 * n], refs[2 * n:]
        x, y, c = _place()
        cps = [_remote(g_refs[g].at[k, 1 - c], t_refs[g].at[k], send_sems, recv_sems, N_CHIPS * g + k, (x, y, 1 - c))
               for g in range(n) for k in range(N_CHIPS)]
        for cp in cps:
            cp.start()
        for cp in cps:
            cp.wait()

    return pl.pallas_call(
        body, name="grad_pair_exchange", in_specs=[ANY] * n, out_specs=[ANY] * n,
        out_shape=[jax.ShapeDtypeStruct((a.shape[0],) + a.shape[2:], a.dtype) for a in gs],
        scratch_shapes=[pltpu.SemaphoreType.DMA((N_CHIPS * n,)), pltpu.SemaphoreType.DMA((N_CHIPS * n,))],
    )(*gs)


def _pair_add(g, t, c, name):
    n, _, rh, W = g.shape
    tr = _tile(rh, 256, 2 * SUBLANES)

    def body(c_ref, g_ref, t_ref, o_ref):
        o_ref[...] = (g_ref[0] + t_ref[...]).astype(BF)

    return pl.pallas_call(
        body, name=name,
        grid_spec=pltpu.PrefetchScalarGridSpec(
            num_scalar_prefetch=1, grid=(n, rh // tr),
            in_specs=[pl.BlockSpec((1, 1, tr, W), lambda k, i, c_ref: (k, c_ref[0], i, 0)),
                      pl.BlockSpec((1, tr, W), lambda k, i, c_ref: (k, i, 0))],
            out_specs=pl.BlockSpec((1, tr, W), lambda k, i, c_ref: (k, i, 0))),
        out_shape=jax.ShapeDtypeStruct((n, rh, W), BF),
        compiler_params=_params("parallel", "parallel"),
    )(c.reshape(1).astype(jnp.int32), g, t)


def _chip_exchange(parts):
    n = len(parts)

    def body(*refs):
        a_refs, t_refs, (send_sems, recv_sems) = refs[:n], refs[n:2 * n], refs[2 * n:]
        x, y, c = _place()
        cps = [_remote(a_refs[g].at[2 * cx + cy], t_refs[g].at[j], send_sems, recv_sems, 3 * g + j, (cx, cy, c))
               for j, (cx, cy) in enumerate(_other_chips(x, y)) for g in range(n)]
        for cp in cps:
            cp.start()
        for cp in cps:
            cp.wait()

    return pl.pallas_call(
        body, name="grad_chip_exchange", in_specs=[ANY] * n, out_specs=[ANY] * n,
        out_shape=[jax.ShapeDtypeStruct((3,) + a.shape[1:], a.dtype) for a in parts],
        scratch_shapes=[pltpu.SemaphoreType.DMA((3 * n,)), pltpu.SemaphoreType.DMA((3 * n,))],
    )(*parts)


def _chip_add(g, t1, t2, c, me, name):
    _, _, rh, W = g.shape
    tr = _tile(rh, 256, 2 * SUBLANES)

    def body(c_ref, me_ref, g_ref, t1_ref, t2_ref, o_ref):
        own = g_ref[0, 0] + t1_ref[0]
        o_ref[...] = own + t2_ref[0].astype(F32) + t2_ref[1].astype(F32) + t2_ref[2].astype(F32)

    return pl.pallas_call(
        body, name=name,
        grid_spec=pltpu.PrefetchScalarGridSpec(
            num_scalar_prefetch=2, grid=(rh // tr,),
            in_specs=[pl.BlockSpec((1, 1, tr, W), lambda i, c_ref, me_ref: (me_ref[0], c_ref[0], i, 0)),
                      pl.BlockSpec((1, tr, W), lambda i, c_ref, me_ref: (me_ref[0], i, 0)),
                      pl.BlockSpec((3, tr, W), lambda i, c_ref, me_ref: (0, i, 0))],
            out_specs=pl.BlockSpec((tr, W), lambda i, c_ref, me_ref: (i, 0))),
        out_shape=jax.ShapeDtypeStruct((rh, W), F32),
        compiler_params=_params("parallel"),
    )(c.reshape(1).astype(jnp.int32), me.reshape(1).astype(jnp.int32), g, t1, t2)


def _pair_share(hs):
    n = len(hs)

    def body(*refs):
        h_refs, f_refs, (send_sems, recv_sems) = refs[:n], refs[n:2 * n], refs[2 * n:]
        x, y, c = _place()
        cps = [_remote(h_refs[g], f_refs[g], send_sems, recv_sems, g, (x, y, 1 - c)) for g in range(n)]
        for cp in cps:
            cp.start()
        for cp in cps:
            cp.wait()

    return pl.pallas_call(
        body, name="grad_pair_share", in_specs=[ANY] * n, out_specs=[ANY] * n,
        out_shape=[jax.ShapeDtypeStruct(a.shape, a.dtype) for a in hs],
        scratch_shapes=[pltpu.SemaphoreType.DMA((n,)), pltpu.SemaphoreType.DMA((n,))],
    )(*hs)


def _allreduce_small(pack, name):
    rows, W = pack.shape

    def body(p_ref, o_ref, buf, send_sems, recv_sems):
        x, y, c = _place()
        me = 4 * x + 2 * y + c
        buf[me] = p_ref[...]
        cps = []
        for r in range(1, 8):
            fx, fy, fc = (r >> 2) & 1, (r >> 1) & 1, r & 1
            to = (1 - x if fx else x, 1 - y if fy else y, 1 - c if fc else c)
            cps.append(_remote(p_ref, buf.at[me], send_sems, recv_sems, r - 1, to))
        for cp in cps:
            cp.start()
        for r in range(1, 8):
            fx, fy, fc = (r >> 2) & 1, (r >> 1) & 1, r & 1
            frm = 4 * (1 - x if fx else x) + 2 * (1 - y if fy else y) + (1 - c if fc else c)
            _remote(p_ref, buf.at[frm], send_sems, recv_sems, r - 1, (x, y, c)).wait_recv()
        for cp in cps:
            cp.wait_send()
        acc = buf[0]
        for i in range(1, 8):
            acc = acc + buf[i]
        o_ref[...] = acc

    return pl.pallas_call(
        body, name=name, in_specs=[VMEM], out_specs=VMEM,
        out_shape=jax.ShapeDtypeStruct((rows, W), F32),
        scratch_shapes=[pltpu.VMEM((8, rows, W), F32), pltpu.SemaphoreType.DMA((7,)), pltpu.SemaphoreType.DMA((7,))],
    )(pack)


def _width_groups(arrs):
    widths = []
    for a in arrs:
        if a.shape[-1] not in widths:
            widths.append(a.shape[-1])
    return [[i for i, a in enumerate(arrs) if a.shape[-1] == w] for w in widths]


def _rows2d(a):
    return a.reshape(-1, a.shape[-1])


def _split_rows_like(buf, like, lead=()):
    out, off = [], 0
    for a in like:
        n = a.size // a.shape[-1]
        out.append(buf[..., off:off + n, :].reshape(tuple(lead) + a.shape))
        off += n
    return out


def _join_cols(g):
    nd = g.ndim
    return jnp.moveaxis(g, 0, nd - 2).reshape(g.shape[1:-1] + (N_CHIPS * g.shape[-1],))


def _join_rows(g):
    return jnp.moveaxis(g, 0, 1).reshape(g.shape[1], N_CHIPS * g.shape[2], g.shape[3])


def _row_layout(a, tq):
    P, S, _ = a.shape
    return a.reshape(P, S // tq, tq, 2).transpose(0, 1, 3, 2)


def _pad_row(v, width=FLAT_W):
    flat = v.reshape(-1)
    rows = -(-flat.shape[0] // width)
    return jnp.pad(flat, (0, rows * width - flat.shape[0]))


def kernel(x, attn_norm, ffn_norm, a_w_in, a_conv, a_w_out, kv_norm, w_kvf, b_f, k_norm, b_w_qg, q_norm, b_w_out, ffn_w_up, ffn_conv, ffn_w_down, loss_target, m_attn_norm, m_ffn_norm, m_a_w_in, m_a_conv, m_a_w_out, m_kv_norm, m_w_kvf, m_b_f, m_k_norm, m_b_w_qg, m_q_norm, m_b_w_out, m_ffn_w_up, m_ffn_conv, m_ffn_w_down, v_attn_norm, v_ffn_norm, v_a_w_in, v_a_conv, v_a_w_out, v_kv_norm, v_w_kvf, v_b_f, v_k_norm, v_b_w_qg, v_q_norm, v_b_w_out, v_ffn_w_up, v_ffn_conv, v_ffn_w_down):
    xs = x[0]
    S, D = xs.shape
    H, hd = b_f.shape[0], k_norm.shape[0]
    depth = attn_norm.shape[0]
    n_a = a_w_in.shape[0]
    P = D // LANES
    assert LANES == 2 * hd and H * hd == D, "the attention kernels hold two heads per lane tile"
    mx, my, mc = _place()
    chip = 2 * mx + my

    big = [a_w_in, a_w_out, w_kvf, b_w_qg, b_w_out, ffn_w_up, ffn_w_down]
    groups = _width_groups(big)
    assert depth - n_a >= 2, "the late weight gather takes the second attention layer's projections"
    early = [a_w_in, a_w_out, w_kvf, b_w_qg[:1], ffn_w_up[:n_a], ffn_w_down[:n_a]]
    late = [b_w_qg[1:], b_w_out, ffn_w_up[n_a:], ffn_w_down[n_a:]]

    def packed(ws):
        idx_groups = _width_groups(ws)
        return [jnp.concatenate([_rows2d(ws[i]).astype(BF) for i in idx]) for idx in idx_groups], idx_groups

    def unpacked(bufs, idx_groups, ws):
        out = [None] * len(ws)
        for idx, buf in zip(idx_groups, bufs):
            for i, part in zip(idx, _split_rows_like(buf, [ws[i] for i in idx], (N_CHIPS,))):
                out[i] = part
        return out

    early_src, early_groups = packed(early)
    late_src, late_groups = packed(late)
    g_in, g_out, g_kvf, g_qg, g_up, g_down = unpacked(_allgather_weights(early_src), early_groups, early)
    wa_in, wa_out = _join_cols(g_in), _join_rows(g_out)
    wb_qg, w_up, w_down, wb_out = list(_join_cols(g_qg)), list(_join_cols(g_up)), list(_join_rows(g_down)), []
    kvf_cols = 2 * D + LANES
    wkvf = jnp.pad(_join_cols(g_kvf), ((0, 0), (0, kvf_cols - (2 * D + H))))

    def placed(shard):
        full = jnp.zeros(shard.shape[:-1] + (N_CHIPS, shard.shape[-1]), F32)
        full = lax.dynamic_update_slice_in_dim(full, shard[..., None, :], chip, axis=full.ndim - 2)
        return jnp.where(mc == 0, full, 0.0).reshape(-1)

    conv_pack = jnp.concatenate([_pad_row(placed(a_conv)), _pad_row(placed(ffn_conv))]).reshape(-1, FLAT_W)
    conv_full = _allreduce_small(conv_pack, "allgather_conv_taps").reshape(-1)
    n_ac = a_conv.size * N_CHIPS
    a_conv_f = conv_full[:n_ac].reshape(a_conv.shape[:-1] + (-1,))
    off = _pad_row(placed(a_conv)).shape[0]
    ffn_conv_f = conv_full[off:off + ffn_conv.size * N_CHIPS].reshape(ffn_conv.shape[:-1] + (-1,))
    F = ffn_conv_f.shape[-1]

    b_pad = jnp.pad(b_f, (0, LANES - H)).reshape(1, LANES)
    gate_blk = 2 * D // LANES
    tq = _attn_tile(S)
    scale = hd ** -0.5

    saved = []
    cur = xs
    kv = None
    for l in range(depth):
        rec = {"x_in": cur}
        if l < n_a:
            proj, xn, z = _mixer_in_fwd(cur, attn_norm[l], wa_in[l], a_conv_f[l], f"a_in_{l}")
            mid = _matmul_residual(z, wa_out[l], cur, f"a_out_{l}")
            rec.update(proj=proj, xn=xn, z=z)
        else:
            j = l - n_a
            if kv is None:
                kvf, hn = _norm_matmul(cur, kv_norm, wkvf, 1, F32, "kvf_proj")
                vb = kvf[0, :, D:2 * D].astype(BF)
                cgate = _gate_fwd(kvf, b_pad, gate_blk, "gate_cumsum")
                kv = dict(kvf=kvf, hn=hn, vb=vb, cgate=cgate, x_in=cur, dk=[], dv=[], dc=[],
                          ka=_augment(kvf, 0, 0, cgate, "k", hd, D, "k_augment", norm_w=k_norm),
                          va=_augment(kvf, 0, 1, cgate, "v", hd, D, "v_augment"))
            qg, xn = _norm_matmul(cur, attn_norm[l], wb_qg[j], 2, F32, f"qg_proj_{j}")
            qa = _augment(qg, 0, 0, kv["cgate"], "q", hd, D, f"q_augment_{j}", norm_w=q_norm[j], scale=scale * LOG2E)
            o, og, m_max, l_sum, late_bufs = _attn_fwd(qa, kv["ka"], kv["va"], qg, hd, f"attn_fwd_{j}",
                                                       gather=late_src if j == 0 else ())
            if j == 0:
                g_qg, g_bout, g_up, g_down = unpacked(late_bufs, late_groups, late)
                wb_qg += list(_join_cols(g_qg))
                wb_out += list(_join_rows(g_bout))
                w_up += list(_join_cols(g_up))
                w_down += list(_join_rows(g_down))
            mid = _matmul_residual(og, wb_out[j], cur, f"b_out_{j}")
            rec.update(qg=qg, xn=xn, qa=qa, o=o, og=og, m=m_max, l=l_sum)
        up, xn2, z2 = _ffn_up_fwd(mid, ffn_norm[l], w_up[l], ffn_conv_f[l], f"ffn_up_{l}")
        cur = _matmul_residual(z2, w_down[l], mid, f"ffn_down_{l}")
        rec.update(x_mid=mid, up=up, xn2=xn2, z2=z2)
        saved.append(rec)

    dy, loss_part = _loss_head(cur, loss_target[0], "loss_head")

    g_attn_norm, g_ffn_norm = [None] * depth, [None] * depth
    g_a_in, g_a_conv, g_a_out = [None] * n_a, [None] * n_a, [None] * n_a
    g_qg, g_qn, g_bo = [None] * (depth - n_a), [None] * (depth - n_a), [None] * (depth - n_a)
    g_up, g_fc, g_down = [None] * depth, [None] * depth, [None] * depth
    for l in reversed(range(depth)):
        rec = saved[l]
        dup, g_fc[l] = _ffn_mid_bwd(rec["up"], dy, w_down[l], ffn_conv_f[l], f"ffn_mid_bwd_{l}")
        g_down[l] = _wgrad(rec["z2"], dy[None], f"ffn_down_wgrad_{l}")
        g_up[l] = _wgrad(rec["xn2"], dup, f"ffn_up_wgrad_{l}")
        dy, g_ffn_norm[l] = _dnorm(dup, w_up[l], rec["x_mid"], ffn_norm[l], dy, f"ffn_up_bwd_{l}")
        if l < n_a:
            dproj, g_a_conv[l] = _mixer_mid_bwd(rec["proj"], dy, wa_out[l], a_conv_f[l], f"a_mid_bwd_{l}")
            g_a_out[l] = _wgrad(rec["z"], dy[None], f"a_out_wgrad_{l}")
            g_a_in[l] = _wgrad(rec["xn"], dproj, f"a_in_wgrad_{l}")
            dy, g_attn_norm[l] = _dnorm(dproj, wa_in[l], rec["x_in"], attn_norm[l], dy, f"a_in_bwd_{l}")
        else:
            j = l - n_a
            g_out, dgate, evec = _attn_out_bwd(dy, wb_out[j], rec["o"], rec["qg"], rec["l"], hd,
                                               f"attn_gate_bwd_{j}")
            g_bo[j] = _wgrad(rec["og"], dy[None], f"b_out_wgrad_{j}")
            dqn, dk, dv, dc = _attn_bwd(rec["qa"], kv["ka"], kv["vb"], g_out, _row_layout(rec["m"], tq),
                                        _row_layout(evec, tq), hd, f"attn_bwd_{j}")
            kv["dk"].append(dk)
            kv["dv"].append(dv)
            kv["dc"].append(dc)
            dq_pre, g_qn[j] = _headnorm_bwd(rec["qg"], 0, 0, q_norm[j], [dqn], D, f"q_norm_bwd_{j}")
            dqg = jnp.stack([dq_pre, dgate])
            g_qg[j] = _wgrad(rec["xn"], dqg, f"qg_wgrad_{j}")
            dy, g_attn_norm[l] = _dnorm(dqg, wb_qg[j], rec["x_in"], attn_norm[l], dy, f"qg_bwd_{j}")
            if l == n_a:
                dk_s, g_k_norm = _headnorm_bwd(kv["kvf"], 0, 0, k_norm, kv["dk"], D, "k_norm_bwd")
                dv_s = functools.reduce(jnp.add, kv["dv"]).astype(BF)
                dc_sum = functools.reduce(jnp.add, kv["dc"])
                dc_pad = jnp.pad(dc_sum.transpose(1, 0, 2).reshape(S, H), ((0, 0), (0, LANES - H)))
                df, db = _gate_bwd(dc_pad, kv["kvf"], b_pad, gate_blk, "gate_bwd")
                dkvf = jnp.concatenate([dk_s, dv_s, df.astype(BF)], axis=1)[None]
                g_kvf = _wgrad(kv["hn"], dkvf, "kvf_wgrad")[:, :2 * D + H]
                g_b_f = db[:H]
                dy, g_kv_norm = _dnorm(dkvf, wkvf, kv["x_in"], kv_norm, dy, "kvf_bwd")
    grad_x = dy[None]

    def cols_of(g, k):
        c = g.shape[-1] // N_CHIPS
        return g[:, k * c:(k + 1) * c]

    def rows_of(g, k):
        r = g.shape[0] // N_CHIPS
        return g[k * r:(k + 1) * r]

    per_layer = [g_a_in, g_a_out, [g_kvf], g_qg, g_bo, g_up, g_down]
    of_chip = [cols_of, rows_of, cols_of, cols_of, rows_of, cols_of, rows_of]

    def group_buffer(idx):
        rows = [of_chip[i](g, k) for k in range(N_CHIPS) for i in idx for g in per_layer[i]]
        buf = jnp.concatenate(rows)
        return buf.reshape(N_CHIPS, 2, buf.shape[0] // (2 * N_CHIPS), buf.shape[1])

    g4 = [group_buffer(idx) for idx in groups]
    from_sibling = _pair_exchange(g4)
    from_chips = _chip_exchange([_pair_add(g, t, mc, f"grad_pair_add_{n}") for n, (g, t) in
                                 enumerate(zip(g4, from_sibling))])
    mine = [_chip_add(g, t1, t2, mc, chip, f"grad_chip_add_{n}") for n, (g, t1, t2) in
            enumerate(zip(g4, from_sibling, from_chips))]
    theirs = _pair_share(mine)
    big_grads = [None] * len(big)
    for idx, m_half, t_half in zip(groups, mine, theirs):
        shard = jnp.where(mc == 0, jnp.concatenate([m_half, t_half]), jnp.concatenate([t_half, m_half]))
        for i, part in zip(idx, _split_rows_like(shard, [big[i] for i in idx])):
            big_grads[i] = part

    small = [loss_part[0, :1], jnp.stack(g_attn_norm), jnp.stack(g_ffn_norm), g_kv_norm, g_b_f, g_k_norm,
             jnp.stack(g_qn), jnp.stack(g_a_conv), jnp.stack(g_fc)]
    small_sum = _allreduce_small(jnp.concatenate([_pad_row(s) for s in small]).reshape(-1, FLAT_W),
                                 "allreduce_small_grads").reshape(-1)
    parts, off = [], 0
    for s in small:
        parts.append(small_sum[off:off + s.size].reshape(s.shape))
        off += _pad_row(s).shape[0]
    loss = parts[0][0]
    gr_attn_norm, gr_ffn_norm, gr_kv_norm, gr_b_f, gr_k_norm, gr_q_norm, gr_a_conv_full, gr_ffn_conv_full = parts[1:]

    def my_cols(full):
        c = full.shape[-1] // N_CHIPS
        return lax.dynamic_slice_in_dim(full, chip * c, c, axis=full.ndim - 1)

    gr_a_in, gr_a_out, gr_kvf, gr_qg, gr_bo, gr_up, gr_down = big_grads
    grads = [gr_attn_norm, gr_ffn_norm, gr_a_in, my_cols(gr_a_conv_full), gr_a_out, gr_kv_norm, gr_kvf, gr_b_f,
             gr_k_norm, gr_qg, gr_q_norm, gr_bo, gr_up, my_cols(gr_ffn_conv_full), gr_down]
    weights = [attn_norm, ffn_norm, a_w_in, a_conv, a_w_out, kv_norm, w_kvf, b_f, k_norm, b_w_qg, q_norm, b_w_out,
               ffn_w_up, ffn_conv, ffn_w_down]
    ms = [m_attn_norm, m_ffn_norm, m_a_w_in, m_a_conv, m_a_w_out, m_kv_norm, m_w_kvf, m_b_f, m_k_norm, m_b_w_qg,
          m_q_norm, m_b_w_out, m_ffn_w_up, m_ffn_conv, m_ffn_w_down]
    vs = [v_attn_norm, v_ffn_norm, v_a_w_in, v_a_conv, v_a_w_out, v_kv_norm, v_w_kvf, v_b_f, v_k_norm, v_b_w_qg,
          v_q_norm, v_b_w_out, v_ffn_w_up, v_ffn_conv, v_ffn_w_down]
    deltas, new_ms, new_vs = [], [], []
    for i, (w, g, m, v) in enumerate(zip(weights, grads, ms, vs)):
        d, nm, nv = _adamw(w, g, m, v, f"adamw_{i}")
        deltas.append(d)
        new_ms.append(nm)
        new_vs.append(nv)
    return (loss, grad_x, *grads, *deltas, *new_ms, *new_vs)
```

```python
import functools

import jax
import jax.numpy as jnp
from jax import lax
from jax.experimental import pallas as pl
from jax.experimental.pallas import tpu as pltpu

F32 = jnp.float32
BF = jnp.bfloat16
LANES = 128
SUBLANES = 8
RMS_EPS = 1e-6
LOG2E = 1.4426950408889634
FLAT_W = 1024
N_CHIPS = 4
CONV_W = 3
HALO = SUBLANES

ADAM_LR = 0.001
ADAM_B1 = 0.9
ADAM_B2 = 0.999
ADAM_EPS = 1e-08
ADAM_WD = 0.01
ADAM_STEP = 10

MESH = pl.DeviceIdType.MESH
ANY = pl.BlockSpec(memory_space=pl.ANY)
VMEM = pl.BlockSpec(memory_space=pltpu.VMEM)
NT_DIMS = (((1,), (1,)), ((), ()))
TN_DIMS = (((0,), (0,)), ((), ()))


def _tile(n, pref, mult=LANES):
    t = (min(pref, n) // mult) * mult
    while t >= mult:
        if n % t == 0:
            break
        t -= mult
    if t < mult or (t * 4 < pref and n <= 4 * pref):
        return n
    return t


def _params(*sem):
    return pltpu.CompilerParams(dimension_semantics=sem)


def _norm_matmul(x, g, w, parts, out_dtype, name):
    S, D = x.shape
    C = w.shape[1] // parts
    ts, tn = _tile(S, 512, SUBLANES), _tile(C, 1408)
    npc = C // tn

    def body(x_ref, g_ref, w_ref, o_ref, xn_ref):
        @pl.when(pl.program_id(1) == 0)
        def _():
            xf = x_ref[...]
            r = lax.rsqrt(jnp.mean(xf * xf, axis=-1, keepdims=True) + RMS_EPS)
            xn_ref[...] = (xf * r * g_ref[...]).astype(BF)

        o_ref[0] = jnp.dot(xn_ref[...], w_ref[...], preferred_element_type=F32).astype(out_dtype)

    return pl.pallas_call(
        body, name=name, grid=(S // ts, parts * npc),
        in_specs=[pl.BlockSpec((ts, D), lambda s, n: (s, 0)),
                  pl.BlockSpec((1, D), lambda s, n: (0, 0)),
                  pl.BlockSpec((D, tn), lambda s, n: (0, n))],
        out_specs=[pl.BlockSpec((1, ts, tn), lambda s, n: (n // npc, s, n % npc)),
                   pl.BlockSpec((ts, D), lambda s, n: (s, 0))],
        out_shape=[jax.ShapeDtypeStruct((parts, S, C), out_dtype), jax.ShapeDtypeStruct((S, D), BF)],
        compiler_params=_params("parallel", "arbitrary"),
    )(x, g.reshape(1, D), w)


def _shift_down(u, prev, k):
    r = pltpu.roll(u, k, 0)
    row = lax.broadcasted_iota(jnp.int32, (HALO, u.shape[1]), 0)
    head = r[0:HALO]
    for j in range(k):
        head = jnp.where(row == j, prev[HALO - k + j:HALO - k + j + 1, :], head)
    return jnp.concatenate([head, r[HALO:]], axis=0)


def _shift_up(d, nxt, k):
    n = d.shape[0]
    r = pltpu.roll(d, n - k, 0)
    row = lax.broadcasted_iota(jnp.int32, (HALO, d.shape[1]), 0)
    tail = r[n - HALO:n]
    for j in range(k):
        tail = jnp.where(row == HALO - k + j, nxt[j:j + 1, :], tail)
    return jnp.concatenate([r[0:n - HALO], tail], axis=0)


def _conv3(u, prev, w):
    return _shift_down(u, prev, 2) * w[0:1] + _shift_down(u, prev, 1) * w[1:2] + u * w[2:3]


def _conv3_t(d, nxt, w):
    return d * w[2:3] + _shift_up(d, nxt, 1) * w[1:2] + _shift_up(d, nxt, 2) * w[0:1]


def _tap_rows(t0, t1, t2):
    row = lax.broadcasted_iota(jnp.int32, (SUBLANES, t0.shape[1]), 0)
    return jnp.where(row == 0, t0, jnp.where(row == 1, t1, jnp.where(row == 2, t2, 0.0)))


def _pad_conv(cw):
    return jnp.pad(cw, ((0, SUBLANES - CONV_W), (0, 0)))


def _mixer_in_fwd(x, g, w, cw, name):
    S, D = x.shape
    C = w.shape[1] // 3
    ts, tc = _tile(S, 512, SUBLANES), _tile(C, 1024)
    nc = C // tc

    def body(x_ref, g_ref, wb_ref, wc_ref, wh_ref, cw_ref, p_ref, xn_ref, z_ref, carry):
        s, c = pl.program_id(0), pl.program_id(1)

        @pl.when(c == 0)
        def _():
            xf = x_ref[...]
            r = lax.rsqrt(jnp.mean(xf * xf, axis=-1, keepdims=True) + RMS_EPS)
            xn_ref[...] = (xf * r * g_ref[...]).astype(BF)

        @pl.when(s == 0)
        def _():
            carry[c] = jnp.zeros((HALO, tc), F32)

        xn = xn_ref[...]
        parts = [jnp.dot(xn, w_ref[...], preferred_element_type=F32).astype(BF) for w_ref in (wb_ref, wc_ref, wh_ref)]
        for p, v in enumerate(parts):
            p_ref[p] = v
        u = parts[1].astype(F32) * parts[2].astype(F32)
        cv = _conv3(u, carry[c], cw_ref[...])
        z_ref[...] = (parts[0].astype(F32) * cv).astype(BF)
        carry[c] = u[ts - HALO:ts, :]

    wspec = lambda p: pl.BlockSpec((D, tc), lambda s, c: (0, p * nc + c))
    return pl.pallas_call(
        body, name=name, grid=(S // ts, nc),
        in_specs=[pl.BlockSpec((ts, D), lambda s, c: (s, 0)), pl.BlockSpec((1, D), lambda s, c: (0, 0)),
                  wspec(0), wspec(1), wspec(2), pl.BlockSpec((SUBLANES, tc), lambda s, c: (0, c))],
        out_specs=[pl.BlockSpec((3, ts, tc), lambda s, c: (0, s, c)),
                   pl.BlockSpec((ts, D), lambda s, c: (s, 0)),
                   pl.BlockSpec((ts, tc), lambda s, c: (s, c))],
        out_shape=[jax.ShapeDtypeStruct((3, S, C), BF), jax.ShapeDtypeStruct((S, D), BF),
                   jax.ShapeDtypeStruct((S, C), BF)],
        scratch_shapes=[pltpu.VMEM((nc, HALO, tc), F32)],
        compiler_params=_params("arbitrary", "arbitrary"),
    )(x, g.reshape(1, D), w, w, w, _pad_conv(cw))


def _ffn_up_fwd(x, g, w, cw, name, gather=()):
    S, D = x.shape
    C = w.shape[1] // 2
    ts, tc = _tile(S, 512, SUBLANES), _tile(C, 1408)
    nc = C // tc
    n_s = S // ts
    n_g = len(gather)

    def body(x_ref, g_ref, wa_ref, wg_ref, cw_ref, *rest):
        up_ref, xn_ref, z_ref = rest[n_g:n_g + 3]
        carry = rest[2 * n_g + 3]
        s, c = pl.program_id(0), pl.program_id(1)
        if n_g:
            start, pass_on, finish = _gather_phases([a.shape[0] for a in gather], rest[:n_g],
                                                    rest[n_g + 3:2 * n_g + 3], *rest[2 * n_g + 4:])
            pl.when((s == 0) & (c == 0))(start)
            pl.when((s == (3 * n_s) // 4) & (c == 0))(pass_on)

        @pl.when(c == 0)
        def _():
            xf = x_ref[...]
            r = lax.rsqrt(jnp.mean(xf * xf, axis=-1, keepdims=True) + RMS_EPS)
            xn_ref[...] = (xf * r * g_ref[...]).astype(BF)

        @pl.when(s == 0)
        def _():
            carry[c] = jnp.zeros((HALO, tc), F32)

        xn = xn_ref[...]
        a_b = jnp.dot(xn, wa_ref[...], preferred_element_type=F32).astype(BF)
        g_b = jnp.dot(xn, wg_ref[...], preferred_element_type=F32).astype(BF)
        up_ref[0] = a_b
        up_ref[1] = g_b
        a_pre = a_b.astype(F32)
        a = _conv3(a_pre, carry[c], cw_ref[...])
        z_ref[...] = (a * jax.nn.sigmoid(a) * g_b.astype(F32)).astype(BF)
        carry[c] = a_pre[ts - HALO:ts, :]
        if n_g:
            pl.when((s == n_s - 1) & (c == nc - 1))(finish)

    sems = [pltpu.SemaphoreType.DMA((6 * n_g,)), pltpu.SemaphoreType.DMA((6 * n_g,))] if n_g else []
    outs = pl.pallas_call(
        body, name=name, grid=(n_s, nc),
        in_specs=[pl.BlockSpec((ts, D), lambda s, c: (s, 0)),
                  pl.BlockSpec((1, D), lambda s, c: (0, 0)),
                  pl.BlockSpec((D, tc), lambda s, c: (0, c)),
                  pl.BlockSpec((D, tc), lambda s, c: (0, nc + c)),
                  pl.BlockSpec((SUBLANES, tc), lambda s, c: (0, c))] + [ANY] * n_g,
        out_specs=[pl.BlockSpec((2, ts, tc), lambda s, c: (0, s, c)),
                   pl.BlockSpec((ts, D), lambda s, c: (s, 0)),
                   pl.BlockSpec((ts, tc), lambda s, c: (s, c))] + [ANY] * n_g,
        out_shape=[jax.ShapeDtypeStruct((2, S, C), BF), jax.ShapeDtypeStruct((S, D), BF),
                   jax.ShapeDtypeStruct((S, C), BF)]
        + [jax.ShapeDtypeStruct((N_CHIPS,) + a.shape, a.dtype) for a in gather],
        scratch_shapes=[pltpu.VMEM((nc, HALO, tc), F32)] + sems,
        compiler_params=_params("arbitrary", "arbitrary"),
    )(x, g.reshape(1, D), w, w, _pad_conv(cw), *gather)
    return tuple(outs[:3]) + (_fill_own(outs[3:], gather),)


def _mixer_mid_bwd(proj, dy, w_out, cw, name):
    _, S, C = proj.shape
    D = dy.shape[1]
    ts, tc = _tile(S, 512, SUBLANES), _tile(C, 1024)
    n_s = S // ts
    per = ts // HALO

    def body(b_ref, c_ref, h_ref, dy_ref, w_ref, cp_ref, hp_ref, cw_ref, d_ref, dcw_ref, carry):
        i = pl.program_id(1)
        w = cw_ref[...]
        dz = lax.dot_general(dy_ref[...].astype(BF), w_ref[...], NT_DIMS, preferred_element_type=F32)
        b, c, h = b_ref[0].astype(F32), c_ref[0].astype(F32), h_ref[0].astype(F32)
        u = c * h
        prev = jnp.where(i < n_s - 1, cp_ref[0].astype(F32) * hp_ref[0].astype(F32), 0.0)
        u1, u2 = _shift_down(u, prev, 1), _shift_down(u, prev, 2)
        cv = u2 * w[0:1] + u1 * w[1:2] + u * w[2:3]
        dcv = dz * b
        nxt = jnp.where(i > 0, carry[...], 0.0)
        du = _conv3_t(dcv, nxt, w)
        d_ref[0] = (dz * cv).astype(BF)
        d_ref[1] = (du * h).astype(BF)
        d_ref[2] = (du * c).astype(BF)
        carry[...] = dcv[0:HALO, :]
        part = _tap_rows(jnp.sum(dcv * u2, axis=0, keepdims=True), jnp.sum(dcv * u1, axis=0, keepdims=True),
                         jnp.sum(dcv * u, axis=0, keepdims=True))

        @pl.when(i == 0)
        def _():
            dcw_ref[...] = part

        @pl.when(i > 0)
        def _():
            dcw_ref[...] += part

    tile = lambda p: pl.BlockSpec((1, ts, tc), lambda c, i: (p, n_s - 1 - i, c))
    before = lambda p: pl.BlockSpec((1, HALO, tc), lambda c, i: (p, jnp.maximum((n_s - 1 - i) * per - 1, 0), c))
    dproj, dcw = pl.pallas_call(
        body, name=name, grid=(C // tc, n_s),
        in_specs=[tile(0), tile(1), tile(2),
                  pl.BlockSpec((ts, D), lambda c, i: (n_s - 1 - i, 0)),
                  pl.BlockSpec((tc, D), lambda c, i: (c, 0)),
                  before(1), before(2),
                  pl.BlockSpec((SUBLANES, tc), lambda c, i: (0, c))],
        out_specs=[pl.BlockSpec((3, ts, tc), lambda c, i: (0, n_s - 1 - i, c)),
                   pl.BlockSpec((SUBLANES, tc), lambda c, i: (0, c))],
        out_shape=[jax.ShapeDtypeStruct((3, S, C), BF), jax.ShapeDtypeStruct((SUBLANES, C), F32)],
        scratch_shapes=[pltpu.VMEM((HALO, tc), F32)],
        compiler_params=_params("parallel", "arbitrary"),
    )(proj, proj, proj, dy, w_out, proj, proj, _pad_conv(cw))
    return dproj, dcw[:CONV_W]


def _ffn_mid_bwd(up, dy, w_down, cw, name):
    _, S, C = up.shape
    D = dy.shape[1]
    ts, tc = _tile(S, 512, SUBLANES), _tile(C, 1408)
    n_s = S // ts
    per = ts // HALO

    def body(a_ref, g_ref, dy_ref, w_ref, ap_ref, cw_ref, d_ref, dcw_ref, carry):
        i = pl.program_id(1)
        w = cw_ref[...]
        dz = lax.dot_general(dy_ref[...].astype(BF), w_ref[...], NT_DIMS, preferred_element_type=F32)
        a_pre, g = a_ref[0].astype(F32), g_ref[0].astype(F32)
        prev = jnp.where(i < n_s - 1, ap_ref[0].astype(F32), 0.0)
        a1, a2 = _shift_down(a_pre, prev, 1), _shift_down(a_pre, prev, 2)
        a = a2 * w[0:1] + a1 * w[1:2] + a_pre * w[2:3]
        sg = jax.nn.sigmoid(a)
        da = dz * g * (sg * (1.0 + a * (1.0 - sg)))
        nxt = jnp.where(i > 0, carry[...], 0.0)
        d_ref[0] = _conv3_t(da, nxt, w).astype(BF)
        d_ref[1] = (dz * (a * sg)).astype(BF)
        carry[...] = da[0:HALO, :]
        part = _tap_rows(jnp.sum(da * a2, axis=0, keepdims=True), jnp.sum(da * a1, axis=0, keepdims=True),
                         jnp.sum(da * a_pre, axis=0, keepdims=True))

        @pl.when(i == 0)
        def _():
            dcw_ref[...] = part

        @pl.when(i > 0)
        def _():
            dcw_ref[...] += part

    tile = lambda p: pl.BlockSpec((1, ts, tc), lambda c, i: (p, n_s - 1 - i, c))
    dup, dcw = pl.pallas_call(
        body, name=name, grid=(C // tc, n_s),
        in_specs=[tile(0), tile(1),
                  pl.BlockSpec((ts, D), lambda c, i: (n_s - 1 - i, 0)),
                  pl.BlockSpec((tc, D), lambda c, i: (c, 0)),
                  pl.BlockSpec((1, HALO, tc), lambda c, i: (0, jnp.maximum((n_s - 1 - i) * per - 1, 0), c)),
                  pl.BlockSpec((SUBLANES, tc), lambda c, i: (0, c))],
        out_specs=[pl.BlockSpec((2, ts, tc), lambda c, i: (0, n_s - 1 - i, c)),
                   pl.BlockSpec((SUBLANES, tc), lambda c, i: (0, c))],
        out_shape=[jax.ShapeDtypeStruct((2, S, C), BF), jax.ShapeDtypeStruct((SUBLANES, C), F32)],
        scratch_shapes=[pltpu.VMEM((HALO, tc), F32)],
        compiler_params=_params("parallel", "arbitrary"),
    )(up, up, dy, w_down, up, _pad_conv(cw))
    return dup, dcw[:CONV_W]


def _matmul_residual(z, w, x, name):
    S, K = z.shape
    D = w.shape[1]
    ts = _tile(S, 512, SUBLANES)

    def body(z_ref, w_ref, x_ref, o_ref):
        o_ref[...] = x_ref[...] + jnp.dot(z_ref[...], w_ref[...], preferred_element_type=F32)

    return pl.pallas_call(
        body, name=name, grid=(S // ts,),
        in_specs=[pl.BlockSpec((ts, K), lambda s: (s, 0)), pl.BlockSpec((K, D), lambda s: (0, 0)),
                  pl.BlockSpec((ts, D), lambda s: (s, 0))],
        out_specs=pl.BlockSpec((ts, D), lambda s: (s, 0)),
        out_shape=jax.ShapeDtypeStruct((S, D), F32),
        compiler_params=_params("parallel"),
    )(z, w, x)


def _wgrad(a, b, name):
    S, M = a.shape
    P, _, C = b.shape
    tm, tn, tk = _tile(M, 1408), _tile(C, 1408), _tile(S, 1024, SUBLANES)
    nnc = C // tn

    def body(a_ref, b_ref, o_ref):
        @pl.when(pl.program_id(2) == 0)
        def _():
            o_ref[...] = jnp.zeros_like(o_ref)

        o_ref[...] += lax.dot_general(a_ref[...], b_ref[0].astype(BF), TN_DIMS, preferred_element_type=F32)

    return pl.pallas_call(
        body, name=name, grid=(M // tm, P * nnc, S // tk),
        in_specs=[pl.BlockSpec((tk, tm), lambda m, n, k: (k, m)),
                  pl.BlockSpec((1, tk, tn), lambda m, n, k: (n // nnc, k, n % nnc))],
        out_specs=pl.BlockSpec((tm, tn), lambda m, n, k: (m, n)),
        out_shape=jax.ShapeDtypeStruct((M, P * C), F32),
        compiler_params=_params("parallel", "parallel", "arbitrary"),
    )(a, b)


def _dnorm(dp, w, x, g, dy, name):
    P, S, C = dp.shape
    D = x.shape[1]
    ts = _tile(S, 512, SUBLANES)

    def body(dp_ref, w_ref, x_ref, g_ref, dy_ref, dx_ref, dg_ref):
        @pl.when(pl.program_id(0) == 0)
        def _():
            dg_ref[...] = jnp.zeros_like(dg_ref)

        dxn = lax.dot_general(dp_ref[0], w_ref[:, 0:C], NT_DIMS, preferred_element_type=F32)
        for p in range(1, P):
            dxn = dxn + lax.dot_general(dp_ref[p], w_ref[:, p * C:(p + 1) * C], NT_DIMS, preferred_element_type=F32)
        xf = x_ref[...]
        r = lax.rsqrt(jnp.mean(xf * xf, axis=-1, keepdims=True) + RMS_EPS)
        xhat = xf * r
        dxhat = dxn * g_ref[...]
        dx_ref[...] = dy_ref[...] + r * (dxhat - xhat * jnp.mean(dxhat * xhat, axis=-1, keepdims=True))
        dg_ref[...] += jnp.broadcast_to(jnp.sum(dxn * xhat, axis=0, keepdims=True), dg_ref.shape)

    dx, dg = pl.pallas_call(
        body, name=name, grid=(S // ts,),
        in_specs=[pl.BlockSpec((P, ts, C), lambda s: (0, s, 0)),
                  pl.BlockSpec((D, P * C), lambda s: (0, 0), pipeline_mode=pl.Buffered(1)),
                  pl.BlockSpec((ts, D), lambda s: (s, 0)),
                  pl.BlockSpec((1, D), lambda s: (0, 0)),
                  pl.BlockSpec((ts, D), lambda s: (s, 0))],
        out_specs=[pl.BlockSpec((ts, D), lambda s: (s, 0)),
                   pl.BlockSpec((SUBLANES, D), lambda s: (0, 0))],
        out_shape=[jax.ShapeDtypeStruct((S, D), F32), jax.ShapeDtypeStruct((SUBLANES, D), F32)],
        compiler_params=_params("arbitrary"),
    )(dp, w, x, g.reshape(1, D), dy)
    return dx, dg[0]


def _head_masks(shape, hd):
    lane = lax.broadcasted_iota(jnp.int32, shape, 1)
    return lane < hd


def _pair_sum(v, lo):
    s0 = jnp.sum(jnp.where(lo, v, 0.0), axis=-1, keepdims=True)
    s1 = jnp.sum(jnp.where(lo, 0.0, v), axis=-1, keepdims=True)
    return jnp.where(lo, s0, s1)


def _headnorm_bwd(src, part, colblk, w, dys, D, name):
    S = src.shape[1]
    hd = w.shape[0]
    ts = _tile(S, 512, SUBLANES)
    w2 = jnp.tile(w, LANES // hd).reshape(1, LANES)
    n_dy = len(dys)

    def body(x_ref, w_ref, *rest):
        dy_refs, dx_ref, dw_ref = rest[:n_dy], rest[n_dy], rest[n_dy + 1]

        @pl.when(pl.program_id(0) == 0)
        def _():
            dw_ref[...] = jnp.zeros_like(dw_ref)

        lo = _head_masks((ts, LANES), hd)
        for t in range(D // LANES):
            cols = slice(t * LANES, (t + 1) * LANES)
            xt = x_ref[0, :, cols]
            dy = dy_refs[0][:, cols]
            for other in dy_refs[1:]:
                dy = dy + other[:, cols]
            r = lax.rsqrt(_pair_sum(xt * xt, lo) * (1.0 / hd) + RMS_EPS)
            xhat = xt * r
            dxhat = dy * w_ref[...]
            mean = _pair_sum(dxhat * xhat, lo) * (1.0 / hd)
            dx_ref[:, cols] = (r * (dxhat - xhat * mean)).astype(BF)
            dw_ref[:, cols] += jnp.broadcast_to(jnp.sum(dy * xhat, axis=0, keepdims=True), (SUBLANES, LANES))

    dx, dw = pl.pallas_call(
        body, name=name, grid=(S // ts,),
        in_specs=[pl.BlockSpec((1, ts, D), lambda s: (part, s, colblk)), pl.BlockSpec((1, LANES), lambda s: (0, 0))]
        + [pl.BlockSpec((ts, D), lambda s: (s, 0))] * n_dy,
        out_specs=[pl.BlockSpec((ts, D), lambda s: (s, 0)), pl.BlockSpec((SUBLANES, D), lambda s: (0, 0))],
        out_shape=[jax.ShapeDtypeStruct((S, D), BF), jax.ShapeDtypeStruct((SUBLANES, D), F32)],
        compiler_params=_params("arbitrary"),
    )(src, w2, *dys)
    return dx, jnp.sum(dw[0].reshape(D // hd, hd), axis=0)


def _tri(n, lower):
    r, c = lax.broadcasted_iota(jnp.int32, (n, n), 0), lax.broadcasted_iota(jnp.int32, (n, n), 1)
    return jnp.where((c <= r) if lower else (c >= r), 1.0, 0.0).astype(BF)


def _dot_exact(t, v):
    hi = v.astype(BF)
    r1 = v - hi.astype(F32)
    mid = r1.astype(BF)
    lo = (r1 - mid.astype(F32)).astype(BF)
    dot = lambda u: jnp.dot(t, u, preferred_element_type=F32)
    return dot(hi) + dot(mid) + dot(lo)


def _gate_fwd(kvf, b_pad, colblk, name):
    S = kvf.shape[1]
    ts = _tile(S, 512, SUBLANES)

    def body(f_ref, b_ref, c_ref, carry):
        @pl.when(pl.program_id(0) == 0)
        def _():
            carry[...] = jnp.zeros_like(carry)

        f = f_ref[0] + b_ref[...]
        ls = jnp.minimum(f, 0.0) - jnp.log1p(jnp.exp(-jnp.abs(f)))
        tri = _tri(ts, lower=True)
        c = _dot_exact(tri, ls) + carry[0:1, :]
        c_ref[...] = c
        carry[...] = jnp.broadcast_to(c[ts - 1:ts, :], carry.shape)

    return pl.pallas_call(
        body, name=name, grid=(S // ts,),
        in_specs=[pl.BlockSpec((1, ts, LANES), lambda s: (0, s, colblk)), pl.BlockSpec((1, LANES), lambda s: (0, 0))],
        out_specs=pl.BlockSpec((ts, LANES), lambda s: (s, 0)),
        out_shape=jax.ShapeDtypeStruct((S, LANES), F32),
        scratch_shapes=[pltpu.VMEM((SUBLANES, LANES), F32)],
        compiler_params=_params("arbitrary"),
    )(kvf, b_pad)


def _gate_bwd(dc, kvf, b_pad, colblk, name):
    S = kvf.shape[1]
    ts = _tile(S, 512, SUBLANES)
    n_s = S // ts

    def body(dc_ref, f_ref, b_ref, df_ref, db_ref, carry):
        @pl.when(pl.program_id(0) == 0)
        def _():
            carry[...] = jnp.zeros_like(carry)
            db_ref[...] = jnp.zeros_like(db_ref)

        tri = _tri(ts, lower=False)
        dls = _dot_exact(tri, dc_ref[...]) + carry[0:1, :]
        f = f_ref[0] + b_ref[...]
        df = dls * jax.nn.sigmoid(-f)
        df_ref[...] = df
        db_ref[...] += jnp.broadcast_to(jnp.sum(df, axis=0, keepdims=True), db_ref.shape)
        carry[...] = jnp.broadcast_to(dls[0:1, :], carry.shape)

    df, db = pl.pallas_call(
        body, name=name, grid=(n_s,),
        in_specs=[pl.BlockSpec((ts, LANES), lambda s: (n_s - 1 - s, 0)),
                  pl.BlockSpec((1, ts, LANES), lambda s: (0, n_s - 1 - s, colblk)),
                  pl.BlockSpec((1, LANES), lambda s: (0, 0))],
        out_specs=[pl.BlockSpec((ts, LANES), lambda s: (n_s - 1 - s, 0)),
                   pl.BlockSpec((SUBLANES, LANES), lambda s: (0, 0))],
        out_shape=[jax.ShapeDtypeStruct((S, LANES), F32), jax.ShapeDtypeStruct((SUBLANES, LANES), F32)],
        scratch_shapes=[pltpu.VMEM((SUBLANES, LANES), F32)],
        compiler_params=_params("arbitrary"),
    )(dc, kvf, b_pad)
    return df, db[0]


def _attn_tile(S):
    return _tile(S, 512, LANES)


def _split_heads(v, lo):
    zero = jnp.zeros_like(v)
    return jnp.where(lo, v, zero), jnp.where(lo, zero, v)


def _augment(src, part, colblk, c, mode, hd, D, name, norm_w=None, scale=1.0):
    S = src.shape[1]
    ts = _tile(S, 512, 2 * SUBLANES)
    w2 = jnp.tile(jnp.ones((hd,), F32) if norm_w is None else norm_w, LANES // hd).reshape(1, LANES)

    def body(b_ref, w_ref, c_ref, o0_ref, o1_ref):
        lane = lax.broadcasted_iota(jnp.int32, (ts, LANES), 1)
        lo = lane < hd
        cc = c_ref[...] * LOG2E
        for t in range(D // LANES):
            cols = slice(t * LANES, (t + 1) * LANES)
            bt = b_ref[0, :, cols]
            if norm_w is not None:
                r = lax.rsqrt(_pair_sum(bt * bt, lo) * (1.0 / hd) + RMS_EPS)
                bt = bt * r * w_ref[...] * scale
            bt = bt.astype(BF)
            for h, o_ref in ((0, o0_ref), (1, o1_ref)):
                first = hd if h == 0 else 0
                keep = (lane < hd) if h == 0 else (lane >= hd)
                if mode == "v":
                    vals = (1.0,)
                else:
                    col = cc[:, 2 * t + h:2 * t + h + 1]
                    hi = col.astype(BF).astype(F32)
                    mid = (col - hi).astype(BF).astype(F32)
                    pieces = (hi, mid, col - hi - mid)
                    vals = pieces + (1.0, 1.0, 1.0) if mode == "q" else (1.0, 1.0, 1.0) + tuple(-v for v in pieces)
                aug = jnp.zeros((ts, LANES), F32)
                for i, v in enumerate(vals):
                    aug = jnp.where(lane == first + i, v, aug)
                o_ref[:, cols] = jnp.where(keep, bt, aug.astype(BF))

    spec = pl.BlockSpec((ts, D), lambda s: (s, 0))
    return pl.pallas_call(
        body, name=name, grid=(S // ts,),
        in_specs=[pl.BlockSpec((1, ts, D), lambda s: (part, s, colblk)), pl.BlockSpec((1, LANES), lambda s: (0, 0)),
                  pl.BlockSpec((ts, LANES), lambda s: (s, 0))],
        out_specs=[spec, spec],
        out_shape=[jax.ShapeDtypeStruct((S, D), BF)] * 2,
        compiler_params=_params("parallel"),
    )(src, w2, c)


def _attn_fwd(qa, ka, va, qg, hd, name, gather=()):
    S, D = qa[0].shape
    P = D // LANES
    tq = _attn_tile(S)
    nq = S // tq
    n_g = len(gather)

    def body(q0_ref, q1_ref, k0_ref, k1_ref, v0_ref, v1_ref, g_ref, *rest):
        o_ref, og_ref, m_ref, l_ref = rest[n_g:n_g + 4]
        s_buf = rest[2 * n_g + 4]
        pair, qi = pl.program_id(0), pl.program_id(1)
        if n_g:
            start, pass_on, finish = _gather_phases([a.shape[0] for a in gather], rest[:n_g],
                                                    rest[n_g + 4:2 * n_g + 4], *rest[2 * n_g + 5:])
            pl.when((pair == 0) & (qi == 0))(start)
            pl.when((pair == P // 2) & (qi == 0))(pass_on)
        lo = _head_masks((tq, LANES), hd)
        qh = (q0_ref[...], q1_ref[...])
        k_refs, v_refs = (k0_ref, k1_ref), (v0_ref, v1_ref)
        causal = lax.broadcasted_iota(jnp.int32, (tq, tq), 1) <= lax.broadcasted_iota(jnp.int32, (tq, tq), 0)

        def scores(ki, slot):
            off = pl.multiple_of(ki * tq, tq)
            for h in range(2):
                s_buf[slot, h] = lax.dot_general(qh[h], k_refs[h][pl.ds(off, tq), :], NT_DIMS,
                                                 preferred_element_type=F32)

        def consume(ki, slot, carry, masked):
            off = pl.multiple_of(ki * tq, tq)
            out = []
            for h in range(2):
                m, acc = carry[h]
                s = s_buf[slot, h]
                if masked:
                    s = jnp.where(causal, s, -jnp.inf)
                m_new = jnp.maximum(m, jnp.ceil(jnp.max(s, axis=-1, keepdims=True)))
                p = jnp.exp2(s - m_new)
                acc = jnp.exp2(m - m_new) * acc + jnp.dot(p.astype(BF), v_refs[h][pl.ds(off, tq), :],
                                                          preferred_element_type=F32)
                out.append((m_new, acc))
            return tuple(out)

        def step(j, carry):
            scores(2 * j + 1, 1)
            carry = consume(2 * j, 0, carry, False)
            scores(2 * j + 2, 0)
            return consume(2 * j + 1, 1, carry, False)

        def finish_even(carry):
            return consume(qi, 0, carry, True)

        def finish_odd(carry):
            scores(qi, 1)
            return consume(qi, 1, consume(qi - 1, 0, carry, False), True)

        init = tuple((jnp.full((tq, 1), -jnp.inf, F32), jnp.zeros((tq, LANES), F32)) for _ in range(2))
        scores(0, 0)
        carry = lax.fori_loop(0, qi // 2, step, init)
        (m0, a0), (m1, a1) = lax.cond(qi % 2 == 0, finish_even, finish_odd, carry)
        l0, l1 = a0[:, hd:hd + 1], a1[:, 0:1]
        o = jnp.where(lo, a0 / l0, a1 / l1)
        o_ref[...] = o
        og_ref[...] = (o * jax.nn.sigmoid(g_ref[0])).astype(BF)
        lane2 = lax.broadcasted_iota(jnp.int32, (tq, 2), 1)
        m_ref[0] = jnp.where(lane2 == 0, m0, m1)
        l_ref[0] = jnp.where(lane2 == 0, l0, l1)
        if n_g:
            pl.when((pair == P - 1) & (qi == nq - 1))(finish)

    tile = pl.BlockSpec((tq, LANES), lambda p, i: (i, p))
    whole = pl.BlockSpec((S, LANES), lambda p, i: (0, p))
    stat = pl.BlockSpec((1, tq, 2), lambda p, i: (p, i, 0))
    sems = [pltpu.SemaphoreType.DMA((6 * n_g,)), pltpu.SemaphoreType.DMA((6 * n_g,))] if n_g else []
    outs = pl.pallas_call(
        body, name=name, grid=(P, nq),
        in_specs=[tile, tile, whole, whole, whole, whole, pl.BlockSpec((1, tq, LANES), lambda p, i: (1, i, p))]
        + [ANY] * n_g,
        out_specs=[tile, tile, stat, stat] + [ANY] * n_g,
        out_shape=[jax.ShapeDtypeStruct((S, D), F32), jax.ShapeDtypeStruct((S, D), BF),
                   jax.ShapeDtypeStruct((P, S, 2), F32), jax.ShapeDtypeStruct((P, S, 2), F32)]
        + [jax.ShapeDtypeStruct((N_CHIPS,) + a.shape, a.dtype) for a in gather],
        scratch_shapes=[pltpu.VMEM((2, 2, tq, tq), F32)] + sems,
        compiler_params=_params("arbitrary" if n_g else "parallel", "arbitrary"),
    )(*qa, *ka, *va, qg, *gather)
    return tuple(outs[:4]) + (_fill_own(outs[4:], gather),)


def _attn_out_bwd(dy, w_out, o, qg, l, hd, name):
    S, D = o.shape
    P = D // LANES
    ts = _tile(S, 512, 2 * SUBLANES)

    def body(dy_ref, w_ref, o_ref, g_ref, l_ref, do_ref, dg_ref, e_ref):
        lo = _head_masks((ts, LANES), hd)
        lane2 = lax.broadcasted_iota(jnp.int32, (ts, 2), 1)
        dog = lax.dot_general(dy_ref[...].astype(BF), w_ref[...], NT_DIMS, preferred_element_type=F32)
        for t in range(P):
            cols = slice(t * LANES, (t + 1) * LANES)
            sg = jax.nn.sigmoid(g_ref[0, :, cols])
            dog_t, o_t, l_t = dog[:, cols], o_ref[:, cols], l_ref[t]
            g = (dog_t * sg / jnp.where(lo, l_t[:, 0:1], l_t[:, 1:2])).astype(BF)
            do_ref[:, cols] = g
            dg_ref[:, cols] = (dog_t * o_t * sg * (1.0 - sg)).astype(BF)
            prod = g.astype(F32) * o_t
            e0 = jnp.sum(jnp.where(lo, prod, 0.0), axis=-1, keepdims=True)
            e1 = jnp.sum(jnp.where(lo, 0.0, prod), axis=-1, keepdims=True)
            e_ref[t] = jnp.where(lane2 == 0, e0, e1)

    rows = pl.BlockSpec((ts, D), lambda s: (s, 0))
    stat = pl.BlockSpec((P, ts, 2), lambda s: (0, s, 0))
    return pl.pallas_call(
        body, name=name, grid=(S // ts,),
        in_specs=[rows, pl.BlockSpec(w_out.shape, lambda s: (0, 0)), rows,
                  pl.BlockSpec((1, ts, D), lambda s: (1, s, 0)), stat],
        out_specs=[rows, rows, stat],
        out_shape=[jax.ShapeDtypeStruct((S, D), BF), jax.ShapeDtypeStruct((S, D), BF),
                   jax.ShapeDtypeStruct((P, S, 2), F32)],
        compiler_params=_params("parallel"),
    )(dy, w_out, o, qg, l)


def _attn_bwd(qa, ka, vb, g, m_row, e_row, hd, name):
    S, D = vb.shape
    P = D // LANES
    tk = _attn_tile(S)
    nk = S // tk
    scale = hd ** -0.5

    def body(q0_ref, q1_ref, g_ref, k0_ref, k1_ref, v_ref, m_ref, e_ref, dq_ref, dk_ref, dv_ref, dc_ref,
             st_buf, dp_buf):
        ki = pl.program_id(1)

        @pl.when(ki == 0)
        def _():
            dq_ref[...] = jnp.zeros_like(dq_ref)

        lo = _head_masks((tk, LANES), hd)
        kh = (k0_ref[...], k1_ref[...])
        q_refs = (q0_ref, q1_ref)
        vh = _split_heads(v_ref[...], lo)
        causal_t = lax.broadcasted_iota(jnp.int32, (tk, tk), 0) <= lax.broadcasted_iota(jnp.int32, (tk, tk), 1)

        def stage(qi, slot):
            off = pl.multiple_of(qi * tk, tk)
            gb = g_ref[pl.ds(off, tk), :]
            for h in range(2):
                st_buf[slot, h] = lax.dot_general(kh[h], q_refs[h][pl.ds(off, tk), :], NT_DIMS,
                                                  preferred_element_type=F32)
                dp_buf[slot, h] = lax.dot_general(vh[h], gb, NT_DIMS, preferred_element_type=F32)

        def consume(qi, slot, carry, masked):
            off = pl.multiple_of(qi * tk, tk)
            gb = g_ref[pl.ds(off, tk), :]
            m_t, e_t = m_ref[0, qi], e_ref[0, qi]
            out, dq_parts = [], []
            for h in range(2):
                dk, dv, dc = carry[h]
                qb = q_refs[h][pl.ds(off, tk), :]
                pt = jnp.exp2(st_buf[slot, h] - m_t[h:h + 1, :])
                if masked:
                    pt = jnp.where(causal_t, pt, 0.0)
                pb = pt.astype(BF)
                dv = dv + jnp.dot(pb, gb, preferred_element_type=F32)
                dst = pb.astype(F32) * (dp_buf[slot, h] - e_t[h:h + 1, :])
                db = dst.astype(BF)
                dk = dk + jnp.dot(db, qb, preferred_element_type=F32)
                dc = dc - jnp.sum(dst, axis=-1, keepdims=True)
                dq_parts.append(lax.dot_general(db, kh[h], TN_DIMS, preferred_element_type=F32))
                out.append((dk, dv, dc))
            dq_ref[pl.ds(off, tk), :] += jnp.where(lo, dq_parts[0], dq_parts[1]) * scale
            return tuple(out)

        n_after = nk - 1 - ki

        def step(j, carry):
            b = ki + 1 + 2 * j
            stage(b + 1, 0)
            carry = consume(b, 1, carry, False)
            stage(b + 2, 1)
            return consume(b + 1, 0, carry, False)

        def rest_one(carry):
            return consume(nk - 1, 1, carry, False)

        def rest_two(carry):
            stage(nk - 1, 0)
            return consume(nk - 1, 0, consume(nk - 2, 1, carry, False), False)

        init = tuple((jnp.zeros((tk, LANES), F32), jnp.zeros((tk, LANES), F32), jnp.zeros((tk, 1), F32))
                     for _ in range(2))
        stage(ki, 0)
        stage(jnp.minimum(ki + 1, nk - 1), 1)
        carry = consume(ki, 0, init, True)
        carry = lax.fori_loop(0, (n_after - 1) // 2, step, carry)
        which = jnp.where(n_after == 0, 0, 2 - n_after % 2)
        (dk0, dv0, dc0), (dk1, dv1, dc1) = lax.switch(which, [lambda c: c, rest_one, rest_two], carry)
        dk_ref[...] = jnp.where(lo, dk0, dk1) * (1.0 / LOG2E)
        dv_ref[...] = jnp.where(lo, dv0, dv1)
        lane2 = lax.broadcasted_iota(jnp.int32, (tk, 2), 1)
        dc_ref[0] = jnp.where(lane2 == 0, dc0, dc1)

    tile = pl.BlockSpec((tk, LANES), lambda p, i: (i, p))
    whole = pl.BlockSpec((S, LANES), lambda p, i: (0, p))
    row_spec = pl.BlockSpec((1, nk, 2, tk), lambda p, i: (p, 0, 0, 0))
    return pl.pallas_call(
        body, name=name, grid=(P, nk),
        in_specs=[whole, whole, whole, tile, tile, tile, row_spec, row_spec],
        out_specs=[whole, tile, tile, pl.BlockSpec((1, tk, 2), lambda p, i: (p, i, 0))],
        out_shape=[jax.ShapeDtypeStruct((S, D), F32), jax.ShapeDtypeStruct((S, D), F32),
                   jax.ShapeDtypeStruct((S, D), F32), jax.ShapeDtypeStruct((P, S, 2), F32)],
        scratch_shapes=[pltpu.VMEM((2, 2, tk, tk), F32), pltpu.VMEM((2, 2, tk, tk), F32)],
        compiler_params=_params("parallel", "arbitrary"),
    )(*qa, g, *ka, vb, m_row, e_row)


def _loss_head(y, t, name):
    S, D = y.shape
    ts = _tile(S, 512, SUBLANES)

    def body(y_ref, t_ref, dy_ref, l_ref):
        @pl.when(pl.program_id(0) == 0)
        def _():
            l_ref[...] = jnp.zeros_like(l_ref)

        e = y_ref[...] - t_ref[...]
        dy_ref[...] = e * (1.0 / D)
        part = 0.5 * jnp.sum(jnp.mean(e * e, axis=-1, keepdims=True), axis=0, keepdims=True)
        l_ref[...] += jnp.broadcast_to(part, l_ref.shape)

    return pl.pallas_call(
        body, name=name, grid=(S // ts,),
        in_specs=[pl.BlockSpec((ts, D), lambda s: (s, 0)), pl.BlockSpec((ts, D), lambda s: (s, 0))],
        out_specs=[pl.BlockSpec((ts, D), lambda s: (s, 0)), pl.BlockSpec((SUBLANES, LANES), lambda s: (0, 0))],
        out_shape=[jax.ShapeDtypeStruct((S, D), F32), jax.ShapeDtypeStruct((SUBLANES, LANES), F32)],
        compiler_params=_params("arbitrary"),
    )(y, t)


def _adamw(w, g, m, v, name):
    shape = w.shape
    cols = shape[-1]
    as2d = lambda a: a.reshape(-1, cols)
    rows = as2d(w).shape[0]
    tr = _tile(rows, 256, SUBLANES) if rows % SUBLANES == 0 else rows
    c1 = 1.0 - ADAM_B1 ** ADAM_STEP
    c2 = 1.0 - ADAM_B2 ** ADAM_STEP

    def body(w_ref, g_ref, m_ref, v_ref, d_ref, nm_ref, nv_ref):
        gg = g_ref[...]
        nm = ADAM_B1 * m_ref[...] + (1.0 - ADAM_B1) * gg
        nv = ADAM_B2 * v_ref[...] + (1.0 - ADAM_B2) * (gg * gg)
        d_ref[...] = -ADAM_LR * ((nm / c1) / (jnp.sqrt(nv / c2) + ADAM_EPS) + ADAM_WD * w_ref[...])
        nm_ref[...] = nm
        nv_ref[...] = nv

    spec = pl.BlockSpec((tr, cols), lambda r: (r, 0))
    outs = pl.pallas_call(
        body, name=name, grid=(rows // tr,), in_specs=[spec] * 4, out_specs=[spec] * 3,
        out_shape=[jax.ShapeDtypeStruct((rows, cols), F32)] * 3,
        compiler_params=_params("parallel"),
    )(as2d(w), as2d(g), as2d(m), as2d(v))
    return tuple(o.reshape(shape) for o in outs)


def _place():
    return lax.axis_index("x"), lax.axis_index("y"), lax.axis_index("c")


def _other_chips(x, y):
    return [(1 - x, y), (x, 1 - y), (1 - x, 1 - y)]


def _remote(src, dst, send_sems, recv_sems, k, to):
    return pltpu.make_async_remote_copy(src_ref=src, dst_ref=dst, send_sem=send_sems.at[k], recv_sem=recv_sems.at[k],
                                        device_id=to, device_id_type=MESH)


def _half(c, rh):
    return pl.ds(pl.multiple_of(c * rh, 2 * SUBLANES), rh)


def _gather_phases(rows, src, dst, send_sems, recv_sems):
    n = len(rows)
    x, y, c = _place()
    me = 2 * x + y
    sib = (x, y, 1 - c)
    chips = _other_chips(x, y)
    rh = [r // 2 for r in rows]

    def first():
        return [_remote(src[g].at[_half(c, rh[g])], dst[g].at[me, _half(c, rh[g])], send_sems, recv_sems,
                        6 * g + j, (cx, cy, c)) for j, (cx, cy) in enumerate(chips) for g in range(n)]

    def landed(j, g, core):
        cx, cy = chips[j]
        return dst[g].at[2 * cx + cy, _half(core, rh[g])]

    def passed():
        return [_remote(landed(j, g, c), landed(j, g, c), send_sems, recv_sems, 6 * g + 3 + j, sib)
                for j in range(3) for g in range(n)]

    def start():
        for cp in first():
            cp.start()

    def pass_on():
        cps = passed()
        for j in range(3):
            for g in range(n):
                _remote(landed(j, g, c), landed(j, g, c), send_sems, recv_sems, 6 * g + j, sib).wait_recv()
                cps[j * n + g].start()

    def finish():
        for j in range(3):
            for g in range(n):
                _remote(landed(j, g, 1 - c), landed(j, g, 1 - c), send_sems, recv_sems, 6 * g + 3 + j, sib).wait_recv()
        for cp in first() + passed():
            cp.wait_send()

    return start, pass_on, finish


def _fill_own(outs, srcs):
    x, y, _ = _place()
    return [lax.dynamic_update_slice_in_dim(o, a[None], 2 * x + y, axis=0) for o, a in zip(outs, srcs)]


def _allgather_weights(srcs):
    n = len(srcs)

    def body(*refs):
        for step in _gather_phases([a.shape[0] for a in srcs], refs[:n], refs[n:2 * n], *refs[2 * n:]):
            step()

    outs = pl.pallas_call(
        body, name="allgather_weights", in_specs=[ANY] * n, out_specs=[ANY] * n,
        out_shape=[jax.ShapeDtypeStruct((N_CHIPS,) + a.shape, a.dtype) for a in srcs],
        scratch_shapes=[pltpu.SemaphoreType.DMA((6 * n,)), pltpu.SemaphoreType.DMA((6 * n,))],
    )(*srcs)
    return _fill_own(outs, srcs)


def _pair_exchange(gs):
    n = len(gs)

    def body(*refs):
        g_refs, t_refs, (send_sems, recv_sems) = refs[:n], refs[n:2 * n], refs[2 * n:]
        x, y, c = _place()
        cps = [_remote(g_refs[g].at[k, 1 - c], t_refs[g].at[k], send_sems, recv_sems, N_CHIPS * g + k, (x, y, 1 - c))
               for g in range(n) for k in range(N_CHIPS)]
        for cp in cps:
            cp.start()
        for cp in cps:
            cp.wait()

    return pl.pallas_call(
        body, name="grad_pair_exchange", in_specs=[ANY] * n, out_specs=[ANY] * n,
        out_shape=[jax.ShapeDtypeStruct((a.shape[0],) + a.shape[2:], a.dtype) for a in gs],
        scratch_shapes=[pltpu.SemaphoreType.DMA((N_CHIPS * n,)), pltpu.SemaphoreType.DMA((N_CHIPS * n,))],
    )(*gs)


def _pair_add(g, t, c, name):
    n, _, rh, W = g.shape
    tr = _tile(rh, 256, 2 * SUBLANES)

    def body(c_ref, g_ref, t_ref, o_ref):
        o_ref[...] = (g_ref[0] + t_ref[...]).astype(BF)

    return pl.pallas_call(
        body, name=name,
        grid_spec=pltpu.PrefetchScalarGridSpec(
            num_scalar_prefetch=1, grid=(n, rh // tr),
            in_specs=[pl.BlockSpec((1, 1, tr, W), lambda k, i, c_ref: (k, c_ref[0], i, 0)),
                      pl.BlockSpec((1, tr, W), lambda k, i, c_ref: (k, i, 0))],
            out_specs=pl.BlockSpec((1, tr, W), lambda k, i, c_ref: (k, i, 0))),
        out_shape=jax.ShapeDtypeStruct((n, rh, W), BF),
        compiler_params=_params("parallel", "parallel"),
    )(c.reshape(1).astype(jnp.int32), g, t)


def _chip_exchange(parts):
    n = len(parts)

    def body(*refs):
        a_refs, t_refs, (send_sems, recv_sems) = refs[:n], refs[n:2 * n], refs[2 * n:]
        x, y, c = _place()
        cps = [_remote(a_refs[g].at[2 * cx + cy], t_refs[g].at[j], send_sems, recv_sems, 3 * g + j, (cx, cy, c))
               for j, (cx, cy) in enumerate(_other_chips(x, y)) for g in range(n)]
        for cp in cps:
            cp.start()
        for cp in cps:
            cp.wait()

    return pl.pallas_call(
        body, name="grad_chip_exchange", in_specs=[ANY] * n, out_specs=[ANY] * n,
        out_shape=[jax.ShapeDtypeStruct((3,) + a.shape[1:], a.dtype) for a in parts],
        scratch_shapes=[pltpu.SemaphoreType.DMA((3 * n,)), pltpu.SemaphoreType.DMA((3 * n,))],
    )(*parts)


def _chip_add(g, t1, t2, c, me, name):
    _, _, rh, W = g.shape
    tr = _tile(rh, 256, 2 * SUBLANES)

    def body(c_ref, me_ref, g_ref, t1_ref, t2_ref, o_ref):
        own = g_ref[0, 0] + t1_ref[0]
        o_ref[...] = own + t2_ref[0].astype(F32) + t2_ref[1].astype(F32) + t2_ref[2].astype(F32)

    return pl.pallas_call(
        body, name=name,
        grid_spec=pltpu.PrefetchScalarGridSpec(
            num_scalar_prefetch=2, grid=(rh // tr,),
            in_specs=[pl.BlockSpec((1, 1, tr, W), lambda i, c_ref, me_ref: (me_ref[0], c_ref[0], i, 0)),
                      pl.BlockSpec((1, tr, W), lambda i, c_ref, me_ref: (me_ref[0], i, 0)),
                      pl.BlockSpec((3, tr, W), lambda i, c_ref, me_ref: (0, i, 0))],
            out_specs=pl.BlockSpec((tr, W), lambda i, c_ref, me_ref: (i, 0))),
        out_shape=jax.ShapeDtypeStruct((rh, W), F32),
        compiler_params=_params("parallel"),
    )(c.reshape(1).astype(jnp.int32), me.reshape(1).astype(jnp.int32), g, t1, t2)


def _pair_share(hs):
    n = len(hs)

    def body(*refs):
        h_refs, f_refs, (send_sems, recv_sems) = refs[:n], refs[n:2 * n], refs[2 * n:]
        x, y, c = _place()
        cps = [_remote(h_refs[g], f_refs[g], send_sems, recv_sems, g, (x, y, 1 - c)) for g in range(n)]
        for cp in cps:
            cp.start()
        for cp in cps:
            cp.wait()

    return pl.pallas_call(
        body, name="grad_pair_share", in_specs=[ANY] * n, out_specs=[ANY] * n,
        out_shape=[jax.ShapeDtypeStruct(a.shape, a.dtype) for a in hs],
        scratch_shapes=[pltpu.SemaphoreType.DMA((n,)), pltpu.SemaphoreType.DMA((n,))],
    )(*hs)


def _allreduce_small(pack, name):
    rows, W = pack.shape

    def body(p_ref, o_ref, buf, send_sems, recv_sems):
        x, y, c = _place()
        me = 4 * x + 2 * y + c
        buf[me] = p_ref[...]
        cps = []
        for r in range(1, 8):
            fx, fy, fc = (r >> 2) & 1, (r >> 1) & 1, r & 1
            to = (1 - x if fx else x, 1 - y if fy else y, 1 - c if fc else c)
            cps.append(_remote(p_ref, buf.at[me], send_sems, recv_sems, r - 1, to))
        for cp in cps:
            cp.start()
        for r in range(1, 8):
            fx, fy, fc = (r >> 2) & 1, (r >> 1) & 1, r & 1
            frm = 4 * (1 - x if fx else x) + 2 * (1 - y if fy else y) + (1 - c if fc else c)
            _remote(p_ref, buf.at[frm], send_sems, recv_sems, r - 1, (x, y, c)).wait_recv()
        for cp in cps:
            cp.wait_send()
        acc = buf[0]
        for i in range(1, 8):
            acc = acc + buf[i]
        o_ref[...] = acc

    return pl.pallas_call(
        body, name=name, in_specs=[VMEM], out_specs=VMEM,
        out_shape=jax.ShapeDtypeStruct((rows, W), F32),
        scratch_shapes=[pltpu.VMEM((8, rows, W), F32), pltpu.SemaphoreType.DMA((7,)), pltpu.SemaphoreType.DMA((7,))],
    )(pack)


def _width_groups(arrs):
    widths = []
    for a in arrs:
        if a.shape[-1] not in widths:
            widths.append(a.shape[-1])
    return [[i for i, a in enumerate(arrs) if a.shape[-1] == w] for w in widths]


def _rows2d(a):
    return a.reshape(-1, a.shape[-1])


def _split_rows_like(buf, like, lead=()):
    out, off = [], 0
    for a in like:
        n = a.size // a.shape[-1]
        out.append(buf[..., off:off + n, :].reshape(tuple(lead) + a.shape))
        off += n
    return out


def _join_cols(g):
    nd = g.ndim
    return jnp.moveaxis(g, 0, nd - 2).reshape(g.shape[1:-1] + (N_CHIPS * g.shape[-1],))


def _join_rows(g):
    return jnp.moveaxis(g, 0, 1).reshape(g.shape[1], N_CHIPS * g.shape[2], g.shape[3])


def _row_layout(a, tq):
    P, S, _ = a.shape
    return a.reshape(P, S // tq, tq, 2).transpose(0, 1, 3, 2)


def _pad_row(v, width=FLAT_W):
    flat = v.reshape(-1)
    rows = -(-flat.shape[0] // width)
    return jnp.pad(flat, (0, rows * width - flat.shape[0]))


def kernel(x, attn_norm, ffn_norm, a_w_in, a_conv, a_w_out, kv_norm, w_kvf, b_f, k_norm, b_w_qg, q_norm, b_w_out, ffn_w_up, ffn_conv, ffn_w_down, loss_target, m_attn_norm, m_ffn_norm, m_a_w_in, m_a_conv, m_a_w_out, m_kv_norm, m_w_kvf, m_b_f, m_k_norm, m_b_w_qg, m_q_norm, m_b_w_out, m_ffn_w_up, m_ffn_conv, m_ffn_w_down, v_attn_norm, v_ffn_norm, v_a_w_in, v_a_conv, v_a_w_out, v_kv_norm, v_w_kvf, v_b_f, v_k_norm, v_b_w_qg, v_q_norm, v_b_w_out, v_ffn_w_up, v_ffn_conv, v_ffn_w_down):
    xs = x[0]
    S, D = xs.shape
    H, hd = b_f.shape[0], k_norm.shape[0]
    depth = attn_norm.shape[0]
    n_a = a_w_in.shape[0]
    P = D // LANES
    assert LANES == 2 * hd and H * hd == D, "the attention kernels hold two heads per lane tile"
    mx, my, mc = _place()
    chip = 2 * mx + my

    big = [a_w_in, a_w_out, w_kvf, b_w_qg, b_w_out, ffn_w_up, ffn_w_down]
    groups = _width_groups(big)
    assert n_a >= 2 and depth - n_a >= 2, "the hosted weight gathers are laid out for two layers of each kind"
    first = [a_w_in[:1], a_w_out[:1], ffn_w_up[:1], ffn_w_down[:1]]
    second = [a_w_in[1:], a_w_out[1:], ffn_w_up[1:n_a], ffn_w_down[1:n_a]]
    third = [w_kvf, b_w_qg[:1]]
    late = [b_w_qg[1:], b_w_out, ffn_w_up[n_a:], ffn_w_down[n_a:]]

    def packed(ws):
        idx_groups = _width_groups(ws)
        return [jnp.concatenate([_rows2d(ws[i]).astype(BF) for i in idx]) for idx in idx_groups], idx_groups

    def unpacked(bufs, idx_groups, ws):
        out = [None] * len(ws)
        for idx, buf in zip(idx_groups, bufs):
            for i, part in zip(idx, _split_rows_like(buf, [ws[i] for i in idx], (N_CHIPS,))):
                out[i] = part
        return out

    first_src, first_groups = packed(first)
    second_src, second_groups = packed(second)
    third_src, third_groups = packed(third)
    late_src, late_groups = packed(late)
    g_in, g_out, g_up, g_down = unpacked(_allgather_weights(first_src), first_groups, first)
    wa_in, wa_out = list(_join_cols(g_in)), list(_join_rows(g_out))
    w_up, w_down, wb_qg, wb_out = list(_join_cols(g_up)), list(_join_rows(g_down)), [], []
    behind_ffn_up = {0: second_src, 1: third_src}
    kvf_cols = 2 * D + LANES

    def placed(shard):
        full = jnp.zeros(shard.shape[:-1] + (N_CHIPS, shard.shape[-1]), F32)
        full = lax.dynamic_update_slice_in_dim(full, shard[..., None, :], chip, axis=full.ndim - 2)
        return jnp.where(mc == 0, full, 0.0).reshape(-1)

    conv_pack = jnp.concatenate([_pad_row(placed(a_conv)), _pad_row(placed(ffn_conv))]).reshape(-1, FLAT_W)
    conv_full = _allreduce_small(conv_pack, "allgather_conv_taps").reshape(-1)
    n_ac = a_conv.size * N_CHIPS
    a_conv_f = conv_full[:n_ac].reshape(a_conv.shape[:-1] + (-1,))
    off = _pad_row(placed(a_conv)).shape[0]
    ffn_conv_f = conv_full[off:off + ffn_conv.size * N_CHIPS].reshape(ffn_conv.shape[:-1] + (-1,))
    F = ffn_conv_f.shape[-1]

    b_pad = jnp.pad(b_f, (0, LANES - H)).reshape(1, LANES)
    gate_blk = 2 * D // LANES
    tq = _attn_tile(S)
    scale = hd ** -0.5

    saved = []
    cur = xs
    kv = None
    for l in range(depth):
        rec = {"x_in": cur}
        if l < n_a:
            proj, xn, z = _mixer_in_fwd(cur, attn_norm[l], wa_in[l], a_conv_f[l], f"a_in_{l}")
            mid = _matmul_residual(z, wa_out[l], cur, f"a_out_{l}")
            rec.update(proj=proj, xn=xn, z=z)
        else:
            j = l - n_a
            if kv is None:
                kvf, hn = _norm_matmul(cur, kv_norm, wkvf, 1, F32, "kvf_proj")
                vb = kvf[0, :, D:2 * D].astype(BF)
                cgate = _gate_fwd(kvf, b_pad, gate_blk, "gate_cumsum")
                kv = dict(kvf=kvf, hn=hn, vb=vb, cgate=cgate, x_in=cur, dk=[], dv=[], dc=[],
                          ka=_augment(kvf, 0, 0, cgate, "k", hd, D, "k_augment", norm_w=k_norm),
                          va=_augment(kvf, 0, 1, cgate, "v", hd, D, "v_augment"))
            qg, xn = _norm_matmul(cur, attn_norm[l], wb_qg[j], 2, F32, f"qg_proj_{j}")
            qa = _augment(qg, 0, 0, kv["cgate"], "q", hd, D, f"q_augment_{j}", norm_w=q_norm[j], scale=scale * LOG2E)
            o, og, m_max, l_sum, late_bufs = _attn_fwd(qa, kv["ka"], kv["va"], qg, hd, f"attn_fwd_{j}",
                                                       gather=late_src if j == 0 else ())
            if j == 0:
                g_qg, g_bout, g_up, g_down = unpacked(late_bufs, late_groups, late)
                wb_qg += list(_join_cols(g_qg))
                wb_out += list(_join_rows(g_bout))
                w_up += list(_join_cols(g_up))
                w_down += list(_join_rows(g_down))
            mid = _matmul_residual(og, wb_out[j], cur, f"b_out_{j}")
            rec.update(qg=qg, xn=xn, qa=qa, o=o, og=og, m=m_max, l=l_sum)
        up, xn2, z2, bufs = _ffn_up_fwd(mid, ffn_norm[l], w_up[l], ffn_conv_f[l], f"ffn_up_{l}",
                                        gather=behind_ffn_up.get(l, ()))
        if l == 0:
            g_in, g_out, g_up, g_down = unpacked(bufs, second_groups, second)
            wa_in += list(_join_cols(g_in))
            wa_out += list(_join_rows(g_out))
            w_up += list(_join_cols(g_up))
            w_down += list(_join_rows(g_down))
        if l == 1:
            g_kvf, g_qg = unpacked(bufs, third_groups, third)
            wkvf = jnp.pad(_join_cols(g_kvf), ((0, 0), (0, kvf_cols - (2 * D + H))))
            wb_qg += list(_join_cols(g_qg))
        cur = _matmul_residual(z2, w_down[l], mid, f"ffn_down_{l}")
        rec.update(x_mid=mid, up=up, xn2=xn2, z2=z2)
        saved.append(rec)

    dy, loss_part = _loss_head(cur, loss_target[0], "loss_head")

    g_attn_norm, g_ffn_norm = [None] * depth, [None] * depth
    g_a_in, g_a_conv, g_a_out = [None] * n_a, [None] * n_a, [None] * n_a
    g_qg, g_qn, g_bo = [None] * (depth - n_a), [None] * (depth - n_a), [None] * (depth - n_a)
    g_up, g_fc, g_down = [None] * depth, [None] * depth, [None] * depth
    for l in reversed(range(depth)):
        rec = saved[l]
        dup, g_fc[l] = _ffn_mid_bwd(rec["up"], dy, w_down[l], ffn_conv_f[l], f"ffn_mid_bwd_{l}")
        g_down[l] = _wgrad(rec["z2"], dy[None], f"ffn_down_wgrad_{l}")
        g_up[l] = _wgrad(rec["xn2"], dup, f"ffn_up_wgrad_{l}")
        dy, g_ffn_norm[l] = _dnorm(dup, w_up[l], rec["x_mid"], ffn_norm[l], dy, f"ffn_up_bwd_{l}")
        if l < n_a:
            dproj, g_a_conv[l] = _mixer_mid_bwd(rec["proj"], dy, wa_out[l], a_conv_f[l], f"a_mid_bwd_{l}")
            g_a_out[l] = _wgrad(rec["z"], dy[None], f"a_out_wgrad_{l}")
            g_a_in[l] = _wgrad(rec["xn"], dproj, f"a_in_wgrad_{l}")
            dy, g_attn_norm[l] = _dnorm(dproj, wa_in[l], rec["x_in"], attn_norm[l], dy, f"a_in_bwd_{l}")
        else:
            j = l - n_a
            g_out, dgate, evec = _attn_out_bwd(dy, wb_out[j], rec["o"], rec["qg"], rec["l"], hd,
                                               f"attn_gate_bwd_{j}")
            g_bo[j] = _wgrad(rec["og"], dy[None], f"b_out_wgrad_{j}")
            dqn, dk, dv, dc = _attn_bwd(rec["qa"], kv["ka"], kv["vb"], g_out, _row_layout(rec["m"], tq),
                                        _row_layout(evec, tq), hd, f"attn_bwd_{j}")
            kv["dk"].append(dk)
            kv["dv"].append(dv)
            kv["dc"].append(dc)
            dq_pre, g_qn[j] = _headnorm_bwd(rec["qg"], 0, 0, q_norm[j], [dqn], D, f"q_norm_bwd_{j}")
            dqg = jnp.stack([dq_pre, dgate])
            g_qg[j] = _wgrad(rec["xn"], dqg, f"qg_wgrad_{j}")
            dy, g_attn_norm[l] = _dnorm(dqg, wb_qg[j], rec["x_in"], attn_norm[l], dy, f"qg_bwd_{j}")
            if l == n_a:
                dk_s, g_k_norm = _headnorm_bwd(kv["kvf"], 0, 0, k_norm, kv["dk"], D, "k_norm_bwd")
                dv_s = functools.reduce(jnp.add, kv["dv"]).astype(BF)
                dc_sum = functools.reduce(jnp.add, kv["dc"])
                dc_pad = jnp.pad(dc_sum.transpose(1, 0, 2).reshape(S, H), ((0, 0), (0, LANES - H)))
                df, db = _gate_bwd(dc_pad, kv["kvf"], b_pad, gate_blk, "gate_bwd")
                dkvf = jnp.concatenate([dk_s, dv_s, df.astype(BF)], axis=1)[None]
                g_kvf = _wgrad(kv["hn"], dkvf, "kvf_wgrad")[:, :2 * D + H]
                g_b_f = db[:H]
                dy, g_kv_norm = _dnorm(dkvf, wkvf, kv["x_in"], kv_norm, dy, "kvf_bwd")
    grad_x = dy[None]

    def cols_of(g, k):
        c = g.shape[-1] // N_CHIPS
        return g[:, k * c:(k + 1) * c]

    def rows_of(g, k):
        r = g.shape[0] // N_CHIPS
        return g[k * r:(k + 1) * r]

    per_layer = [g_a_in, g_a_out, [g_kvf], g_qg, g_bo, g_up, g_down]
    of_chip = [cols_of, rows_of, cols_of, cols_of, rows_of, cols_of, rows_of]

    def group_buffer(idx):
        rows = [of_chip[i](g, k) for k in range(N_CHIPS) for i in idx for g in per_layer[i]]
        buf = jnp.concatenate(rows)
        return buf.reshape(N_CHIPS, 2, buf.shape[0] // (2 * N_CHIPS), buf.shape[1])

    g4 = [group_buffer(idx) for idx in groups]
    from_sibling = _pair_exchange(g4)
    from_chips = _chip_exchange([_pair_add(g, t, mc, f"grad_pair_add_{n}") for n, (g, t) in
                                 enumerate(zip(g4, from_sibling))])
    mine = [_chip_add(g, t1, t2, mc, chip, f"grad_chip_add_{n}") for n, (g, t1, t2) in
            enumerate(zip(g4, from_sibling, from_chips))]
    theirs = _pair_share(mine)
    big_grads = [None] * len(big)
    for idx, m_half, t_half in zip(groups, mine, theirs):
        shard = jnp.where(mc == 0, jnp.concatenate([m_half, t_half]), jnp.concatenate([t_half, m_half]))
        for i, part in zip(idx, _split_rows_like(shard, [big[i] for i in idx])):
            big_grads[i] = part

    small = [loss_part[0, :1], jnp.stack(g_attn_norm), jnp.stack(g_ffn_norm), g_kv_norm, g_b_f, g_k_norm,
             jnp.stack(g_qn), jnp.stack(g_a_conv), jnp.stack(g_fc)]
    small_sum = _allreduce_small(jnp.concatenate([_pad_row(s) for s in small]).reshape(-1, FLAT_W),
                                 "allreduce_small_grads").reshape(-1)
    parts, off = [], 0
    for s in small:
        parts.append(small_sum[off:off + s.size].reshape(s.shape))
        off += _pad_row(s).shape[0]
    loss = parts[0][0]
    gr_attn_norm, gr_ffn_norm, gr_kv_norm, gr_b_f, gr_k_norm, gr_q_norm, gr_a_conv_full, gr_ffn_conv_full = parts[1:]

    def my_cols(full):
        c = full.shape[-1] // N_CHIPS
        return lax.dynamic_slice_in_dim(full, chip * c, c, axis=full.ndim - 1)

    gr_a_in, gr_a_out, gr_kvf, gr_qg, gr_bo, gr_up, gr_down = big_grads
    grads = [gr_attn_norm, gr_ffn_norm, gr_a_in, my_cols(gr_a_conv_full), gr_a_out, gr_kv_norm, gr_kvf, gr_b_f,
             gr_k_norm, gr_qg, gr_q_norm, gr_bo, gr_up, my_cols(gr_ffn_conv_full), gr_down]
    weights = [attn_norm, ffn_norm, a_w_in, a_conv, a_w_out, kv_norm, w_kvf, b_f, k_norm, b_w_qg, q_norm, b_w_out,
               ffn_w_up, ffn_conv, ffn_w_down]
    ms = [m_attn_norm, m_ffn_norm, m_a_w_in, m_a_conv, m_a_w_out, m_kv_norm, m_w_kvf, m_b_f, m_k_norm, m_b_w_qg,
          m_q_norm, m_b_w_out, m_ffn_w_up, m_ffn_conv, m_ffn_w_down]
    vs = [v_attn_norm, v_ffn_norm, v_a_w_in, v_a_conv, v_a_w_out, v_kv_norm, v_w_kvf, v_b_f, v_k_norm, v_b_w_qg,
          v_q_norm, v_b_w_out, v_ffn_w_up, v_ffn_conv, v_ffn_w_down]
    deltas, new_ms, new_vs = [], [], []
    for i, (w, g, m, v) in enumerate(zip(weights, grads, ms, vs)):
        d, nm, nv = _adamw(w, g, m, v, f"adamw_{i}")
        deltas.append(d)
        new_ms.append(nm)
        new_vs.append(nv)
    return (loss, grad_x, *grads, *deltas, *new_ms, *new_vs)
```

```python
import functools

import jax
import jax.numpy as jnp
from jax import lax
from jax.experimental import pallas as pl
from jax.experimental.pallas import tpu as pltpu

F32 = jnp.float32
BF = jnp.bfloat16
LANES = 128
SUBLANES = 8
RMS_EPS = 1e-6
LOG2E = 1.4426950408889634
FLAT_W = 1024
N_CHIPS = 4
CONV_W = 3
HALO = SUBLANES

ADAM_LR = 0.001
ADAM_B1 = 0.9
ADAM_B2 = 0.999
ADAM_EPS = 1e-08
ADAM_WD = 0.01
ADAM_STEP = 10

MESH = pl.DeviceIdType.MESH
ANY = pl.BlockSpec(memory_space=pl.ANY)
VMEM = pl.BlockSpec(memory_space=pltpu.VMEM)
NT_DIMS = (((1,), (1,)), ((), ()))
TN_DIMS = (((0,), (0,)), ((), ()))


def _tile(n, pref, mult=LANES):
    t = (min(pref, n) // mult) * mult
    while t >= mult:
        if n % t == 0:
            break
        t -= mult
    if t < mult or (t * 4 < pref and n <= 4 * pref):
        return n
    return t


def _params(*sem):
    return pltpu.CompilerParams(dimension_semantics=sem)


def _norm_matmul(x, g, w, parts, out_dtype, name):
    S, D = x.shape
    C = w.shape[1] // parts
    ts, tn = _tile(S, 512, SUBLANES), _tile(C, 1408)
    npc = C // tn

    def body(x_ref, g_ref, w_ref, o_ref, xn_ref):
        @pl.when(pl.program_id(1) == 0)
        def _():
            xf = x_ref[...]
            r = lax.rsqrt(jnp.mean(xf * xf, axis=-1, keepdims=True) + RMS_EPS)
            xn_ref[...] = (xf * r * g_ref[...]).astype(BF)

        o_ref[0] = jnp.dot(xn_ref[...], w_ref[...], preferred_element_type=F32).astype(out_dtype)

    return pl.pallas_call(
        body, name=name, grid=(S // ts, parts * npc),
        in_specs=[pl.BlockSpec((ts, D), lambda s, n: (s, 0)),
                  pl.BlockSpec((1, D), lambda s, n: (0, 0)),
                  pl.BlockSpec((D, tn), lambda s, n: (0, n))],
        out_specs=[pl.BlockSpec((1, ts, tn), lambda s, n: (n // npc, s, n % npc)),
                   pl.BlockSpec((ts, D), lambda s, n: (s, 0))],
        out_shape=[jax.ShapeDtypeStruct((parts, S, C), out_dtype), jax.ShapeDtypeStruct((S, D), BF)],
        compiler_params=_params("parallel", "arbitrary"),
    )(x, g.reshape(1, D), w)


def _shift_down(u, prev, k):
    r = pltpu.roll(u, k, 0)
    row = lax.broadcasted_iota(jnp.int32, (HALO, u.shape[1]), 0)
    head = r[0:HALO]
    for j in range(k):
        head = jnp.where(row == j, prev[HALO - k + j:HALO - k + j + 1, :], head)
    return jnp.concatenate([head, r[HALO:]], axis=0)


def _shift_up(d, nxt, k):
    n = d.shape[0]
    r = pltpu.roll(d, n - k, 0)
    row = lax.broadcasted_iota(jnp.int32, (HALO, d.shape[1]), 0)
    tail = r[n - HALO:n]
    for j in range(k):
        tail = jnp.where(row == HALO - k + j, nxt[j:j + 1, :], tail)
    return jnp.concatenate([r[0:n - HALO], tail], axis=0)


def _conv3(u, prev, w):
    return _shift_down(u, prev, 2) * w[0:1] + _shift_down(u, prev, 1) * w[1:2] + u * w[2:3]


def _conv3_t(d, nxt, w):
    return d * w[2:3] + _shift_up(d, nxt, 1) * w[1:2] + _shift_up(d, nxt, 2) * w[0:1]


def _tap_rows(t0, t1, t2):
    row = lax.broadcasted_iota(jnp.int32, (SUBLANES, t0.shape[1]), 0)
    return jnp.where(row == 0, t0, jnp.where(row == 1, t1, jnp.where(row == 2, t2, 0.0)))


def _pad_conv(cw):
    return jnp.pad(cw, ((0, SUBLANES - CONV_W), (0, 0)))


def _mixer_in_fwd(x, g, w, cw, name, gather=()):
    S, D = x.shape
    C = w.shape[1] // 3
    ts, tc = _tile(S, 512, SUBLANES), _tile(C, 1024)
    nc = C // tc
    n_s = S // ts
    n_g = len(gather)

    def body(x_ref, g_ref, wb_ref, wc_ref, wh_ref, cw_ref, *rest):
        p_ref, xn_ref, z_ref = rest[n_g:n_g + 3]
        carry = rest[2 * n_g + 3]
        s, c = pl.program_id(0), pl.program_id(1)
        if n_g:
            start, pass_on, finish = _gather_phases([a.shape[0] for a in gather], rest[:n_g],
                                                    rest[n_g + 3:2 * n_g + 3], *rest[2 * n_g + 4:])
            pl.when((s == 0) & (c == 0))(start)
            pl.when((s == (3 * n_s) // 4) & (c == 0))(pass_on)

        @pl.when(c == 0)
        def _():
            xf = x_ref[...]
            r = lax.rsqrt(jnp.mean(xf * xf, axis=-1, keepdims=True) + RMS_EPS)
            xn_ref[...] = (xf * r * g_ref[...]).astype(BF)

        @pl.when(s == 0)
        def _():
            carry[c] = jnp.zeros((HALO, tc), F32)

        xn = xn_ref[...]
        parts = [jnp.dot(xn, w_ref[...], preferred_element_type=F32).astype(BF) for w_ref in (wb_ref, wc_ref, wh_ref)]
        for p, v in enumerate(parts):
            p_ref[p] = v
        u = parts[1].astype(F32) * parts[2].astype(F32)
        cv = _conv3(u, carry[c], cw_ref[...])
        z_ref[...] = (parts[0].astype(F32) * cv).astype(BF)
        carry[c] = u[ts - HALO:ts, :]
        if n_g:
            pl.when((s == n_s - 1) & (c == nc - 1))(finish)

    wspec = lambda p: pl.BlockSpec((D, tc), lambda s, c: (0, p * nc + c))
    sems = [pltpu.SemaphoreType.DMA((6 * n_g,)), pltpu.SemaphoreType.DMA((6 * n_g,))] if n_g else []
    outs = pl.pallas_call(
        body, name=name, grid=(n_s, nc),
        in_specs=[pl.BlockSpec((ts, D), lambda s, c: (s, 0)), pl.BlockSpec((1, D), lambda s, c: (0, 0)),
                  wspec(0), wspec(1), wspec(2), pl.BlockSpec((SUBLANES, tc), lambda s, c: (0, c))] + [ANY] * n_g,
        out_specs=[pl.BlockSpec((3, ts, tc), lambda s, c: (0, s, c)),
                   pl.BlockSpec((ts, D), lambda s, c: (s, 0)),
                   pl.BlockSpec((ts, tc), lambda s, c: (s, c))] + [ANY] * n_g,
        out_shape=[jax.ShapeDtypeStruct((3, S, C), BF), jax.ShapeDtypeStruct((S, D), BF),
                   jax.ShapeDtypeStruct((S, C), BF)]
        + [jax.ShapeDtypeStruct((N_CHIPS,) + a.shape, a.dtype) for a in gather],
        scratch_shapes=[pltpu.VMEM((nc, HALO, tc), F32)] + sems,
        compiler_params=_params("arbitrary", "arbitrary"),
    )(x, g.reshape(1, D), w, w, w, _pad_conv(cw), *gather)
    return tuple(outs[:3]) + (_fill_own(outs[3:], gather),)


def _ffn_up_fwd(x, g, w, cw, name, gather=()):
    S, D = x.shape
    C = w.shape[1] // 2
    ts, tc = _tile(S, 512, SUBLANES), _tile(C, 1408)
    nc = C // tc
    n_s = S // ts
    n_g = len(gather)

    def body(x_ref, g_ref, wa_ref, wg_ref, cw_ref, *rest):
        up_ref, xn_ref, z_ref = rest[n_g:n_g + 3]
        carry = rest[2 * n_g + 3]
        s, c = pl.program_id(0), pl.program_id(1)
        if n_g:
            start, pass_on, finish = _gather_phases([a.shape[0] for a in gather], rest[:n_g],
                                                    rest[n_g + 3:2 * n_g + 3], *rest[2 * n_g + 4:])
            pl.when((s == 0) & (c == 0))(start)
            pl.when((s == (3 * n_s) // 4) & (c == 0))(pass_on)

        @pl.when(c == 0)
        def _():
            xf = x_ref[...]
            r = lax.rsqrt(jnp.mean(xf * xf, axis=-1, keepdims=True) + RMS_EPS)
            xn_ref[...] = (xf * r * g_ref[...]).astype(BF)

        @pl.when(s == 0)
        def _():
            carry[c] = jnp.zeros((HALO, tc), F32)

        xn = xn_ref[...]
        a_b = jnp.dot(xn, wa_ref[...], preferred_element_type=F32).astype(BF)
        g_b = jnp.dot(xn, wg_ref[...], preferred_element_type=F32).astype(BF)
        up_ref[0] = a_b
        up_ref[1] = g_b
        a_pre = a_b.astype(F32)
        a = _conv3(a_pre, carry[c], cw_ref[...])
        z_ref[...] = (a * jax.nn.sigmoid(a) * g_b.astype(F32)).astype(BF)
        carry[c] = a_pre[ts - HALO:ts, :]
        if n_g:
            pl.when((s == n_s - 1) & (c == nc - 1))(finish)

    sems = [pltpu.SemaphoreType.DMA((6 * n_g,)), pltpu.SemaphoreType.DMA((6 * n_g,))] if n_g else []
    outs = pl.pallas_call(
        body, name=name, grid=(n_s, nc),
        in_specs=[pl.BlockSpec((ts, D), lambda s, c: (s, 0)),
                  pl.BlockSpec((1, D), lambda s, c: (0, 0)),
                  pl.BlockSpec((D, tc), lambda s, c: (0, c)),
                  pl.BlockSpec((D, tc), lambda s, c: (0, nc + c)),
                  pl.BlockSpec((SUBLANES, tc), lambda s, c: (0, c))] + [ANY] * n_g,
        out_specs=[pl.BlockSpec((2, ts, tc), lambda s, c: (0, s, c)),
                   pl.BlockSpec((ts, D), lambda s, c: (s, 0)),
                   pl.BlockSpec((ts, tc), lambda s, c: (s, c))] + [ANY] * n_g,
        out_shape=[jax.ShapeDtypeStruct((2, S, C), BF), jax.ShapeDtypeStruct((S, D), BF),
                   jax.ShapeDtypeStruct((S, C), BF)]
        + [jax.ShapeDtypeStruct((N_CHIPS,) + a.shape, a.dtype) for a in gather],
        scratch_shapes=[pltpu.VMEM((nc, HALO, tc), F32)] + sems,
        compiler_params=_params("arbitrary", "arbitrary"),
    )(x, g.reshape(1, D), w, w, _pad_conv(cw), *gather)
    return tuple(outs[:3]) + (_fill_own(outs[3:], gather),)


def _mixer_mid_bwd(proj, dy, w_out, cw, name):
    _, S, C = proj.shape
    D = dy.shape[1]
    ts, tc = _tile(S, 512, SUBLANES), _tile(C, 1024)
    n_s = S // ts
    per = ts // HALO

    def body(b_ref, c_ref, h_ref, dy_ref, w_ref, cp_ref, hp_ref, cw_ref, d_ref, dcw_ref, carry):
        i = pl.program_id(1)
        w = cw_ref[...]
        dz = lax.dot_general(dy_ref[...].astype(BF), w_ref[...], NT_DIMS, preferred_element_type=F32)
        b, c, h = b_ref[0].astype(F32), c_ref[0].astype(F32), h_ref[0].astype(F32)
        u = c * h
        prev = jnp.where(i < n_s - 1, cp_ref[0].astype(F32) * hp_ref[0].astype(F32), 0.0)
        u1, u2 = _shift_down(u, prev, 1), _shift_down(u, prev, 2)
        cv = u2 * w[0:1] + u1 * w[1:2] + u * w[2:3]
        dcv = dz * b
        nxt = jnp.where(i > 0, carry[...], 0.0)
        du = _conv3_t(dcv, nxt, w)
        d_ref[0] = (dz * cv).astype(BF)
        d_ref[1] = (du * h).astype(BF)
        d_ref[2] = (du * c).astype(BF)
        carry[...] = dcv[0:HALO, :]
        part = _tap_rows(jnp.sum(dcv * u2, axis=0, keepdims=True), jnp.sum(dcv * u1, axis=0, keepdims=True),
                         jnp.sum(dcv * u, axis=0, keepdims=True))

        @pl.when(i == 0)
        def _():
            dcw_ref[...] = part

        @pl.when(i > 0)
        def _():
            dcw_ref[...] += part

    tile = lambda p: pl.BlockSpec((1, ts, tc), lambda c, i: (p, n_s - 1 - i, c))
    before = lambda p: pl.BlockSpec((1, HALO, tc), lambda c, i: (p, jnp.maximum((n_s - 1 - i) * per - 1, 0), c))
    dproj, dcw = pl.pallas_call(
        body, name=name, grid=(C // tc, n_s),
        in_specs=[tile(0), tile(1), tile(2),
                  pl.BlockSpec((ts, D), lambda c, i: (n_s - 1 - i, 0)),
                  pl.BlockSpec((tc, D), lambda c, i: (c, 0)),
                  before(1), before(2),
                  pl.BlockSpec((SUBLANES, tc), lambda c, i: (0, c))],
        out_specs=[pl.BlockSpec((3, ts, tc), lambda c, i: (0, n_s - 1 - i, c)),
                   pl.BlockSpec((SUBLANES, tc), lambda c, i: (0, c))],
        out_shape=[jax.ShapeDtypeStruct((3, S, C), BF), jax.ShapeDtypeStruct((SUBLANES, C), F32)],
        scratch_shapes=[pltpu.VMEM((HALO, tc), F32)],
        compiler_params=_params("parallel", "arbitrary"),
    )(proj, proj, proj, dy, w_out, proj, proj, _pad_conv(cw))
    return dproj, dcw[:CONV_W]


def _ffn_mid_bwd(up, dy, w_down, cw, name):
    _, S, C = up.shape
    D = dy.shape[1]
    ts, tc = _tile(S, 512, SUBLANES), _tile(C, 1408)
    n_s = S // ts
    per = ts // HALO

    def body(a_ref, g_ref, dy_ref, w_ref, ap_ref, cw_ref, d_ref, dcw_ref, carry):
        i = pl.program_id(1)
        w = cw_ref[...]
        dz = lax.dot_general(dy_ref[...].astype(BF), w_ref[...], NT_DIMS, preferred_element_type=F32)
        a_pre, g = a_ref[0].astype(F32), g_ref[0].astype(F32)
        prev = jnp.where(i < n_s - 1, ap_ref[0].astype(F32), 0.0)
        a1, a2 = _shift_down(a_pre, prev, 1), _shift_down(a_pre, prev, 2)
        a = a2 * w[0:1] + a1 * w[1:2] + a_pre * w[2:3]
        sg = jax.nn.sigmoid(a)
        da = dz * g * (sg * (1.0 + a * (1.0 - sg)))
        nxt = jnp.where(i > 0, carry[...], 0.0)
        d_ref[0] = _conv3_t(da, nxt, w).astype(BF)
        d_ref[1] = (dz * (a * sg)).astype(BF)
        carry[...] = da[0:HALO, :]
        part = _tap_rows(jnp.sum(da * a2, axis=0, keepdims=True), jnp.sum(da * a1, axis=0, keepdims=True),
                         jnp.sum(da * a_pre, axis=0, keepdims=True))

        @pl.when(i == 0)
        def _():
            dcw_ref[...] = part

        @pl.when(i > 0)
        def _():
            dcw_ref[...] += part

    tile = lambda p: pl.BlockSpec((1, ts, tc), lambda c, i: (p, n_s - 1 - i, c))
    dup, dcw = pl.pallas_call(
        body, name=name, grid=(C // tc, n_s),
        in_specs=[tile(0), tile(1),
                  pl.BlockSpec((ts, D), lambda c, i: (n_s - 1 - i, 0)),
                  pl.BlockSpec((tc, D), lambda c, i: (c, 0)),
                  pl.BlockSpec((1, HALO, tc), lambda c, i: (0, jnp.maximum((n_s - 1 - i) * per - 1, 0), c)),
                  pl.BlockSpec((SUBLANES, tc), lambda c, i: (0, c))],
        out_specs=[pl.BlockSpec((2, ts, tc), lambda c, i: (0, n_s - 1 - i, c)),
                   pl.BlockSpec((SUBLANES, tc), lambda c, i: (0, c))],
        out_shape=[jax.ShapeDtypeStruct((2, S, C), BF), jax.ShapeDtypeStruct((SUBLANES, C), F32)],
        scratch_shapes=[pltpu.VMEM((HALO, tc), F32)],
        compiler_params=_params("parallel", "arbitrary"),
    )(up, up, dy, w_down, up, _pad_conv(cw))
    return dup, dcw[:CONV_W]


def _matmul_residual(z, w, x, name):
    S, K = z.shape
    D = w.shape[1]
    ts = _tile(S, 512, SUBLANES)

    def body(z_ref, w_ref, x_ref, o_ref):
        o_ref[...] = x_ref[...] + jnp.dot(z_ref[...], w_ref[...], preferred_element_type=F32)

    return pl.pallas_call(
        body, name=name, grid=(S // ts,),
        in_specs=[pl.BlockSpec((ts, K), lambda s: (s, 0)), pl.BlockSpec((K, D), lambda s: (0, 0)),
                  pl.BlockSpec((ts, D), lambda s: (s, 0))],
        out_specs=pl.BlockSpec((ts, D), lambda s: (s, 0)),
        out_shape=jax.ShapeDtypeStruct((S, D), F32),
        compiler_params=_params("parallel"),
    )(z, w, x)


def _wgrad(a, b, name):
    S, M = a.shape
    P, _, C = b.shape
    tm, tn, tk = _tile(M, 1408), _tile(C, 1408), _tile(S, 1024, SUBLANES)
    nnc = C // tn

    def body(a_ref, b_ref, o_ref):
        @pl.when(pl.program_id(2) == 0)
        def _():
            o_ref[...] = jnp.zeros_like(o_ref)

        o_ref[...] += lax.dot_general(a_ref[...], b_ref[0].astype(BF), TN_DIMS, preferred_element_type=F32)

    return pl.pallas_call(
        body, name=name, grid=(M // tm, P * nnc, S // tk),
        in_specs=[pl.BlockSpec((tk, tm), lambda m, n, k: (k, m)),
                  pl.BlockSpec((1, tk, tn), lambda m, n, k: (n // nnc, k, n % nnc))],
        out_specs=pl.BlockSpec((tm, tn), lambda m, n, k: (m, n)),
        out_shape=jax.ShapeDtypeStruct((M, P * C), F32),
        compiler_params=_params("parallel", "parallel", "arbitrary"),
    )(a, b)


def _dnorm(dp, w, x, g, dy, name):
    P, S, C = dp.shape
    D = x.shape[1]
    ts = _tile(S, 512, SUBLANES)

    def body(dp_ref, w_ref, x_ref, g_ref, dy_ref, dx_ref, dg_ref):
        @pl.when(pl.program_id(0) == 0)
        def _():
            dg_ref[...] = jnp.zeros_like(dg_ref)

        dxn = lax.dot_general(dp_ref[0], w_ref[:, 0:C], NT_DIMS, preferred_element_type=F32)
        for p in range(1, P):
            dxn = dxn + lax.dot_general(dp_ref[p], w_ref[:, p * C:(p + 1) * C], NT_DIMS, preferred_element_type=F32)
        xf = x_ref[...]
        r = lax.rsqrt(jnp.mean(xf * xf, axis=-1, keepdims=True) + RMS_EPS)
        xhat = xf * r
        dxhat = dxn * g_ref[...]
        dx_ref[...] = dy_ref[...] + r * (dxhat - xhat * jnp.mean(dxhat * xhat, axis=-1, keepdims=True))
        dg_ref[...] += jnp.broadcast_to(jnp.sum(dxn * xhat, axis=0, keepdims=True), dg_ref.shape)

    dx, dg = pl.pallas_call(
        body, name=name, grid=(S // ts,),
        in_specs=[pl.BlockSpec((P, ts, C), lambda s: (0, s, 0)),
                  pl.BlockSpec((D, P * C), lambda s: (0, 0), pipeline_mode=pl.Buffered(1)),
                  pl.BlockSpec((ts, D), lambda s: (s, 0)),
                  pl.BlockSpec((1, D), lambda s: (0, 0)),
                  pl.BlockSpec((ts, D), lambda s: (s, 0))],
        out_specs=[pl.BlockSpec((ts, D), lambda s: (s, 0)),
                   pl.BlockSpec((SUBLANES, D), lambda s: (0, 0))],
        out_shape=[jax.ShapeDtypeStruct((S, D), F32), jax.ShapeDtypeStruct((SUBLANES, D), F32)],
        compiler_params=_params("arbitrary"),
    )(dp, w, x, g.reshape(1, D), dy)
    return dx, dg[0]


def _head_masks(shape, hd):
    lane = lax.broadcasted_iota(jnp.int32, shape, 1)
    return lane < hd


def _pair_sum(v, lo):
    s0 = jnp.sum(jnp.where(lo, v, 0.0), axis=-1, keepdims=True)
    s1 = jnp.sum(jnp.where(lo, 0.0, v), axis=-1, keepdims=True)
    return jnp.where(lo, s0, s1)


def _headnorm_bwd(src, part, colblk, w, dys, D, name):
    S = src.shape[1]
    hd = w.shape[0]
    ts = _tile(S, 512, SUBLANES)
    w2 = jnp.tile(w, LANES // hd).reshape(1, LANES)
    n_dy = len(dys)

    def body(x_ref, w_ref, *rest):
        dy_refs, dx_ref, dw_ref = rest[:n_dy], rest[n_dy], rest[n_dy + 1]

        @pl.when(pl.program_id(0) == 0)
        def _():
            dw_ref[...] = jnp.zeros_like(dw_ref)

        lo = _head_masks((ts, LANES), hd)
        for t in range(D // LANES):
            cols = slice(t * LANES, (t + 1) * LANES)
            xt = x_ref[0, :, cols]
            dy = dy_refs[0][:, cols]
            for other in dy_refs[1:]:
                dy = dy + other[:, cols]
            r = lax.rsqrt(_pair_sum(xt * xt, lo) * (1.0 / hd) + RMS_EPS)
            xhat = xt * r
            dxhat = dy * w_ref[...]
            mean = _pair_sum(dxhat * xhat, lo) * (1.0 / hd)
            dx_ref[:, cols] = (r * (dxhat - xhat * mean)).astype(BF)
            dw_ref[:, cols] += jnp.broadcast_to(jnp.sum(dy * xhat, axis=0, keepdims=True), (SUBLANES, LANES))

    dx, dw = pl.pallas_call(
        body, name=name, grid=(S // ts,),
        in_specs=[pl.BlockSpec((1, ts, D), lambda s: (part, s, colblk)), pl.BlockSpec((1, LANES), lambda s: (0, 0))]
        + [pl.BlockSpec((ts, D), lambda s: (s, 0))] * n_dy,
        out_specs=[pl.BlockSpec((ts, D), lambda s: (s, 0)), pl.BlockSpec((SUBLANES, D), lambda s: (0, 0))],
        out_shape=[jax.ShapeDtypeStruct((S, D), BF), jax.ShapeDtypeStruct((SUBLANES, D), F32)],
        compiler_params=_params("arbitrary"),
    )(src, w2, *dys)
    return dx, jnp.sum(dw[0].reshape(D // hd, hd), axis=0)


def _tri(n, lower):
    r, c = lax.broadcasted_iota(jnp.int32, (n, n), 0), lax.broadcasted_iota(jnp.int32, (n, n), 1)
    return jnp.where((c <= r) if lower else (c >= r), 1.0, 0.0).astype(BF)


def _dot_exact(t, v):
    hi = v.astype(BF)
    r1 = v - hi.astype(F32)
    mid = r1.astype(BF)
    lo = (r1 - mid.astype(F32)).astype(BF)
    dot = lambda u: jnp.dot(t, u, preferred_element_type=F32)
    return dot(hi) + dot(mid) + dot(lo)


def _gate_fwd(kvf, b_pad, colblk, name):
    S = kvf.shape[1]
    ts = _tile(S, 512, SUBLANES)

    def body(f_ref, b_ref, c_ref, carry):
        @pl.when(pl.program_id(0) == 0)
        def _():
            carry[...] = jnp.zeros_like(carry)

        f = f_ref[0] + b_ref[...]
        ls = jnp.minimum(f, 0.0) - jnp.log1p(jnp.exp(-jnp.abs(f)))
        tri = _tri(ts, lower=True)
        c = _dot_exact(tri, ls) + carry[0:1, :]
        c_ref[...] = c
        carry[...] = jnp.broadcast_to(c[ts - 1:ts, :], carry.shape)

    return pl.pallas_call(
        body, name=name, grid=(S // ts,),
        in_specs=[pl.BlockSpec((1, ts, LANES), lambda s: (0, s, colblk)), pl.BlockSpec((1, LANES), lambda s: (0, 0))],
        out_specs=pl.BlockSpec((ts, LANES), lambda s: (s, 0)),
        out_shape=jax.ShapeDtypeStruct((S, LANES), F32),
        scratch_shapes=[pltpu.VMEM((SUBLANES, LANES), F32)],
        compiler_params=_params("arbitrary"),
    )(kvf, b_pad)


def _gate_bwd(dc, kvf, b_pad, colblk, name):
    S = kvf.shape[1]
    ts = _tile(S, 512, SUBLANES)
    n_s = S // ts

    def body(dc_ref, f_ref, b_ref, df_ref, db_ref, carry):
        @pl.when(pl.program_id(0) == 0)
        def _():
            carry[...] = jnp.zeros_like(carry)
            db_ref[...] = jnp.zeros_like(db_ref)

        tri = _tri(ts, lower=False)
        dls = _dot_exact(tri, dc_ref[...]) + carry[0:1, :]
        f = f_ref[0] + b_ref[...]
        df = dls * jax.nn.sigmoid(-f)
        df_ref[...] = df
        db_ref[...] += jnp.broadcast_to(jnp.sum(df, axis=0, keepdims=True), db_ref.shape)
        carry[...] = jnp.broadcast_to(dls[0:1, :], carry.shape)

    df, db = pl.pallas_call(
        body, name=name, grid=(n_s,),
        in_specs=[pl.BlockSpec((ts, LANES), lambda s: (n_s - 1 - s, 0)),
                  pl.BlockSpec((1, ts, LANES), lambda s: (0, n_s - 1 - s, colblk)),
                  pl.BlockSpec((1, LANES), lambda s: (0, 0))],
        out_specs=[pl.BlockSpec((ts, LANES), lambda s: (n_s - 1 - s, 0)),
                   pl.BlockSpec((SUBLANES, LANES), lambda s: (0, 0))],
        out_shape=[jax.ShapeDtypeStruct((S, LANES), F32), jax.ShapeDtypeStruct((SUBLANES, LANES), F32)],
        scratch_shapes=[pltpu.VMEM((SUBLANES, LANES), F32)],
        compiler_params=_params("arbitrary"),
    )(dc, kvf, b_pad)
    return df, db[0]


def _attn_tile(S):
    return _tile(S, 512, LANES)


def _split_heads(v, lo):
    zero = jnp.zeros_like(v)
    return jnp.where(lo, v, zero), jnp.where(lo, zero, v)


def _augment(src, part, colblk, c, mode, hd, D, name, norm_w=None, scale=1.0):
    S = src.shape[1]
    ts = _tile(S, 512, 2 * SUBLANES)
    w2 = jnp.tile(jnp.ones((hd,), F32) if norm_w is None else norm_w, LANES // hd).reshape(1, LANES)

    def body(b_ref, w_ref, c_ref, o0_ref, o1_ref):
        lane = lax.broadcasted_iota(jnp.int32, (ts, LANES), 1)
        lo = lane < hd
        cc = c_ref[...] * LOG2E
        for t in range(D // LANES):
            cols = slice(t * LANES, (t + 1) * LANES)
            bt = b_ref[0, :, cols]
            if norm_w is not None:
                r = lax.rsqrt(_pair_sum(bt * bt, lo) * (1.0 / hd) + RMS_EPS)
                bt = bt * r * w_ref[...] * scale
            bt = bt.astype(BF)
            for h, o_ref in ((0, o0_ref), (1, o1_ref)):
                first = hd if h == 0 else 0
                keep = (lane < hd) if h == 0 else (lane >= hd)
                if mode == "v":
                    vals = (1.0,)
                else:
                    col = cc[:, 2 * t + h:2 * t + h + 1]
                    hi = col.astype(BF).astype(F32)
                    mid = (col - hi).astype(BF).astype(F32)
                    pieces = (hi, mid, col - hi - mid)
                    vals = pieces + (1.0, 1.0, 1.0) if mode == "q" else (1.0, 1.0, 1.0) + tuple(-v for v in pieces)
                aug = jnp.zeros((ts, LANES), F32)
                for i, v in enumerate(vals):
                    aug = jnp.where(lane == first + i, v, aug)
                o_ref[:, cols] = jnp.where(keep, bt, aug.astype(BF))

    spec = pl.BlockSpec((ts, D), lambda s: (s, 0))
    return pl.pallas_call(
        body, name=name, grid=(S // ts,),
        in_specs=[pl.BlockSpec((1, ts, D), lambda s: (part, s, colblk)), pl.BlockSpec((1, LANES), lambda s: (0, 0)),
                  pl.BlockSpec((ts, LANES), lambda s: (s, 0))],
        out_specs=[spec, spec],
        out_shape=[jax.ShapeDtypeStruct((S, D), BF)] * 2,
        compiler_params=_params("parallel"),
    )(src, w2, c)


def _attn_fwd(qa, ka, va, qg, hd, name, gather=()):
    S, D = qa[0].shape
    P = D // LANES
    tq = _attn_tile(S)
    nq = S // tq
    n_g = len(gather)

    def body(q0_ref, q1_ref, k0_ref, k1_ref, v0_ref, v1_ref, g_ref, *rest):
        o_ref, og_ref, m_ref, l_ref = rest[n_g:n_g + 4]
        s_buf = rest[2 * n_g + 4]
        pair, qi = pl.program_id(0), pl.program_id(1)
        if n_g:
            start, pass_on, finish = _gather_phases([a.shape[0] for a in gather], rest[:n_g],
                                                    rest[n_g + 4:2 * n_g + 4], *rest[2 * n_g + 5:])
            pl.when((pair == 0) & (qi == 0))(start)
            pl.when((pair == P // 2) & (qi == 0))(pass_on)
        lo = _head_masks((tq, LANES), hd)
        qh = (q0_ref[...], q1_ref[...])
        k_refs, v_refs = (k0_ref, k1_ref), (v0_ref, v1_ref)
        causal = lax.broadcasted_iota(jnp.int32, (tq, tq), 1) <= lax.broadcasted_iota(jnp.int32, (tq, tq), 0)

        def scores(ki, slot):
            off = pl.multiple_of(ki * tq, tq)
            for h in range(2):
                s_buf[slot, h] = lax.dot_general(qh[h], k_refs[h][pl.ds(off, tq), :], NT_DIMS,
                                                 preferred_element_type=F32)

        def consume(ki, slot, carry, masked):
            off = pl.multiple_of(ki * tq, tq)
            out = []
            for h in range(2):
                m, acc = carry[h]
                s = s_buf[slot, h]
                if masked:
                    s = jnp.where(causal, s, -jnp.inf)
                m_new = jnp.maximum(m, jnp.ceil(jnp.max(s, axis=-1, keepdims=True)))
                p = jnp.exp2(s - m_new)
                acc = jnp.exp2(m - m_new) * acc + jnp.dot(p.astype(BF), v_refs[h][pl.ds(off, tq), :],
                                                          preferred_element_type=F32)
                out.append((m_new, acc))
            return tuple(out)

        def step(j, carry):
            scores(2 * j + 1, 1)
            carry = consume(2 * j, 0, carry, False)
            scores(2 * j + 2, 0)
            return consume(2 * j + 1, 1, carry, False)

        def finish_even(carry):
            return consume(qi, 0, carry, True)

        def finish_odd(carry):
            scores(qi, 1)
            return consume(qi, 1, consume(qi - 1, 0, carry, False), True)

        init = tuple((jnp.full((tq, 1), -jnp.inf, F32), jnp.zeros((tq, LANES), F32)) for _ in range(2))
        scores(0, 0)
        carry = lax.fori_loop(0, qi // 2, step, init)
        (m0, a0), (m1, a1) = lax.cond(qi % 2 == 0, finish_even, finish_odd, carry)
        l0, l1 = a0[:, hd:hd + 1], a1[:, 0:1]
        o = jnp.where(lo, a0 / l0, a1 / l1)
        o_ref[...] = o
        og_ref[...] = (o * jax.nn.sigmoid(g_ref[0])).astype(BF)
        lane2 = lax.broadcasted_iota(jnp.int32, (tq, 2), 1)
        m_ref[0] = jnp.where(lane2 == 0, m0, m1)
        l_ref[0] = jnp.where(lane2 == 0, l0, l1)
        if n_g:
            pl.when((pair == P - 1) & (qi == nq - 1))(finish)

    tile = pl.BlockSpec((tq, LANES), lambda p, i: (i, p))
    whole = pl.BlockSpec((S, LANES), lambda p, i: (0, p))
    stat = pl.BlockSpec((1, tq, 2), lambda p, i: (p, i, 0))
    sems = [pltpu.SemaphoreType.DMA((6 * n_g,)), pltpu.SemaphoreType.DMA((6 * n_g,))] if n_g else []
    outs = pl.pallas_call(
        body, name=name, grid=(P, nq),
        in_specs=[tile, tile, whole, whole, whole, whole, pl.BlockSpec((1, tq, LANES), lambda p, i: (1, i, p))]
        + [ANY] * n_g,
        out_specs=[tile, tile, stat, stat] + [ANY] * n_g,
        out_shape=[jax.ShapeDtypeStruct((S, D), F32), jax.ShapeDtypeStruct((S, D), BF),
                   jax.ShapeDtypeStruct((P, S, 2), F32), jax.ShapeDtypeStruct((P, S, 2), F32)]
        + [jax.ShapeDtypeStruct((N_CHIPS,) + a.shape, a.dtype) for a in gather],
        scratch_shapes=[pltpu.VMEM((2, 2, tq, tq), F32)] + sems,
        compiler_params=_params("arbitrary" if n_g else "parallel", "arbitrary"),
    )(*qa, *ka, *va, qg, *gather)
    return tuple(outs[:4]) + (_fill_own(outs[4:], gather),)


def _attn_out_bwd(dy, w_out, o, qg, l, hd, name):
    S, D = o.shape
    P = D // LANES
    ts = _tile(S, 512, 2 * SUBLANES)

    def body(dy_ref, w_ref, o_ref, g_ref, l_ref, do_ref, dg_ref, e_ref):
        lo = _head_masks((ts, LANES), hd)
        lane2 = lax.broadcasted_iota(jnp.int32, (ts, 2), 1)
        dog = lax.dot_general(dy_ref[...].astype(BF), w_ref[...], NT_DIMS, preferred_element_type=F32)
        for t in range(P):
            cols = slice(t * LANES, (t + 1) * LANES)
            sg = jax.nn.sigmoid(g_ref[0, :, cols])
            dog_t, o_t, l_t = dog[:, cols], o_ref[:, cols], l_ref[t]
            g = (dog_t * sg / jnp.where(lo, l_t[:, 0:1], l_t[:, 1:2])).astype(BF)
            do_ref[:, cols] = g
            dg_ref[:, cols] = (dog_t * o_t * sg * (1.0 - sg)).astype(BF)
            prod = g.astype(F32) * o_t
            e0 = jnp.sum(jnp.where(lo, prod, 0.0), axis=-1, keepdims=True)
            e1 = jnp.sum(jnp.where(lo, 0.0, prod), axis=-1, keepdims=True)
            e_ref[t] = jnp.where(lane2 == 0, e0, e1)

    rows = pl.BlockSpec((ts, D), lambda s: (s, 0))
    stat = pl.BlockSpec((P, ts, 2), lambda s: (0, s, 0))
    return pl.pallas_call(
        body, name=name, grid=(S // ts,),
        in_specs=[rows, pl.BlockSpec(w_out.shape, lambda s: (0, 0)), rows,
                  pl.BlockSpec((1, ts, D), lambda s: (1, s, 0)), stat],
        out_specs=[rows, rows, stat],
        out_shape=[jax.ShapeDtypeStruct((S, D), BF), jax.ShapeDtypeStruct((S, D), BF),
                   jax.ShapeDtypeStruct((P, S, 2), F32)],
        compiler_params=_params("parallel"),
    )(dy, w_out, o, qg, l)


def _attn_bwd(qa, ka, vb, g, m_row, e_row, hd, name):
    S, D = vb.shape
    P = D // LANES
    tk = _attn_tile(S)
    nk = S // tk
    scale = hd ** -0.5

    def body(q0_ref, q1_ref, g_ref, k0_ref, k1_ref, v_ref, m_ref, e_ref, dq_ref, dk_ref, dv_ref, dc_ref,
             st_buf, dp_buf):
        ki = pl.program_id(1)

        @pl.when(ki == 0)
        def _():
            dq_ref[...] = jnp.zeros_like(dq_ref)

        lo = _head_masks((tk, LANES), hd)
        kh = (k0_ref[...], k1_ref[...])
        q_refs = (q0_ref, q1_ref)
        vh = _split_heads(v_ref[...], lo)
        causal_t = lax.broadcasted_iota(jnp.int32, (tk, tk), 0) <= lax.broadcasted_iota(jnp.int32, (tk, tk), 1)

        def stage(qi, slot):
            off = pl.multiple_of(qi * tk, tk)
            gb = g_ref[pl.ds(off, tk), :]
            for h in range(2):
                st_buf[slot, h] = lax.dot_general(kh[h], q_refs[h][pl.ds(off, tk), :], NT_DIMS,
                                                  preferred_element_type=F32)
                dp_buf[slot, h] = lax.dot_general(vh[h], gb, NT_DIMS, preferred_element_type=F32)

        def consume(qi, slot, carry, masked):
            off = pl.multiple_of(qi * tk, tk)
            gb = g_ref[pl.ds(off, tk), :]
            m_t, e_t = m_ref[0, qi], e_ref[0, qi]
            out, dq_parts = [], []
            for h in range(2):
                dk, dv, dc = carry[h]
                qb = q_refs[h][pl.ds(off, tk), :]
                pt = jnp.exp2(st_buf[slot, h] - m_t[h:h + 1, :])
                if masked:
                    pt = jnp.where(causal_t, pt, 0.0)
                pb = pt.astype(BF)
                dv = dv + jnp.dot(pb, gb, preferred_element_type=F32)
                dst = pb.astype(F32) * (dp_buf[slot, h] - e_t[h:h + 1, :])
                db = dst.astype(BF)
                dk = dk + jnp.dot(db, qb, preferred_element_type=F32)
                dc = dc - jnp.sum(dst, axis=-1, keepdims=True)
                dq_parts.append(lax.dot_general(db, kh[h], TN_DIMS, preferred_element_type=F32))
                out.append((dk, dv, dc))
            dq_ref[pl.ds(off, tk), :] += jnp.where(lo, dq_parts[0], dq_parts[1]) * scale
            return tuple(out)

        n_after = nk - 1 - ki

        def step(j, carry):
            b = ki + 1 + 2 * j
            stage(b + 1, 0)
            carry = consume(b, 1, carry, False)
            stage(b + 2, 1)
            return consume(b + 1, 0, carry, False)

        def rest_one(carry):
            return consume(nk - 1, 1, carry, False)

        def rest_two(carry):
            stage(nk - 1, 0)
            return consume(nk - 1, 0, consume(nk - 2, 1, carry, False), False)

        init = tuple((jnp.zeros((tk, LANES), F32), jnp.zeros((tk, LANES), F32), jnp.zeros((tk, 1), F32))
                     for _ in range(2))
        stage(ki, 0)
        stage(jnp.minimum(ki + 1, nk - 1), 1)
        carry = consume(ki, 0, init, True)
        carry = lax.fori_loop(0, (n_after - 1) // 2, step, carry)
        which = jnp.where(n_after == 0, 0, 2 - n_after % 2)
        (dk0, dv0, dc0), (dk1, dv1, dc1) = lax.switch(which, [lambda c: c, rest_one, rest_two], carry)
        dk_ref[...] = jnp.where(lo, dk0, dk1) * (1.0 / LOG2E)
        dv_ref[...] = jnp.where(lo, dv0, dv1)
        lane2 = lax.broadcasted_iota(jnp.int32, (tk, 2), 1)
        dc_ref[0] = jnp.where(lane2 == 0, dc0, dc1)

    tile = pl.BlockSpec((tk, LANES), lambda p, i: (i, p))
    whole = pl.BlockSpec((S, LANES), lambda p, i: (0, p))
    row_spec = pl.BlockSpec((1, nk, 2, tk), lambda p, i: (p, 0, 0, 0))
    return pl.pallas_call(
        body, name=name, grid=(P, nk),
        in_specs=[whole, whole, whole, tile, tile, tile, row_spec, row_spec],
        out_specs=[whole, tile, tile, pl.BlockSpec((1, tk, 2), lambda p, i: (p, i, 0))],
        out_shape=[jax.ShapeDtypeStruct((S, D), F32), jax.ShapeDtypeStruct((S, D), F32),
                   jax.ShapeDtypeStruct((S, D), F32), jax.ShapeDtypeStruct((P, S, 2), F32)],
        scratch_shapes=[pltpu.VMEM((2, 2, tk, tk), F32), pltpu.VMEM((2, 2, tk, tk), F32)],
        compiler_params=_params("parallel", "arbitrary"),
    )(*qa, g, *ka, vb, m_row, e_row)


def _loss_head(y, t, name):
    S, D = y.shape
    ts = _tile(S, 512, SUBLANES)

    def body(y_ref, t_ref, dy_ref, l_ref):
        @pl.when(pl.program_id(0) == 0)
        def _():
            l_ref[...] = jnp.zeros_like(l_ref)

        e = y_ref[...] - t_ref[...]
        dy_ref[...] = e * (1.0 / D)
        part = 0.5 * jnp.sum(jnp.mean(e * e, axis=-1, keepdims=True), axis=0, keepdims=True)
        l_ref[...] += jnp.broadcast_to(part, l_ref.shape)

    return pl.pallas_call(
        body, name=name, grid=(S // ts,),
        in_specs=[pl.BlockSpec((ts, D), lambda s: (s, 0)), pl.BlockSpec((ts, D), lambda s: (s, 0))],
        out_specs=[pl.BlockSpec((ts, D), lambda s: (s, 0)), pl.BlockSpec((SUBLANES, LANES), lambda s: (0, 0))],
        out_shape=[jax.ShapeDtypeStruct((S, D), F32), jax.ShapeDtypeStruct((SUBLANES, LANES), F32)],
        compiler_params=_params("arbitrary"),
    )(y, t)


def _adamw(w, g, m, v, name):
    shape = w.shape
    cols = shape[-1]
    as2d = lambda a: a.reshape(-1, cols)
    rows = as2d(w).shape[0]
    tr = _tile(rows, 256, SUBLANES) if rows % SUBLANES == 0 else rows
    c1 = 1.0 - ADAM_B1 ** ADAM_STEP
    c2 = 1.0 - ADAM_B2 ** ADAM_STEP

    def body(w_ref, g_ref, m_ref, v_ref, d_ref, nm_ref, nv_ref):
        gg = g_ref[...]
        nm = ADAM_B1 * m_ref[...] + (1.0 - ADAM_B1) * gg
        nv = ADAM_B2 * v_ref[...] + (1.0 - ADAM_B2) * (gg * gg)
        d_ref[...] = -ADAM_LR * ((nm / c1) / (jnp.sqrt(nv / c2) + ADAM_EPS) + ADAM_WD * w_ref[...])
        nm_ref[...] = nm
        nv_ref[...] = nv

    spec = pl.BlockSpec((tr, cols), lambda r: (r, 0))
    outs = pl.pallas_call(
        body, name=name, grid=(rows // tr,), in_specs=[spec] * 4, out_specs=[spec] * 3,
        out_shape=[jax.ShapeDtypeStruct((rows, cols), F32)] * 3,
        compiler_params=_params("parallel"),
    )(as2d(w), as2d(g), as2d(m), as2d(v))
    return tuple(o.reshape(shape) for o in outs)


def _place():
    return lax.axis_index("x"), lax.axis_index("y"), lax.axis_index("c")


def _other_chips(x, y):
    return [(1 - x, y), (x, 1 - y), (1 - x, 1 - y)]


def _remote(src, dst, send_sems, recv_sems, k, to):
    return pltpu.make_async_remote_copy(src_ref=src, dst_ref=dst, send_sem=send_sems.at[k], recv_sem=recv_sems.at[k],
                                        device_id=to, device_id_type=MESH)


def _half(c, rh):
    return pl.ds(pl.multiple_of(c * rh, 2 * SUBLANES), rh)


def _gather_phases(rows, src, dst, send_sems, recv_sems):
    n = len(rows)
    x, y, c = _place()
    me = 2 * x + y
    sib = (x, y, 1 - c)
    chips = _other_chips(x, y)
    rh = [r // 2 for r in rows]

    def first():
        return [_remote(src[g].at[_half(c, rh[g])], dst[g].at[me, _half(c, rh[g])], send_sems, recv_sems,
                        6 * g + j, (cx, cy, c)) for j, (cx, cy) in enumerate(chips) for g in range(n)]

    def landed(j, g, core):
        cx, cy = chips[j]
        return dst[g].at[2 * cx + cy, _half(core, rh[g])]

    def passed():
        return [_remote(landed(j, g, c), landed(j, g, c), send_sems, recv_sems, 6 * g + 3 + j, sib)
                for j in range(3) for g in range(n)]

    def start():
        for cp in first():
            cp.start()

    def pass_on():
        cps = passed()
        for j in range(3):
            for g in range(n):
                _remote(landed(j, g, c), landed(j, g, c), send_sems, recv_sems, 6 * g + j, sib).wait_recv()
                cps[j * n + g].start()

    def finish():
        for j in range(3):
            for g in range(n):
                _remote(landed(j, g, 1 - c), landed(j, g, 1 - c), send_sems, recv_sems, 6 * g + 3 + j, sib).wait_recv()
        for cp in first() + passed():
            cp.wait_send()

    return start, pass_on, finish


def _fill_own(outs, srcs):
    x, y, _ = _place()
    return [lax.dynamic_update_slice_in_dim(o, a[None], 2 * x + y, axis=0) for o, a in zip(outs, srcs)]


def _allgather_weights(srcs):
    n = len(srcs)

    def body(*refs):
        for step in _gather_phases([a.shape[0] for a in srcs], refs[:n], refs[n:2 * n], *refs[2 * n:]):
            step()

    outs = pl.pallas_call(
        body, name="allgather_weights", in_specs=[ANY] * n, out_specs=[ANY] * n,
        out_shape=[jax.ShapeDtypeStruct((N_CHIPS,) + a.shape, a.dtype) for a in srcs],
        scratch_shapes=[pltpu.SemaphoreType.DMA((6 * n,)), pltpu.SemaphoreType.DMA((6 * n,))],
    )(*srcs)
    return _fill_own(outs, srcs)


def _pair_exchange(gs):
    n = len(gs)

    def body(*refs):
        g_refs, t_refs, (send_sems, recv_sems) = refs[:n], refs[n:2 * n], refs[2 * n:]
        x, y, c = _place()
        cps = [_remote(g_refs[g].at[k, 1 - c], t_refs[g].at[k], send_sems, recv_sems, N_CHIPS * g + k, (x, y, 1 - c))
               for g in range(n) for k in range(N_CHIPS)]
        for cp in cps:
            cp.start()
        for cp in cps:
            cp.wait()

    return pl.pallas_call(
        body, name="grad_pair_exchange", in_specs=[ANY] * n, out_specs=[ANY] * n,
        out_shape=[jax.ShapeDtypeStruct((a.shape[0],) + a.shape[2:], a.dtype) for a in gs],
        scratch_shapes=[pltpu.SemaphoreType.DMA((N_CHIPS * n,)), pltpu.SemaphoreType.DMA((N_CHIPS * n,))],
    )(*gs)


def _pair_add(g, t, c, name):
    n, _, rh, W = g.shape
    tr = _tile(rh, 256, 2 * SUBLANES)

    def body(c_ref, g_ref, t_ref, o_ref):
        o_ref[...] = (g_ref[0] + t_ref[...]).astype(BF)

    return pl.pallas_call(
        body, name=name,
        grid_spec=pltpu.PrefetchScalarGridSpec(
            num_scalar_prefetch=1, grid=(n, rh // tr),
            in_specs=[pl.BlockSpec((1, 1, tr, W), lambda k, i, c_ref: (k, c_ref[0], i, 0)),
                      pl.BlockSpec((1, tr, W), lambda k, i, c_ref: (k, i, 0))],
            out_specs=pl.BlockSpec((1, tr, W), lambda k, i, c_ref: (k, i, 0))),
        out_shape=jax.ShapeDtypeStruct((n, rh, W), BF),
        compiler_params=_params("parallel", "parallel"),
    )(c.reshape(1).astype(jnp.int32), g, t)


def _chip_exchange(parts):
    n = len(parts)

    def body(*refs):
        a_refs, t_refs, (send_sems, recv_sems) = refs[:n], refs[n:2 * n], refs[2 * n:]
        x, y, c = _place()
        cps = [_remote(a_refs[g].at[2 * cx + cy], t_refs[g].at[j], send_sems, recv_sems, 3 * g + j, (cx, cy, c))
               for j, (cx, cy) in enumerate(_other_chips(x, y)) for g in range(n)]
        for cp in cps:
            cp.start()
        for cp in cps:
            cp.wait()

    return pl.pallas_call(
        body, name="grad_chip_exchange", in_specs=[ANY] * n, out_specs=[ANY] * n,
        out_shape=[jax.ShapeDtypeStruct((3,) + a.shape[1:], a.dtype) for a in parts],
        scratch_shapes=[pltpu.SemaphoreType.DMA((3 * n,)), pltpu.SemaphoreType.DMA((3 * n,))],
    )(*parts)


def _chip_add(g, t1, t2, c, me, name):
    _, _, rh, W = g.shape
    tr = _tile(rh, 256, 2 * SUBLANES)

    def body(c_ref, me_ref, g_ref, t1_ref, t2_ref, o_ref):
        own = g_ref[0, 0] + t1_ref[0]
        o_ref[...] = own + t2_ref[0].astype(F32) + t2_ref[1].astype(F32) + t2_ref[2].astype(F32)

    return pl.pallas_call(
        body, name=name,
        grid_spec=pltpu.PrefetchScalarGridSpec(
            num_scalar_prefetch=2, grid=(rh // tr,),
            in_specs=[pl.BlockSpec((1, 1, tr, W), lambda i, c_ref, me_ref: (me_ref[0], c_ref[0], i, 0)),
                      pl.BlockSpec((1, tr, W), lambda i, c_ref, me_ref: (me_ref[0], i, 0)),
                      pl.BlockSpec((3, tr, W), lambda i, c_ref, me_ref: (0, i, 0))],
            out_specs=pl.BlockSpec((tr, W), lambda i, c_ref, me_ref: (i, 0))),
        out_shape=jax.ShapeDtypeStruct((rh, W), F32),
        compiler_params=_params("parallel"),
    )(c.reshape(1).astype(jnp.int32), me.reshape(1).astype(jnp.int32), g, t1, t2)


def _pair_share(hs):
    n = len(hs)

    def body(*refs):
        h_refs, f_refs, (send_sems, recv_sems) = refs[:n], refs[n:2 * n], refs[2 * n:]
        x, y, c = _place()
        cps = [_remote(h_refs[g], f_refs[g], send_sems, recv_sems, g, (x, y, 1 - c)) for g in range(n)]
        for cp in cps:
            cp.start()
        for cp in cps:
            cp.wait()

    return pl.pallas_call(
        body, name="grad_pair_share", in_specs=[ANY] * n, out_specs=[ANY] * n,
        out_shape=[jax.ShapeDtypeStruct(a.shape, a.dtype) for a in hs],
        scratch_shapes=[pltpu.SemaphoreType.DMA((n,)), pltpu.SemaphoreType.DMA((n,))],
    )(*hs)


def _allreduce_small(pack, name):
    rows, W = pack.shape

    def body(p_ref, o_ref, buf, send_sems, recv_sems):
        x, y, c = _place()
        me = 4 * x + 2 * y + c
        buf[me] = p_ref[...]
        cps = []
        for r in range(1, 8):
            fx, fy, fc = (r >> 2) & 1, (r >> 1) & 1, r & 1
            to = (1 - x if fx else x, 1 - y if fy else y, 1 - c if fc else c)
            cps.append(_remote(p_ref, buf.at[me], send_sems, recv_sems, r - 1, to))
        for cp in cps:
            cp.start()
        for r in range(1, 8):
            fx, fy, fc = (r >> 2) & 1, (r >> 1) & 1, r & 1
            frm = 4 * (1 - x if fx else x) + 2 * (1 - y if fy else y) + (1 - c if fc else c)
            _remote(p_ref, buf.at[frm], send_sems, recv_sems, r - 1, (x, y, c)).wait_recv()
        for cp in cps:
            cp.wait_send()
        acc = buf[0]
        for i in range(1, 8):
            acc = acc + buf[i]
        o_ref[...] = acc

    return pl.pallas_call(
        body, name=name, in_specs=[VMEM], out_specs=VMEM,
        out_shape=jax.ShapeDtypeStruct((rows, W), F32),
        scratch_shapes=[pltpu.VMEM((8, rows, W), F32), pltpu.SemaphoreType.DMA((7,)), pltpu.SemaphoreType.DMA((7,))],
    )(pack)


def _width_groups(arrs):
    widths = []
    for a in arrs:
        if a.shape[-1] not in widths:
            widths.append(a.shape[-1])
    return [[i for i, a in enumerate(arrs) if a.shape[-1] == w] for w in widths]


def _rows2d(a):
    return a.reshape(-1, a.shape[-1])


def _split_rows_like(buf, like, lead=()):
    out, off = [], 0
    for a in like:
        n = a.size // a.shape[-1]
        out.append(buf[..., off:off + n, :].reshape(tuple(lead) + a.shape))
        off += n
    return out


def _join_cols(g):
    nd = g.ndim
    return jnp.moveaxis(g, 0, nd - 2).reshape(g.shape[1:-1] + (N_CHIPS * g.shape[-1],))


def _join_rows(g):
    return jnp.moveaxis(g, 0, 1).reshape(g.shape[1], N_CHIPS * g.shape[2], g.shape[3])


def _row_layout(a, tq):
    P, S, _ = a.shape
    return a.reshape(P, S // tq, tq, 2).transpose(0, 1, 3, 2)


def _pad_row(v, width=FLAT_W):
    flat = v.reshape(-1)
    rows = -(-flat.shape[0] // width)
    return jnp.pad(flat, (0, rows * width - flat.shape[0]))


def kernel(x, attn_norm, ffn_norm, a_w_in, a_conv, a_w_out, kv_norm, w_kvf, b_f, k_norm, b_w_qg, q_norm, b_w_out, ffn_w_up, ffn_conv, ffn_w_down, loss_target, m_attn_norm, m_ffn_norm, m_a_w_in, m_a_conv, m_a_w_out, m_kv_norm, m_w_kvf, m_b_f, m_k_norm, m_b_w_qg, m_q_norm, m_b_w_out, m_ffn_w_up, m_ffn_conv, m_ffn_w_down, v_attn_norm, v_ffn_norm, v_a_w_in, v_a_conv, v_a_w_out, v_kv_norm, v_w_kvf, v_b_f, v_k_norm, v_b_w_qg, v_q_norm, v_b_w_out, v_ffn_w_up, v_ffn_conv, v_ffn_w_down):
    xs = x[0]
    S, D = xs.shape
    H, hd = b_f.shape[0], k_norm.shape[0]
    depth = attn_norm.shape[0]
    n_a = a_w_in.shape[0]
    P = D // LANES
    assert LANES == 2 * hd and H * hd == D, "the attention kernels hold two heads per lane tile"
    mx, my, mc = _place()
    chip = 2 * mx + my

    big = [a_w_in, a_w_out, w_kvf, b_w_qg, b_w_out, ffn_w_up, ffn_w_down]
    groups = _width_groups(big)
    assert n_a >= 2 and depth - n_a >= 2, "the hosted weight gathers are laid out for two layers of each kind"
    first = [a_w_in[:1]]
    behind_a_in = {0: [a_w_out[:1], ffn_w_up[:1]], 1: [ffn_w_up[1:n_a]]}
    behind_ffn_up = {0: [ffn_w_down[:1], a_w_in[1:], a_w_out[1:]], 1: [ffn_w_down[1:n_a], w_kvf, b_w_qg[:1]]}
    late = [b_w_qg[1:], b_w_out, ffn_w_up[n_a:], ffn_w_down[n_a:]]

    def packed(ws):
        idx_groups = _width_groups(ws)
        return [jnp.concatenate([_rows2d(ws[i]).astype(BF) for i in idx]) for idx in idx_groups], idx_groups

    def unpacked(bufs, idx_groups, ws):
        out = [None] * len(ws)
        for idx, buf in zip(idx_groups, bufs):
            for i, part in zip(idx, _split_rows_like(buf, [ws[i] for i in idx], (N_CHIPS,))):
                out[i] = part
        return out

    first_src, first_groups = packed(first)
    late_src, late_groups = packed(late)
    (g_in,) = unpacked(_allgather_weights(first_src), first_groups, first)
    wa_in, wa_out, w_up, w_down, wb_qg, wb_out = list(_join_cols(g_in)), [], [], [], [], []
    kvf_cols = 2 * D + LANES

    def placed(shard):
        full = jnp.zeros(shard.shape[:-1] + (N_CHIPS, shard.shape[-1]), F32)
        full = lax.dynamic_update_slice_in_dim(full, shard[..., None, :], chip, axis=full.ndim - 2)
        return jnp.where(mc == 0, full, 0.0).reshape(-1)

    conv_pack = jnp.concatenate([_pad_row(placed(a_conv)), _pad_row(placed(ffn_conv))]).reshape(-1, FLAT_W)
    conv_full = _allreduce_small(conv_pack, "allgather_conv_taps").reshape(-1)
    n_ac = a_conv.size * N_CHIPS
    a_conv_f = conv_full[:n_ac].reshape(a_conv.shape[:-1] + (-1,))
    off = _pad_row(placed(a_conv)).shape[0]
    ffn_conv_f = conv_full[off:off + ffn_conv.size * N_CHIPS].reshape(ffn_conv.shape[:-1] + (-1,))
    F = ffn_conv_f.shape[-1]

    b_pad = jnp.pad(b_f, (0, LANES - H)).reshape(1, LANES)
    gate_blk = 2 * D // LANES
    tq = _attn_tile(S)
    scale = hd ** -0.5

    saved = []
    cur = xs
    kv = None
    for l in range(depth):
        rec = {"x_in": cur}
        if l < n_a:
            ws = behind_a_in.get(l, [])
            srcs, idx_groups = packed(ws) if ws else ((), [])
            proj, xn, z, bufs = _mixer_in_fwd(cur, attn_norm[l], wa_in[l], a_conv_f[l], f"a_in_{l}", gather=srcs)
            if l == 0:
                g_out, g_up = unpacked(bufs, idx_groups, ws)
                wa_out += list(_join_rows(g_out))
                w_up += list(_join_cols(g_up))
            if l == 1:
                (g_up,) = unpacked(bufs, idx_groups, ws)
                w_up += list(_join_cols(g_up))
            mid = _matmul_residual(z, wa_out[l], cur, f"a_out_{l}")
            rec.update(proj=proj, xn=xn, z=z)
        else:
            j = l - n_a
            if kv is None:
                kvf, hn = _norm_matmul(cur, kv_norm, wkvf, 1, F32, "kvf_proj")
                vb = kvf[0, :, D:2 * D].astype(BF)
                cgate = _gate_fwd(kvf, b_pad, gate_blk, "gate_cumsum")
                kv = dict(kvf=kvf, hn=hn, vb=vb, cgate=cgate, x_in=cur, dk=[], dv=[], dc=[],
                          ka=_augment(kvf, 0, 0, cgate, "k", hd, D, "k_augment", norm_w=k_norm),
                          va=_augment(kvf, 0, 1, cgate, "v", hd, D, "v_augment"))
            qg, xn = _norm_matmul(cur, attn_norm[l], wb_qg[j], 2, F32, f"qg_proj_{j}")
            qa = _augment(qg, 0, 0, kv["cgate"], "q", hd, D, f"q_augment_{j}", norm_w=q_norm[j], scale=scale * LOG2E)
            o, og, m_max, l_sum, late_bufs = _attn_fwd(qa, kv["ka"], kv["va"], qg, hd, f"attn_fwd_{j}",
                                                       gather=late_src if j == 0 else ())
            if j == 0:
                g_qg, g_bout, g_up, g_down = unpacked(late_bufs, late_groups, late)
                wb_qg += list(_join_cols(g_qg))
                wb_out += list(_join_rows(g_bout))
                w_up += list(_join_cols(g_up))
                w_down += list(_join_rows(g_down))
            mid = _matmul_residual(og, wb_out[j], cur, f"b_out_{j}")
            rec.update(qg=qg, xn=xn, qa=qa, o=o, og=og, m=m_max, l=l_sum)
        ws = behind_ffn_up.get(l, [])
        srcs, idx_groups = packed(ws) if ws else ((), [])
        up, xn2, z2, bufs = _ffn_up_fwd(mid, ffn_norm[l], w_up[l], ffn_conv_f[l], f"ffn_up_{l}", gather=srcs)
        if l == 0:
            g_down, g_in, g_out = unpacked(bufs, idx_groups, ws)
            w_down += list(_join_rows(g_down))
            wa_in += list(_join_cols(g_in))
            wa_out += list(_join_rows(g_out))
        if l == 1:
            g_down, g_kvf, g_qg = unpacked(bufs, idx_groups, ws)
            w_down += list(_join_rows(g_down))
            wkvf = jnp.pad(_join_cols(g_kvf), ((0, 0), (0, kvf_cols - (2 * D + H))))
            wb_qg += list(_join_cols(g_qg))
        cur = _matmul_residual(z2, w_down[l], mid, f"ffn_down_{l}")
        rec.update(x_mid=mid, up=up, xn2=xn2, z2=z2)
        saved.append(rec)

    dy, loss_part = _loss_head(cur, loss_target[0], "loss_head")

    g_attn_norm, g_ffn_norm = [None] * depth, [None] * depth
    g_a_in, g_a_conv, g_a_out = [None] * n_a, [None] * n_a, [None] * n_a
    g_qg, g_qn, g_bo = [None] * (depth - n_a), [None] * (depth - n_a), [None] * (depth - n_a)
    g_up, g_fc, g_down = [None] * depth, [None] * depth, [None] * depth
    for l in reversed(range(depth)):
        rec = saved[l]
        dup, g_fc[l] = _ffn_mid_bwd(rec["up"], dy, w_down[l], ffn_conv_f[l], f"ffn_mid_bwd_{l}")
        g_down[l] = _wgrad(rec["z2"], dy[None], f"ffn_down_wgrad_{l}")
        g_up[l] = _wgrad(rec["xn2"], dup, f"ffn_up_wgrad_{l}")
        dy, g_ffn_norm[l] = _dnorm(dup, w_up[l], rec["x_mid"], ffn_norm[l], dy, f"ffn_up_bwd_{l}")
        if l < n_a:
            dproj, g_a_conv[l] = _mixer_mid_bwd(rec["proj"], dy, wa_out[l], a_conv_f[l], f"a_mid_bwd_{l}")
            g_a_out[l] = _wgrad(rec["z"], dy[None], f"a_out_wgrad_{l}")
            g_a_in[l] = _wgrad(rec["xn"], dproj, f"a_in_wgrad_{l}")
            dy, g_attn_norm[l] = _dnorm(dproj, wa_in[l], rec["x_in"], attn_norm[l], dy, f"a_in_bwd_{l}")
        else:
            j = l - n_a
            g_out, dgate, evec = _attn_out_bwd(dy, wb_out[j], rec["o"], rec["qg"], rec["l"], hd,
                                               f"attn_gate_bwd_{j}")
            g_bo[j] = _wgrad(rec["og"], dy[None], f"b_out_wgrad_{j}")
            dqn, dk, dv, dc = _attn_bwd(rec["qa"], kv["ka"], kv["vb"], g_out, _row_layout(rec["m"], tq),
                                        _row_layout(evec, tq), hd, f"attn_bwd_{j}")
            kv["dk"].append(dk)
            kv["dv"].append(dv)
            kv["dc"].append(dc)
            dq_pre, g_qn[j] = _headnorm_bwd(rec["qg"], 0, 0, q_norm[j], [dqn], D, f"q_norm_bwd_{j}")
            dqg = jnp.stack([dq_pre, dgate])
            g_qg[j] = _wgrad(rec["xn"], dqg, f"qg_wgrad_{j}")
            dy, g_attn_norm[l] = _dnorm(dqg, wb_qg[j], rec["x_in"], attn_norm[l], dy, f"qg_bwd_{j}")
            if l == n_a:
                dk_s, g_k_norm = _headnorm_bwd(kv["kvf"], 0, 0, k_norm, kv["dk"], D, "k_norm_bwd")
                dv_s = functools.reduce(jnp.add, kv["dv"]).astype(BF)
                dc_sum = functools.reduce(jnp.add, kv["dc"])
                dc_pad = jnp.pad(dc_sum.transpose(1, 0, 2).reshape(S, H), ((0, 0), (0, LANES - H)))
                df, db = _gate_bwd(dc_pad, kv["kvf"], b_pad, gate_blk, "gate_bwd")
                dkvf = jnp.concatenate([dk_s, dv_s, df.astype(BF)], axis=1)[None]
                g_kvf = _wgrad(kv["hn"], dkvf, "kvf_wgrad")[:, :2 * D + H]
                g_b_f = db[:H]
                dy, g_kv_norm = _dnorm(dkvf, wkvf, kv["x_in"], kv_norm, dy, "kvf_bwd")
    grad_x = dy[None]

    def cols_of(g, k):
        c = g.shape[-1] // N_CHIPS
        return g[:, k * c:(k + 1) * c]

    def rows_of(g, k):
        r = g.shape[0] // N_CHIPS
        return g[k * r:(k + 1) * r]

    per_layer = [g_a_in, g_a_out, [g_kvf], g_qg, g_bo, g_up, g_down]
    of_chip = [cols_of, rows_of, cols_of, cols_of, rows_of, cols_of, rows_of]

    def group_buffer(idx):
        rows = [of_chip[i](g, k) for k in range(N_CHIPS) for i in idx for g in per_layer[i]]
        buf = jnp.concatenate(rows)
        return buf.reshape(N_CHIPS, 2, buf.shape[0] // (2 * N_CHIPS), buf.shape[1])

    g4 = [group_buffer(idx) for idx in groups]
    from_sibling = _pair_exchange(g4)
    from_chips = _chip_exchange([_pair_add(g, t, mc, f"grad_pair_add_{n}") for n, (g, t) in
                                 enumerate(zip(g4, from_sibling))])
    mine = [_chip_add(g, t1, t2, mc, chip, f"grad_chip_add_{n}") for n, (g, t1, t2) in
            enumerate(zip(g4, from_sibling, from_chips))]
    theirs = _pair_share(mine)
    big_grads = [None] * len(big)
    for idx, m_half, t_half in zip(groups, mine, theirs):
        shard = jnp.where(mc == 0, jnp.concatenate([m_half, t_half]), jnp.concatenate([t_half, m_half]))
        for i, part in zip(idx, _split_rows_like(shard, [big[i] for i in idx])):
            big_grads[i] = part

    small = [loss_part[0, :1], jnp.stack(g_attn_norm), jnp.stack(g_ffn_norm), g_kv_norm, g_b_f, g_k_norm,
             jnp.stack(g_qn), jnp.stack(g_a_conv), jnp.stack(g_fc)]
    small_sum = _allreduce_small(jnp.concatenate([_pad_row(s) for s in small]).reshape(-1, FLAT_W),
                                 "allreduce_small_grads").reshape(-1)
    parts, off = [], 0
    for s in small:
        parts.append(small_sum[off:off + s.size].reshape(s.shape))
        off += _pad_row(s).shape[0]
    loss = parts[0][0]
    gr_attn_norm, gr_ffn_norm, gr_kv_norm, gr_b_f, gr_k_norm, gr_q_norm, gr_a_conv_full, gr_ffn_conv_full = parts[1:]

    def my_cols(full):
        c = full.shape[-1] // N_CHIPS
        return lax.dynamic_slice_in_dim(full, chip * c, c, axis=full.ndim - 1)

    gr_a_in, gr_a_out, gr_kvf, gr_qg, gr_bo, gr_up, gr_down = big_grads
    grads = [gr_attn_norm, gr_ffn_norm, gr_a_in, my_cols(gr_a_conv_full), gr_a_out, gr_kv_norm, gr_kvf, gr_b_f,
             gr_k_norm, gr_qg, gr_q_norm, gr_bo, gr_up, my_cols(gr_ffn_conv_full), gr_down]
    weights = [attn_norm, ffn_norm, a_w_in, a_conv, a_w_out, kv_norm, w_kvf, b_f, k_norm, b_w_qg, q_norm, b_w_out,
               ffn_w_up, ffn_conv, ffn_w_down]
    ms = [m_attn_norm, m_ffn_norm, m_a_w_in, m_a_conv, m_a_w_out, m_kv_norm, m_w_kvf, m_b_f, m_k_norm, m_b_w_qg,
          m_q_norm, m_b_w_out, m_ffn_w_up, m_ffn_conv, m_ffn_w_down]
    vs = [v_attn_norm, v_ffn_norm, v_a_w_in, v_a_conv, v_a_w_out, v_kv_norm, v_w_kvf, v_b_f, v_k_norm, v_b_w_qg,
          v_q_norm, v_b_w_out, v_ffn_w_up, v_ffn_conv, v_ffn_w_down]
    deltas, new_ms, new_vs = [], [], []
    for i, (w, g, m, v) in enumerate(zip(weights, grads, ms, vs)):
        d, nm, nv = _adamw(w, g, m, v, f"adamw_{i}")
        deltas.append(d)
        new_ms.append(nm)
        new_vs.append(nv)
    return (loss, grad_x, *grads, *deltas, *new_ms, *new_vs)
```

```python
import functools

import jax
import jax.numpy as jnp
from jax import lax
from jax.experimental import pallas as pl
from jax.experimental.pallas import tpu as pltpu

F32 = jnp.float32
BF = jnp.bfloat16
LANES = 128
SUBLANES = 8
RMS_EPS = 1e-6
LOG2E = 1.4426950408889634
FLAT_W = 1024
N_CHIPS = 4
CONV_W = 3
HALO = SUBLANES

ADAM_LR = 0.001
ADAM_B1 = 0.9
ADAM_B2 = 0.999
ADAM_EPS = 1e-08
ADAM_WD = 0.01
ADAM_STEP = 10

MESH = pl.DeviceIdType.MESH
ANY = pl.BlockSpec(memory_space=pl.ANY)
VMEM = pl.BlockSpec(memory_space=pltpu.VMEM)
NT_DIMS = (((1,), (1,)), ((), ()))
TN_DIMS = (((0,), (0,)), ((), ()))


def _tile(n, pref, mult=LANES):
    t = (min(pref, n) // mult) * mult
    while t >= mult:
        if n % t == 0:
            break
        t -= mult
    if t < mult or (t * 4 < pref and n <= 4 * pref):
        return n
    return t


def _params(*sem):
    return pltpu.CompilerParams(dimension_semantics=sem)


def _norm_matmul(x, g, w, parts, out_dtype, name):
    S, D = x.shape
    C = w.shape[1] // parts
    ts, tn = _tile(S, 512, SUBLANES), _tile(C, 1408)
    npc = C // tn

    def body(x_ref, g_ref, w_ref, o_ref, xn_ref):
        @pl.when(pl.program_id(1) == 0)
        def _():
            xf = x_ref[...]
            r = lax.rsqrt(jnp.mean(xf * xf, axis=-1, keepdims=True) + RMS_EPS)
            xn_ref[...] = (xf * r * g_ref[...]).astype(BF)

        o_ref[0] = jnp.dot(xn_ref[...], w_ref[...], preferred_element_type=F32).astype(out_dtype)

    return pl.pallas_call(
        body, name=name, grid=(S // ts, parts * npc),
        in_specs=[pl.BlockSpec((ts, D), lambda s, n: (s, 0)),
                  pl.BlockSpec((1, D), lambda s, n: (0, 0)),
                  pl.BlockSpec((D, tn), lambda s, n: (0, n))],
        out_specs=[pl.BlockSpec((1, ts, tn), lambda s, n: (n // npc, s, n % npc)),
                   pl.BlockSpec((ts, D), lambda s, n: (s, 0))],
        out_shape=[jax.ShapeDtypeStruct((parts, S, C), out_dtype), jax.ShapeDtypeStruct((S, D), BF)],
        compiler_params=_params("parallel", "arbitrary"),
    )(x, g.reshape(1, D), w)


def _shift_down(u, prev, k):
    r = pltpu.roll(u, k, 0)
    row = lax.broadcasted_iota(jnp.int32, (HALO, u.shape[1]), 0)
    head = r[0:HALO]
    for j in range(k):
        head = jnp.where(row == j, prev[HALO - k + j:HALO - k + j + 1, :], head)
    return jnp.concatenate([head, r[HALO:]], axis=0)


def _shift_up(d, nxt, k):
    n = d.shape[0]
    r = pltpu.roll(d, n - k, 0)
    row = lax.broadcasted_iota(jnp.int32, (HALO, d.shape[1]), 0)
    tail = r[n - HALO:n]
    for j in range(k):
        tail = jnp.where(row == HALO - k + j, nxt[j:j + 1, :], tail)
    return jnp.concatenate([r[0:n - HALO], tail], axis=0)


def _conv3(u, prev, w):
    return _shift_down(u, prev, 2) * w[0:1] + _shift_down(u, prev, 1) * w[1:2] + u * w[2:3]


def _conv3_t(d, nxt, w):
    return d * w[2:3] + _shift_up(d, nxt, 1) * w[1:2] + _shift_up(d, nxt, 2) * w[0:1]


def _tap_rows(t0, t1, t2):
    row = lax.broadcasted_iota(jnp.int32, (SUBLANES, t0.shape[1]), 0)
    return jnp.where(row == 0, t0, jnp.where(row == 1, t1, jnp.where(row == 2, t2, 0.0)))


def _pad_conv(cw):
    return jnp.pad(cw, ((0, SUBLANES - CONV_W), (0, 0)))


def _mixer_in_fwd(x, g, w, cw, name, gather=()):
    S, D = x.shape
    C = w.shape[1] // 3
    ts, tc = _tile(S, 512, SUBLANES), _tile(C, 1024)
    nc = C // tc
    n_s = S // ts
    n_g = len(gather)

    def body(x_ref, g_ref, wb_ref, wc_ref, wh_ref, cw_ref, *rest):
        p_ref, xn_ref, z_ref = rest[n_g:n_g + 3]
        carry = rest[2 * n_g + 3]
        s, c = pl.program_id(0), pl.program_id(1)
        if n_g:
            start, pass_on, finish = _gather_phases([a.shape[0] for a in gather], rest[:n_g],
                                                    rest[n_g + 3:2 * n_g + 3], *rest[2 * n_g + 4:])
            pl.when((s == 0) & (c == 0))(start)
            pl.when((s == (3 * n_s) // 4) & (c == 0))(pass_on)

        @pl.when(c == 0)
        def _():
            xf = x_ref[...]
            r = lax.rsqrt(jnp.mean(xf * xf, axis=-1, keepdims=True) + RMS_EPS)
            xn_ref[...] = (xf * r * g_ref[...]).astype(BF)

        @pl.when(s == 0)
        def _():
            carry[c] = jnp.zeros((HALO, tc), F32)

        xn = xn_ref[...]
        parts = [jnp.dot(xn, w_ref[...], preferred_element_type=F32).astype(BF) for w_ref in (wb_ref, wc_ref, wh_ref)]
        for p, v in enumerate(parts):
            p_ref[p] = v
        u = parts[1].astype(F32) * parts[2].astype(F32)
        cv = _conv3(u, carry[c], cw_ref[...])
        z_ref[...] = (parts[0].astype(F32) * cv).astype(BF)
        carry[c] = u[ts - HALO:ts, :]
        if n_g:
            pl.when((s == n_s - 1) & (c == nc - 1))(finish)

    wspec = lambda p: pl.BlockSpec((D, tc), lambda s, c: (0, p * nc + c))
    sems = [pltpu.SemaphoreType.DMA((6 * n_g,)), pltpu.SemaphoreType.DMA((6 * n_g,))] if n_g else []
    outs = pl.pallas_call(
        body, name=name, grid=(n_s, nc),
        in_specs=[pl.BlockSpec((ts, D), lambda s, c: (s, 0)), pl.BlockSpec((1, D), lambda s, c: (0, 0)),
                  wspec(0), wspec(1), wspec(2), pl.BlockSpec((SUBLANES, tc), lambda s, c: (0, c))] + [ANY] * n_g,
        out_specs=[pl.BlockSpec((3, ts, tc), lambda s, c: (0, s, c)),
                   pl.BlockSpec((ts, D), lambda s, c: (s, 0)),
                   pl.BlockSpec((ts, tc), lambda s, c: (s, c))] + [ANY] * n_g,
        out_shape=[jax.ShapeDtypeStruct((3, S, C), BF), jax.ShapeDtypeStruct((S, D), BF),
                   jax.ShapeDtypeStruct((S, C), BF)]
        + [jax.ShapeDtypeStruct((N_CHIPS,) + a.shape, a.dtype) for a in gather],
        scratch_shapes=[pltpu.VMEM((nc, HALO, tc), F32)] + sems,
        compiler_params=_params("arbitrary", "arbitrary"),
    )(x, g.reshape(1, D), w, w, w, _pad_conv(cw), *gather)
    return tuple(outs[:3]) + (_fill_own(outs[3:], gather),)


def _ffn_up_fwd(x, g, w, cw, name, gather=()):
    S, D = x.shape
    C = w.shape[1] // 2
    ts, tc = _tile(S, 512, SUBLANES), _tile(C, 1408)
    nc = C // tc
    n_s = S // ts
    n_g = len(gather)

    def body(x_ref, g_ref, wa_ref, wg_ref, cw_ref, *rest):
        up_ref, xn_ref, z_ref = rest[n_g:n_g + 3]
        carry = rest[2 * n_g + 3]
        s, c = pl.program_id(0), pl.program_id(1)
        if n_g:
            start, pass_on, finish = _gather_phases([a.shape[0] for a in gather], rest[:n_g],
                                                    rest[n_g + 3:2 * n_g + 3], *rest[2 * n_g + 4:])
            pl.when((s == 0) & (c == 0))(start)
            pl.when((s == (3 * n_s) // 4) & (c == 0))(pass_on)

        @pl.when(c == 0)
        def _():
            xf = x_ref[...]
            r = lax.rsqrt(jnp.mean(xf * xf, axis=-1, keepdims=True) + RMS_EPS)
            xn_ref[...] = (xf * r * g_ref[...]).astype(BF)

        @pl.when(s == 0)
        def _():
            carry[c] = jnp.zeros((HALO, tc), F32)

        xn = xn_ref[...]
        a_b = jnp.dot(xn, wa_ref[...], preferred_element_type=F32).astype(BF)
        g_b = jnp.dot(xn, wg_ref[...], preferred_element_type=F32).astype(BF)
        up_ref[0] = a_b
        up_ref[1] = g_b
        a_pre = a_b.astype(F32)
        a = _conv3(a_pre, carry[c], cw_ref[...])
        z_ref[...] = (a * jax.nn.sigmoid(a) * g_b.astype(F32)).astype(BF)
        carry[c] = a_pre[ts - HALO:ts, :]
        if n_g:
            pl.when((s == n_s - 1) & (c == nc - 1))(finish)

    sems = [pltpu.SemaphoreType.DMA((6 * n_g,)), pltpu.SemaphoreType.DMA((6 * n_g,))] if n_g else []
    outs = pl.pallas_call(
        body, name=name, grid=(n_s, nc),
        in_specs=[pl.BlockSpec((ts, D), lambda s, c: (s, 0)),
                  pl.BlockSpec((1, D), lambda s, c: (0, 0)),
                  pl.BlockSpec((D, tc), lambda s, c: (0, c)),
                  pl.BlockSpec((D, tc), lambda s, c: (0, nc + c)),
                  pl.BlockSpec((SUBLANES, tc), lambda s, c: (0, c))] + [ANY] * n_g,
        out_specs=[pl.BlockSpec((2, ts, tc), lambda s, c: (0, s, c)),
                   pl.BlockSpec((ts, D), lambda s, c: (s, 0)),
                   pl.BlockSpec((ts, tc), lambda s, c: (s, c))] + [ANY] * n_g,
        out_shape=[jax.ShapeDtypeStruct((2, S, C), BF), jax.ShapeDtypeStruct((S, D), BF),
                   jax.ShapeDtypeStruct((S, C), BF)]
        + [jax.ShapeDtypeStruct((N_CHIPS,) + a.shape, a.dtype) for a in gather],
        scratch_shapes=[pltpu.VMEM((nc, HALO, tc), F32)] + sems,
        compiler_params=_params("arbitrary", "arbitrary"),
    )(x, g.reshape(1, D), w, w, _pad_conv(cw), *gather)
    return tuple(outs[:3]) + (_fill_own(outs[3:], gather),)


def _mixer_mid_bwd(proj, dy, w_out, cw, name):
    _, S, C = proj.shape
    D = dy.shape[1]
    ts, tc = _tile(S, 512, SUBLANES), _tile(C, 1024)
    n_s = S // ts
    per = ts // HALO

    def body(b_ref, c_ref, h_ref, dy_ref, w_ref, cp_ref, hp_ref, cw_ref, d_ref, dcw_ref, carry):
        i = pl.program_id(1)
        w = cw_ref[...]
        dz = lax.dot_general(dy_ref[...].astype(BF), w_ref[...], NT_DIMS, preferred_element_type=F32)
        b, c, h = b_ref[0].astype(F32), c_ref[0].astype(F32), h_ref[0].astype(F32)
        u = c * h
        prev = jnp.where(i < n_s - 1, cp_ref[0].astype(F32) * hp_ref[0].astype(F32), 0.0)
        u1, u2 = _shift_down(u, prev, 1), _shift_down(u, prev, 2)
        cv = u2 * w[0:1] + u1 * w[1:2] + u * w[2:3]
        dcv = dz * b
        nxt = jnp.where(i > 0, carry[...], 0.0)
        du = _conv3_t(dcv, nxt, w)
        d_ref[0] = (dz * cv).astype(BF)
        d_ref[1] = (du * h).astype(BF)
        d_ref[2] = (du * c).astype(BF)
        carry[...] = dcv[0:HALO, :]
        part = _tap_rows(jnp.sum(dcv * u2, axis=0, keepdims=True), jnp.sum(dcv * u1, axis=0, keepdims=True),
                         jnp.sum(dcv * u, axis=0, keepdims=True))

        @pl.when(i == 0)
        def _():
            dcw_ref[...] = part

        @pl.when(i > 0)
        def _():
            dcw_ref[...] += part

    tile = lambda p: pl.BlockSpec((1, ts, tc), lambda c, i: (p, n_s - 1 - i, c))
    before = lambda p: pl.BlockSpec((1, HALO, tc), lambda c, i: (p, jnp.maximum((n_s - 1 - i) * per - 1, 0), c))
    dproj, dcw = pl.pallas_call(
        body, name=name, grid=(C // tc, n_s),
        in_specs=[tile(0), tile(1), tile(2),
                  pl.BlockSpec((ts, D), lambda c, i: (n_s - 1 - i, 0)),
                  pl.BlockSpec((tc, D), lambda c, i: (c, 0)),
                  before(1), before(2),
                  pl.BlockSpec((SUBLANES, tc), lambda c, i: (0, c))],
        out_specs=[pl.BlockSpec((3, ts, tc), lambda c, i: (0, n_s - 1 - i, c)),
                   pl.BlockSpec((SUBLANES, tc), lambda c, i: (0, c))],
        out_shape=[jax.ShapeDtypeStruct((3, S, C), BF), jax.ShapeDtypeStruct((SUBLANES, C), F32)],
        scratch_shapes=[pltpu.VMEM((HALO, tc), F32)],
        compiler_params=_params("parallel", "arbitrary"),
    )(proj, proj, proj, dy, w_out, proj, proj, _pad_conv(cw))
    return dproj, dcw[:CONV_W]


def _ffn_mid_bwd(up, dy, w_down, cw, name):
    _, S, C = up.shape
    D = dy.shape[1]
    ts, tc = _tile(S, 512, SUBLANES), _tile(C, 1408)
    n_s = S // ts
    per = ts // HALO

    def body(a_ref, g_ref, dy_ref, w_ref, ap_ref, cw_ref, d_ref, dcw_ref, carry):
        i = pl.program_id(1)
        w = cw_ref[...]
        dz = lax.dot_general(dy_ref[...].astype(BF), w_ref[...], NT_DIMS, preferred_element_type=F32)
        a_pre, g = a_ref[0].astype(F32), g_ref[0].astype(F32)
        prev = jnp.where(i < n_s - 1, ap_ref[0].astype(F32), 0.0)
        a1, a2 = _shift_down(a_pre, prev, 1), _shift_down(a_pre, prev, 2)
        a = a2 * w[0:1] + a1 * w[1:2] + a_pre * w[2:3]
        sg = jax.nn.sigmoid(a)
        da = dz * g * (sg * (1.0 + a * (1.0 - sg)))
        nxt = jnp.where(i > 0, carry[...], 0.0)
        d_ref[0] = _conv3_t(da, nxt, w).astype(BF)
        d_ref[1] = (dz * (a * sg)).astype(BF)
        carry[...] = da[0:HALO, :]
        part = _tap_rows(jnp.sum(da * a2, axis=0, keepdims=True), jnp.sum(da * a1, axis=0, keepdims=True),
                         jnp.sum(da * a_pre, axis=0, keepdims=True))

        @pl.when(i == 0)
        def _():
            dcw_ref[...] = part

        @pl.when(i > 0)
        def _():
            dcw_ref[...] += part

    tile = lambda p: pl.BlockSpec((1, ts, tc), lambda c, i: (p, n_s - 1 - i, c))
    dup, dcw = pl.pallas_call(
        body, name=name, grid=(C // tc, n_s),
        in_specs=[tile(0), tile(1),
                  pl.BlockSpec((ts, D), lambda c, i: (n_s - 1 - i, 0)),
                  pl.BlockSpec((tc, D), lambda c, i: (c, 0)),
                  pl.BlockSpec((1, HALO, tc), lambda c, i: (0, jnp.maximum((n_s - 1 - i) * per - 1, 0), c)),
                  pl.BlockSpec((SUBLANES, tc), lambda c, i: (0, c))],
        out_specs=[pl.BlockSpec((2, ts, tc), lambda c, i: (0, n_s - 1 - i, c)),
                   pl.BlockSpec((SUBLANES, tc), lambda c, i: (0, c))],
        out_shape=[jax.ShapeDtypeStruct((2, S, C), BF), jax.ShapeDtypeStruct((SUBLANES, C), F32)],
        scratch_shapes=[pltpu.VMEM((HALO, tc), F32)],
        compiler_params=_params("parallel", "arbitrary"),
    )(up, up, dy, w_down, up, _pad_conv(cw))
    return dup, dcw[:CONV_W]


def _matmul_residual(z, w, x, name):
    S, K = z.shape
    D = w.shape[1]
    ts = _tile(S, 512, SUBLANES)

    def body(z_ref, w_ref, x_ref, o_ref):
        o_ref[...] = x_ref[...] + jnp.dot(z_ref[...], w_ref[...], preferred_element_type=F32)

    return pl.pallas_call(
        body, name=name, grid=(S // ts,),
        in_specs=[pl.BlockSpec((ts, K), lambda s: (s, 0)), pl.BlockSpec((K, D), lambda s: (0, 0)),
                  pl.BlockSpec((ts, D), lambda s: (s, 0))],
        out_specs=pl.BlockSpec((ts, D), lambda s: (s, 0)),
        out_shape=jax.ShapeDtypeStruct((S, D), F32),
        compiler_params=_params("parallel"),
    )(z, w, x)


def _wgrad(a, b, name):
    S, M = a.shape
    P, _, C = b.shape
    tm, tn, tk = _tile(M, 1408), _tile(C, 1408), _tile(S, 1024, SUBLANES)
    nnc = C // tn

    def body(a_ref, b_ref, o_ref):
        @pl.when(pl.program_id(2) == 0)
        def _():
            o_ref[...] = jnp.zeros_like(o_ref)

        o_ref[...] += lax.dot_general(a_ref[...], b_ref[0].astype(BF), TN_DIMS, preferred_element_type=F32)

    return pl.pallas_call(
        body, name=name, grid=(M // tm, P * nnc, S // tk),
        in_specs=[pl.BlockSpec((tk, tm), lambda m, n, k: (k, m)),
                  pl.BlockSpec((1, tk, tn), lambda m, n, k: (n // nnc, k, n % nnc))],
        out_specs=pl.BlockSpec((tm, tn), lambda m, n, k: (m, n)),
        out_shape=jax.ShapeDtypeStruct((M, P * C), F32),
        compiler_params=_params("parallel", "parallel", "arbitrary"),
    )(a, b)


def _dnorm(dp, w, x, g, dy, name):
    P, S, C = dp.shape
    D = x.shape[1]
    ts = _tile(S, 512, SUBLANES)

    def body(dp_ref, w_ref, x_ref, g_ref, dy_ref, dx_ref, dg_ref):
        @pl.when(pl.program_id(0) == 0)
        def _():
            dg_ref[...] = jnp.zeros_like(dg_ref)

        dxn = lax.dot_general(dp_ref[0], w_ref[:, 0:C], NT_DIMS, preferred_element_type=F32)
        for p in range(1, P):
            dxn = dxn + lax.dot_general(dp_ref[p], w_ref[:, p * C:(p + 1) * C], NT_DIMS, preferred_element_type=F32)
        xf = x_ref[...]
        r = lax.rsqrt(jnp.mean(xf * xf, axis=-1, keepdims=True) + RMS_EPS)
        xhat = xf * r
        dxhat = dxn * g_ref[...]
        dx_ref[...] = dy_ref[...] + r * (dxhat - xhat * jnp.mean(dxhat * xhat, axis=-1, keepdims=True))
        dg_ref[...] += jnp.broadcast_to(jnp.sum(dxn * xhat, axis=0, keepdims=True), dg_ref.shape)

    dx, dg = pl.pallas_call(
        body, name=name, grid=(S // ts,),
        in_specs=[pl.BlockSpec((P, ts, C), lambda s: (0, s, 0)),
                  pl.BlockSpec((D, P * C), lambda s: (0, 0), pipeline_mode=pl.Buffered(1)),
                  pl.BlockSpec((ts, D), lambda s: (s, 0)),
                  pl.BlockSpec((1, D), lambda s: (0, 0)),
                  pl.BlockSpec((ts, D), lambda s: (s, 0))],
        out_specs=[pl.BlockSpec((ts, D), lambda s: (s, 0)),
                   pl.BlockSpec((SUBLANES, D), lambda s: (0, 0))],
        out_shape=[jax.ShapeDtypeStruct((S, D), F32), jax.ShapeDtypeStruct((SUBLANES, D), F32)],
        compiler_params=_params("arbitrary"),
    )(dp, w, x, g.reshape(1, D), dy)
    return dx, dg[0]


def _head_masks(shape, hd):
    lane = lax.broadcasted_iota(jnp.int32, shape, 1)
    return lane < hd


def _pair_sum(v, lo):
    s0 = jnp.sum(jnp.where(lo, v, 0.0), axis=-1, keepdims=True)
    s1 = jnp.sum(jnp.where(lo, 0.0, v), axis=-1, keepdims=True)
    return jnp.where(lo, s0, s1)


def _headnorm_bwd(src, part, colblk, w, dys, D, name):
    S = src.shape[1]
    hd = w.shape[0]
    ts = _tile(S, 512, SUBLANES)
    w2 = jnp.tile(w, LANES // hd).reshape(1, LANES)
    n_dy = len(dys)

    def body(x_ref, w_ref, *rest):
        dy_refs, dx_ref, dw_ref = rest[:n_dy], rest[n_dy], rest[n_dy + 1]

        @pl.when(pl.program_id(0) == 0)
        def _():
            dw_ref[...] = jnp.zeros_like(dw_ref)

        lo = _head_masks((ts, LANES), hd)
        for t in range(D // LANES):
            cols = slice(t * LANES, (t + 1) * LANES)
            xt = x_ref[0, :, cols]
            dy = dy_refs[0][:, cols]
            for other in dy_refs[1:]:
                dy = dy + other[:, cols]
            r = lax.rsqrt(_pair_sum(xt * xt, lo) * (1.0 / hd) + RMS_EPS)
            xhat = xt * r
            dxhat = dy * w_ref[...]
            mean = _pair_sum(dxhat * xhat, lo) * (1.0 / hd)
            dx_ref[:, cols] = (r * (dxhat - xhat * mean)).astype(BF)
            dw_ref[:, cols] += jnp.broadcast_to(jnp.sum(dy * xhat, axis=0, keepdims=True), (SUBLANES, LANES))

    dx, dw = pl.pallas_call(
        body, name=name, grid=(S // ts,),
        in_specs=[pl.BlockSpec((1, ts, D), lambda s: (part, s, colblk)), pl.BlockSpec((1, LANES), lambda s: (0, 0))]
        + [pl.BlockSpec((ts, D), lambda s: (s, 0))] * n_dy,
        out_specs=[pl.BlockSpec((ts, D), lambda s: (s, 0)), pl.BlockSpec((SUBLANES, D), lambda s: (0, 0))],
        out_shape=[jax.ShapeDtypeStruct((S, D), BF), jax.ShapeDtypeStruct((SUBLANES, D), F32)],
        compiler_params=_params("arbitrary"),
    )(src, w2, *dys)
    return dx, jnp.sum(dw[0].reshape(D // hd, hd), axis=0)


def _tri(n, lower):
    r, c = lax.broadcasted_iota(jnp.int32, (n, n), 0), lax.broadcasted_iota(jnp.int32, (n, n), 1)
    return jnp.where((c <= r) if lower else (c >= r), 1.0, 0.0).astype(BF)


def _dot_exact(t, v):
    hi = v.astype(BF)
    r1 = v - hi.astype(F32)
    mid = r1.astype(BF)
    lo = (r1 - mid.astype(F32)).astype(BF)
    dot = lambda u: jnp.dot(t, u, preferred_element_type=F32)
    return dot(hi) + dot(mid) + dot(lo)


def _gate_fwd(kvf, b_pad, colblk, name):
    S = kvf.shape[1]
    ts = _tile(S, 512, SUBLANES)

    def body(f_ref, b_ref, c_ref, carry):
        @pl.when(pl.program_id(0) == 0)
        def _():
            carry[...] = jnp.zeros_like(carry)

        f = f_ref[0] + b_ref[...]
        ls = jnp.minimum(f, 0.0) - jnp.log1p(jnp.exp(-jnp.abs(f)))
        tri = _tri(ts, lower=True)
        c = _dot_exact(tri, ls) + carry[0:1, :]
        c_ref[...] = c
        carry[...] = jnp.broadcast_to(c[ts - 1:ts, :], carry.shape)

    return pl.pallas_call(
        body, name=name, grid=(S // ts,),
        in_specs=[pl.BlockSpec((1, ts, LANES), lambda s: (0, s, colblk)), pl.BlockSpec((1, LANES), lambda s: (0, 0))],
        out_specs=pl.BlockSpec((ts, LANES), lambda s: (s, 0)),
        out_shape=jax.ShapeDtypeStruct((S, LANES), F32),
        scratch_shapes=[pltpu.VMEM((SUBLANES, LANES), F32)],
        compiler_params=_params("arbitrary"),
    )(kvf, b_pad)


def _gate_bwd(dc, kvf, b_pad, colblk, name):
    S = kvf.shape[1]
    ts = _tile(S, 512, SUBLANES)
    n_s = S // ts

    def body(dc_ref, f_ref, b_ref, df_ref, db_ref, carry):
        @pl.when(pl.program_id(0) == 0)
        def _():
            carry[...] = jnp.zeros_like(carry)
            db_ref[...] = jnp.zeros_like(db_ref)

        tri = _tri(ts, lower=False)
        dls = _dot_exact(tri, dc_ref[...]) + carry[0:1, :]
        f = f_ref[0] + b_ref[...]
        df = dls * jax.nn.sigmoid(-f)
        df_ref[...] = df
        db_ref[...] += jnp.broadcast_to(jnp.sum(df, axis=0, keepdims=True), db_ref.shape)
        carry[...] = jnp.broadcast_to(dls[0:1, :], carry.shape)

    df, db = pl.pallas_call(
        body, name=name, grid=(n_s,),
        in_specs=[pl.BlockSpec((ts, LANES), lambda s: (n_s - 1 - s, 0)),
                  pl.BlockSpec((1, ts, LANES), lambda s: (0, n_s - 1 - s, colblk)),
                  pl.BlockSpec((1, LANES), lambda s: (0, 0))],
        out_specs=[pl.BlockSpec((ts, LANES), lambda s: (n_s - 1 - s, 0)),
                   pl.BlockSpec((SUBLANES, LANES), lambda s: (0, 0))],
        out_shape=[jax.ShapeDtypeStruct((S, LANES), F32), jax.ShapeDtypeStruct((SUBLANES, LANES), F32)],
        scratch_shapes=[pltpu.VMEM((SUBLANES, LANES), F32)],
        compiler_params=_params("arbitrary"),
    )(dc, kvf, b_pad)
    return df, db[0]


def _attn_tile(S):
    return _tile(S, 512, LANES)


def _split_heads(v, lo):
    zero = jnp.zeros_like(v)
    return jnp.where(lo, v, zero), jnp.where(lo, zero, v)


def _augment(src, part, colblk, c, mode, hd, D, name, norm_w=None, scale=1.0):
    S = src.shape[1]
    ts = _tile(S, 512, 2 * SUBLANES)
    w2 = jnp.tile(jnp.ones((hd,), F32) if norm_w is None else norm_w, LANES // hd).reshape(1, LANES)

    def body(b_ref, w_ref, c_ref, o0_ref, o1_ref):
        lane = lax.broadcasted_iota(jnp.int32, (ts, LANES), 1)
        lo = lane < hd
        cc = c_ref[...] * LOG2E
        for t in range(D // LANES):
            cols = slice(t * LANES, (t + 1) * LANES)
            bt = b_ref[0, :, cols]
            if norm_w is not None:
                r = lax.rsqrt(_pair_sum(bt * bt, lo) * (1.0 / hd) + RMS_EPS)
                bt = bt * r * w_ref[...] * scale
            bt = bt.astype(BF)
            for h, o_ref in ((0, o0_ref), (1, o1_ref)):
                first = hd if h == 0 else 0
                keep = (lane < hd) if h == 0 else (lane >= hd)
                if mode == "v":
                    vals = (1.0,)
                else:
                    col = cc[:, 2 * t + h:2 * t + h + 1]
                    hi = col.astype(BF).astype(F32)
                    mid = (col - hi).astype(BF).astype(F32)
                    pieces = (hi, mid, col - hi - mid)
                    vals = pieces + (1.0, 1.0, 1.0) if mode == "q" else (1.0, 1.0, 1.0) + tuple(-v for v in pieces)
                aug = jnp.zeros((ts, LANES), F32)
                for i, v in enumerate(vals):
                    aug = jnp.where(lane == first + i, v, aug)
                o_ref[:, cols] = jnp.where(keep, bt, aug.astype(BF))

    spec = pl.BlockSpec((ts, D), lambda s: (s, 0))
    return pl.pallas_call(
        body, name=name, grid=(S // ts,),
        in_specs=[pl.BlockSpec((1, ts, D), lambda s: (part, s, colblk)), pl.BlockSpec((1, LANES), lambda s: (0, 0)),
                  pl.BlockSpec((ts, LANES), lambda s: (s, 0))],
        out_specs=[spec, spec],
        out_shape=[jax.ShapeDtypeStruct((S, D), BF)] * 2,
        compiler_params=_params("parallel"),
    )(src, w2, c)


def _attn_fwd(qa, ka, va, qg, hd, name, gather=()):
    S, D = qa[0].shape
    P = D // LANES
    tq = _attn_tile(S)
    nq = S // tq
    n_g = len(gather)

    def body(q0_ref, q1_ref, k0_ref, k1_ref, v0_ref, v1_ref, g_ref, *rest):
        o_ref, og_ref, m_ref, l_ref = rest[n_g:n_g + 4]
        s_buf = rest[2 * n_g + 4]
        pair, qi = pl.program_id(0), pl.program_id(1)
        if n_g:
            start, pass_on, finish = _gather_phases([a.shape[0] for a in gather], rest[:n_g],
                                                    rest[n_g + 4:2 * n_g + 4], *rest[2 * n_g + 5:])
            pl.when((pair == 0) & (qi == 0))(start)
            pl.when((pair == P // 2) & (qi == 0))(pass_on)
        lo = _head_masks((tq, LANES), hd)
        qh = (q0_ref[...], q1_ref[...])
        k_refs, v_refs = (k0_ref, k1_ref), (v0_ref, v1_ref)
        causal = lax.broadcasted_iota(jnp.int32, (tq, tq), 1) <= lax.broadcasted_iota(jnp.int32, (tq, tq), 0)

        def scores(ki, slot):
            off = pl.multiple_of(ki * tq, tq)
            for h in range(2):
                s_buf[slot, h] = lax.dot_general(qh[h], k_refs[h][pl.ds(off, tq), :], NT_DIMS,
                                                 preferred_element_type=F32)

        def consume(ki, slot, carry, masked):
            off = pl.multiple_of(ki * tq, tq)
            out = []
            for h in range(2):
                m, acc = carry[h]
                s = s_buf[slot, h]
                if masked:
                    s = jnp.where(causal, s, -jnp.inf)
                m_new = jnp.maximum(m, jnp.ceil(jnp.max(s, axis=-1, keepdims=True)))
                p = jnp.exp2(s - m_new)
                acc = jnp.exp2(m - m_new) * acc + jnp.dot(p.astype(BF), v_refs[h][pl.ds(off, tq), :],
                                                          preferred_element_type=F32)
                out.append((m_new, acc))
            return tuple(out)

        def step(j, carry):
            scores(2 * j + 1, 1)
            carry = consume(2 * j, 0, carry, False)
            scores(2 * j + 2, 0)
            return consume(2 * j + 1, 1, carry, False)

        def finish_even(carry):
            return consume(qi, 0, carry, True)

        def finish_odd(carry):
            scores(qi, 1)
            return consume(qi, 1, consume(qi - 1, 0, carry, False), True)

        init = tuple((jnp.full((tq, 1), -jnp.inf, F32), jnp.zeros((tq, LANES), F32)) for _ in range(2))
        scores(0, 0)
        carry = lax.fori_loop(0, qi // 2, step, init)
        (m0, a0), (m1, a1) = lax.cond(qi % 2 == 0, finish_even, finish_odd, carry)
        l0, l1 = a0[:, hd:hd + 1], a1[:, 0:1]
        o = jnp.where(lo, a0 / l0, a1 / l1)
        o_ref[...] = o
        og_ref[...] = (o * jax.nn.sigmoid(g_ref[0])).astype(BF)
        lane2 = lax.broadcasted_iota(jnp.int32, (tq, 2), 1)
        m_ref[0] = jnp.where(lane2 == 0, m0, m1)
        l_ref[0] = jnp.where(lane2 == 0, l0, l1)
        if n_g:
            pl.when((pair == P - 1) & (qi == nq - 1))(finish)

    tile = pl.BlockSpec((tq, LANES), lambda p, i: (i, p))
    whole = pl.BlockSpec((S, LANES), lambda p, i: (0, p))
    stat = pl.BlockSpec((1, tq, 2), lambda p, i: (p, i, 0))
    sems = [pltpu.SemaphoreType.DMA((6 * n_g,)), pltpu.SemaphoreType.DMA((6 * n_g,))] if n_g else []
    outs = pl.pallas_call(
        body, name=name, grid=(P, nq),
        in_specs=[tile, tile, whole, whole, whole, whole, pl.BlockSpec((1, tq, LANES), lambda p, i: (1, i, p))]
        + [ANY] * n_g,
        out_specs=[tile, tile, stat, stat] + [ANY] * n_g,
        out_shape=[jax.ShapeDtypeStruct((S, D), F32), jax.ShapeDtypeStruct((S, D), BF),
                   jax.ShapeDtypeStruct((P, S, 2), F32), jax.ShapeDtypeStruct((P, S, 2), F32)]
        + [jax.ShapeDtypeStruct((N_CHIPS,) + a.shape, a.dtype) for a in gather],
        scratch_shapes=[pltpu.VMEM((2, 2, tq, tq), F32)] + sems,
        compiler_params=_params("arbitrary" if n_g else "parallel", "arbitrary"),
    )(*qa, *ka, *va, qg, *gather)
    return tuple(outs[:4]) + (_fill_own(outs[4:], gather),)


def _attn_out_bwd(dy, w_out, o, qg, l, hd, name):
    S, D = o.shape
    P = D // LANES
    ts = _tile(S, 512, 2 * SUBLANES)

    def body(dy_ref, w_ref, o_ref, g_ref, l_ref, do_ref, dg_ref, e_ref):
        lo = _head_masks((ts, LANES), hd)
        lane2 = lax.broadcasted_iota(jnp.int32, (ts, 2), 1)
        dog = lax.dot_general(dy_ref[...].astype(BF), w_ref[...], NT_DIMS, preferred_element_type=F32)
        for t in range(P):
            cols = slice(t * LANES, (t + 1) * LANES)
            sg = jax.nn.sigmoid(g_ref[0, :, cols])
            dog_t, o_t, l_t = dog[:, cols], o_ref[:, cols], l_ref[t]
            g = (dog_t * sg / jnp.where(lo, l_t[:, 0:1], l_t[:, 1:2])).astype(BF)
            do_ref[:, cols] = g
            dg_ref[:, cols] = (dog_t * o_t * sg * (1.0 - sg)).astype(BF)
            prod = g.astype(F32) * o_t
            e0 = jnp.sum(jnp.where(lo, prod, 0.0), axis=-1, keepdims=True)
            e1 = jnp.sum(jnp.where(lo, 0.0, prod), axis=-1, keepdims=True)
            e_ref[t] = jnp.where(lane2 == 0, e0, e1)

    rows = pl.BlockSpec((ts, D), lambda s: (s, 0))
    stat = pl.BlockSpec((P, ts, 2), lambda s: (0, s, 0))
    return pl.pallas_call(
        body, name=name, grid=(S // ts,),
        in_specs=[rows, pl.BlockSpec(w_out.shape, lambda s: (0, 0)), rows,
                  pl.BlockSpec((1, ts, D), lambda s: (1, s, 0)), stat],
        out_specs=[rows, rows, stat],
        out_shape=[jax.ShapeDtypeStruct((S, D), BF), jax.ShapeDtypeStruct((S, D), BF),
                   jax.ShapeDtypeStruct((P, S, 2), F32)],
        compiler_params=_params("parallel"),
    )(dy, w_out, o, qg, l)


def _attn_bwd(qa, ka, vb, g, m_row, e_row, hd, name, exchange=()):
    S, D = vb.shape
    P = D // LANES
    tk = _attn_tile(S)
    nk = S // tk
    scale = hd ** -0.5
    n_x = len(exchange)

    def body(q0_ref, q1_ref, g_ref, k0_ref, k1_ref, v_ref, m_ref, e_ref, *rest):
        dq_ref, dk_ref, dv_ref, dc_ref = rest[n_x:n_x + 4]
        st_buf, dp_buf = rest[2 * n_x + 4:2 * n_x + 6]
        pair, ki = pl.program_id(0), pl.program_id(1)
        if n_x:
            copies = lambda: _chip_copies(rest[:n_x], rest[n_x + 4:2 * n_x + 4], *rest[2 * n_x + 6:])

            @pl.when((pair == 0) & (ki == 0))
            def _():
                for cp in copies():
                    cp.start()

        @pl.when(ki == 0)
        def _():
            dq_ref[...] = jnp.zeros_like(dq_ref)

        lo = _head_masks((tk, LANES), hd)
        kh = (k0_ref[...], k1_ref[...])
        q_refs = (q0_ref, q1_ref)
        vh = _split_heads(v_ref[...], lo)
        causal_t = lax.broadcasted_iota(jnp.int32, (tk, tk), 0) <= lax.broadcasted_iota(jnp.int32, (tk, tk), 1)

        def stage(qi, slot):
            off = pl.multiple_of(qi * tk, tk)
            gb = g_ref[pl.ds(off, tk), :]
            for h in range(2):
                st_buf[slot, h] = lax.dot_general(kh[h], q_refs[h][pl.ds(off, tk), :], NT_DIMS,
                                                  preferred_element_type=F32)
                dp_buf[slot, h] = lax.dot_general(vh[h], gb, NT_DIMS, preferred_element_type=F32)

        def consume(qi, slot, carry, masked):
            off = pl.multiple_of(qi * tk, tk)
            gb = g_ref[pl.ds(off, tk), :]
            m_t, e_t = m_ref[0, qi], e_ref[0, qi]
            out, dq_parts = [], []
            for h in range(2):
                dk, dv, dc = carry[h]
                qb = q_refs[h][pl.ds(off, tk), :]
                pt = jnp.exp2(st_buf[slot, h] - m_t[h:h + 1, :])
                if masked:
                    pt = jnp.where(causal_t, pt, 0.0)
                pb = pt.astype(BF)
                dv = dv + jnp.dot(pb, gb, preferred_element_type=F32)
                dst = pb.astype(F32) * (dp_buf[slot, h] - e_t[h:h + 1, :])
                db = dst.astype(BF)
                dk = dk + jnp.dot(db, qb, preferred_element_type=F32)
                dc = dc - jnp.sum(dst, axis=-1, keepdims=True)
                dq_parts.append(lax.dot_general(db, kh[h], TN_DIMS, preferred_element_type=F32))
                out.append((dk, dv, dc))
            dq_ref[pl.ds(off, tk), :] += jnp.where(lo, dq_parts[0], dq_parts[1]) * scale
            return tuple(out)

        n_after = nk - 1 - ki

        def step(j, carry):
            b = ki + 1 + 2 * j
            stage(b + 1, 0)
            carry = consume(b, 1, carry, False)
            stage(b + 2, 1)
            return consume(b + 1, 0, carry, False)

        def rest_one(carry):
            return consume(nk - 1, 1, carry, False)

        def rest_two(carry):
            stage(nk - 1, 0)
            return consume(nk - 1, 0, consume(nk - 2, 1, carry, False), False)

        init = tuple((jnp.zeros((tk, LANES), F32), jnp.zeros((tk, LANES), F32), jnp.zeros((tk, 1), F32))
                     for _ in range(2))
        stage(ki, 0)
        stage(jnp.minimum(ki + 1, nk - 1), 1)
        carry = consume(ki, 0, init, True)
        carry = lax.fori_loop(0, (n_after - 1) // 2, step, carry)
        which = jnp.where(n_after == 0, 0, 2 - n_after % 2)
        (dk0, dv0, dc0), (dk1, dv1, dc1) = lax.switch(which, [lambda c: c, rest_one, rest_two], carry)
        dk_ref[...] = jnp.where(lo, dk0, dk1) * (1.0 / LOG2E)
        dv_ref[...] = jnp.where(lo, dv0, dv1)
        lane2 = lax.broadcasted_iota(jnp.int32, (tk, 2), 1)
        dc_ref[0] = jnp.where(lane2 == 0, dc0, dc1)
        if n_x:
            @pl.when((pair == P - 1) & (ki == nk - 1))
            def _():
                for cp in copies():
                    cp.wait()

    tile = pl.BlockSpec((tk, LANES), lambda p, i: (i, p))
    whole = pl.BlockSpec((S, LANES), lambda p, i: (0, p))
    row_spec = pl.BlockSpec((1, nk, 2, tk), lambda p, i: (p, 0, 0, 0))
    sems = [pltpu.SemaphoreType.DMA((3 * n_x,)), pltpu.SemaphoreType.DMA((3 * n_x,))] if n_x else []
    outs = pl.pallas_call(
        body, name=name, grid=(P, nk),
        in_specs=[whole, whole, whole, tile, tile, tile, row_spec, row_spec] + [ANY] * n_x,
        out_specs=[whole, tile, tile, pl.BlockSpec((1, tk, 2), lambda p, i: (p, i, 0))] + [ANY] * n_x,
        out_shape=[jax.ShapeDtypeStruct((S, D), F32), jax.ShapeDtypeStruct((S, D), F32),
                   jax.ShapeDtypeStruct((S, D), F32), jax.ShapeDtypeStruct((P, S, 2), F32)]
        + [jax.ShapeDtypeStruct((3,) + a.shape[1:], a.dtype) for a in exchange],
        scratch_shapes=[pltpu.VMEM((2, 2, tk, tk), F32), pltpu.VMEM((2, 2, tk, tk), F32)] + sems,
        compiler_params=_params("arbitrary" if n_x else "parallel", "arbitrary"),
    )(*qa, g, *ka, vb, m_row, e_row, *exchange)
    return tuple(outs[:4]) + (list(outs[4:]),)


def _loss_head(y, t, name):
    S, D = y.shape
    ts = _tile(S, 512, SUBLANES)

    def body(y_ref, t_ref, dy_ref, l_ref):
        @pl.when(pl.program_id(0) == 0)
        def _():
            l_ref[...] = jnp.zeros_like(l_ref)

        e = y_ref[...] - t_ref[...]
        dy_ref[...] = e * (1.0 / D)
        part = 0.5 * jnp.sum(jnp.mean(e * e, axis=-1, keepdims=True), axis=0, keepdims=True)
        l_ref[...] += jnp.broadcast_to(part, l_ref.shape)

    return pl.pallas_call(
        body, name=name, grid=(S // ts,),
        in_specs=[pl.BlockSpec((ts, D), lambda s: (s, 0)), pl.BlockSpec((ts, D), lambda s: (s, 0))],
        out_specs=[pl.BlockSpec((ts, D), lambda s: (s, 0)), pl.BlockSpec((SUBLANES, LANES), lambda s: (0, 0))],
        out_shape=[jax.ShapeDtypeStruct((S, D), F32), jax.ShapeDtypeStruct((SUBLANES, LANES), F32)],
        compiler_params=_params("arbitrary"),
    )(y, t)


def _adamw(w, g, m, v, name):
    shape = w.shape
    cols = shape[-1]
    as2d = lambda a: a.reshape(-1, cols)
    rows = as2d(w).shape[0]
    tr = _tile(rows, 256, SUBLANES) if rows % SUBLANES == 0 else rows
    c1 = 1.0 - ADAM_B1 ** ADAM_STEP
    c2 = 1.0 - ADAM_B2 ** ADAM_STEP

    def body(w_ref, g_ref, m_ref, v_ref, d_ref, nm_ref, nv_ref):
        gg = g_ref[...]
        nm = ADAM_B1 * m_ref[...] + (1.0 - ADAM_B1) * gg
        nv = ADAM_B2 * v_ref[...] + (1.0 - ADAM_B2) * (gg * gg)
        d_ref[...] = -ADAM_LR * ((nm / c1) / (jnp.sqrt(nv / c2) + ADAM_EPS) + ADAM_WD * w_ref[...])
        nm_ref[...] = nm
        nv_ref[...] = nv

    spec = pl.BlockSpec((tr, cols), lambda r: (r, 0))
    outs = pl.pallas_call(
        body, name=name, grid=(rows // tr,), in_specs=[spec] * 4, out_specs=[spec] * 3,
        out_shape=[jax.ShapeDtypeStruct((rows, cols), F32)] * 3,
        compiler_params=_params("parallel"),
    )(as2d(w), as2d(g), as2d(m), as2d(v))
    return tuple(o.reshape(shape) for o in outs)


def _place():
    return lax.axis_index("x"), lax.axis_index("y"), lax.axis_index("c")


def _other_chips(x, y):
    return [(1 - x, y), (x, 1 - y), (1 - x, 1 - y)]


def _remote(src, dst, send_sems, recv_sems, k, to):
    return pltpu.make_async_remote_copy(src_ref=src, dst_ref=dst, send_sem=send_sems.at[k], recv_sem=recv_sems.at[k],
                                        device_id=to, device_id_type=MESH)


def _half(c, rh):
    return pl.ds(pl.multiple_of(c * rh, 2 * SUBLANES), rh)


def _gather_phases(rows, src, dst, send_sems, recv_sems):
    n = len(rows)
    x, y, c = _place()
    me = 2 * x + y
    sib = (x, y, 1 - c)
    chips = _other_chips(x, y)
    rh = [r // 2 for r in rows]

    def first():
        return [_remote(src[g].at[_half(c, rh[g])], dst[g].at[me, _half(c, rh[g])], send_sems, recv_sems,
                        6 * g + j, (cx, cy, c)) for j, (cx, cy) in enumerate(chips) for g in range(n)]

    def landed(j, g, core):
        cx, cy = chips[j]
        return dst[g].at[2 * cx + cy, _half(core, rh[g])]

    def passed():
        return [_remote(landed(j, g, c), landed(j, g, c), send_sems, recv_sems, 6 * g + 3 + j, sib)
                for j in range(3) for g in range(n)]

    def start():
        for cp in first():
            cp.start()

    def pass_on():
        cps = passed()
        for j in range(3):
            for g in range(n):
                _remote(landed(j, g, c), landed(j, g, c), send_sems, recv_sems, 6 * g + j, sib).wait_recv()
                cps[j * n + g].start()

    def finish():
        for j in range(3):
            for g in range(n):
                _remote(landed(j, g, 1 - c), landed(j, g, 1 - c), send_sems, recv_sems, 6 * g + 3 + j, sib).wait_recv()
        for cp in first() + passed():
            cp.wait_send()

    return start, pass_on, finish


def _fill_own(outs, srcs):
    x, y, _ = _place()
    return [lax.dynamic_update_slice_in_dim(o, a[None], 2 * x + y, axis=0) for o, a in zip(outs, srcs)]


def _allgather_weights(srcs):
    n = len(srcs)

    def body(*refs):
        for step in _gather_phases([a.shape[0] for a in srcs], refs[:n], refs[n:2 * n], *refs[2 * n:]):
            step()

    outs = pl.pallas_call(
        body, name="allgather_weights", in_specs=[ANY] * n, out_specs=[ANY] * n,
        out_shape=[jax.ShapeDtypeStruct((N_CHIPS,) + a.shape, a.dtype) for a in srcs],
        scratch_shapes=[pltpu.SemaphoreType.DMA((6 * n,)), pltpu.SemaphoreType.DMA((6 * n,))],
    )(*srcs)
    return _fill_own(outs, srcs)


def _pair_exchange(gs, name):
    n = len(gs)

    def body(*refs):
        g_refs, t_refs, (send_sems, recv_sems) = refs[:n], refs[n:2 * n], refs[2 * n:]
        x, y, c = _place()
        cps = [_remote(g_refs[g].at[k, 1 - c], t_refs[g].at[k], send_sems, recv_sems, N_CHIPS * g + k, (x, y, 1 - c))
               for g in range(n) for k in range(N_CHIPS)]
        for cp in cps:
            cp.start()
        for cp in cps:
            cp.wait()

    return pl.pallas_call(
        body, name=name, in_specs=[ANY] * n, out_specs=[ANY] * n,
        out_shape=[jax.ShapeDtypeStruct((a.shape[0],) + a.shape[2:], a.dtype) for a in gs],
        scratch_shapes=[pltpu.SemaphoreType.DMA((N_CHIPS * n,)), pltpu.SemaphoreType.DMA((N_CHIPS * n,))],
    )(*gs)


def _pair_add(g, t, c, name):
    n, _, rh, W = g.shape
    tr = _tile(rh, 256, 2 * SUBLANES)

    def body(c_ref, g_ref, t_ref, o_ref):
        o_ref[...] = (g_ref[0] + t_ref[...]).astype(BF)

    return pl.pallas_call(
        body, name=name,
        grid_spec=pltpu.PrefetchScalarGridSpec(
            num_scalar_prefetch=1, grid=(n, rh // tr),
            in_specs=[pl.BlockSpec((1, 1, tr, W), lambda k, i, c_ref: (k, c_ref[0], i, 0)),
                      pl.BlockSpec((1, tr, W), lambda k, i, c_ref: (k, i, 0))],
            out_specs=pl.BlockSpec((1, tr, W), lambda k, i, c_ref: (k, i, 0))),
        out_shape=jax.ShapeDtypeStruct((n, rh, W), BF),
        compiler_params=_params("parallel", "parallel"),
    )(c.reshape(1).astype(jnp.int32), g, t)


def _chip_copies(a_refs, t_refs, send_sems, recv_sems):
    x, y, c = _place()
    return [_remote(a_refs[g].at[2 * cx + cy], t_refs[g].at[j], send_sems, recv_sems, 3 * g + j, (cx, cy, c))
            for j, (cx, cy) in enumerate(_other_chips(x, y)) for g in range(len(a_refs))]


def _chip_exchange(parts, name):
    n = len(parts)

    def body(*refs):
        cps = _chip_copies(refs[:n], refs[n:2 * n], *refs[2 * n:])
        for cp in cps:
            cp.start()
        for cp in cps:
            cp.wait()

    return pl.pallas_call(
        body, name=name, in_specs=[ANY] * n, out_specs=[ANY] * n,
        out_shape=[jax.ShapeDtypeStruct((3,) + a.shape[1:], a.dtype) for a in parts],
        scratch_shapes=[pltpu.SemaphoreType.DMA((3 * n,)), pltpu.SemaphoreType.DMA((3 * n,))],
    )(*parts)


def _chip_add(g, t1, t2, c, me, name):
    _, _, rh, W = g.shape
    tr = _tile(rh, 256, 2 * SUBLANES)

    def body(c_ref, me_ref, g_ref, t1_ref, t2_ref, o_ref):
        own = g_ref[0, 0] + t1_ref[0]
        o_ref[...] = own + t2_ref[0].astype(F32) + t2_ref[1].astype(F32) + t2_ref[2].astype(F32)

    return pl.pallas_call(
        body, name=name,
        grid_spec=pltpu.PrefetchScalarGridSpec(
            num_scalar_prefetch=2, grid=(rh // tr,),
            in_specs=[pl.BlockSpec((1, 1, tr, W), lambda i, c_ref, me_ref: (me_ref[0], c_ref[0], i, 0)),
                      pl.BlockSpec((1, tr, W), lambda i, c_ref, me_ref: (me_ref[0], i, 0)),
                      pl.BlockSpec((3, tr, W), lambda i, c_ref, me_ref: (0, i, 0))],
            out_specs=pl.BlockSpec((tr, W), lambda i, c_ref, me_ref: (i, 0))),
        out_shape=jax.ShapeDtypeStruct((rh, W), F32),
        compiler_params=_params("parallel"),
    )(c.reshape(1).astype(jnp.int32), me.reshape(1).astype(jnp.int32), g, t1, t2)


def _pair_share(hs, name):
    n = len(hs)

    def body(*refs):
        h_refs, f_refs, (send_sems, recv_sems) = refs[:n], refs[n:2 * n], refs[2 * n:]
        x, y, c = _place()
        cps = [_remote(h_refs[g], f_refs[g], send_sems, recv_sems, g, (x, y, 1 - c)) for g in range(n)]
        for cp in cps:
            cp.start()
        for cp in cps:
            cp.wait()

    return pl.pallas_call(
        body, name=name, in_specs=[ANY] * n, out_specs=[ANY] * n,
        out_shape=[jax.ShapeDtypeStruct(a.shape, a.dtype) for a in hs],
        scratch_shapes=[pltpu.SemaphoreType.DMA((n,)), pltpu.SemaphoreType.DMA((n,))],
    )(*hs)


def _allreduce_small(pack, name):
    rows, W = pack.shape

    def body(p_ref, o_ref, buf, send_sems, recv_sems):
        x, y, c = _place()
        me = 4 * x + 2 * y + c
        buf[me] = p_ref[...]
        cps = []
        for r in range(1, 8):
            fx, fy, fc = (r >> 2) & 1, (r >> 1) & 1, r & 1
            to = (1 - x if fx else x, 1 - y if fy else y, 1 - c if fc else c)
            cps.append(_remote(p_ref, buf.at[me], send_sems, recv_sems, r - 1, to))
        for cp in cps:
            cp.start()
        for r in range(1, 8):
            fx, fy, fc = (r >> 2) & 1, (r >> 1) & 1, r & 1
            frm = 4 * (1 - x if fx else x) + 2 * (1 - y if fy else y) + (1 - c if fc else c)
            _remote(p_ref, buf.at[frm], send_sems, recv_sems, r - 1, (x, y, c)).wait_recv()
        for cp in cps:
            cp.wait_send()
        acc = buf[0]
        for i in range(1, 8):
            acc = acc + buf[i]
        o_ref[...] = acc

    return pl.pallas_call(
        body, name=name, in_specs=[VMEM], out_specs=VMEM,
        out_shape=jax.ShapeDtypeStruct((rows, W), F32),
        scratch_shapes=[pltpu.VMEM((8, rows, W), F32), pltpu.SemaphoreType.DMA((7,)), pltpu.SemaphoreType.DMA((7,))],
    )(pack)


def _width_groups(arrs):
    widths = []
    for a in arrs:
        if a.shape[-1] not in widths:
            widths.append(a.shape[-1])
    return [[i for i, a in enumerate(arrs) if a.shape[-1] == w] for w in widths]


def _rows2d(a):
    return a.reshape(-1, a.shape[-1])


def _split_rows_like(buf, like, lead=()):
    out, off = [], 0
    for a in like:
        n = a.size // a.shape[-1]
        out.append(buf[..., off:off + n, :].reshape(tuple(lead) + a.shape))
        off += n
    return out


def _join_cols(g):
    nd = g.ndim
    return jnp.moveaxis(g, 0, nd - 2).reshape(g.shape[1:-1] + (N_CHIPS * g.shape[-1],))


def _join_rows(g):
    return jnp.moveaxis(g, 0, 1).reshape(g.shape[1], N_CHIPS * g.shape[2], g.shape[3])


def _row_layout(a, tq):
    P, S, _ = a.shape
    return a.reshape(P, S // tq, tq, 2).transpose(0, 1, 3, 2)


def _pad_row(v, width=FLAT_W):
    flat = v.reshape(-1)
    rows = -(-flat.shape[0] // width)
    return jnp.pad(flat, (0, rows * width - flat.shape[0]))


def kernel(x, attn_norm, ffn_norm, a_w_in, a_conv, a_w_out, kv_norm, w_kvf, b_f, k_norm, b_w_qg, q_norm, b_w_out, ffn_w_up, ffn_conv, ffn_w_down, loss_target, m_attn_norm, m_ffn_norm, m_a_w_in, m_a_conv, m_a_w_out, m_kv_norm, m_w_kvf, m_b_f, m_k_norm, m_b_w_qg, m_q_norm, m_b_w_out, m_ffn_w_up, m_ffn_conv, m_ffn_w_down, v_attn_norm, v_ffn_norm, v_a_w_in, v_a_conv, v_a_w_out, v_kv_norm, v_w_kvf, v_b_f, v_k_norm, v_b_w_qg, v_q_norm, v_b_w_out, v_ffn_w_up, v_ffn_conv, v_ffn_w_down):
    xs = x[0]
    S, D = xs.shape
    H, hd = b_f.shape[0], k_norm.shape[0]
    depth = attn_norm.shape[0]
    n_a = a_w_in.shape[0]
    P = D // LANES
    assert LANES == 2 * hd and H * hd == D, "the attention kernels hold two heads per lane tile"
    mx, my, mc = _place()
    chip = 2 * mx + my

    big = [a_w_in, a_w_out, w_kvf, b_w_qg, b_w_out, ffn_w_up, ffn_w_down]
    groups = _width_groups(big)
    assert n_a >= 2 and depth - n_a >= 2, "the hosted weight gathers are laid out for two layers of each kind"
    first = [a_w_in[:1]]
    behind_a_in = {0: [a_w_out[:1], ffn_w_up[:1]], 1: [ffn_w_up[1:n_a]]}
    behind_ffn_up = {0: [ffn_w_down[:1], a_w_in[1:], a_w_out[1:]], 1: [ffn_w_down[1:n_a], w_kvf, b_w_qg[:1]]}
    late = [b_w_qg[1:], b_w_out, ffn_w_up[n_a:], ffn_w_down[n_a:]]

    def packed(ws):
        idx_groups = _width_groups(ws)
        return [jnp.concatenate([_rows2d(ws[i]).astype(BF) for i in idx]) for idx in idx_groups], idx_groups

    def unpacked(bufs, idx_groups, ws):
        out = [None] * len(ws)
        for idx, buf in zip(idx_groups, bufs):
            for i, part in zip(idx, _split_rows_like(buf, [ws[i] for i in idx], (N_CHIPS,))):
                out[i] = part
        return out

    first_src, first_groups = packed(first)
    late_src, late_groups = packed(late)
    (g_in,) = unpacked(_allgather_weights(first_src), first_groups, first)
    wa_in, wa_out, w_up, w_down, wb_qg, wb_out = list(_join_cols(g_in)), [], [], [], [], []
    kvf_cols = 2 * D + LANES

    def placed(shard):
        full = jnp.zeros(shard.shape[:-1] + (N_CHIPS, shard.shape[-1]), F32)
        full = lax.dynamic_update_slice_in_dim(full, shard[..., None, :], chip, axis=full.ndim - 2)
        return jnp.where(mc == 0, full, 0.0).reshape(-1)

    conv_pack = jnp.concatenate([_pad_row(placed(a_conv)), _pad_row(placed(ffn_conv))]).reshape(-1, FLAT_W)
    conv_full = _allreduce_small(conv_pack, "allgather_conv_taps").reshape(-1)
    n_ac = a_conv.size * N_CHIPS
    a_conv_f = conv_full[:n_ac].reshape(a_conv.shape[:-1] + (-1,))
    off = _pad_row(placed(a_conv)).shape[0]
    ffn_conv_f = conv_full[off:off + ffn_conv.size * N_CHIPS].reshape(ffn_conv.shape[:-1] + (-1,))
    F = ffn_conv_f.shape[-1]

    b_pad = jnp.pad(b_f, (0, LANES - H)).reshape(1, LANES)
    gate_blk = 2 * D // LANES
    tq = _attn_tile(S)
    scale = hd ** -0.5

    saved = []
    cur = xs
    kv = None
    for l in range(depth):
        rec = {"x_in": cur}
        if l < n_a:
            ws = behind_a_in.get(l, [])
            srcs, idx_groups = packed(ws) if ws else ((), [])
            proj, xn, z, bufs = _mixer_in_fwd(cur, attn_norm[l], wa_in[l], a_conv_f[l], f"a_in_{l}", gather=srcs)
            if l == 0:
                g_out, g_up = unpacked(bufs, idx_groups, ws)
                wa_out += list(_join_rows(g_out))
                w_up += list(_join_cols(g_up))
            if l == 1:
                (g_up,) = unpacked(bufs, idx_groups, ws)
                w_up += list(_join_cols(g_up))
            mid = _matmul_residual(z, wa_out[l], cur, f"a_out_{l}")
            rec.update(proj=proj, xn=xn, z=z)
        else:
            j = l - n_a
            if kv is None:
                kvf, hn = _norm_matmul(cur, kv_norm, wkvf, 1, F32, "kvf_proj")
                vb = kvf[0, :, D:2 * D].astype(BF)
                cgate = _gate_fwd(kvf, b_pad, gate_blk, "gate_cumsum")
                kv = dict(kvf=kvf, hn=hn, vb=vb, cgate=cgate, x_in=cur, dk=[], dv=[], dc=[],
                          ka=_augment(kvf, 0, 0, cgate, "k", hd, D, "k_augment", norm_w=k_norm),
                          va=_augment(kvf, 0, 1, cgate, "v", hd, D, "v_augment"))
            qg, xn = _norm_matmul(cur, attn_norm[l], wb_qg[j], 2, F32, f"qg_proj_{j}")
            qa = _augment(qg, 0, 0, kv["cgate"], "q", hd, D, f"q_augment_{j}", norm_w=q_norm[j], scale=scale * LOG2E)
            o, og, m_max, l_sum, late_bufs = _attn_fwd(qa, kv["ka"], kv["va"], qg, hd, f"attn_fwd_{j}",
                                                       gather=late_src if j == 0 else ())
            if j == 0:
                g_qg, g_bout, g_up, g_down = unpacked(late_bufs, late_groups, late)
                wb_qg += list(_join_cols(g_qg))
                wb_out += list(_join_rows(g_bout))
                w_up += list(_join_cols(g_up))
                w_down += list(_join_rows(g_down))
            mid = _matmul_residual(og, wb_out[j], cur, f"b_out_{j}")
            rec.update(qg=qg, xn=xn, qa=qa, o=o, og=og, m=m_max, l=l_sum)
        ws = behind_ffn_up.get(l, [])
        srcs, idx_groups = packed(ws) if ws else ((), [])
        up, xn2, z2, bufs = _ffn_up_fwd(mid, ffn_norm[l], w_up[l], ffn_conv_f[l], f"ffn_up_{l}", gather=srcs)
        if l == 0:
            g_down, g_in, g_out = unpacked(bufs, idx_groups, ws)
            w_down += list(_join_rows(g_down))
            wa_in += list(_join_cols(g_in))
            wa_out += list(_join_rows(g_out))
        if l == 1:
            g_down, g_kvf, g_qg = unpacked(bufs, idx_groups, ws)
            w_down += list(_join_rows(g_down))
            wkvf = jnp.pad(_join_cols(g_kvf), ((0, 0), (0, kvf_cols - (2 * D + H))))
            wb_qg += list(_join_cols(g_qg))
        cur = _matmul_residual(z2, w_down[l], mid, f"ffn_down_{l}")
        rec.update(x_mid=mid, up=up, xn2=xn2, z2=z2)
        saved.append(rec)

    dy, loss_part = _loss_head(cur, loss_target[0], "loss_head")

    def cols_of(g, k):
        c = g.shape[-1] // N_CHIPS
        return g[:, k * c:(k + 1) * c]

    def rows_of(g, k):
        r = g.shape[0] // N_CHIPS
        return g[k * r:(k + 1) * r]

    def reduce_scatter(items, tag, host=None):
        keys = list(items)
        likes = [items[k][1](items[k][0], 0) for k in keys]
        idx_groups = _width_groups(likes)

        def group_buffer(idx):
            rows = [items[keys[i]][1](items[keys[i]][0], k) for k in range(N_CHIPS) for i in idx]
            buf = jnp.concatenate(rows)
            return buf.reshape(N_CHIPS, 2, buf.shape[0] // (2 * N_CHIPS), buf.shape[1])

        g4 = [group_buffer(idx) for idx in idx_groups]
        from_sibling = _pair_exchange(g4, f"grad_pair_exchange_{tag}")
        parts = [_pair_add(g, t, mc, f"grad_pair_add_{tag}{n}") for n, (g, t) in enumerate(zip(g4, from_sibling))]
        from_chips = _chip_exchange(parts, f"grad_chip_exchange_{tag}") if host is None else host(parts)
        mine = [_chip_add(g, t1, t2, mc, chip, f"grad_chip_add_{tag}{n}") for n, (g, t1, t2) in
                enumerate(zip(g4, from_sibling, from_chips))]
        theirs = _pair_share(mine, f"grad_pair_share_{tag}")
        out = {}
        for idx, m_half, t_half in zip(idx_groups, mine, theirs):
            shard = jnp.where(mc == 0, jnp.concatenate([m_half, t_half]), jnp.concatenate([t_half, m_half]))
            for i, part in zip(idx, _split_rows_like(shard, [likes[i] for i in idx])):
                out[keys[i]] = part
        return out

    reduced = {}

    g_attn_norm, g_ffn_norm = [None] * depth, [None] * depth
    g_a_in, g_a_conv, g_a_out = [None] * n_a, [None] * n_a, [None] * n_a
    g_qg, g_qn, g_bo = [None] * (depth - n_a), [None] * (depth - n_a), [None] * (depth - n_a)
    g_up, g_fc, g_down = [None] * depth, [None] * depth, [None] * depth
    for l in reversed(range(depth)):
        rec = saved[l]
        dup, g_fc[l] = _ffn_mid_bwd(rec["up"], dy, w_down[l], ffn_conv_f[l], f"ffn_mid_bwd_{l}")
        g_down[l] = _wgrad(rec["z2"], dy[None], f"ffn_down_wgrad_{l}")
        g_up[l] = _wgrad(rec["xn2"], dup, f"ffn_up_wgrad_{l}")
        dy, g_ffn_norm[l] = _dnorm(dup, w_up[l], rec["x_mid"], ffn_norm[l], dy, f"ffn_up_bwd_{l}")
        if l < n_a:
            dproj, g_a_conv[l] = _mixer_mid_bwd(rec["proj"], dy, wa_out[l], a_conv_f[l], f"a_mid_bwd_{l}")
            g_a_out[l] = _wgrad(rec["z"], dy[None], f"a_out_wgrad_{l}")
            g_a_in[l] = _wgrad(rec["xn"], dproj, f"a_in_wgrad_{l}")
            dy, g_attn_norm[l] = _dnorm(dproj, wa_in[l], rec["x_in"], attn_norm[l], dy, f"a_in_bwd_{l}")
        else:
            j = l - n_a
            g_out, dgate, evec = _attn_out_bwd(dy, wb_out[j], rec["o"], rec["qg"], rec["l"], hd,
                                               f"attn_gate_bwd_{j}")
            g_bo[j] = _wgrad(rec["og"], dy[None], f"b_out_wgrad_{j}")
            ready = {(5, l): (g_up[l], cols_of), (6, l): (g_down[l], rows_of), (4, j): (g_bo[j], rows_of)}
            if j + 1 < depth - n_a:
                ready[(3, j + 1)] = (g_qg[j + 1], cols_of)
            held = {}

            def behind_attn_bwd(parts):
                *held["grads"], exchanged = _attn_bwd(rec["qa"], kv["ka"], kv["vb"], g_out, _row_layout(rec["m"], tq),
                                                      _row_layout(evec, tq), hd, f"attn_bwd_{j}", exchange=parts)
                return exchanged

            reduced.update(reduce_scatter(ready, f"b{j}", behind_attn_bwd))
            dqn, dk, dv, dc = held["grads"]
            kv["dk"].append(dk)
            kv["dv"].append(dv)
            kv["dc"].append(dc)
            dq_pre, g_qn[j] = _headnorm_bwd(rec["qg"], 0, 0, q_norm[j], [dqn], D, f"q_norm_bwd_{j}")
            dqg = jnp.stack([dq_pre, dgate])
            g_qg[j] = _wgrad(rec["xn"], dqg, f"qg_wgrad_{j}")
            dy, g_attn_norm[l] = _dnorm(dqg, wb_qg[j], rec["x_in"], attn_norm[l], dy, f"qg_bwd_{j}")
            if l == n_a:
                dk_s, g_k_norm = _headnorm_bwd(kv["kvf"], 0, 0, k_norm, kv["dk"], D, "k_norm_bwd")
                dv_s = functools.reduce(jnp.add, kv["dv"]).astype(BF)
                dc_sum = functools.reduce(jnp.add, kv["dc"])
                dc_pad = jnp.pad(dc_sum.transpose(1, 0, 2).reshape(S, H), ((0, 0), (0, LANES - H)))
                df, db = _gate_bwd(dc_pad, kv["kvf"], b_pad, gate_blk, "gate_bwd")
                dkvf = jnp.concatenate([dk_s, dv_s, df.astype(BF)], axis=1)[None]
                g_kvf = _wgrad(kv["hn"], dkvf, "kvf_wgrad")[:, :2 * D + H]
                g_b_f = db[:H]
                dy, g_kv_norm = _dnorm(dkvf, wkvf, kv["x_in"], kv_norm, dy, "kvf_bwd")
    grad_x = dy[None]

    last = {(0, l): (g_a_in[l], cols_of) for l in range(n_a)}
    last.update({(1, l): (g_a_out[l], rows_of) for l in range(n_a)})
    last.update({(2, 0): (g_kvf, cols_of), (3, 0): (g_qg[0], cols_of)})
    last.update({(5, l): (g_up[l], cols_of) for l in range(n_a)})
    last.update({(6, l): (g_down[l], rows_of) for l in range(n_a)})
    reduced.update(reduce_scatter(last, "c"))
    big_grads = [jnp.concatenate([reduced[(i, l)] for l in range(w.shape[0] if w.ndim == 3 else 1)]).reshape(w.shape)
                 for i, w in enumerate(big)]

    small = [loss_part[0, :1], jnp.stack(g_attn_norm), jnp.stack(g_ffn_norm), g_kv_norm, g_b_f, g_k_norm,
             jnp.stack(g_qn), jnp.stack(g_a_conv), jnp.stack(g_fc)]
    small_sum = _allreduce_small(jnp.concatenate([_pad_row(s) for s in small]).reshape(-1, FLAT_W),
                                 "allreduce_small_grads").reshape(-1)
    parts, off = [], 0
    for s in small:
        parts.append(small_sum[off:off + s.size].reshape(s.shape))
        off += _pad_row(s).shape[0]
    loss = parts[0][0]
    gr_attn_norm, gr_ffn_norm, gr_kv_norm, gr_b_f, gr_k_norm, gr_q_norm, gr_a_conv_full, gr_ffn_conv_full = parts[1:]

    def my_cols(full):
        c = full.shape[-1] // N_CHIPS
        return lax.dynamic_slice_in_dim(full, chip * c, c, axis=full.ndim - 1)

    gr_a_in, gr_a_out, gr_kvf, gr_qg, gr_bo, gr_up, gr_down = big_grads
    grads = [gr_attn_norm, gr_ffn_norm, gr_a_in, my_cols(gr_a_conv_full), gr_a_out, gr_kv_norm, gr_kvf, gr_b_f,
             gr_k_norm, gr_qg, gr_q_norm, gr_bo, gr_up, my_cols(gr_ffn_conv_full), gr_down]
    weights = [attn_norm, ffn_norm, a_w_in, a_conv, a_w_out, kv_norm, w_kvf, b_f, k_norm, b_w_qg, q_norm, b_w_out,
               ffn_w_up, ffn_conv, ffn_w_down]
    ms = [m_attn_norm, m_ffn_norm, m_a_w_in, m_a_conv, m_a_w_out, m_kv_norm, m_w_kvf, m_b_f, m_k_norm, m_b_w_qg,
          m_q_norm, m_b_w_out, m_ffn_w_up, m_ffn_conv, m_ffn_w_down]
    vs = [v_attn_norm, v_ffn_norm, v_a_w_in, v_a_conv, v_a_w_out, v_kv_norm, v_w_kvf, v_b_f, v_k_norm, v_b_w_qg,
          v_q_norm, v_b_w_out, v_ffn_w_up, v_ffn_conv, v_ffn_w_down]
    deltas, new_ms, new_vs = [], [], []
    for i, (w, g, m, v) in enumerate(zip(weights, grads, ms, vs)):
        d, nm, nv = _adamw(w, g, m, v, f"adamw_{i}")
        deltas.append(d)
        new_ms.append(nm)
        new_vs.append(nv)
    return (loss, grad_x, *grads, *deltas, *new_ms, *new_vs)
```

```python
import functools

import jax
import jax.numpy as jnp
from jax import lax
from jax.experimental import pallas as pl
from jax.experimental.pallas import tpu as pltpu

F32 = jnp.float32
BF = jnp.bfloat16
LANES = 128
SUBLANES = 8
RMS_EPS = 1e-6
LOG2E = 1.4426950408889634
FLAT_W = 1024
N_CHIPS = 4
CONV_W = 3
HALO = SUBLANES

ADAM_LR = 0.001
ADAM_B1 = 0.9
ADAM_B2 = 0.999
ADAM_EPS = 1e-08
ADAM_WD = 0.01
ADAM_STEP = 10

MESH = pl.DeviceIdType.MESH
ANY = pl.BlockSpec(memory_space=pl.ANY)
VMEM = pl.BlockSpec(memory_space=pltpu.VMEM)
NT_DIMS = (((1,), (1,)), ((), ()))
TN_DIMS = (((0,), (0,)), ((), ()))


def _tile(n, pref, mult=LANES):
    t = (min(pref, n) // mult) * mult
    while t >= mult:
        if n % t == 0:
            break
        t -= mult
    if t < mult or (t * 4 < pref and n <= 4 * pref):
        return n
    return t


def _params(*sem):
    return pltpu.CompilerParams(dimension_semantics=sem)


def _norm_matmul(x, g, w, parts, out_dtype, name):
    S, D = x.shape
    C = w.shape[1] // parts
    ts, tn = _tile(S, 512, SUBLANES), _tile(C, 1408)
    npc = C // tn

    def body(x_ref, g_ref, w_ref, o_ref, xn_ref):
        @pl.when(pl.program_id(1) == 0)
        def _():
            xf = x_ref[...]
            r = lax.rsqrt(jnp.mean(xf * xf, axis=-1, keepdims=True) + RMS_EPS)
            xn_ref[...] = (xf * r * g_ref[...]).astype(BF)

        o_ref[0] = jnp.dot(xn_ref[...], w_ref[...], preferred_element_type=F32).astype(out_dtype)

    return pl.pallas_call(
        body, name=name, grid=(S // ts, parts * npc),
        in_specs=[pl.BlockSpec((ts, D), lambda s, n: (s, 0)),
                  pl.BlockSpec((1, D), lambda s, n: (0, 0)),
                  pl.BlockSpec((D, tn), lambda s, n: (0, n))],
        out_specs=[pl.BlockSpec((1, ts, tn), lambda s, n: (n // npc, s, n % npc)),
                   pl.BlockSpec((ts, D), lambda s, n: (s, 0))],
        out_shape=[jax.ShapeDtypeStruct((parts, S, C), out_dtype), jax.ShapeDtypeStruct((S, D), BF)],
        compiler_params=_params("parallel", "arbitrary"),
    )(x, g.reshape(1, D), w)


def _shift_down(u, prev, k):
    r = pltpu.roll(u, k, 0)
    row = lax.broadcasted_iota(jnp.int32, (HALO, u.shape[1]), 0)
    head = r[0:HALO]
    for j in range(k):
        head = jnp.where(row == j, prev[HALO - k + j:HALO - k + j + 1, :], head)
    return jnp.concatenate([head, r[HALO:]], axis=0)


def _shift_up(d, nxt, k):
    n = d.shape[0]
    r = pltpu.roll(d, n - k, 0)
    row = lax.broadcasted_iota(jnp.int32, (HALO, d.shape[1]), 0)
    tail = r[n - HALO:n]
    for j in range(k):
        tail = jnp.where(row == HALO - k + j, nxt[j:j + 1, :], tail)
    return jnp.concatenate([r[0:n - HALO], tail], axis=0)


def _conv3(u, prev, w):
    return _shift_down(u, prev, 2) * w[0:1] + _shift_down(u, prev, 1) * w[1:2] + u * w[2:3]


def _conv3_t(d, nxt, w):
    return d * w[2:3] + _shift_up(d, nxt, 1) * w[1:2] + _shift_up(d, nxt, 2) * w[0:1]


def _tap_rows(t0, t1, t2):
    row = lax.broadcasted_iota(jnp.int32, (SUBLANES, t0.shape[1]), 0)
    return jnp.where(row == 0, t0, jnp.where(row == 1, t1, jnp.where(row == 2, t2, 0.0)))


def _pad_conv(cw):
    return jnp.pad(cw, ((0, SUBLANES - CONV_W), (0, 0)))


def _mixer_in_fwd(x, g, w, cw, name, gather=()):
    S, D = x.shape
    C = w.shape[1] // 3
    ts, tc = _tile(S, 512, SUBLANES), _tile(C, 1024)
    nc = C // tc
    n_s = S // ts
    n_g = len(gather)

    def body(x_ref, g_ref, wb_ref, wc_ref, wh_ref, cw_ref, *rest):
        p_ref, xn_ref, z_ref = rest[n_g:n_g + 3]
        carry = rest[2 * n_g + 3]
        s, c = pl.program_id(0), pl.program_id(1)
        if n_g:
            start, pass_on, finish = _gather_phases([a.shape[0] for a in gather], rest[:n_g],
                                                    rest[n_g + 3:2 * n_g + 3], *rest[2 * n_g + 4:])
            pl.when((s == 0) & (c == 0))(start)
            pl.when((s == (3 * n_s) // 4) & (c == 0))(pass_on)

        @pl.when(c == 0)
        def _():
            xf = x_ref[...]
            r = lax.rsqrt(jnp.mean(xf * xf, axis=-1, keepdims=True) + RMS_EPS)
            xn_ref[...] = (xf * r * g_ref[...]).astype(BF)

        @pl.when(s == 0)
        def _():
            carry[c] = jnp.zeros((HALO, tc), F32)

        xn = xn_ref[...]
        parts = [jnp.dot(xn, w_ref[...], preferred_element_type=F32).astype(BF) for w_ref in (wb_ref, wc_ref, wh_ref)]
        for p, v in enumerate(parts):
            p_ref[p] = v
        u = parts[1].astype(F32) * parts[2].astype(F32)
        cv = _conv3(u, carry[c], cw_ref[...])
        z_ref[...] = (parts[0].astype(F32) * cv).astype(BF)
        carry[c] = u[ts - HALO:ts, :]
        if n_g:
            pl.when((s == n_s - 1) & (c == nc - 1))(finish)

    wspec = lambda p: pl.BlockSpec((D, tc), lambda s, c: (0, p * nc + c))
    sems = [pltpu.SemaphoreType.DMA((6 * n_g,)), pltpu.SemaphoreType.DMA((6 * n_g,))] if n_g else []
    outs = pl.pallas_call(
        body, name=name, grid=(n_s, nc),
        in_specs=[pl.BlockSpec((ts, D), lambda s, c: (s, 0)), pl.BlockSpec((1, D), lambda s, c: (0, 0)),
                  wspec(0), wspec(1), wspec(2), pl.BlockSpec((SUBLANES, tc), lambda s, c: (0, c))] + [ANY] * n_g,
        out_specs=[pl.BlockSpec((3, ts, tc), lambda s, c: (0, s, c)),
                   pl.BlockSpec((ts, D), lambda s, c: (s, 0)),
                   pl.BlockSpec((ts, tc), lambda s, c: (s, c))] + [ANY] * n_g,
        out_shape=[jax.ShapeDtypeStruct((3, S, C), BF), jax.ShapeDtypeStruct((S, D), BF),
                   jax.ShapeDtypeStruct((S, C), BF)]
        + [jax.ShapeDtypeStruct((N_CHIPS,) + a.shape, a.dtype) for a in gather],
        scratch_shapes=[pltpu.VMEM((nc, HALO, tc), F32)] + sems,
        compiler_params=_params("arbitrary", "arbitrary"),
    )(x, g.reshape(1, D), w, w, w, _pad_conv(cw), *gather)
    return tuple(outs[:3]) + (_fill_own(outs[3:], gather),)


def _ffn_up_fwd(x, g, w, cw, name, gather=()):
    S, D = x.shape
    C = w.shape[1] // 2
    ts, tc = _tile(S, 512, SUBLANES), _tile(C, 1408)
    nc = C // tc
    n_s = S // ts
    n_g = len(gather)

    def body(x_ref, g_ref, wa_ref, wg_ref, cw_ref, *rest):
        up_ref, xn_ref, z_ref = rest[n_g:n_g + 3]
        carry = rest[2 * n_g + 3]
        s, c = pl.program_id(0), pl.program_id(1)
        if n_g:
            start, pass_on, finish = _gather_phases([a.shape[0] for a in gather], rest[:n_g],
                                                    rest[n_g + 3:2 * n_g + 3], *rest[2 * n_g + 4:])
            pl.when((s == 0) & (c == 0))(start)
            pl.when((s == (3 * n_s) // 4) & (c == 0))(pass_on)

        @pl.when(c == 0)
        def _():
            xf = x_ref[...]
            r = lax.rsqrt(jnp.mean(xf * xf, axis=-1, keepdims=True) + RMS_EPS)
            xn_ref[...] = (xf * r * g_ref[...]).astype(BF)

        @pl.when(s == 0)
        def _():
            carry[c] = jnp.zeros((HALO, tc), F32)

        xn = xn_ref[...]
        a_b = jnp.dot(xn, wa_ref[...], preferred_element_type=F32).astype(BF)
        g_b = jnp.dot(xn, wg_ref[...], preferred_element_type=F32).astype(BF)
        up_ref[0] = a_b
        up_ref[1] = g_b
        a_pre = a_b.astype(F32)
        a = _conv3(a_pre, carry[c], cw_ref[...])
        z_ref[...] = (a * jax.nn.sigmoid(a) * g_b.astype(F32)).astype(BF)
        carry[c] = a_pre[ts - HALO:ts, :]
        if n_g:
            pl.when((s == n_s - 1) & (c == nc - 1))(finish)

    sems = [pltpu.SemaphoreType.DMA((6 * n_g,)), pltpu.SemaphoreType.DMA((6 * n_g,))] if n_g else []
    outs = pl.pallas_call(
        body, name=name, grid=(n_s, nc),
        in_specs=[pl.BlockSpec((ts, D), lambda s, c: (s, 0)),
                  pl.BlockSpec((1, D), lambda s, c: (0, 0)),
                  pl.BlockSpec((D, tc), lambda s, c: (0, c)),
                  pl.BlockSpec((D, tc), lambda s, c: (0, nc + c)),
                  pl.BlockSpec((SUBLANES, tc), lambda s, c: (0, c))] + [ANY] * n_g,
        out_specs=[pl.BlockSpec((2, ts, tc), lambda s, c: (0, s, c)),
                   pl.BlockSpec((ts, D), lambda s, c: (s, 0)),
                   pl.BlockSpec((ts, tc), lambda s, c: (s, c))] + [ANY] * n_g,
        out_shape=[jax.ShapeDtypeStruct((2, S, C), BF), jax.ShapeDtypeStruct((S, D), BF),
                   jax.ShapeDtypeStruct((S, C), BF)]
        + [jax.ShapeDtypeStruct((N_CHIPS,) + a.shape, a.dtype) for a in gather],
        scratch_shapes=[pltpu.VMEM((nc, HALO, tc), F32)] + sems,
        compiler_params=_params("arbitrary", "arbitrary"),
    )(x, g.reshape(1, D), w, w, _pad_conv(cw), *gather)
    return tuple(outs[:3]) + (_fill_own(outs[3:], gather),)


def _mixer_mid_bwd(proj, dy, w_out, cw, name):
    _, S, C = proj.shape
    D = dy.shape[1]
    ts, tc = _tile(S, 512, SUBLANES), _tile(C, 1024)
    n_s = S // ts
    per = ts // HALO

    def body(b_ref, c_ref, h_ref, dy_ref, w_ref, cp_ref, hp_ref, cw_ref, d_ref, dcw_ref, carry):
        i = pl.program_id(1)
        w = cw_ref[...]
        dz = lax.dot_general(dy_ref[...].astype(BF), w_ref[...], NT_DIMS, preferred_element_type=F32)
        b, c, h = b_ref[0].astype(F32), c_ref[0].astype(F32), h_ref[0].astype(F32)
        u = c * h
        prev = jnp.where(i < n_s - 1, cp_ref[0].astype(F32) * hp_ref[0].astype(F32), 0.0)
        u1, u2 = _shift_down(u, prev, 1), _shift_down(u, prev, 2)
        cv = u2 * w[0:1] + u1 * w[1:2] + u * w[2:3]
        dcv = dz * b
        nxt = jnp.where(i > 0, carry[...], 0.0)
        du = _conv3_t(dcv, nxt, w)
        d_ref[0] = (dz * cv).astype(BF)
        d_ref[1] = (du * h).astype(BF)
        d_ref[2] = (du * c).astype(BF)
        carry[...] = dcv[0:HALO, :]
        part = _tap_rows(jnp.sum(dcv * u2, axis=0, keepdims=True), jnp.sum(dcv * u1, axis=0, keepdims=True),
                         jnp.sum(dcv * u, axis=0, keepdims=True))

        @pl.when(i == 0)
        def _():
            dcw_ref[...] = part

        @pl.when(i > 0)
        def _():
            dcw_ref[...] += part

    tile = lambda p: pl.BlockSpec((1, ts, tc), lambda c, i: (p, n_s - 1 - i, c))
    before = lambda p: pl.BlockSpec((1, HALO, tc), lambda c, i: (p, jnp.maximum((n_s - 1 - i) * per - 1, 0), c))
    dproj, dcw = pl.pallas_call(
        body, name=name, grid=(C // tc, n_s),
        in_specs=[tile(0), tile(1), tile(2),
                  pl.BlockSpec((ts, D), lambda c, i: (n_s - 1 - i, 0)),
                  pl.BlockSpec((tc, D), lambda c, i: (c, 0)),
                  before(1), before(2),
                  pl.BlockSpec((SUBLANES, tc), lambda c, i: (0, c))],
        out_specs=[pl.BlockSpec((3, ts, tc), lambda c, i: (0, n_s - 1 - i, c)),
                   pl.BlockSpec((SUBLANES, tc), lambda c, i: (0, c))],
        out_shape=[jax.ShapeDtypeStruct((3, S, C), BF), jax.ShapeDtypeStruct((SUBLANES, C), F32)],
        scratch_shapes=[pltpu.VMEM((HALO, tc), F32)],
        compiler_params=_params("parallel", "arbitrary"),
    )(proj, proj, proj, dy, w_out, proj, proj, _pad_conv(cw))
    return dproj, dcw[:CONV_W]


def _ffn_mid_bwd(up, dy, w_down, cw, name, exchange=()):
    _, S, C = up.shape
    D = dy.shape[1]
    ts, tc = _tile(S, 512, SUBLANES), _tile(C, 1408)
    n_s = S // ts
    n_c = C // tc
    per = ts // HALO
    n_x = len(exchange)

    def body(a_ref, g_ref, dy_ref, w_ref, ap_ref, cw_ref, *rest):
        d_ref, dcw_ref = rest[n_x:n_x + 2]
        carry = rest[2 * n_x + 2]
        i = pl.program_id(1)
        if n_x:
            copies = lambda: _chip_copies(rest[:n_x], rest[n_x + 2:2 * n_x + 2], *rest[2 * n_x + 3:])

            @pl.when((pl.program_id(0) == 0) & (i == 0))
            def _():
                for cp in copies():
                    cp.start()

        w = cw_ref[...]
        dz = lax.dot_general(dy_ref[...].astype(BF), w_ref[...], NT_DIMS, preferred_element_type=F32)
        a_pre, g = a_ref[0].astype(F32), g_ref[0].astype(F32)
        prev = jnp.where(i < n_s - 1, ap_ref[0].astype(F32), 0.0)
        a1, a2 = _shift_down(a_pre, prev, 1), _shift_down(a_pre, prev, 2)
        a = a2 * w[0:1] + a1 * w[1:2] + a_pre * w[2:3]
        sg = jax.nn.sigmoid(a)
        da = dz * g * (sg * (1.0 + a * (1.0 - sg)))
        nxt = jnp.where(i > 0, carry[...], 0.0)
        d_ref[0] = _conv3_t(da, nxt, w).astype(BF)
        d_ref[1] = (dz * (a * sg)).astype(BF)
        carry[...] = da[0:HALO, :]
        part = _tap_rows(jnp.sum(da * a2, axis=0, keepdims=True), jnp.sum(da * a1, axis=0, keepdims=True),
                         jnp.sum(da * a_pre, axis=0, keepdims=True))

        @pl.when(i == 0)
        def _():
            dcw_ref[...] = part

        @pl.when(i > 0)
        def _():
            dcw_ref[...] += part

        if n_x:
            @pl.when((pl.program_id(0) == n_c - 1) & (i == n_s - 1))
            def _():
                for cp in copies():
                    cp.wait()

    tile = lambda p: pl.BlockSpec((1, ts, tc), lambda c, i: (p, n_s - 1 - i, c))
    sems = [pltpu.SemaphoreType.DMA((3 * n_x,)), pltpu.SemaphoreType.DMA((3 * n_x,))] if n_x else []
    outs = pl.pallas_call(
        body, name=name, grid=(n_c, n_s),
        in_specs=[tile(0), tile(1),
                  pl.BlockSpec((ts, D), lambda c, i: (n_s - 1 - i, 0)),
                  pl.BlockSpec((tc, D), lambda c, i: (c, 0)),
                  pl.BlockSpec((1, HALO, tc), lambda c, i: (0, jnp.maximum((n_s - 1 - i) * per - 1, 0), c)),
                  pl.BlockSpec((SUBLANES, tc), lambda c, i: (0, c))] + [ANY] * n_x,
        out_specs=[pl.BlockSpec((2, ts, tc), lambda c, i: (0, n_s - 1 - i, c)),
                   pl.BlockSpec((SUBLANES, tc), lambda c, i: (0, c))] + [ANY] * n_x,
        out_shape=[jax.ShapeDtypeStruct((2, S, C), BF), jax.ShapeDtypeStruct((SUBLANES, C), F32)]
        + [jax.ShapeDtypeStruct((3,) + a.shape[1:], a.dtype) for a in exchange],
        scratch_shapes=[pltpu.VMEM((HALO, tc), F32)] + sems,
        compiler_params=_params("arbitrary" if n_x else "parallel", "arbitrary"),
    )(up, up, dy, w_down, up, _pad_conv(cw), *exchange)
    return outs[0], outs[1][:CONV_W], list(outs[2:])


def _matmul_residual(z, w, x, name):
    S, K = z.shape
    D = w.shape[1]
    ts = _tile(S, 512, SUBLANES)

    def body(z_ref, w_ref, x_ref, o_ref):
        o_ref[...] = x_ref[...] + jnp.dot(z_ref[...], w_ref[...], preferred_element_type=F32)

    return pl.pallas_call(
        body, name=name, grid=(S // ts,),
        in_specs=[pl.BlockSpec((ts, K), lambda s: (s, 0)), pl.BlockSpec((K, D), lambda s: (0, 0)),
                  pl.BlockSpec((ts, D), lambda s: (s, 0))],
        out_specs=pl.BlockSpec((ts, D), lambda s: (s, 0)),
        out_shape=jax.ShapeDtypeStruct((S, D), F32),
        compiler_params=_params("parallel"),
    )(z, w, x)


def _wgrad(a, b, name):
    S, M = a.shape
    P, _, C = b.shape
    tm, tn, tk = _tile(M, 1408), _tile(C, 1408), _tile(S, 1024, SUBLANES)
    nnc = C // tn

    def body(a_ref, b_ref, o_ref):
        @pl.when(pl.program_id(2) == 0)
        def _():
            o_ref[...] = jnp.zeros_like(o_ref)

        o_ref[...] += lax.dot_general(a_ref[...], b_ref[0].astype(BF), TN_DIMS, preferred_element_type=F32)

    return pl.pallas_call(
        body, name=name, grid=(M // tm, P * nnc, S // tk),
        in_specs=[pl.BlockSpec((tk, tm), lambda m, n, k: (k, m)),
                  pl.BlockSpec((1, tk, tn), lambda m, n, k: (n // nnc, k, n % nnc))],
        out_specs=pl.BlockSpec((tm, tn), lambda m, n, k: (m, n)),
        out_shape=jax.ShapeDtypeStruct((M, P * C), F32),
        compiler_params=_params("parallel", "parallel", "arbitrary"),
    )(a, b)


def _dnorm(dp, w, x, g, dy, name):
    P, S, C = dp.shape
    D = x.shape[1]
    ts = _tile(S, 512, SUBLANES)

    def body(dp_ref, w_ref, x_ref, g_ref, dy_ref, dx_ref, dg_ref):
        @pl.when(pl.program_id(0) == 0)
        def _():
            dg_ref[...] = jnp.zeros_like(dg_ref)

        dxn = lax.dot_general(dp_ref[0], w_ref[:, 0:C], NT_DIMS, preferred_element_type=F32)
        for p in range(1, P):
            dxn = dxn + lax.dot_general(dp_ref[p], w_ref[:, p * C:(p + 1) * C], NT_DIMS, preferred_element_type=F32)
        xf = x_ref[...]
        r = lax.rsqrt(jnp.mean(xf * xf, axis=-1, keepdims=True) + RMS_EPS)
        xhat = xf * r
        dxhat = dxn * g_ref[...]
        dx_ref[...] = dy_ref[...] + r * (dxhat - xhat * jnp.mean(dxhat * xhat, axis=-1, keepdims=True))
        dg_ref[...] += jnp.broadcast_to(jnp.sum(dxn * xhat, axis=0, keepdims=True), dg_ref.shape)

    dx, dg = pl.pallas_call(
        body, name=name, grid=(S // ts,),
        in_specs=[pl.BlockSpec((P, ts, C), lambda s: (0, s, 0)),
                  pl.BlockSpec((D, P * C), lambda s: (0, 0), pipeline_mode=pl.Buffered(1)),
                  pl.BlockSpec((ts, D), lambda s: (s, 0)),
                  pl.BlockSpec((1, D), lambda s: (0, 0)),
                  pl.BlockSpec((ts, D), lambda s: (s, 0))],
        out_specs=[pl.BlockSpec((ts, D), lambda s: (s, 0)),
                   pl.BlockSpec((SUBLANES, D), lambda s: (0, 0))],
        out_shape=[jax.ShapeDtypeStruct((S, D), F32), jax.ShapeDtypeStruct((SUBLANES, D), F32)],
        compiler_params=_params("arbitrary"),
    )(dp, w, x, g.reshape(1, D), dy)
    return dx, dg[0]


def _head_masks(shape, hd):
    lane = lax.broadcasted_iota(jnp.int32, shape, 1)
    return lane < hd


def _pair_sum(v, lo):
    s0 = jnp.sum(jnp.where(lo, v, 0.0), axis=-1, keepdims=True)
    s1 = jnp.sum(jnp.where(lo, 0.0, v), axis=-1, keepdims=True)
    return jnp.where(lo, s0, s1)


def _headnorm_bwd(src, part, colblk, w, dys, D, name):
    S = src.shape[1]
    hd = w.shape[0]
    ts = _tile(S, 512, SUBLANES)
    w2 = jnp.tile(w, LANES // hd).reshape(1, LANES)
    n_dy = len(dys)

    def body(x_ref, w_ref, *rest):
        dy_refs, dx_ref, dw_ref = rest[:n_dy], rest[n_dy], rest[n_dy + 1]

        @pl.when(pl.program_id(0) == 0)
        def _():
            dw_ref[...] = jnp.zeros_like(dw_ref)

        lo = _head_masks((ts, LANES), hd)
        for t in range(D // LANES):
            cols = slice(t * LANES, (t + 1) * LANES)
            xt = x_ref[0, :, cols]
            dy = dy_refs[0][:, cols]
            for other in dy_refs[1:]:
                dy = dy + other[:, cols]
            r = lax.rsqrt(_pair_sum(xt * xt, lo) * (1.0 / hd) + RMS_EPS)
            xhat = xt * r
            dxhat = dy * w_ref[...]
            mean = _pair_sum(dxhat * xhat, lo) * (1.0 / hd)
            dx_ref[:, cols] = (r * (dxhat - xhat * mean)).astype(BF)
            dw_ref[:, cols] += jnp.broadcast_to(jnp.sum(dy * xhat, axis=0, keepdims=True), (SUBLANES, LANES))

    dx, dw = pl.pallas_call(
        body, name=name, grid=(S // ts,),
        in_specs=[pl.BlockSpec((1, ts, D), lambda s: (part, s, colblk)), pl.BlockSpec((1, LANES), lambda s: (0, 0))]
        + [pl.BlockSpec((ts, D), lambda s: (s, 0))] * n_dy,
        out_specs=[pl.BlockSpec((ts, D), lambda s: (s, 0)), pl.BlockSpec((SUBLANES, D), lambda s: (0, 0))],
        out_shape=[jax.ShapeDtypeStruct((S, D), BF), jax.ShapeDtypeStruct((SUBLANES, D), F32)],
        compiler_params=_params("arbitrary"),
    )(src, w2, *dys)
    return dx, jnp.sum(dw[0].reshape(D // hd, hd), axis=0)


def _tri(n, lower):
    r, c = lax.broadcasted_iota(jnp.int32, (n, n), 0), lax.broadcasted_iota(jnp.int32, (n, n), 1)
    return jnp.where((c <= r) if lower else (c >= r), 1.0, 0.0).astype(BF)


def _dot_exact(t, v):
    hi = v.astype(BF)
    r1 = v - hi.astype(F32)
    mid = r1.astype(BF)
    lo = (r1 - mid.astype(F32)).astype(BF)
    dot = lambda u: jnp.dot(t, u, preferred_element_type=F32)
    return dot(hi) + dot(mid) + dot(lo)


def _gate_fwd(kvf, b_pad, colblk, name):
    S = kvf.shape[1]
    ts = _tile(S, 512, SUBLANES)

    def body(f_ref, b_ref, c_ref, carry):
        @pl.when(pl.program_id(0) == 0)
        def _():
            carry[...] = jnp.zeros_like(carry)

        f = f_ref[0] + b_ref[...]
        ls = jnp.minimum(f, 0.0) - jnp.log1p(jnp.exp(-jnp.abs(f)))
        tri = _tri(ts, lower=True)
        c = _dot_exact(tri, ls) + carry[0:1, :]
        c_ref[...] = c
        carry[...] = jnp.broadcast_to(c[ts - 1:ts, :], carry.shape)

    return pl.pallas_call(
        body, name=name, grid=(S // ts,),
        in_specs=[pl.BlockSpec((1, ts, LANES), lambda s: (0, s, colblk)), pl.BlockSpec((1, LANES), lambda s: (0, 0))],
        out_specs=pl.BlockSpec((ts, LANES), lambda s: (s, 0)),
        out_shape=jax.ShapeDtypeStruct((S, LANES), F32),
        scratch_shapes=[pltpu.VMEM((SUBLANES, LANES), F32)],
        compiler_params=_params("arbitrary"),
    )(kvf, b_pad)


def _gate_bwd(dc, kvf, b_pad, colblk, name):
    S = kvf.shape[1]
    ts = _tile(S, 512, SUBLANES)
    n_s = S // ts

    def body(dc_ref, f_ref, b_ref, df_ref, db_ref, carry):
        @pl.when(pl.program_id(0) == 0)
        def _():
            carry[...] = jnp.zeros_like(carry)
            db_ref[...] = jnp.zeros_like(db_ref)

        tri = _tri(ts, lower=False)
        dls = _dot_exact(tri, dc_ref[...]) + carry[0:1, :]
        f = f_ref[0] + b_ref[...]
        df = dls * jax.nn.sigmoid(-f)
        df_ref[...] = df
        db_ref[...] += jnp.broadcast_to(jnp.sum(df, axis=0, keepdims=True), db_ref.shape)
        carry[...] = jnp.broadcast_to(dls[0:1, :], carry.shape)

    df, db = pl.pallas_call(
        body, name=name, grid=(n_s,),
        in_specs=[pl.BlockSpec((ts, LANES), lambda s: (n_s - 1 - s, 0)),
                  pl.BlockSpec((1, ts, LANES), lambda s: (0, n_s - 1 - s, colblk)),
                  pl.BlockSpec((1, LANES), lambda s: (0, 0))],
        out_specs=[pl.BlockSpec((ts, LANES), lambda s: (n_s - 1 - s, 0)),
                   pl.BlockSpec((SUBLANES, LANES), lambda s: (0, 0))],
        out_shape=[jax.ShapeDtypeStruct((S, LANES), F32), jax.ShapeDtypeStruct((SUBLANES, LANES), F32)],
        scratch_shapes=[pltpu.VMEM((SUBLANES, LANES), F32)],
        compiler_params=_params("arbitrary"),
    )(dc, kvf, b_pad)
    return df, db[0]


def _attn_tile(S):
    return _tile(S, 512, LANES)


def _split_heads(v, lo):
    zero = jnp.zeros_like(v)
    return jnp.where(lo, v, zero), jnp.where(lo, zero, v)


def _augment(src, part, colblk, c, mode, hd, D, name, norm_w=None, scale=1.0):
    S = src.shape[1]
    ts = _tile(S, 512, 2 * SUBLANES)
    w2 = jnp.tile(jnp.ones((hd,), F32) if norm_w is None else norm_w, LANES // hd).reshape(1, LANES)

    def body(b_ref, w_ref, c_ref, o0_ref, o1_ref):
        lane = lax.broadcasted_iota(jnp.int32, (ts, LANES), 1)
        lo = lane < hd
        cc = c_ref[...] * LOG2E
        for t in range(D // LANES):
            cols = slice(t * LANES, (t + 1) * LANES)
            bt = b_ref[0, :, cols]
            if norm_w is not None:
                r = lax.rsqrt(_pair_sum(bt * bt, lo) * (1.0 / hd) + RMS_EPS)
                bt = bt * r * w_ref[...] * scale
            bt = bt.astype(BF)
            for h, o_ref in ((0, o0_ref), (1, o1_ref)):
                first = hd if h == 0 else 0
                keep = (lane < hd) if h == 0 else (lane >= hd)
                if mode == "v":
                    vals = (1.0,)
                else:
                    col = cc[:, 2 * t + h:2 * t + h + 1]
                    hi = col.astype(BF).astype(F32)
                    mid = (col - hi).astype(BF).astype(F32)
                    pieces = (hi, mid, col - hi - mid)
                    vals = pieces + (1.0, 1.0, 1.0) if mode == "q" else (1.0, 1.0, 1.0) + tuple(-v for v in pieces)
                aug = jnp.zeros((ts, LANES), F32)
                for i, v in enumerate(vals):
                    aug = jnp.where(lane == first + i, v, aug)
                o_ref[:, cols] = jnp.where(keep, bt, aug.astype(BF))

    spec = pl.BlockSpec((ts, D), lambda s: (s, 0))
    return pl.pallas_call(
        body, name=name, grid=(S // ts,),
        in_specs=[pl.BlockSpec((1, ts, D), lambda s: (part, s, colblk)), pl.BlockSpec((1, LANES), lambda s: (0, 0)),
                  pl.BlockSpec((ts, LANES), lambda s: (s, 0))],
        out_specs=[spec, spec],
        out_shape=[jax.ShapeDtypeStruct((S, D), BF)] * 2,
        compiler_params=_params("parallel"),
    )(src, w2, c)


def _attn_fwd(qa, ka, va, qg, hd, name, gather=()):
    S, D = qa[0].shape
    P = D // LANES
    tq = _attn_tile(S)
    nq = S // tq
    n_g = len(gather)

    def body(q0_ref, q1_ref, k0_ref, k1_ref, v0_ref, v1_ref, g_ref, *rest):
        o_ref, og_ref, m_ref, l_ref = rest[n_g:n_g + 4]
        s_buf = rest[2 * n_g + 4]
        pair, qi = pl.program_id(0), pl.program_id(1)
        if n_g:
            start, pass_on, finish = _gather_phases([a.shape[0] for a in gather], rest[:n_g],
                                                    rest[n_g + 4:2 * n_g + 4], *rest[2 * n_g + 5:])
            pl.when((pair == 0) & (qi == 0))(start)
            pl.when((pair == P // 2) & (qi == 0))(pass_on)
        lo = _head_masks((tq, LANES), hd)
        qh = (q0_ref[...], q1_ref[...])
        k_refs, v_refs = (k0_ref, k1_ref), (v0_ref, v1_ref)
        causal = lax.broadcasted_iota(jnp.int32, (tq, tq), 1) <= lax.broadcasted_iota(jnp.int32, (tq, tq), 0)

        def scores(ki, slot):
            off = pl.multiple_of(ki * tq, tq)
            for h in range(2):
                s_buf[slot, h] = lax.dot_general(qh[h], k_refs[h][pl.ds(off, tq), :], NT_DIMS,
                                                 preferred_element_type=F32)

        def consume(ki, slot, carry, masked):
            off = pl.multiple_of(ki * tq, tq)
            out = []
            for h in range(2):
                m, acc = carry[h]
                s = s_buf[slot, h]
                if masked:
                    s = jnp.where(causal, s, -jnp.inf)
                m_new = jnp.maximum(m, jnp.ceil(jnp.max(s, axis=-1, keepdims=True)))
                p = jnp.exp2(s - m_new)
                acc = jnp.exp2(m - m_new) * acc + jnp.dot(p.astype(BF), v_refs[h][pl.ds(off, tq), :],
                                                          preferred_element_type=F32)
                out.append((m_new, acc))
            return tuple(out)

        def step(j, carry):
            scores(2 * j + 1, 1)
            carry = consume(2 * j, 0, carry, False)
            scores(2 * j + 2, 0)
            return consume(2 * j + 1, 1, carry, False)

        def finish_even(carry):
            return consume(qi, 0, carry, True)

        def finish_odd(carry):
            scores(qi, 1)
            return consume(qi, 1, consume(qi - 1, 0, carry, False), True)

        init = tuple((jnp.full((tq, 1), -jnp.inf, F32), jnp.zeros((tq, LANES), F32)) for _ in range(2))
        scores(0, 0)
        carry = lax.fori_loop(0, qi // 2, step, init)
        (m0, a0), (m1, a1) = lax.cond(qi % 2 == 0, finish_even, finish_odd, carry)
        l0, l1 = a0[:, hd:hd + 1], a1[:, 0:1]
        o = jnp.where(lo, a0 / l0, a1 / l1)
        o_ref[...] = o
        og_ref[...] = (o * jax.nn.sigmoid(g_ref[0])).astype(BF)
        lane2 = lax.broadcasted_iota(jnp.int32, (tq, 2), 1)
        m_ref[0] = jnp.where(lane2 == 0, m0, m1)
        l_ref[0] = jnp.where(lane2 == 0, l0, l1)
        if n_g:
            pl.when((pair == P - 1) & (qi == nq - 1))(finish)

    tile = pl.BlockSpec((tq, LANES), lambda p, i: (i, p))
    whole = pl.BlockSpec((S, LANES), lambda p, i: (0, p))
    stat = pl.BlockSpec((1, tq, 2), lambda p, i: (p, i, 0))
    sems = [pltpu.SemaphoreType.DMA((6 * n_g,)), pltpu.SemaphoreType.DMA((6 * n_g,))] if n_g else []
    outs = pl.pallas_call(
        body, name=name, grid=(P, nq),
        in_specs=[tile, tile, whole, whole, whole, whole, pl.BlockSpec((1, tq, LANES), lambda p, i: (1, i, p))]
        + [ANY] * n_g,
        out_specs=[tile, tile, stat, stat] + [ANY] * n_g,
        out_shape=[jax.ShapeDtypeStruct((S, D), F32), jax.ShapeDtypeStruct((S, D), BF),
                   jax.ShapeDtypeStruct((P, S, 2), F32), jax.ShapeDtypeStruct((P, S, 2), F32)]
        + [jax.ShapeDtypeStruct((N_CHIPS,) + a.shape, a.dtype) for a in gather],
        scratch_shapes=[pltpu.VMEM((2, 2, tq, tq), F32)] + sems,
        compiler_params=_params("arbitrary" if n_g else "parallel", "arbitrary"),
    )(*qa, *ka, *va, qg, *gather)
    return tuple(outs[:4]) + (_fill_own(outs[4:], gather),)


def _attn_out_bwd(dy, w_out, o, qg, l, hd, name):
    S, D = o.shape
    P = D // LANES
    ts = _tile(S, 512, 2 * SUBLANES)

    def body(dy_ref, w_ref, o_ref, g_ref, l_ref, do_ref, dg_ref, e_ref):
        lo = _head_masks((ts, LANES), hd)
        lane2 = lax.broadcasted_iota(jnp.int32, (ts, 2), 1)
        dog = lax.dot_general(dy_ref[...].astype(BF), w_ref[...], NT_DIMS, preferred_element_type=F32)
        for t in range(P):
            cols = slice(t * LANES, (t + 1) * LANES)
            sg = jax.nn.sigmoid(g_ref[0, :, cols])
            dog_t, o_t, l_t = dog[:, cols], o_ref[:, cols], l_ref[t]
            g = (dog_t * sg / jnp.where(lo, l_t[:, 0:1], l_t[:, 1:2])).astype(BF)
            do_ref[:, cols] = g
            dg_ref[:, cols] = (dog_t * o_t * sg * (1.0 - sg)).astype(BF)
            prod = g.astype(F32) * o_t
            e0 = jnp.sum(jnp.where(lo, prod, 0.0), axis=-1, keepdims=True)
            e1 = jnp.sum(jnp.where(lo, 0.0, prod), axis=-1, keepdims=True)
            e_ref[t] = jnp.where(lane2 == 0, e0, e1)

    rows = pl.BlockSpec((ts, D), lambda s: (s, 0))
    stat = pl.BlockSpec((P, ts, 2), lambda s: (0, s, 0))
    return pl.pallas_call(
        body, name=name, grid=(S // ts,),
        in_specs=[rows, pl.BlockSpec(w_out.shape, lambda s: (0, 0)), rows,
                  pl.BlockSpec((1, ts, D), lambda s: (1, s, 0)), stat],
        out_specs=[rows, rows, stat],
        out_shape=[jax.ShapeDtypeStruct((S, D), BF), jax.ShapeDtypeStruct((S, D), BF),
                   jax.ShapeDtypeStruct((P, S, 2), F32)],
        compiler_params=_params("parallel"),
    )(dy, w_out, o, qg, l)


def _attn_bwd(qa, ka, vb, g, m_row, e_row, hd, name, exchange=()):
    S, D = vb.shape
    P = D // LANES
    tk = _attn_tile(S)
    nk = S // tk
    scale = hd ** -0.5
    n_x = len(exchange)

    def body(q0_ref, q1_ref, g_ref, k0_ref, k1_ref, v_ref, m_ref, e_ref, *rest):
        dq_ref, dk_ref, dv_ref, dc_ref = rest[n_x:n_x + 4]
        st_buf, dp_buf = rest[2 * n_x + 4:2 * n_x + 6]
        pair, ki = pl.program_id(0), pl.program_id(1)
        if n_x:
            copies = lambda: _chip_copies(rest[:n_x], rest[n_x + 4:2 * n_x + 4], *rest[2 * n_x + 6:])

            @pl.when((pair == 0) & (ki == 0))
            def _():
                for cp in copies():
                    cp.start()

        @pl.when(ki == 0)
        def _():
            dq_ref[...] = jnp.zeros_like(dq_ref)

        lo = _head_masks((tk, LANES), hd)
        kh = (k0_ref[...], k1_ref[...])
        q_refs = (q0_ref, q1_ref)
        vh = _split_heads(v_ref[...], lo)
        causal_t = lax.broadcasted_iota(jnp.int32, (tk, tk), 0) <= lax.broadcasted_iota(jnp.int32, (tk, tk), 1)

        def stage(qi, slot):
            off = pl.multiple_of(qi * tk, tk)
            gb = g_ref[pl.ds(off, tk), :]
            for h in range(2):
                st_buf[slot, h] = lax.dot_general(kh[h], q_refs[h][pl.ds(off, tk), :], NT_DIMS,
                                                  preferred_element_type=F32)
                dp_buf[slot, h] = lax.dot_general(vh[h], gb, NT_DIMS, preferred_element_type=F32)

        def consume(qi, slot, carry, masked):
            off = pl.multiple_of(qi * tk, tk)
            gb = g_ref[pl.ds(off, tk), :]
            m_t, e_t = m_ref[0, qi], e_ref[0, qi]
            out, dq_parts = [], []
            for h in range(2):
                dk, dv, dc = carry[h]
                qb = q_refs[h][pl.ds(off, tk), :]
                pt = jnp.exp2(st_buf[slot, h] - m_t[h:h + 1, :])
                if masked:
                    pt = jnp.where(causal_t, pt, 0.0)
                pb = pt.astype(BF)
                dv = dv + jnp.dot(pb, gb, preferred_element_type=F32)
                dst = pb.astype(F32) * (dp_buf[slot, h] - e_t[h:h + 1, :])
                db = dst.astype(BF)
                dk = dk + jnp.dot(db, qb, preferred_element_type=F32)
                dc = dc - jnp.sum(dst, axis=-1, keepdims=True)
                dq_parts.append(lax.dot_general(db, kh[h], TN_DIMS, preferred_element_type=F32))
                out.append((dk, dv, dc))
            dq_ref[pl.ds(off, tk), :] += jnp.where(lo, dq_parts[0], dq_parts[1]) * scale
            return tuple(out)

        n_after = nk - 1 - ki

        def step(j, carry):
            b = ki + 1 + 2 * j
            stage(b + 1, 0)
            carry = consume(b, 1, carry, False)
            stage(b + 2, 1)
            return consume(b + 1, 0, carry, False)

        def rest_one(carry):
            return consume(nk - 1, 1, carry, False)

        def rest_two(carry):
            stage(nk - 1, 0)
            return consume(nk - 1, 0, consume(nk - 2, 1, carry, False), False)

        init = tuple((jnp.zeros((tk, LANES), F32), jnp.zeros((tk, LANES), F32), jnp.zeros((tk, 1), F32))
                     for _ in range(2))
        stage(ki, 0)
        stage(jnp.minimum(ki + 1, nk - 1), 1)
        carry = consume(ki, 0, init, True)
        carry = lax.fori_loop(0, (n_after - 1) // 2, step, carry)
        which = jnp.where(n_after == 0, 0, 2 - n_after % 2)
        (dk0, dv0, dc0), (dk1, dv1, dc1) = lax.switch(which, [lambda c: c, rest_one, rest_two], carry)
        dk_ref[...] = jnp.where(lo, dk0, dk1) * (1.0 / LOG2E)
        dv_ref[...] = jnp.where(lo, dv0, dv1)
        lane2 = lax.broadcasted_iota(jnp.int32, (tk, 2), 1)
        dc_ref[0] = jnp.where(lane2 == 0, dc0, dc1)
        if n_x:
            @pl.when((pair == P - 1) & (ki == nk - 1))
            def _():
                for cp in copies():
                    cp.wait()

    tile = pl.BlockSpec((tk, LANES), lambda p, i: (i, p))
    whole = pl.BlockSpec((S, LANES), lambda p, i: (0, p))
    row_spec = pl.BlockSpec((1, nk, 2, tk), lambda p, i: (p, 0, 0, 0))
    sems = [pltpu.SemaphoreType.DMA((3 * n_x,)), pltpu.SemaphoreType.DMA((3 * n_x,))] if n_x else []
    outs = pl.pallas_call(
        body, name=name, grid=(P, nk),
        in_specs=[whole, whole, whole, tile, tile, tile, row_spec, row_spec] + [ANY] * n_x,
        out_specs=[whole, tile, tile, pl.BlockSpec((1, tk, 2), lambda p, i: (p, i, 0))] + [ANY] * n_x,
        out_shape=[jax.ShapeDtypeStruct((S, D), F32), jax.ShapeDtypeStruct((S, D), F32),
                   jax.ShapeDtypeStruct((S, D), F32), jax.ShapeDtypeStruct((P, S, 2), F32)]
        + [jax.ShapeDtypeStruct((3,) + a.shape[1:], a.dtype) for a in exchange],
        scratch_shapes=[pltpu.VMEM((2, 2, tk, tk), F32), pltpu.VMEM((2, 2, tk, tk), F32)] + sems,
        compiler_params=_params("arbitrary" if n_x else "parallel", "arbitrary"),
    )(*qa, g, *ka, vb, m_row, e_row, *exchange)
    return tuple(outs[:4]) + (list(outs[4:]),)


def _loss_head(y, t, name):
    S, D = y.shape
    ts = _tile(S, 512, SUBLANES)

    def body(y_ref, t_ref, dy_ref, l_ref):
        @pl.when(pl.program_id(0) == 0)
        def _():
            l_ref[...] = jnp.zeros_like(l_ref)

        e = y_ref[...] - t_ref[...]
        dy_ref[...] = e * (1.0 / D)
        part = 0.5 * jnp.sum(jnp.mean(e * e, axis=-1, keepdims=True), axis=0, keepdims=True)
        l_ref[...] += jnp.broadcast_to(part, l_ref.shape)

    return pl.pallas_call(
        body, name=name, grid=(S // ts,),
        in_specs=[pl.BlockSpec((ts, D), lambda s: (s, 0)), pl.BlockSpec((ts, D), lambda s: (s, 0))],
        out_specs=[pl.BlockSpec((ts, D), lambda s: (s, 0)), pl.BlockSpec((SUBLANES, LANES), lambda s: (0, 0))],
        out_shape=[jax.ShapeDtypeStruct((S, D), F32), jax.ShapeDtypeStruct((SUBLANES, LANES), F32)],
        compiler_params=_params("arbitrary"),
    )(y, t)


def _adamw(w, g, m, v, name):
    shape = w.shape
    cols = shape[-1]
    as2d = lambda a: a.reshape(-1, cols)
    rows = as2d(w).shape[0]
    tr = _tile(rows, 256, SUBLANES) if rows % SUBLANES == 0 else rows
    c1 = 1.0 - ADAM_B1 ** ADAM_STEP
    c2 = 1.0 - ADAM_B2 ** ADAM_STEP

    def body(w_ref, g_ref, m_ref, v_ref, d_ref, nm_ref, nv_ref):
        gg = g_ref[...]
        nm = ADAM_B1 * m_ref[...] + (1.0 - ADAM_B1) * gg
        nv = ADAM_B2 * v_ref[...] + (1.0 - ADAM_B2) * (gg * gg)
        d_ref[...] = -ADAM_LR * ((nm / c1) / (jnp.sqrt(nv / c2) + ADAM_EPS) + ADAM_WD * w_ref[...])
        nm_ref[...] = nm
        nv_ref[...] = nv

    spec = pl.BlockSpec((tr, cols), lambda r: (r, 0))
    outs = pl.pallas_call(
        body, name=name, grid=(rows // tr,), in_specs=[spec] * 4, out_specs=[spec] * 3,
        out_shape=[jax.ShapeDtypeStruct((rows, cols), F32)] * 3,
        compiler_params=_params("parallel"),
    )(as2d(w), as2d(g), as2d(m), as2d(v))
    return tuple(o.reshape(shape) for o in outs)


def _place():
    return lax.axis_index("x"), lax.axis_index("y"), lax.axis_index("c")


def _other_chips(x, y):
    return [(1 - x, y), (x, 1 - y), (1 - x, 1 - y)]


def _remote(src, dst, send_sems, recv_sems, k, to):
    return pltpu.make_async_remote_copy(src_ref=src, dst_ref=dst, send_sem=send_sems.at[k], recv_sem=recv_sems.at[k],
                                        device_id=to, device_id_type=MESH)


def _half(c, rh):
    return pl.ds(pl.multiple_of(c * rh, 2 * SUBLANES), rh)


def _gather_phases(rows, src, dst, send_sems, recv_sems):
    n = len(rows)
    x, y, c = _place()
    me = 2 * x + y
    sib = (x, y, 1 - c)
    chips = _other_chips(x, y)
    rh = [r // 2 for r in rows]

    def first():
        return [_remote(src[g].at[_half(c, rh[g])], dst[g].at[me, _half(c, rh[g])], send_sems, recv_sems,
                        6 * g + j, (cx, cy, c)) for j, (cx, cy) in enumerate(chips) for g in range(n)]

    def landed(j, g, core):
        cx, cy = chips[j]
        return dst[g].at[2 * cx + cy, _half(core, rh[g])]

    def passed():
        return [_remote(landed(j, g, c), landed(j, g, c), send_sems, recv_sems, 6 * g + 3 + j, sib)
                for j in range(3) for g in range(n)]

    def start():
        for cp in first():
            cp.start()

    def pass_on():
        cps = passed()
        for j in range(3):
            for g in range(n):
                _remote(landed(j, g, c), landed(j, g, c), send_sems, recv_sems, 6 * g + j, sib).wait_recv()
                cps[j * n + g].start()

    def finish():
        for j in range(3):
            for g in range(n):
                _remote(landed(j, g, 1 - c), landed(j, g, 1 - c), send_sems, recv_sems, 6 * g + 3 + j, sib).wait_recv()
        for cp in first() + passed():
            cp.wait_send()

    return start, pass_on, finish


def _fill_own(outs, srcs):
    x, y, _ = _place()
    return [lax.dynamic_update_slice_in_dim(o, a[None], 2 * x + y, axis=0) for o, a in zip(outs, srcs)]


def _allgather_weights(srcs):
    n = len(srcs)

    def body(*refs):
        for step in _gather_phases([a.shape[0] for a in srcs], refs[:n], refs[n:2 * n], *refs[2 * n:]):
            step()

    outs = pl.pallas_call(
        body, name="allgather_weights", in_specs=[ANY] * n, out_specs=[ANY] * n,
        out_shape=[jax.ShapeDtypeStruct((N_CHIPS,) + a.shape, a.dtype) for a in srcs],
        scratch_shapes=[pltpu.SemaphoreType.DMA((6 * n,)), pltpu.SemaphoreType.DMA((6 * n,))],
    )(*srcs)
    return _fill_own(outs, srcs)


def _pair_exchange(gs, name):
    n = len(gs)

    def body(*refs):
        g_refs, t_refs, (send_sems, recv_sems) = refs[:n], refs[n:2 * n], refs[2 * n:]
        x, y, c = _place()
        cps = [_remote(g_refs[g].at[k, 1 - c], t_refs[g].at[k], send_sems, recv_sems, N_CHIPS * g + k, (x, y, 1 - c))
               for g in range(n) for k in range(N_CHIPS)]
        for cp in cps:
            cp.start()
        for cp in cps:
            cp.wait()

    return pl.pallas_call(
        body, name=name, in_specs=[ANY] * n, out_specs=[ANY] * n,
        out_shape=[jax.ShapeDtypeStruct((a.shape[0],) + a.shape[2:], a.dtype) for a in gs],
        scratch_shapes=[pltpu.SemaphoreType.DMA((N_CHIPS * n,)), pltpu.SemaphoreType.DMA((N_CHIPS * n,))],
    )(*gs)


def _pair_add(g, t, c, name):
    n, _, rh, W = g.shape
    tr = _tile(rh, 256, 2 * SUBLANES)

    def body(c_ref, g_ref, t_ref, o_ref):
        o_ref[...] = (g_ref[0] + t_ref[...]).astype(BF)

    return pl.pallas_call(
        body, name=name,
        grid_spec=pltpu.PrefetchScalarGridSpec(
            num_scalar_prefetch=1, grid=(n, rh // tr),
            in_specs=[pl.BlockSpec((1, 1, tr, W), lambda k, i, c_ref: (k, c_ref[0], i, 0)),
                      pl.BlockSpec((1, tr, W), lambda k, i, c_ref: (k, i, 0))],
            out_specs=pl.BlockSpec((1, tr, W), lambda k, i, c_ref: (k, i, 0))),
        out_shape=jax.ShapeDtypeStruct((n, rh, W), BF),
        compiler_params=_params("parallel", "parallel"),
    )(c.reshape(1).astype(jnp.int32), g, t)


def _chip_copies(a_refs, t_refs, send_sems, recv_sems):
    x, y, c = _place()
    return [_remote(a_refs[g].at[2 * cx + cy], t_refs[g].at[j], send_sems, recv_sems, 3 * g + j, (cx, cy, c))
            for j, (cx, cy) in enumerate(_other_chips(x, y)) for g in range(len(a_refs))]


def _chip_exchange(parts, name):
    n = len(parts)

    def body(*refs):
        cps = _chip_copies(refs[:n], refs[n:2 * n], *refs[2 * n:])
        for cp in cps:
            cp.start()
        for cp in cps:
            cp.wait()

    return pl.pallas_call(
        body, name=name, in_specs=[ANY] * n, out_specs=[ANY] * n,
        out_shape=[jax.ShapeDtypeStruct((3,) + a.shape[1:], a.dtype) for a in parts],
        scratch_shapes=[pltpu.SemaphoreType.DMA((3 * n,)), pltpu.SemaphoreType.DMA((3 * n,))],
    )(*parts)


def _chip_add(g, t1, t2, c, me, name):
    _, _, rh, W = g.shape
    tr = _tile(rh, 256, 2 * SUBLANES)

    def body(c_ref, me_ref, g_ref, t1_ref, t2_ref, o_ref):
        own = g_ref[0, 0] + t1_ref[0]
        o_ref[...] = own + t2_ref[0].astype(F32) + t2_ref[1].astype(F32) + t2_ref[2].astype(F32)

    return pl.pallas_call(
        body, name=name,
        grid_spec=pltpu.PrefetchScalarGridSpec(
            num_scalar_prefetch=2, grid=(rh // tr,),
            in_specs=[pl.BlockSpec((1, 1, tr, W), lambda i, c_ref, me_ref: (me_ref[0], c_ref[0], i, 0)),
                      pl.BlockSpec((1, tr, W), lambda i, c_ref, me_ref: (me_ref[0], i, 0)),
                      pl.BlockSpec((3, tr, W), lambda i, c_ref, me_ref: (0, i, 0))],
            out_specs=pl.BlockSpec((tr, W), lambda i, c_ref, me_ref: (i, 0))),
        out_shape=jax.ShapeDtypeStruct((rh, W), F32),
        compiler_params=_params("parallel"),
    )(c.reshape(1).astype(jnp.int32), me.reshape(1).astype(jnp.int32), g, t1, t2)


def _pair_share(hs, name):
    n = len(hs)

    def body(*refs):
        h_refs, f_refs, (send_sems, recv_sems) = refs[:n], refs[n:2 * n], refs[2 * n:]
        x, y, c = _place()
        cps = [_remote(h_refs[g], f_refs[g], send_sems, recv_sems, g, (x, y, 1 - c)) for g in range(n)]
        for cp in cps:
            cp.start()
        for cp in cps:
            cp.wait()

    return pl.pallas_call(
        body, name=name, in_specs=[ANY] * n, out_specs=[ANY] * n,
        out_shape=[jax.ShapeDtypeStruct(a.shape, a.dtype) for a in hs],
        scratch_shapes=[pltpu.SemaphoreType.DMA((n,)), pltpu.SemaphoreType.DMA((n,))],
    )(*hs)


def _allreduce_small(pack, name):
    rows, W = pack.shape

    def body(p_ref, o_ref, buf, send_sems, recv_sems):
        x, y, c = _place()
        me = 4 * x + 2 * y + c
        buf[me] = p_ref[...]
        cps = []
        for r in range(1, 8):
            fx, fy, fc = (r >> 2) & 1, (r >> 1) & 1, r & 1
            to = (1 - x if fx else x, 1 - y if fy else y, 1 - c if fc else c)
            cps.append(_remote(p_ref, buf.at[me], send_sems, recv_sems, r - 1, to))
        for cp in cps:
            cp.start()
        for r in range(1, 8):
            fx, fy, fc = (r >> 2) & 1, (r >> 1) & 1, r & 1
            frm = 4 * (1 - x if fx else x) + 2 * (1 - y if fy else y) + (1 - c if fc else c)
            _remote(p_ref, buf.at[frm], send_sems, recv_sems, r - 1, (x, y, c)).wait_recv()
        for cp in cps:
            cp.wait_send()
        acc = buf[0]
        for i in range(1, 8):
            acc = acc + buf[i]
        o_ref[...] = acc

    return pl.pallas_call(
        body, name=name, in_specs=[VMEM], out_specs=VMEM,
        out_shape=jax.ShapeDtypeStruct((rows, W), F32),
        scratch_shapes=[pltpu.VMEM((8, rows, W), F32), pltpu.SemaphoreType.DMA((7,)), pltpu.SemaphoreType.DMA((7,))],
    )(pack)


def _width_groups(arrs):
    widths = []
    for a in arrs:
        if a.shape[-1] not in widths:
            widths.append(a.shape[-1])
    return [[i for i, a in enumerate(arrs) if a.shape[-1] == w] for w in widths]


def _rows2d(a):
    return a.reshape(-1, a.shape[-1])


def _split_rows_like(buf, like, lead=()):
    out, off = [], 0
    for a in like:
        n = a.size // a.shape[-1]
        out.append(buf[..., off:off + n, :].reshape(tuple(lead) + a.shape))
        off += n
    return out


def _join_cols(g):
    nd = g.ndim
    return jnp.moveaxis(g, 0, nd - 2).reshape(g.shape[1:-1] + (N_CHIPS * g.shape[-1],))


def _join_rows(g):
    return jnp.moveaxis(g, 0, 1).reshape(g.shape[1], N_CHIPS * g.shape[2], g.shape[3])


def _row_layout(a, tq):
    P, S, _ = a.shape
    return a.reshape(P, S // tq, tq, 2).transpose(0, 1, 3, 2)


def _pad_row(v, width=FLAT_W):
    flat = v.reshape(-1)
    rows = -(-flat.shape[0] // width)
    return jnp.pad(flat, (0, rows * width - flat.shape[0]))


def kernel(x, attn_norm, ffn_norm, a_w_in, a_conv, a_w_out, kv_norm, w_kvf, b_f, k_norm, b_w_qg, q_norm, b_w_out, ffn_w_up, ffn_conv, ffn_w_down, loss_target, m_attn_norm, m_ffn_norm, m_a_w_in, m_a_conv, m_a_w_out, m_kv_norm, m_w_kvf, m_b_f, m_k_norm, m_b_w_qg, m_q_norm, m_b_w_out, m_ffn_w_up, m_ffn_conv, m_ffn_w_down, v_attn_norm, v_ffn_norm, v_a_w_in, v_a_conv, v_a_w_out, v_kv_norm, v_w_kvf, v_b_f, v_k_norm, v_b_w_qg, v_q_norm, v_b_w_out, v_ffn_w_up, v_ffn_conv, v_ffn_w_down):
    xs = x[0]
    S, D = xs.shape
    H, hd = b_f.shape[0], k_norm.shape[0]
    depth = attn_norm.shape[0]
    n_a = a_w_in.shape[0]
    P = D // LANES
    assert LANES == 2 * hd and H * hd == D, "the attention kernels hold two heads per lane tile"
    mx, my, mc = _place()
    chip = 2 * mx + my

    big = [a_w_in, a_w_out, w_kvf, b_w_qg, b_w_out, ffn_w_up, ffn_w_down]
    groups = _width_groups(big)
    assert n_a >= 2 and depth - n_a >= 2, "the hosted weight gathers are laid out for two layers of each kind"
    first = [a_w_in[:1]]
    behind_a_in = {0: [a_w_out[:1], ffn_w_up[:1]], 1: [ffn_w_up[1:n_a]]}
    behind_ffn_up = {0: [ffn_w_down[:1], a_w_in[1:], a_w_out[1:]], 1: [ffn_w_down[1:n_a], w_kvf, b_w_qg[:1]]}
    late = [b_w_qg[1:], b_w_out, ffn_w_up[n_a:], ffn_w_down[n_a:]]

    def packed(ws):
        idx_groups = _width_groups(ws)
        return [jnp.concatenate([_rows2d(ws[i]).astype(BF) for i in idx]) for idx in idx_groups], idx_groups

    def unpacked(bufs, idx_groups, ws):
        out = [None] * len(ws)
        for idx, buf in zip(idx_groups, bufs):
            for i, part in zip(idx, _split_rows_like(buf, [ws[i] for i in idx], (N_CHIPS,))):
                out[i] = part
        return out

    first_src, first_groups = packed(first)
    late_src, late_groups = packed(late)
    (g_in,) = unpacked(_allgather_weights(first_src), first_groups, first)
    wa_in, wa_out, w_up, w_down, wb_qg, wb_out = list(_join_cols(g_in)), [], [], [], [], []
    kvf_cols = 2 * D + LANES

    def placed(shard):
        full = jnp.zeros(shard.shape[:-1] + (N_CHIPS, shard.shape[-1]), F32)
        full = lax.dynamic_update_slice_in_dim(full, shard[..., None, :], chip, axis=full.ndim - 2)
        return jnp.where(mc == 0, full, 0.0).reshape(-1)

    conv_pack = jnp.concatenate([_pad_row(placed(a_conv)), _pad_row(placed(ffn_conv))]).reshape(-1, FLAT_W)
    conv_full = _allreduce_small(conv_pack, "allgather_conv_taps").reshape(-1)
    n_ac = a_conv.size * N_CHIPS
    a_conv_f = conv_full[:n_ac].reshape(a_conv.shape[:-1] + (-1,))
    off = _pad_row(placed(a_conv)).shape[0]
    ffn_conv_f = conv_full[off:off + ffn_conv.size * N_CHIPS].reshape(ffn_conv.shape[:-1] + (-1,))
    F = ffn_conv_f.shape[-1]

    b_pad = jnp.pad(b_f, (0, LANES - H)).reshape(1, LANES)
    gate_blk = 2 * D // LANES
    tq = _attn_tile(S)
    scale = hd ** -0.5

    saved = []
    cur = xs
    kv = None
    for l in range(depth):
        rec = {"x_in": cur}
        if l < n_a:
            ws = behind_a_in.get(l, [])
            srcs, idx_groups = packed(ws) if ws else ((), [])
            proj, xn, z, bufs = _mixer_in_fwd(cur, attn_norm[l], wa_in[l], a_conv_f[l], f"a_in_{l}", gather=srcs)
            if l == 0:
                g_out, g_up = unpacked(bufs, idx_groups, ws)
                wa_out += list(_join_rows(g_out))
                w_up += list(_join_cols(g_up))
            if l == 1:
                (g_up,) = unpacked(bufs, idx_groups, ws)
                w_up += list(_join_cols(g_up))
            mid = _matmul_residual(z, wa_out[l], cur, f"a_out_{l}")
            rec.update(proj=proj, xn=xn, z=z)
        else:
            j = l - n_a
            if kv is None:
                kvf, hn = _norm_matmul(cur, kv_norm, wkvf, 1, F32, "kvf_proj")
                vb = kvf[0, :, D:2 * D].astype(BF)
                cgate = _gate_fwd(kvf, b_pad, gate_blk, "gate_cumsum")
                kv = dict(kvf=kvf, hn=hn, vb=vb, cgate=cgate, x_in=cur, dk=[], dv=[], dc=[],
                          ka=_augment(kvf, 0, 0, cgate, "k", hd, D, "k_augment", norm_w=k_norm),
                          va=_augment(kvf, 0, 1, cgate, "v", hd, D, "v_augment"))
            qg, xn = _norm_matmul(cur, attn_norm[l], wb_qg[j], 2, F32, f"qg_proj_{j}")
            qa = _augment(qg, 0, 0, kv["cgate"], "q", hd, D, f"q_augment_{j}", norm_w=q_norm[j], scale=scale * LOG2E)
            o, og, m_max, l_sum, late_bufs = _attn_fwd(qa, kv["ka"], kv["va"], qg, hd, f"attn_fwd_{j}",
                                                       gather=late_src if j == 0 else ())
            if j == 0:
                g_qg, g_bout, g_up, g_down = unpacked(late_bufs, late_groups, late)
                wb_qg += list(_join_cols(g_qg))
                wb_out += list(_join_rows(g_bout))
                w_up += list(_join_cols(g_up))
                w_down += list(_join_rows(g_down))
            mid = _matmul_residual(og, wb_out[j], cur, f"b_out_{j}")
            rec.update(qg=qg, xn=xn, qa=qa, o=o, og=og, m=m_max, l=l_sum)
        ws = behind_ffn_up.get(l, [])
        srcs, idx_groups = packed(ws) if ws else ((), [])
        up, xn2, z2, bufs = _ffn_up_fwd(mid, ffn_norm[l], w_up[l], ffn_conv_f[l], f"ffn_up_{l}", gather=srcs)
        if l == 0:
            g_down, g_in, g_out = unpacked(bufs, idx_groups, ws)
            w_down += list(_join_rows(g_down))
            wa_in += list(_join_cols(g_in))
            wa_out += list(_join_rows(g_out))
        if l == 1:
            g_down, g_kvf, g_qg = unpacked(bufs, idx_groups, ws)
            w_down += list(_join_rows(g_down))
            wkvf = jnp.pad(_join_cols(g_kvf), ((0, 0), (0, kvf_cols - (2 * D + H))))
            wb_qg += list(_join_cols(g_qg))
        cur = _matmul_residual(z2, w_down[l], mid, f"ffn_down_{l}")
        rec.update(x_mid=mid, up=up, xn2=xn2, z2=z2)
        saved.append(rec)

    dy, loss_part = _loss_head(cur, loss_target[0], "loss_head")

    def cols_of(g, k):
        c = g.shape[-1] // N_CHIPS
        return g[:, k * c:(k + 1) * c]

    def rows_of(g, k):
        r = g.shape[0] // N_CHIPS
        return g[k * r:(k + 1) * r]

    def reduce_scatter(items, tag, host=None):
        keys = list(items)
        likes = [items[k][1](items[k][0], 0) for k in keys]
        idx_groups = _width_groups(likes)

        def group_buffer(idx):
            rows = [items[keys[i]][1](items[keys[i]][0], k) for k in range(N_CHIPS) for i in idx]
            buf = jnp.concatenate(rows)
            return buf.reshape(N_CHIPS, 2, buf.shape[0] // (2 * N_CHIPS), buf.shape[1])

        g4 = [group_buffer(idx) for idx in idx_groups]
        from_sibling = _pair_exchange(g4, f"grad_pair_exchange_{tag}")
        parts = [_pair_add(g, t, mc, f"grad_pair_add_{tag}{n}") for n, (g, t) in enumerate(zip(g4, from_sibling))]
        from_chips = _chip_exchange(parts, f"grad_chip_exchange_{tag}") if host is None else host(parts)
        mine = [_chip_add(g, t1, t2, mc, chip, f"grad_chip_add_{tag}{n}") for n, (g, t1, t2) in
                enumerate(zip(g4, from_sibling, from_chips))]
        theirs = _pair_share(mine, f"grad_pair_share_{tag}")
        out = {}
        for idx, m_half, t_half in zip(idx_groups, mine, theirs):
            shard = jnp.where(mc == 0, jnp.concatenate([m_half, t_half]), jnp.concatenate([t_half, m_half]))
            for i, part in zip(idx, _split_rows_like(shard, [likes[i] for i in idx])):
                out[keys[i]] = part
        return out

    reduced = {}

    g_attn_norm, g_ffn_norm = [None] * depth, [None] * depth
    g_a_in, g_a_conv, g_a_out = [None] * n_a, [None] * n_a, [None] * n_a
    g_qg, g_qn, g_bo = [None] * (depth - n_a), [None] * (depth - n_a), [None] * (depth - n_a)
    g_up, g_fc, g_down = [None] * depth, [None] * depth, [None] * depth
    for l in reversed(range(depth)):
        rec = saved[l]
        if l == n_a - 1:
            ready = {(2, 0): (g_kvf, cols_of), (3, 0): (g_qg[0], cols_of)}
        elif l < n_a - 1:
            ready = {(5, l + 1): (g_up[l + 1], cols_of), (6, l + 1): (g_down[l + 1], rows_of),
                     (0, l + 1): (g_a_in[l + 1], cols_of), (1, l + 1): (g_a_out[l + 1], rows_of)}
        else:
            ready = {}
        if ready:
            held = {}

            def behind_mid_bwd(parts):
                *held["out"], exchanged = _ffn_mid_bwd(rec["up"], dy, w_down[l], ffn_conv_f[l], f"ffn_mid_bwd_{l}",
                                                       exchange=parts)
                return exchanged

            reduced.update(reduce_scatter(ready, f"a{l}", behind_mid_bwd))
            dup, g_fc[l] = held["out"]
        else:
            dup, g_fc[l], _ = _ffn_mid_bwd(rec["up"], dy, w_down[l], ffn_conv_f[l], f"ffn_mid_bwd_{l}")
        g_down[l] = _wgrad(rec["z2"], dy[None], f"ffn_down_wgrad_{l}")
        g_up[l] = _wgrad(rec["xn2"], dup, f"ffn_up_wgrad_{l}")
        dy, g_ffn_norm[l] = _dnorm(dup, w_up[l], rec["x_mid"], ffn_norm[l], dy, f"ffn_up_bwd_{l}")
        if l < n_a:
            dproj, g_a_conv[l] = _mixer_mid_bwd(rec["proj"], dy, wa_out[l], a_conv_f[l], f"a_mid_bwd_{l}")
            g_a_out[l] = _wgrad(rec["z"], dy[None], f"a_out_wgrad_{l}")
            g_a_in[l] = _wgrad(rec["xn"], dproj, f"a_in_wgrad_{l}")
            dy, g_attn_norm[l] = _dnorm(dproj, wa_in[l], rec["x_in"], attn_norm[l], dy, f"a_in_bwd_{l}")
        else:
            j = l - n_a
            g_out, dgate, evec = _attn_out_bwd(dy, wb_out[j], rec["o"], rec["qg"], rec["l"], hd,
                                               f"attn_gate_bwd_{j}")
            g_bo[j] = _wgrad(rec["og"], dy[None], f"b_out_wgrad_{j}")
            ready = {(5, l): (g_up[l], cols_of), (6, l): (g_down[l], rows_of), (4, j): (g_bo[j], rows_of)}
            if j + 1 < depth - n_a:
                ready[(3, j + 1)] = (g_qg[j + 1], cols_of)
            held = {}

            def behind_attn_bwd(parts):
                *held["grads"], exchanged = _attn_bwd(rec["qa"], kv["ka"], kv["vb"], g_out, _row_layout(rec["m"], tq),
                                                      _row_layout(evec, tq), hd, f"attn_bwd_{j}", exchange=parts)
                return exchanged

            reduced.update(reduce_scatter(ready, f"b{j}", behind_attn_bwd))
            dqn, dk, dv, dc = held["grads"]
            kv["dk"].append(dk)
            kv["dv"].append(dv)
            kv["dc"].append(dc)
            dq_pre, g_qn[j] = _headnorm_bwd(rec["qg"], 0, 0, q_norm[j], [dqn], D, f"q_norm_bwd_{j}")
            dqg = jnp.stack([dq_pre, dgate])
            g_qg[j] = _wgrad(rec["xn"], dqg, f"qg_wgrad_{j}")
            dy, g_attn_norm[l] = _dnorm(dqg, wb_qg[j], rec["x_in"], attn_norm[l], dy, f"qg_bwd_{j}")
            if l == n_a:
                dk_s, g_k_norm = _headnorm_bwd(kv["kvf"], 0, 0, k_norm, kv["dk"], D, "k_norm_bwd")
                dv_s = functools.reduce(jnp.add, kv["dv"]).astype(BF)
                dc_sum = functools.reduce(jnp.add, kv["dc"])
                dc_pad = jnp.pad(dc_sum.transpose(1, 0, 2).reshape(S, H), ((0, 0), (0, LANES - H)))
                df, db = _gate_bwd(dc_pad, kv["kvf"], b_pad, gate_blk, "gate_bwd")
                dkvf = jnp.concatenate([dk_s, dv_s, df.astype(BF)], axis=1)[None]
                g_kvf = _wgrad(kv["hn"], dkvf, "kvf_wgrad")[:, :2 * D + H]
                g_b_f = db[:H]
                dy, g_kv_norm = _dnorm(dkvf, wkvf, kv["x_in"], kv_norm, dy, "kvf_bwd")
    grad_x = dy[None]

    last = {(0, 0): (g_a_in[0], cols_of), (1, 0): (g_a_out[0], rows_of), (5, 0): (g_up[0], cols_of),
            (6, 0): (g_down[0], rows_of)}
    reduced.update(reduce_scatter(last, "c"))
    big_grads = [jnp.concatenate([reduced[(i, l)] for l in range(w.shape[0] if w.ndim == 3 else 1)]).reshape(w.shape)
                 for i, w in enumerate(big)]

    small = [loss_part[0, :1], jnp.stack(g_attn_norm), jnp.stack(g_ffn_norm), g_kv_norm, g_b_f, g_k_norm,
             jnp.stack(g_qn), jnp.stack(g_a_conv), jnp.stack(g_fc)]
    small_sum = _allreduce_small(jnp.concatenate([_pad_row(s) for s in small]).reshape(-1, FLAT_W),
                                 "allreduce_small_grads").reshape(-1)
    parts, off = [], 0
    for s in small:
        parts.append(small_sum[off:off + s.size].reshape(s.shape))
        off += _pad_row(s).shape[0]
    loss = parts[0][0]
    gr_attn_norm, gr_ffn_norm, gr_kv_norm, gr_b_f, gr_k_norm, gr_q_norm, gr_a_conv_full, gr_ffn_conv_full = parts[1:]

    def my_cols(full):
        c = full.shape[-1] // N_CHIPS
        return lax.dynamic_slice_in_dim(full, chip * c, c, axis=full.ndim - 1)

    gr_a_in, gr_a_out, gr_kvf, gr_qg, gr_bo, gr_up, gr_down = big_grads
    grads = [gr_attn_norm, gr_ffn_norm, gr_a_in, my_cols(gr_a_conv_full), gr_a_out, gr_kv_norm, gr_kvf, gr_b_f,
             gr_k_norm, gr_qg, gr_q_norm, gr_bo, gr_up, my_cols(gr_ffn_conv_full), gr_down]
    weights = [attn_norm, ffn_norm, a_w_in, a_conv, a_w_out, kv_norm, w_kvf, b_f, k_norm, b_w_qg, q_norm, b_w_out,
               ffn_w_up, ffn_conv, ffn_w_down]
    ms = [m_attn_norm, m_ffn_norm, m_a_w_in, m_a_conv, m_a_w_out, m_kv_norm, m_w_kvf, m_b_f, m_k_norm, m_b_w_qg,
          m_q_norm, m_b_w_out, m_ffn_w_up, m_ffn_conv, m_ffn_w_down]
    vs = [v_attn_norm, v_ffn_norm, v_a_w_in, v_a_conv, v_a_w_out, v_kv_norm, v_w_kvf, v_b_f, v_k_norm, v_b_w_qg,
          v_q_norm, v_b_w_out, v_ffn_w_up, v_ffn_conv, v_ffn_w_down]
    deltas, new_ms, new_vs = [], [], []
    for i, (w, g, m, v) in enumerate(zip(weights, grads, ms, vs)):
        d, nm, nv = _adamw(w, g, m, v, f"adamw_{i}")
        deltas.append(d)
        new_ms.append(nm)
        new_vs.append(nv)
    return (loss, grad_x, *grads, *deltas, *new_ms, *new_vs)
```

```python
import functools

import jax
import jax.numpy as jnp
from jax import lax
from jax.experimental import pallas as pl
from jax.experimental.pallas import tpu as pltpu

F32 = jnp.float32
BF = jnp.bfloat16
LANES = 128
SUBLANES = 8
RMS_EPS = 1e-6
LOG2E = 1.4426950408889634
FLAT_W = 1024
N_CHIPS = 4
CONV_W = 3
HALO = SUBLANES

ADAM_LR = 0.001
ADAM_B1 = 0.9
ADAM_B2 = 0.999
ADAM_EPS = 1e-08
ADAM_WD = 0.01
ADAM_STEP = 10

MESH = pl.DeviceIdType.MESH
ANY = pl.BlockSpec(memory_space=pl.ANY)
VMEM = pl.BlockSpec(memory_space=pltpu.VMEM)
NT_DIMS = (((1,), (1,)), ((), ()))
TN_DIMS = (((0,), (0,)), ((), ()))


def _tile(n, pref, mult=LANES):
    t = (min(pref, n) // mult) * mult
    while t >= mult:
        if n % t == 0:
            break
        t -= mult
    if t < mult or (t * 4 < pref and n <= 4 * pref):
        return n
    return t


def _params(*sem):
    return pltpu.CompilerParams(dimension_semantics=sem)


def _norm_matmul(x, g, w, parts, out_dtype, name):
    S, D = x.shape
    C = w.shape[1] // parts
    ts, tn = _tile(S, 512, SUBLANES), _tile(C, 1408)
    npc = C // tn

    def body(x_ref, g_ref, w_ref, o_ref, xn_ref):
        @pl.when(pl.program_id(1) == 0)
        def _():
            xf = x_ref[...]
            r = lax.rsqrt(jnp.mean(xf * xf, axis=-1, keepdims=True) + RMS_EPS)
            xn_ref[...] = (xf * r * g_ref[...]).astype(BF)

        o_ref[0] = jnp.dot(xn_ref[...], w_ref[...], preferred_element_type=F32).astype(out_dtype)

    return pl.pallas_call(
        body, name=name, grid=(S // ts, parts * npc),
        in_specs=[pl.BlockSpec((ts, D), lambda s, n: (s, 0)),
                  pl.BlockSpec((1, D), lambda s, n: (0, 0)),
                  pl.BlockSpec((D, tn), lambda s, n: (0, n))],
        out_specs=[pl.BlockSpec((1, ts, tn), lambda s, n: (n // npc, s, n % npc)),
                   pl.BlockSpec((ts, D), lambda s, n: (s, 0))],
        out_shape=[jax.ShapeDtypeStruct((parts, S, C), out_dtype), jax.ShapeDtypeStruct((S, D), BF)],
        compiler_params=_params("parallel", "arbitrary"),
    )(x, g.reshape(1, D), w)


def _shift_down(u, prev, k):
    r = pltpu.roll(u, k, 0)
    row = lax.broadcasted_iota(jnp.int32, (HALO, u.shape[1]), 0)
    head = r[0:HALO]
    for j in range(k):
        head = jnp.where(row == j, prev[HALO - k + j:HALO - k + j + 1, :], head)
    return jnp.concatenate([head, r[HALO:]], axis=0)


def _shift_up(d, nxt, k):
    n = d.shape[0]
    r = pltpu.roll(d, n - k, 0)
    row = lax.broadcasted_iota(jnp.int32, (HALO, d.shape[1]), 0)
    tail = r[n - HALO:n]
    for j in range(k):
        tail = jnp.where(row == HALO - k + j, nxt[j:j + 1, :], tail)
    return jnp.concatenate([r[0:n - HALO], tail], axis=0)


def _conv3(u, prev, w):
    return _shift_down(u, prev, 2) * w[0:1] + _shift_down(u, prev, 1) * w[1:2] + u * w[2:3]


def _conv3_t(d, nxt, w):
    return d * w[2:3] + _shift_up(d, nxt, 1) * w[1:2] + _shift_up(d, nxt, 2) * w[0:1]


def _tap_rows(t0, t1, t2):
    row = lax.broadcasted_iota(jnp.int32, (SUBLANES, t0.shape[1]), 0)
    return jnp.where(row == 0, t0, jnp.where(row == 1, t1, jnp.where(row == 2, t2, 0.0)))


def _pad_conv(cw):
    return jnp.pad(cw, ((0, SUBLANES - CONV_W), (0, 0)))


def _mixer_in_fwd(x, g, w, cw, name, gather=()):
    S, D = x.shape
    C = w.shape[1] // 3
    ts, tc = _tile(S, 512, SUBLANES), _tile(C, 1024)
    nc = C // tc
    n_s = S // ts
    n_g = len(gather)

    def body(x_ref, g_ref, wb_ref, wc_ref, wh_ref, cw_ref, *rest):
        p_ref, xn_ref, z_ref = rest[n_g:n_g + 3]
        carry = rest[2 * n_g + 3]
        s, c = pl.program_id(0), pl.program_id(1)
        if n_g:
            start, pass_on, finish = _gather_phases([a.shape[0] for a in gather], rest[:n_g],
                                                    rest[n_g + 3:2 * n_g + 3], *rest[2 * n_g + 4:])
            pl.when((s == 0) & (c == 0))(start)
            pl.when((s == (3 * n_s) // 4) & (c == 0))(pass_on)

        @pl.when(c == 0)
        def _():
            xf = x_ref[...]
            r = lax.rsqrt(jnp.mean(xf * xf, axis=-1, keepdims=True) + RMS_EPS)
            xn_ref[...] = (xf * r * g_ref[...]).astype(BF)

        @pl.when(s == 0)
        def _():
            carry[c] = jnp.zeros((HALO, tc), F32)

        xn = xn_ref[...]
        parts = [jnp.dot(xn, w_ref[...], preferred_element_type=F32).astype(BF) for w_ref in (wb_ref, wc_ref, wh_ref)]
        for p, v in enumerate(parts):
            p_ref[p] = v
        u = parts[1].astype(F32) * parts[2].astype(F32)
        cv = _conv3(u, carry[c], cw_ref[...])
        z_ref[...] = (parts[0].astype(F32) * cv).astype(BF)
        carry[c] = u[ts - HALO:ts, :]
        if n_g:
            pl.when((s == n_s - 1) & (c == nc - 1))(finish)

    wspec = lambda p: pl.BlockSpec((D, tc), lambda s, c: (0, p * nc + c))
    sems = [pltpu.SemaphoreType.DMA((6 * n_g,)), pltpu.SemaphoreType.DMA((6 * n_g,))] if n_g else []
    outs = pl.pallas_call(
        body, name=name, grid=(n_s, nc),
        in_specs=[pl.BlockSpec((ts, D), lambda s, c: (s, 0)), pl.BlockSpec((1, D), lambda s, c: (0, 0)),
                  wspec(0), wspec(1), wspec(2), pl.BlockSpec((SUBLANES, tc), lambda s, c: (0, c))] + [ANY] * n_g,
        out_specs=[pl.BlockSpec((3, ts, tc), lambda s, c: (0, s, c)),
                   pl.BlockSpec((ts, D), lambda s, c: (s, 0)),
                   pl.BlockSpec((ts, tc), lambda s, c: (s, c))] + [ANY] * n_g,
        out_shape=[jax.ShapeDtypeStruct((3, S, C), BF), jax.ShapeDtypeStruct((S, D), BF),
                   jax.ShapeDtypeStruct((S, C), BF)]
        + [jax.ShapeDtypeStruct((N_CHIPS,) + a.shape, a.dtype) for a in gather],
        scratch_shapes=[pltpu.VMEM((nc, HALO, tc), F32)] + sems,
        compiler_params=_params("arbitrary", "arbitrary"),
    )(x, g.reshape(1, D), w, w, w, _pad_conv(cw), *gather)
    return tuple(outs[:3]) + (_fill_own(outs[3:], gather),)


def _ffn_up_fwd(x, g, w, cw, name, gather=()):
    S, D = x.shape
    C = w.shape[1] // 2
    ts, tc = _tile(S, 512, SUBLANES), _tile(C, 1408)
    nc = C // tc
    n_s = S // ts
    n_g = len(gather)

    def body(x_ref, g_ref, wa_ref, wg_ref, cw_ref, *rest):
        up_ref, xn_ref, z_ref = rest[n_g:n_g + 3]
        carry = rest[2 * n_g + 3]
        s, c = pl.program_id(0), pl.program_id(1)
        if n_g:
            start, pass_on, finish = _gather_phases([a.shape[0] for a in gather], rest[:n_g],
                                                    rest[n_g + 3:2 * n_g + 3], *rest[2 * n_g + 4:])
            pl.when((s == 0) & (c == 0))(start)
            pl.when((s == (3 * n_s) // 4) & (c == 0))(pass_on)

        @pl.when(c == 0)
        def _():
            xf = x_ref[...]
            r = lax.rsqrt(jnp.mean(xf * xf, axis=-1, keepdims=True) + RMS_EPS)
            xn_ref[...] = (xf * r * g_ref[...]).astype(BF)

        @pl.when(s == 0)
        def _():
            carry[c] = jnp.zeros((HALO, tc), F32)

        xn = xn_ref[...]
        a_b = jnp.dot(xn, wa_ref[...], preferred_element_type=F32).astype(BF)
        g_b = jnp.dot(xn, wg_ref[...], preferred_element_type=F32).astype(BF)
        up_ref[0] = a_b
        up_ref[1] = g_b
        a_pre = a_b.astype(F32)
        a = _conv3(a_pre, carry[c], cw_ref[...])
        z_ref[...] = (a * jax.nn.sigmoid(a) * g_b.astype(F32)).astype(BF)
        carry[c] = a_pre[ts - HALO:ts, :]
        if n_g:
            pl.when((s == n_s - 1) & (c == nc - 1))(finish)

    sems = [pltpu.SemaphoreType.DMA((6 * n_g,)), pltpu.SemaphoreType.DMA((6 * n_g,))] if n_g else []
    outs = pl.pallas_call(
        body, name=name, grid=(n_s, nc),
        in_specs=[pl.BlockSpec((ts, D), lambda s, c: (s, 0)),
                  pl.BlockSpec((1, D), lambda s, c: (0, 0)),
                  pl.BlockSpec((D, tc), lambda s, c: (0, c)),
                  pl.BlockSpec((D, tc), lambda s, c: (0, nc + c)),
                  pl.BlockSpec((SUBLANES, tc), lambda s, c: (0, c))] + [ANY] * n_g,
        out_specs=[pl.BlockSpec((2, ts, tc), lambda s, c: (0, s, c)),
                   pl.BlockSpec((ts, D), lambda s, c: (s, 0)),
                   pl.BlockSpec((ts, tc), lambda s, c: (s, c))] + [ANY] * n_g,
        out_shape=[jax.ShapeDtypeStruct((2, S, C), BF), jax.ShapeDtypeStruct((S, D), BF),
                   jax.ShapeDtypeStruct((S, C), BF)]
        + [jax.ShapeDtypeStruct((N_CHIPS,) + a.shape, a.dtype) for a in gather],
        scratch_shapes=[pltpu.VMEM((nc, HALO, tc), F32)] + sems,
        compiler_params=_params("arbitrary", "arbitrary"),
    )(x, g.reshape(1, D), w, w, _pad_conv(cw), *gather)
    return tuple(outs[:3]) + (_fill_own(outs[3:], gather),)


def _mixer_mid_bwd(proj, dy, w_out, cw, name):
    _, S, C = proj.shape
    D = dy.shape[1]
    ts, tc = _tile(S, 512, SUBLANES), _tile(C, 1024)
    n_s = S // ts
    per = ts // HALO

    def body(b_ref, c_ref, h_ref, dy_ref, w_ref, cp_ref, hp_ref, cw_ref, d_ref, dcw_ref, carry):
        i = pl.program_id(1)
        w = cw_ref[...]
        dz = lax.dot_general(dy_ref[...].astype(BF), w_ref[...], NT_DIMS, preferred_element_type=F32)
        b, c, h = b_ref[0].astype(F32), c_ref[0].astype(F32), h_ref[0].astype(F32)
        u = c * h
        prev = jnp.where(i < n_s - 1, cp_ref[0].astype(F32) * hp_ref[0].astype(F32), 0.0)
        u1, u2 = _shift_down(u, prev, 1), _shift_down(u, prev, 2)
        cv = u2 * w[0:1] + u1 * w[1:2] + u * w[2:3]
        dcv = dz * b
        nxt = jnp.where(i > 0, carry[...], 0.0)
        du = _conv3_t(dcv, nxt, w)
        d_ref[0] = (dz * cv).astype(BF)
        d_ref[1] = (du * h).astype(BF)
        d_ref[2] = (du * c).astype(BF)
        carry[...] = dcv[0:HALO, :]
        part = _tap_rows(jnp.sum(dcv * u2, axis=0, keepdims=True), jnp.sum(dcv * u1, axis=0, keepdims=True),
                         jnp.sum(dcv * u, axis=0, keepdims=True))

        @pl.when(i == 0)
        def _():
            dcw_ref[...] = part

        @pl.when(i > 0)
        def _():
            dcw_ref[...] += part

    tile = lambda p: pl.BlockSpec((1, ts, tc), lambda c, i: (p, n_s - 1 - i, c))
    before = lambda p: pl.BlockSpec((1, HALO, tc), lambda c, i: (p, jnp.maximum((n_s - 1 - i) * per - 1, 0), c))
    dproj, dcw = pl.pallas_call(
        body, name=name, grid=(C // tc, n_s),
        in_specs=[tile(0), tile(1), tile(2),
                  pl.BlockSpec((ts, D), lambda c, i: (n_s - 1 - i, 0)),
                  pl.BlockSpec((tc, D), lambda c, i: (c, 0)),
                  before(1), before(2),
                  pl.BlockSpec((SUBLANES, tc), lambda c, i: (0, c))],
        out_specs=[pl.BlockSpec((3, ts, tc), lambda c, i: (0, n_s - 1 - i, c)),
                   pl.BlockSpec((SUBLANES, tc), lambda c, i: (0, c))],
        out_shape=[jax.ShapeDtypeStruct((3, S, C), BF), jax.ShapeDtypeStruct((SUBLANES, C), F32)],
        scratch_shapes=[pltpu.VMEM((HALO, tc), F32)],
        compiler_params=_params("parallel", "arbitrary"),
    )(proj, proj, proj, dy, w_out, proj, proj, _pad_conv(cw))
    return dproj, dcw[:CONV_W]


def _ffn_mid_bwd(up, dy, w_down, cw, name, exchange=()):
    _, S, C = up.shape
    D = dy.shape[1]
    ts, tc = _tile(S, 512, SUBLANES), _tile(C, 1408)
    n_s = S // ts
    n_c = C // tc
    per = ts // HALO
    n_x = len(exchange)

    def body(a_ref, g_ref, dy_ref, w_ref, ap_ref, cw_ref, *rest):
        d_ref, dcw_ref = rest[n_x:n_x + 2]
        carry = rest[2 * n_x + 2]
        i = pl.program_id(1)
        if n_x:
            copies = lambda: _chip_copies(rest[:n_x], rest[n_x + 2:2 * n_x + 2], *rest[2 * n_x + 3:])

            @pl.when((pl.program_id(0) == 0) & (i == 0))
            def _():
                for cp in copies():
                    cp.start()

        w = cw_ref[...]
        dz = lax.dot_general(dy_ref[...].astype(BF), w_ref[...], NT_DIMS, preferred_element_type=F32)
        a_pre, g = a_ref[0].astype(F32), g_ref[0].astype(F32)
        prev = jnp.where(i < n_s - 1, ap_ref[0].astype(F32), 0.0)
        a1, a2 = _shift_down(a_pre, prev, 1), _shift_down(a_pre, prev, 2)
        a = a2 * w[0:1] + a1 * w[1:2] + a_pre * w[2:3]
        sg = jax.nn.sigmoid(a)
        da = dz * g * (sg * (1.0 + a * (1.0 - sg)))
        nxt = jnp.where(i > 0, carry[...], 0.0)
        d_ref[0] = _conv3_t(da, nxt, w).astype(BF)
        d_ref[1] = (dz * (a * sg)).astype(BF)
        carry[...] = da[0:HALO, :]
        part = _tap_rows(jnp.sum(da * a2, axis=0, keepdims=True), jnp.sum(da * a1, axis=0, keepdims=True),
                         jnp.sum(da * a_pre, axis=0, keepdims=True))

        @pl.when(i == 0)
        def _():
            dcw_ref[...] = part

        @pl.when(i > 0)
        def _():
            dcw_ref[...] += part

        if n_x:
            @pl.when((pl.program_id(0) == n_c - 1) & (i == n_s - 1))
            def _():
                for cp in copies():
                    cp.wait()

    tile = lambda p: pl.BlockSpec((1, ts, tc), lambda c, i: (p, n_s - 1 - i, c))
    sems = [pltpu.SemaphoreType.DMA((3 * n_x,)), pltpu.SemaphoreType.DMA((3 * n_x,))] if n_x else []
    outs = pl.pallas_call(
        body, name=name, grid=(n_c, n_s),
        in_specs=[tile(0), tile(1),
                  pl.BlockSpec((ts, D), lambda c, i: (n_s - 1 - i, 0)),
                  pl.BlockSpec((tc, D), lambda c, i: (c, 0)),
                  pl.BlockSpec((1, HALO, tc), lambda c, i: (0, jnp.maximum((n_s - 1 - i) * per - 1, 0), c)),
                  pl.BlockSpec((SUBLANES, tc), lambda c, i: (0, c))] + [ANY] * n_x,
        out_specs=[pl.BlockSpec((2, ts, tc), lambda c, i: (0, n_s - 1 - i, c)),
                   pl.BlockSpec((SUBLANES, tc), lambda c, i: (0, c))] + [ANY] * n_x,
        out_shape=[jax.ShapeDtypeStruct((2, S, C), BF), jax.ShapeDtypeStruct((SUBLANES, C), F32)]
        + [jax.ShapeDtypeStruct((3,) + a.shape[1:], a.dtype) for a in exchange],
        scratch_shapes=[pltpu.VMEM((HALO, tc), F32)] + sems,
        compiler_params=_params("arbitrary" if n_x else "parallel", "arbitrary"),
    )(up, up, dy, w_down, up, _pad_conv(cw), *exchange)
    return outs[0], outs[1][:CONV_W], list(outs[2:])


def _matmul_residual(z, w, x, name):
    S, K = z.shape
    D = w.shape[1]
    ts = _tile(S, 512, SUBLANES)

    def body(z_ref, w_ref, x_ref, o_ref):
        o_ref[...] = x_ref[...] + jnp.dot(z_ref[...], w_ref[...], preferred_element_type=F32)

    return pl.pallas_call(
        body, name=name, grid=(S // ts,),
        in_specs=[pl.BlockSpec((ts, K), lambda s: (s, 0)), pl.BlockSpec((K, D), lambda s: (0, 0)),
                  pl.BlockSpec((ts, D), lambda s: (s, 0))],
        out_specs=pl.BlockSpec((ts, D), lambda s: (s, 0)),
        out_shape=jax.ShapeDtypeStruct((S, D), F32),
        compiler_params=_params("parallel"),
    )(z, w, x)


def _wgrad(a, b, name):
    S, M = a.shape
    P, _, C = b.shape
    tm, tn, tk = _tile(M, 1408), _tile(C, 1408), _tile(S, 1024, SUBLANES)
    nnc = C // tn

    def body(a_ref, b_ref, o_ref):
        @pl.when(pl.program_id(2) == 0)
        def _():
            o_ref[...] = jnp.zeros_like(o_ref)

        o_ref[...] += lax.dot_general(a_ref[...], b_ref[0].astype(BF), TN_DIMS, preferred_element_type=F32)

    return pl.pallas_call(
        body, name=name, grid=(M // tm, P * nnc, S // tk),
        in_specs=[pl.BlockSpec((tk, tm), lambda m, n, k: (k, m)),
                  pl.BlockSpec((1, tk, tn), lambda m, n, k: (n // nnc, k, n % nnc))],
        out_specs=pl.BlockSpec((tm, tn), lambda m, n, k: (m, n)),
        out_shape=jax.ShapeDtypeStruct((M, P * C), F32),
        compiler_params=_params("parallel", "parallel", "arbitrary"),
    )(a, b)


def _dnorm(dp, w, x, g, dy, name):
    P, S, C = dp.shape
    D = x.shape[1]
    ts = _tile(S, 512, SUBLANES)

    def body(dp_ref, w_ref, x_ref, g_ref, dy_ref, dx_ref, dg_ref):
        @pl.when(pl.program_id(0) == 0)
        def _():
            dg_ref[...] = jnp.zeros_like(dg_ref)

        dxn = lax.dot_general(dp_ref[0], w_ref[:, 0:C], NT_DIMS, preferred_element_type=F32)
        for p in range(1, P):
            dxn = dxn + lax.dot_general(dp_ref[p], w_ref[:, p * C:(p + 1) * C], NT_DIMS, preferred_element_type=F32)
        xf = x_ref[...]
        r = lax.rsqrt(jnp.mean(xf * xf, axis=-1, keepdims=True) + RMS_EPS)
        xhat = xf * r
        dxhat = dxn * g_ref[...]
        dx_ref[...] = dy_ref[...] + r * (dxhat - xhat * jnp.mean(dxhat * xhat, axis=-1, keepdims=True))
        dg_ref[...] += jnp.broadcast_to(jnp.sum(dxn * xhat, axis=0, keepdims=True), dg_ref.shape)

    dx, dg = pl.pallas_call(
        body, name=name, grid=(S // ts,),
        in_specs=[pl.BlockSpec((P, ts, C), lambda s: (0, s, 0)),
                  pl.BlockSpec((D, P * C), lambda s: (0, 0), pipeline_mode=pl.Buffered(1)),
                  pl.BlockSpec((ts, D), lambda s: (s, 0)),
                  pl.BlockSpec((1, D), lambda s: (0, 0)),
                  pl.BlockSpec((ts, D), lambda s: (s, 0))],
        out_specs=[pl.BlockSpec((ts, D), lambda s: (s, 0)),
                   pl.BlockSpec((SUBLANES, D), lambda s: (0, 0))],
        out_shape=[jax.ShapeDtypeStruct((S, D), F32), jax.ShapeDtypeStruct((SUBLANES, D), F32)],
        compiler_params=_params("arbitrary"),
    )(dp, w, x, g.reshape(1, D), dy)
    return dx, dg[0]


def _head_masks(shape, hd):
    lane = lax.broadcasted_iota(jnp.int32, shape, 1)
    return lane < hd


def _pair_sum(v, lo):
    s0 = jnp.sum(jnp.where(lo, v, 0.0), axis=-1, keepdims=True)
    s1 = jnp.sum(jnp.where(lo, 0.0, v), axis=-1, keepdims=True)
    return jnp.where(lo, s0, s1)


def _headnorm_bwd(src, part, colblk, w, dys, D, name):
    S = src.shape[1]
    hd = w.shape[0]
    ts = _tile(S, 512, SUBLANES)
    w2 = jnp.tile(w, LANES // hd).reshape(1, LANES)
    n_dy = len(dys)

    def body(x_ref, w_ref, *rest):
        dy_refs, dx_ref, dw_ref = rest[:n_dy], rest[n_dy], rest[n_dy + 1]

        @pl.when(pl.program_id(0) == 0)
        def _():
            dw_ref[...] = jnp.zeros_like(dw_ref)

        lo = _head_masks((ts, LANES), hd)
        for t in range(D // LANES):
            cols = slice(t * LANES, (t + 1) * LANES)
            xt = x_ref[0, :, cols]
            dy = dy_refs[0][:, cols]
            for other in dy_refs[1:]:
                dy = dy + other[:, cols]
            r = lax.rsqrt(_pair_sum(xt * xt, lo) * (1.0 / hd) + RMS_EPS)
            xhat = xt * r
            dxhat = dy * w_ref[...]
            mean = _pair_sum(dxhat * xhat, lo) * (1.0 / hd)
            dx_ref[:, cols] = (r * (dxhat - xhat * mean)).astype(BF)
            dw_ref[:, cols] += jnp.broadcast_to(jnp.sum(dy * xhat, axis=0, keepdims=True), (SUBLANES, LANES))

    dx, dw = pl.pallas_call(
        body, name=name, grid=(S // ts,),
        in_specs=[pl.BlockSpec((1, ts, D), lambda s: (part, s, colblk)), pl.BlockSpec((1, LANES), lambda s: (0, 0))]
        + [pl.BlockSpec((ts, D), lambda s: (s, 0))] * n_dy,
        out_specs=[pl.BlockSpec((ts, D), lambda s: (s, 0)), pl.BlockSpec((SUBLANES, D), lambda s: (0, 0))],
        out_shape=[jax.ShapeDtypeStruct((S, D), BF), jax.ShapeDtypeStruct((SUBLANES, D), F32)],
        compiler_params=_params("arbitrary"),
    )(src, w2, *dys)
    return dx, jnp.sum(dw[0].reshape(D // hd, hd), axis=0)


def _tri(n, lower):
    r, c = lax.broadcasted_iota(jnp.int32, (n, n), 0), lax.broadcasted_iota(jnp.int32, (n, n), 1)
    return jnp.where((c <= r) if lower else (c >= r), 1.0, 0.0).astype(BF)


def _dot_exact(t, v):
    hi = v.astype(BF)
    r1 = v - hi.astype(F32)
    mid = r1.astype(BF)
    lo = (r1 - mid.astype(F32)).astype(BF)
    dot = lambda u: jnp.dot(t, u, preferred_element_type=F32)
    return dot(hi) + dot(mid) + dot(lo)


def _gate_fwd(kvf, b_pad, colblk, name):
    S = kvf.shape[1]
    ts = _tile(S, 512, SUBLANES)

    def body(f_ref, b_ref, c_ref, carry):
        @pl.when(pl.program_id(0) == 0)
        def _():
            carry[...] = jnp.zeros_like(carry)

        f = f_ref[0] + b_ref[...]
        ls = jnp.minimum(f, 0.0) - jnp.log1p(jnp.exp(-jnp.abs(f)))
        tri = _tri(ts, lower=True)
        c = _dot_exact(tri, ls) + carry[0:1, :]
        c_ref[...] = c
        carry[...] = jnp.broadcast_to(c[ts - 1:ts, :], carry.shape)

    return pl.pallas_call(
        body, name=name, grid=(S // ts,),
        in_specs=[pl.BlockSpec((1, ts, LANES), lambda s: (0, s, colblk)), pl.BlockSpec((1, LANES), lambda s: (0, 0))],
        out_specs=pl.BlockSpec((ts, LANES), lambda s: (s, 0)),
        out_shape=jax.ShapeDtypeStruct((S, LANES), F32),
        scratch_shapes=[pltpu.VMEM((SUBLANES, LANES), F32)],
        compiler_params=_params("arbitrary"),
    )(kvf, b_pad)


def _gate_bwd(dc, kvf, b_pad, colblk, name):
    S = kvf.shape[1]
    ts = _tile(S, 512, SUBLANES)
    n_s = S // ts

    def body(dc_ref, f_ref, b_ref, df_ref, db_ref, carry):
        @pl.when(pl.program_id(0) == 0)
        def _():
            carry[...] = jnp.zeros_like(carry)
            db_ref[...] = jnp.zeros_like(db_ref)

        tri = _tri(ts, lower=False)
        dls = _dot_exact(tri, dc_ref[...]) + carry[0:1, :]
        f = f_ref[0] + b_ref[...]
        df = dls * jax.nn.sigmoid(-f)
        df_ref[...] = df
        db_ref[...] += jnp.broadcast_to(jnp.sum(df, axis=0, keepdims=True), db_ref.shape)
        carry[...] = jnp.broadcast_to(dls[0:1, :], carry.shape)

    df, db = pl.pallas_call(
        body, name=name, grid=(n_s,),
        in_specs=[pl.BlockSpec((ts, LANES), lambda s: (n_s - 1 - s, 0)),
                  pl.BlockSpec((1, ts, LANES), lambda s: (0, n_s - 1 - s, colblk)),
                  pl.BlockSpec((1, LANES), lambda s: (0, 0))],
        out_specs=[pl.BlockSpec((ts, LANES), lambda s: (n_s - 1 - s, 0)),
                   pl.BlockSpec((SUBLANES, LANES), lambda s: (0, 0))],
        out_shape=[jax.ShapeDtypeStruct((S, LANES), F32), jax.ShapeDtypeStruct((SUBLANES, LANES), F32)],
        scratch_shapes=[pltpu.VMEM((SUBLANES, LANES), F32)],
        compiler_params=_params("arbitrary"),
    )(dc, kvf, b_pad)
    return df, db[0]


def _attn_tile(S):
    return _tile(S, 512, LANES)


def _split_heads(v, lo):
    zero = jnp.zeros_like(v)
    return jnp.where(lo, v, zero), jnp.where(lo, zero, v)


def _augment(src, part, colblk, c, mode, hd, D, name, norm_w=None, scale=1.0):
    S = src.shape[1]
    ts = _tile(S, 512, 2 * SUBLANES)
    w2 = jnp.tile(jnp.ones((hd,), F32) if norm_w is None else norm_w, LANES // hd).reshape(1, LANES)

    def body(b_ref, w_ref, c_ref, o0_ref, o1_ref):
        lane = lax.broadcasted_iota(jnp.int32, (ts, LANES), 1)
        lo = lane < hd
        cc = c_ref[...] * LOG2E
        for t in range(D // LANES):
            cols = slice(t * LANES, (t + 1) * LANES)
            bt = b_ref[0, :, cols]
            if norm_w is not None:
                r = lax.rsqrt(_pair_sum(bt * bt, lo) * (1.0 / hd) + RMS_EPS)
                bt = bt * r * w_ref[...] * scale
            bt = bt.astype(BF)
            for h, o_ref in ((0, o0_ref), (1, o1_ref)):
                first = hd if h == 0 else 0
                keep = (lane < hd) if h == 0 else (lane >= hd)
                if mode == "v":
                    vals = (1.0,)
                else:
                    col = cc[:, 2 * t + h:2 * t + h + 1]
                    hi = col.astype(BF).astype(F32)
                    mid = (col - hi).astype(BF).astype(F32)
                    pieces = (hi, mid, col - hi - mid)
                    vals = pieces + (1.0, 1.0, 1.0) if mode == "q" else (1.0, 1.0, 1.0) + tuple(-v for v in pieces)
                aug = jnp.zeros((ts, LANES), F32)
                for i, v in enumerate(vals):
                    aug = jnp.where(lane == first + i, v, aug)
                o_ref[:, cols] = jnp.where(keep, bt, aug.astype(BF))

    spec = pl.BlockSpec((ts, D), lambda s: (s, 0))
    return pl.pallas_call(
        body, name=name, grid=(S // ts,),
        in_specs=[pl.BlockSpec((1, ts, D), lambda s: (part, s, colblk)), pl.BlockSpec((1, LANES), lambda s: (0, 0)),
                  pl.BlockSpec((ts, LANES), lambda s: (s, 0))],
        out_specs=[spec, spec],
        out_shape=[jax.ShapeDtypeStruct((S, D), BF)] * 2,
        compiler_params=_params("parallel"),
    )(src, w2, c)


def _attn_fwd(qa, ka, va, qg, hd, name, gather=()):
    S, D = qa[0].shape
    P = D // LANES
    tq = _attn_tile(S)
    nq = S // tq
    n_g = len(gather)

    def body(q0_ref, q1_ref, k0_ref, k1_ref, v0_ref, v1_ref, g_ref, *rest):
        o_ref, og_ref, m_ref, l_ref = rest[n_g:n_g + 4]
        s_buf = rest[2 * n_g + 4]
        pair, qi = pl.program_id(0), pl.program_id(1)
        if n_g:
            start, pass_on, finish = _gather_phases([a.shape[0] for a in gather], rest[:n_g],
                                                    rest[n_g + 4:2 * n_g + 4], *rest[2 * n_g + 5:])
            pl.when((pair == 0) & (qi == 0))(start)
            pl.when((pair == P // 2) & (qi == 0))(pass_on)
        lo = _head_masks((tq, LANES), hd)
        qh = (q0_ref[...], q1_ref[...])
        k_refs, v_refs = (k0_ref, k1_ref), (v0_ref, v1_ref)
        causal = lax.broadcasted_iota(jnp.int32, (tq, tq), 1) <= lax.broadcasted_iota(jnp.int32, (tq, tq), 0)

        def scores(ki, slot):
            off = pl.multiple_of(ki * tq, tq)
            for h in range(2):
                s_buf[slot, h] = lax.dot_general(qh[h], k_refs[h][pl.ds(off, tq), :], NT_DIMS,
                                                 preferred_element_type=F32)

        def consume(ki, slot, carry, masked):
            off = pl.multiple_of(ki * tq, tq)
            out = []
            for h in range(2):
                m, acc = carry[h]
                s = s_buf[slot, h]
                if masked:
                    s = jnp.where(causal, s, -jnp.inf)
                m_new = jnp.maximum(m, jnp.ceil(jnp.max(s, axis=-1, keepdims=True)))
                p = jnp.exp2(s - m_new)
                acc = jnp.exp2(m - m_new) * acc + jnp.dot(p.astype(BF), v_refs[h][pl.ds(off, tq), :],
                                                          preferred_element_type=F32)
                out.append((m_new, acc))
            return tuple(out)

        def step(j, carry):
            scores(2 * j + 1, 1)
            carry = consume(2 * j, 0, carry, False)
            scores(2 * j + 2, 0)
            return consume(2 * j + 1, 1, carry, False)

        def finish_even(carry):
            return consume(qi, 0, carry, True)

        def finish_odd(carry):
            scores(qi, 1)
            return consume(qi, 1, consume(qi - 1, 0, carry, False), True)

        init = tuple((jnp.full((tq, 1), -jnp.inf, F32), jnp.zeros((tq, LANES), F32)) for _ in range(2))
        scores(0, 0)
        carry = lax.fori_loop(0, qi // 2, step, init)
        (m0, a0), (m1, a1) = lax.cond(qi % 2 == 0, finish_even, finish_odd, carry)
        l0, l1 = a0[:, hd:hd + 1], a1[:, 0:1]
        o = jnp.where(lo, a0 / l0, a1 / l1)
        o_ref[...] = o
        og_ref[...] = (o * jax.nn.sigmoid(g_ref[0])).astype(BF)
        lane2 = lax.broadcasted_iota(jnp.int32, (tq, 2), 1)
        m_ref[0] = jnp.where(lane2 == 0, m0, m1)
        l_ref[0] = jnp.where(lane2 == 0, l0, l1)
        if n_g:
            pl.when((pair == P - 1) & (qi == nq - 1))(finish)

    tile = pl.BlockSpec((tq, LANES), lambda p, i: (i, p))
    whole = pl.BlockSpec((S, LANES), lambda p, i: (0, p))
    stat = pl.BlockSpec((1, tq, 2), lambda p, i: (p, i, 0))
    sems = [pltpu.SemaphoreType.DMA((6 * n_g,)), pltpu.SemaphoreType.DMA((6 * n_g,))] if n_g else []
    outs = pl.pallas_call(
        body, name=name, grid=(P, nq),
        in_specs=[tile, tile, whole, whole, whole, whole, pl.BlockSpec((1, tq, LANES), lambda p, i: (1, i, p))]
        + [ANY] * n_g,
        out_specs=[tile, tile, stat, stat] + [ANY] * n_g,
        out_shape=[jax.ShapeDtypeStruct((S, D), F32), jax.ShapeDtypeStruct((S, D), BF),
                   jax.ShapeDtypeStruct((P, S, 2), F32), jax.ShapeDtypeStruct((P, S, 2), F32)]
        + [jax.ShapeDtypeStruct((N_CHIPS,) + a.shape, a.dtype) for a in gather],
        scratch_shapes=[pltpu.VMEM((2, 2, tq, tq), F32)] + sems,
        compiler_params=_params("arbitrary" if n_g else "parallel", "arbitrary"),
    )(*qa, *ka, *va, qg, *gather)
    return tuple(outs[:4]) + (_fill_own(outs[4:], gather),)


def _attn_out_bwd(dy, w_out, o, qg, l, hd, name):
    S, D = o.shape
    P = D // LANES
    ts = _tile(S, 512, 2 * SUBLANES)

    def body(dy_ref, w_ref, o_ref, g_ref, l_ref, do_ref, dg_ref, e_ref):
        lo = _head_masks((ts, LANES), hd)
        lane2 = lax.broadcasted_iota(jnp.int32, (ts, 2), 1)
        dog = lax.dot_general(dy_ref[...].astype(BF), w_ref[...], NT_DIMS, preferred_element_type=F32)
        for t in range(P):
            cols = slice(t * LANES, (t + 1) * LANES)
            sg = jax.nn.sigmoid(g_ref[0, :, cols])
            dog_t, o_t, l_t = dog[:, cols], o_ref[:, cols], l_ref[t]
            g = (dog_t * sg / jnp.where(lo, l_t[:, 0:1], l_t[:, 1:2])).astype(BF)
            do_ref[:, cols] = g
            dg_ref[:, cols] = (dog_t * o_t * sg * (1.0 - sg)).astype(BF)
            prod = g.astype(F32) * o_t
            e0 = jnp.sum(jnp.where(lo, prod, 0.0), axis=-1, keepdims=True)
            e1 = jnp.sum(jnp.where(lo, 0.0, prod), axis=-1, keepdims=True)
            e_ref[t] = jnp.where(lane2 == 0, e0, e1)

    rows = pl.BlockSpec((ts, D), lambda s: (s, 0))
    stat = pl.BlockSpec((P, ts, 2), lambda s: (0, s, 0))
    return pl.pallas_call(
        body, name=name, grid=(S // ts,),
        in_specs=[rows, pl.BlockSpec(w_out.shape, lambda s: (0, 0)), rows,
                  pl.BlockSpec((1, ts, D), lambda s: (1, s, 0)), stat],
        out_specs=[rows, rows, stat],
        out_shape=[jax.ShapeDtypeStruct((S, D), BF), jax.ShapeDtypeStruct((S, D), BF),
                   jax.ShapeDtypeStruct((P, S, 2), F32)],
        compiler_params=_params("parallel"),
    )(dy, w_out, o, qg, l)


def _attn_bwd(qa, ka, vb, g, m_row, e_row, hd, name, exchange=()):
    S, D = vb.shape
    P = D // LANES
    tk = _attn_tile(S)
    nk = S // tk
    scale = hd ** -0.5
    n_x = len(exchange)

    def body(q0_ref, q1_ref, g_ref, k0_ref, k1_ref, v_ref, m_ref, e_ref, *rest):
        dq_ref, dk_ref, dv_ref, dc_ref = rest[n_x:n_x + 4]
        st_buf, dp_buf = rest[2 * n_x + 4:2 * n_x + 6]
        pair, ki = pl.program_id(0), pl.program_id(1)
        if n_x:
            copies = lambda: _chip_copies(rest[:n_x], rest[n_x + 4:2 * n_x + 4], *rest[2 * n_x + 6:])

            @pl.when((pair == 0) & (ki == 0))
            def _():
                for cp in copies():
                    cp.start()

        @pl.when(ki == 0)
        def _():
            dq_ref[...] = jnp.zeros_like(dq_ref)

        lo = _head_masks((tk, LANES), hd)
        kh = (k0_ref[...], k1_ref[...])
        q_refs = (q0_ref, q1_ref)
        vh = _split_heads(v_ref[...], lo)
        causal_t = lax.broadcasted_iota(jnp.int32, (tk, tk), 0) <= lax.broadcasted_iota(jnp.int32, (tk, tk), 1)

        def stage(qi, slot):
            off = pl.multiple_of(qi * tk, tk)
            gb = g_ref[pl.ds(off, tk), :]
            for h in range(2):
                st_buf[slot, h] = lax.dot_general(kh[h], q_refs[h][pl.ds(off, tk), :], NT_DIMS,
                                                  preferred_element_type=F32)
                dp_buf[slot, h] = lax.dot_general(vh[h], gb, NT_DIMS, preferred_element_type=F32)

        def consume(qi, slot, carry, masked):
            off = pl.multiple_of(qi * tk, tk)
            gb = g_ref[pl.ds(off, tk), :]
            m_t, e_t = m_ref[0, qi], e_ref[0, qi]
            out, dq_parts = [], []
            for h in range(2):
                dk, dv, dc = carry[h]
                qb = q_refs[h][pl.ds(off, tk), :]
                pt = jnp.exp2(st_buf[slot, h] - m_t[h:h + 1, :])
                if masked:
                    pt = jnp.where(causal_t, pt, 0.0)
                pb = pt.astype(BF)
                dv = dv + jnp.dot(pb, gb, preferred_element_type=F32)
                dst = pb.astype(F32) * (dp_buf[slot, h] - e_t[h:h + 1, :])
                db = dst.astype(BF)
                dk = dk + jnp.dot(db, qb, preferred_element_type=F32)
                dc = dc - jnp.sum(dst, axis=-1, keepdims=True)
                dq_parts.append(lax.dot_general(db, kh[h], TN_DIMS, preferred_element_type=F32))
                out.append((dk, dv, dc))
            dq_ref[pl.ds(off, tk), :] += jnp.where(lo, dq_parts[0], dq_parts[1]) * scale
            return tuple(out)

        n_after = nk - 1 - ki

        def step(j, carry):
            b = ki + 1 + 2 * j
            stage(b + 1, 0)
            carry = consume(b, 1, carry, False)
            stage(b + 2, 1)
            return consume(b + 1, 0, carry, False)

        def rest_one(carry):
            return consume(nk - 1, 1, carry, False)

        def rest_two(carry):
            stage(nk - 1, 0)
            return consume(nk - 1, 0, consume(nk - 2, 1, carry, False), False)

        init = tuple((jnp.zeros((tk, LANES), F32), jnp.zeros((tk, LANES), F32), jnp.zeros((tk, 1), F32))
                     for _ in range(2))
        stage(ki, 0)
        stage(jnp.minimum(ki + 1, nk - 1), 1)
        carry = consume(ki, 0, init, True)
        carry = lax.fori_loop(0, (n_after - 1) // 2, step, carry)
        which = jnp.where(n_after == 0, 0, 2 - n_after % 2)
        (dk0, dv0, dc0), (dk1, dv1, dc1) = lax.switch(which, [lambda c: c, rest_one, rest_two], carry)
        dk_ref[...] = jnp.where(lo, dk0, dk1) * (1.0 / LOG2E)
        dv_ref[...] = jnp.where(lo, dv0, dv1)
        lane2 = lax.broadcasted_iota(jnp.int32, (tk, 2), 1)
        dc_ref[0] = jnp.where(lane2 == 0, dc0, dc1)
        if n_x:
            @pl.when((pair == P - 1) & (ki == nk - 1))
            def _():
                for cp in copies():
                    cp.wait()

    tile = pl.BlockSpec((tk, LANES), lambda p, i: (i, p))
    whole = pl.BlockSpec((S, LANES), lambda p, i: (0, p))
    row_spec = pl.BlockSpec((1, nk, 2, tk), lambda p, i: (p, 0, 0, 0))
    sems = [pltpu.SemaphoreType.DMA((3 * n_x,)), pltpu.SemaphoreType.DMA((3 * n_x,))] if n_x else []
    outs = pl.pallas_call(
        body, name=name, grid=(P, nk),
        in_specs=[whole, whole, whole, tile, tile, tile, row_spec, row_spec] + [ANY] * n_x,
        out_specs=[whole, tile, tile, pl.BlockSpec((1, tk, 2), lambda p, i: (p, i, 0))] + [ANY] * n_x,
        out_shape=[jax.ShapeDtypeStruct((S, D), F32), jax.ShapeDtypeStruct((S, D), F32),
                   jax.ShapeDtypeStruct((S, D), F32), jax.ShapeDtypeStruct((P, S, 2), F32)]
        + [jax.ShapeDtypeStruct((3,) + a.shape[1:], a.dtype) for a in exchange],
        scratch_shapes=[pltpu.VMEM((2, 2, tk, tk), F32), pltpu.VMEM((2, 2, tk, tk), F32)] + sems,
        compiler_params=_params("arbitrary" if n_x else "parallel", "arbitrary"),
    )(*qa, g, *ka, vb, m_row, e_row, *exchange)
    return tuple(outs[:4]) + (list(outs[4:]),)


def _loss_head(y, t, name):
    S, D = y.shape
    ts = _tile(S, 512, SUBLANES)

    def body(y_ref, t_ref, dy_ref, l_ref):
        @pl.when(pl.program_id(0) == 0)
        def _():
            l_ref[...] = jnp.zeros_like(l_ref)

        e = y_ref[...] - t_ref[...]
        dy_ref[...] = e * (1.0 / D)
        part = 0.5 * jnp.sum(jnp.mean(e * e, axis=-1, keepdims=True), axis=0, keepdims=True)
        l_ref[...] += jnp.broadcast_to(part, l_ref.shape)

    return pl.pallas_call(
        body, name=name, grid=(S // ts,),
        in_specs=[pl.BlockSpec((ts, D), lambda s: (s, 0)), pl.BlockSpec((ts, D), lambda s: (s, 0))],
        out_specs=[pl.BlockSpec((ts, D), lambda s: (s, 0)), pl.BlockSpec((SUBLANES, LANES), lambda s: (0, 0))],
        out_shape=[jax.ShapeDtypeStruct((S, D), F32), jax.ShapeDtypeStruct((SUBLANES, LANES), F32)],
        compiler_params=_params("arbitrary"),
    )(y, t)


def _adamw(w, g, m, v, name):
    shape = w.shape
    cols = shape[-1]
    as2d = lambda a: a.reshape(-1, cols)
    rows = as2d(w).shape[0]
    tr = _tile(rows, 256, SUBLANES) if rows % SUBLANES == 0 else rows
    c1 = 1.0 - ADAM_B1 ** ADAM_STEP
    c2 = 1.0 - ADAM_B2 ** ADAM_STEP

    def body(w_ref, g_ref, m_ref, v_ref, d_ref, nm_ref, nv_ref):
        gg = g_ref[...]
        nm = ADAM_B1 * m_ref[...] + (1.0 - ADAM_B1) * gg
        nv = ADAM_B2 * v_ref[...] + (1.0 - ADAM_B2) * (gg * gg)
        d_ref[...] = -ADAM_LR * ((nm / c1) / (jnp.sqrt(nv / c2) + ADAM_EPS) + ADAM_WD * w_ref[...])
        nm_ref[...] = nm
        nv_ref[...] = nv

    spec = pl.BlockSpec((tr, cols), lambda r: (r, 0))
    outs = pl.pallas_call(
        body, name=name, grid=(rows // tr,), in_specs=[spec] * 4, out_specs=[spec] * 3,
        out_shape=[jax.ShapeDtypeStruct((rows, cols), F32)] * 3,
        compiler_params=_params("parallel"),
    )(as2d(w), as2d(g), as2d(m), as2d(v))
    return tuple(o.reshape(shape) for o in outs)


def _place():
    return lax.axis_index("x"), lax.axis_index("y"), lax.axis_index("c")


def _other_chips(x, y):
    return [(1 - x, y), (x, 1 - y), (1 - x, 1 - y)]


def _remote(src, dst, send_sems, recv_sems, k, to):
    return pltpu.make_async_remote_copy(src_ref=src, dst_ref=dst, send_sem=send_sems.at[k], recv_sem=recv_sems.at[k],
                                        device_id=to, device_id_type=MESH)


def _half(c, rh):
    return pl.ds(pl.multiple_of(c * rh, 2 * SUBLANES), rh)


def _gather_phases(rows, src, dst, send_sems, recv_sems):
    n = len(rows)
    x, y, c = _place()
    me = 2 * x + y
    sib = (x, y, 1 - c)
    chips = _other_chips(x, y)
    rh = [r // 2 for r in rows]

    def first():
        return [_remote(src[g].at[_half(c, rh[g])], dst[g].at[me, _half(c, rh[g])], send_sems, recv_sems,
                        6 * g + j, (cx, cy, c)) for j, (cx, cy) in enumerate(chips) for g in range(n)]

    def landed(j, g, core):
        cx, cy = chips[j]
        return dst[g].at[2 * cx + cy, _half(core, rh[g])]

    def passed():
        return [_remote(landed(j, g, c), landed(j, g, c), send_sems, recv_sems, 6 * g + 3 + j, sib)
                for j in range(3) for g in range(n)]

    def start():
        for cp in first():
            cp.start()

    def pass_on():
        cps = passed()
        for j in range(3):
            for g in range(n):
                _remote(landed(j, g, c), landed(j, g, c), send_sems, recv_sems, 6 * g + j, sib).wait_recv()
                cps[j * n + g].start()

    def finish():
        for j in range(3):
            for g in range(n):
                _remote(landed(j, g, 1 - c), landed(j, g, 1 - c), send_sems, recv_sems, 6 * g + 3 + j, sib).wait_recv()
        for cp in first() + passed():
            cp.wait_send()

    return start, pass_on, finish


def _fill_own(outs, srcs):
    x, y, _ = _place()
    return [lax.dynamic_update_slice_in_dim(o, a[None], 2 * x + y, axis=0) for o, a in zip(outs, srcs)]


def _allgather_weights(srcs):
    n = len(srcs)

    def body(*refs):
        for step in _gather_phases([a.shape[0] for a in srcs], refs[:n], refs[n:2 * n], *refs[2 * n:]):
            step()

    outs = pl.pallas_call(
        body, name="allgather_weights", in_specs=[ANY] * n, out_specs=[ANY] * n,
        out_shape=[jax.ShapeDtypeStruct((N_CHIPS,) + a.shape, a.dtype) for a in srcs],
        scratch_shapes=[pltpu.SemaphoreType.DMA((6 * n,)), pltpu.SemaphoreType.DMA((6 * n,))],
    )(*srcs)
    return _fill_own(outs, srcs)


def _pair_exchange(gs, name):
    n = len(gs)

    def body(*refs):
        g_refs, t_refs, (send_sems, recv_sems) = refs[:n], refs[n:2 * n], refs[2 * n:]
        x, y, c = _place()
        cps = [_remote(g_refs[g].at[k, 1 - c], t_refs[g].at[k], send_sems, recv_sems, N_CHIPS * g + k, (x, y, 1 - c))
               for g in range(n) for k in range(N_CHIPS)]
        for cp in cps:
            cp.start()
        for cp in cps:
            cp.wait()

    return pl.pallas_call(
        body, name=name, in_specs=[ANY] * n, out_specs=[ANY] * n,
        out_shape=[jax.ShapeDtypeStruct((a.shape[0],) + a.shape[2:], a.dtype) for a in gs],
        scratch_shapes=[pltpu.SemaphoreType.DMA((N_CHIPS * n,)), pltpu.SemaphoreType.DMA((N_CHIPS * n,))],
    )(*gs)


def _pair_add(g, t, c, name):
    n, _, rh, W = g.shape
    tr = _tile(rh, 256, 2 * SUBLANES)

    def body(c_ref, g_ref, t_ref, o_ref):
        o_ref[...] = (g_ref[0] + t_ref[...]).astype(BF)

    return pl.pallas_call(
        body, name=name,
        grid_spec=pltpu.PrefetchScalarGridSpec(
            num_scalar_prefetch=1, grid=(n, rh // tr),
            in_specs=[pl.BlockSpec((1, 1, tr, W), lambda k, i, c_ref: (k, c_ref[0], i, 0)),
                      pl.BlockSpec((1, tr, W), lambda k, i, c_ref: (k, i, 0))],
            out_specs=pl.BlockSpec((1, tr, W), lambda k, i, c_ref: (k, i, 0))),
        out_shape=jax.ShapeDtypeStruct((n, rh, W), BF),
        compiler_params=_params("parallel", "parallel"),
    )(c.reshape(1).astype(jnp.int32), g, t)


def _chip_copies(a_refs, t_refs, send_sems, recv_sems):
    x, y, c = _place()
    return [_remote(a_refs[g].at[2 * cx + cy], t_refs[g].at[j], send_sems, recv_sems, 3 * g + j, (cx, cy, c))
            for j, (cx, cy) in enumerate(_other_chips(x, y)) for g in range(len(a_refs))]


def _chip_exchange(parts, name):
    n = len(parts)

    def body(*refs):
        cps = _chip_copies(refs[:n], refs[n:2 * n], *refs[2 * n:])
        for cp in cps:
            cp.start()
        for cp in cps:
            cp.wait()

    return pl.pallas_call(
        body, name=name, in_specs=[ANY] * n, out_specs=[ANY] * n,
        out_shape=[jax.ShapeDtypeStruct((3,) + a.shape[1:], a.dtype) for a in parts],
        scratch_shapes=[pltpu.SemaphoreType.DMA((3 * n,)), pltpu.SemaphoreType.DMA((3 * n,))],
    )(*parts)


def _chip_add(g, t1, t2, c, me, name):
    _, _, rh, W = g.shape
    tr = _tile(rh, 256, 2 * SUBLANES)

    def body(c_ref, me_ref, g_ref, t1_ref, t2_ref, o_ref):
        own = g_ref[0, 0] + t1_ref[0]
        o_ref[...] = own + t2_ref[0].astype(F32) + t2_ref[1].astype(F32) + t2_ref[2].astype(F32)

    return pl.pallas_call(
        body, name=name,
        grid_spec=pltpu.PrefetchScalarGridSpec(
            num_scalar_prefetch=2, grid=(rh // tr,),
            in_specs=[pl.BlockSpec((1, 1, tr, W), lambda i, c_ref, me_ref: (me_ref[0], c_ref[0], i, 0)),
                      pl.BlockSpec((1, tr, W), lambda i, c_ref, me_ref: (me_ref[0], i, 0)),
                      pl.BlockSpec((3, tr, W), lambda i, c_ref, me_ref: (0, i, 0))],
            out_specs=pl.BlockSpec((tr, W), lambda i, c_ref, me_ref: (i, 0))),
        out_shape=jax.ShapeDtypeStruct((rh, W), F32),
        compiler_params=_params("parallel"),
    )(c.reshape(1).astype(jnp.int32), me.reshape(1).astype(jnp.int32), g, t1, t2)


def _pair_share(hs, name):
    n = len(hs)

    def body(*refs):
        h_refs, f_refs, (send_sems, recv_sems) = refs[:n], refs[n:2 * n], refs[2 * n:]
        x, y, c = _place()
        cps = [_remote(h_refs[g], f_refs[g], send_sems, recv_sems, g, (x, y, 1 - c)) for g in range(n)]
        for cp in cps:
            cp.start()
        for cp in cps:
            cp.wait()

    return pl.pallas_call(
        body, name=name, in_specs=[ANY] * n, out_specs=[ANY] * n,
        out_shape=[jax.ShapeDtypeStruct(a.shape, a.dtype) for a in hs],
        scratch_shapes=[pltpu.SemaphoreType.DMA((n,)), pltpu.SemaphoreType.DMA((n,))],
    )(*hs)


def _allreduce_small(pack, name):
    rows, W = pack.shape

    def body(p_ref, o_ref, buf, send_sems, recv_sems):
        x, y, c = _place()
        me = 4 * x + 2 * y + c
        buf[me] = p_ref[...]
        cps = []
        for r in range(1, 8):
            fx, fy, fc = (r >> 2) & 1, (r >> 1) & 1, r & 1
            to = (1 - x if fx else x, 1 - y if fy else y, 1 - c if fc else c)
            cps.append(_remote(p_ref, buf.at[me], send_sems, recv_sems, r - 1, to))
        for cp in cps:
            cp.start()
        for r in range(1, 8):
            fx, fy, fc = (r >> 2) & 1, (r >> 1) & 1, r & 1
            frm = 4 * (1 - x if fx else x) + 2 * (1 - y if fy else y) + (1 - c if fc else c)
            _remote(p_ref, buf.at[frm], send_sems, recv_sems, r - 1, (x, y, c)).wait_recv()
        for cp in cps:
            cp.wait_send()
        acc = buf[0]
        for i in range(1, 8):
            acc = acc + buf[i]
        o_ref[...] = acc

    return pl.pallas_call(
        body, name=name, in_specs=[VMEM], out_specs=VMEM,
        out_shape=jax.ShapeDtypeStruct((rows, W), F32),
        scratch_shapes=[pltpu.VMEM((8, rows, W), F32), pltpu.SemaphoreType.DMA((7,)), pltpu.SemaphoreType.DMA((7,))],
    )(pack)


def _width_groups(arrs):
    widths = []
    for a in arrs:
        if a.shape[-1] not in widths:
            widths.append(a.shape[-1])
    return [[i for i, a in enumerate(arrs) if a.shape[-1] == w] for w in widths]


def _rows2d(a):
    return a.reshape(-1, a.shape[-1])


def _split_rows_like(buf, like, lead=()):
    out, off = [], 0
    for a in like:
        n = a.size // a.shape[-1]
        out.append(buf[..., off:off + n, :].reshape(tuple(lead) + a.shape))
        off += n
    return out


def _join_cols(g):
    nd = g.ndim
    return jnp.moveaxis(g, 0, nd - 2).reshape(g.shape[1:-1] + (N_CHIPS * g.shape[-1],))


def _join_rows(g):
    return jnp.moveaxis(g, 0, 1).reshape(g.shape[1], N_CHIPS * g.shape[2], g.shape[3])


def _row_layout(a, tq):
    P, S, _ = a.shape
    return a.reshape(P, S // tq, tq, 2).transpose(0, 1, 3, 2)


def _pad_row(v, width=FLAT_W):
    flat = v.reshape(-1)
    rows = -(-flat.shape[0] // width)
    return jnp.pad(flat, (0, rows * width - flat.shape[0]))


def kernel(x, attn_norm, ffn_norm, a_w_in, a_conv, a_w_out, kv_norm, w_kvf, b_f, k_norm, b_w_qg, q_norm, b_w_out, ffn_w_up, ffn_conv, ffn_w_down, loss_target, m_attn_norm, m_ffn_norm, m_a_w_in, m_a_conv, m_a_w_out, m_kv_norm, m_w_kvf, m_b_f, m_k_norm, m_b_w_qg, m_q_norm, m_b_w_out, m_ffn_w_up, m_ffn_conv, m_ffn_w_down, v_attn_norm, v_ffn_norm, v_a_w_in, v_a_conv, v_a_w_out, v_kv_norm, v_w_kvf, v_b_f, v_k_norm, v_b_w_qg, v_q_norm, v_b_w_out, v_ffn_w_up, v_ffn_conv, v_ffn_w_down):
    xs = x[0]
    S, D = xs.shape
    H, hd = b_f.shape[0], k_norm.shape[0]
    depth = attn_norm.shape[0]
    n_a = a_w_in.shape[0]
    P = D // LANES
    assert LANES == 2 * hd and H * hd == D, "the attention kernels hold two heads per lane tile"
    mx, my, mc = _place()
    chip = 2 * mx + my

    big = [a_w_in, a_w_out, w_kvf, b_w_qg, b_w_out, ffn_w_up, ffn_w_down]
    groups = _width_groups(big)
    assert n_a >= 2 and depth - n_a >= 2, "the hosted weight gathers are laid out for two layers of each kind"
    first = [a_w_in[:1]]
    behind_a_in = {0: [a_w_out[:1], ffn_w_up[:1]], 1: [ffn_w_up[1:n_a]]}
    behind_ffn_up = {0: [ffn_w_down[:1], a_w_in[1:], a_w_out[1:]], 1: [ffn_w_down[1:n_a], w_kvf, b_w_qg[:1]]}
    late = [b_w_qg[1:], b_w_out, ffn_w_up[n_a:], ffn_w_down[n_a:]]

    def packed(ws):
        idx_groups = _width_groups(ws)
        return [jnp.concatenate([_rows2d(ws[i]).astype(BF) for i in idx]) for idx in idx_groups], idx_groups

    def unpacked(bufs, idx_groups, ws):
        out = [None] * len(ws)
        for idx, buf in zip(idx_groups, bufs):
            for i, part in zip(idx, _split_rows_like(buf, [ws[i] for i in idx], (N_CHIPS,))):
                out[i] = part
        return out

    first_src, first_groups = packed(first)
    late_src, late_groups = packed(late)
    (g_in,) = unpacked(_allgather_weights(first_src), first_groups, first)
    wa_in, wa_out, w_up, w_down, wb_qg, wb_out = list(_join_cols(g_in)), [], [], [], [], []
    kvf_cols = 2 * D + LANES

    def placed(shard):
        full = jnp.zeros(shard.shape[:-1] + (N_CHIPS, shard.shape[-1]), F32)
        full = lax.dynamic_update_slice_in_dim(full, shard[..., None, :], chip, axis=full.ndim - 2)
        return jnp.where(mc == 0, full, 0.0).reshape(-1)

    conv_pack = jnp.concatenate([_pad_row(placed(a_conv)), _pad_row(placed(ffn_conv))]).reshape(-1, FLAT_W)
    conv_full = _allreduce_small(conv_pack, "allgather_conv_taps").reshape(-1)
    n_ac = a_conv.size * N_CHIPS
    a_conv_f = conv_full[:n_ac].reshape(a_conv.shape[:-1] + (-1,))
    off = _pad_row(placed(a_conv)).shape[0]
    ffn_conv_f = conv_full[off:off + ffn_conv.size * N_CHIPS].reshape(ffn_conv.shape[:-1] + (-1,))
    F = ffn_conv_f.shape[-1]

    b_pad = jnp.pad(b_f, (0, LANES - H)).reshape(1, LANES)
    gate_blk = 2 * D // LANES
    tq = _attn_tile(S)
    scale = hd ** -0.5

    saved = []
    cur = xs
    kv = None
    for l in range(depth):
        rec = {"x_in": cur}
        if l < n_a:
            ws = behind_a_in.get(l, [])
            srcs, idx_groups = packed(ws) if ws else ((), [])
            proj, xn, z, bufs = _mixer_in_fwd(cur, attn_norm[l], wa_in[l], a_conv_f[l], f"a_in_{l}", gather=srcs)
            if l == 0:
                g_out, g_up = unpacked(bufs, idx_groups, ws)
                wa_out += list(_join_rows(g_out))
                w_up += list(_join_cols(g_up))
            if l == 1:
                (g_up,) = unpacked(bufs, idx_groups, ws)
                w_up += list(_join_cols(g_up))
            mid = _matmul_residual(z, wa_out[l], cur, f"a_out_{l}")
            rec.update(proj=proj, xn=xn, z=z)
        else:
            j = l - n_a
            if kv is None:
                kvf, hn = _norm_matmul(cur, kv_norm, wkvf, 1, F32, "kvf_proj")
                vb = kvf[0, :, D:2 * D].astype(BF)
                cgate = _gate_fwd(kvf, b_pad, gate_blk, "gate_cumsum")
                kv = dict(kvf=kvf, hn=hn, vb=vb, cgate=cgate, x_in=cur, dk=[], dv=[], dc=[],
                          ka=_augment(kvf, 0, 0, cgate, "k", hd, D, "k_augment", norm_w=k_norm),
                          va=_augment(kvf, 0, 1, cgate, "v", hd, D, "v_augment"))
            qg, xn = _norm_matmul(cur, attn_norm[l], wb_qg[j], 2, F32, f"qg_proj_{j}")
            qa = _augment(qg, 0, 0, kv["cgate"], "q", hd, D, f"q_augment_{j}", norm_w=q_norm[j], scale=scale * LOG2E)
            o, og, m_max, l_sum, late_bufs = _attn_fwd(qa, kv["ka"], kv["va"], qg, hd, f"attn_fwd_{j}",
                                                       gather=late_src if j == 0 else ())
            if j == 0:
                g_qg, g_bout, g_up, g_down = unpacked(late_bufs, late_groups, late)
                wb_qg += list(_join_cols(g_qg))
                wb_out += list(_join_rows(g_bout))
                w_up += list(_join_cols(g_up))
                w_down += list(_join_rows(g_down))
            mid = _matmul_residual(og, wb_out[j], cur, f"b_out_{j}")
            rec.update(qg=qg, xn=xn, qa=qa, o=o, og=og, m=m_max, l=l_sum)
        ws = behind_ffn_up.get(l, [])
        srcs, idx_groups = packed(ws) if ws else ((), [])
        up, xn2, z2, bufs = _ffn_up_fwd(mid, ffn_norm[l], w_up[l], ffn_conv_f[l], f"ffn_up_{l}", gather=srcs)
        if l == 0:
            g_down, g_in, g_out = unpacked(bufs, idx_groups, ws)
            w_down += list(_join_rows(g_down))
            wa_in += list(_join_cols(g_in))
            wa_out += list(_join_rows(g_out))
        if l == 1:
            g_down, g_kvf, g_qg = unpacked(bufs, idx_groups, ws)
            w_down += list(_join_rows(g_down))
            wkvf = jnp.pad(_join_cols(g_kvf), ((0, 0), (0, kvf_cols - (2 * D + H))))
            wb_qg += list(_join_cols(g_qg))
        cur = _matmul_residual(z2, w_down[l], mid, f"ffn_down_{l}")
        rec.update(x_mid=mid, up=up, xn2=xn2, z2=z2)
        saved.append(rec)

    dy, loss_part = _loss_head(cur, loss_target[0], "loss_head")

    def cols_of(g, k):
        c = g.shape[-1] // N_CHIPS
        return g[:, k * c:(k + 1) * c]

    def rows_of(g, k):
        r = g.shape[0] // N_CHIPS
        return g[k * r:(k + 1) * r]

    def reduce_scatter(items, tag, host=None):
        keys = list(items)
        likes = [items[k][1](items[k][0], 0) for k in keys]
        idx_groups = _width_groups(likes)

        def group_buffer(idx):
            rows = [items[keys[i]][1](items[keys[i]][0], k) for k in range(N_CHIPS) for i in idx]
            buf = jnp.concatenate(rows)
            return buf.reshape(N_CHIPS, 2, buf.shape[0] // (2 * N_CHIPS), buf.shape[1])

        g4 = [group_buffer(idx) for idx in idx_groups]
        from_sibling = _pair_exchange(g4, f"grad_pair_exchange_{tag}")
        parts = [_pair_add(g, t, mc, f"grad_pair_add_{tag}{n}") for n, (g, t) in enumerate(zip(g4, from_sibling))]
        from_chips = _chip_exchange(parts, f"grad_chip_exchange_{tag}") if host is None else host(parts)
        mine = [_chip_add(g, t1, t2, mc, chip, f"grad_chip_add_{tag}{n}") for n, (g, t1, t2) in
                enumerate(zip(g4, from_sibling, from_chips))]
        pending.append((keys, likes, idx_groups, mine))

    pending = []

    g_attn_norm, g_ffn_norm = [None] * depth, [None] * depth
    g_a_in, g_a_conv, g_a_out = [None] * n_a, [None] * n_a, [None] * n_a
    g_qg, g_qn, g_bo = [None] * (depth - n_a), [None] * (depth - n_a), [None] * (depth - n_a)
    g_up, g_fc, g_down = [None] * depth, [None] * depth, [None] * depth
    for l in reversed(range(depth)):
        rec = saved[l]
        if l == n_a - 1:
            ready = {(2, 0): (g_kvf, cols_of), (3, 0): (g_qg[0], cols_of)}
        elif l < n_a - 1:
            ready = {(5, l + 1): (g_up[l + 1], cols_of), (6, l + 1): (g_down[l + 1], rows_of),
                     (0, l + 1): (g_a_in[l + 1], cols_of), (1, l + 1): (g_a_out[l + 1], rows_of)}
        else:
            ready = {}
        if ready:
            held = {}

            def behind_mid_bwd(parts):
                *held["out"], exchanged = _ffn_mid_bwd(rec["up"], dy, w_down[l], ffn_conv_f[l], f"ffn_mid_bwd_{l}",
                                                       exchange=parts)
                return exchanged

            reduce_scatter(ready, f"a{l}", behind_mid_bwd)
            dup, g_fc[l] = held["out"]
        else:
            dup, g_fc[l], _ = _ffn_mid_bwd(rec["up"], dy, w_down[l], ffn_conv_f[l], f"ffn_mid_bwd_{l}")
        g_down[l] = _wgrad(rec["z2"], dy[None], f"ffn_down_wgrad_{l}")
        g_up[l] = _wgrad(rec["xn2"], dup, f"ffn_up_wgrad_{l}")
        dy, g_ffn_norm[l] = _dnorm(dup, w_up[l], rec["x_mid"], ffn_norm[l], dy, f"ffn_up_bwd_{l}")
        if l < n_a:
            dproj, g_a_conv[l] = _mixer_mid_bwd(rec["proj"], dy, wa_out[l], a_conv_f[l], f"a_mid_bwd_{l}")
            g_a_out[l] = _wgrad(rec["z"], dy[None], f"a_out_wgrad_{l}")
            g_a_in[l] = _wgrad(rec["xn"], dproj, f"a_in_wgrad_{l}")
            dy, g_attn_norm[l] = _dnorm(dproj, wa_in[l], rec["x_in"], attn_norm[l], dy, f"a_in_bwd_{l}")
        else:
            j = l - n_a
            g_out, dgate, evec = _attn_out_bwd(dy, wb_out[j], rec["o"], rec["qg"], rec["l"], hd,
                                               f"attn_gate_bwd_{j}")
            g_bo[j] = _wgrad(rec["og"], dy[None], f"b_out_wgrad_{j}")
            ready = {(5, l): (g_up[l], cols_of), (6, l): (g_down[l], rows_of), (4, j): (g_bo[j], rows_of)}
            if j + 1 < depth - n_a:
                ready[(3, j + 1)] = (g_qg[j + 1], cols_of)
            held = {}

            def behind_attn_bwd(parts):
                *held["grads"], exchanged = _attn_bwd(rec["qa"], kv["ka"], kv["vb"], g_out, _row_layout(rec["m"], tq),
                                                      _row_layout(evec, tq), hd, f"attn_bwd_{j}", exchange=parts)
                return exchanged

            reduce_scatter(ready, f"b{j}", behind_attn_bwd)
            dqn, dk, dv, dc = held["grads"]
            kv["dk"].append(dk)
            kv["dv"].append(dv)
            kv["dc"].append(dc)
            dq_pre, g_qn[j] = _headnorm_bwd(rec["qg"], 0, 0, q_norm[j], [dqn], D, f"q_norm_bwd_{j}")
            dqg = jnp.stack([dq_pre, dgate])
            g_qg[j] = _wgrad(rec["xn"], dqg, f"qg_wgrad_{j}")
            dy, g_attn_norm[l] = _dnorm(dqg, wb_qg[j], rec["x_in"], attn_norm[l], dy, f"qg_bwd_{j}")
            if l == n_a:
                dk_s, g_k_norm = _headnorm_bwd(kv["kvf"], 0, 0, k_norm, kv["dk"], D, "k_norm_bwd")
                dv_s = functools.reduce(jnp.add, kv["dv"]).astype(BF)
                dc_sum = functools.reduce(jnp.add, kv["dc"])
                dc_pad = jnp.pad(dc_sum.transpose(1, 0, 2).reshape(S, H), ((0, 0), (0, LANES - H)))
                df, db = _gate_bwd(dc_pad, kv["kvf"], b_pad, gate_blk, "gate_bwd")
                dkvf = jnp.concatenate([dk_s, dv_s, df.astype(BF)], axis=1)[None]
                g_kvf = _wgrad(kv["hn"], dkvf, "kvf_wgrad")[:, :2 * D + H]
                g_b_f = db[:H]
                dy, g_kv_norm = _dnorm(dkvf, wkvf, kv["x_in"], kv_norm, dy, "kvf_bwd")
    grad_x = dy[None]

    last = {(0, 0): (g_a_in[0], cols_of), (1, 0): (g_a_out[0], rows_of), (5, 0): (g_up[0], cols_of),
            (6, 0): (g_down[0], rows_of)}
    reduce_scatter(last, "c")
    shared = _pair_share([m for batch in pending for m in batch[3]], "grad_pair_share")
    reduced = {}
    for keys, likes, idx_groups, mine in pending:
        theirs, shared = shared[:len(mine)], shared[len(mine):]
        for idx, m_half, t_half in zip(idx_groups, mine, theirs):
            shard = jnp.where(mc == 0, jnp.concatenate([m_half, t_half]), jnp.concatenate([t_half, m_half]))
            for i, part in zip(idx, _split_rows_like(shard, [likes[i] for i in idx])):
                reduced[keys[i]] = part
    big_grads = [jnp.concatenate([reduced[(i, l)] for l in range(w.shape[0] if w.ndim == 3 else 1)]).reshape(w.shape)
                 for i, w in enumerate(big)]

    small = [loss_part[0, :1], jnp.stack(g_attn_norm), jnp.stack(g_ffn_norm), g_kv_norm, g_b_f, g_k_norm,
             jnp.stack(g_qn), jnp.stack(g_a_conv), jnp.stack(g_fc)]
    small_sum = _allreduce_small(jnp.concatenate([_pad_row(s) for s in small]).reshape(-1, FLAT_W),
                                 "allreduce_small_grads").reshape(-1)
    parts, off = [], 0
    for s in small:
        parts.append(small_sum[off:off + s.size].reshape(s.shape))
        off += _pad_row(s).shape[0]
    loss = parts[0][0]
    gr_attn_norm, gr_ffn_norm, gr_kv_norm, gr_b_f, gr_k_norm, gr_q_norm, gr_a_conv_full, gr_ffn_conv_full = parts[1:]

    def my_cols(full):
        c = full.shape[-1] // N_CHIPS
        return lax.dynamic_slice_in_dim(full, chip * c, c, axis=full.ndim - 1)

    gr_a_in, gr_a_out, gr_kvf, gr_qg, gr_bo, gr_up, gr_down = big_grads
    grads = [gr_attn_norm, gr_ffn_norm, gr_a_in, my_cols(gr_a_conv_full), gr_a_out, gr_kv_norm, gr_kvf, gr_b_f,
             gr_k_norm, gr_qg, gr_q_norm, gr_bo, gr_up, my_cols(gr_ffn_conv_full), gr_down]
    weights = [attn_norm, ffn_norm, a_w_in, a_conv, a_w_out, kv_norm, w_kvf, b_f, k_norm, b_w_qg, q_norm, b_w_out,
               ffn_w_up, ffn_conv, ffn_w_down]
    ms = [m_attn_norm, m_ffn_norm, m_a_w_in, m_a_conv, m_a_w_out, m_kv_norm, m_w_kvf, m_b_f, m_k_norm, m_b_w_qg,
          m_q_norm, m_b_w_out, m_ffn_w_up, m_ffn_conv, m_ffn_w_down]
    vs = [v_attn_norm, v_ffn_norm, v_a_w_in, v_a_conv, v_a_w_out, v_kv_norm, v_w_kvf, v_b_f, v_k_norm, v_b_w_qg,
          v_q_norm, v_b_w_out, v_ffn_w_up, v_ffn_conv, v_ffn_w_down]
    deltas, new_ms, new_vs = [], [], []
    for i, (w, g, m, v) in enumerate(zip(weights, grads, ms, vs)):
        d, nm, nv = _adamw(w, g, m, v, f"adamw_{i}")
        deltas.append(d)
        new_ms.append(nm)
        new_vs.append(nv)
    return (loss, grad_x, *grads, *deltas, *new_ms, *new_vs)
```

```python
import functools

import jax
import jax.numpy as jnp
from jax import lax
from jax.experimental import pallas as pl
from jax.experimental.pallas import tpu as pltpu

F32 = jnp.float32
BF = jnp.bfloat16
LANES = 128
SUBLANES = 8
RMS_EPS = 1e-6
LOG2E = 1.4426950408889634
FLAT_W = 1024
N_CHIPS = 4
CONV_W = 3
HALO = SUBLANES

ADAM_LR = 0.001
ADAM_B1 = 0.9
ADAM_B2 = 0.999
ADAM_EPS = 1e-08
ADAM_WD = 0.01
ADAM_STEP = 10

MESH = pl.DeviceIdType.MESH
ANY = pl.BlockSpec(memory_space=pl.ANY)
VMEM = pl.BlockSpec(memory_space=pltpu.VMEM)
NT_DIMS = (((1,), (1,)), ((), ()))
TN_DIMS = (((0,), (0,)), ((), ()))


def _tile(n, pref, mult=LANES):
    t = (min(pref, n) // mult) * mult
    while t >= mult:
        if n % t == 0:
            break
        t -= mult
    if t < mult or (t * 4 < pref and n <= 4 * pref):
        return n
    return t


def _params(*sem):
    return pltpu.CompilerParams(dimension_semantics=sem)


def _norm_matmul(x, g, w, parts, out_dtype, name):
    S, D = x.shape
    C = w.shape[1] // parts
    ts, tn = _tile(S, 512, SUBLANES), _tile(C, 1408)
    npc = C // tn

    def body(x_ref, g_ref, w_ref, o_ref, xn_ref):
        @pl.when(pl.program_id(1) == 0)
        def _():
            xf = x_ref[...]
            r = lax.rsqrt(jnp.mean(xf * xf, axis=-1, keepdims=True) + RMS_EPS)
            xn_ref[...] = (xf * r * g_ref[...]).astype(BF)

        o_ref[0] = jnp.dot(xn_ref[...], w_ref[...], preferred_element_type=F32).astype(out_dtype)

    return pl.pallas_call(
        body, name=name, grid=(S // ts, parts * npc),
        in_specs=[pl.BlockSpec((ts, D), lambda s, n: (s, 0)),
                  pl.BlockSpec((1, D), lambda s, n: (0, 0)),
                  pl.BlockSpec((D, tn), lambda s, n: (0, n))],
        out_specs=[pl.BlockSpec((1, ts, tn), lambda s, n: (n // npc, s, n % npc)),
                   pl.BlockSpec((ts, D), lambda s, n: (s, 0))],
        out_shape=[jax.ShapeDtypeStruct((parts, S, C), out_dtype), jax.ShapeDtypeStruct((S, D), BF)],
        compiler_params=_params("parallel", "arbitrary"),
    )(x, g.reshape(1, D), w)


def _shift_down(u, prev, k):
    r = pltpu.roll(u, k, 0)
    row = lax.broadcasted_iota(jnp.int32, (HALO, u.shape[1]), 0)
    head = r[0:HALO]
    for j in range(k):
        head = jnp.where(row == j, prev[HALO - k + j:HALO - k + j + 1, :], head)
    return jnp.concatenate([head, r[HALO:]], axis=0)


def _shift_up(d, nxt, k):
    n = d.shape[0]
    r = pltpu.roll(d, n - k, 0)
    row = lax.broadcasted_iota(jnp.int32, (HALO, d.shape[1]), 0)
    tail = r[n - HALO:n]
    for j in range(k):
        tail = jnp.where(row == HALO - k + j, nxt[j:j + 1, :], tail)
    return jnp.concatenate([r[0:n - HALO], tail], axis=0)


def _conv3(u, prev, w):
    return _shift_down(u, prev, 2) * w[0:1] + _shift_down(u, prev, 1) * w[1:2] + u * w[2:3]


def _conv3_t(d, nxt, w):
    return d * w[2:3] + _shift_up(d, nxt, 1) * w[1:2] + _shift_up(d, nxt, 2) * w[0:1]


def _tap_rows(t0, t1, t2):
    row = lax.broadcasted_iota(jnp.int32, (SUBLANES, t0.shape[1]), 0)
    return jnp.where(row == 0, t0, jnp.where(row == 1, t1, jnp.where(row == 2, t2, 0.0)))


def _pad_conv(cw):
    return jnp.pad(cw, ((0, SUBLANES - CONV_W), (0, 0)))


def _mixer_in_fwd(x, g, w, cw, name, gather=()):
    S, D = x.shape
    C = w.shape[1] // 3
    ts, tc = _tile(S, 512, SUBLANES), _tile(C, 1024)
    nc = C // tc
    n_s = S // ts
    n_g = len(gather)

    def body(x_ref, g_ref, wb_ref, wc_ref, wh_ref, cw_ref, *rest):
        p_ref, xn_ref, z_ref = rest[n_g:n_g + 3]
        carry = rest[2 * n_g + 3]
        s, c = pl.program_id(0), pl.program_id(1)
        if n_g:
            start, pass_on, finish = _gather_phases([a.shape[0] for a in gather], rest[:n_g],
                                                    rest[n_g + 3:2 * n_g + 3], *rest[2 * n_g + 4:])
            pl.when((s == 0) & (c == 0))(start)
            pl.when((s == (3 * n_s) // 4) & (c == 0))(pass_on)

        @pl.when(c == 0)
        def _():
            xf = x_ref[...]
            r = lax.rsqrt(jnp.mean(xf * xf, axis=-1, keepdims=True) + RMS_EPS)
            xn_ref[...] = (xf * r * g_ref[...]).astype(BF)

        @pl.when(s == 0)
        def _():
            carry[c] = jnp.zeros((HALO, tc), F32)

        xn = xn_ref[...]
        parts = [jnp.dot(xn, w_ref[...], preferred_element_type=F32).astype(BF) for w_ref in (wb_ref, wc_ref, wh_ref)]
        for p, v in enumerate(parts):
            p_ref[p] = v
        u = parts[1].astype(F32) * parts[2].astype(F32)
        cv = _conv3(u, carry[c], cw_ref[...])
        z_ref[...] = (parts[0].astype(F32) * cv).astype(BF)
        carry[c] = u[ts - HALO:ts, :]
        if n_g:
            pl.when((s == n_s - 1) & (c == nc - 1))(finish)

    wspec = lambda p: pl.BlockSpec((D, tc), lambda s, c: (0, p * nc + c))
    sems = [pltpu.SemaphoreType.DMA((6 * n_g,)), pltpu.SemaphoreType.DMA((6 * n_g,))] if n_g else []
    outs = pl.pallas_call(
        body, name=name, grid=(n_s, nc),
        in_specs=[pl.BlockSpec((ts, D), lambda s, c: (s, 0)), pl.BlockSpec((1, D), lambda s, c: (0, 0)),
                  wspec(0), wspec(1), wspec(2), pl.BlockSpec((SUBLANES, tc), lambda s, c: (0, c))] + [ANY] * n_g,
        out_specs=[pl.BlockSpec((3, ts, tc), lambda s, c: (0, s, c)),
                   pl.BlockSpec((ts, D), lambda s, c: (s, 0)),
                   pl.BlockSpec((ts, tc), lambda s, c: (s, c))] + [ANY] * n_g,
        out_shape=[jax.ShapeDtypeStruct((3, S, C), BF), jax.ShapeDtypeStruct((S, D), BF),
                   jax.ShapeDtypeStruct((S, C), BF)]
        + [jax.ShapeDtypeStruct((N_CHIPS,) + a.shape, a.dtype) for a in gather],
        scratch_shapes=[pltpu.VMEM((nc, HALO, tc), F32)] + sems,
        compiler_params=_params("arbitrary", "arbitrary"),
    )(x, g.reshape(1, D), w, w, w, _pad_conv(cw), *gather)
    return tuple(outs[:3]) + (_fill_own(outs[3:], gather),)


def _ffn_up_fwd(x, g, w, cw, name, gather=()):
    S, D = x.shape
    C = w.shape[1] // 2
    ts, tc = _tile(S, 512, SUBLANES), _tile(C, 1408)
    nc = C // tc
    n_s = S // ts
    n_g = len(gather)

    def body(x_ref, g_ref, wa_ref, wg_ref, cw_ref, *rest):
        up_ref, xn_ref, z_ref = rest[n_g:n_g + 3]
        carry = rest[2 * n_g + 3]
        s, c = pl.program_id(0), pl.program_id(1)
        if n_g:
            start, pass_on, finish = _gather_phases([a.shape[0] for a in gather], rest[:n_g],
                                                    rest[n_g + 3:2 * n_g + 3], *rest[2 * n_g + 4:])
            pl.when((s == 0) & (c == 0))(start)
            pl.when((s == (3 * n_s) // 4) & (c == 0))(pass_on)

        @pl.when(c == 0)
        def _():
            xf = x_ref[...]
            r = lax.rsqrt(jnp.mean(xf * xf, axis=-1, keepdims=True) + RMS_EPS)
            xn_ref[...] = (xf * r * g_ref[...]).astype(BF)

        @pl.when(s == 0)
        def _():
            carry[c] = jnp.zeros((HALO, tc), F32)

        xn = xn_ref[...]
        a_b = jnp.dot(xn, wa_ref[...], preferred_element_type=F32).astype(BF)
        g_b = jnp.dot(xn, wg_ref[...], preferred_element_type=F32).astype(BF)
        up_ref[0] = a_b
        up_ref[1] = g_b
        a_pre = a_b.astype(F32)
        a = _conv3(a_pre, carry[c], cw_ref[...])
        z_ref[...] = (a * jax.nn.sigmoid(a) * g_b.astype(F32)).astype(BF)
        carry[c] = a_pre[ts - HALO:ts, :]
        if n_g:
            pl.when((s == n_s - 1) & (c == nc - 1))(finish)

    sems = [pltpu.SemaphoreType.DMA((6 * n_g,)), pltpu.SemaphoreType.DMA((6 * n_g,))] if n_g else []
    outs = pl.pallas_call(
        body, name=name, grid=(n_s, nc),
        in_specs=[pl.BlockSpec((ts, D), lambda s, c: (s, 0)),
                  pl.BlockSpec((1, D), lambda s, c: (0, 0)),
                  pl.BlockSpec((D, tc), lambda s, c: (0, c)),
                  pl.BlockSpec((D, tc), lambda s, c: (0, nc + c)),
                  pl.BlockSpec((SUBLANES, tc), lambda s, c: (0, c))] + [ANY] * n_g,
        out_specs=[pl.BlockSpec((2, ts, tc), lambda s, c: (0, s, c)),
                   pl.BlockSpec((ts, D), lambda s, c: (s, 0)),
                   pl.BlockSpec((ts, tc), lambda s, c: (s, c))] + [ANY] * n_g,
        out_shape=[jax.ShapeDtypeStruct((2, S, C), BF), jax.ShapeDtypeStruct((S, D), BF),
                   jax.ShapeDtypeStruct((S, C), BF)]
        + [jax.ShapeDtypeStruct((N_CHIPS,) + a.shape, a.dtype) for a in gather],
        scratch_shapes=[pltpu.VMEM((nc, HALO, tc), F32)] + sems,
        compiler_params=_params("arbitrary", "arbitrary"),
    )(x, g.reshape(1, D), w, w, _pad_conv(cw), *gather)
    return tuple(outs[:3]) + (_fill_own(outs[3:], gather),)


def _mixer_mid_bwd(proj, dy, w_out, cw, name, exchange=()):
    _, S, C = proj.shape
    D = dy.shape[1]
    ts, tc = _tile(S, 512, SUBLANES), _tile(C, 1024)
    n_s = S // ts
    per = ts // HALO

    n_x = len(exchange)
    n_c = C // tc

    def body(b_ref, c_ref, h_ref, dy_ref, w_ref, cp_ref, hp_ref, cw_ref, *rest):
        d_ref, dcw_ref = rest[n_x:n_x + 2]
        carry = rest[2 * n_x + 2]
        i = pl.program_id(1)
        if n_x:
            copies = lambda: _chip_copies(rest[:n_x], rest[n_x + 2:2 * n_x + 2], *rest[2 * n_x + 3:])

            @pl.when((pl.program_id(0) == 0) & (i == 0))
            def _():
                for cp in copies():
                    cp.start()

        w = cw_ref[...]
        dz = lax.dot_general(dy_ref[...].astype(BF), w_ref[...], NT_DIMS, preferred_element_type=F32)
        b, c, h = b_ref[0].astype(F32), c_ref[0].astype(F32), h_ref[0].astype(F32)
        u = c * h
        prev = jnp.where(i < n_s - 1, cp_ref[0].astype(F32) * hp_ref[0].astype(F32), 0.0)
        u1, u2 = _shift_down(u, prev, 1), _shift_down(u, prev, 2)
        cv = u2 * w[0:1] + u1 * w[1:2] + u * w[2:3]
        dcv = dz * b
        nxt = jnp.where(i > 0, carry[...], 0.0)
        du = _conv3_t(dcv, nxt, w)
        d_ref[0] = (dz * cv).astype(BF)
        d_ref[1] = (du * h).astype(BF)
        d_ref[2] = (du * c).astype(BF)
        carry[...] = dcv[0:HALO, :]
        part = _tap_rows(jnp.sum(dcv * u2, axis=0, keepdims=True), jnp.sum(dcv * u1, axis=0, keepdims=True),
                         jnp.sum(dcv * u, axis=0, keepdims=True))

        @pl.when(i == 0)
        def _():
            dcw_ref[...] = part

        @pl.when(i > 0)
        def _():
            dcw_ref[...] += part

        if n_x:
            @pl.when((pl.program_id(0) == n_c - 1) & (i == n_s - 1))
            def _():
                for cp in copies():
                    cp.wait()

    tile = lambda p: pl.BlockSpec((1, ts, tc), lambda c, i: (p, n_s - 1 - i, c))
    before = lambda p: pl.BlockSpec((1, HALO, tc), lambda c, i: (p, jnp.maximum((n_s - 1 - i) * per - 1, 0), c))
    sems = [pltpu.SemaphoreType.DMA((3 * n_x,)), pltpu.SemaphoreType.DMA((3 * n_x,))] if n_x else []
    outs = pl.pallas_call(
        body, name=name, grid=(C // tc, n_s),
        in_specs=[tile(0), tile(1), tile(2),
                  pl.BlockSpec((ts, D), lambda c, i: (n_s - 1 - i, 0)),
                  pl.BlockSpec((tc, D), lambda c, i: (c, 0)),
                  before(1), before(2),
                  pl.BlockSpec((SUBLANES, tc), lambda c, i: (0, c))] + [ANY] * n_x,
        out_specs=[pl.BlockSpec((3, ts, tc), lambda c, i: (0, n_s - 1 - i, c)),
                   pl.BlockSpec((SUBLANES, tc), lambda c, i: (0, c))] + [ANY] * n_x,
        out_shape=[jax.ShapeDtypeStruct((3, S, C), BF), jax.ShapeDtypeStruct((SUBLANES, C), F32)]
        + [jax.ShapeDtypeStruct((3,) + a.shape[1:], a.dtype) for a in exchange],
        scratch_shapes=[pltpu.VMEM((HALO, tc), F32)] + sems,
        compiler_params=_params("arbitrary" if n_x else "parallel", "arbitrary"),
    )(proj, proj, proj, dy, w_out, proj, proj, _pad_conv(cw), *exchange)
    return outs[0], outs[1][:CONV_W], list(outs[2:])


def _ffn_mid_bwd(up, dy, w_down, cw, name, exchange=()):
    _, S, C = up.shape
    D = dy.shape[1]
    ts, tc = _tile(S, 512, SUBLANES), _tile(C, 1408)
    n_s = S // ts
    n_c = C // tc
    per = ts // HALO
    n_x = len(exchange)

    def body(a_ref, g_ref, dy_ref, w_ref, ap_ref, cw_ref, *rest):
        d_ref, dcw_ref = rest[n_x:n_x + 2]
        carry = rest[2 * n_x + 2]
        i = pl.program_id(1)
        if n_x:
            copies = lambda: _chip_copies(rest[:n_x], rest[n_x + 2:2 * n_x + 2], *rest[2 * n_x + 3:])

            @pl.when((pl.program_id(0) == 0) & (i == 0))
            def _():
                for cp in copies():
                    cp.start()

        w = cw_ref[...]
        dz = lax.dot_general(dy_ref[...].astype(BF), w_ref[...], NT_DIMS, preferred_element_type=F32)
        a_pre, g = a_ref[0].astype(F32), g_ref[0].astype(F32)
        prev = jnp.where(i < n_s - 1, ap_ref[0].astype(F32), 0.0)
        a1, a2 = _shift_down(a_pre, prev, 1), _shift_down(a_pre, prev, 2)
        a = a2 * w[0:1] + a1 * w[1:2] + a_pre * w[2:3]
        sg = jax.nn.sigmoid(a)
        da = dz * g * (sg * (1.0 + a * (1.0 - sg)))
        nxt = jnp.where(i > 0, carry[...], 0.0)
        d_ref[0] = _conv3_t(da, nxt, w).astype(BF)
        d_ref[1] = (dz * (a * sg)).astype(BF)
        carry[...] = da[0:HALO, :]
        part = _tap_rows(jnp.sum(da * a2, axis=0, keepdims=True), jnp.sum(da * a1, axis=0, keepdims=True),
                         jnp.sum(da * a_pre, axis=0, keepdims=True))

        @pl.when(i == 0)
        def _():
            dcw_ref[...] = part

        @pl.when(i > 0)
        def _():
            dcw_ref[...] += part

        if n_x:
            @pl.when((pl.program_id(0) == n_c - 1) & (i == n_s - 1))
            def _():
                for cp in copies():
                    cp.wait()

    tile = lambda p: pl.BlockSpec((1, ts, tc), lambda c, i: (p, n_s - 1 - i, c))
    sems = [pltpu.SemaphoreType.DMA((3 * n_x,)), pltpu.SemaphoreType.DMA((3 * n_x,))] if n_x else []
    outs = pl.pallas_call(
        body, name=name, grid=(n_c, n_s),
        in_specs=[tile(0), tile(1),
                  pl.BlockSpec((ts, D), lambda c, i: (n_s - 1 - i, 0)),
                  pl.BlockSpec((tc, D), lambda c, i: (c, 0)),
                  pl.BlockSpec((1, HALO, tc), lambda c, i: (0, jnp.maximum((n_s - 1 - i) * per - 1, 0), c)),
                  pl.BlockSpec((SUBLANES, tc), lambda c, i: (0, c))] + [ANY] * n_x,
        out_specs=[pl.BlockSpec((2, ts, tc), lambda c, i: (0, n_s - 1 - i, c)),
                   pl.BlockSpec((SUBLANES, tc), lambda c, i: (0, c))] + [ANY] * n_x,
        out_shape=[jax.ShapeDtypeStruct((2, S, C), BF), jax.ShapeDtypeStruct((SUBLANES, C), F32)]
        + [jax.ShapeDtypeStruct((3,) + a.shape[1:], a.dtype) for a in exchange],
        scratch_shapes=[pltpu.VMEM((HALO, tc), F32)] + sems,
        compiler_params=_params("arbitrary" if n_x else "parallel", "arbitrary"),
    )(up, up, dy, w_down, up, _pad_conv(cw), *exchange)
    return outs[0], outs[1][:CONV_W], list(outs[2:])


def _matmul_residual(z, w, x, name):
    S, K = z.shape
    D = w.shape[1]
    ts = _tile(S, 512, SUBLANES)

    def body(z_ref, w_ref, x_ref, o_ref):
        o_ref[...] = x_ref[...] + jnp.dot(z_ref[...], w_ref[...], preferred_element_type=F32)

    return pl.pallas_call(
        body, name=name, grid=(S // ts,),
        in_specs=[pl.BlockSpec((ts, K), lambda s: (s, 0)), pl.BlockSpec((K, D), lambda s: (0, 0)),
                  pl.BlockSpec((ts, D), lambda s: (s, 0))],
        out_specs=pl.BlockSpec((ts, D), lambda s: (s, 0)),
        out_shape=jax.ShapeDtypeStruct((S, D), F32),
        compiler_params=_params("parallel"),
    )(z, w, x)


def _wgrad(a, b, name):
    S, M = a.shape
    P, _, C = b.shape
    tm, tn, tk = _tile(M, 1408), _tile(C, 1408), _tile(S, 1024, SUBLANES)
    nnc = C // tn

    def body(a_ref, b_ref, o_ref):
        @pl.when(pl.program_id(2) == 0)
        def _():
            o_ref[...] = jnp.zeros_like(o_ref)

        o_ref[...] += lax.dot_general(a_ref[...], b_ref[0].astype(BF), TN_DIMS, preferred_element_type=F32)

    return pl.pallas_call(
        body, name=name, grid=(M // tm, P * nnc, S // tk),
        in_specs=[pl.BlockSpec((tk, tm), lambda m, n, k: (k, m)),
                  pl.BlockSpec((1, tk, tn), lambda m, n, k: (n // nnc, k, n % nnc))],
        out_specs=pl.BlockSpec((tm, tn), lambda m, n, k: (m, n)),
        out_shape=jax.ShapeDtypeStruct((M, P * C), F32),
        compiler_params=_params("parallel", "parallel", "arbitrary"),
    )(a, b)


def _dnorm(dp, w, x, g, dy, name):
    P, S, C = dp.shape
    D = x.shape[1]
    ts = _tile(S, 512, SUBLANES)

    def body(dp_ref, w_ref, x_ref, g_ref, dy_ref, dx_ref, dg_ref):
        @pl.when(pl.program_id(0) == 0)
        def _():
            dg_ref[...] = jnp.zeros_like(dg_ref)

        dxn = lax.dot_general(dp_ref[0], w_ref[:, 0:C], NT_DIMS, preferred_element_type=F32)
        for p in range(1, P):
            dxn = dxn + lax.dot_general(dp_ref[p], w_ref[:, p * C:(p + 1) * C], NT_DIMS, preferred_element_type=F32)
        xf = x_ref[...]
        r = lax.rsqrt(jnp.mean(xf * xf, axis=-1, keepdims=True) + RMS_EPS)
        xhat = xf * r
        dxhat = dxn * g_ref[...]
        dx_ref[...] = dy_ref[...] + r * (dxhat - xhat * jnp.mean(dxhat * xhat, axis=-1, keepdims=True))
        dg_ref[...] += jnp.broadcast_to(jnp.sum(dxn * xhat, axis=0, keepdims=True), dg_ref.shape)

    dx, dg = pl.pallas_call(
        body, name=name, grid=(S // ts,),
        in_specs=[pl.BlockSpec((P, ts, C), lambda s: (0, s, 0)),
                  pl.BlockSpec((D, P * C), lambda s: (0, 0), pipeline_mode=pl.Buffered(1)),
                  pl.BlockSpec((ts, D), lambda s: (s, 0)),
                  pl.BlockSpec((1, D), lambda s: (0, 0)),
                  pl.BlockSpec((ts, D), lambda s: (s, 0))],
        out_specs=[pl.BlockSpec((ts, D), lambda s: (s, 0)),
                   pl.BlockSpec((SUBLANES, D), lambda s: (0, 0))],
        out_shape=[jax.ShapeDtypeStruct((S, D), F32), jax.ShapeDtypeStruct((SUBLANES, D), F32)],
        compiler_params=_params("arbitrary"),
    )(dp, w, x, g.reshape(1, D), dy)
    return dx, dg[0]


def _head_masks(shape, hd):
    lane = lax.broadcasted_iota(jnp.int32, shape, 1)
    return lane < hd


def _pair_sum(v, lo):
    s0 = jnp.sum(jnp.where(lo, v, 0.0), axis=-1, keepdims=True)
    s1 = jnp.sum(jnp.where(lo, 0.0, v), axis=-1, keepdims=True)
    return jnp.where(lo, s0, s1)


def _headnorm_bwd(src, part, colblk, w, dys, D, name):
    S = src.shape[1]
    hd = w.shape[0]
    ts = _tile(S, 512, SUBLANES)
    w2 = jnp.tile(w, LANES // hd).reshape(1, LANES)
    n_dy = len(dys)

    def body(x_ref, w_ref, *rest):
        dy_refs, dx_ref, dw_ref = rest[:n_dy], rest[n_dy], rest[n_dy + 1]

        @pl.when(pl.program_id(0) == 0)
        def _():
            dw_ref[...] = jnp.zeros_like(dw_ref)

        lo = _head_masks((ts, LANES), hd)
        for t in range(D // LANES):
            cols = slice(t * LANES, (t + 1) * LANES)
            xt = x_ref[0, :, cols]
            dy = dy_refs[0][:, cols]
            for other in dy_refs[1:]:
                dy = dy + other[:, cols]
            r = lax.rsqrt(_pair_sum(xt * xt, lo) * (1.0 / hd) + RMS_EPS)
            xhat = xt * r
            dxhat = dy * w_ref[...]
            mean = _pair_sum(dxhat * xhat, lo) * (1.0 / hd)
            dx_ref[:, cols] = (r * (dxhat - xhat * mean)).astype(BF)
            dw_ref[:, cols] += jnp.broadcast_to(jnp.sum(dy * xhat, axis=0, keepdims=True), (SUBLANES, LANES))

    dx, dw = pl.pallas_call(
        body, name=name, grid=(S // ts,),
        in_specs=[pl.BlockSpec((1, ts, D), lambda s: (part, s, colblk)), pl.BlockSpec((1, LANES), lambda s: (0, 0))]
        + [pl.BlockSpec((ts, D), lambda s: (s, 0))] * n_dy,
        out_specs=[pl.BlockSpec((ts, D), lambda s: (s, 0)), pl.BlockSpec((SUBLANES, D), lambda s: (0, 0))],
        out_shape=[jax.ShapeDtypeStruct((S, D), BF), jax.ShapeDtypeStruct((SUBLANES, D), F32)],
        compiler_params=_params("arbitrary"),
    )(src, w2, *dys)
    return dx, jnp.sum(dw[0].reshape(D // hd, hd), axis=0)


def _tri(n, lower):
    r, c = lax.broadcasted_iota(jnp.int32, (n, n), 0), lax.broadcasted_iota(jnp.int32, (n, n), 1)
    return jnp.where((c <= r) if lower else (c >= r), 1.0, 0.0).astype(BF)


def _dot_exact(t, v):
    hi = v.astype(BF)
    r1 = v - hi.astype(F32)
    mid = r1.astype(BF)
    lo = (r1 - mid.astype(F32)).astype(BF)
    dot = lambda u: jnp.dot(t, u, preferred_element_type=F32)
    return dot(hi) + dot(mid) + dot(lo)


def _gate_fwd(kvf, b_pad, colblk, name):
    S = kvf.shape[1]
    ts = _tile(S, 512, SUBLANES)

    def body(f_ref, b_ref, c_ref, carry):
        @pl.when(pl.program_id(0) == 0)
        def _():
            carry[...] = jnp.zeros_like(carry)

        f = f_ref[0] + b_ref[...]
        ls = jnp.minimum(f, 0.0) - jnp.log1p(jnp.exp(-jnp.abs(f)))
        tri = _tri(ts, lower=True)
        c = _dot_exact(tri, ls) + carry[0:1, :]
        c_ref[...] = c
        carry[...] = jnp.broadcast_to(c[ts - 1:ts, :], carry.shape)

    return pl.pallas_call(
        body, name=name, grid=(S // ts,),
        in_specs=[pl.BlockSpec((1, ts, LANES), lambda s: (0, s, colblk)), pl.BlockSpec((1, LANES), lambda s: (0, 0))],
        out_specs=pl.BlockSpec((ts, LANES), lambda s: (s, 0)),
        out_shape=jax.ShapeDtypeStruct((S, LANES), F32),
        scratch_shapes=[pltpu.VMEM((SUBLANES, LANES), F32)],
        compiler_params=_params("arbitrary"),
    )(kvf, b_pad)


def _gate_bwd(dc, kvf, b_pad, colblk, name):
    S = kvf.shape[1]
    ts = _tile(S, 512, SUBLANES)
    n_s = S // ts

    def body(dc_ref, f_ref, b_ref, df_ref, db_ref, carry):
        @pl.when(pl.program_id(0) == 0)
        def _():
            carry[...] = jnp.zeros_like(carry)
            db_ref[...] = jnp.zeros_like(db_ref)

        tri = _tri(ts, lower=False)
        dls = _dot_exact(tri, dc_ref[...]) + carry[0:1, :]
        f = f_ref[0] + b_ref[...]
        df = dls * jax.nn.sigmoid(-f)
        df_ref[...] = df
        db_ref[...] += jnp.broadcast_to(jnp.sum(df, axis=0, keepdims=True), db_ref.shape)
        carry[...] = jnp.broadcast_to(dls[0:1, :], carry.shape)

    df, db = pl.pallas_call(
        body, name=name, grid=(n_s,),
        in_specs=[pl.BlockSpec((ts, LANES), lambda s: (n_s - 1 - s, 0)),
                  pl.BlockSpec((1, ts, LANES), lambda s: (0, n_s - 1 - s, colblk)),
                  pl.BlockSpec((1, LANES), lambda s: (0, 0))],
        out_specs=[pl.BlockSpec((ts, LANES), lambda s: (n_s - 1 - s, 0)),
                   pl.BlockSpec((SUBLANES, LANES), lambda s: (0, 0))],
        out_shape=[jax.ShapeDtypeStruct((S, LANES), F32), jax.ShapeDtypeStruct((SUBLANES, LANES), F32)],
        scratch_shapes=[pltpu.VMEM((SUBLANES, LANES), F32)],
        compiler_params=_params("arbitrary"),
    )(dc, kvf, b_pad)
    return df, db[0]


def _attn_tile(S):
    return _tile(S, 512, LANES)


def _split_heads(v, lo):
    zero = jnp.zeros_like(v)
    return jnp.where(lo, v, zero), jnp.where(lo, zero, v)


def _augment(src, part, colblk, c, mode, hd, D, name, norm_w=None, scale=1.0):
    S = src.shape[1]
    ts = _tile(S, 512, 2 * SUBLANES)
    w2 = jnp.tile(jnp.ones((hd,), F32) if norm_w is None else norm_w, LANES // hd).reshape(1, LANES)

    def body(b_ref, w_ref, c_ref, o0_ref, o1_ref):
        lane = lax.broadcasted_iota(jnp.int32, (ts, LANES), 1)
        lo = lane < hd
        cc = c_ref[...] * LOG2E
        for t in range(D // LANES):
            cols = slice(t * LANES, (t + 1) * LANES)
            bt = b_ref[0, :, cols]
            if norm_w is not None:
                r = lax.rsqrt(_pair_sum(bt * bt, lo) * (1.0 / hd) + RMS_EPS)
                bt = bt * r * w_ref[...] * scale
            bt = bt.astype(BF)
            for h, o_ref in ((0, o0_ref), (1, o1_ref)):
                first = hd if h == 0 else 0
                keep = (lane < hd) if h == 0 else (lane >= hd)
                if mode == "v":
                    vals = (1.0,)
                else:
                    col = cc[:, 2 * t + h:2 * t + h + 1]
                    hi = col.astype(BF).astype(F32)
                    mid = (col - hi).astype(BF).astype(F32)
                    pieces = (hi, mid, col - hi - mid)
                    vals = pieces + (1.0, 1.0, 1.0) if mode == "q" else (1.0, 1.0, 1.0) + tuple(-v for v in pieces)
                aug = jnp.zeros((ts, LANES), F32)
                for i, v in enumerate(vals):
                    aug = jnp.where(lane == first + i, v, aug)
                o_ref[:, cols] = jnp.where(keep, bt, aug.astype(BF))

    spec = pl.BlockSpec((ts, D), lambda s: (s, 0))
    return pl.pallas_call(
        body, name=name, grid=(S // ts,),
        in_specs=[pl.BlockSpec((1, ts, D), lambda s: (part, s, colblk)), pl.BlockSpec((1, LANES), lambda s: (0, 0)),
                  pl.BlockSpec((ts, LANES), lambda s: (s, 0))],
        out_specs=[spec, spec],
        out_shape=[jax.ShapeDtypeStruct((S, D), BF)] * 2,
        compiler_params=_params("parallel"),
    )(src, w2, c)


def _attn_fwd(qa, ka, va, qg, hd, name, gather=()):
    S, D = qa[0].shape
    P = D // LANES
    tq = _attn_tile(S)
    nq = S // tq
    n_g = len(gather)

    def body(q0_ref, q1_ref, k0_ref, k1_ref, v0_ref, v1_ref, g_ref, *rest):
        o_ref, og_ref, m_ref, l_ref = rest[n_g:n_g + 4]
        s_buf = rest[2 * n_g + 4]
        pair, qi = pl.program_id(0), pl.program_id(1)
        if n_g:
            start, pass_on, finish = _gather_phases([a.shape[0] for a in gather], rest[:n_g],
                                                    rest[n_g + 4:2 * n_g + 4], *rest[2 * n_g + 5:])
            pl.when((pair == 0) & (qi == 0))(start)
            pl.when((pair == P // 2) & (qi == 0))(pass_on)
        lo = _head_masks((tq, LANES), hd)
        qh = (q0_ref[...], q1_ref[...])
        k_refs, v_refs = (k0_ref, k1_ref), (v0_ref, v1_ref)
        causal = lax.broadcasted_iota(jnp.int32, (tq, tq), 1) <= lax.broadcasted_iota(jnp.int32, (tq, tq), 0)

        def scores(ki, slot):
            off = pl.multiple_of(ki * tq, tq)
            for h in range(2):
                s_buf[slot, h] = lax.dot_general(qh[h], k_refs[h][pl.ds(off, tq), :], NT_DIMS,
                                                 preferred_element_type=F32)

        def consume(ki, slot, carry, masked):
            off = pl.multiple_of(ki * tq, tq)
            out = []
            for h in range(2):
                m, acc = carry[h]
                s = s_buf[slot, h]
                if masked:
                    s = jnp.where(causal, s, -jnp.inf)
                m_new = jnp.maximum(m, jnp.ceil(jnp.max(s, axis=-1, keepdims=True)))
                p = jnp.exp2(s - m_new)
                acc = jnp.exp2(m - m_new) * acc + jnp.dot(p.astype(BF), v_refs[h][pl.ds(off, tq), :],
                                                          preferred_element_type=F32)
                out.append((m_new, acc))
            return tuple(out)

        def step(j, carry):
            scores(2 * j + 1, 1)
            carry = consume(2 * j, 0, carry, False)
            scores(2 * j + 2, 0)
            return consume(2 * j + 1, 1, carry, False)

        def finish_even(carry):
            return consume(qi, 0, carry, True)

        def finish_odd(carry):
            scores(qi, 1)
            return consume(qi, 1, consume(qi - 1, 0, carry, False), True)

        init = tuple((jnp.full((tq, 1), -jnp.inf, F32), jnp.zeros((tq, LANES), F32)) for _ in range(2))
        scores(0, 0)
        carry = lax.fori_loop(0, qi // 2, step, init)
        (m0, a0), (m1, a1) = lax.cond(qi % 2 == 0, finish_even, finish_odd, carry)
        l0, l1 = a0[:, hd:hd + 1], a1[:, 0:1]
        o = jnp.where(lo, a0 / l0, a1 / l1)
        o_ref[...] = o
        og_ref[...] = (o * jax.nn.sigmoid(g_ref[0])).astype(BF)
        lane2 = lax.broadcasted_iota(jnp.int32, (tq, 2), 1)
        m_ref[0] = jnp.where(lane2 == 0, m0, m1)
        l_ref[0] = jnp.where(lane2 == 0, l0, l1)
        if n_g:
            pl.when((pair == P - 1) & (qi == nq - 1))(finish)

    tile = pl.BlockSpec((tq, LANES), lambda p, i: (i, p))
    whole = pl.BlockSpec((S, LANES), lambda p, i: (0, p))
    stat = pl.BlockSpec((1, tq, 2), lambda p, i: (p, i, 0))
    sems = [pltpu.SemaphoreType.DMA((6 * n_g,)), pltpu.SemaphoreType.DMA((6 * n_g,))] if n_g else []
    outs = pl.pallas_call(
        body, name=name, grid=(P, nq),
        in_specs=[tile, tile, whole, whole, whole, whole, pl.BlockSpec((1, tq, LANES), lambda p, i: (1, i, p))]
        + [ANY] * n_g,
        out_specs=[tile, tile, stat, stat] + [ANY] * n_g,
        out_shape=[jax.ShapeDtypeStruct((S, D), F32), jax.ShapeDtypeStruct((S, D), BF),
                   jax.ShapeDtypeStruct((P, S, 2), F32), jax.ShapeDtypeStruct((P, S, 2), F32)]
        + [jax.ShapeDtypeStruct((N_CHIPS,) + a.shape, a.dtype) for a in gather],
        scratch_shapes=[pltpu.VMEM((2, 2, tq, tq), F32)] + sems,
        compiler_params=_params("arbitrary" if n_g else "parallel", "arbitrary"),
    )(*qa, *ka, *va, qg, *gather)
    return tuple(outs[:4]) + (_fill_own(outs[4:], gather),)


def _attn_out_bwd(dy, w_out, o, qg, l, hd, name):
    S, D = o.shape
    P = D // LANES
    ts = _tile(S, 512, 2 * SUBLANES)

    def body(dy_ref, w_ref, o_ref, g_ref, l_ref, do_ref, dg_ref, e_ref):
        lo = _head_masks((ts, LANES), hd)
        lane2 = lax.broadcasted_iota(jnp.int32, (ts, 2), 1)
        dog = lax.dot_general(dy_ref[...].astype(BF), w_ref[...], NT_DIMS, preferred_element_type=F32)
        for t in range(P):
            cols = slice(t * LANES, (t + 1) * LANES)
            sg = jax.nn.sigmoid(g_ref[0, :, cols])
            dog_t, o_t, l_t = dog[:, cols], o_ref[:, cols], l_ref[t]
            g = (dog_t * sg / jnp.where(lo, l_t[:, 0:1], l_t[:, 1:2])).astype(BF)
            do_ref[:, cols] = g
            dg_ref[:, cols] = (dog_t * o_t * sg * (1.0 - sg)).astype(BF)
            prod = g.astype(F32) * o_t
            e0 = jnp.sum(jnp.where(lo, prod, 0.0), axis=-1, keepdims=True)
            e1 = jnp.sum(jnp.where(lo, 0.0, prod), axis=-1, keepdims=True)
            e_ref[t] = jnp.where(lane2 == 0, e0, e1)

    rows = pl.BlockSpec((ts, D), lambda s: (s, 0))
    stat = pl.BlockSpec((P, ts, 2), lambda s: (0, s, 0))
    return pl.pallas_call(
        body, name=name, grid=(S // ts,),
        in_specs=[rows, pl.BlockSpec(w_out.shape, lambda s: (0, 0)), rows,
                  pl.BlockSpec((1, ts, D), lambda s: (1, s, 0)), stat],
        out_specs=[rows, rows, stat],
        out_shape=[jax.ShapeDtypeStruct((S, D), BF), jax.ShapeDtypeStruct((S, D), BF),
                   jax.ShapeDtypeStruct((P, S, 2), F32)],
        compiler_params=_params("parallel"),
    )(dy, w_out, o, qg, l)


def _attn_bwd(qa, ka, vb, g, m_row, e_row, hd, name, exchange=()):
    S, D = vb.shape
    P = D // LANES
    tk = _attn_tile(S)
    nk = S // tk
    scale = hd ** -0.5
    n_x = len(exchange)

    def body(q0_ref, q1_ref, g_ref, k0_ref, k1_ref, v_ref, m_ref, e_ref, *rest):
        dq_ref, dk_ref, dv_ref, dc_ref = rest[n_x:n_x + 4]
        st_buf, dp_buf = rest[2 * n_x + 4:2 * n_x + 6]
        pair, ki = pl.program_id(0), pl.program_id(1)
        if n_x:
            copies = lambda: _chip_copies(rest[:n_x], rest[n_x + 4:2 * n_x + 4], *rest[2 * n_x + 6:])

            @pl.when((pair == 0) & (ki == 0))
            def _():
                for cp in copies():
                    cp.start()

        @pl.when(ki == 0)
        def _():
            dq_ref[...] = jnp.zeros_like(dq_ref)

        lo = _head_masks((tk, LANES), hd)
        kh = (k0_ref[...], k1_ref[...])
        q_refs = (q0_ref, q1_ref)
        vh = _split_heads(v_ref[...], lo)
        causal_t = lax.broadcasted_iota(jnp.int32, (tk, tk), 0) <= lax.broadcasted_iota(jnp.int32, (tk, tk), 1)

        def stage(qi, slot):
            off = pl.multiple_of(qi * tk, tk)
            gb = g_ref[pl.ds(off, tk), :]
            for h in range(2):
                st_buf[slot, h] = lax.dot_general(kh[h], q_refs[h][pl.ds(off, tk), :], NT_DIMS,
                                                  preferred_element_type=F32)
                dp_buf[slot, h] = lax.dot_general(vh[h], gb, NT_DIMS, preferred_element_type=F32)

        def consume(qi, slot, carry, masked):
            off = pl.multiple_of(qi * tk, tk)
            gb = g_ref[pl.ds(off, tk), :]
            m_t, e_t = m_ref[0, qi], e_ref[0, qi]
            out, dq_parts = [], []
            for h in range(2):
                dk, dv, dc = carry[h]
                qb = q_refs[h][pl.ds(off, tk), :]
                pt = jnp.exp2(st_buf[slot, h] - m_t[h:h + 1, :])
                if masked:
                    pt = jnp.where(causal_t, pt, 0.0)
                pb = pt.astype(BF)
                dv = dv + jnp.dot(pb, gb, preferred_element_type=F32)
                dst = pb.astype(F32) * (dp_buf[slot, h] - e_t[h:h + 1, :])
                db = dst.astype(BF)
                dk = dk + jnp.dot(db, qb, preferred_element_type=F32)
                dc = dc - jnp.sum(dst, axis=-1, keepdims=True)
                dq_parts.append(lax.dot_general(db, kh[h], TN_DIMS, preferred_element_type=F32))
                out.append((dk, dv, dc))
            dq_ref[pl.ds(off, tk), :] += jnp.where(lo, dq_parts[0], dq_parts[1]) * scale
            return tuple(out)

        n_after = nk - 1 - ki

        def step(j, carry):
            b = ki + 1 + 2 * j
            stage(b + 1, 0)
            carry = consume(b, 1, carry, False)
            stage(b + 2, 1)
            return consume(b + 1, 0, carry, False)

        def rest_one(carry):
            return consume(nk - 1, 1, carry, False)

        def rest_two(carry):
            stage(nk - 1, 0)
            return consume(nk - 1, 0, consume(nk - 2, 1, carry, False), False)

        init = tuple((jnp.zeros((tk, LANES), F32), jnp.zeros((tk, LANES), F32), jnp.zeros((tk, 1), F32))
                     for _ in range(2))
        stage(ki, 0)
        stage(jnp.minimum(ki + 1, nk - 1), 1)
        carry = consume(ki, 0, init, True)
        carry = lax.fori_loop(0, (n_after - 1) // 2, step, carry)
        which = jnp.where(n_after == 0, 0, 2 - n_after % 2)
        (dk0, dv0, dc0), (dk1, dv1, dc1) = lax.switch(which, [lambda c: c, rest_one, rest_two], carry)
        dk_ref[...] = jnp.where(lo, dk0, dk1) * (1.0 / LOG2E)
        dv_ref[...] = jnp.where(lo, dv0, dv1)
        lane2 = lax.broadcasted_iota(jnp.int32, (tk, 2), 1)
        dc_ref[0] = jnp.where(lane2 == 0, dc0, dc1)
        if n_x:
            @pl.when((pair == P - 1) & (ki == nk - 1))
            def _():
                for cp in copies():
                    cp.wait()

    tile = pl.BlockSpec((tk, LANES), lambda p, i: (i, p))
    whole = pl.BlockSpec((S, LANES), lambda p, i: (0, p))
    row_spec = pl.BlockSpec((1, nk, 2, tk), lambda p, i: (p, 0, 0, 0))
    sems = [pltpu.SemaphoreType.DMA((3 * n_x,)), pltpu.SemaphoreType.DMA((3 * n_x,))] if n_x else []
    outs = pl.pallas_call(
        body, name=name, grid=(P, nk),
        in_specs=[whole, whole, whole, tile, tile, tile, row_spec, row_spec] + [ANY] * n_x,
        out_specs=[whole, tile, tile, pl.BlockSpec((1, tk, 2), lambda p, i: (p, i, 0))] + [ANY] * n_x,
        out_shape=[jax.ShapeDtypeStruct((S, D), F32), jax.ShapeDtypeStruct((S, D), F32),
                   jax.ShapeDtypeStruct((S, D), F32), jax.ShapeDtypeStruct((P, S, 2), F32)]
        + [jax.ShapeDtypeStruct((3,) + a.shape[1:], a.dtype) for a in exchange],
        scratch_shapes=[pltpu.VMEM((2, 2, tk, tk), F32), pltpu.VMEM((2, 2, tk, tk), F32)] + sems,
        compiler_params=_params("arbitrary" if n_x else "parallel", "arbitrary"),
    )(*qa, g, *ka, vb, m_row, e_row, *exchange)
    return tuple(outs[:4]) + (list(outs[4:]),)


def _loss_head(y, t, name):
    S, D = y.shape
    ts = _tile(S, 512, SUBLANES)

    def body(y_ref, t_ref, dy_ref, l_ref):
        @pl.when(pl.program_id(0) == 0)
        def _():
            l_ref[...] = jnp.zeros_like(l_ref)

        e = y_ref[...] - t_ref[...]
        dy_ref[...] = e * (1.0 / D)
        part = 0.5 * jnp.sum(jnp.mean(e * e, axis=-1, keepdims=True), axis=0, keepdims=True)
        l_ref[...] += jnp.broadcast_to(part, l_ref.shape)

    return pl.pallas_call(
        body, name=name, grid=(S // ts,),
        in_specs=[pl.BlockSpec((ts, D), lambda s: (s, 0)), pl.BlockSpec((ts, D), lambda s: (s, 0))],
        out_specs=[pl.BlockSpec((ts, D), lambda s: (s, 0)), pl.BlockSpec((SUBLANES, LANES), lambda s: (0, 0))],
        out_shape=[jax.ShapeDtypeStruct((S, D), F32), jax.ShapeDtypeStruct((SUBLANES, LANES), F32)],
        compiler_params=_params("arbitrary"),
    )(y, t)


def _adamw(w, g, m, v, name):
    shape = w.shape
    cols = shape[-1]
    as2d = lambda a: a.reshape(-1, cols)
    rows = as2d(w).shape[0]
    tr = _tile(rows, 256, SUBLANES) if rows % SUBLANES == 0 else rows
    c1 = 1.0 - ADAM_B1 ** ADAM_STEP
    c2 = 1.0 - ADAM_B2 ** ADAM_STEP

    def body(w_ref, g_ref, m_ref, v_ref, d_ref, nm_ref, nv_ref):
        gg = g_ref[...]
        nm = ADAM_B1 * m_ref[...] + (1.0 - ADAM_B1) * gg
        nv = ADAM_B2 * v_ref[...] + (1.0 - ADAM_B2) * (gg * gg)
        d_ref[...] = -ADAM_LR * ((nm / c1) / (jnp.sqrt(nv / c2) + ADAM_EPS) + ADAM_WD * w_ref[...])
        nm_ref[...] = nm
        nv_ref[...] = nv

    spec = pl.BlockSpec((tr, cols), lambda r: (r, 0))
    outs = pl.pallas_call(
        body, name=name, grid=(rows // tr,), in_specs=[spec] * 4, out_specs=[spec] * 3,
        out_shape=[jax.ShapeDtypeStruct((rows, cols), F32)] * 3,
        compiler_params=_params("parallel"),
    )(as2d(w), as2d(g), as2d(m), as2d(v))
    return tuple(o.reshape(shape) for o in outs)


def _place():
    return lax.axis_index("x"), lax.axis_index("y"), lax.axis_index("c")


def _other_chips(x, y):
    return [(1 - x, y), (x, 1 - y), (1 - x, 1 - y)]


def _remote(src, dst, send_sems, recv_sems, k, to):
    return pltpu.make_async_remote_copy(src_ref=src, dst_ref=dst, send_sem=send_sems.at[k], recv_sem=recv_sems.at[k],
                                        device_id=to, device_id_type=MESH)


def _half(c, rh):
    return pl.ds(pl.multiple_of(c * rh, 2 * SUBLANES), rh)


def _gather_phases(rows, src, dst, send_sems, recv_sems):
    n = len(rows)
    x, y, c = _place()
    me = 2 * x + y
    sib = (x, y, 1 - c)
    chips = _other_chips(x, y)
    rh = [r // 2 for r in rows]

    def first():
        return [_remote(src[g].at[_half(c, rh[g])], dst[g].at[me, _half(c, rh[g])], send_sems, recv_sems,
                        6 * g + j, (cx, cy, c)) for j, (cx, cy) in enumerate(chips) for g in range(n)]

    def landed(j, g, core):
        cx, cy = chips[j]
        return dst[g].at[2 * cx + cy, _half(core, rh[g])]

    def passed():
        return [_remote(landed(j, g, c), landed(j, g, c), send_sems, recv_sems, 6 * g + 3 + j, sib)
                for j in range(3) for g in range(n)]

    def start():
        for cp in first():
            cp.start()

    def pass_on():
        cps = passed()
        for j in range(3):
            for g in range(n):
                _remote(landed(j, g, c), landed(j, g, c), send_sems, recv_sems, 6 * g + j, sib).wait_recv()
                cps[j * n + g].start()

    def finish():
        for j in range(3):
            for g in range(n):
                _remote(landed(j, g, 1 - c), landed(j, g, 1 - c), send_sems, recv_sems, 6 * g + 3 + j, sib).wait_recv()
        for cp in first() + passed():
            cp.wait_send()

    return start, pass_on, finish


def _fill_own(outs, srcs):
    x, y, _ = _place()
    return [lax.dynamic_update_slice_in_dim(o, a[None], 2 * x + y, axis=0) for o, a in zip(outs, srcs)]


def _allgather_weights(srcs):
    n = len(srcs)

    def body(*refs):
        for step in _gather_phases([a.shape[0] for a in srcs], refs[:n], refs[n:2 * n], *refs[2 * n:]):
            step()

    outs = pl.pallas_call(
        body, name="allgather_weights", in_specs=[ANY] * n, out_specs=[ANY] * n,
        out_shape=[jax.ShapeDtypeStruct((N_CHIPS,) + a.shape, a.dtype) for a in srcs],
        scratch_shapes=[pltpu.SemaphoreType.DMA((6 * n,)), pltpu.SemaphoreType.DMA((6 * n,))],
    )(*srcs)
    return _fill_own(outs, srcs)


def _pair_exchange(gs, name):
    n = len(gs)

    def body(*refs):
        g_refs, t_refs, (send_sems, recv_sems) = refs[:n], refs[n:2 * n], refs[2 * n:]
        x, y, c = _place()
        cps = [_remote(g_refs[g].at[k, 1 - c], t_refs[g].at[k], send_sems, recv_sems, N_CHIPS * g + k, (x, y, 1 - c))
               for g in range(n) for k in range(N_CHIPS)]
        for cp in cps:
            cp.start()
        for cp in cps:
            cp.wait()

    return pl.pallas_call(
        body, name=name, in_specs=[ANY] * n, out_specs=[ANY] * n,
        out_shape=[jax.ShapeDtypeStruct((a.shape[0],) + a.shape[2:], a.dtype) for a in gs],
        scratch_shapes=[pltpu.SemaphoreType.DMA((N_CHIPS * n,)), pltpu.SemaphoreType.DMA((N_CHIPS * n,))],
    )(*gs)


def _pair_add(g, t, c, name):
    n, _, rh, W = g.shape
    tr = _tile(rh, 256, 2 * SUBLANES)

    def body(c_ref, g_ref, t_ref, o_ref):
        o_ref[...] = (g_ref[0] + t_ref[...]).astype(BF)

    return pl.pallas_call(
        body, name=name,
        grid_spec=pltpu.PrefetchScalarGridSpec(
            num_scalar_prefetch=1, grid=(n, rh // tr),
            in_specs=[pl.BlockSpec((1, 1, tr, W), lambda k, i, c_ref: (k, c_ref[0], i, 0)),
                      pl.BlockSpec((1, tr, W), lambda k, i, c_ref: (k, i, 0))],
            out_specs=pl.BlockSpec((1, tr, W), lambda k, i, c_ref: (k, i, 0))),
        out_shape=jax.ShapeDtypeStruct((n, rh, W), BF),
        compiler_params=_params("parallel", "parallel"),
    )(c.reshape(1).astype(jnp.int32), g, t)


def _chip_copies(a_refs, t_refs, send_sems, recv_sems):
    x, y, c = _place()
    return [_remote(a_refs[g].at[2 * cx + cy], t_refs[g].at[j], send_sems, recv_sems, 3 * g + j, (cx, cy, c))
            for j, (cx, cy) in enumerate(_other_chips(x, y)) for g in range(len(a_refs))]


def _chip_exchange(parts, name):
    n = len(parts)

    def body(*refs):
        cps = _chip_copies(refs[:n], refs[n:2 * n], *refs[2 * n:])
        for cp in cps:
            cp.start()
        for cp in cps:
            cp.wait()

    return pl.pallas_call(
        body, name=name, in_specs=[ANY] * n, out_specs=[ANY] * n,
        out_shape=[jax.ShapeDtypeStruct((3,) + a.shape[1:], a.dtype) for a in parts],
        scratch_shapes=[pltpu.SemaphoreType.DMA((3 * n,)), pltpu.SemaphoreType.DMA((3 * n,))],
    )(*parts)


def _chip_add(g, t1, t2, c, me, name):
    _, _, rh, W = g.shape
    tr = _tile(rh, 256, 2 * SUBLANES)

    def body(c_ref, me_ref, g_ref, t1_ref, t2_ref, o_ref):
        own = g_ref[0, 0] + t1_ref[0]
        o_ref[...] = own + t2_ref[0].astype(F32) + t2_ref[1].astype(F32) + t2_ref[2].astype(F32)

    return pl.pallas_call(
        body, name=name,
        grid_spec=pltpu.PrefetchScalarGridSpec(
            num_scalar_prefetch=2, grid=(rh // tr,),
            in_specs=[pl.BlockSpec((1, 1, tr, W), lambda i, c_ref, me_ref: (me_ref[0], c_ref[0], i, 0)),
                      pl.BlockSpec((1, tr, W), lambda i, c_ref, me_ref: (me_ref[0], i, 0)),
                      pl.BlockSpec((3, tr, W), lambda i, c_ref, me_ref: (0, i, 0))],
            out_specs=pl.BlockSpec((tr, W), lambda i, c_ref, me_ref: (i, 0))),
        out_shape=jax.ShapeDtypeStruct((rh, W), F32),
        compiler_params=_params("parallel"),
    )(c.reshape(1).astype(jnp.int32), me.reshape(1).astype(jnp.int32), g, t1, t2)


def _pair_share(hs, name):
    n = len(hs)

    def body(*refs):
        h_refs, f_refs, (send_sems, recv_sems) = refs[:n], refs[n:2 * n], refs[2 * n:]
        x, y, c = _place()
        cps = [_remote(h_refs[g], f_refs[g], send_sems, recv_sems, g, (x, y, 1 - c)) for g in range(n)]
        for cp in cps:
            cp.start()
        for cp in cps:
            cp.wait()

    return pl.pallas_call(
        body, name=name, in_specs=[ANY] * n, out_specs=[ANY] * n,
        out_shape=[jax.ShapeDtypeStruct(a.shape, a.dtype) for a in hs],
        scratch_shapes=[pltpu.SemaphoreType.DMA((n,)), pltpu.SemaphoreType.DMA((n,))],
    )(*hs)


def _allreduce_small(pack, name):
    rows, W = pack.shape

    def body(p_ref, o_ref, buf, send_sems, recv_sems):
        x, y, c = _place()
        me = 4 * x + 2 * y + c
        buf[me] = p_ref[...]
        cps = []
        for r in range(1, 8):
            fx, fy, fc = (r >> 2) & 1, (r >> 1) & 1, r & 1
            to = (1 - x if fx else x, 1 - y if fy else y, 1 - c if fc else c)
            cps.append(_remote(p_ref, buf.at[me], send_sems, recv_sems, r - 1, to))
        for cp in cps:
            cp.start()
        for r in range(1, 8):
            fx, fy, fc = (r >> 2) & 1, (r >> 1) & 1, r & 1
            frm = 4 * (1 - x if fx else x) + 2 * (1 - y if fy else y) + (1 - c if fc else c)
            _remote(p_ref, buf.at[frm], send_sems, recv_sems, r - 1, (x, y, c)).wait_recv()
        for cp in cps:
            cp.wait_send()
        acc = buf[0]
        for i in range(1, 8):
            acc = acc + buf[i]
        o_ref[...] = acc

    return pl.pallas_call(
        body, name=name, in_specs=[VMEM], out_specs=VMEM,
        out_shape=jax.ShapeDtypeStruct((rows, W), F32),
        scratch_shapes=[pltpu.VMEM((8, rows, W), F32), pltpu.SemaphoreType.DMA((7,)), pltpu.SemaphoreType.DMA((7,))],
    )(pack)


def _width_groups(arrs):
    widths = []
    for a in arrs:
        if a.shape[-1] not in widths:
            widths.append(a.shape[-1])
    return [[i for i, a in enumerate(arrs) if a.shape[-1] == w] for w in widths]


def _rows2d(a):
    return a.reshape(-1, a.shape[-1])


def _split_rows_like(buf, like, lead=()):
    out, off = [], 0
    for a in like:
        n = a.size // a.shape[-1]
        out.append(buf[..., off:off + n, :].reshape(tuple(lead) + a.shape))
        off += n
    return out


def _join_cols(g):
    nd = g.ndim
    return jnp.moveaxis(g, 0, nd - 2).reshape(g.shape[1:-1] + (N_CHIPS * g.shape[-1],))


def _join_rows(g):
    return jnp.moveaxis(g, 0, 1).reshape(g.shape[1], N_CHIPS * g.shape[2], g.shape[3])


def _row_layout(a, tq):
    P, S, _ = a.shape
    return a.reshape(P, S // tq, tq, 2).transpose(0, 1, 3, 2)


def _pad_row(v, width=FLAT_W):
    flat = v.reshape(-1)
    rows = -(-flat.shape[0] // width)
    return jnp.pad(flat, (0, rows * width - flat.shape[0]))


def kernel(x, attn_norm, ffn_norm, a_w_in, a_conv, a_w_out, kv_norm, w_kvf, b_f, k_norm, b_w_qg, q_norm, b_w_out, ffn_w_up, ffn_conv, ffn_w_down, loss_target, m_attn_norm, m_ffn_norm, m_a_w_in, m_a_conv, m_a_w_out, m_kv_norm, m_w_kvf, m_b_f, m_k_norm, m_b_w_qg, m_q_norm, m_b_w_out, m_ffn_w_up, m_ffn_conv, m_ffn_w_down, v_attn_norm, v_ffn_norm, v_a_w_in, v_a_conv, v_a_w_out, v_kv_norm, v_w_kvf, v_b_f, v_k_norm, v_b_w_qg, v_q_norm, v_b_w_out, v_ffn_w_up, v_ffn_conv, v_ffn_w_down):
    xs = x[0]
    S, D = xs.shape
    H, hd = b_f.shape[0], k_norm.shape[0]
    depth = attn_norm.shape[0]
    n_a = a_w_in.shape[0]
    P = D // LANES
    assert LANES == 2 * hd and H * hd == D, "the attention kernels hold two heads per lane tile"
    mx, my, mc = _place()
    chip = 2 * mx + my

    big = [a_w_in, a_w_out, w_kvf, b_w_qg, b_w_out, ffn_w_up, ffn_w_down]
    groups = _width_groups(big)
    assert n_a >= 2 and depth - n_a >= 2, "the hosted weight gathers are laid out for two layers of each kind"
    first = [a_w_in[:1]]
    behind_a_in = {0: [a_w_out[:1], ffn_w_up[:1]], 1: [ffn_w_up[1:n_a]]}
    behind_ffn_up = {0: [ffn_w_down[:1], a_w_in[1:], a_w_out[1:]], 1: [ffn_w_down[1:n_a], w_kvf, b_w_qg[:1]]}
    late = [b_w_qg[1:], b_w_out, ffn_w_up[n_a:], ffn_w_down[n_a:]]

    def packed(ws):
        idx_groups = _width_groups(ws)
        return [jnp.concatenate([_rows2d(ws[i]).astype(BF) for i in idx]) for idx in idx_groups], idx_groups

    def unpacked(bufs, idx_groups, ws):
        out = [None] * len(ws)
        for idx, buf in zip(idx_groups, bufs):
            for i, part in zip(idx, _split_rows_like(buf, [ws[i] for i in idx], (N_CHIPS,))):
                out[i] = part
        return out

    first_src, first_groups = packed(first)
    late_src, late_groups = packed(late)
    (g_in,) = unpacked(_allgather_weights(first_src), first_groups, first)
    wa_in, wa_out, w_up, w_down, wb_qg, wb_out = list(_join_cols(g_in)), [], [], [], [], []
    kvf_cols = 2 * D + LANES

    def placed(shard):
        full = jnp.zeros(shard.shape[:-1] + (N_CHIPS, shard.shape[-1]), F32)
        full = lax.dynamic_update_slice_in_dim(full, shard[..., None, :], chip, axis=full.ndim - 2)
        return jnp.where(mc == 0, full, 0.0).reshape(-1)

    conv_pack = jnp.concatenate([_pad_row(placed(a_conv)), _pad_row(placed(ffn_conv))]).reshape(-1, FLAT_W)
    conv_full = _allreduce_small(conv_pack, "allgather_conv_taps").reshape(-1)
    n_ac = a_conv.size * N_CHIPS
    a_conv_f = conv_full[:n_ac].reshape(a_conv.shape[:-1] + (-1,))
    off = _pad_row(placed(a_conv)).shape[0]
    ffn_conv_f = conv_full[off:off + ffn_conv.size * N_CHIPS].reshape(ffn_conv.shape[:-1] + (-1,))
    F = ffn_conv_f.shape[-1]

    b_pad = jnp.pad(b_f, (0, LANES - H)).reshape(1, LANES)
    gate_blk = 2 * D // LANES
    tq = _attn_tile(S)
    scale = hd ** -0.5

    saved = []
    cur = xs
    kv = None
    for l in range(depth):
        rec = {"x_in": cur}
        if l < n_a:
            ws = behind_a_in.get(l, [])
            srcs, idx_groups = packed(ws) if ws else ((), [])
            proj, xn, z, bufs = _mixer_in_fwd(cur, attn_norm[l], wa_in[l], a_conv_f[l], f"a_in_{l}", gather=srcs)
            if l == 0:
                g_out, g_up = unpacked(bufs, idx_groups, ws)
                wa_out += list(_join_rows(g_out))
                w_up += list(_join_cols(g_up))
            if l == 1:
                (g_up,) = unpacked(bufs, idx_groups, ws)
                w_up += list(_join_cols(g_up))
            mid = _matmul_residual(z, wa_out[l], cur, f"a_out_{l}")
            rec.update(proj=proj, xn=xn, z=z)
        else:
            j = l - n_a
            if kv is None:
                kvf, hn = _norm_matmul(cur, kv_norm, wkvf, 1, F32, "kvf_proj")
                vb = kvf[0, :, D:2 * D].astype(BF)
                cgate = _gate_fwd(kvf, b_pad, gate_blk, "gate_cumsum")
                kv = dict(kvf=kvf, hn=hn, vb=vb, cgate=cgate, x_in=cur, dk=[], dv=[], dc=[],
                          ka=_augment(kvf, 0, 0, cgate, "k", hd, D, "k_augment", norm_w=k_norm),
                          va=_augment(kvf, 0, 1, cgate, "v", hd, D, "v_augment"))
            qg, xn = _norm_matmul(cur, attn_norm[l], wb_qg[j], 2, F32, f"qg_proj_{j}")
            qa = _augment(qg, 0, 0, kv["cgate"], "q", hd, D, f"q_augment_{j}", norm_w=q_norm[j], scale=scale * LOG2E)
            o, og, m_max, l_sum, late_bufs = _attn_fwd(qa, kv["ka"], kv["va"], qg, hd, f"attn_fwd_{j}",
                                                       gather=late_src if j == 0 else ())
            if j == 0:
                g_qg, g_bout, g_up, g_down = unpacked(late_bufs, late_groups, late)
                wb_qg += list(_join_cols(g_qg))
                wb_out += list(_join_rows(g_bout))
                w_up += list(_join_cols(g_up))
                w_down += list(_join_rows(g_down))
            mid = _matmul_residual(og, wb_out[j], cur, f"b_out_{j}")
            rec.update(qg=qg, xn=xn, qa=qa, o=o, og=og, m=m_max, l=l_sum)
        ws = behind_ffn_up.get(l, [])
        srcs, idx_groups = packed(ws) if ws else ((), [])
        up, xn2, z2, bufs = _ffn_up_fwd(mid, ffn_norm[l], w_up[l], ffn_conv_f[l], f"ffn_up_{l}", gather=srcs)
        if l == 0:
            g_down, g_in, g_out = unpacked(bufs, idx_groups, ws)
            w_down += list(_join_rows(g_down))
            wa_in += list(_join_cols(g_in))
            wa_out += list(_join_rows(g_out))
        if l == 1:
            g_down, g_kvf, g_qg = unpacked(bufs, idx_groups, ws)
            w_down += list(_join_rows(g_down))
            wkvf = jnp.pad(_join_cols(g_kvf), ((0, 0), (0, kvf_cols - (2 * D + H))))
            wb_qg += list(_join_cols(g_qg))
        cur = _matmul_residual(z2, w_down[l], mid, f"ffn_down_{l}")
        rec.update(x_mid=mid, up=up, xn2=xn2, z2=z2)
        saved.append(rec)

    dy, loss_part = _loss_head(cur, loss_target[0], "loss_head")

    def cols_of(g, k):
        c = g.shape[-1] // N_CHIPS
        return g[:, k * c:(k + 1) * c]

    def rows_of(g, k):
        r = g.shape[0] // N_CHIPS
        return g[k * r:(k + 1) * r]

    def reduce_scatter(items, tag, host=None):
        keys = list(items)
        likes = [items[k][1](items[k][0], 0) for k in keys]
        idx_groups = _width_groups(likes)

        def group_buffer(idx):
            rows = [items[keys[i]][1](items[keys[i]][0], k) for k in range(N_CHIPS) for i in idx]
            buf = jnp.concatenate(rows)
            return buf.reshape(N_CHIPS, 2, buf.shape[0] // (2 * N_CHIPS), buf.shape[1])

        g4 = [group_buffer(idx) for idx in idx_groups]
        from_sibling = _pair_exchange(g4, f"grad_pair_exchange_{tag}")
        parts = [_pair_add(g, t, mc, f"grad_pair_add_{tag}{n}") for n, (g, t) in enumerate(zip(g4, from_sibling))]
        from_chips = _chip_exchange(parts, f"grad_chip_exchange_{tag}") if host is None else host(parts)
        mine = [_chip_add(g, t1, t2, mc, chip, f"grad_chip_add_{tag}{n}") for n, (g, t1, t2) in
                enumerate(zip(g4, from_sibling, from_chips))]
        pending.append((keys, likes, idx_groups, mine))

    pending = []

    g_attn_norm, g_ffn_norm = [None] * depth, [None] * depth
    g_a_in, g_a_conv, g_a_out = [None] * n_a, [None] * n_a, [None] * n_a
    g_qg, g_qn, g_bo = [None] * (depth - n_a), [None] * (depth - n_a), [None] * (depth - n_a)
    g_up, g_fc, g_down = [None] * depth, [None] * depth, [None] * depth
    for l in reversed(range(depth)):
        rec = saved[l]
        if l == n_a - 1:
            ready = {(2, 0): (g_kvf, cols_of), (3, 0): (g_qg[0], cols_of)}
        elif l < n_a - 1:
            ready = {(5, l + 1): (g_up[l + 1], cols_of), (6, l + 1): (g_down[l + 1], rows_of),
                     (0, l + 1): (g_a_in[l + 1], cols_of), (1, l + 1): (g_a_out[l + 1], rows_of)}
        else:
            ready = {}
        if ready:
            held = {}

            def behind_mid_bwd(parts):
                *held["out"], exchanged = _ffn_mid_bwd(rec["up"], dy, w_down[l], ffn_conv_f[l], f"ffn_mid_bwd_{l}",
                                                       exchange=parts)
                return exchanged

            reduce_scatter(ready, f"a{l}", behind_mid_bwd)
            dup, g_fc[l] = held["out"]
        else:
            dup, g_fc[l], _ = _ffn_mid_bwd(rec["up"], dy, w_down[l], ffn_conv_f[l], f"ffn_mid_bwd_{l}")
        g_down[l] = _wgrad(rec["z2"], dy[None], f"ffn_down_wgrad_{l}")
        g_up[l] = _wgrad(rec["xn2"], dup, f"ffn_up_wgrad_{l}")
        dy, g_ffn_norm[l] = _dnorm(dup, w_up[l], rec["x_mid"], ffn_norm[l], dy, f"ffn_up_bwd_{l}")
        if l < n_a:
            if l == 0:
                held = {}

                def behind_a_mid_bwd(parts):
                    *held["out"], exchanged = _mixer_mid_bwd(rec["proj"], dy, wa_out[l], a_conv_f[l],
                                                             f"a_mid_bwd_{l}", exchange=parts)
                    return exchanged

                reduce_scatter({(5, 0): (g_up[0], cols_of), (6, 0): (g_down[0], rows_of)}, "f0", behind_a_mid_bwd)
                dproj, g_a_conv[l] = held["out"]
            else:
                dproj, g_a_conv[l], _ = _mixer_mid_bwd(rec["proj"], dy, wa_out[l], a_conv_f[l], f"a_mid_bwd_{l}")
            g_a_out[l] = _wgrad(rec["z"], dy[None], f"a_out_wgrad_{l}")
            g_a_in[l] = _wgrad(rec["xn"], dproj, f"a_in_wgrad_{l}")
            dy, g_attn_norm[l] = _dnorm(dproj, wa_in[l], rec["x_in"], attn_norm[l], dy, f"a_in_bwd_{l}")
        else:
            j = l - n_a
            g_out, dgate, evec = _attn_out_bwd(dy, wb_out[j], rec["o"], rec["qg"], rec["l"], hd,
                                               f"attn_gate_bwd_{j}")
            g_bo[j] = _wgrad(rec["og"], dy[None], f"b_out_wgrad_{j}")
            ready = {(5, l): (g_up[l], cols_of), (6, l): (g_down[l], rows_of), (4, j): (g_bo[j], rows_of)}
            if j + 1 < depth - n_a:
                ready[(3, j + 1)] = (g_qg[j + 1], cols_of)
            held = {}

            def behind_attn_bwd(parts):
                *held["grads"], exchanged = _attn_bwd(rec["qa"], kv["ka"], kv["vb"], g_out, _row_layout(rec["m"], tq),
                                                      _row_layout(evec, tq), hd, f"attn_bwd_{j}", exchange=parts)
                return exchanged

            reduce_scatter(ready, f"b{j}", behind_attn_bwd)
            dqn, dk, dv, dc = held["grads"]
            kv["dk"].append(dk)
            kv["dv"].append(dv)
            kv["dc"].append(dc)
            dq_pre, g_qn[j] = _headnorm_bwd(rec["qg"], 0, 0, q_norm[j], [dqn], D, f"q_norm_bwd_{j}")
            dqg = jnp.stack([dq_pre, dgate])
            g_qg[j] = _wgrad(rec["xn"], dqg, f"qg_wgrad_{j}")
            dy, g_attn_norm[l] = _dnorm(dqg, wb_qg[j], rec["x_in"], attn_norm[l], dy, f"qg_bwd_{j}")
            if l == n_a:
                dk_s, g_k_norm = _headnorm_bwd(kv["kvf"], 0, 0, k_norm, kv["dk"], D, "k_norm_bwd")
                dv_s = functools.reduce(jnp.add, kv["dv"]).astype(BF)
                dc_sum = functools.reduce(jnp.add, kv["dc"])
                dc_pad = jnp.pad(dc_sum.transpose(1, 0, 2).reshape(S, H), ((0, 0), (0, LANES - H)))
                df, db = _gate_bwd(dc_pad, kv["kvf"], b_pad, gate_blk, "gate_bwd")
                dkvf = jnp.concatenate([dk_s, dv_s, df.astype(BF)], axis=1)[None]
                g_kvf = _wgrad(kv["hn"], dkvf, "kvf_wgrad")[:, :2 * D + H]
                g_b_f = db[:H]
                dy, g_kv_norm = _dnorm(dkvf, wkvf, kv["x_in"], kv_norm, dy, "kvf_bwd")
    grad_x = dy[None]

    last = {(0, 0): (g_a_in[0], cols_of), (1, 0): (g_a_out[0], rows_of)}
    reduce_scatter(last, "c")
    shared = _pair_share([m for batch in pending for m in batch[3]], "grad_pair_share")
    reduced = {}
    for keys, likes, idx_groups, mine in pending:
        theirs, shared = shared[:len(mine)], shared[len(mine):]
        for idx, m_half, t_half in zip(idx_groups, mine, theirs):
            shard = jnp.where(mc == 0, jnp.concatenate([m_half, t_half]), jnp.concatenate([t_half, m_half]))
            for i, part in zip(idx, _split_rows_like(shard, [likes[i] for i in idx])):
                reduced[keys[i]] = part
    big_grads = [jnp.concatenate([reduced[(i, l)] for l in range(w.shape[0] if w.ndim == 3 else 1)]).reshape(w.shape)
                 for i, w in enumerate(big)]

    small = [loss_part[0, :1], jnp.stack(g_attn_norm), jnp.stack(g_ffn_norm), g_kv_norm, g_b_f, g_k_norm,
             jnp.stack(g_qn), jnp.stack(g_a_conv), jnp.stack(g_fc)]
    small_sum = _allreduce_small(jnp.concatenate([_pad_row(s) for s in small]).reshape(-1, FLAT_W),
                                 "allreduce_small_grads").reshape(-1)
    parts, off = [], 0
    for s in small:
        parts.append(small_sum[off:off + s.size].reshape(s.shape))
        off += _pad_row(s).shape[0]
    loss = parts[0][0]
    gr_attn_norm, gr_ffn_norm, gr_kv_norm, gr_b_f, gr_k_norm, gr_q_norm, gr_a_conv_full, gr_ffn_conv_full = parts[1:]

    def my_cols(full):
        c = full.shape[-1] // N_CHIPS
        return lax.dynamic_slice_in_dim(full, chip * c, c, axis=full.ndim - 1)

    gr_a_in, gr_a_out, gr_kvf, gr_qg, gr_bo, gr_up, gr_down = big_grads
    grads = [gr_attn_norm, gr_ffn_norm, gr_a_in, my_cols(gr_a_conv_full), gr_a_out, gr_kv_norm, gr_kvf, gr_b_f,
             gr_k_norm, gr_qg, gr_q_norm, gr_bo, gr_up, my_cols(gr_ffn_conv_full), gr_down]
    weights = [attn_norm, ffn_norm, a_w_in, a_conv, a_w_out, kv_norm, w_kvf, b_f, k_norm, b_w_qg, q_norm, b_w_out,
               ffn_w_up, ffn_conv, ffn_w_down]
    ms = [m_attn_norm, m_ffn_norm, m_a_w_in, m_a_conv, m_a_w_out, m_kv_norm, m_w_kvf, m_b_f, m_k_norm, m_b_w_qg,
          m_q_norm, m_b_w_out, m_ffn_w_up, m_ffn_conv, m_ffn_w_down]
    vs = [v_attn_norm, v_ffn_norm, v_a_w_in, v_a_conv, v_a_w_out, v_kv_norm, v_w_kvf, v_b_f, v_k_norm, v_b_w_qg,
          v_q_norm, v_b_w_out, v_ffn_w_up, v_ffn_conv, v_ffn_w_down]
    deltas, new_ms, new_vs = [], [], []
    for i, (w, g, m, v) in enumerate(zip(weights, grads, ms, vs)):
        d, nm, nv = _adamw(w, g, m, v, f"adamw_{i}")
        deltas.append(d)
        new_ms.append(nm)
        new_vs.append(nv)
    return (loss, grad_x, *grads, *deltas, *new_ms, *new_vs)
```
